```python
import math
import jax, jax.numpy as jnp
from jax import lax
import numpy as np

D_MODEL = 1024
BATCH = 16
SEQ = 2048
DEPTH = 1

HEAD_DIM = 64
N_ATTN_HEADS = 12
ATTN_WIDTH = N_ATTN_HEADS * HEAD_DIM
CONV_WIDTH = D_MODEL - ATTN_WIDTH
IN_WIDTH = 3 * ATTN_WIDTH + 2 * CONV_WIDTH
CONV_KERNEL = 31
DILATED_CONFIGS = ((128, 1), (512, 4), (2048, 16))
ATTN_BLOCK = 128
REL_BUCKETS = 32
REL_MAX_DIST = 2048
D_FF = 2816
FFN_CONV_KERNEL = 3
ALPHA = (2 * DEPTH) ** 0.25
BETA = (8 * DEPTH) ** -0.25
LN_EPS = 1e-5
NEG_INF = -1e30

kernel_name = "hymba_dilated_conformer_convffn_deepnorm"


def _layer_norm(x, g, b):
    xf = x.astype(jnp.float32)
    mu = jnp.mean(xf, axis=-1, keepdims=True)
    var = jnp.mean(jnp.square(xf - mu), axis=-1, keepdims=True)
    y = (xf - mu) * lax.rsqrt(var + LN_EPS)
    return (y * g.astype(jnp.float32) + b.astype(jnp.float32)).astype(x.dtype)


def _rms_norm(x, g):
    xf = x.astype(jnp.float32)
    y = xf * lax.rsqrt(jnp.mean(jnp.square(xf), axis=-1, keepdims=True) + LN_EPS)
    return (y * g.astype(jnp.float32)).astype(x.dtype)


def _causal_dwconv(x, w, b):
    K, C = w.shape
    y = lax.conv_general_dilated(
        x, w[:, None, :].astype(x.dtype), window_strides=(1,), padding=[(K - 1, 0)],
        dimension_numbers=("NWC", "WIO", "NWC"), feature_group_count=C)
    return y + b.astype(x.dtype)


def _t5_bucket(dist):
    exact = REL_BUCKETS // 2
    d_f = jnp.maximum(dist, 1).astype(jnp.float32)
    large = exact + (jnp.log(d_f / exact) / math.log(REL_MAX_DIST / exact)
                     * (REL_BUCKETS - exact)).astype(jnp.int32)
    large = jnp.minimum(large, REL_BUCKETS - 1)
    return jnp.where(dist < exact, dist, large)


def _dilated_branch(q, k, v, rel_table, window, dilation):
    B, H, S, E = q.shape
    L = S // dilation
    nb = -(-L // ATTN_BLOCK)
    Lp = nb * ATTN_BLOCK
    max_steps = window // dilation
    scale = 1.0 / math.sqrt(E)

    def to_sub(t):
        t = t.reshape(B, H, L, dilation, E).transpose(0, 1, 3, 2, 4)
        t = jnp.pad(t, ((0, 0), (0, 0), (0, 0), (0, Lp - L), (0, 0)))
        return t.reshape(B, H, dilation, nb, ATTN_BLOCK, E)

    def with_prev(t):
        prev = jnp.pad(t, ((0, 0), (0, 0), (0, 0), (1, 0), (0, 0), (0, 0)))[:, :, :, :nb]
        return jnp.concatenate([prev, t], axis=4)

    qs = to_sub(q)
    kk = with_prev(to_sub(k))
    vv = with_prev(to_sub(v))
    s = jnp.einsum("bhrnqe,bhrnke->bhrnqk", qs, kk,
                   preferred_element_type=jnp.float32) * scale

    qi = jnp.arange(ATTN_BLOCK)[:, None]
    kj = jnp.arange(2 * ATTN_BLOCK)[None, :]
    steps = qi + ATTN_BLOCK - kj
    band = (steps >= 0) & (steps <= max_steps)
    has_prev = (jnp.arange(nb)[:, None, None] > 0) | (kj >= ATTN_BLOCK)[None]
    valid = band[None] & has_prev
    bucket = _t5_bucket(jnp.maximum(steps, 0) * dilation)
    bias = rel_table[bucket].astype(jnp.float32).transpose(2, 0, 1)
    s = s + bias[:, None, None]
    s = jnp.where(valid, s, NEG_INF)

    m = jnp.max(s, axis=-1)
    p = jnp.exp(s - m[..., None])
    l = jnp.sum(p, axis=-1)
    o = jnp.einsum("bhrnqk,bhrnke->bhrnqe", p, vv.astype(jnp.float32))

    def from_sub(t):
        tail = t.shape[5:]
        t = t.reshape((B, H, dilation, Lp) + tail)[:, :, :, :L]
        t = jnp.moveaxis(t, 2, 3)
        return t.reshape((B, H, S) + tail)

    return from_sub(o), from_sub(m), from_sub(l)


def _dilated_attention(q, k, v, rel_table):
    branches = [_dilated_branch(q, k, v, rel_table, w, d) for (w, d) in DILATED_CONFIGS]
    m_all = jnp.max(jnp.stack([b[1] for b in branches]), axis=0)
    num = 0.0
    den = 0.0
    for o_i, m_i, l_i in branches:
        c = jnp.exp(m_i - m_all)
        num = num + o_i * c[..., None]
        den = den + l_i * c
    return num / den[..., None]


def _layer(x, rel_table, w_in, b_in, conv_w, conv_b, conv_ln_g, conv_ln_b,
           attn_norm_g, conv_norm_g, w_out, ln1_g, ln1_b,
           w_up, ffn_conv_w, ffn_conv_b, w_down, ln2_g, ln2_b):
    B, S, _ = x.shape
    h = x @ w_in + b_in
    def heads(t):
        return t.reshape(B, S, N_ATTN_HEADS, HEAD_DIM).transpose(0, 2, 1, 3)
    q = heads(h[..., :ATTN_WIDTH])
    k = heads(h[..., ATTN_WIDTH:2 * ATTN_WIDTH])
    v = heads(h[..., 2 * ATTN_WIDTH:3 * ATTN_WIDTH])
    attn = _dilated_attention(q, k, v, rel_table)
    attn = attn.transpose(0, 2, 1, 3).reshape(B, S, ATTN_WIDTH).astype(x.dtype)

    a, g = jnp.split(h[..., 3 * ATTN_WIDTH:], 2, axis=-1)
    u = a * jax.nn.sigmoid(g)
    u = _causal_dwconv(u, conv_w, conv_b)
    u = jax.nn.silu(_layer_norm(u, conv_ln_g, conv_ln_b))

    mixed = jnp.concatenate([_rms_norm(attn, attn_norm_g), _rms_norm(u, conv_norm_g)], axis=-1)
    x = _layer_norm(ALPHA * x + mixed @ w_out, ln1_g, ln1_b)

    up = _causal_dwconv(x @ w_up, ffn_conv_w, ffn_conv_b)
    gate, val = jnp.split(up, 2, axis=-1)
    y = (jax.nn.silu(gate) * val) @ w_down
    return _layer_norm(ALPHA * x + y, ln2_g, ln2_b)


def _fwd_setup_inputs(seed: int = 0) -> dict:
    key = jax.random.key(seed)
    ks = jax.random.split(key, 20)
    f32 = jnp.float32
    nrm = lambda k, shape, s: jax.random.normal(k, shape, f32) * s
    w_in = nrm(ks[1], (DEPTH, D_MODEL, IN_WIDTH), D_MODEL ** -0.5)
    v_scale = jnp.ones((IN_WIDTH,), f32).at[2 * ATTN_WIDTH:3 * ATTN_WIDTH].set(BETA)
    w_in = w_in * v_scale
    return {
        "x": jax.random.normal(ks[0], (BATCH, SEQ, D_MODEL), f32),
        "rel_table": nrm(ks[2], (REL_BUCKETS, N_ATTN_HEADS), 0.5),
        "w_in": w_in,
        "b_in": nrm(ks[3], (DEPTH, IN_WIDTH), 0.02),
        "conv_w": nrm(ks[4], (DEPTH, CONV_KERNEL, CONV_WIDTH), CONV_KERNEL ** -0.5),
        "conv_b": nrm(ks[5], (DEPTH, CONV_WIDTH), 0.02),
        "conv_ln_g": 1.0 + nrm(ks[6], (DEPTH, CONV_WIDTH), 0.02),
        "conv_ln_b": nrm(ks[7], (DEPTH, CONV_WIDTH), 0.02),
        "attn_norm_g": 1.0 + nrm(ks[8], (DEPTH, ATTN_WIDTH), 0.02),
        "conv_norm_g": 1.0 + nrm(ks[9], (DEPTH, CONV_WIDTH), 0.02),
        "w_out": nrm(ks[10], (DEPTH, D_MODEL, D_MODEL), BETA * D_MODEL ** -0.5),
        "ln1_g": 1.0 + nrm(ks[11], (DEPTH, D_MODEL), 0.02),
        "ln1_b": nrm(ks[12], (DEPTH, D_MODEL), 0.02),
        "w_up": nrm(ks[13], (DEPTH, D_MODEL, 2 * D_FF), D_MODEL ** -0.5),
        "ffn_conv_w": nrm(ks[14], (DEPTH, FFN_CONV_KERNEL, 2 * D_FF), FFN_CONV_KERNEL ** -0.5),
        "ffn_conv_b": nrm(ks[15], (DEPTH, 2 * D_FF), 0.02),
        "w_down": nrm(ks[16], (DEPTH, D_FF, D_MODEL), BETA * D_FF ** -0.5),
        "ln2_g": 1.0 + nrm(ks[17], (DEPTH, D_MODEL), 0.02),
        "ln2_b": nrm(ks[18], (DEPTH, D_MODEL), 0.02),
    }


def _fwd_reference(x, rel_table, w_in, b_in, conv_w, conv_b, conv_ln_g, conv_ln_b,
              attn_norm_g, conv_norm_g, w_out, ln1_g, ln1_b,
              w_up, ffn_conv_w, ffn_conv_b, w_down, ln2_g, ln2_b):
    for i in range(DEPTH):
        x = _layer(x, rel_table, w_in[i], b_in[i], conv_w[i], conv_b[i], conv_ln_g[i],
                   conv_ln_b[i], attn_norm_g[i], conv_norm_g[i], w_out[i], ln1_g[i],
                   ln1_b[i], w_up[i], ffn_conv_w[i], ffn_conv_b[i], w_down[i],
                   ln2_g[i], ln2_b[i])
    return x


import jax as _jax
import jax.numpy as _jnp

TWIN_FORMAT = 'train_step'
FWD_PARAMS = ['x', 'rel_table', 'w_in', 'b_in', 'conv_w', 'conv_b', 'conv_ln_g', 'conv_ln_b', 'attn_norm_g', 'conv_norm_g', 'w_out', 'ln1_g', 'ln1_b', 'w_up', 'ffn_conv_w', 'ffn_conv_b', 'w_down', 'ln2_g', 'ln2_b']
TWIN_WEIGHTS = ['rel_table', 'w_in', 'b_in', 'conv_w', 'conv_b', 'conv_ln_g', 'conv_ln_b', 'attn_norm_g', 'conv_norm_g', 'w_out', 'ln1_g', 'ln1_b', 'w_up', 'ffn_conv_w', 'ffn_conv_b', 'w_down', 'ln2_g', 'ln2_b']
TWIN_DIFF_INPUT = 'x'
TWIN_INPUTS = ['x', 'rel_table', 'w_in', 'b_in', 'conv_w', 'conv_b', 'conv_ln_g', 'conv_ln_b', 'attn_norm_g', 'conv_norm_g', 'w_out', 'ln1_g', 'ln1_b', 'w_up', 'ffn_conv_w', 'ffn_conv_b', 'w_down', 'ln2_g', 'ln2_b', 'loss_target', 'm_rel_table', 'm_w_in', 'm_b_in', 'm_conv_w', 'm_conv_b', 'm_conv_ln_g', 'm_conv_ln_b', 'm_attn_norm_g', 'm_conv_norm_g', 'm_w_out', 'm_ln1_g', 'm_ln1_b', 'm_w_up', 'm_ffn_conv_w', 'm_ffn_conv_b', 'm_w_down', 'm_ln2_g', 'm_ln2_b', 'v_rel_table', 'v_w_in', 'v_b_in', 'v_conv_w', 'v_conv_b', 'v_conv_ln_g', 'v_conv_ln_b', 'v_attn_norm_g', 'v_conv_norm_g', 'v_w_out', 'v_ln1_g', 'v_ln1_b', 'v_w_up', 'v_ffn_conv_w', 'v_ffn_conv_b', 'v_w_down', 'v_ln2_g', 'v_ln2_b']
TWIN_OUTPUTS = ['loss', 'grad_x', 'grad_rel_table', 'grad_w_in', 'grad_b_in', 'grad_conv_w', 'grad_conv_b', 'grad_conv_ln_g', 'grad_conv_ln_b', 'grad_attn_norm_g', 'grad_conv_norm_g', 'grad_w_out', 'grad_ln1_g', 'grad_ln1_b', 'grad_w_up', 'grad_ffn_conv_w', 'grad_ffn_conv_b', 'grad_w_down', 'grad_ln2_g', 'grad_ln2_b', 'delta_rel_table', 'delta_w_in', 'delta_b_in', 'delta_conv_w', 'delta_conv_b', 'delta_conv_ln_g', 'delta_conv_ln_b', 'delta_attn_norm_g', 'delta_conv_norm_g', 'delta_w_out', 'delta_ln1_g', 'delta_ln1_b', 'delta_w_up', 'delta_ffn_conv_w', 'delta_ffn_conv_b', 'delta_w_down', 'delta_ln2_g', 'delta_ln2_b', 'new_m_rel_table', 'new_m_w_in', 'new_m_b_in', 'new_m_conv_w', 'new_m_conv_b', 'new_m_conv_ln_g', 'new_m_conv_ln_b', 'new_m_attn_norm_g', 'new_m_conv_norm_g', 'new_m_w_out', 'new_m_ln1_g', 'new_m_ln1_b', 'new_m_w_up', 'new_m_ffn_conv_w', 'new_m_ffn_conv_b', 'new_m_w_down', 'new_m_ln2_g', 'new_m_ln2_b', 'new_v_rel_table', 'new_v_w_in', 'new_v_b_in', 'new_v_conv_w', 'new_v_conv_b', 'new_v_conv_ln_g', 'new_v_conv_ln_b', 'new_v_attn_norm_g', 'new_v_conv_norm_g', 'new_v_w_out', 'new_v_ln1_g', 'new_v_ln1_b', 'new_v_w_up', 'new_v_ffn_conv_w', 'new_v_ffn_conv_b', 'new_v_w_down', 'new_v_ln2_g', 'new_v_ln2_b']
TWIN_LEAF_KINDS = {'loss': 'loss', 'grad_x': 'grad_x', 'grad_rel_table': 'grad_w', 'grad_w_in': 'grad_w', 'grad_b_in': 'grad_w', 'grad_conv_w': 'grad_w', 'grad_conv_b': 'grad_w', 'grad_conv_ln_g': 'grad_w', 'grad_conv_ln_b': 'grad_w', 'grad_attn_norm_g': 'grad_w', 'grad_conv_norm_g': 'grad_w', 'grad_w_out': 'grad_w', 'grad_ln1_g': 'grad_w', 'grad_ln1_b': 'grad_w', 'grad_w_up': 'grad_w', 'grad_ffn_conv_w': 'grad_w', 'grad_ffn_conv_b': 'grad_w', 'grad_w_down': 'grad_w', 'grad_ln2_g': 'grad_w', 'grad_ln2_b': 'grad_w', 'delta_rel_table': 'delta_w', 'delta_w_in': 'delta_w', 'delta_b_in': 'delta_w', 'delta_conv_w': 'delta_w', 'delta_conv_b': 'delta_w', 'delta_conv_ln_g': 'delta_w', 'delta_conv_ln_b': 'delta_w', 'delta_attn_norm_g': 'delta_w', 'delta_conv_norm_g': 'delta_w', 'delta_w_out': 'delta_w', 'delta_ln1_g': 'delta_w', 'delta_ln1_b': 'delta_w', 'delta_w_up': 'delta_w', 'delta_ffn_conv_w': 'delta_w', 'delta_ffn_conv_b': 'delta_w', 'delta_w_down': 'delta_w', 'delta_ln2_g': 'delta_w', 'delta_ln2_b': 'delta_w', 'new_m_rel_table': 'new_m', 'new_m_w_in': 'new_m', 'new_m_b_in': 'new_m', 'new_m_conv_w': 'new_m', 'new_m_conv_b': 'new_m', 'new_m_conv_ln_g': 'new_m', 'new_m_conv_ln_b': 'new_m', 'new_m_attn_norm_g': 'new_m', 'new_m_conv_norm_g': 'new_m', 'new_m_w_out': 'new_m', 'new_m_ln1_g': 'new_m', 'new_m_ln1_b': 'new_m', 'new_m_w_up': 'new_m', 'new_m_ffn_conv_w': 'new_m', 'new_m_ffn_conv_b': 'new_m', 'new_m_w_down': 'new_m', 'new_m_ln2_g': 'new_m', 'new_m_ln2_b': 'new_m', 'new_v_rel_table': 'new_v', 'new_v_w_in': 'new_v', 'new_v_b_in': 'new_v', 'new_v_conv_w': 'new_v', 'new_v_conv_b': 'new_v', 'new_v_conv_ln_g': 'new_v', 'new_v_conv_ln_b': 'new_v', 'new_v_attn_norm_g': 'new_v', 'new_v_conv_norm_g': 'new_v', 'new_v_w_out': 'new_v', 'new_v_ln1_g': 'new_v', 'new_v_ln1_b': 'new_v', 'new_v_w_up': 'new_v', 'new_v_ffn_conv_w': 'new_v', 'new_v_ffn_conv_b': 'new_v', 'new_v_w_down': 'new_v', 'new_v_ln2_g': 'new_v', 'new_v_ln2_b': 'new_v'}


def _forward(args):
    return _fwd_reference(*[args[k] for k in FWD_PARAMS])


def _output_shape():
    out = _jax.eval_shape(lambda: _forward(_fwd_setup_inputs(0)))
    return out.shape, out.dtype

N_MICROBATCH = 1
ADAM_LR = 0.001
ADAM_B1 = 0.9
ADAM_B2 = 0.999
ADAM_EPS = 1e-08
ADAM_WD = 0.01
ADAM_STEP = 10
PER_EXAMPLE_BATCH_AXIS = {'x': 0, 'loss_target': 0}
SHARED_INPUTS = []
_WEIGHT_DTYPES = {'rel_table': _jnp.float32, 'w_in': _jnp.float32, 'b_in': _jnp.float32, 'conv_w': _jnp.float32, 'conv_b': _jnp.float32, 'conv_ln_g': _jnp.float32, 'conv_ln_b': _jnp.float32, 'attn_norm_g': _jnp.float32, 'conv_norm_g': _jnp.float32, 'w_out': _jnp.float32, 'ln1_g': _jnp.float32, 'ln1_b': _jnp.float32, 'w_up': _jnp.float32, 'ffn_conv_w': _jnp.float32, 'ffn_conv_b': _jnp.float32, 'w_down': _jnp.float32, 'ln2_g': _jnp.float32, 'ln2_b': _jnp.float32}
MOMENT_SCALE = {'rel_table': 1.049101e-01, 'w_in': 9.535781e-02, 'b_in': 1.877331e+00, 'conv_w': 8.690848e-02, 'conv_b': 2.688997e-01, 'conv_ln_g': 1.258633e-01, 'conv_ln_b': 1.749835e-01, 'attn_norm_g': 8.718200e-02, 'conv_norm_g': 9.204590e-02, 'w_out': 1.568674e-01, 'ln1_g': 9.271859e-01, 'ln1_b': 4.562816e-01, 'w_up': 3.145693e-02, 'ffn_conv_w': 3.234273e-02, 'ffn_conv_b': 3.742938e-02, 'w_down': 8.706012e-02, 'ln2_g': 3.201772e+01, 'ln2_b': 4.159108e+00}


def _to_microbatches(a, axis):
    t = _jnp.moveaxis(a, axis, 0)
    t = t.reshape((N_MICROBATCH, t.shape[0] // N_MICROBATCH) + t.shape[1:])
    return _jnp.moveaxis(t, 1, axis + 1)


def setup_inputs(seed: int = 0) -> dict:
    inp = _fwd_setup_inputs(seed)
    key = _jax.random.fold_in(_jax.random.key(seed), 7919)
    shape, _ = _output_shape()
    out = dict(inp)
    out["loss_target"] = _jax.random.normal(_jax.random.fold_in(key, 0), shape, _jnp.float32)
    for i, name in enumerate(TWIN_WEIGHTS):
        w = inp[name].astype(_jnp.float32)
        if MOMENT_SCALE is None:
            s = _jnp.sqrt(_jnp.mean(_jnp.square(w)) + 1e-30)
        else:
            s = MOMENT_SCALE[name]
        km, kv = _jax.random.split(_jax.random.fold_in(key, i + 1))
        out[name] = w
        out["m_" + name] = s * _jax.random.normal(km, w.shape, _jnp.float32)
        out["v_" + name] = (s * s) * _jax.random.uniform(kv, w.shape, _jnp.float32, 0.5, 1.5)
    if N_MICROBATCH > 1:
        for name, axis in PER_EXAMPLE_BATCH_AXIS.items():
            out[name] = _to_microbatches(out[name], axis)
    return {'x': out['x'], 'rel_table': out['rel_table'], 'w_in': out['w_in'], 'b_in': out['b_in'], 'conv_w': out['conv_w'], 'conv_b': out['conv_b'], 'conv_ln_g': out['conv_ln_g'], 'conv_ln_b': out['conv_ln_b'], 'attn_norm_g': out['attn_norm_g'], 'conv_norm_g': out['conv_norm_g'], 'w_out': out['w_out'], 'ln1_g': out['ln1_g'], 'ln1_b': out['ln1_b'], 'w_up': out['w_up'], 'ffn_conv_w': out['ffn_conv_w'], 'ffn_conv_b': out['ffn_conv_b'], 'w_down': out['w_down'], 'ln2_g': out['ln2_g'], 'ln2_b': out['ln2_b'], 'loss_target': out['loss_target'], 'm_rel_table': out['m_rel_table'], 'm_w_in': out['m_w_in'], 'm_b_in': out['m_b_in'], 'm_conv_w': out['m_conv_w'], 'm_conv_b': out['m_conv_b'], 'm_conv_ln_g': out['m_conv_ln_g'], 'm_conv_ln_b': out['m_conv_ln_b'], 'm_attn_norm_g': out['m_attn_norm_g'], 'm_conv_norm_g': out['m_conv_norm_g'], 'm_w_out': out['m_w_out'], 'm_ln1_g': out['m_ln1_g'], 'm_ln1_b': out['m_ln1_b'], 'm_w_up': out['m_w_up'], 'm_ffn_conv_w': out['m_ffn_conv_w'], 'm_ffn_conv_b': out['m_ffn_conv_b'], 'm_w_down': out['m_w_down'], 'm_ln2_g': out['m_ln2_g'], 'm_ln2_b': out['m_ln2_b'], 'v_rel_table': out['v_rel_table'], 'v_w_in': out['v_w_in'], 'v_b_in': out['v_b_in'], 'v_conv_w': out['v_conv_w'], 'v_conv_b': out['v_conv_b'], 'v_conv_ln_g': out['v_conv_ln_g'], 'v_conv_ln_b': out['v_conv_ln_b'], 'v_attn_norm_g': out['v_attn_norm_g'], 'v_conv_norm_g': out['v_conv_norm_g'], 'v_w_out': out['v_w_out'], 'v_ln1_g': out['v_ln1_g'], 'v_ln1_b': out['v_ln1_b'], 'v_w_up': out['v_w_up'], 'v_ffn_conv_w': out['v_ffn_conv_w'], 'v_ffn_conv_b': out['v_ffn_conv_b'], 'v_w_down': out['v_w_down'], 'v_ln2_g': out['v_ln2_g'], 'v_ln2_b': out['v_ln2_b']}


def _loss(weights, diff, rest, loss_target):
    with _jax.named_scope("forward"):
        args = {**rest, TWIN_DIFF_INPUT: diff, **{k: w.astype(_WEIGHT_DTYPES[k]) for k, w in weights.items()}}
        y = _forward(args)
    with _jax.named_scope("loss_head"):
        err = _jnp.square(y.astype(_jnp.float32) - loss_target)
        return 0.5 * _jnp.sum(_jnp.mean(err, axis=-1)) if err.ndim else 0.5 * err


def _adamw(w, g, m, v):
    m = ADAM_B1 * m + (1.0 - ADAM_B1) * g
    v = ADAM_B2 * v + (1.0 - ADAM_B2) * _jnp.square(g)
    m_hat = m / (1.0 - ADAM_B1 ** ADAM_STEP)
    v_hat = v / (1.0 - ADAM_B2 ** ADAM_STEP)
    delta = -ADAM_LR * (m_hat / (_jnp.sqrt(v_hat) + ADAM_EPS) + ADAM_WD * w)
    return delta, m, v


def reference(x, rel_table, w_in, b_in, conv_w, conv_b, conv_ln_g, conv_ln_b, attn_norm_g, conv_norm_g, w_out, ln1_g, ln1_b, w_up, ffn_conv_w, ffn_conv_b, w_down, ln2_g, ln2_b, loss_target, m_rel_table, m_w_in, m_b_in, m_conv_w, m_conv_b, m_conv_ln_g, m_conv_ln_b, m_attn_norm_g, m_conv_norm_g, m_w_out, m_ln1_g, m_ln1_b, m_w_up, m_ffn_conv_w, m_ffn_conv_b, m_w_down, m_ln2_g, m_ln2_b, v_rel_table, v_w_in, v_b_in, v_conv_w, v_conv_b, v_conv_ln_g, v_conv_ln_b, v_attn_norm_g, v_conv_norm_g, v_w_out, v_ln1_g, v_ln1_b, v_w_up, v_ffn_conv_w, v_ffn_conv_b, v_w_down, v_ln2_g, v_ln2_b):
    given = dict(x=x, rel_table=rel_table, w_in=w_in, b_in=b_in, conv_w=conv_w, conv_b=conv_b, conv_ln_g=conv_ln_g, conv_ln_b=conv_ln_b, attn_norm_g=attn_norm_g, conv_norm_g=conv_norm_g, w_out=w_out, ln1_g=ln1_g, ln1_b=ln1_b, w_up=w_up, ffn_conv_w=ffn_conv_w, ffn_conv_b=ffn_conv_b, w_down=w_down, ln2_g=ln2_g, ln2_b=ln2_b, loss_target=loss_target, m_rel_table=m_rel_table, m_w_in=m_w_in, m_b_in=m_b_in, m_conv_w=m_conv_w, m_conv_b=m_conv_b, m_conv_ln_g=m_conv_ln_g, m_conv_ln_b=m_conv_ln_b, m_attn_norm_g=m_attn_norm_g, m_conv_norm_g=m_conv_norm_g, m_w_out=m_w_out, m_ln1_g=m_ln1_g, m_ln1_b=m_ln1_b, m_w_up=m_w_up, m_ffn_conv_w=m_ffn_conv_w, m_ffn_conv_b=m_ffn_conv_b, m_w_down=m_w_down, m_ln2_g=m_ln2_g, m_ln2_b=m_ln2_b, v_rel_table=v_rel_table, v_w_in=v_w_in, v_b_in=v_b_in, v_conv_w=v_conv_w, v_conv_b=v_conv_b, v_conv_ln_g=v_conv_ln_g, v_conv_ln_b=v_conv_ln_b, v_attn_norm_g=v_attn_norm_g, v_conv_norm_g=v_conv_norm_g, v_w_out=v_w_out, v_ln1_g=v_ln1_g, v_ln1_b=v_ln1_b, v_w_up=v_w_up, v_ffn_conv_w=v_ffn_conv_w, v_ffn_conv_b=v_ffn_conv_b, v_w_down=v_w_down, v_ln2_g=v_ln2_g, v_ln2_b=v_ln2_b)
    weights = {n: given[n] for n in TWIN_WEIGHTS}
    shared = {n: given[n] for n in SHARED_INPUTS}
    per_example = {n: given[n] for n in ['x']}
    grad_fn = _jax.value_and_grad(_loss, argnums=(0, 1))

    def one_microbatch(ex, loss_target):
        ex = dict(ex)
        diff = ex.pop(TWIN_DIFF_INPUT)
        return grad_fn(weights, diff, {**shared, **ex}, loss_target)

    if N_MICROBATCH == 1:
        loss, (grad_w, grad_x) = one_microbatch(per_example, given["loss_target"])
    else:
        def body(carry, xs):
            loss_sum, grad_sum = carry
            l_k, (gw_k, gx_k) = one_microbatch(xs[0], xs[1])
            with _jax.named_scope("update"):
                return (loss_sum + l_k, _jax.tree.map(_jnp.add, grad_sum, gw_k)), gx_k

        init = (_jnp.zeros((), _jnp.float32), _jax.tree.map(_jnp.zeros_like, weights))
        (loss, grad_w), grad_x = _jax.lax.scan(body, init, (per_example, given["loss_target"]))
    with _jax.named_scope("update"):
        delta_w, new_m, new_v = {}, {}, {}
        for n in TWIN_WEIGHTS:
            delta_w[n], new_m[n], new_v[n] = _adamw(weights[n], grad_w[n], given["m_" + n], given["v_" + n])
    return (loss, grad_x, *[grad_w[n] for n in TWIN_WEIGHTS], *[delta_w[n] for n in TWIN_WEIGHTS],
            *[new_m[n] for n in TWIN_WEIGHTS], *[new_v[n] for n in TWIN_WEIGHTS])
```

```python
import functools
import math

import numpy as np
import jax
import jax.numpy as jnp
from jax import lax
from jax.experimental import pallas as pl
from jax.experimental.pallas import tpu as pltpu

F32 = jnp.float32
BF16 = jnp.bfloat16
SDS = jax.ShapeDtypeStruct

NDEV = 8
D = 1024
S = 2048
BL = 2
T = BL * S
NH = 12
HD = 64
AW = NH * HD
CW = D - AW
INW = 3 * AW + 2 * CW
CK = 31
DFF = 2816
FK = 3
BLK = 128
NBUCKET = 32
BRANCHES = ((128, 1), (512, 4), (2048, 16))
ALPHA = 2.0 ** 0.25
LN_EPS = 1e-5
NEG_INF = -1e30
LR, B1, B2, AEPS, WD, STEP = 0.001, 0.9, 0.999, 1e-08, 0.01, 10

TM = 512
FT = 256
NFT = DFF // FT
PACK_LANES = 128

assert all(w // d == BLK for w, d in BRANCHES)


def _dot(a, b):
    return jnp.dot(a, b, preferred_element_type=F32)


def _dot_nt(a, b):
    return lax.dot_general(a, b, (((1,), (1,)), ((), ())), preferred_element_type=F32)


def _dot_tn(a, b):
    return lax.dot_general(a, b, (((0,), (0,)), ((), ())), preferred_element_type=F32)


def _rowmean(v):
    return jnp.mean(v, axis=-1, keepdims=True)


def _colsum(v):
    return jnp.sum(v, axis=0, keepdims=True)


def _sigmoid(v):
    return jax.nn.sigmoid(v)


def _exchange(items, name):
    n = len(items)
    arrs = [a for a, _ in items]
    kinds = [k for _, k in items]
    out_shapes = []
    for a, k in items:
        shp = (NDEV,) + tuple(a.shape) if k == "gather" else tuple(a.shape)
        out_shapes.append(SDS(shp, a.dtype))

    def body(*refs):
        ins = refs[:n]
        outs = refs[n:2 * n]
        send_sems, recv_sems, local_sems = refs[2 * n:]
        x, y, c = lax.axis_index("x"), lax.axis_index("y"), lax.axis_index("c")
        me = 4 * x + 2 * y + c

        def peer(k):
            px = 1 - x if k & 4 else x
            py = 1 - y if k & 2 else y
            pc = 1 - c if k & 1 else c
            return (px, py, pc), 4 * px + 2 * py + pc

        local = []
        for i in range(n):
            src = ins[i] if kinds[i] == "gather" else ins[i].at[me]
            cp = pltpu.make_async_copy(src, outs[i].at[me], local_sems.at[i])
            cp.start()
            local.append(cp)
        sends = []
        for k in range(1, NDEV):
            dev, pid = peer(k)
            for i in range(n):
                src = ins[i] if kinds[i] == "gather" else ins[i].at[pid]
                cp = pltpu.make_async_remote_copy(
                    src_ref=src, dst_ref=outs[i].at[me],
                    send_sem=send_sems.at[i, k - 1], recv_sem=recv_sems.at[i, k - 1],
                    device_id=dev, device_id_type=pl.DeviceIdType.MESH)
                cp.start()
                sends.append(cp)
        for k in range(1, NDEV):
            dev, pid = peer(k)
            for i in range(n):
                src = ins[i] if kinds[i] == "gather" else ins[i].at[pid]
                pltpu.make_async_remote_copy(
                    src_ref=src, dst_ref=outs[i].at[pid],
                    send_sem=send_sems.at[i, k - 1], recv_sem=recv_sems.at[i, k - 1],
                    device_id=dev, device_id_type=pl.DeviceIdType.MESH).wait_recv()
        for cp in sends:
            cp.wait_send()
        for cp in local:
            cp.wait()

    any_spec = pl.BlockSpec(memory_space=pl.ANY)
    return pl.pallas_call(
        body, name=name,
        out_shape=tuple(out_shapes),
        in_specs=[any_spec] * n,
        out_specs=tuple([any_spec] * n),
        scratch_shapes=[pltpu.SemaphoreType.DMA((n, NDEV - 1)),
                        pltpu.SemaphoreType.DMA((n, NDEV - 1)),
                        pltpu.SemaphoreType.DMA((n,))],
        compiler_params=pltpu.CompilerParams(has_side_effects=True),
    )(*arrs)


def _mm_nn_bias(a, b, bias, col_blk0, nblk, tn, out_dtype, name):
    m_, k_ = a.shape

    def body(a_ref, b_ref, bias_ref, o_ref):
        acc = _dot(a_ref[...].astype(BF16), b_ref[...])
        o_ref[...] = (acc + bias_ref[...]).astype(o_ref.dtype)

    return pl.pallas_call(
        body, name=name, grid=(nblk, m_ // TM),
        in_specs=[pl.BlockSpec((TM, k_), lambda n, m: (m, 0)),
                  pl.BlockSpec((k_, tn), lambda n, m: (0, col_blk0 + n)),
                  pl.BlockSpec((1, tn), lambda n, m: (0, col_blk0 + n))],
        out_specs=pl.BlockSpec((TM, tn), lambda n, m: (m, n)),
        out_shape=SDS((m_, nblk * tn), out_dtype),
    )(a, b, bias)


def _mm_nt(a, b, res, res_scale, name):
    m_, k_ = a.shape
    n_ = b.shape[0]
    has_res = res is not None

    def body(*refs):
        if has_res:
            a_ref, b_ref, r_ref, o_ref = refs
        else:
            a_ref, b_ref, o_ref = refs
        acc = _dot_nt(a_ref[...].astype(BF16), b_ref[...].astype(BF16))
        if has_res:
            acc = acc + res_scale * r_ref[...]
        o_ref[...] = acc

    in_specs = [pl.BlockSpec((TM, k_), lambda m: (m, 0)), pl.BlockSpec((n_, k_), lambda m: (0, 0))]
    args = [a, b]
    if has_res:
        in_specs.append(pl.BlockSpec((TM, n_), lambda m: (m, 0)))
        args.append(res)
    return pl.pallas_call(
        body, name=name, grid=(m_ // TM,), in_specs=in_specs,
        out_specs=pl.BlockSpec((TM, n_), lambda m: (m, 0)),
        out_shape=SDS((m_, n_), F32),
    )(*args)


def _mm_tn(a, b, tn, tk, name):
    t_, na = a.shape
    nb = b.shape[1]

    def body(a_ref, b_ref, o_ref):
        @pl.when(pl.program_id(1) == 0)
        def _():
            o_ref[...] = jnp.zeros_like(o_ref)
        o_ref[...] += _dot_tn(a_ref[...].astype(BF16), b_ref[...].astype(BF16))

    return pl.pallas_call(
        body, name=name, grid=(na // tn, t_ // tk),
        in_specs=[pl.BlockSpec((tk, tn), lambda n, k: (k, n)),
                  pl.BlockSpec((tk, nb), lambda n, k: (k, 0))],
        out_specs=pl.BlockSpec((tn, nb), lambda n, k: (n, 0)),
        out_shape=SDS((na, nb), F32),
    )(a, b)


def _bucket_maps():
    qi = np.arange(BLK)[:, None]
    kj = np.arange(2 * BLK)[None, :]
    steps = np.maximum(qi + BLK - kj, 0)
    exact = NBUCKET // 2
    maps = []
    for _, dil in BRANCHES:
        dist = steps * dil
        d_f = np.maximum(dist, 1).astype(np.float32)
        large = exact + (np.log(d_f / np.float32(exact)) / np.float32(math.log(S / exact))
                         * np.float32(NBUCKET - exact)).astype(np.int32)
        large = np.minimum(large, NBUCKET - 1)
        maps.append(np.where(dist < exact, dist, large).astype(np.int32))
    return np.stack(maps)


def _bias_table(rel_table, buckets):
    def body(t_ref, b_ref, o_ref):
        bk = b_ref[0]
        for h in range(NH):
            acc = jnp.zeros((BLK, 2 * BLK), F32)
            for k in range(NBUCKET):
                acc = jnp.where(bk == k, t_ref[k, h], acc)
            o_ref[0, h] = acc

    return pl.pallas_call(
        body, name="bias_table", grid=(len(BRANCHES),),
        in_specs=[pl.BlockSpec(memory_space=pltpu.SMEM),
                  pl.BlockSpec((1, BLK, 2 * BLK), lambda i: (i, 0, 0))],
        out_specs=pl.BlockSpec((1, NH, BLK, 2 * BLK), lambda i: (i, 0, 0, 0)),
        out_shape=SDS((len(BRANCHES), NH, BLK, 2 * BLK), F32),
    )(rel_table, buckets)


def _rel_table_grad(dbias, buckets):
    def body(d_ref, b_ref, o_ref):
        h = pl.program_id(0)
        for k in range(NBUCKET):
            tot = jnp.zeros((1, 1), F32)
            for br in range(len(BRANCHES)):
                sel = jnp.where(b_ref[br] == k, d_ref[br, 0], 0.0)
                tot = tot + jnp.sum(jnp.sum(sel, axis=1, keepdims=True), axis=0, keepdims=True)
            o_ref[0, :, pl.ds(k, 1)] = tot

    out = pl.pallas_call(
        body, name="rel_table_grad", grid=(NH,),
        in_specs=[pl.BlockSpec((len(BRANCHES), 1, BLK, 2 * BLK), lambda h: (0, h, 0, 0)),
                  pl.BlockSpec((len(BRANCHES), BLK, 2 * BLK), lambda h: (0, 0, 0))],
        out_specs=pl.BlockSpec((1, 1, NBUCKET), lambda h: (h, 0, 0)),
        out_shape=SDS((NH, 1, NBUCKET), F32),
    )(dbias, buckets)
    return out.reshape(NH, NBUCKET).T


def _block_rows(br, i):
    _, dil = BRANCHES[br]
    nb = S // dil // BLK
    if nb == 16:
        r, nidx = 0, i
    elif nb == 4:
        r, nidx = lax.shift_right_logical(i, 2), lax.bitwise_and(i, 3)
    else:
        r, nidx = i, 0
    start = r + dil * BLK * nidx
    if nb == 1:
        return start, None, None
    prev = r + dil * BLK * jnp.maximum(nidx - 1, 0)
    return start, prev, nidx > 0


def _rows(start, dil):
    if dil == 1:
        return pl.ds(pl.multiple_of(start, BLK), BLK)
    return pl.ds(start, BLK, stride=dil)


def _attn_masks():
    lane = lax.broadcasted_iota(jnp.int32, (BLK, BLK), 1)
    qi = lax.broadcasted_iota(jnp.int32, (BLK, BLK), 0)
    head0 = lane < HD
    valid_cur = lane <= qi
    valid_prev = lane >= qi
    return head0, valid_cur, valid_prev


def _attn_fwd(qkv, bias):
    scale = 1.0 / math.sqrt(HD)
    nbr = len(BRANCHES)

    def body(q_ref, k_ref, v_ref, bias_ref, o_ref, lse_ref, qf, kf, vf, ob, mb, lb):
        qf[...] = q_ref[...].astype(F32)
        kf[...] = k_ref[...].astype(F32)
        vf[...] = v_ref[...].astype(F32)
        head0, valid_cur, valid_prev = _attn_masks()

        for br in range(nbr):
            dil = BRANCHES[br][1]

            def blk(i, carry, br=br, dil=dil):
                start, prev, has_prev = _block_rows(br, i)
                rows = _rows(start, dil)
                q = qf[rows, :]
                kc = kf[rows, :].astype(BF16)
                vc = vf[rows, :].astype(BF16)
                if prev is not None:
                    prows = _rows(prev, dil)
                    kp = kf[prows, :].astype(BF16)
                    vp = vf[prows, :].astype(BF16)
                    ok_prev = jnp.logical_and(valid_prev, has_prev)
                o_acc = jnp.zeros((BLK, BLK), F32)
                m_acc = jnp.zeros((BLK, BLK), F32)
                l_acc = jnp.zeros((BLK, BLK), F32)
                for j in range(2):
                    mj = head0 if j == 0 else jnp.logical_not(head0)
                    qj = jnp.where(mj, q, 0.0).astype(BF16)
                    sc = _dot_nt(qj, kc) * scale + bias_ref[br, j, :, BLK:]
                    sc = jnp.where(valid_cur, sc, NEG_INF)
                    mx = jnp.max(sc, axis=-1, keepdims=True)
                    if prev is not None:
                        sp = _dot_nt(qj, kp) * scale + bias_ref[br, j, :, :BLK]
                        sp = jnp.where(ok_prev, sp, NEG_INF)
                        mx = jnp.maximum(mx, jnp.max(sp, axis=-1, keepdims=True))
                    pc = jnp.exp(sc - mx)
                    ls = jnp.sum(pc, axis=-1, keepdims=True)
                    o = _dot(pc.astype(BF16), vc)
                    if prev is not None:
                        pp = jnp.exp(sp - mx)
                        ls = ls + jnp.sum(pp, axis=-1, keepdims=True)
                        o = o + _dot(pp.astype(BF16), vp)
                    o_acc = jnp.where(mj, o, o_acc)
                    m_acc = jnp.where(mj, mx, m_acc)
                    l_acc = jnp.where(mj, ls, l_acc)
                ob[br, rows, :] = o_acc
                mb[br, rows, :] = m_acc
                lb[br, rows, :] = l_acc
                return carry

            lax.fori_loop(0, 16, blk, 0)

        def merge(i, carry):
            rows = pl.ds(pl.multiple_of(i * 256, 256), 256)
            m_all = jnp.maximum(jnp.maximum(mb[0, rows, :], mb[1, rows, :]), mb[2, rows, :])
            num = jnp.zeros((256, BLK), F32)
            den = jnp.zeros((256, BLK), F32)
            for br in range(nbr):
                c = jnp.exp(mb[br, rows, :] - m_all)
                num = num + ob[br, rows, :] * c
                den = den + lb[br, rows, :] * c
            o_ref[rows, :] = num / den
            lse_ref[rows, :] = m_all + jnp.log(den)
            return carry

        lax.fori_loop(0, S // 256, merge, 0)

    npair = NH // 2
    blk_spec = lambda off: pl.BlockSpec((S, BLK), lambda b, hp: (b, off + hp))
    return pl.pallas_call(
        body, name="attn_fwd", grid=(BL, npair),
        in_specs=[blk_spec(0), blk_spec(npair), blk_spec(2 * npair),
                  pl.BlockSpec((nbr, 2, BLK, 2 * BLK), lambda b, hp: (0, hp, 0, 0))],
        out_specs=(blk_spec(0), blk_spec(0)),
        out_shape=(SDS((T, AW), F32), SDS((T, AW), F32)),
        scratch_shapes=[pltpu.VMEM((S, BLK), F32)] * 3 + [pltpu.VMEM((nbr, S, BLK), F32)] * 3,
    )(qkv, qkv, qkv, bias)


def _attn_bwd(qkv, attn, lse, dattn, bias):
    scale = 1.0 / math.sqrt(HD)
    nbr = len(BRANCHES)

    def body(q_ref, k_ref, v_ref, o_ref, lse_ref, do_ref, bias_ref,
             dq_ref, dk_ref, dv_ref, sq_ref, sk_ref, sv_ref, db_ref,
             qf, kf, vf, dl, dqa, dka, dva):
        b = pl.program_id(1)
        qf[...] = q_ref[...].astype(F32)
        kf[...] = k_ref[...].astype(F32)
        vf[...] = v_ref[...].astype(F32)
        dqa[...] = jnp.zeros_like(dqa)
        dka[...] = jnp.zeros_like(dka)
        dva[...] = jnp.zeros_like(dva)
        head0, valid_cur, valid_prev = _attn_masks()

        @pl.when(b == 0)
        def _():
            db_ref[...] = jnp.zeros_like(db_ref)
            sq_ref[...] = jnp.zeros_like(sq_ref)
            sk_ref[...] = jnp.zeros_like(sk_ref)
            sv_ref[...] = jnp.zeros_like(sv_ref)

        def delta(i, carry):
            rows = pl.ds(pl.multiple_of(i * 256, 256), 256)
            prod = do_ref[rows, :] * o_ref[rows, :]
            h0 = lax.broadcasted_iota(jnp.int32, (256, BLK), 1) < HD
            d0 = jnp.sum(jnp.where(h0, prod, 0.0), axis=-1, keepdims=True)
            d1 = jnp.sum(jnp.where(h0, 0.0, prod), axis=-1, keepdims=True)
            dl[rows, :] = jnp.where(h0, d0, d1)
            return carry

        lax.fori_loop(0, S // 256, delta, 0)

        for br in range(nbr):
            dil = BRANCHES[br][1]

            def blk(i, carry, br=br, dil=dil):
                start, prev, has_prev = _block_rows(br, i)
                rows = _rows(start, dil)
                q = qf[rows, :]
                kc = kf[rows, :].astype(BF16)
                vc = vf[rows, :].astype(BF16)
                do = do_ref[rows, :]
                lse_b = lse_ref[rows, :]
                dl_b = dl[rows, :]
                if prev is not None:
                    prows = _rows(prev, dil)
                    kp = kf[prows, :].astype(BF16)
                    vp = vf[prows, :].astype(BF16)
                    ok_prev = jnp.logical_and(valid_prev, has_prev)
                    dk_p = jnp.zeros((BLK, BLK), F32)
                    dv_p = jnp.zeros((BLK, BLK), F32)
                dq = jnp.zeros((BLK, BLK), F32)
                dk_c = jnp.zeros((BLK, BLK), F32)
                dv_c = jnp.zeros((BLK, BLK), F32)
                for j in range(2):
                    mj = head0 if j == 0 else jnp.logical_not(head0)
                    qj = jnp.where(mj, q, 0.0).astype(BF16)
                    doj = jnp.where(mj, do, 0.0).astype(BF16)
                    lse_j = lse_b[:, j * HD:j * HD + 1]
                    dl_j = dl_b[:, j * HD:j * HD + 1]
                    sc = _dot_nt(qj, kc) * scale + bias_ref[br, j, :, BLK:]
                    pc = jnp.where(valid_cur, jnp.exp(sc - lse_j), 0.0)
                    ds_c = pc * (_dot_nt(doj, vc) - dl_j)
                    db_ref[br, j, :, BLK:] += ds_c
                    dsb = (ds_c * scale).astype(BF16)
                    dqj = _dot(dsb, kc)
                    dk_c = dk_c + _dot_tn(dsb, qj)
                    dv_c = dv_c + _dot_tn(pc.astype(BF16), doj)
                    if prev is not None:
                        sp = _dot_nt(qj, kp) * scale + bias_ref[br, j, :, :BLK]
                        pp = jnp.where(ok_prev, jnp.exp(sp - lse_j), 0.0)
                        ds_p = pp * (_dot_nt(doj, vp) - dl_j)
                        db_ref[br, j, :, :BLK] += ds_p
                        dsbp = (ds_p * scale).astype(BF16)
                        dqj = dqj + _dot(dsbp, kp)
                        dk_p = dk_p + _dot_tn(dsbp, qj)
                        dv_p = dv_p + _dot_tn(pp.astype(BF16), doj)
                    dq = jnp.where(mj, dqj, dq)
                dqa[rows, :] = dqa[rows, :] + dq
                dka[rows, :] = dka[rows, :] + dk_c
                dva[rows, :] = dva[rows, :] + dv_c
                if prev is not None:
                    dka[prows, :] = dka[prows, :] + dk_p
                    dva[prows, :] = dva[prows, :] + dv_p
                return carry

            lax.fori_loop(0, 16, blk, 0)

        def flush(i, carry):
            rows = pl.ds(pl.multiple_of(i * 256, 256), 256)
            for acc, out, cs in ((dqa, dq_ref, sq_ref), (dka, dk_ref, sk_ref), (dva, dv_ref, sv_ref)):
                val = acc[rows, :]
                out[rows, :] = val.astype(BF16)
                cs[...] += _colsum(val)
            return carry

        lax.fori_loop(0, S // 256, flush, 0)

    npair = NH // 2
    blk_spec = lambda off: pl.BlockSpec((S, BLK), lambda hp, b: (b, off + hp))
    sum_spec = pl.BlockSpec((1, BLK), lambda hp, b: (0, hp))
    return pl.pallas_call(
        body, name="attn_bwd", grid=(npair, BL),
        in_specs=[blk_spec(0), blk_spec(npair), blk_spec(2 * npair), blk_spec(0), blk_spec(0), blk_spec(0),
                  pl.BlockSpec((nbr, 2, BLK, 2 * BLK), lambda hp, b: (0, hp, 0, 0))],
        out_specs=(blk_spec(0), blk_spec(0), blk_spec(0), sum_spec, sum_spec, sum_spec,
                   pl.BlockSpec((nbr, 2, BLK, 2 * BLK), lambda hp, b: (0, hp, 0, 0))),
        out_shape=(SDS((T, AW), BF16), SDS((T, AW), BF16), SDS((T, AW), BF16),
                   SDS((1, AW), F32), SDS((1, AW), F32), SDS((1, AW), F32),
                   SDS((nbr, NH, BLK, 2 * BLK), F32)),
        scratch_shapes=[pltpu.VMEM((S, BLK), F32)] * 7,
    )(qkv, qkv, qkv, attn, lse, dattn, bias)


CH = 256
PADR = 32


def _conv_fwd(ag, conv_w, conv_b):
    def body(ag_ref, w_ref, b_ref, u1_ref, u0p):
        u0p[pl.ds(0, PADR), :] = jnp.zeros((PADR, CW), F32)

        def glu(i, carry):
            t0 = pl.multiple_of(i * CH, CH)
            a = ag_ref[pl.ds(t0, CH), :CW]
            g = ag_ref[pl.ds(t0, CH), CW:]
            u0p[pl.ds(PADR + t0, CH), :] = a * _sigmoid(g)
            return carry

        lax.fori_loop(0, S // CH, glu, 0)

        def conv(i, carry):
            t0 = pl.multiple_of(i * CH, CH)
            win = u0p[pl.ds(t0, CH + PADR), :]
            acc = jnp.zeros((CH, CW), F32) + b_ref[...]
            for k in range(CK):
                off = PADR - (CK - 1) + k
                acc = acc + win[off:off + CH, :] * w_ref[k:k + 1, :]
            u1_ref[pl.ds(t0, CH), :] = acc
            return carry

        lax.fori_loop(0, S // CH, conv, 0)

    return pl.pallas_call(
        body, name="conv_fwd", grid=(BL,),
        in_specs=[pl.BlockSpec((S, 2 * CW), lambda b: (b, 0)),
                  pl.BlockSpec((CK, CW), lambda b: (0, 0)),
                  pl.BlockSpec((1, CW), lambda b: (0, 0))],
        out_specs=pl.BlockSpec((S, CW), lambda b: (b, 0)),
        out_shape=SDS((T, CW), F32),
        scratch_shapes=[pltpu.VMEM((S + PADR, CW), F32)],
    )(ag, conv_w, conv_b)


def _conv_post(u1, cg, cb):
    mu = _rowmean(u1)
    uc = u1 - mu
    rstd = lax.rsqrt(_rowmean(uc * uc) + LN_EPS)
    xh = uc * rstd
    u2 = xh * cg + cb
    sg = _sigmoid(u2)
    return xh, rstd, u2, sg, u2 * sg


def _mix_fwd(attn, u1, ga, gc, cg, cb):
    def body(a_ref, u_ref, ga_ref, gc_ref, cg_ref, cb_ref, o_ref):
        a = a_ref[...]
        ra = lax.rsqrt(_rowmean(a * a) + LN_EPS)
        o_ref[:, :AW] = (a * ra * ga_ref[...]).astype(BF16)
        _, _, _, _, u3 = _conv_post(u_ref[...], cg_ref[...], cb_ref[...])
        rc = lax.rsqrt(_rowmean(u3 * u3) + LN_EPS)
        o_ref[:, AW:] = (u3 * rc * gc_ref[...]).astype(BF16)

    vec = lambda w: pl.BlockSpec((1, w), lambda m: (0, 0))
    return pl.pallas_call(
        body, name="mix_fwd", grid=(T // TM,),
        in_specs=[pl.BlockSpec((TM, AW), lambda m: (m, 0)), pl.BlockSpec((TM, CW), lambda m: (m, 0)),
                  vec(AW), vec(CW), vec(CW), vec(CW)],
        out_specs=pl.BlockSpec((TM, D), lambda m: (m, 0)),
        out_shape=SDS((T, D), BF16),
    )(attn, u1, ga, gc, cg, cb)


def _mix_bwd(dmixed, attn, u1, ga, gc, cg, cb):
    def body(dm_ref, a_ref, u_ref, ga_ref, gc_ref, cg_ref, cb_ref,
             da_ref, du_ref, g_an, g_cn, g_lg, g_lb, g_cb):
        @pl.when(pl.program_id(0) == 0)
        def _():
            for r in (g_an, g_cn, g_lg, g_lb, g_cb):
                r[...] = jnp.zeros_like(r)

        a = a_ref[...]
        dna = dm_ref[:, :AW]
        ra = lax.rsqrt(_rowmean(a * a) + LN_EPS)
        g_an[...] += _colsum(dna * a * ra)
        dat = dna * ga_ref[...]
        da_ref[...] = ra * dat - a * (ra * ra * ra) * _rowmean(dat * a)

        xh, rstd, u2, sg, u3 = _conv_post(u_ref[...], cg_ref[...], cb_ref[...])
        dnc = dm_ref[:, AW:]
        rc = lax.rsqrt(_rowmean(u3 * u3) + LN_EPS)
        g_cn[...] += _colsum(dnc * u3 * rc)
        dut = dnc * gc_ref[...]
        du3 = rc * dut - u3 * (rc * rc * rc) * _rowmean(dut * u3)
        du2 = du3 * sg * (1.0 + u2 * (1.0 - sg))
        g_lg[...] += _colsum(du2 * xh)
        g_lb[...] += _colsum(du2)
        dxh = du2 * cg_ref[...]
        du1 = rstd * (dxh - _rowmean(dxh) - xh * _rowmean(dxh * xh))
        g_cb[...] += _colsum(du1)
        du_ref[...] = du1

    vec = lambda w: pl.BlockSpec((1, w), lambda m: (0, 0))
    return pl.pallas_call(
        body, name="mix_bwd", grid=(T // TM,),
        in_specs=[pl.BlockSpec((TM, D), lambda m: (m, 0)), pl.BlockSpec((TM, AW), lambda m: (m, 0)),
                  pl.BlockSpec((TM, CW), lambda m: (m, 0)), vec(AW), vec(CW), vec(CW), vec(CW)],
        out_specs=(pl.BlockSpec((TM, AW), lambda m: (m, 0)), pl.BlockSpec((TM, CW), lambda m: (m, 0)),
                   vec(AW), vec(CW), vec(CW), vec(CW), vec(CW)),
        out_shape=(SDS((T, AW), F32), SDS((T, CW), F32),
                   SDS((1, AW), F32), SDS((1, CW), F32), SDS((1, CW), F32), SDS((1, CW), F32), SDS((1, CW), F32)),
    )(dmixed, attn, u1, ga, gc, cg, cb)


def _conv_bwd(du1, ag, conv_w):
    def body(du_ref, ag_ref, w_ref, dag_ref, cs_ref, gw_ref, u0p, dup):
        @pl.when(pl.program_id(0) == 0)
        def _():
            cs_ref[...] = jnp.zeros_like(cs_ref)
            gw_ref[...] = jnp.zeros_like(gw_ref)

        u0p[pl.ds(0, PADR), :] = jnp.zeros((PADR, CW), F32)
        dup[pl.ds(S, PADR), :] = jnp.zeros((PADR, CW), F32)

        def fill(i, carry):
            t0 = pl.multiple_of(i * CH, CH)
            a = ag_ref[pl.ds(t0, CH), :CW]
            g = ag_ref[pl.ds(t0, CH), CW:]
            u0p[pl.ds(PADR + t0, CH), :] = a * _sigmoid(g)
            dup[pl.ds(t0, CH), :] = du_ref[pl.ds(t0, CH), :]
            return carry

        lax.fori_loop(0, S // CH, fill, 0)

        def chunk(i, carry):
            t0 = pl.multiple_of(i * CH, CH)
            d = dup[pl.ds(t0, CH), :]
            win_u = u0p[pl.ds(t0, CH + PADR), :]
            win_d = dup[pl.ds(t0, CH + PADR), :]
            du0 = jnp.zeros((CH, CW), F32)
            for k in range(CK):
                off = PADR - (CK - 1) + k
                gw_ref[k:k + 1, :] += _colsum(d * win_u[off:off + CH, :])
                fo = CK - 1 - k
                du0 = du0 + win_d[fo:fo + CH, :] * w_ref[k:k + 1, :]
            a = ag_ref[pl.ds(t0, CH), :CW]
            sg = _sigmoid(ag_ref[pl.ds(t0, CH), CW:])
            da = du0 * sg
            dg = du0 * a * sg * (1.0 - sg)
            dag_ref[pl.ds(t0, CH), :CW] = da.astype(BF16)
            dag_ref[pl.ds(t0, CH), CW:] = dg.astype(BF16)
            cs_ref[:, :CW] += _colsum(da)
            cs_ref[:, CW:] += _colsum(dg)
            return carry

        lax.fori_loop(0, S // CH, chunk, 0)

    return pl.pallas_call(
        body, name="conv_bwd", grid=(BL,),
        in_specs=[pl.BlockSpec((S, CW), lambda b: (b, 0)), pl.BlockSpec((S, 2 * CW), lambda b: (b, 0)),
                  pl.BlockSpec((CK, CW), lambda b: (0, 0))],
        out_specs=(pl.BlockSpec((S, 2 * CW), lambda b: (b, 0)),
                   pl.BlockSpec((1, 2 * CW), lambda b: (0, 0)),
                   pl.BlockSpec((PADR, CW), lambda b: (0, 0))),
        out_shape=(SDS((T, 2 * CW), BF16), SDS((1, 2 * CW), F32), SDS((PADR, CW), F32)),
        scratch_shapes=[pltpu.VMEM((S + PADR, CW), F32), pltpu.VMEM((S + PADR, CW), F32)],
    )(du1, ag, conv_w)


def _layer_norm_fwd(z):
    mu = _rowmean(z)
    zc = z - mu
    rstd = lax.rsqrt(_rowmean(zc * zc) + LN_EPS)
    return zc * rstd, rstd


def _layer_norm_bwd(dy, xh, rstd, g):
    dxh = dy * g
    return rstd * (dxh - _rowmean(dxh) - xh * _rowmean(dxh * xh))


def _out_proj_ln1(mixed, w_out, x2, g1, b1):
    def body(a_ref, w_ref, x_ref, g_ref, b_ref, xh_ref, rstd_ref, x1_ref):
        z = ALPHA * x_ref[...] + _dot(a_ref[...], w_ref[...])
        xh, rstd = _layer_norm_fwd(z)
        xh_ref[...] = xh
        rstd_ref[...] = rstd
        x1_ref[...] = (xh * g_ref[...] + b_ref[...]).astype(BF16)

    vec = pl.BlockSpec((1, D), lambda m: (0, 0))
    row = pl.BlockSpec((TM, D), lambda m: (m, 0))
    return pl.pallas_call(
        body, name="out_proj_ln1", grid=(T // TM,),
        in_specs=[row, pl.BlockSpec((D, D), lambda m: (0, 0)), row, vec, vec],
        out_specs=(row, pl.BlockSpec((TM, 1), lambda m: (m, 0)), row),
        out_shape=(SDS((T, D), F32), SDS((T, 1), F32), SDS((T, D), BF16)),
    )(mixed, w_out, x2, g1, b1)


def _seq_start(m):
    return lax.bitwise_and(m, S // TM - 1) == 0


def _causal3(ext, w_ref, b_ref):
    x0 = ext[pl.ds(8, TM), :]
    x1 = ext[pl.ds(7, TM), :]
    x2 = ext[pl.ds(6, TM), :]
    y = w_ref[2:3, :] * x0 + w_ref[1:2, :] * x1 + w_ref[0:1, :] * x2 + b_ref[...]
    return y, x0, x1, x2


def _ffn_up(x1b, w_up, fcw, fcb):
    def body(x_ref, wg_ref, wv_ref, cwg_ref, cwv_ref, cbg_ref, cbv_ref, up_ref, act_ref, extg, extv):
        @pl.when(_seq_start(pl.program_id(1)))
        def _():
            extg[pl.ds(0, 8), :] = jnp.zeros((8, FT), F32)
            extv[pl.ds(0, 8), :] = jnp.zeros((8, FT), F32)

        x = x_ref[...]
        ug = _dot(x, wg_ref[...]).astype(BF16)
        uv = _dot(x, wv_ref[...]).astype(BF16)
        up_ref[:, :FT] = ug
        up_ref[:, FT:] = uv
        extg[pl.ds(8, TM), :] = ug.astype(F32)
        extv[pl.ds(8, TM), :] = uv.astype(F32)
        gate = _causal3(extg, cwg_ref, cbg_ref)[0]
        val = _causal3(extv, cwv_ref, cbv_ref)[0]
        act_ref[...] = (gate * _sigmoid(gate) * val).astype(BF16)
        extg[pl.ds(0, 8), :] = extg[pl.ds(TM, 8), :]
        extv[pl.ds(0, 8), :] = extv[pl.ds(TM, 8), :]

    wspec = lambda off: pl.BlockSpec((D, FT), lambda n, m: (0, n + off))
    cwspec = lambda off: pl.BlockSpec((FK, FT), lambda n, m: (0, n + off))
    cbspec = lambda off: pl.BlockSpec((1, FT), lambda n, m: (0, n + off))
    return pl.pallas_call(
        body, name="ffn_up", grid=(NFT, T // TM),
        in_specs=[pl.BlockSpec((TM, D), lambda n, m: (m, 0)), wspec(0), wspec(NFT),
                  cwspec(0), cwspec(NFT), cbspec(0), cbspec(NFT)],
        out_specs=(pl.BlockSpec((TM, 2 * FT), lambda n, m: (m, n)), pl.BlockSpec((TM, FT), lambda n, m: (m, n))),
        out_shape=(SDS((T, 2 * DFF), BF16), SDS((T, DFF), BF16)),
        scratch_shapes=[pltpu.VMEM((TM + 8, FT), F32)] * 2,
    )(x1b, w_up, w_up, fcw, fcw, fcb, fcb)


def _ffn_down_loss(act, w_down, xh1, g1, b1, g2, b2, target):
    def body(a_ref, w_ref, xh1_ref, g1_ref, b1_ref, g2_ref, b2_ref, t_ref, dz_ref, loss_ref, gg_ref, gb_ref):
        @pl.when(pl.program_id(0) == 0)
        def _():
            loss_ref[...] = jnp.zeros_like(loss_ref)
            gg_ref[...] = jnp.zeros_like(gg_ref)
            gb_ref[...] = jnp.zeros_like(gb_ref)

        x1 = xh1_ref[...] * g1_ref[...] + b1_ref[...]
        z = ALPHA * x1 + _dot(a_ref[...], w_ref[...])
        xh, rstd = _layer_norm_fwd(z)
        diff = xh * g2_ref[...] + b2_ref[...] - t_ref[...]
        loss_ref[...] += 0.5 * _colsum(_rowmean(diff * diff))
        dout = diff * (1.0 / D)
        gg_ref[...] += _colsum(dout * xh)
        gb_ref[...] += _colsum(dout)
        dz_ref[...] = _layer_norm_bwd(dout, xh, rstd, g2_ref[...])

    vec = pl.BlockSpec((1, D), lambda m: (0, 0))
    row = pl.BlockSpec((TM, D), lambda m: (m, 0))
    return pl.pallas_call(
        body, name="ffn_down_loss", grid=(T // TM,),
        in_specs=[pl.BlockSpec((TM, DFF), lambda m: (m, 0)), pl.BlockSpec((DFF, D), lambda m: (0, 0)),
                  row, vec, vec, vec, vec, row],
        out_specs=(row, pl.BlockSpec((1, 1), lambda m: (0, 0)), vec, vec),
        out_shape=(SDS((T, D), F32), SDS((1, 1), F32), SDS((1, D), F32), SDS((1, D), F32)),
    )(act, w_down, xh1, g1, b1, g2, b2, target)


def _ffn_down_bwd(dz2, w_down, up, fcw, fcb):
    def body(dz_ref, wd_ref, up_ref, cwg_ref, cwv_ref, cbg_ref, cbv_ref,
             dup_ref, csg_ref, csv_ref, gwg_ref, gwv_ref, extg, extv):
        m = pl.program_id(1)

        @pl.when(_seq_start(m))
        def _():
            extg[pl.ds(0, 8), :] = jnp.zeros((8, FT), F32)
            extv[pl.ds(0, 8), :] = jnp.zeros((8, FT), F32)

        @pl.when(m == 0)
        def _():
            for r in (csg_ref, csv_ref, gwg_ref, gwv_ref):
                r[...] = jnp.zeros_like(r)

        dact = _dot_nt(dz_ref[...].astype(BF16), wd_ref[...])
        extg[pl.ds(8, TM), :] = up_ref[:, :FT].astype(F32)
        extv[pl.ds(8, TM), :] = up_ref[:, FT:].astype(F32)
        gate, g0, g1, g2 = _causal3(extg, cwg_ref, cbg_ref)
        val, v0, v1, v2 = _causal3(extv, cwv_ref, cbv_ref)
        sg = _sigmoid(gate)
        dgate = dact * val * sg * (1.0 + gate * (1.0 - sg))
        dval = dact * gate * sg
        dup_ref[:, :FT] = dgate.astype(BF16)
        dup_ref[:, FT:] = dval.astype(BF16)
        csg_ref[...] += _colsum(dgate)
        csv_ref[...] += _colsum(dval)
        for k, (xg, xv) in enumerate(((g2, v2), (g1, v1), (g0, v0))):
            gwg_ref[k:k + 1, :] += _colsum(dgate * xg)
            gwv_ref[k:k + 1, :] += _colsum(dval * xv)
        extg[pl.ds(0, 8), :] = extg[pl.ds(TM, 8), :]
        extv[pl.ds(0, 8), :] = extv[pl.ds(TM, 8), :]

    cwspec = lambda off: pl.BlockSpec((FK, FT), lambda n, m: (0, n + off))
    cbspec = lambda off: pl.BlockSpec((1, FT), lambda n, m: (0, n + off))
    cs = pl.BlockSpec((1, FT), lambda n, m: (0, n))
    gw = pl.BlockSpec((FK, FT), lambda n, m: (0, n))
    return pl.pallas_call(
        body, name="ffn_down_bwd", grid=(NFT, T // TM),
        in_specs=[pl.BlockSpec((TM, D), lambda n, m: (m, 0)), pl.BlockSpec((FT, D), lambda n, m: (n, 0)),
                  pl.BlockSpec((TM, 2 * FT), lambda n, m: (m, n)),
                  cwspec(0), cwspec(NFT), cbspec(0), cbspec(NFT)],
        out_specs=(pl.BlockSpec((TM, 2 * FT), lambda n, m: (m, n)), cs, cs, gw, gw),
        out_shape=(SDS((T, 2 * DFF), BF16), SDS((1, DFF), F32), SDS((1, DFF), F32),
                   SDS((FK, DFF), F32), SDS((FK, DFF), F32)),
        scratch_shapes=[pltpu.VMEM((TM + 8, FT), F32)] * 2,
    )(dz2, w_down, up, fcw, fcw, fcb, fcb)


HALO = 16


def _conv3_transpose(dup, fcw_il):
    tiles = T // TM

    def body(d_ref, h_ref, w_ref, o_ref, ext):
        m = pl.program_id(1)
        ext[pl.ds(0, TM), :] = d_ref[...].astype(F32)
        last = lax.bitwise_and(m + 1, S // TM - 1) == 0
        ext[pl.ds(TM, HALO), :] = jnp.where(last, 0.0, h_ref[...].astype(F32))
        y = (w_ref[2:3, :] * ext[pl.ds(0, TM), :] + w_ref[1:2, :] * ext[pl.ds(1, TM), :]
             + w_ref[0:1, :] * ext[pl.ds(2, TM), :])
        o_ref[...] = y.astype(BF16)

    return pl.pallas_call(
        body, name="conv3_transpose", grid=(NFT, tiles),
        in_specs=[pl.BlockSpec((TM, 2 * FT), lambda n, m: (m, n)),
                  pl.BlockSpec((HALO, 2 * FT), lambda n, m: (jnp.minimum((m + 1) * (TM // HALO), T // HALO - 1), n)),
                  pl.BlockSpec((FK, 2 * FT), lambda n, m: (0, n))],
        out_specs=pl.BlockSpec((TM, 2 * FT), lambda n, m: (m, n)),
        out_shape=SDS((T, 2 * DFF), BF16),
        scratch_shapes=[pltpu.VMEM((TM + HALO, 2 * FT), F32)],
    )(dup, dup, fcw_il)


def _ffn_up_bwd_ln1(dpre, w_up, dz2, xh1, rstd1, g1):
    def body(a_ref, wg_ref, wv_ref, dz2_ref, xh_ref, rstd_ref, g_ref, dz1_ref, gg_ref, gb_ref, acc):
        m, kk = pl.program_id(0), pl.program_id(1)

        @pl.when(kk == 0)
        def _():
            acc[...] = jnp.zeros_like(acc)

        acc[...] += _dot_nt(a_ref[:, :FT], wg_ref[...]) + _dot_nt(a_ref[:, FT:], wv_ref[...])

        @pl.when(kk == NFT - 1)
        def _():
            @pl.when(m == 0)
            def _():
                gg_ref[...] = jnp.zeros_like(gg_ref)
                gb_ref[...] = jnp.zeros_like(gb_ref)

            dx1 = acc[...] + ALPHA * dz2_ref[...]
            xh = xh_ref[...]
            gg_ref[...] += _colsum(dx1 * xh)
            gb_ref[...] += _colsum(dx1)
            dz1_ref[...] = _layer_norm_bwd(dx1, xh, rstd_ref[...], g_ref[...])

    vec = pl.BlockSpec((1, D), lambda m, k: (0, 0))
    row = pl.BlockSpec((TM, D), lambda m, k: (m, 0))
    wspec = lambda off: pl.BlockSpec((D, FT), lambda m, k: (0, k + off))
    return pl.pallas_call(
        body, name="ffn_up_bwd_ln1", grid=(T // TM, NFT),
        in_specs=[pl.BlockSpec((TM, 2 * FT), lambda m, k: (m, k)), wspec(0), wspec(NFT),
                  row, row, pl.BlockSpec((TM, 1), lambda m, k: (m, 0)), vec],
        out_specs=(row, vec, vec),
        out_shape=(SDS((T, D), F32), SDS((1, D), F32), SDS((1, D), F32)),
        scratch_shapes=[pltpu.VMEM((TM, D), F32)],
    )(dpre, w_up, w_up, dz2, xh1, rstd1, g1)


def _grad_w_up(dpre, x1b):
    tk = 1024

    def body(a_ref, b_ref, og_ref, ov_ref):
        @pl.when(pl.program_id(1) == 0)
        def _():
            og_ref[...] = jnp.zeros_like(og_ref)
            ov_ref[...] = jnp.zeros_like(ov_ref)

        r = _dot_tn(a_ref[...], b_ref[...])
        og_ref[...] += r[:FT]
        ov_ref[...] += r[FT:]

    out = pl.BlockSpec((FT, D), lambda n, k: (n, 0))
    return pl.pallas_call(
        body, name="grad_w_up", grid=(NFT, T // tk),
        in_specs=[pl.BlockSpec((tk, 2 * FT), lambda n, k: (k, n)), pl.BlockSpec((tk, D), lambda n, k: (k, 0))],
        out_specs=(out, out),
        out_shape=(SDS((DFF, D), F32), SDS((DFF, D), F32)),
    )(dpre, x1b)


def _row_tile(rows, cols):
    if rows * cols * 4 <= (1 << 20) or rows % 8:
        return rows
    for t in (256, 176, 128, 88, 64, 32, 16, 8):
        if rows % t == 0 and t * cols * 4 <= (1 << 20):
            return t
    return 8


def _sum8(r, name):
    _, rows, cols = r.shape
    tr = _row_tile(rows, cols)

    def body(r_ref, o_ref):
        acc = r_ref[0].astype(F32)
        for p in range(1, NDEV):
            acc = acc + r_ref[p].astype(F32)
        o_ref[...] = acc

    return pl.pallas_call(
        body, name=name, grid=(rows // tr,),
        in_specs=[pl.BlockSpec((NDEV, tr, cols), lambda i: (0, i, 0))],
        out_specs=pl.BlockSpec((tr, cols), lambda i: (i, 0)),
        out_shape=SDS((rows, cols), F32),
    )(r)


def _adamw(w, g, m, v, name):
    rows, cols = w.shape
    tr = _row_tile(rows, cols)

    def body(w_ref, g_ref, m_ref, v_ref, d_ref, nm_ref, nv_ref):
        g_ = g_ref[...]
        m_ = B1 * m_ref[...] + (1.0 - B1) * g_
        v_ = B2 * v_ref[...] + (1.0 - B2) * jnp.square(g_)
        m_hat = m_ / (1.0 - B1 ** STEP)
        v_hat = v_ / (1.0 - B2 ** STEP)
        d_ref[...] = -LR * (m_hat / (jnp.sqrt(v_hat) + AEPS) + WD * w_ref[...])
        nm_ref[...] = m_
        nv_ref[...] = v_

    spec = pl.BlockSpec((tr, cols), lambda i: (i, 0))
    shp = SDS((rows, cols), F32)
    return pl.pallas_call(
        body, name=name, grid=(rows // tr,), in_specs=[spec] * 4, out_specs=(spec,) * 3,
        out_shape=(shp, shp, shp),
    )(w, g, m, v)


def _interleave(a):
    r = a.shape[0]
    return a.reshape(r, 2, NFT, FT).transpose(0, 2, 1, 3).reshape(r, 2 * DFF)


def _local_step(x2, target, rel_table, w_in, b_in, conv_w, conv_b, conv_ln_g, conv_ln_b, attn_norm_g,
                conv_norm_g, w_out, ln1_g, ln1_b, w_up, ffn_conv_w, ffn_conv_b, w_down, ln2_g, ln2_b):
    buckets = jnp.asarray(_bucket_maps())
    bias = _bias_table(rel_table, buckets)

    qkv = _mm_nn_bias(x2, w_in, b_in, 0, 3, AW, BF16, "proj_qkv")
    ag = _mm_nn_bias(x2, w_in, b_in, 3 * AW // CW, 2, CW, F32, "proj_ag")
    attn, lse = _attn_fwd(qkv, bias)
    u1 = _conv_fwd(ag, conv_w, conv_b)
    mixed = _mix_fwd(attn, u1, attn_norm_g, conv_norm_g, conv_ln_g, conv_ln_b)
    xh1, rstd1, x1b = _out_proj_ln1(mixed, w_out, x2, ln1_g, ln1_b)
    up, act = _ffn_up(x1b, w_up, ffn_conv_w, ffn_conv_b)
    dz2, loss, g_ln2_g, g_ln2_b = _ffn_down_loss(act, w_down, xh1, ln1_g, ln1_b, ln2_g, ln2_b, target)

    dup, cs_g, cs_v, gfw_g, gfw_v = _ffn_down_bwd(dz2, w_down, up, ffn_conv_w, ffn_conv_b)
    g_w_down = _mm_tn(act, dz2, DFF // 2, 512, "grad_w_down")
    dpre = _conv3_transpose(dup, _interleave(ffn_conv_w))
    dz1, g_ln1_g, g_ln1_b = _ffn_up_bwd_ln1(dpre, w_up, dz2, xh1, rstd1, ln1_g)
    g_w_up_gate_t, g_w_up_val_t = _grad_w_up(dpre, x1b)
    dmixed = _mm_nt(dz1, w_out, None, 0.0, "dmixed")
    g_w_out = _mm_tn(mixed, dz1, D, 512, "grad_w_out")
    dattn, du1, g_an, g_cn, g_clg, g_clb, g_cb = _mix_bwd(
        dmixed, attn, u1, attn_norm_g, conv_norm_g, conv_ln_g, conv_ln_b)
    dag, cs_ag, g_conv_w = _conv_bwd(du1, ag, conv_w)
    dq, dk, dv, cs_q, cs_k, cs_v2, dbias = _attn_bwd(qkv, attn, lse, dattn, bias)
    g_rel = _rel_table_grad(dbias, buckets)
    dh = jnp.concatenate([dq, dk, dv, dag], axis=1)
    grad_x = _mm_nt(dh, w_in, dz1, ALPHA, "grad_x")
    g_w_in_t = _mm_tn(dh, x2, INW // 2, 512, "grad_w_in")

    grads = dict(
        rel_table=g_rel,
        b_in=jnp.concatenate([cs_q, cs_k, cs_v2, cs_ag], axis=1),
        conv_b=g_cb, conv_ln_g=g_clg, conv_ln_b=g_clb, attn_norm_g=g_an, conv_norm_g=g_cn,
        ln1_g=g_ln1_g, ln1_b=g_ln1_b,
        ffn_conv_b=jnp.concatenate([cs_g, cs_v], axis=1),
        ln2_g=g_ln2_g, ln2_b=g_ln2_b,
        conv_w=g_conv_w[:CK],
        ffn_conv_w=jnp.concatenate([gfw_g, gfw_v], axis=1),
        w_in_t=g_w_in_t, w_out=g_w_out, w_up_gate_t=g_w_up_gate_t, w_up_val_t=g_w_up_val_t, w_down=g_w_down,
    )
    return loss, grad_x, grads


SMALL = (("rel_table", (NBUCKET, NH)), ("b_in", (1, INW)), ("conv_b", (1, CW)), ("conv_ln_g", (1, CW)),
         ("conv_ln_b", (1, CW)), ("attn_norm_g", (1, AW)), ("conv_norm_g", (1, CW)), ("ln1_g", (1, D)),
         ("ln1_b", (1, D)), ("ffn_conv_b", (1, 2 * DFF)), ("ln2_g", (1, D)), ("ln2_b", (1, D)))
SHARDED_SMALL = (("conv_w", (CK, CW)), ("ffn_conv_w", (FK, 2 * DFF)))


def _pack(parts):
    flat = jnp.concatenate([p.reshape(-1) for p in parts])
    tile = 8 * PACK_LANES
    pad = (-flat.shape[0]) % tile
    return jnp.pad(flat, (0, pad)).reshape(-1, PACK_LANES)


def _unpack(packed, specs):
    flat = packed.reshape(-1)
    out, off = {}, 0
    for name, shp in specs:
        size = int(np.prod(shp))
        out[name] = flat[off:off + size].reshape(shp)
        off += size
    return out


def kernel(x, rel_table, w_in, b_in, conv_w, conv_b, conv_ln_g, conv_ln_b, attn_norm_g, conv_norm_g, w_out, ln1_g, ln1_b, w_up, ffn_conv_w, ffn_conv_b, w_down, ln2_g, ln2_b, loss_target, m_rel_table, m_w_in, m_b_in, m_conv_w, m_conv_b, m_conv_ln_g, m_conv_ln_b, m_attn_norm_g, m_conv_norm_g, m_w_out, m_ln1_g, m_ln1_b, m_w_up, m_ffn_conv_w, m_ffn_conv_b, m_w_down, m_ln2_g, m_ln2_b, v_rel_table, v_w_in, v_b_in, v_conv_w, v_conv_b, v_conv_ln_g, v_conv_ln_b, v_attn_norm_g, v_conv_norm_g, v_w_out, v_ln1_g, v_ln1_b, v_w_up, v_ffn_conv_w, v_ffn_conv_b, v_w_down, v_ln2_g, v_ln2_b):
    given = dict(locals())
    me = 4 * lax.axis_index("x") + 2 * lax.axis_index("y") + lax.axis_index("c")

    gathered = _exchange(
        [(w_in[0].astype(BF16), "gather"), (w_out[0].astype(BF16), "gather"), (w_up[0].astype(BF16), "gather"),
         (w_down[0].astype(BF16), "gather"), (conv_w[0], "gather"), (ffn_conv_w[0], "gather")],
        "gather_weights")
    cols = lambda a: a.transpose(1, 0, 2).reshape(a.shape[1], NDEV * a.shape[2])
    rows = lambda a: a.reshape(NDEV * a.shape[1], a.shape[2])
    w_in_f, w_out_f, w_up_f, w_down_f = cols(gathered[0]), rows(gathered[1]), cols(gathered[2]), rows(gathered[3])
    conv_w_f, ffn_conv_w_f = cols(gathered[4]), cols(gathered[5])

    loss, grad_x, g = _local_step(
        x.reshape(T, D), loss_target.reshape(T, D), rel_table, w_in_f, b_in, conv_w_f, conv_b, conv_ln_g,
        conv_ln_b, attn_norm_g, conv_norm_g, w_out_f, ln1_g, ln1_b, w_up_f, ffn_conv_w_f, ffn_conv_b,
        w_down_f, ln2_g, ln2_b)

    small_specs = SMALL + SHARDED_SMALL
    packed = _pack([g[n] for n, _ in small_specs])
    g_w_up_t = jnp.concatenate([g["w_up_gate_t"], g["w_up_val_t"]], axis=0)
    stack = lambda a: a.reshape(NDEV, a.shape[0] // NDEV, a.shape[1])
    got = _exchange(
        [(stack(g["w_in_t"]), "scatter"), (stack(g["w_out"]), "scatter"), (stack(g_w_up_t), "scatter"),
         (stack(g["w_down"]), "scatter"), (packed, "gather")],
        "exchange_grads")

    grad = {}
    grad["w_in"] = _sum8(got[0], "sum_w_in").T[None]
    grad["w_out"] = _sum8(got[1], "sum_w_out")[None]
    grad["w_up"] = _sum8(got[2], "sum_w_up").T[None]
    grad["w_down"] = _sum8(got[3], "sum_w_down")[None]
    small = _unpack(_sum8(got[4], "sum_small"), small_specs)
    for n, _ in SMALL:
        grad[n] = small[n]
    grad["conv_w"] = lax.dynamic_slice_in_dim(small["conv_w"], me * (CW // NDEV), CW // NDEV, axis=1)[None]
    grad["ffn_conv_w"] = lax.dynamic_slice_in_dim(small["ffn_conv_w"], me * (2 * DFF // NDEV), 2 * DFF // NDEV, axis=1)[None]

    delta, new_m, new_v = {}, {}, {}
    for n in ("w_in", "w_out", "w_up", "w_down", "conv_w", "ffn_conv_w"):
        shp = given[n].shape
        two = lambda a: a.reshape(shp[-2], shp[-1])
        d_, m_, v_ = _adamw(two(given[n]), two(grad[n]), two(given["m_" + n]), two(given["v_" + n]), "adamw_" + n)
        delta[n], new_m[n], new_v[n] = d_.reshape(shp), m_.reshape(shp), v_.reshape(shp)
    wp = _pack([given[n] for n, _ in SMALL])
    gp = _pack([grad[n] for n, _ in SMALL])
    mp = _pack([given["m_" + n] for n, _ in SMALL])
    vp = _pack([given["v_" + n] for n, _ in SMALL])
    dp, nmp, nvp = _adamw(wp, gp, mp, vp, "adamw_small")
    for dst, src in ((delta, dp), (new_m, nmp), (new_v, nvp)):
        dst.update(_unpack(src, SMALL))

    order = ("rel_table", "w_in", "b_in", "conv_w", "conv_b", "conv_ln_g", "conv_ln_b", "attn_norm_g",
             "conv_norm_g", "w_out", "ln1_g", "ln1_b", "w_up", "ffn_conv_w", "ffn_conv_b", "w_down", "ln2_g", "ln2_b")
    total_loss = lax.psum(loss[0, 0], ("x", "y", "c"))
    return (total_loss, grad_x.reshape(BL, S, D), *[grad[n] for n in order], *[delta[n] for n in order],
            *[new_m[n] for n in order], *[new_v[n] for n in order])
```

```python
import functools
import math

import numpy as np
import jax
import jax.numpy as jnp
from jax import lax
from jax.experimental import pallas as pl
from jax.experimental.pallas import tpu as pltpu

F32 = jnp.float32
BF16 = jnp.bfloat16
SDS = jax.ShapeDtypeStruct

NDEV = 8
D = 1024
S = 2048
BL = 2
T = BL * S
NH = 12
HD = 64
AW = NH * HD
CW = D - AW
INW = 3 * AW + 2 * CW
CK = 31
DFF = 2816
FK = 3
BLK = 128
NBUCKET = 32
BRANCHES = ((128, 1), (512, 4), (2048, 16))
ALPHA = 2.0 ** 0.25
LN_EPS = 1e-5
NEG_INF = -1e30
LR, B1, B2, AEPS, WD, STEP = 0.001, 0.9, 0.999, 1e-08, 0.01, 10

TM = 512
FT = 256
NFT = DFF // FT
PACK_LANES = 128

assert all(w // d == BLK for w, d in BRANCHES)


def _dot(a, b):
    return jnp.dot(a, b, preferred_element_type=F32)


def _dot_nt(a, b):
    return lax.dot_general(a, b, (((1,), (1,)), ((), ())), preferred_element_type=F32)


def _dot_tn(a, b):
    return lax.dot_general(a, b, (((0,), (0,)), ((), ())), preferred_element_type=F32)


def _rowmean(v):
    return jnp.mean(v, axis=-1, keepdims=True)


def _colsum(v):
    return jnp.sum(v, axis=0, keepdims=True)


def _sigmoid(v):
    return jax.nn.sigmoid(v)


def _exchange(items, name):
    n = len(items)
    arrs = [a for a, _ in items]
    kinds = [k for _, k in items]
    out_shapes = []
    for a, k in items:
        shp = (NDEV,) + tuple(a.shape) if k == "gather" else tuple(a.shape)
        out_shapes.append(SDS(shp, a.dtype))

    def body(*refs):
        ins = refs[:n]
        outs = refs[n:2 * n]
        send_sems, recv_sems, local_sems = refs[2 * n:]
        x, y, c = lax.axis_index("x"), lax.axis_index("y"), lax.axis_index("c")
        me = 4 * x + 2 * y + c

        def peer(k):
            px = 1 - x if k & 4 else x
            py = 1 - y if k & 2 else y
            pc = 1 - c if k & 1 else c
            return (px, py, pc), 4 * px + 2 * py + pc

        local = []
        for i in range(n):
            src = ins[i] if kinds[i] == "gather" else ins[i].at[me]
            cp = pltpu.make_async_copy(src, outs[i].at[me], local_sems.at[i])
            cp.start()
            local.append(cp)
        sends = []
        for k in range(1, NDEV):
            dev, pid = peer(k)
            for i in range(n):
                src = ins[i] if kinds[i] == "gather" else ins[i].at[pid]
                cp = pltpu.make_async_remote_copy(
                    src_ref=src, dst_ref=outs[i].at[me],
                    send_sem=send_sems.at[i, k - 1], recv_sem=recv_sems.at[i, k - 1],
                    device_id=dev, device_id_type=pl.DeviceIdType.MESH)
                cp.start()
                sends.append(cp)
        for k in range(1, NDEV):
            dev, pid = peer(k)
            for i in range(n):
                src = ins[i] if kinds[i] == "gather" else ins[i].at[pid]
                pltpu.make_async_remote_copy(
                    src_ref=src, dst_ref=outs[i].at[pid],
                    send_sem=send_sems.at[i, k - 1], recv_sem=recv_sems.at[i, k - 1],
                    device_id=dev, device_id_type=pl.DeviceIdType.MESH).wait_recv()
        for cp in sends:
            cp.wait_send()
        for cp in local:
            cp.wait()

    any_spec = pl.BlockSpec(memory_space=pl.ANY)
    return pl.pallas_call(
        body, name=name,
        out_shape=tuple(out_shapes),
        in_specs=[any_spec] * n,
        out_specs=tuple([any_spec] * n),
        scratch_shapes=[pltpu.SemaphoreType.DMA((n, NDEV - 1)),
                        pltpu.SemaphoreType.DMA((n, NDEV - 1)),
                        pltpu.SemaphoreType.DMA((n,))],
        compiler_params=pltpu.CompilerParams(has_side_effects=True),
    )(*arrs)


def _mm_nn_bias(a, b, bias, col_blk0, nblk, tn, out_dtype, name):
    m_, k_ = a.shape

    def body(a_ref, b_ref, bias_ref, o_ref):
        acc = _dot(a_ref[...].astype(BF16), b_ref[...])
        o_ref[...] = (acc + bias_ref[...]).astype(o_ref.dtype)

    return pl.pallas_call(
        body, name=name, grid=(nblk, m_ // TM),
        in_specs=[pl.BlockSpec((TM, k_), lambda n, m: (m, 0)),
                  pl.BlockSpec((k_, tn), lambda n, m: (0, col_blk0 + n)),
                  pl.BlockSpec((1, tn), lambda n, m: (0, col_blk0 + n))],
        out_specs=pl.BlockSpec((TM, tn), lambda n, m: (m, n)),
        out_shape=SDS((m_, nblk * tn), out_dtype),
    )(a, b, bias)


def _mm_nt(a, b, res, res_scale, name):
    m_, k_ = a.shape
    n_ = b.shape[0]
    has_res = res is not None

    def body(*refs):
        if has_res:
            a_ref, b_ref, r_ref, o_ref = refs
        else:
            a_ref, b_ref, o_ref = refs
        acc = _dot_nt(a_ref[...].astype(BF16), b_ref[...].astype(BF16))
        if has_res:
            acc = acc + res_scale * r_ref[...]
        o_ref[...] = acc

    in_specs = [pl.BlockSpec((TM, k_), lambda m: (m, 0)), pl.BlockSpec((n_, k_), lambda m: (0, 0))]
    args = [a, b]
    if has_res:
        in_specs.append(pl.BlockSpec((TM, n_), lambda m: (m, 0)))
        args.append(res)
    return pl.pallas_call(
        body, name=name, grid=(m_ // TM,), in_specs=in_specs,
        out_specs=pl.BlockSpec((TM, n_), lambda m: (m, 0)),
        out_shape=SDS((m_, n_), F32),
    )(*args)


def _mm_tn(a, b, tn, tk, name):
    t_, na = a.shape
    nb = b.shape[1]

    def body(a_ref, b_ref, o_ref):
        @pl.when(pl.program_id(1) == 0)
        def _():
            o_ref[...] = jnp.zeros_like(o_ref)
        o_ref[...] += _dot_tn(a_ref[...].astype(BF16), b_ref[...].astype(BF16))

    return pl.pallas_call(
        body, name=name, grid=(na // tn, t_ // tk),
        in_specs=[pl.BlockSpec((tk, tn), lambda n, k: (k, n)),
                  pl.BlockSpec((tk, nb), lambda n, k: (k, 0))],
        out_specs=pl.BlockSpec((tn, nb), lambda n, k: (n, 0)),
        out_shape=SDS((na, nb), F32),
    )(a, b)


def _bucket_maps():
    qi = np.arange(BLK)[:, None]
    kj = np.arange(2 * BLK)[None, :]
    steps = np.maximum(qi + BLK - kj, 0)
    exact = NBUCKET // 2
    maps = []
    for _, dil in BRANCHES:
        dist = steps * dil
        d_f = np.maximum(dist, 1).astype(np.float32)
        large = exact + (np.log(d_f / np.float32(exact)) / np.float32(math.log(S / exact))
                         * np.float32(NBUCKET - exact)).astype(np.int32)
        large = np.minimum(large, NBUCKET - 1)
        maps.append(np.where(dist < exact, dist, large).astype(np.int32))
    return np.stack(maps)


def _bias_table(rel_table, buckets):
    def body(t_ref, b_ref, o_ref):
        bk = b_ref[0]
        for h in range(NH):
            acc = jnp.zeros((BLK, 2 * BLK), F32)
            for k in range(NBUCKET):
                acc = jnp.where(bk == k, t_ref[k, h], acc)
            o_ref[0, h] = acc

    return pl.pallas_call(
        body, name="bias_table", grid=(len(BRANCHES),),
        in_specs=[pl.BlockSpec(memory_space=pltpu.SMEM),
                  pl.BlockSpec((1, BLK, 2 * BLK), lambda i: (i, 0, 0))],
        out_specs=pl.BlockSpec((1, NH, BLK, 2 * BLK), lambda i: (i, 0, 0, 0)),
        out_shape=SDS((len(BRANCHES), NH, BLK, 2 * BLK), F32),
    )(rel_table, buckets)


def _rel_table_grad(dbias, buckets):
    def body(d_ref, b_ref, o_ref):
        h = pl.program_id(0)
        for k in range(NBUCKET):
            tot = jnp.zeros((1, 1), F32)
            for br in range(len(BRANCHES)):
                sel = jnp.where(b_ref[br] == k, d_ref[br, 0], 0.0)
                tot = tot + jnp.sum(jnp.sum(sel, axis=1, keepdims=True), axis=0, keepdims=True)
            o_ref[0, :, pl.ds(k, 1)] = tot

    out = pl.pallas_call(
        body, name="rel_table_grad", grid=(NH,),
        in_specs=[pl.BlockSpec((len(BRANCHES), 1, BLK, 2 * BLK), lambda h: (0, h, 0, 0)),
                  pl.BlockSpec((len(BRANCHES), BLK, 2 * BLK), lambda h: (0, 0, 0))],
        out_specs=pl.BlockSpec((1, 1, NBUCKET), lambda h: (h, 0, 0)),
        out_shape=SDS((NH, 1, NBUCKET), F32),
    )(dbias, buckets)
    return out.reshape(NH, NBUCKET).T


def _block_rows(br, i):
    _, dil = BRANCHES[br]
    nb = S // dil // BLK
    if nb == 16:
        r, nidx = 0, i
    elif nb == 4:
        r, nidx = lax.shift_right_logical(i, 2), lax.bitwise_and(i, 3)
    else:
        r, nidx = i, 0
    start = r + dil * BLK * nidx
    if nb == 1:
        return start, None, None
    prev = r + dil * BLK * jnp.maximum(nidx - 1, 0)
    return start, prev, nidx > 0


def _rows(start, dil):
    if dil == 1:
        return pl.ds(pl.multiple_of(start, BLK), BLK)
    return pl.ds(start, BLK, stride=dil)


def _attn_masks():
    lane = lax.broadcasted_iota(jnp.int32, (BLK, BLK), 1)
    qi = lax.broadcasted_iota(jnp.int32, (BLK, BLK), 0)
    head0 = lane < HD
    valid_cur = lane <= qi
    valid_prev = lane >= qi
    return head0, valid_cur, valid_prev


def _attn_fwd_v1(qkv, bias):
    scale = 1.0 / math.sqrt(HD)
    nbr = len(BRANCHES)

    def body(q_ref, k_ref, v_ref, bias_ref, o_ref, lse_ref, qf, kf, vf, ob, mb, lb):
        qf[...] = q_ref[...].astype(F32)
        kf[...] = k_ref[...].astype(F32)
        vf[...] = v_ref[...].astype(F32)
        head0, valid_cur, valid_prev = _attn_masks()

        for br in range(nbr):
            dil = BRANCHES[br][1]

            def blk(i, carry, br=br, dil=dil):
                start, prev, has_prev = _block_rows(br, i)
                rows = _rows(start, dil)
                q = qf[rows, :]
                kc = kf[rows, :].astype(BF16)
                vc = vf[rows, :].astype(BF16)
                if prev is not None:
                    prows = _rows(prev, dil)
                    kp = kf[prows, :].astype(BF16)
                    vp = vf[prows, :].astype(BF16)
                    ok_prev = jnp.logical_and(valid_prev, has_prev)
                o_acc = jnp.zeros((BLK, BLK), F32)
                m_acc = jnp.zeros((BLK, BLK), F32)
                l_acc = jnp.zeros((BLK, BLK), F32)
                for j in range(2):
                    mj = head0 if j == 0 else jnp.logical_not(head0)
                    qj = jnp.where(mj, q, 0.0).astype(BF16)
                    sc = _dot_nt(qj, kc) * scale + bias_ref[br, j, :, BLK:]
                    sc = jnp.where(valid_cur, sc, NEG_INF)
                    mx = jnp.max(sc, axis=-1, keepdims=True)
                    if prev is not None:
                        sp = _dot_nt(qj, kp) * scale + bias_ref[br, j, :, :BLK]
                        sp = jnp.where(ok_prev, sp, NEG_INF)
                        mx = jnp.maximum(mx, jnp.max(sp, axis=-1, keepdims=True))
                    pc = jnp.exp(sc - mx)
                    ls = jnp.sum(pc, axis=-1, keepdims=True)
                    o = _dot(pc.astype(BF16), vc)
                    if prev is not None:
                        pp = jnp.exp(sp - mx)
                        ls = ls + jnp.sum(pp, axis=-1, keepdims=True)
                        o = o + _dot(pp.astype(BF16), vp)
                    o_acc = jnp.where(mj, o, o_acc)
                    m_acc = jnp.where(mj, mx, m_acc)
                    l_acc = jnp.where(mj, ls, l_acc)
                ob[br, rows, :] = o_acc
                mb[br, rows, :] = m_acc
                lb[br, rows, :] = l_acc
                return carry

            lax.fori_loop(0, 16, blk, 0)

        def merge(i, carry):
            rows = pl.ds(pl.multiple_of(i * 256, 256), 256)
            m_all = jnp.maximum(jnp.maximum(mb[0, rows, :], mb[1, rows, :]), mb[2, rows, :])
            num = jnp.zeros((256, BLK), F32)
            den = jnp.zeros((256, BLK), F32)
            for br in range(nbr):
                c = jnp.exp(mb[br, rows, :] - m_all)
                num = num + ob[br, rows, :] * c
                den = den + lb[br, rows, :] * c
            o_ref[rows, :] = num / den
            lse_ref[rows, :] = m_all + jnp.log(den)
            return carry

        lax.fori_loop(0, S // 256, merge, 0)

    npair = NH // 2
    blk_spec = lambda off: pl.BlockSpec((S, BLK), lambda b, hp: (b, off + hp))
    return pl.pallas_call(
        body, name="attn_fwd", grid=(BL, npair),
        in_specs=[blk_spec(0), blk_spec(npair), blk_spec(2 * npair),
                  pl.BlockSpec((nbr, 2, BLK, 2 * BLK), lambda b, hp: (0, hp, 0, 0))],
        out_specs=(blk_spec(0), blk_spec(0)),
        out_shape=(SDS((T, AW), F32), SDS((T, AW), F32)),
        scratch_shapes=[pltpu.VMEM((S, BLK), F32)] * 3 + [pltpu.VMEM((nbr, S, BLK), F32)] * 3,
    )(qkv, qkv, qkv, bias)


def _attn_bwd_v1(qkv, attn, lse, dattn, bias):
    scale = 1.0 / math.sqrt(HD)
    nbr = len(BRANCHES)

    def body(q_ref, k_ref, v_ref, o_ref, lse_ref, do_ref, bias_ref,
             dq_ref, dk_ref, dv_ref, sq_ref, sk_ref, sv_ref, db_ref,
             qf, kf, vf, dl, dqa, dka, dva):
        b = pl.program_id(1)
        qf[...] = q_ref[...].astype(F32)
        kf[...] = k_ref[...].astype(F32)
        vf[...] = v_ref[...].astype(F32)
        dqa[...] = jnp.zeros_like(dqa)
        dka[...] = jnp.zeros_like(dka)
        dva[...] = jnp.zeros_like(dva)
        head0, valid_cur, valid_prev = _attn_masks()

        @pl.when(b == 0)
        def _():
            db_ref[...] = jnp.zeros_like(db_ref)
            sq_ref[...] = jnp.zeros_like(sq_ref)
            sk_ref[...] = jnp.zeros_like(sk_ref)
            sv_ref[...] = jnp.zeros_like(sv_ref)

        def delta(i, carry):
            rows = pl.ds(pl.multiple_of(i * 256, 256), 256)
            prod = do_ref[rows, :] * o_ref[rows, :]
            h0 = lax.broadcasted_iota(jnp.int32, (256, BLK), 1) < HD
            d0 = jnp.sum(jnp.where(h0, prod, 0.0), axis=-1, keepdims=True)
            d1 = jnp.sum(jnp.where(h0, 0.0, prod), axis=-1, keepdims=True)
            dl[rows, :] = jnp.where(h0, d0, d1)
            return carry

        lax.fori_loop(0, S // 256, delta, 0)

        for br in range(nbr):
            dil = BRANCHES[br][1]

            def blk(i, carry, br=br, dil=dil):
                start, prev, has_prev = _block_rows(br, i)
                rows = _rows(start, dil)
                q = qf[rows, :]
                kc = kf[rows, :].astype(BF16)
                vc = vf[rows, :].astype(BF16)
                do = do_ref[rows, :]
                lse_b = lse_ref[rows, :]
                dl_b = dl[rows, :]
                if prev is not None:
                    prows = _rows(prev, dil)
                    kp = kf[prows, :].astype(BF16)
                    vp = vf[prows, :].astype(BF16)
                    ok_prev = jnp.logical_and(valid_prev, has_prev)
                    dk_p = jnp.zeros((BLK, BLK), F32)
                    dv_p = jnp.zeros((BLK, BLK), F32)
                dq = jnp.zeros((BLK, BLK), F32)
                dk_c = jnp.zeros((BLK, BLK), F32)
                dv_c = jnp.zeros((BLK, BLK), F32)
                for j in range(2):
                    mj = head0 if j == 0 else jnp.logical_not(head0)
                    qj = jnp.where(mj, q, 0.0).astype(BF16)
                    doj = jnp.where(mj, do, 0.0).astype(BF16)
                    lse_j = lse_b[:, j * HD:j * HD + 1]
                    dl_j = dl_b[:, j * HD:j * HD + 1]
                    sc = _dot_nt(qj, kc) * scale + bias_ref[br, j, :, BLK:]
                    pc = jnp.where(valid_cur, jnp.exp(sc - lse_j), 0.0)
                    ds_c = pc * (_dot_nt(doj, vc) - dl_j)
                    db_ref[br, j, :, BLK:] += ds_c
                    dsb = (ds_c * scale).astype(BF16)
                    dqj = _dot(dsb, kc)
                    dk_c = dk_c + _dot_tn(dsb, qj)
                    dv_c = dv_c + _dot_tn(pc.astype(BF16), doj)
                    if prev is not None:
                        sp = _dot_nt(qj, kp) * scale + bias_ref[br, j, :, :BLK]
                        pp = jnp.where(ok_prev, jnp.exp(sp - lse_j), 0.0)
                        ds_p = pp * (_dot_nt(doj, vp) - dl_j)
                        db_ref[br, j, :, :BLK] += ds_p
                        dsbp = (ds_p * scale).astype(BF16)
                        dqj = dqj + _dot(dsbp, kp)
                        dk_p = dk_p + _dot_tn(dsbp, qj)
                        dv_p = dv_p + _dot_tn(pp.astype(BF16), doj)
                    dq = jnp.where(mj, dqj, dq)
                dqa[rows, :] = dqa[rows, :] + dq
                dka[rows, :] = dka[rows, :] + dk_c
                dva[rows, :] = dva[rows, :] + dv_c
                if prev is not None:
                    dka[prows, :] = dka[prows, :] + dk_p
                    dva[prows, :] = dva[prows, :] + dv_p
                return carry

            lax.fori_loop(0, 16, blk, 0)

        def flush(i, carry):
            rows = pl.ds(pl.multiple_of(i * 256, 256), 256)
            for acc, out, cs in ((dqa, dq_ref, sq_ref), (dka, dk_ref, sk_ref), (dva, dv_ref, sv_ref)):
                val = acc[rows, :]
                out[rows, :] = val.astype(BF16)
                cs[...] += _colsum(val)
            return carry

        lax.fori_loop(0, S // 256, flush, 0)

    npair = NH // 2
    blk_spec = lambda off: pl.BlockSpec((S, BLK), lambda hp, b: (b, off + hp))
    sum_spec = pl.BlockSpec((1, BLK), lambda hp, b: (0, hp))
    return pl.pallas_call(
        body, name="attn_bwd", grid=(npair, BL),
        in_specs=[blk_spec(0), blk_spec(npair), blk_spec(2 * npair), blk_spec(0), blk_spec(0), blk_spec(0),
                  pl.BlockSpec((nbr, 2, BLK, 2 * BLK), lambda hp, b: (0, hp, 0, 0))],
        out_specs=(blk_spec(0), blk_spec(0), blk_spec(0), sum_spec, sum_spec, sum_spec,
                   pl.BlockSpec((nbr, 2, BLK, 2 * BLK), lambda hp, b: (0, hp, 0, 0))),
        out_shape=(SDS((T, AW), BF16), SDS((T, AW), BF16), SDS((T, AW), BF16),
                   SDS((1, AW), F32), SDS((1, AW), F32), SDS((1, AW), F32),
                   SDS((nbr, NH, BLK, 2 * BLK), F32)),
        scratch_shapes=[pltpu.VMEM((S, BLK), F32)] * 7,
    )(qkv, qkv, qkv, attn, lse, dattn, bias)


PADK = BLK
SCALE = 1.0 / math.sqrt(HD)
ATTN_UNROLL = 8


def _branch_geometry(br):
    dil = BRANCHES[br][1]
    sub = S // dil
    return dil, sub, sub // BLK


def _token_rows(br, i):
    dil, _, nblk = _branch_geometry(br)
    if dil == 1:
        return pl.ds(pl.multiple_of(i * BLK, BLK), BLK), i
    r = lax.shift_right_logical(i, nblk.bit_length() - 1)
    n = lax.bitwise_and(i, nblk - 1)
    return pl.ds(r + dil * BLK * n, BLK, stride=dil), n


def _sub_layout_loop(br, step):
    dil, sub, _ = _branch_geometry(br)
    rows = min(sub, 256)
    nchunk = sub // rows

    def it_step(it, carry):
        if dil == 1:
            src = pl.ds(pl.multiple_of(it * rows, rows), rows)
        else:
            r = lax.shift_right_logical(it, nchunk.bit_length() - 1)
            src = pl.ds(r + dil * rows * lax.bitwise_and(it, nchunk - 1), rows, stride=dil)
        step(src, pl.multiple_of(it * rows, BLK), rows)
        return carry

    lax.fori_loop(0, dil * nchunk, it_step, 0)


def _masked_bias(bias_ref, bm):
    qi = lax.broadcasted_iota(jnp.int32, (BLK, 2 * BLK), 0)
    kj = lax.broadcasted_iota(jnp.int32, (BLK, 2 * BLK), 1)
    first = jnp.logical_and(kj >= BLK, kj - BLK <= qi)
    valid = jnp.logical_or(first, jnp.logical_and(kj < BLK, kj >= qi))
    for br in range(len(BRANCHES)):
        for j in range(2):
            b = bias_ref[br, j]
            bm[br, 1, pl.ds(j * BLK, BLK), :] = jnp.where(valid, b, NEG_INF)
            bm[br, 0, pl.ds(j * BLK, BLK), :] = jnp.where(first, b, NEG_INF)


def _head_split(fn):
    def split(t):
        h0 = lax.broadcasted_iota(jnp.int32, t.shape, 1) < HD
        t = fn(t)
        return jnp.where(h0, t, 0.0).astype(BF16), jnp.where(h0, 0.0, t).astype(BF16)
    return split


def _attn_fwd(qkv, bias):
    nbr = len(BRANCHES)

    def body(q_ref, k_ref, v_ref, bias_ref, o_ref, lse_ref, qf, kf, vf, qs0, qs1, ks, vs, bm, ob, mb, lb):
        qf[...] = q_ref[...].astype(F32)
        kf[...] = k_ref[...].astype(F32)
        vf[...] = v_ref[...].astype(F32)
        _masked_bias(bias_ref, bm)
        ks[pl.ds(0, PADK), :] = jnp.zeros((PADK, BLK), BF16)
        vs[pl.ds(0, PADK), :] = jnp.zeros((PADK, BLK), BF16)
        head0 = lax.broadcasted_iota(jnp.int32, (BLK, BLK), 1) < HD
        split_q = _head_split(lambda t: t * SCALE)

        for br in range(nbr):
            nblk = _branch_geometry(br)[2]

            def stage(src, off, rows):
                qs0[pl.ds(off, rows), :], qs1[pl.ds(off, rows), :] = split_q(qf[src, :])
                ks[pl.ds(PADK + off, rows), :] = kf[src, :].astype(BF16)
                vs[pl.ds(PADK + off, rows), :] = vf[src, :].astype(BF16)

            _sub_layout_loop(br, stage)

            def blk(i, carry, br=br, nblk=nblk):
                base = pl.multiple_of(i * BLK, BLK)
                rows, n = _token_rows(br, i)
                q01 = jnp.concatenate([qs0[pl.ds(base, BLK), :], qs1[pl.ds(base, BLK), :]], axis=0)
                if nblk > 1:
                    kcat = ks[pl.ds(base, 2 * BLK), :]
                    vcat = vs[pl.ds(base, 2 * BLK), :]
                    s = _dot_nt(q01, kcat) + bm[br, jnp.minimum(n, 1)]
                else:
                    kcat = ks[pl.ds(PADK + base, BLK), :]
                    vcat = vs[pl.ds(PADK + base, BLK), :]
                    s = _dot_nt(q01, kcat) + bm[br, 0, :, BLK:]
                mx = jnp.max(s, axis=-1, keepdims=True)
                p = jnp.exp(s - mx)
                ls = jnp.sum(p, axis=-1, keepdims=True)
                o = _dot(p.astype(BF16), vcat)
                ob[br, rows, :] = jnp.where(head0, o[:BLK], o[BLK:])
                mb[br, rows, :] = jnp.where(head0, mx[:BLK], mx[BLK:])
                lb[br, rows, :] = jnp.where(head0, ls[:BLK], ls[BLK:])
                return carry

            lax.fori_loop(0, 16, blk, 0, unroll=ATTN_UNROLL)

        def merge(i, carry):
            rows = pl.ds(pl.multiple_of(i * 256, 256), 256)
            m_all = jnp.maximum(jnp.maximum(mb[0, rows, :], mb[1, rows, :]), mb[2, rows, :])
            num = jnp.zeros((256, BLK), F32)
            den = jnp.zeros((256, BLK), F32)
            for br in range(nbr):
                c = jnp.exp(mb[br, rows, :] - m_all)
                num = num + ob[br, rows, :] * c
                den = den + lb[br, rows, :] * c
            o_ref[rows, :] = num / den
            lse_ref[rows, :] = m_all + jnp.log(den)
            return carry

        lax.fori_loop(0, S // 256, merge, 0)

    npair = NH // 2
    blk_spec = lambda off: pl.BlockSpec((S, BLK), lambda b, hp: (b, off + hp))
    return pl.pallas_call(
        body, name="attn_fwd", grid=(BL, npair),
        in_specs=[blk_spec(0), blk_spec(npair), blk_spec(2 * npair),
                  pl.BlockSpec((nbr, 2, BLK, 2 * BLK), lambda b, hp: (0, hp, 0, 0))],
        out_specs=(blk_spec(0), blk_spec(0)),
        out_shape=(SDS((T, AW), F32), SDS((T, AW), F32)),
        scratch_shapes=[pltpu.VMEM((S, BLK), F32)] * 3 + [pltpu.VMEM((S, BLK), BF16)] * 2
        + [pltpu.VMEM((PADK + S, BLK), BF16)] * 2 + [pltpu.VMEM((nbr, 2, 2 * BLK, 2 * BLK), F32)]
        + [pltpu.VMEM((nbr, S, BLK), F32)] * 3,
    )(qkv, qkv, qkv, bias)


def _attn_bwd(qkv, attn, lse, dattn, bias):
    nbr = len(BRANCHES)

    def body(q_ref, k_ref, v_ref, o_ref, lse_ref, do_ref, bias_ref,
             dq_ref, dk_ref, dv_ref, sq_ref, sk_ref, sv_ref, db_ref,
             qf, kf, vf, dl, dqa, dka, dva, qs0, qs1, ds0, ds1, ks, vs, dks, dvs, bm):
        b = pl.program_id(1)
        qf[...] = q_ref[...].astype(F32)
        kf[...] = k_ref[...].astype(F32)
        vf[...] = v_ref[...].astype(F32)
        dqa[...] = jnp.zeros_like(dqa)
        dka[...] = jnp.zeros_like(dka)
        dva[...] = jnp.zeros_like(dva)
        _masked_bias(bias_ref, bm)
        ks[pl.ds(0, PADK), :] = jnp.zeros((PADK, BLK), BF16)
        vs[pl.ds(0, PADK), :] = jnp.zeros((PADK, BLK), BF16)
        head0 = lax.broadcasted_iota(jnp.int32, (BLK, BLK), 1) < HD
        split_q = _head_split(lambda t: t * SCALE)
        split_do = _head_split(lambda t: t)

        @pl.when(b == 0)
        def _():
            db_ref[...] = jnp.zeros_like(db_ref)
            sq_ref[...] = jnp.zeros_like(sq_ref)
            sk_ref[...] = jnp.zeros_like(sk_ref)
            sv_ref[...] = jnp.zeros_like(sv_ref)

        def delta(i, carry):
            rows = pl.ds(pl.multiple_of(i * 256, 256), 256)
            prod = do_ref[rows, :] * o_ref[rows, :]
            h0 = lax.broadcasted_iota(jnp.int32, (256, BLK), 1) < HD
            d0 = jnp.sum(jnp.where(h0, prod, 0.0), axis=-1, keepdims=True)
            d1 = jnp.sum(jnp.where(h0, 0.0, prod), axis=-1, keepdims=True)
            dl[rows, :] = jnp.where(h0, d0, d1)
            return carry

        lax.fori_loop(0, S // 256, delta, 0)

        for br in range(nbr):
            nblk = _branch_geometry(br)[2]

            def stage(src, off, rows):
                qs0[pl.ds(off, rows), :], qs1[pl.ds(off, rows), :] = split_q(qf[src, :])
                ds0[pl.ds(off, rows), :], ds1[pl.ds(off, rows), :] = split_do(do_ref[src, :])
                ks[pl.ds(PADK + off, rows), :] = kf[src, :].astype(BF16)
                vs[pl.ds(PADK + off, rows), :] = vf[src, :].astype(BF16)

            _sub_layout_loop(br, stage)
            dks[...] = jnp.zeros_like(dks)
            dvs[...] = jnp.zeros_like(dvs)

            def blk(i, carry, br=br, nblk=nblk):
                base = pl.multiple_of(i * BLK, BLK)
                rows, n = _token_rows(br, i)
                q01 = jnp.concatenate([qs0[pl.ds(base, BLK), :], qs1[pl.ds(base, BLK), :]], axis=0)
                do01 = jnp.concatenate([ds0[pl.ds(base, BLK), :], ds1[pl.ds(base, BLK), :]], axis=0)
                lse_b = lse_ref[rows, :]
                dl_b = dl[rows, :]
                lse01 = jnp.concatenate([lse_b[:, 0:1], lse_b[:, HD:HD + 1]], axis=0)
                dl01 = jnp.concatenate([dl_b[:, 0:1], dl_b[:, HD:HD + 1]], axis=0)
                if nblk > 1:
                    krows = pl.ds(base, 2 * BLK)
                    bias_m = bm[br, jnp.minimum(n, 1)]
                else:
                    krows = pl.ds(PADK + base, BLK)
                    bias_m = bm[br, 0, :, BLK:]
                kcat = ks[krows, :]
                vcat = vs[krows, :]
                p = jnp.exp(_dot_nt(q01, kcat) + bias_m - lse01)
                dsv = p * (_dot_nt(do01, vcat) - dl01)
                if nblk > 1:
                    db_ref[br, 0] += dsv[:BLK]
                    db_ref[br, 1] += dsv[BLK:]
                else:
                    db_ref[br, 0, :, BLK:] += dsv[:BLK]
                    db_ref[br, 1, :, BLK:] += dsv[BLK:]
                dsb = dsv.astype(BF16)
                dq01 = _dot(dsb, kcat)
                dqa[rows, :] = dqa[rows, :] + jnp.where(head0, dq01[:BLK], dq01[BLK:])
                dks[krows, :] = dks[krows, :] + _dot_tn(dsb, q01)
                dvs[krows, :] = dvs[krows, :] + _dot_tn(p.astype(BF16), do01)
                return carry

            lax.fori_loop(0, 16, blk, 0, unroll=ATTN_UNROLL)

            def fold(src, off, rows):
                dka[src, :] = dka[src, :] + dks[pl.ds(PADK + off, rows), :]
                dva[src, :] = dva[src, :] + dvs[pl.ds(PADK + off, rows), :]

            _sub_layout_loop(br, fold)

        def flush(i, carry):
            rows = pl.ds(pl.multiple_of(i * 256, 256), 256)
            for acc, out, cs, mul in ((dqa, dq_ref, sq_ref, SCALE), (dka, dk_ref, sk_ref, 1.0), (dva, dv_ref, sv_ref, 1.0)):
                val = acc[rows, :] * mul
                out[rows, :] = val.astype(BF16)
                cs[...] += _colsum(val)
            return carry

        lax.fori_loop(0, S // 256, flush, 0)

    npair = NH // 2
    blk_spec = lambda off: pl.BlockSpec((S, BLK), lambda hp, b: (b, off + hp))
    sum_spec = pl.BlockSpec((1, BLK), lambda hp, b: (0, hp))
    return pl.pallas_call(
        body, name="attn_bwd", grid=(npair, BL),
        in_specs=[blk_spec(0), blk_spec(npair), blk_spec(2 * npair), blk_spec(0), blk_spec(0), blk_spec(0),
                  pl.BlockSpec((nbr, 2, BLK, 2 * BLK), lambda hp, b: (0, hp, 0, 0))],
        out_specs=(blk_spec(0), blk_spec(0), blk_spec(0), sum_spec, sum_spec, sum_spec,
                   pl.BlockSpec((nbr, 2, BLK, 2 * BLK), lambda hp, b: (0, hp, 0, 0))),
        out_shape=(SDS((T, AW), BF16), SDS((T, AW), BF16), SDS((T, AW), BF16),
                   SDS((1, AW), F32), SDS((1, AW), F32), SDS((1, AW), F32),
                   SDS((nbr, NH, BLK, 2 * BLK), F32)),
        scratch_shapes=[pltpu.VMEM((S, BLK), F32)] * 7 + [pltpu.VMEM((S, BLK), BF16)] * 4
        + [pltpu.VMEM((PADK + S, BLK), BF16)] * 2 + [pltpu.VMEM((PADK + S, BLK), F32)] * 2
        + [pltpu.VMEM((nbr, 2, 2 * BLK, 2 * BLK), F32)],
    )(qkv, qkv, qkv, attn, lse, dattn, bias)


CH = 256
PADR = 32


def _conv_fwd(ag, conv_w, conv_b):
    def body(ag_ref, w_ref, b_ref, u1_ref, u0p):
        u0p[pl.ds(0, PADR), :] = jnp.zeros((PADR, CW), F32)

        def glu(i, carry):
            t0 = pl.multiple_of(i * CH, CH)
            a = ag_ref[pl.ds(t0, CH), :CW]
            g = ag_ref[pl.ds(t0, CH), CW:]
            u0p[pl.ds(PADR + t0, CH), :] = a * _sigmoid(g)
            return carry

        lax.fori_loop(0, S // CH, glu, 0)

        def conv(i, carry):
            t0 = pl.multiple_of(i * CH, CH)
            win = u0p[pl.ds(t0, CH + PADR), :]
            acc = jnp.zeros((CH, CW), F32) + b_ref[...]
            for k in range(CK):
                off = PADR - (CK - 1) + k
                acc = acc + win[off:off + CH, :] * w_ref[k:k + 1, :]
            u1_ref[pl.ds(t0, CH), :] = acc
            return carry

        lax.fori_loop(0, S // CH, conv, 0)

    return pl.pallas_call(
        body, name="conv_fwd", grid=(BL,),
        in_specs=[pl.BlockSpec((S, 2 * CW), lambda b: (b, 0)),
                  pl.BlockSpec((CK, CW), lambda b: (0, 0)),
                  pl.BlockSpec((1, CW), lambda b: (0, 0))],
        out_specs=pl.BlockSpec((S, CW), lambda b: (b, 0)),
        out_shape=SDS((T, CW), F32),
        scratch_shapes=[pltpu.VMEM((S + PADR, CW), F32)],
    )(ag, conv_w, conv_b)


def _conv_post(u1, cg, cb):
    mu = _rowmean(u1)
    uc = u1 - mu
    rstd = lax.rsqrt(_rowmean(uc * uc) + LN_EPS)
    xh = uc * rstd
    u2 = xh * cg + cb
    sg = _sigmoid(u2)
    return xh, rstd, u2, sg, u2 * sg


def _mix_fwd(attn, u1, ga, gc, cg, cb):
    def body(a_ref, u_ref, ga_ref, gc_ref, cg_ref, cb_ref, o_ref):
        a = a_ref[...]
        ra = lax.rsqrt(_rowmean(a * a) + LN_EPS)
        o_ref[:, :AW] = (a * ra * ga_ref[...]).astype(BF16)
        _, _, _, _, u3 = _conv_post(u_ref[...], cg_ref[...], cb_ref[...])
        rc = lax.rsqrt(_rowmean(u3 * u3) + LN_EPS)
        o_ref[:, AW:] = (u3 * rc * gc_ref[...]).astype(BF16)

    vec = lambda w: pl.BlockSpec((1, w), lambda m: (0, 0))
    return pl.pallas_call(
        body, name="mix_fwd", grid=(T // TM,),
        in_specs=[pl.BlockSpec((TM, AW), lambda m: (m, 0)), pl.BlockSpec((TM, CW), lambda m: (m, 0)),
                  vec(AW), vec(CW), vec(CW), vec(CW)],
        out_specs=pl.BlockSpec((TM, D), lambda m: (m, 0)),
        out_shape=SDS((T, D), BF16),
    )(attn, u1, ga, gc, cg, cb)


def _mix_bwd(dmixed, attn, u1, ga, gc, cg, cb):
    def body(dm_ref, a_ref, u_ref, ga_ref, gc_ref, cg_ref, cb_ref,
             da_ref, du_ref, g_an, g_cn, g_lg, g_lb, g_cb):
        @pl.when(pl.program_id(0) == 0)
        def _():
            for r in (g_an, g_cn, g_lg, g_lb, g_cb):
                r[...] = jnp.zeros_like(r)

        a = a_ref[...]
        dna = dm_ref[:, :AW]
        ra = lax.rsqrt(_rowmean(a * a) + LN_EPS)
        g_an[...] += _colsum(dna * a * ra)
        dat = dna * ga_ref[...]
        da_ref[...] = ra * dat - a * (ra * ra * ra) * _rowmean(dat * a)

        xh, rstd, u2, sg, u3 = _conv_post(u_ref[...], cg_ref[...], cb_ref[...])
        dnc = dm_ref[:, AW:]
        rc = lax.rsqrt(_rowmean(u3 * u3) + LN_EPS)
        g_cn[...] += _colsum(dnc * u3 * rc)
        dut = dnc * gc_ref[...]
        du3 = rc * dut - u3 * (rc * rc * rc) * _rowmean(dut * u3)
        du2 = du3 * sg * (1.0 + u2 * (1.0 - sg))
        g_lg[...] += _colsum(du2 * xh)
        g_lb[...] += _colsum(du2)
        dxh = du2 * cg_ref[...]
        du1 = rstd * (dxh - _rowmean(dxh) - xh * _rowmean(dxh * xh))
        g_cb[...] += _colsum(du1)
        du_ref[...] = du1

    vec = lambda w: pl.BlockSpec((1, w), lambda m: (0, 0))
    return pl.pallas_call(
        body, name="mix_bwd", grid=(T // TM,),
        in_specs=[pl.BlockSpec((TM, D), lambda m: (m, 0)), pl.BlockSpec((TM, AW), lambda m: (m, 0)),
                  pl.BlockSpec((TM, CW), lambda m: (m, 0)), vec(AW), vec(CW), vec(CW), vec(CW)],
        out_specs=(pl.BlockSpec((TM, AW), lambda m: (m, 0)), pl.BlockSpec((TM, CW), lambda m: (m, 0)),
                   vec(AW), vec(CW), vec(CW), vec(CW), vec(CW)),
        out_shape=(SDS((T, AW), F32), SDS((T, CW), F32),
                   SDS((1, AW), F32), SDS((1, CW), F32), SDS((1, CW), F32), SDS((1, CW), F32), SDS((1, CW), F32)),
    )(dmixed, attn, u1, ga, gc, cg, cb)


def _conv_bwd(du1, ag, conv_w):
    def body(du_ref, ag_ref, w_ref, dag_ref, cs_ref, gw_ref, u0p, dup):
        @pl.when(pl.program_id(0) == 0)
        def _():
            cs_ref[...] = jnp.zeros_like(cs_ref)
            gw_ref[...] = jnp.zeros_like(gw_ref)

        u0p[pl.ds(0, PADR), :] = jnp.zeros((PADR, CW), F32)
        dup[pl.ds(S, PADR), :] = jnp.zeros((PADR, CW), F32)

        def fill(i, carry):
            t0 = pl.multiple_of(i * CH, CH)
            a = ag_ref[pl.ds(t0, CH), :CW]
            g = ag_ref[pl.ds(t0, CH), CW:]
            u0p[pl.ds(PADR + t0, CH), :] = a * _sigmoid(g)
            dup[pl.ds(t0, CH), :] = du_ref[pl.ds(t0, CH), :]
            return carry

        lax.fori_loop(0, S // CH, fill, 0)

        def chunk(i, carry):
            t0 = pl.multiple_of(i * CH, CH)
            d = dup[pl.ds(t0, CH), :]
            win_u = u0p[pl.ds(t0, CH + PADR), :]
            win_d = dup[pl.ds(t0, CH + PADR), :]
            du0 = jnp.zeros((CH, CW), F32)
            for k in range(CK):
                off = PADR - (CK - 1) + k
                gw_ref[k:k + 1, :] += _colsum(d * win_u[off:off + CH, :])
                fo = CK - 1 - k
                du0 = du0 + win_d[fo:fo + CH, :] * w_ref[k:k + 1, :]
            a = ag_ref[pl.ds(t0, CH), :CW]
            sg = _sigmoid(ag_ref[pl.ds(t0, CH), CW:])
            da = du0 * sg
            dg = du0 * a * sg * (1.0 - sg)
            dag_ref[pl.ds(t0, CH), :CW] = da.astype(BF16)
            dag_ref[pl.ds(t0, CH), CW:] = dg.astype(BF16)
            cs_ref[:, :CW] += _colsum(da)
            cs_ref[:, CW:] += _colsum(dg)
            return carry

        lax.fori_loop(0, S // CH, chunk, 0)

    return pl.pallas_call(
        body, name="conv_bwd", grid=(BL,),
        in_specs=[pl.BlockSpec((S, CW), lambda b: (b, 0)), pl.BlockSpec((S, 2 * CW), lambda b: (b, 0)),
                  pl.BlockSpec((CK, CW), lambda b: (0, 0))],
        out_specs=(pl.BlockSpec((S, 2 * CW), lambda b: (b, 0)),
                   pl.BlockSpec((1, 2 * CW), lambda b: (0, 0)),
                   pl.BlockSpec((PADR, CW), lambda b: (0, 0))),
        out_shape=(SDS((T, 2 * CW), BF16), SDS((1, 2 * CW), F32), SDS((PADR, CW), F32)),
        scratch_shapes=[pltpu.VMEM((S + PADR, CW), F32), pltpu.VMEM((S + PADR, CW), F32)],
    )(du1, ag, conv_w)


def _layer_norm_fwd(z):
    mu = _rowmean(z)
    zc = z - mu
    rstd = lax.rsqrt(_rowmean(zc * zc) + LN_EPS)
    return zc * rstd, rstd


def _layer_norm_bwd(dy, xh, rstd, g):
    dxh = dy * g
    return rstd * (dxh - _rowmean(dxh) - xh * _rowmean(dxh * xh))


def _out_proj_ln1(mixed, w_out, x2, g1, b1):
    def body(a_ref, w_ref, x_ref, g_ref, b_ref, xh_ref, rstd_ref, x1_ref):
        z = ALPHA * x_ref[...] + _dot(a_ref[...], w_ref[...])
        xh, rstd = _layer_norm_fwd(z)
        xh_ref[...] = xh
        rstd_ref[...] = rstd
        x1_ref[...] = (xh * g_ref[...] + b_ref[...]).astype(BF16)

    vec = pl.BlockSpec((1, D), lambda m: (0, 0))
    row = pl.BlockSpec((TM, D), lambda m: (m, 0))
    return pl.pallas_call(
        body, name="out_proj_ln1", grid=(T // TM,),
        in_specs=[row, pl.BlockSpec((D, D), lambda m: (0, 0)), row, vec, vec],
        out_specs=(row, pl.BlockSpec((TM, 1), lambda m: (m, 0)), row),
        out_shape=(SDS((T, D), F32), SDS((T, 1), F32), SDS((T, D), BF16)),
    )(mixed, w_out, x2, g1, b1)


def _seq_start(m):
    return lax.bitwise_and(m, S // TM - 1) == 0


def _causal3(ext, w_ref, b_ref):
    x0 = ext[pl.ds(8, TM), :]
    x1 = ext[pl.ds(7, TM), :]
    x2 = ext[pl.ds(6, TM), :]
    y = w_ref[2:3, :] * x0 + w_ref[1:2, :] * x1 + w_ref[0:1, :] * x2 + b_ref[...]
    return y, x0, x1, x2


def _ffn_up(x1b, w_up, fcw, fcb):
    def body(x_ref, wg_ref, wv_ref, cwg_ref, cwv_ref, cbg_ref, cbv_ref, up_ref, act_ref, extg, extv):
        @pl.when(_seq_start(pl.program_id(1)))
        def _():
            extg[pl.ds(0, 8), :] = jnp.zeros((8, FT), F32)
            extv[pl.ds(0, 8), :] = jnp.zeros((8, FT), F32)

        x = x_ref[...]
        ug = _dot(x, wg_ref[...]).astype(BF16)
        uv = _dot(x, wv_ref[...]).astype(BF16)
        up_ref[:, :FT] = ug
        up_ref[:, FT:] = uv
        extg[pl.ds(8, TM), :] = ug.astype(F32)
        extv[pl.ds(8, TM), :] = uv.astype(F32)
        gate = _causal3(extg, cwg_ref, cbg_ref)[0]
        val = _causal3(extv, cwv_ref, cbv_ref)[0]
        act_ref[...] = (gate * _sigmoid(gate) * val).astype(BF16)
        extg[pl.ds(0, 8), :] = extg[pl.ds(TM, 8), :]
        extv[pl.ds(0, 8), :] = extv[pl.ds(TM, 8), :]

    wspec = lambda off: pl.BlockSpec((D, FT), lambda n, m: (0, n + off))
    cwspec = lambda off: pl.BlockSpec((FK, FT), lambda n, m: (0, n + off))
    cbspec = lambda off: pl.BlockSpec((1, FT), lambda n, m: (0, n + off))
    return pl.pallas_call(
        body, name="ffn_up", grid=(NFT, T // TM),
        in_specs=[pl.BlockSpec((TM, D), lambda n, m: (m, 0)), wspec(0), wspec(NFT),
                  cwspec(0), cwspec(NFT), cbspec(0), cbspec(NFT)],
        out_specs=(pl.BlockSpec((TM, 2 * FT), lambda n, m: (m, n)), pl.BlockSpec((TM, FT), lambda n, m: (m, n))),
        out_shape=(SDS((T, 2 * DFF), BF16), SDS((T, DFF), BF16)),
        scratch_shapes=[pltpu.VMEM((TM + 8, FT), F32)] * 2,
    )(x1b, w_up, w_up, fcw, fcw, fcb, fcb)


def _ffn_down_loss(act, w_down, xh1, g1, b1, g2, b2, target):
    def body(a_ref, w_ref, xh1_ref, g1_ref, b1_ref, g2_ref, b2_ref, t_ref, dz_ref, loss_ref, gg_ref, gb_ref):
        @pl.when(pl.program_id(0) == 0)
        def _():
            loss_ref[...] = jnp.zeros_like(loss_ref)
            gg_ref[...] = jnp.zeros_like(gg_ref)
            gb_ref[...] = jnp.zeros_like(gb_ref)

        x1 = xh1_ref[...] * g1_ref[...] + b1_ref[...]
        z = ALPHA * x1 + _dot(a_ref[...], w_ref[...])
        xh, rstd = _layer_norm_fwd(z)
        diff = xh * g2_ref[...] + b2_ref[...] - t_ref[...]
        loss_ref[...] += 0.5 * _colsum(_rowmean(diff * diff))
        dout = diff * (1.0 / D)
        gg_ref[...] += _colsum(dout * xh)
        gb_ref[...] += _colsum(dout)
        dz_ref[...] = _layer_norm_bwd(dout, xh, rstd, g2_ref[...])

    vec = pl.BlockSpec((1, D), lambda m: (0, 0))
    row = pl.BlockSpec((TM, D), lambda m: (m, 0))
    return pl.pallas_call(
        body, name="ffn_down_loss", grid=(T // TM,),
        in_specs=[pl.BlockSpec((TM, DFF), lambda m: (m, 0)), pl.BlockSpec((DFF, D), lambda m: (0, 0)),
                  row, vec, vec, vec, vec, row],
        out_specs=(row, pl.BlockSpec((1, 1), lambda m: (0, 0)), vec, vec),
        out_shape=(SDS((T, D), F32), SDS((1, 1), F32), SDS((1, D), F32), SDS((1, D), F32)),
    )(act, w_down, xh1, g1, b1, g2, b2, target)


def _ffn_down_bwd(dz2, w_down, up, fcw, fcb):
    def body(dz_ref, wd_ref, up_ref, cwg_ref, cwv_ref, cbg_ref, cbv_ref,
             dup_ref, csg_ref, csv_ref, gwg_ref, gwv_ref, extg, extv):
        m = pl.program_id(1)

        @pl.when(_seq_start(m))
        def _():
            extg[pl.ds(0, 8), :] = jnp.zeros((8, FT), F32)
            extv[pl.ds(0, 8), :] = jnp.zeros((8, FT), F32)

        @pl.when(m == 0)
        def _():
            for r in (csg_ref, csv_ref, gwg_ref, gwv_ref):
                r[...] = jnp.zeros_like(r)

        dact = _dot_nt(dz_ref[...].astype(BF16), wd_ref[...])
        extg[pl.ds(8, TM), :] = up_ref[:, :FT].astype(F32)
        extv[pl.ds(8, TM), :] = up_ref[:, FT:].astype(F32)
        gate, g0, g1, g2 = _causal3(extg, cwg_ref, cbg_ref)
        val, v0, v1, v2 = _causal3(extv, cwv_ref, cbv_ref)
        sg = _sigmoid(gate)
        dgate = dact * val * sg * (1.0 + gate * (1.0 - sg))
        dval = dact * gate * sg
        dup_ref[:, :FT] = dgate.astype(BF16)
        dup_ref[:, FT:] = dval.astype(BF16)
        csg_ref[...] += _colsum(dgate)
        csv_ref[...] += _colsum(dval)
        for k, (xg, xv) in enumerate(((g2, v2), (g1, v1), (g0, v0))):
            gwg_ref[k:k + 1, :] += _colsum(dgate * xg)
            gwv_ref[k:k + 1, :] += _colsum(dval * xv)
        extg[pl.ds(0, 8), :] = extg[pl.ds(TM, 8), :]
        extv[pl.ds(0, 8), :] = extv[pl.ds(TM, 8), :]

    cwspec = lambda off: pl.BlockSpec((FK, FT), lambda n, m: (0, n + off))
    cbspec = lambda off: pl.BlockSpec((1, FT), lambda n, m: (0, n + off))
    cs = pl.BlockSpec((1, FT), lambda n, m: (0, n))
    gw = pl.BlockSpec((FK, FT), lambda n, m: (0, n))
    return pl.pallas_call(
        body, name="ffn_down_bwd", grid=(NFT, T // TM),
        in_specs=[pl.BlockSpec((TM, D), lambda n, m: (m, 0)), pl.BlockSpec((FT, D), lambda n, m: (n, 0)),
                  pl.BlockSpec((TM, 2 * FT), lambda n, m: (m, n)),
                  cwspec(0), cwspec(NFT), cbspec(0), cbspec(NFT)],
        out_specs=(pl.BlockSpec((TM, 2 * FT), lambda n, m: (m, n)), cs, cs, gw, gw),
        out_shape=(SDS((T, 2 * DFF), BF16), SDS((1, DFF), F32), SDS((1, DFF), F32),
                   SDS((FK, DFF), F32), SDS((FK, DFF), F32)),
        scratch_shapes=[pltpu.VMEM((TM + 8, FT), F32)] * 2,
    )(dz2, w_down, up, fcw, fcw, fcb, fcb)


HALO = 16


def _conv3_transpose(dup, fcw_il):
    tiles = T // TM

    def body(d_ref, h_ref, w_ref, o_ref, ext):
        m = pl.program_id(1)
        ext[pl.ds(0, TM), :] = d_ref[...].astype(F32)
        last = lax.bitwise_and(m + 1, S // TM - 1) == 0
        ext[pl.ds(TM, HALO), :] = jnp.where(last, 0.0, h_ref[...].astype(F32))
        y = (w_ref[2:3, :] * ext[pl.ds(0, TM), :] + w_ref[1:2, :] * ext[pl.ds(1, TM), :]
             + w_ref[0:1, :] * ext[pl.ds(2, TM), :])
        o_ref[...] = y.astype(BF16)

    return pl.pallas_call(
        body, name="conv3_transpose", grid=(NFT, tiles),
        in_specs=[pl.BlockSpec((TM, 2 * FT), lambda n, m: (m, n)),
                  pl.BlockSpec((HALO, 2 * FT), lambda n, m: (jnp.minimum((m + 1) * (TM // HALO), T // HALO - 1), n)),
                  pl.BlockSpec((FK, 2 * FT), lambda n, m: (0, n))],
        out_specs=pl.BlockSpec((TM, 2 * FT), lambda n, m: (m, n)),
        out_shape=SDS((T, 2 * DFF), BF16),
        scratch_shapes=[pltpu.VMEM((TM + HALO, 2 * FT), F32)],
    )(dup, dup, fcw_il)


def _ffn_up_bwd_ln1(dpre, w_up, dz2, xh1, rstd1, g1):
    def body(a_ref, wg_ref, wv_ref, dz2_ref, xh_ref, rstd_ref, g_ref, dz1_ref, gg_ref, gb_ref, acc):
        m, kk = pl.program_id(0), pl.program_id(1)

        @pl.when(kk == 0)
        def _():
            acc[...] = jnp.zeros_like(acc)

        acc[...] += _dot_nt(a_ref[:, :FT], wg_ref[...]) + _dot_nt(a_ref[:, FT:], wv_ref[...])

        @pl.when(kk == NFT - 1)
        def _():
            @pl.when(m == 0)
            def _():
                gg_ref[...] = jnp.zeros_like(gg_ref)
                gb_ref[...] = jnp.zeros_like(gb_ref)

            dx1 = acc[...] + ALPHA * dz2_ref[...]
            xh = xh_ref[...]
            gg_ref[...] += _colsum(dx1 * xh)
            gb_ref[...] += _colsum(dx1)
            dz1_ref[...] = _layer_norm_bwd(dx1, xh, rstd_ref[...], g_ref[...])

    vec = pl.BlockSpec((1, D), lambda m, k: (0, 0))
    row = pl.BlockSpec((TM, D), lambda m, k: (m, 0))
    wspec = lambda off: pl.BlockSpec((D, FT), lambda m, k: (0, k + off))
    return pl.pallas_call(
        body, name="ffn_up_bwd_ln1", grid=(T // TM, NFT),
        in_specs=[pl.BlockSpec((TM, 2 * FT), lambda m, k: (m, k)), wspec(0), wspec(NFT),
                  row, row, pl.BlockSpec((TM, 1), lambda m, k: (m, 0)), vec],
        out_specs=(row, vec, vec),
        out_shape=(SDS((T, D), F32), SDS((1, D), F32), SDS((1, D), F32)),
        scratch_shapes=[pltpu.VMEM((TM, D), F32)],
    )(dpre, w_up, w_up, dz2, xh1, rstd1, g1)


def _grad_w_up(dpre, x1b):
    tk = 1024

    def body(a_ref, b_ref, og_ref, ov_ref):
        @pl.when(pl.program_id(1) == 0)
        def _():
            og_ref[...] = jnp.zeros_like(og_ref)
            ov_ref[...] = jnp.zeros_like(ov_ref)

        r = _dot_tn(a_ref[...], b_ref[...])
        og_ref[...] += r[:FT]
        ov_ref[...] += r[FT:]

    out = pl.BlockSpec((FT, D), lambda n, k: (n, 0))
    return pl.pallas_call(
        body, name="grad_w_up", grid=(NFT, T // tk),
        in_specs=[pl.BlockSpec((tk, 2 * FT), lambda n, k: (k, n)), pl.BlockSpec((tk, D), lambda n, k: (k, 0))],
        out_specs=(out, out),
        out_shape=(SDS((DFF, D), F32), SDS((DFF, D), F32)),
    )(dpre, x1b)


def _row_tile(rows, cols):
    if rows * cols * 4 <= (1 << 20) or rows % 8:
        return rows
    for t in (256, 176, 128, 88, 64, 32, 16, 8):
        if rows % t == 0 and t * cols * 4 <= (1 << 20):
            return t
    return 8


def _sum8(r, name):
    _, rows, cols = r.shape
    tr = _row_tile(rows, cols)

    def body(r_ref, o_ref):
        acc = r_ref[0].astype(F32)
        for p in range(1, NDEV):
            acc = acc + r_ref[p].astype(F32)
        o_ref[...] = acc

    return pl.pallas_call(
        body, name=name, grid=(rows // tr,),
        in_specs=[pl.BlockSpec((NDEV, tr, cols), lambda i: (0, i, 0))],
        out_specs=pl.BlockSpec((tr, cols), lambda i: (i, 0)),
        out_shape=SDS((rows, cols), F32),
    )(r)


def _adamw(w, g, m, v, name):
    rows, cols = w.shape
    tr = _row_tile(rows, cols)

    def body(w_ref, g_ref, m_ref, v_ref, d_ref, nm_ref, nv_ref):
        g_ = g_ref[...]
        m_ = B1 * m_ref[...] + (1.0 - B1) * g_
        v_ = B2 * v_ref[...] + (1.0 - B2) * jnp.square(g_)
        m_hat = m_ / (1.0 - B1 ** STEP)
        v_hat = v_ / (1.0 - B2 ** STEP)
        d_ref[...] = -LR * (m_hat / (jnp.sqrt(v_hat) + AEPS) + WD * w_ref[...])
        nm_ref[...] = m_
        nv_ref[...] = v_

    spec = pl.BlockSpec((tr, cols), lambda i: (i, 0))
    shp = SDS((rows, cols), F32)
    return pl.pallas_call(
        body, name=name, grid=(rows // tr,), in_specs=[spec] * 4, out_specs=(spec,) * 3,
        out_shape=(shp, shp, shp),
    )(w, g, m, v)


def _interleave(a):
    r = a.shape[0]
    return a.reshape(r, 2, NFT, FT).transpose(0, 2, 1, 3).reshape(r, 2 * DFF)


def _local_step(x2, target, rel_table, w_in, b_in, conv_w, conv_b, conv_ln_g, conv_ln_b, attn_norm_g,
                conv_norm_g, w_out, ln1_g, ln1_b, w_up, ffn_conv_w, ffn_conv_b, w_down, ln2_g, ln2_b):
    buckets = jnp.asarray(_bucket_maps())
    bias = _bias_table(rel_table, buckets)

    qkv = _mm_nn_bias(x2, w_in, b_in, 0, 3, AW, BF16, "proj_qkv")
    ag = _mm_nn_bias(x2, w_in, b_in, 3 * AW // CW, 2, CW, F32, "proj_ag")
    attn, lse = _attn_fwd(qkv, bias)
    u1 = _conv_fwd(ag, conv_w, conv_b)
    mixed = _mix_fwd(attn, u1, attn_norm_g, conv_norm_g, conv_ln_g, conv_ln_b)
    xh1, rstd1, x1b = _out_proj_ln1(mixed, w_out, x2, ln1_g, ln1_b)
    up, act = _ffn_up(x1b, w_up, ffn_conv_w, ffn_conv_b)
    dz2, loss, g_ln2_g, g_ln2_b = _ffn_down_loss(act, w_down, xh1, ln1_g, ln1_b, ln2_g, ln2_b, target)

    dup, cs_g, cs_v, gfw_g, gfw_v = _ffn_down_bwd(dz2, w_down, up, ffn_conv_w, ffn_conv_b)
    g_w_down = _mm_tn(act, dz2, DFF // 2, 512, "grad_w_down")
    dpre = _conv3_transpose(dup, _interleave(ffn_conv_w))
    dz1, g_ln1_g, g_ln1_b = _ffn_up_bwd_ln1(dpre, w_up, dz2, xh1, rstd1, ln1_g)
    g_w_up_gate_t, g_w_up_val_t = _grad_w_up(dpre, x1b)
    dmixed = _mm_nt(dz1, w_out, None, 0.0, "dmixed")
    g_w_out = _mm_tn(mixed, dz1, D, 512, "grad_w_out")
    dattn, du1, g_an, g_cn, g_clg, g_clb, g_cb = _mix_bwd(
        dmixed, attn, u1, attn_norm_g, conv_norm_g, conv_ln_g, conv_ln_b)
    dag, cs_ag, g_conv_w = _conv_bwd(du1, ag, conv_w)
    dq, dk, dv, cs_q, cs_k, cs_v2, dbias = _attn_bwd(qkv, attn, lse, dattn, bias)
    g_rel = _rel_table_grad(dbias, buckets)
    dh = jnp.concatenate([dq, dk, dv, dag], axis=1)
    grad_x = _mm_nt(dh, w_in, dz1, ALPHA, "grad_x")
    g_w_in_t = _mm_tn(dh, x2, INW // 2, 512, "grad_w_in")

    grads = dict(
        rel_table=g_rel,
        b_in=jnp.concatenate([cs_q, cs_k, cs_v2, cs_ag], axis=1),
        conv_b=g_cb, conv_ln_g=g_clg, conv_ln_b=g_clb, attn_norm_g=g_an, conv_norm_g=g_cn,
        ln1_g=g_ln1_g, ln1_b=g_ln1_b,
        ffn_conv_b=jnp.concatenate([cs_g, cs_v], axis=1),
        ln2_g=g_ln2_g, ln2_b=g_ln2_b,
        conv_w=g_conv_w[:CK],
        ffn_conv_w=jnp.concatenate([gfw_g, gfw_v], axis=1),
        w_in_t=g_w_in_t, w_out=g_w_out, w_up_gate_t=g_w_up_gate_t, w_up_val_t=g_w_up_val_t, w_down=g_w_down,
    )
    return loss, grad_x, grads


SMALL = (("rel_table", (NBUCKET, NH)), ("b_in", (1, INW)), ("conv_b", (1, CW)), ("conv_ln_g", (1, CW)),
         ("conv_ln_b", (1, CW)), ("attn_norm_g", (1, AW)), ("conv_norm_g", (1, CW)), ("ln1_g", (1, D)),
         ("ln1_b", (1, D)), ("ffn_conv_b", (1, 2 * DFF)), ("ln2_g", (1, D)), ("ln2_b", (1, D)))
SHARDED_SMALL = (("conv_w", (CK, CW)), ("ffn_conv_w", (FK, 2 * DFF)))


def _pack(parts):
    flat = jnp.concatenate([p.reshape(-1) for p in parts])
    tile = 8 * PACK_LANES
    pad = (-flat.shape[0]) % tile
    return jnp.pad(flat, (0, pad)).reshape(-1, PACK_LANES)


def _unpack(packed, specs):
    flat = packed.reshape(-1)
    out, off = {}, 0
    for name, shp in specs:
        size = int(np.prod(shp))
        out[name] = flat[off:off + size].reshape(shp)
        off += size
    return out


def kernel(x, rel_table, w_in, b_in, conv_w, conv_b, conv_ln_g, conv_ln_b, attn_norm_g, conv_norm_g, w_out, ln1_g, ln1_b, w_up, ffn_conv_w, ffn_conv_b, w_down, ln2_g, ln2_b, loss_target, m_rel_table, m_w_in, m_b_in, m_conv_w, m_conv_b, m_conv_ln_g, m_conv_ln_b, m_attn_norm_g, m_conv_norm_g, m_w_out, m_ln1_g, m_ln1_b, m_w_up, m_ffn_conv_w, m_ffn_conv_b, m_w_down, m_ln2_g, m_ln2_b, v_rel_table, v_w_in, v_b_in, v_conv_w, v_conv_b, v_conv_ln_g, v_conv_ln_b, v_attn_norm_g, v_conv_norm_g, v_w_out, v_ln1_g, v_ln1_b, v_w_up, v_ffn_conv_w, v_ffn_conv_b, v_w_down, v_ln2_g, v_ln2_b):
    given = dict(locals())
    me = 4 * lax.axis_index("x") + 2 * lax.axis_index("y") + lax.axis_index("c")

    gathered = _exchange(
        [(w_in[0].astype(BF16), "gather"), (w_out[0].astype(BF16), "gather"), (w_up[0].astype(BF16), "gather"),
         (w_down[0].astype(BF16), "gather"), (conv_w[0], "gather"), (ffn_conv_w[0], "gather")],
        "gather_weights")
    cols = lambda a: a.transpose(1, 0, 2).reshape(a.shape[1], NDEV * a.shape[2])
    rows = lambda a: a.reshape(NDEV * a.shape[1], a.shape[2])
    w_in_f, w_out_f, w_up_f, w_down_f = cols(gathered[0]), rows(gathered[1]), cols(gathered[2]), rows(gathered[3])
    conv_w_f, ffn_conv_w_f = cols(gathered[4]), cols(gathered[5])

    loss, grad_x, g = _local_step(
        x.reshape(T, D), loss_target.reshape(T, D), rel_table, w_in_f, b_in, conv_w_f, conv_b, conv_ln_g,
        conv_ln_b, attn_norm_g, conv_norm_g, w_out_f, ln1_g, ln1_b, w_up_f, ffn_conv_w_f, ffn_conv_b,
        w_down_f, ln2_g, ln2_b)

    small_specs = SMALL + SHARDED_SMALL
    packed = _pack([g[n] for n, _ in small_specs])
    g_w_up_t = jnp.concatenate([g["w_up_gate_t"], g["w_up_val_t"]], axis=0)
    stack = lambda a: a.reshape(NDEV, a.shape[0] // NDEV, a.shape[1])
    got = _exchange(
        [(stack(g["w_in_t"]), "scatter"), (stack(g["w_out"]), "scatter"), (stack(g_w_up_t), "scatter"),
         (stack(g["w_down"]), "scatter"), (packed, "gather")],
        "exchange_grads")

    grad = {}
    grad["w_in"] = _sum8(got[0], "sum_w_in").T[None]
    grad["w_out"] = _sum8(got[1], "sum_w_out")[None]
    grad["w_up"] = _sum8(got[2], "sum_w_up").T[None]
    grad["w_down"] = _sum8(got[3], "sum_w_down")[None]
    small = _unpack(_sum8(got[4], "sum_small"), small_specs)
    for n, _ in SMALL:
        grad[n] = small[n]
    grad["conv_w"] = lax.dynamic_slice_in_dim(small["conv_w"], me * (CW // NDEV), CW // NDEV, axis=1)[None]
    grad["ffn_conv_w"] = lax.dynamic_slice_in_dim(small["ffn_conv_w"], me * (2 * DFF // NDEV), 2 * DFF // NDEV, axis=1)[None]

    delta, new_m, new_v = {}, {}, {}
    for n in ("w_in", "w_out", "w_up", "w_down", "conv_w", "ffn_conv_w"):
        shp = given[n].shape
        two = lambda a: a.reshape(shp[-2], shp[-1])
        d_, m_, v_ = _adamw(two(given[n]), two(grad[n]), two(given["m_" + n]), two(given["v_" + n]), "adamw_" + n)
        delta[n], new_m[n], new_v[n] = d_.reshape(shp), m_.reshape(shp), v_.reshape(shp)
    wp = _pack([given[n] for n, _ in SMALL])
    gp = _pack([grad[n] for n, _ in SMALL])
    mp = _pack([given["m_" + n] for n, _ in SMALL])
    vp = _pack([given["v_" + n] for n, _ in SMALL])
    dp, nmp, nvp = _adamw(wp, gp, mp, vp, "adamw_small")
    for dst, src in ((delta, dp), (new_m, nmp), (new_v, nvp)):
        dst.update(_unpack(src, SMALL))

    order = ("rel_table", "w_in", "b_in", "conv_w", "conv_b", "conv_ln_g", "conv_ln_b", "attn_norm_g",
             "conv_norm_g", "w_out", "ln1_g", "ln1_b", "w_up", "ffn_conv_w", "ffn_conv_b", "w_down", "ln2_g", "ln2_b")
    total_loss = lax.psum(loss[0, 0], ("x", "y", "c"))
    return (total_loss, grad_x.reshape(BL, S, D), *[grad[n] for n in order], *[delta[n] for n in order],
            *[new_m[n] for n in order], *[new_v[n] for n in order])
```

```python
import functools
import math

import numpy as np
import jax
import jax.numpy as jnp
from jax import lax
from jax.experimental import pallas as pl
from jax.experimental.pallas import tpu as pltpu

F32 = jnp.float32
BF16 = jnp.bfloat16
SDS = jax.ShapeDtypeStruct

NDEV = 8
D = 1024
S = 2048
BL = 2
T = BL * S
NH = 12
HD = 64
AW = NH * HD
CW = D - AW
INW = 3 * AW + 2 * CW
CK = 31
DFF = 2816
FK = 3
BLK = 128
NBUCKET = 32
BRANCHES = ((128, 1), (512, 4), (2048, 16))
ALPHA = 2.0 ** 0.25
LN_EPS = 1e-5
NEG_INF = -1e30
LR, B1, B2, AEPS, WD, STEP = 0.001, 0.9, 0.999, 1e-08, 0.01, 10

TM = 512
FT = 256
NFT = DFF // FT
PACK_LANES = 128
GRAD_WIRE = BF16

assert all(w // d == BLK for w, d in BRANCHES)


def _dot(a, b):
    return jnp.dot(a, b, preferred_element_type=F32)


def _dot_nt(a, b):
    return lax.dot_general(a, b, (((1,), (1,)), ((), ())), preferred_element_type=F32)


def _dot_tn(a, b):
    return lax.dot_general(a, b, (((0,), (0,)), ((), ())), preferred_element_type=F32)


def _rowmean(v):
    return jnp.mean(v, axis=-1, keepdims=True)


def _colsum(v):
    return jnp.sum(v, axis=0, keepdims=True)


def _sigmoid(v):
    return jax.nn.sigmoid(v)


def _exchange(items, name):
    n = len(items)
    arrs = [a for a, _ in items]
    kinds = [k for _, k in items]
    out_shapes = []
    for a, k in items:
        shp = (NDEV,) + tuple(a.shape) if k == "gather" else tuple(a.shape)
        out_shapes.append(SDS(shp, a.dtype))

    def body(*refs):
        ins = refs[:n]
        outs = refs[n:2 * n]
        send_sems, recv_sems, local_sems = refs[2 * n:]
        x, y, c = lax.axis_index("x"), lax.axis_index("y"), lax.axis_index("c")
        me = 4 * x + 2 * y + c

        def peer(k):
            px = 1 - x if k & 4 else x
            py = 1 - y if k & 2 else y
            pc = 1 - c if k & 1 else c
            return (px, py, pc), 4 * px + 2 * py + pc

        local = []
        for i in range(n):
            src = ins[i] if kinds[i] == "gather" else ins[i].at[me]
            cp = pltpu.make_async_copy(src, outs[i].at[me], local_sems.at[i])
            cp.start()
            local.append(cp)
        sends = []
        for k in range(1, NDEV):
            dev, pid = peer(k)
            for i in range(n):
                src = ins[i] if kinds[i] == "gather" else ins[i].at[pid]
                cp = pltpu.make_async_remote_copy(
                    src_ref=src, dst_ref=outs[i].at[me],
                    send_sem=send_sems.at[i, k - 1], recv_sem=recv_sems.at[i, k - 1],
                    device_id=dev, device_id_type=pl.DeviceIdType.MESH)
                cp.start()
                sends.append(cp)
        for k in range(1, NDEV):
            dev, pid = peer(k)
            for i in range(n):
                src = ins[i] if kinds[i] == "gather" else ins[i].at[pid]
                pltpu.make_async_remote_copy(
                    src_ref=src, dst_ref=outs[i].at[pid],
                    send_sem=send_sems.at[i, k - 1], recv_sem=recv_sems.at[i, k - 1],
                    device_id=dev, device_id_type=pl.DeviceIdType.MESH).wait_recv()
        for cp in sends:
            cp.wait_send()
        for cp in local:
            cp.wait()

    any_spec = pl.BlockSpec(memory_space=pl.ANY)
    return pl.pallas_call(
        body, name=name,
        out_shape=tuple(out_shapes),
        in_specs=[any_spec] * n,
        out_specs=tuple([any_spec] * n),
        scratch_shapes=[pltpu.SemaphoreType.DMA((n, NDEV - 1)),
                        pltpu.SemaphoreType.DMA((n, NDEV - 1)),
                        pltpu.SemaphoreType.DMA((n,))],
        compiler_params=pltpu.CompilerParams(has_side_effects=True),
    )(*arrs)


_HBM = pl.BlockSpec(memory_space=pltpu.HBM)
_SEM = pl.BlockSpec(memory_space=pltpu.SEMAPHORE)
_EFFECT = pltpu.SideEffectType.DATAFLOW_SIDE_EFFECTING


def _peer_of(k):
    x, y, c = lax.axis_index("x"), lax.axis_index("y"), lax.axis_index("c")
    px = 1 - x if k & 4 else x
    py = 1 - y if k & 2 else y
    pc = 1 - c if k & 1 else c
    return (px, py, pc), 4 * px + 2 * py + pc


def _split_copies(kinds, ins, lands, send_sems, recv_sems, started):
    me = 4 * lax.axis_index("x") + 2 * lax.axis_index("y") + lax.axis_index("c")
    out = []
    for k in range(1, NDEV):
        dev, pid = _peer_of(k)
        for i, kind in enumerate(kinds):
            src = ins[i] if kind == "gather" else ins[i].at[pid]
            dst = lands[i].at[me] if started else lands[i].at[pid]
            slot = i * (NDEV - 1) + k - 1
            out.append(pltpu.make_async_remote_copy(
                src_ref=src, dst_ref=dst, send_sem=send_sems.at[slot], recv_sem=recv_sems.at[slot],
                device_id=dev, device_id_type=pl.DeviceIdType.MESH))
    return out


def _exchange_start(items, name):
    n = len(items)
    kinds = [k for _, k in items]
    srcs = [pltpu.with_memory_space_constraint(a, pltpu.HBM) for a, _ in items]
    lands = []
    for a, k in items:
        shp = (NDEV,) + tuple(a.shape) if k == "gather" else tuple(a.shape)
        lands.append(pltpu.with_memory_space_constraint(lax.empty(shp, a.dtype), pltpu.HBM))

    def body(*refs):
        ins, land_refs = refs[:n], refs[n:2 * n]
        send_sems, recv_sems = refs[2 * n], refs[2 * n + 1]
        token = refs[-1]
        for cp in _split_copies(kinds, ins, land_refs, send_sems, recv_sems, True):
            cp.start()
        token[...] = jnp.zeros_like(token)

    sems = pltpu.SemaphoreType.DMA((n * (NDEV - 1),))
    res = pl.pallas_call(
        body, name=name,
        out_shape=(sems, sems, *[pltpu.HBM(a.shape, a.dtype) for a in srcs + lands], SDS((8, 128), F32)),
        in_specs=[_HBM] * (2 * n),
        out_specs=(_SEM, _SEM, *[_HBM] * (2 * n), pl.BlockSpec(memory_space=pltpu.VMEM)),
        input_output_aliases={i: 2 + i for i in range(2 * n)},
        compiler_params=pltpu.CompilerParams(has_side_effects=_EFFECT),
    )(*srcs, *lands)
    return (kinds, res[0], res[1], list(res[2:2 + n]), list(res[2 + n:2 + 2 * n])), res[-1][0, 0]


def _exchange_wait(state, after, name):
    kinds, send_sems, recv_sems, srcs, lands = state
    n = len(kinds)

    def body(*refs):
        ins, land_refs = refs[:n], refs[n:2 * n]
        s_sems, r_sems = refs[2 * n], refs[2 * n + 1]
        for cp in _split_copies(kinds, ins, land_refs, s_sems, r_sems, False):
            cp.wait_send()
            cp.wait_recv()

    res = pl.pallas_call(
        body, name=name,
        out_shape=tuple(pltpu.HBM(a.shape, a.dtype) for a in srcs + lands),
        in_specs=[_HBM] * (2 * n) + [_SEM, _SEM, pl.BlockSpec(memory_space=pl.ANY)],
        out_specs=tuple([_HBM] * (2 * n)),
        input_output_aliases={i: i for i in range(2 * n)},
        compiler_params=pltpu.CompilerParams(has_side_effects=_EFFECT),
    )(*srcs, *lands, send_sems, recv_sems, after)
    return list(res[n:])


def _own_slot(land, own, me):
    return lax.dynamic_update_slice_in_dim(land, own[None].astype(land.dtype), me, axis=0)


def _mm_nn_bias(a, b, bias, col_blk0, nblk, tn, out_dtype, name):
    m_, k_ = a.shape

    def body(a_ref, b_ref, bias_ref, o_ref):
        acc = _dot(a_ref[...].astype(BF16), b_ref[...])
        o_ref[...] = (acc + bias_ref[...]).astype(o_ref.dtype)

    return pl.pallas_call(
        body, name=name, grid=(nblk, m_ // TM),
        in_specs=[pl.BlockSpec((TM, k_), lambda n, m: (m, 0)),
                  pl.BlockSpec((k_, tn), lambda n, m: (0, col_blk0 + n)),
                  pl.BlockSpec((1, tn), lambda n, m: (0, col_blk0 + n))],
        out_specs=pl.BlockSpec((TM, tn), lambda n, m: (m, n)),
        out_shape=SDS((m_, nblk * tn), out_dtype),
    )(a, b, bias)


def _mm_nt(a, b, res, res_scale, name):
    m_, k_ = a.shape
    n_ = b.shape[0]
    has_res = res is not None

    def body(*refs):
        if has_res:
            a_ref, b_ref, r_ref, o_ref = refs
        else:
            a_ref, b_ref, o_ref = refs
        acc = _dot_nt(a_ref[...].astype(BF16), b_ref[...].astype(BF16))
        if has_res:
            acc = acc + res_scale * r_ref[...]
        o_ref[...] = acc

    in_specs = [pl.BlockSpec((TM, k_), lambda m: (m, 0)), pl.BlockSpec((n_, k_), lambda m: (0, 0))]
    args = [a, b]
    if has_res:
        in_specs.append(pl.BlockSpec((TM, n_), lambda m: (m, 0)))
        args.append(res)
    return pl.pallas_call(
        body, name=name, grid=(m_ // TM,), in_specs=in_specs,
        out_specs=pl.BlockSpec((TM, n_), lambda m: (m, 0)),
        out_shape=SDS((m_, n_), F32),
    )(*args)


def _mm_tn(a, b, tn, tk, name):
    t_, na = a.shape
    nb = b.shape[1]
    nk = t_ // tk

    def body(a_ref, b_ref, o_ref, acc):
        k = pl.program_id(1)

        @pl.when(k == 0)
        def _():
            acc[...] = jnp.zeros_like(acc)

        acc[...] += _dot_tn(a_ref[...].astype(BF16), b_ref[...].astype(BF16))

        @pl.when(k == nk - 1)
        def _():
            o_ref[...] = acc[...].astype(o_ref.dtype)

    return pl.pallas_call(
        body, name=name, grid=(na // tn, nk),
        in_specs=[pl.BlockSpec((tk, tn), lambda n, k: (k, n)),
                  pl.BlockSpec((tk, nb), lambda n, k: (k, 0))],
        out_specs=pl.BlockSpec((tn, nb), lambda n, k: (n, 0)),
        out_shape=SDS((na, nb), GRAD_WIRE),
        scratch_shapes=[pltpu.VMEM((tn, nb), F32)],
    )(a, b)


def _bucket_maps():
    qi = np.arange(BLK)[:, None]
    kj = np.arange(2 * BLK)[None, :]
    steps = np.maximum(qi + BLK - kj, 0)
    exact = NBUCKET // 2
    maps = []
    for _, dil in BRANCHES:
        dist = steps * dil
        d_f = np.maximum(dist, 1).astype(np.float32)
        large = exact + (np.log(d_f / np.float32(exact)) / np.float32(math.log(S / exact))
                         * np.float32(NBUCKET - exact)).astype(np.int32)
        large = np.minimum(large, NBUCKET - 1)
        maps.append(np.where(dist < exact, dist, large).astype(np.int32))
    return np.stack(maps)


def _bias_table(rel_table, buckets):
    def body(t_ref, b_ref, o_ref):
        bk = b_ref[0]
        for h in range(NH):
            acc = jnp.zeros((BLK, 2 * BLK), F32)
            for k in range(NBUCKET):
                acc = jnp.where(bk == k, t_ref[k, h], acc)
            o_ref[0, h] = acc

    return pl.pallas_call(
        body, name="bias_table", grid=(len(BRANCHES),),
        in_specs=[pl.BlockSpec(memory_space=pltpu.SMEM),
                  pl.BlockSpec((1, BLK, 2 * BLK), lambda i: (i, 0, 0))],
        out_specs=pl.BlockSpec((1, NH, BLK, 2 * BLK), lambda i: (i, 0, 0, 0)),
        out_shape=SDS((len(BRANCHES), NH, BLK, 2 * BLK), F32),
    )(rel_table, buckets)


def _rel_table_grad(dbias, buckets):
    def body(d_ref, b_ref, o_ref):
        h = pl.program_id(0)
        for k in range(NBUCKET):
            tot = jnp.zeros((1, 1), F32)
            for br in range(len(BRANCHES)):
                sel = jnp.where(b_ref[br] == k, d_ref[br, 0], 0.0)
                tot = tot + jnp.sum(jnp.sum(sel, axis=1, keepdims=True), axis=0, keepdims=True)
            o_ref[0, :, pl.ds(k, 1)] = tot

    out = pl.pallas_call(
        body, name="rel_table_grad", grid=(NH,),
        in_specs=[pl.BlockSpec((len(BRANCHES), 1, BLK, 2 * BLK), lambda h: (0, h, 0, 0)),
                  pl.BlockSpec((len(BRANCHES), BLK, 2 * BLK), lambda h: (0, 0, 0))],
        out_specs=pl.BlockSpec((1, 1, NBUCKET), lambda h: (h, 0, 0)),
        out_shape=SDS((NH, 1, NBUCKET), F32),
    )(dbias, buckets)
    return out.reshape(NH, NBUCKET).T


def _block_rows(br, i):
    _, dil = BRANCHES[br]
    nb = S // dil // BLK
    if nb == 16:
        r, nidx = 0, i
    elif nb == 4:
        r, nidx = lax.shift_right_logical(i, 2), lax.bitwise_and(i, 3)
    else:
        r, nidx = i, 0
    start = r + dil * BLK * nidx
    if nb == 1:
        return start, None, None
    prev = r + dil * BLK * jnp.maximum(nidx - 1, 0)
    return start, prev, nidx > 0


def _rows(start, dil):
    if dil == 1:
        return pl.ds(pl.multiple_of(start, BLK), BLK)
    return pl.ds(start, BLK, stride=dil)


def _attn_masks():
    lane = lax.broadcasted_iota(jnp.int32, (BLK, BLK), 1)
    qi = lax.broadcasted_iota(jnp.int32, (BLK, BLK), 0)
    head0 = lane < HD
    valid_cur = lane <= qi
    valid_prev = lane >= qi
    return head0, valid_cur, valid_prev


def _attn_fwd_v1(qkv, bias):
    scale = 1.0 / math.sqrt(HD)
    nbr = len(BRANCHES)

    def body(q_ref, k_ref, v_ref, bias_ref, o_ref, lse_ref, qf, kf, vf, ob, mb, lb):
        qf[...] = q_ref[...].astype(F32)
        kf[...] = k_ref[...].astype(F32)
        vf[...] = v_ref[...].astype(F32)
        head0, valid_cur, valid_prev = _attn_masks()

        for br in range(nbr):
            dil = BRANCHES[br][1]

            def blk(i, carry, br=br, dil=dil):
                start, prev, has_prev = _block_rows(br, i)
                rows = _rows(start, dil)
                q = qf[rows, :]
                kc = kf[rows, :].astype(BF16)
                vc = vf[rows, :].astype(BF16)
                if prev is not None:
                    prows = _rows(prev, dil)
                    kp = kf[prows, :].astype(BF16)
                    vp = vf[prows, :].astype(BF16)
                    ok_prev = jnp.logical_and(valid_prev, has_prev)
                o_acc = jnp.zeros((BLK, BLK), F32)
                m_acc = jnp.zeros((BLK, BLK), F32)
                l_acc = jnp.zeros((BLK, BLK), F32)
                for j in range(2):
                    mj = head0 if j == 0 else jnp.logical_not(head0)
                    qj = jnp.where(mj, q, 0.0).astype(BF16)
                    sc = _dot_nt(qj, kc) * scale + bias_ref[br, j, :, BLK:]
                    sc = jnp.where(valid_cur, sc, NEG_INF)
                    mx = jnp.max(sc, axis=-1, keepdims=True)
                    if prev is not None:
                        sp = _dot_nt(qj, kp) * scale + bias_ref[br, j, :, :BLK]
                        sp = jnp.where(ok_prev, sp, NEG_INF)
                        mx = jnp.maximum(mx, jnp.max(sp, axis=-1, keepdims=True))
                    pc = jnp.exp(sc - mx)
                    ls = jnp.sum(pc, axis=-1, keepdims=True)
                    o = _dot(pc.astype(BF16), vc)
                    if prev is not None:
                        pp = jnp.exp(sp - mx)
                        ls = ls + jnp.sum(pp, axis=-1, keepdims=True)
                        o = o + _dot(pp.astype(BF16), vp)
                    o_acc = jnp.where(mj, o, o_acc)
                    m_acc = jnp.where(mj, mx, m_acc)
                    l_acc = jnp.where(mj, ls, l_acc)
                ob[br, rows, :] = o_acc
                mb[br, rows, :] = m_acc
                lb[br, rows, :] = l_acc
                return carry

            lax.fori_loop(0, 16, blk, 0)

        def merge(i, carry):
            rows = pl.ds(pl.multiple_of(i * 256, 256), 256)
            m_all = jnp.maximum(jnp.maximum(mb[0, rows, :], mb[1, rows, :]), mb[2, rows, :])
            num = jnp.zeros((256, BLK), F32)
            den = jnp.zeros((256, BLK), F32)
            for br in range(nbr):
                c = jnp.exp(mb[br, rows, :] - m_all)
                num = num + ob[br, rows, :] * c
                den = den + lb[br, rows, :] * c
            o_ref[rows, :] = num / den
            lse_ref[rows, :] = m_all + jnp.log(den)
            return carry

        lax.fori_loop(0, S // 256, merge, 0)

    npair = NH // 2
    blk_spec = lambda off: pl.BlockSpec((S, BLK), lambda b, hp: (b, off + hp))
    return pl.pallas_call(
        body, name="attn_fwd", grid=(BL, npair),
        in_specs=[blk_spec(0), blk_spec(npair), blk_spec(2 * npair),
                  pl.BlockSpec((nbr, 2, BLK, 2 * BLK), lambda b, hp: (0, hp, 0, 0))],
        out_specs=(blk_spec(0), blk_spec(0)),
        out_shape=(SDS((T, AW), F32), SDS((T, AW), F32)),
        scratch_shapes=[pltpu.VMEM((S, BLK), F32)] * 3 + [pltpu.VMEM((nbr, S, BLK), F32)] * 3,
    )(qkv, qkv, qkv, bias)


def _attn_bwd_v1(qkv, attn, lse, dattn, bias):
    scale = 1.0 / math.sqrt(HD)
    nbr = len(BRANCHES)

    def body(q_ref, k_ref, v_ref, o_ref, lse_ref, do_ref, bias_ref,
             dq_ref, dk_ref, dv_ref, sq_ref, sk_ref, sv_ref, db_ref,
             qf, kf, vf, dl, dqa, dka, dva):
        b = pl.program_id(1)
        qf[...] = q_ref[...].astype(F32)
        kf[...] = k_ref[...].astype(F32)
        vf[...] = v_ref[...].astype(F32)
        dqa[...] = jnp.zeros_like(dqa)
        dka[...] = jnp.zeros_like(dka)
        dva[...] = jnp.zeros_like(dva)
        head0, valid_cur, valid_prev = _attn_masks()

        @pl.when(b == 0)
        def _():
            db_ref[...] = jnp.zeros_like(db_ref)
            sq_ref[...] = jnp.zeros_like(sq_ref)
            sk_ref[...] = jnp.zeros_like(sk_ref)
            sv_ref[...] = jnp.zeros_like(sv_ref)

        def delta(i, carry):
            rows = pl.ds(pl.multiple_of(i * 256, 256), 256)
            prod = do_ref[rows, :] * o_ref[rows, :]
            h0 = lax.broadcasted_iota(jnp.int32, (256, BLK), 1) < HD
            d0 = jnp.sum(jnp.where(h0, prod, 0.0), axis=-1, keepdims=True)
            d1 = jnp.sum(jnp.where(h0, 0.0, prod), axis=-1, keepdims=True)
            dl[rows, :] = jnp.where(h0, d0, d1)
            return carry

        lax.fori_loop(0, S // 256, delta, 0)

        for br in range(nbr):
            dil = BRANCHES[br][1]

            def blk(i, carry, br=br, dil=dil):
                start, prev, has_prev = _block_rows(br, i)
                rows = _rows(start, dil)
                q = qf[rows, :]
                kc = kf[rows, :].astype(BF16)
                vc = vf[rows, :].astype(BF16)
                do = do_ref[rows, :]
                lse_b = lse_ref[rows, :]
                dl_b = dl[rows, :]
                if prev is not None:
                    prows = _rows(prev, dil)
                    kp = kf[prows, :].astype(BF16)
                    vp = vf[prows, :].astype(BF16)
                    ok_prev = jnp.logical_and(valid_prev, has_prev)
                    dk_p = jnp.zeros((BLK, BLK), F32)
                    dv_p = jnp.zeros((BLK, BLK), F32)
                dq = jnp.zeros((BLK, BLK), F32)
                dk_c = jnp.zeros((BLK, BLK), F32)
                dv_c = jnp.zeros((BLK, BLK), F32)
                for j in range(2):
                    mj = head0 if j == 0 else jnp.logical_not(head0)
                    qj = jnp.where(mj, q, 0.0).astype(BF16)
                    doj = jnp.where(mj, do, 0.0).astype(BF16)
                    lse_j = lse_b[:, j * HD:j * HD + 1]
                    dl_j = dl_b[:, j * HD:j * HD + 1]
                    sc = _dot_nt(qj, kc) * scale + bias_ref[br, j, :, BLK:]
                    pc = jnp.where(valid_cur, jnp.exp(sc - lse_j), 0.0)
                    ds_c = pc * (_dot_nt(doj, vc) - dl_j)
                    db_ref[br, j, :, BLK:] += ds_c
                    dsb = (ds_c * scale).astype(BF16)
                    dqj = _dot(dsb, kc)
                    dk_c = dk_c + _dot_tn(dsb, qj)
                    dv_c = dv_c + _dot_tn(pc.astype(BF16), doj)
                    if prev is not None:
                        sp = _dot_nt(qj, kp) * scale + bias_ref[br, j, :, :BLK]
                        pp = jnp.where(ok_prev, jnp.exp(sp - lse_j), 0.0)
                        ds_p = pp * (_dot_nt(doj, vp) - dl_j)
                        db_ref[br, j, :, :BLK] += ds_p
                        dsbp = (ds_p * scale).astype(BF16)
                        dqj = dqj + _dot(dsbp, kp)
                        dk_p = dk_p + _dot_tn(dsbp, qj)
                        dv_p = dv_p + _dot_tn(pp.astype(BF16), doj)
                    dq = jnp.where(mj, dqj, dq)
                dqa[rows, :] = dqa[rows, :] + dq
                dka[rows, :] = dka[rows, :] + dk_c
                dva[rows, :] = dva[rows, :] + dv_c
                if prev is not None:
                    dka[prows, :] = dka[prows, :] + dk_p
                    dva[prows, :] = dva[prows, :] + dv_p
                return carry

            lax.fori_loop(0, 16, blk, 0)

        def flush(i, carry):
            rows = pl.ds(pl.multiple_of(i * 256, 256), 256)
            for acc, out, cs in ((dqa, dq_ref, sq_ref), (dka, dk_ref, sk_ref), (dva, dv_ref, sv_ref)):
                val = acc[rows, :]
                out[rows, :] = val.astype(BF16)
                cs[...] += _colsum(val)
            return carry

        lax.fori_loop(0, S // 256, flush, 0)

    npair = NH // 2
    blk_spec = lambda off: pl.BlockSpec((S, BLK), lambda hp, b: (b, off + hp))
    sum_spec = pl.BlockSpec((1, BLK), lambda hp, b: (0, hp))
    return pl.pallas_call(
        body, name="attn_bwd", grid=(npair, BL),
        in_specs=[blk_spec(0), blk_spec(npair), blk_spec(2 * npair), blk_spec(0), blk_spec(0), blk_spec(0),
                  pl.BlockSpec((nbr, 2, BLK, 2 * BLK), lambda hp, b: (0, hp, 0, 0))],
        out_specs=(blk_spec(0), blk_spec(0), blk_spec(0), sum_spec, sum_spec, sum_spec,
                   pl.BlockSpec((nbr, 2, BLK, 2 * BLK), lambda hp, b: (0, hp, 0, 0))),
        out_shape=(SDS((T, AW), BF16), SDS((T, AW), BF16), SDS((T, AW), BF16),
                   SDS((1, AW), F32), SDS((1, AW), F32), SDS((1, AW), F32),
                   SDS((nbr, NH, BLK, 2 * BLK), F32)),
        scratch_shapes=[pltpu.VMEM((S, BLK), F32)] * 7,
    )(qkv, qkv, qkv, attn, lse, dattn, bias)


PADK = BLK
SCALE = 1.0 / math.sqrt(HD)
ATTN_UNROLL = 8


def _branch_geometry(br):
    dil = BRANCHES[br][1]
    sub = S // dil
    return dil, sub, sub // BLK


def _token_rows(br, i):
    dil, _, nblk = _branch_geometry(br)
    if dil == 1:
        return pl.ds(pl.multiple_of(i * BLK, BLK), BLK), i
    r = lax.shift_right_logical(i, nblk.bit_length() - 1)
    n = lax.bitwise_and(i, nblk - 1)
    return pl.ds(r + dil * BLK * n, BLK, stride=dil), n


def _sub_layout_loop(br, step):
    dil, sub, _ = _branch_geometry(br)
    rows = min(sub, 256)
    nchunk = sub // rows

    def it_step(it, carry):
        if dil == 1:
            src = pl.ds(pl.multiple_of(it * rows, rows), rows)
        else:
            r = lax.shift_right_logical(it, nchunk.bit_length() - 1)
            src = pl.ds(r + dil * rows * lax.bitwise_and(it, nchunk - 1), rows, stride=dil)
        step(src, pl.multiple_of(it * rows, BLK), rows)
        return carry

    lax.fori_loop(0, dil * nchunk, it_step, 0)


def _masked_bias(bias_ref, bm):
    qi = lax.broadcasted_iota(jnp.int32, (BLK, 2 * BLK), 0)
    kj = lax.broadcasted_iota(jnp.int32, (BLK, 2 * BLK), 1)
    first = jnp.logical_and(kj >= BLK, kj - BLK <= qi)
    valid = jnp.logical_or(first, jnp.logical_and(kj < BLK, kj >= qi))
    for br in range(len(BRANCHES)):
        for j in range(2):
            b = bias_ref[br, j]
            bm[br, 1, pl.ds(j * BLK, BLK), :] = jnp.where(valid, b, NEG_INF)
            bm[br, 0, pl.ds(j * BLK, BLK), :] = jnp.where(first, b, NEG_INF)


def _head_split(fn):
    def split(t):
        h0 = lax.broadcasted_iota(jnp.int32, t.shape, 1) < HD
        t = fn(t)
        return jnp.where(h0, t, 0.0).astype(BF16), jnp.where(h0, 0.0, t).astype(BF16)
    return split


def _attn_fwd(qkv, bias):
    nbr = len(BRANCHES)

    def body(q_ref, k_ref, v_ref, bias_ref, o_ref, lse_ref, qf, kf, vf, qs0, qs1, ks, vs, bm, ob, mb, lb):
        qf[...] = q_ref[...].astype(F32)
        kf[...] = k_ref[...].astype(F32)
        vf[...] = v_ref[...].astype(F32)
        _masked_bias(bias_ref, bm)
        ks[pl.ds(0, PADK), :] = jnp.zeros((PADK, BLK), BF16)
        vs[pl.ds(0, PADK), :] = jnp.zeros((PADK, BLK), BF16)
        head0 = lax.broadcasted_iota(jnp.int32, (BLK, BLK), 1) < HD
        split_q = _head_split(lambda t: t * SCALE)

        for br in range(nbr):
            nblk = _branch_geometry(br)[2]

            def stage(src, off, rows):
                qs0[pl.ds(off, rows), :], qs1[pl.ds(off, rows), :] = split_q(qf[src, :])
                ks[pl.ds(PADK + off, rows), :] = kf[src, :].astype(BF16)
                vs[pl.ds(PADK + off, rows), :] = vf[src, :].astype(BF16)

            _sub_layout_loop(br, stage)

            def blk(i, carry, br=br, nblk=nblk):
                base = pl.multiple_of(i * BLK, BLK)
                rows, n = _token_rows(br, i)
                q01 = jnp.concatenate([qs0[pl.ds(base, BLK), :], qs1[pl.ds(base, BLK), :]], axis=0)
                if nblk > 1:
                    kcat = ks[pl.ds(base, 2 * BLK), :]
                    vcat = vs[pl.ds(base, 2 * BLK), :]
                    s = _dot_nt(q01, kcat) + bm[br, jnp.minimum(n, 1)]
                else:
                    kcat = ks[pl.ds(PADK + base, BLK), :]
                    vcat = vs[pl.ds(PADK + base, BLK), :]
                    s = _dot_nt(q01, kcat) + bm[br, 0, :, BLK:]
                mx = jnp.max(s, axis=-1, keepdims=True)
                p = jnp.exp(s - mx)
                ls = jnp.sum(p, axis=-1, keepdims=True)
                o = _dot(p.astype(BF16), vcat)
                ob[br, rows, :] = jnp.where(head0, o[:BLK], o[BLK:])
                mb[br, rows, :] = jnp.where(head0, mx[:BLK], mx[BLK:])
                lb[br, rows, :] = jnp.where(head0, ls[:BLK], ls[BLK:])
                return carry

            lax.fori_loop(0, 16, blk, 0, unroll=ATTN_UNROLL)

        def merge(i, carry):
            rows = pl.ds(pl.multiple_of(i * 256, 256), 256)
            m_all = jnp.maximum(jnp.maximum(mb[0, rows, :], mb[1, rows, :]), mb[2, rows, :])
            num = jnp.zeros((256, BLK), F32)
            den = jnp.zeros((256, BLK), F32)
            for br in range(nbr):
                c = jnp.exp(mb[br, rows, :] - m_all)
                num = num + ob[br, rows, :] * c
                den = den + lb[br, rows, :] * c
            o_ref[rows, :] = num / den
            lse_ref[rows, :] = m_all + jnp.log(den)
            return carry

        lax.fori_loop(0, S // 256, merge, 0)

    npair = NH // 2
    blk_spec = lambda off: pl.BlockSpec((S, BLK), lambda b, hp: (b, off + hp))
    return pl.pallas_call(
        body, name="attn_fwd", grid=(BL, npair),
        in_specs=[blk_spec(0), blk_spec(npair), blk_spec(2 * npair),
                  pl.BlockSpec((nbr, 2, BLK, 2 * BLK), lambda b, hp: (0, hp, 0, 0))],
        out_specs=(blk_spec(0), blk_spec(0)),
        out_shape=(SDS((T, AW), F32), SDS((T, AW), F32)),
        scratch_shapes=[pltpu.VMEM((S, BLK), F32)] * 3 + [pltpu.VMEM((S, BLK), BF16)] * 2
        + [pltpu.VMEM((PADK + S, BLK), BF16)] * 2 + [pltpu.VMEM((nbr, 2, 2 * BLK, 2 * BLK), F32)]
        + [pltpu.VMEM((nbr, S, BLK), F32)] * 3,
    )(qkv, qkv, qkv, bias)


def _attn_bwd(qkv, attn, lse, dattn, bias):
    nbr = len(BRANCHES)

    def body(q_ref, k_ref, v_ref, o_ref, lse_ref, do_ref, bias_ref,
             dq_ref, dk_ref, dv_ref, sq_ref, sk_ref, sv_ref, db_ref,
             qf, kf, vf, dl, dqa, dka, dva, qs0, qs1, ds0, ds1, ks, vs, dks, dvs, bm):
        b = pl.program_id(1)
        qf[...] = q_ref[...].astype(F32)
        kf[...] = k_ref[...].astype(F32)
        vf[...] = v_ref[...].astype(F32)
        dqa[...] = jnp.zeros_like(dqa)
        dka[...] = jnp.zeros_like(dka)
        dva[...] = jnp.zeros_like(dva)
        _masked_bias(bias_ref, bm)
        ks[pl.ds(0, PADK), :] = jnp.zeros((PADK, BLK), BF16)
        vs[pl.ds(0, PADK), :] = jnp.zeros((PADK, BLK), BF16)
        head0 = lax.broadcasted_iota(jnp.int32, (BLK, BLK), 1) < HD
        split_q = _head_split(lambda t: t * SCALE)
        split_do = _head_split(lambda t: t)

        @pl.when(b == 0)
        def _():
            db_ref[...] = jnp.zeros_like(db_ref)
            sq_ref[...] = jnp.zeros_like(sq_ref)
            sk_ref[...] = jnp.zeros_like(sk_ref)
            sv_ref[...] = jnp.zeros_like(sv_ref)

        def delta(i, carry):
            rows = pl.ds(pl.multiple_of(i * 256, 256), 256)
            prod = do_ref[rows, :] * o_ref[rows, :]
            h0 = lax.broadcasted_iota(jnp.int32, (256, BLK), 1) < HD
            d0 = jnp.sum(jnp.where(h0, prod, 0.0), axis=-1, keepdims=True)
            d1 = jnp.sum(jnp.where(h0, 0.0, prod), axis=-1, keepdims=True)
            dl[rows, :] = jnp.where(h0, d0, d1)
            return carry

        lax.fori_loop(0, S // 256, delta, 0)

        for br in range(nbr):
            nblk = _branch_geometry(br)[2]

            def stage(src, off, rows):
                qs0[pl.ds(off, rows), :], qs1[pl.ds(off, rows), :] = split_q(qf[src, :])
                ds0[pl.ds(off, rows), :], ds1[pl.ds(off, rows), :] = split_do(do_ref[src, :])
                ks[pl.ds(PADK + off, rows), :] = kf[src, :].astype(BF16)
                vs[pl.ds(PADK + off, rows), :] = vf[src, :].astype(BF16)

            _sub_layout_loop(br, stage)
            dks[...] = jnp.zeros_like(dks)
            dvs[...] = jnp.zeros_like(dvs)

            def blk(i, carry, br=br, nblk=nblk):
                base = pl.multiple_of(i * BLK, BLK)
                rows, n = _token_rows(br, i)
                q01 = jnp.concatenate([qs0[pl.ds(base, BLK), :], qs1[pl.ds(base, BLK), :]], axis=0)
                do01 = jnp.concatenate([ds0[pl.ds(base, BLK), :], ds1[pl.ds(base, BLK), :]], axis=0)
                lse_b = lse_ref[rows, :]
                dl_b = dl[rows, :]
                lse01 = jnp.concatenate([lse_b[:, 0:1], lse_b[:, HD:HD + 1]], axis=0)
                dl01 = jnp.concatenate([dl_b[:, 0:1], dl_b[:, HD:HD + 1]], axis=0)
                if nblk > 1:
                    krows = pl.ds(base, 2 * BLK)
                    bias_m = bm[br, jnp.minimum(n, 1)]
                else:
                    krows = pl.ds(PADK + base, BLK)
                    bias_m = bm[br, 0, :, BLK:]
                kcat = ks[krows, :]
                vcat = vs[krows, :]
                p = jnp.exp(_dot_nt(q01, kcat) + bias_m - lse01)
                dsv = p * (_dot_nt(do01, vcat) - dl01)
                if nblk > 1:
                    db_ref[br, 0] += dsv[:BLK]
                    db_ref[br, 1] += dsv[BLK:]
                else:
                    db_ref[br, 0, :, BLK:] += dsv[:BLK]
                    db_ref[br, 1, :, BLK:] += dsv[BLK:]
                dsb = dsv.astype(BF16)
                dq01 = _dot(dsb, kcat)
                dqa[rows, :] = dqa[rows, :] + jnp.where(head0, dq01[:BLK], dq01[BLK:])
                dks[krows, :] = dks[krows, :] + _dot_tn(dsb, q01)
                dvs[krows, :] = dvs[krows, :] + _dot_tn(p.astype(BF16), do01)
                return carry

            lax.fori_loop(0, 16, blk, 0, unroll=ATTN_UNROLL)

            def fold(src, off, rows):
                dka[src, :] = dka[src, :] + dks[pl.ds(PADK + off, rows), :]
                dva[src, :] = dva[src, :] + dvs[pl.ds(PADK + off, rows), :]

            _sub_layout_loop(br, fold)

        def flush(i, carry):
            rows = pl.ds(pl.multiple_of(i * 256, 256), 256)
            for acc, out, cs, mul in ((dqa, dq_ref, sq_ref, SCALE), (dka, dk_ref, sk_ref, 1.0), (dva, dv_ref, sv_ref, 1.0)):
                val = acc[rows, :] * mul
                out[rows, :] = val.astype(BF16)
                cs[...] += _colsum(val)
            return carry

        lax.fori_loop(0, S // 256, flush, 0)

    npair = NH // 2
    blk_spec = lambda off: pl.BlockSpec((S, BLK), lambda hp, b: (b, off + hp))
    sum_spec = pl.BlockSpec((1, BLK), lambda hp, b: (0, hp))
    return pl.pallas_call(
        body, name="attn_bwd", grid=(npair, BL),
        in_specs=[blk_spec(0), blk_spec(npair), blk_spec(2 * npair), blk_spec(0), blk_spec(0), blk_spec(0),
                  pl.BlockSpec((nbr, 2, BLK, 2 * BLK), lambda hp, b: (0, hp, 0, 0))],
        out_specs=(blk_spec(0), blk_spec(0), blk_spec(0), sum_spec, sum_spec, sum_spec,
                   pl.BlockSpec((nbr, 2, BLK, 2 * BLK), lambda hp, b: (0, hp, 0, 0))),
        out_shape=(SDS((T, AW), BF16), SDS((T, AW), BF16), SDS((T, AW), BF16),
                   SDS((1, AW), F32), SDS((1, AW), F32), SDS((1, AW), F32),
                   SDS((nbr, NH, BLK, 2 * BLK), F32)),
        scratch_shapes=[pltpu.VMEM((S, BLK), F32)] * 7 + [pltpu.VMEM((S, BLK), BF16)] * 4
        + [pltpu.VMEM((PADK + S, BLK), BF16)] * 2 + [pltpu.VMEM((PADK + S, BLK), F32)] * 2
        + [pltpu.VMEM((nbr, 2, 2 * BLK, 2 * BLK), F32)],
    )(qkv, qkv, qkv, attn, lse, dattn, bias)


CH = 256
PADR = 32


def _conv_fwd(ag, conv_w, conv_b):
    def body(ag_ref, w_ref, b_ref, u1_ref, u0p):
        u0p[pl.ds(0, PADR), :] = jnp.zeros((PADR, CW), F32)

        def glu(i, carry):
            t0 = pl.multiple_of(i * CH, CH)
            a = ag_ref[pl.ds(t0, CH), :CW]
            g = ag_ref[pl.ds(t0, CH), CW:]
            u0p[pl.ds(PADR + t0, CH), :] = a * _sigmoid(g)
            return carry

        lax.fori_loop(0, S // CH, glu, 0)

        def conv(i, carry):
            t0 = pl.multiple_of(i * CH, CH)
            win = u0p[pl.ds(t0, CH + PADR), :]
            acc = jnp.zeros((CH, CW), F32) + b_ref[...]
            for k in range(CK):
                off = PADR - (CK - 1) + k
                acc = acc + win[off:off + CH, :] * w_ref[k:k + 1, :]
            u1_ref[pl.ds(t0, CH), :] = acc
            return carry

        lax.fori_loop(0, S // CH, conv, 0)

    return pl.pallas_call(
        body, name="conv_fwd", grid=(BL,),
        in_specs=[pl.BlockSpec((S, 2 * CW), lambda b: (b, 0)),
                  pl.BlockSpec((CK, CW), lambda b: (0, 0)),
                  pl.BlockSpec((1, CW), lambda b: (0, 0))],
        out_specs=pl.BlockSpec((S, CW), lambda b: (b, 0)),
        out_shape=SDS((T, CW), F32),
        scratch_shapes=[pltpu.VMEM((S + PADR, CW), F32)],
    )(ag, conv_w, conv_b)


def _conv_post(u1, cg, cb):
    mu = _rowmean(u1)
    uc = u1 - mu
    rstd = lax.rsqrt(_rowmean(uc * uc) + LN_EPS)
    xh = uc * rstd
    u2 = xh * cg + cb
    sg = _sigmoid(u2)
    return xh, rstd, u2, sg, u2 * sg


def _mix_fwd(attn, u1, ga, gc, cg, cb):
    def body(a_ref, u_ref, ga_ref, gc_ref, cg_ref, cb_ref, o_ref):
        a = a_ref[...]
        ra = lax.rsqrt(_rowmean(a * a) + LN_EPS)
        o_ref[:, :AW] = (a * ra * ga_ref[...]).astype(BF16)
        _, _, _, _, u3 = _conv_post(u_ref[...], cg_ref[...], cb_ref[...])
        rc = lax.rsqrt(_rowmean(u3 * u3) + LN_EPS)
        o_ref[:, AW:] = (u3 * rc * gc_ref[...]).astype(BF16)

    vec = lambda w: pl.BlockSpec((1, w), lambda m: (0, 0))
    return pl.pallas_call(
        body, name="mix_fwd", grid=(T // TM,),
        in_specs=[pl.BlockSpec((TM, AW), lambda m: (m, 0)), pl.BlockSpec((TM, CW), lambda m: (m, 0)),
                  vec(AW), vec(CW), vec(CW), vec(CW)],
        out_specs=pl.BlockSpec((TM, D), lambda m: (m, 0)),
        out_shape=SDS((T, D), BF16),
    )(attn, u1, ga, gc, cg, cb)


def _mix_bwd(dmixed, attn, u1, ga, gc, cg, cb):
    def body(dm_ref, a_ref, u_ref, ga_ref, gc_ref, cg_ref, cb_ref,
             da_ref, du_ref, g_an, g_cn, g_lg, g_lb, g_cb):
        @pl.when(pl.program_id(0) == 0)
        def _():
            for r in (g_an, g_cn, g_lg, g_lb, g_cb):
                r[...] = jnp.zeros_like(r)

        a = a_ref[...]
        dna = dm_ref[:, :AW]
        ra = lax.rsqrt(_rowmean(a * a) + LN_EPS)
        g_an[...] += _colsum(dna * a * ra)
        dat = dna * ga_ref[...]
        da_ref[...] = ra * dat - a * (ra * ra * ra) * _rowmean(dat * a)

        xh, rstd, u2, sg, u3 = _conv_post(u_ref[...], cg_ref[...], cb_ref[...])
        dnc = dm_ref[:, AW:]
        rc = lax.rsqrt(_rowmean(u3 * u3) + LN_EPS)
        g_cn[...] += _colsum(dnc * u3 * rc)
        dut = dnc * gc_ref[...]
        du3 = rc * dut - u3 * (rc * rc * rc) * _rowmean(dut * u3)
        du2 = du3 * sg * (1.0 + u2 * (1.0 - sg))
        g_lg[...] += _colsum(du2 * xh)
        g_lb[...] += _colsum(du2)
        dxh = du2 * cg_ref[...]
        du1 = rstd * (dxh - _rowmean(dxh) - xh * _rowmean(dxh * xh))
        g_cb[...] += _colsum(du1)
        du_ref[...] = du1

    vec = lambda w: pl.BlockSpec((1, w), lambda m: (0, 0))
    return pl.pallas_call(
        body, name="mix_bwd", grid=(T // TM,),
        in_specs=[pl.BlockSpec((TM, D), lambda m: (m, 0)), pl.BlockSpec((TM, AW), lambda m: (m, 0)),
                  pl.BlockSpec((TM, CW), lambda m: (m, 0)), vec(AW), vec(CW), vec(CW), vec(CW)],
        out_specs=(pl.BlockSpec((TM, AW), lambda m: (m, 0)), pl.BlockSpec((TM, CW), lambda m: (m, 0)),
                   vec(AW), vec(CW), vec(CW), vec(CW), vec(CW)),
        out_shape=(SDS((T, AW), F32), SDS((T, CW), F32),
                   SDS((1, AW), F32), SDS((1, CW), F32), SDS((1, CW), F32), SDS((1, CW), F32), SDS((1, CW), F32)),
    )(dmixed, attn, u1, ga, gc, cg, cb)


def _conv_bwd(du1, ag, conv_w):
    def body(du_ref, ag_ref, w_ref, dag_ref, cs_ref, gw_ref, u0p, dup):
        @pl.when(pl.program_id(0) == 0)
        def _():
            cs_ref[...] = jnp.zeros_like(cs_ref)
            gw_ref[...] = jnp.zeros_like(gw_ref)

        u0p[pl.ds(0, PADR), :] = jnp.zeros((PADR, CW), F32)
        dup[pl.ds(S, PADR), :] = jnp.zeros((PADR, CW), F32)

        def fill(i, carry):
            t0 = pl.multiple_of(i * CH, CH)
            a = ag_ref[pl.ds(t0, CH), :CW]
            g = ag_ref[pl.ds(t0, CH), CW:]
            u0p[pl.ds(PADR + t0, CH), :] = a * _sigmoid(g)
            dup[pl.ds(t0, CH), :] = du_ref[pl.ds(t0, CH), :]
            return carry

        lax.fori_loop(0, S // CH, fill, 0)

        def chunk(i, carry):
            t0 = pl.multiple_of(i * CH, CH)
            d = dup[pl.ds(t0, CH), :]
            win_u = u0p[pl.ds(t0, CH + PADR), :]
            win_d = dup[pl.ds(t0, CH + PADR), :]
            du0 = jnp.zeros((CH, CW), F32)
            for k in range(CK):
                off = PADR - (CK - 1) + k
                gw_ref[k:k + 1, :] += _colsum(d * win_u[off:off + CH, :])
                fo = CK - 1 - k
                du0 = du0 + win_d[fo:fo + CH, :] * w_ref[k:k + 1, :]
            a = ag_ref[pl.ds(t0, CH), :CW]
            sg = _sigmoid(ag_ref[pl.ds(t0, CH), CW:])
            da = du0 * sg
            dg = du0 * a * sg * (1.0 - sg)
            dag_ref[pl.ds(t0, CH), :CW] = da.astype(BF16)
            dag_ref[pl.ds(t0, CH), CW:] = dg.astype(BF16)
            cs_ref[:, :CW] += _colsum(da)
            cs_ref[:, CW:] += _colsum(dg)
            return carry

        lax.fori_loop(0, S // CH, chunk, 0)

    return pl.pallas_call(
        body, name="conv_bwd", grid=(BL,),
        in_specs=[pl.BlockSpec((S, CW), lambda b: (b, 0)), pl.BlockSpec((S, 2 * CW), lambda b: (b, 0)),
                  pl.BlockSpec((CK, CW), lambda b: (0, 0))],
        out_specs=(pl.BlockSpec((S, 2 * CW), lambda b: (b, 0)),
                   pl.BlockSpec((1, 2 * CW), lambda b: (0, 0)),
                   pl.BlockSpec((PADR, CW), lambda b: (0, 0))),
        out_shape=(SDS((T, 2 * CW), BF16), SDS((1, 2 * CW), F32), SDS((PADR, CW), F32)),
        scratch_shapes=[pltpu.VMEM((S + PADR, CW), F32), pltpu.VMEM((S + PADR, CW), F32)],
    )(du1, ag, conv_w)


def _layer_norm_fwd(z):
    mu = _rowmean(z)
    zc = z - mu
    rstd = lax.rsqrt(_rowmean(zc * zc) + LN_EPS)
    return zc * rstd, rstd


def _layer_norm_bwd(dy, xh, rstd, g):
    dxh = dy * g
    return rstd * (dxh - _rowmean(dxh) - xh * _rowmean(dxh * xh))


def _out_proj_ln1(mixed, w_out, x2, g1, b1):
    def body(a_ref, w_ref, x_ref, g_ref, b_ref, xh_ref, rstd_ref, x1_ref):
        z = ALPHA * x_ref[...] + _dot(a_ref[...], w_ref[...])
        xh, rstd = _layer_norm_fwd(z)
        xh_ref[...] = xh
        rstd_ref[...] = rstd
        x1_ref[...] = (xh * g_ref[...] + b_ref[...]).astype(BF16)

    vec = pl.BlockSpec((1, D), lambda m: (0, 0))
    row = pl.BlockSpec((TM, D), lambda m: (m, 0))
    return pl.pallas_call(
        body, name="out_proj_ln1", grid=(T // TM,),
        in_specs=[row, pl.BlockSpec((D, D), lambda m: (0, 0)), row, vec, vec],
        out_specs=(row, pl.BlockSpec((TM, 1), lambda m: (m, 0)), row),
        out_shape=(SDS((T, D), F32), SDS((T, 1), F32), SDS((T, D), BF16)),
    )(mixed, w_out, x2, g1, b1)


def _seq_start(m):
    return lax.bitwise_and(m, S // TM - 1) == 0


def _causal3(ext, w_ref, b_ref):
    x0 = ext[pl.ds(8, TM), :]
    x1 = ext[pl.ds(7, TM), :]
    x2 = ext[pl.ds(6, TM), :]
    y = w_ref[2:3, :] * x0 + w_ref[1:2, :] * x1 + w_ref[0:1, :] * x2 + b_ref[...]
    return y, x0, x1, x2


def _ffn_up(x1b, w_up, fcw, fcb):
    def body(x_ref, wg_ref, wv_ref, cwg_ref, cwv_ref, cbg_ref, cbv_ref, up_ref, act_ref, extg, extv):
        @pl.when(_seq_start(pl.program_id(1)))
        def _():
            extg[pl.ds(0, 8), :] = jnp.zeros((8, FT), F32)
            extv[pl.ds(0, 8), :] = jnp.zeros((8, FT), F32)

        x = x_ref[...]
        ug = _dot(x, wg_ref[...]).astype(BF16)
        uv = _dot(x, wv_ref[...]).astype(BF16)
        up_ref[:, :FT] = ug
        up_ref[:, FT:] = uv
        extg[pl.ds(8, TM), :] = ug.astype(F32)
        extv[pl.ds(8, TM), :] = uv.astype(F32)
        gate = _causal3(extg, cwg_ref, cbg_ref)[0]
        val = _causal3(extv, cwv_ref, cbv_ref)[0]
        act_ref[...] = (gate * _sigmoid(gate) * val).astype(BF16)
        extg[pl.ds(0, 8), :] = extg[pl.ds(TM, 8), :]
        extv[pl.ds(0, 8), :] = extv[pl.ds(TM, 8), :]

    wspec = lambda off: pl.BlockSpec((D, FT), lambda n, m: (0, n + off))
    cwspec = lambda off: pl.BlockSpec((FK, FT), lambda n, m: (0, n + off))
    cbspec = lambda off: pl.BlockSpec((1, FT), lambda n, m: (0, n + off))
    return pl.pallas_call(
        body, name="ffn_up", grid=(NFT, T // TM),
        in_specs=[pl.BlockSpec((TM, D), lambda n, m: (m, 0)), wspec(0), wspec(NFT),
                  cwspec(0), cwspec(NFT), cbspec(0), cbspec(NFT)],
        out_specs=(pl.BlockSpec((TM, 2 * FT), lambda n, m: (m, n)), pl.BlockSpec((TM, FT), lambda n, m: (m, n))),
        out_shape=(SDS((T, 2 * DFF), BF16), SDS((T, DFF), BF16)),
        scratch_shapes=[pltpu.VMEM((TM + 8, FT), F32)] * 2,
    )(x1b, w_up, w_up, fcw, fcw, fcb, fcb)


def _ffn_down_loss(act, w_down, xh1, g1, b1, g2, b2, target):
    def body(a_ref, w_ref, xh1_ref, g1_ref, b1_ref, g2_ref, b2_ref, t_ref, dz_ref, loss_ref, gg_ref, gb_ref):
        @pl.when(pl.program_id(0) == 0)
        def _():
            loss_ref[...] = jnp.zeros_like(loss_ref)
            gg_ref[...] = jnp.zeros_like(gg_ref)
            gb_ref[...] = jnp.zeros_like(gb_ref)

        x1 = xh1_ref[...] * g1_ref[...] + b1_ref[...]
        z = ALPHA * x1 + _dot(a_ref[...], w_ref[...])
        xh, rstd = _layer_norm_fwd(z)
        diff = xh * g2_ref[...] + b2_ref[...] - t_ref[...]
        loss_ref[...] += 0.5 * _colsum(_rowmean(diff * diff))
        dout = diff * (1.0 / D)
        gg_ref[...] += _colsum(dout * xh)
        gb_ref[...] += _colsum(dout)
        dz_ref[...] = _layer_norm_bwd(dout, xh, rstd, g2_ref[...])

    vec = pl.BlockSpec((1, D), lambda m: (0, 0))
    row = pl.BlockSpec((TM, D), lambda m: (m, 0))
    return pl.pallas_call(
        body, name="ffn_down_loss", grid=(T // TM,),
        in_specs=[pl.BlockSpec((TM, DFF), lambda m: (m, 0)), pl.BlockSpec((DFF, D), lambda m: (0, 0)),
                  row, vec, vec, vec, vec, row],
        out_specs=(row, pl.BlockSpec((1, 1), lambda m: (0, 0)), vec, vec),
        out_shape=(SDS((T, D), F32), SDS((1, 1), F32), SDS((1, D), F32), SDS((1, D), F32)),
    )(act, w_down, xh1, g1, b1, g2, b2, target)


def _ffn_down_bwd(dz2, w_down, up, fcw, fcb):
    def body(dz_ref, wd_ref, up_ref, cwg_ref, cwv_ref, cbg_ref, cbv_ref,
             dup_ref, csg_ref, csv_ref, gwg_ref, gwv_ref, extg, extv):
        m = pl.program_id(1)

        @pl.when(_seq_start(m))
        def _():
            extg[pl.ds(0, 8), :] = jnp.zeros((8, FT), F32)
            extv[pl.ds(0, 8), :] = jnp.zeros((8, FT), F32)

        @pl.when(m == 0)
        def _():
            for r in (csg_ref, csv_ref, gwg_ref, gwv_ref):
                r[...] = jnp.zeros_like(r)

        dact = _dot_nt(dz_ref[...].astype(BF16), wd_ref[...])
        extg[pl.ds(8, TM), :] = up_ref[:, :FT].astype(F32)
        extv[pl.ds(8, TM), :] = up_ref[:, FT:].astype(F32)
        gate, g0, g1, g2 = _causal3(extg, cwg_ref, cbg_ref)
        val, v0, v1, v2 = _causal3(extv, cwv_ref, cbv_ref)
        sg = _sigmoid(gate)
        dgate = dact * val * sg * (1.0 + gate * (1.0 - sg))
        dval = dact * gate * sg
        dup_ref[:, :FT] = dgate.astype(BF16)
        dup_ref[:, FT:] = dval.astype(BF16)
        csg_ref[...] += _colsum(dgate)
        csv_ref[...] += _colsum(dval)
        for k, (xg, xv) in enumerate(((g2, v2), (g1, v1), (g0, v0))):
            gwg_ref[k:k + 1, :] += _colsum(dgate * xg)
            gwv_ref[k:k + 1, :] += _colsum(dval * xv)
        extg[pl.ds(0, 8), :] = extg[pl.ds(TM, 8), :]
        extv[pl.ds(0, 8), :] = extv[pl.ds(TM, 8), :]

    cwspec = lambda off: pl.BlockSpec((FK, FT), lambda n, m: (0, n + off))
    cbspec = lambda off: pl.BlockSpec((1, FT), lambda n, m: (0, n + off))
    cs = pl.BlockSpec((1, FT), lambda n, m: (0, n))
    gw = pl.BlockSpec((FK, FT), lambda n, m: (0, n))
    return pl.pallas_call(
        body, name="ffn_down_bwd", grid=(NFT, T // TM),
        in_specs=[pl.BlockSpec((TM, D), lambda n, m: (m, 0)), pl.BlockSpec((FT, D), lambda n, m: (n, 0)),
                  pl.BlockSpec((TM, 2 * FT), lambda n, m: (m, n)),
                  cwspec(0), cwspec(NFT), cbspec(0), cbspec(NFT)],
        out_specs=(pl.BlockSpec((TM, 2 * FT), lambda n, m: (m, n)), cs, cs, gw, gw),
        out_shape=(SDS((T, 2 * DFF), BF16), SDS((1, DFF), F32), SDS((1, DFF), F32),
                   SDS((FK, DFF), F32), SDS((FK, DFF), F32)),
        scratch_shapes=[pltpu.VMEM((TM + 8, FT), F32)] * 2,
    )(dz2, w_down, up, fcw, fcw, fcb, fcb)


HALO = 16


def _conv3_transpose(dup, fcw_il):
    tiles = T // TM

    def body(d_ref, h_ref, w_ref, o_ref, ext):
        m = pl.program_id(1)
        ext[pl.ds(0, TM), :] = d_ref[...].astype(F32)
        last = lax.bitwise_and(m + 1, S // TM - 1) == 0
        ext[pl.ds(TM, HALO), :] = jnp.where(last, 0.0, h_ref[...].astype(F32))
        y = (w_ref[2:3, :] * ext[pl.ds(0, TM), :] + w_ref[1:2, :] * ext[pl.ds(1, TM), :]
             + w_ref[0:1, :] * ext[pl.ds(2, TM), :])
        o_ref[...] = y.astype(BF16)

    return pl.pallas_call(
        body, name="conv3_transpose", grid=(NFT, tiles),
        in_specs=[pl.BlockSpec((TM, 2 * FT), lambda n, m: (m, n)),
                  pl.BlockSpec((HALO, 2 * FT), lambda n, m: (jnp.minimum((m + 1) * (TM // HALO), T // HALO - 1), n)),
                  pl.BlockSpec((FK, 2 * FT), lambda n, m: (0, n))],
        out_specs=pl.BlockSpec((TM, 2 * FT), lambda n, m: (m, n)),
        out_shape=SDS((T, 2 * DFF), BF16),
        scratch_shapes=[pltpu.VMEM((TM + HALO, 2 * FT), F32)],
    )(dup, dup, fcw_il)


def _ffn_up_bwd_ln1(dpre, w_up, dz2, xh1, rstd1, g1):
    def body(a_ref, wg_ref, wv_ref, dz2_ref, xh_ref, rstd_ref, g_ref, dz1_ref, gg_ref, gb_ref, acc):
        m, kk = pl.program_id(0), pl.program_id(1)

        @pl.when(kk == 0)
        def _():
            acc[...] = jnp.zeros_like(acc)

        acc[...] += _dot_nt(a_ref[:, :FT], wg_ref[...]) + _dot_nt(a_ref[:, FT:], wv_ref[...])

        @pl.when(kk == NFT - 1)
        def _():
            @pl.when(m == 0)
            def _():
                gg_ref[...] = jnp.zeros_like(gg_ref)
                gb_ref[...] = jnp.zeros_like(gb_ref)

            dx1 = acc[...] + ALPHA * dz2_ref[...]
            xh = xh_ref[...]
            gg_ref[...] += _colsum(dx1 * xh)
            gb_ref[...] += _colsum(dx1)
            dz1_ref[...] = _layer_norm_bwd(dx1, xh, rstd_ref[...], g_ref[...])

    vec = pl.BlockSpec((1, D), lambda m, k: (0, 0))
    row = pl.BlockSpec((TM, D), lambda m, k: (m, 0))
    wspec = lambda off: pl.BlockSpec((D, FT), lambda m, k: (0, k + off))
    return pl.pallas_call(
        body, name="ffn_up_bwd_ln1", grid=(T // TM, NFT),
        in_specs=[pl.BlockSpec((TM, 2 * FT), lambda m, k: (m, k)), wspec(0), wspec(NFT),
                  row, row, pl.BlockSpec((TM, 1), lambda m, k: (m, 0)), vec],
        out_specs=(row, vec, vec),
        out_shape=(SDS((T, D), F32), SDS((1, D), F32), SDS((1, D), F32)),
        scratch_shapes=[pltpu.VMEM((TM, D), F32)],
    )(dpre, w_up, w_up, dz2, xh1, rstd1, g1)


def _grad_w_up(dpre, x1b):
    tk = 1024

    def body(a_ref, b_ref, o_ref, acc):
        k = pl.program_id(1)

        @pl.when(k == 0)
        def _():
            acc[...] = jnp.zeros_like(acc)

        acc[...] += _dot_tn(a_ref[...], b_ref[...])

        @pl.when(k == T // tk - 1)
        def _():
            o_ref[0] = acc[pl.ds(0, FT), :].astype(o_ref.dtype)
            o_ref[1] = acc[pl.ds(FT, FT), :].astype(o_ref.dtype)

    out = pl.pallas_call(
        body, name="grad_w_up", grid=(NFT, T // tk),
        in_specs=[pl.BlockSpec((tk, 2 * FT), lambda n, k: (k, n)), pl.BlockSpec((tk, D), lambda n, k: (k, 0))],
        out_specs=pl.BlockSpec((2, FT, D), lambda n, k: (0, n, 0)),
        out_shape=SDS((2, DFF, D), GRAD_WIRE),
        scratch_shapes=[pltpu.VMEM((2 * FT, D), F32)],
    )(dpre, x1b)
    return out.reshape(2 * DFF, D)


def _row_tile(rows, cols):
    if rows * cols * 4 <= (1 << 20) or rows % 8:
        return rows
    for t in (256, 176, 128, 88, 64, 32, 16, 8):
        if rows % t == 0 and t * cols * 4 <= (1 << 20):
            return t
    return 8


def _sum8(r, name):
    _, rows, cols = r.shape
    tr = _row_tile(rows, cols)

    def body(r_ref, o_ref):
        acc = r_ref[0].astype(F32)
        for p in range(1, NDEV):
            acc = acc + r_ref[p].astype(F32)
        o_ref[...] = acc

    return pl.pallas_call(
        body, name=name, grid=(rows // tr,),
        in_specs=[pl.BlockSpec((NDEV, tr, cols), lambda i: (0, i, 0))],
        out_specs=pl.BlockSpec((tr, cols), lambda i: (i, 0)),
        out_shape=SDS((rows, cols), F32),
    )(r)


def _adamw(w, g, m, v, name):
    rows, cols = w.shape
    tr = _row_tile(rows, cols)

    def body(w_ref, g_ref, m_ref, v_ref, d_ref, nm_ref, nv_ref):
        g_ = g_ref[...]
        m_ = B1 * m_ref[...] + (1.0 - B1) * g_
        v_ = B2 * v_ref[...] + (1.0 - B2) * jnp.square(g_)
        m_hat = m_ / (1.0 - B1 ** STEP)
        v_hat = v_ / (1.0 - B2 ** STEP)
        d_ref[...] = -LR * (m_hat / (jnp.sqrt(v_hat) + AEPS) + WD * w_ref[...])
        nm_ref[...] = m_
        nv_ref[...] = v_

    spec = pl.BlockSpec((tr, cols), lambda i: (i, 0))
    shp = SDS((rows, cols), F32)
    return pl.pallas_call(
        body, name=name, grid=(rows // tr,), in_specs=[spec] * 4, out_specs=(spec,) * 3,
        out_shape=(shp, shp, shp),
    )(w, g, m, v)


def _interleave(a):
    r = a.shape[0]
    return a.reshape(r, 2, NFT, FT).transpose(0, 2, 1, 3).reshape(r, 2 * DFF)


def _local_step(x2, target, rel_table, w_in, b_in, conv_w, conv_b, conv_ln_g, conv_ln_b, attn_norm_g,
                conv_norm_g, late_weights, ln1_g, ln1_b, ffn_conv_w, ffn_conv_b, ln2_g, ln2_b, ship_ffn_grads):
    buckets = jnp.asarray(_bucket_maps())
    bias = _bias_table(rel_table, buckets)

    qkv = _mm_nn_bias(x2, w_in, b_in, 0, 3, AW, BF16, "proj_qkv")
    ag = _mm_nn_bias(x2, w_in, b_in, 3 * AW // CW, 2, CW, F32, "proj_ag")
    attn, lse = _attn_fwd(qkv, bias)
    u1 = _conv_fwd(ag, conv_w, conv_b)
    mixed = _mix_fwd(attn, u1, attn_norm_g, conv_norm_g, conv_ln_g, conv_ln_b)
    w_out, w_up, w_down = late_weights(mixed)
    xh1, rstd1, x1b = _out_proj_ln1(mixed, w_out, x2, ln1_g, ln1_b)
    up, act = _ffn_up(x1b, w_up, ffn_conv_w, ffn_conv_b)
    dz2, loss, g_ln2_g, g_ln2_b = _ffn_down_loss(act, w_down, xh1, ln1_g, ln1_b, ln2_g, ln2_b, target)

    dup, cs_g, cs_v, gfw_g, gfw_v = _ffn_down_bwd(dz2, w_down, up, ffn_conv_w, ffn_conv_b)
    g_w_down = _mm_tn(act, dz2, DFF // 2, 512, "grad_w_down")
    dpre = _conv3_transpose(dup, _interleave(ffn_conv_w))
    dz1, g_ln1_g, g_ln1_b = _ffn_up_bwd_ln1(dpre, w_up, dz2, xh1, rstd1, ln1_g)
    zero = ship_ffn_grads(g_w_down, _grad_w_up(dpre, x1b))
    dmixed = _mm_nt(dz1, w_out, None, 0.0, "dmixed")
    g_w_out = _mm_tn(mixed, dz1, D, 512, "grad_w_out")
    dattn, du1, g_an, g_cn, g_clg, g_clb, g_cb = _mix_bwd(
        dmixed, attn, u1, attn_norm_g + zero, conv_norm_g, conv_ln_g, conv_ln_b)
    dag, cs_ag, g_conv_w = _conv_bwd(du1, ag, conv_w)
    dq, dk, dv, cs_q, cs_k, cs_v2, dbias = _attn_bwd(qkv, attn, lse, dattn, bias)
    g_rel = _rel_table_grad(dbias, buckets)
    dh = jnp.concatenate([dq, dk, dv, dag], axis=1)
    grad_x = _mm_nt(dh, w_in, dz1, ALPHA, "grad_x")
    g_w_in_t = _mm_tn(dh, x2, INW // 2, 512, "grad_w_in")

    grads = dict(
        rel_table=g_rel,
        b_in=jnp.concatenate([cs_q, cs_k, cs_v2, cs_ag], axis=1),
        conv_b=g_cb, conv_ln_g=g_clg, conv_ln_b=g_clb, attn_norm_g=g_an, conv_norm_g=g_cn,
        ln1_g=g_ln1_g, ln1_b=g_ln1_b,
        ffn_conv_b=jnp.concatenate([cs_g, cs_v], axis=1),
        ln2_g=g_ln2_g, ln2_b=g_ln2_b,
        conv_w=g_conv_w[:CK],
        ffn_conv_w=jnp.concatenate([gfw_g, gfw_v], axis=1),
        w_in_t=g_w_in_t, w_out=g_w_out,
    )
    return loss, grad_x, grads


SMALL = (("rel_table", (NBUCKET, NH)), ("b_in", (1, INW)), ("conv_b", (1, CW)), ("conv_ln_g", (1, CW)),
         ("conv_ln_b", (1, CW)), ("attn_norm_g", (1, AW)), ("conv_norm_g", (1, CW)), ("ln1_g", (1, D)),
         ("ln1_b", (1, D)), ("ffn_conv_b", (1, 2 * DFF)), ("ln2_g", (1, D)), ("ln2_b", (1, D)))
SHARDED_SMALL = (("conv_w", (CK, CW)), ("ffn_conv_w", (FK, 2 * DFF)))


def _pack(parts):
    flat = jnp.concatenate([p.reshape(-1) for p in parts])
    tile = 8 * PACK_LANES
    pad = (-flat.shape[0]) % tile
    return jnp.pad(flat, (0, pad)).reshape(-1, PACK_LANES)


def _unpack(packed, specs):
    flat = packed.reshape(-1)
    out, off = {}, 0
    for name, shp in specs:
        size = int(np.prod(shp))
        out[name] = flat[off:off + size].reshape(shp)
        off += size
    return out


def kernel(x, rel_table, w_in, b_in, conv_w, conv_b, conv_ln_g, conv_ln_b, attn_norm_g, conv_norm_g, w_out, ln1_g, ln1_b, w_up, ffn_conv_w, ffn_conv_b, w_down, ln2_g, ln2_b, loss_target, m_rel_table, m_w_in, m_b_in, m_conv_w, m_conv_b, m_conv_ln_g, m_conv_ln_b, m_attn_norm_g, m_conv_norm_g, m_w_out, m_ln1_g, m_ln1_b, m_w_up, m_ffn_conv_w, m_ffn_conv_b, m_w_down, m_ln2_g, m_ln2_b, v_rel_table, v_w_in, v_b_in, v_conv_w, v_conv_b, v_conv_ln_g, v_conv_ln_b, v_attn_norm_g, v_conv_norm_g, v_w_out, v_ln1_g, v_ln1_b, v_w_up, v_ffn_conv_w, v_ffn_conv_b, v_w_down, v_ln2_g, v_ln2_b):
    given = dict(locals())
    me = 4 * lax.axis_index("x") + 2 * lax.axis_index("y") + lax.axis_index("c")

    cols = lambda a: a.transpose(1, 0, 2).reshape(a.shape[1], NDEV * a.shape[2])
    rows = lambda a: a.reshape(NDEV * a.shape[1], a.shape[2])
    stack = lambda a: a.reshape(NDEV, a.shape[0] // NDEV, a.shape[1])

    first = _exchange([(w_in[0].astype(BF16), "gather"), (conv_w[0], "gather"), (ffn_conv_w[0], "gather")],
                      "gather_first")
    w_in_f, conv_w_f, ffn_conv_w_f = cols(first[0]), cols(first[1]), cols(first[2])
    late_own = [w_out[0].astype(BF16), w_up[0].astype(BF16), w_down[0].astype(BF16)]
    late_state, zero1 = _exchange_start([(a, "gather") for a in late_own], "gather_late_start")

    def late_weights(after):
        lands = _exchange_wait(late_state, after, "gather_late_wait")
        w_out_g, w_up_g, w_down_g = [_own_slot(l, o, me) for l, o in zip(lands, late_own)]
        return rows(w_out_g), cols(w_up_g), rows(w_down_g)

    shipped = {}

    def ship_ffn_grads(g_w_down, g_w_up_t):
        shipped["own"] = [stack(g_w_down), stack(g_w_up_t)]
        shipped["state"], zero2 = _exchange_start([(a, "scatter") for a in shipped["own"]], "ffn_grads_start")
        return zero2

    loss, grad_x, g = _local_step(
        x.reshape(T, D), loss_target.reshape(T, D), rel_table, w_in_f, b_in + zero1, conv_w_f, conv_b, conv_ln_g,
        conv_ln_b, attn_norm_g, conv_norm_g, late_weights, ln1_g, ln1_b, ffn_conv_w_f, ffn_conv_b,
        ln2_g, ln2_b, ship_ffn_grads)

    small_specs = SMALL + SHARDED_SMALL
    packed = _pack([g[n] for n, _ in small_specs])
    got = _exchange([(stack(g["w_in_t"]), "scatter"), (stack(g["w_out"]), "scatter"), (packed, "gather")],
                    "exchange_grads")
    ffn_lands = _exchange_wait(shipped["state"], got[2], "ffn_grads_wait")
    got_down, got_up = [_own_slot(l, lax.dynamic_index_in_dim(o, me, 0, keepdims=False), me)
                        for l, o in zip(ffn_lands, shipped["own"])]

    grad = {}
    grad["w_in"] = _sum8(got[0], "sum_w_in").T[None]
    grad["w_out"] = _sum8(got[1], "sum_w_out")[None]
    grad["w_up"] = _sum8(got_up, "sum_w_up").T[None]
    grad["w_down"] = _sum8(got_down, "sum_w_down")[None]
    small = _unpack(_sum8(got[2], "sum_small"), small_specs)
    for n, _ in SMALL:
        grad[n] = small[n]
    grad["conv_w"] = lax.dynamic_slice_in_dim(small["conv_w"], me * (CW // NDEV), CW // NDEV, axis=1)[None]
    grad["ffn_conv_w"] = lax.dynamic_slice_in_dim(small["ffn_conv_w"], me * (2 * DFF // NDEV), 2 * DFF // NDEV, axis=1)[None]

    delta, new_m, new_v = {}, {}, {}
    for n in ("w_in", "w_out", "w_up", "w_down", "conv_w", "ffn_conv_w"):
        shp = given[n].shape
        two = lambda a: a.reshape(shp[-2], shp[-1])
        d_, m_, v_ = _adamw(two(given[n]), two(grad[n]), two(given["m_" + n]), two(given["v_" + n]), "adamw_" + n)
        delta[n], new_m[n], new_v[n] = d_.reshape(shp), m_.reshape(shp), v_.reshape(shp)
    wp = _pack([given[n] for n, _ in SMALL])
    gp = _pack([grad[n] for n, _ in SMALL])
    mp = _pack([given["m_" + n] for n, _ in SMALL])
    vp = _pack([given["v_" + n] for n, _ in SMALL])
    dp, nmp, nvp = _adamw(wp, gp, mp, vp, "adamw_small")
    for dst, src in ((delta, dp), (new_m, nmp), (new_v, nvp)):
        dst.update(_unpack(src, SMALL))

    order = ("rel_table", "w_in", "b_in", "conv_w", "conv_b", "conv_ln_g", "conv_ln_b", "attn_norm_g",
             "conv_norm_g", "w_out", "ln1_g", "ln1_b", "w_up", "ffn_conv_w", "ffn_conv_b", "w_down", "ln2_g", "ln2_b")
    total_loss = lax.psum(loss[0, 0], ("x", "y", "c"))
    return (total_loss, grad_x.reshape(BL, S, D), *[grad[n] for n in order], *[delta[n] for n in order],
            *[new_m[n] for n in order], *[new_v[n] for n in order])
```

```python
import functools
import math

import numpy as np
import jax
import jax.numpy as jnp
from jax import lax
from jax.experimental import pallas as pl
from jax.experimental.pallas import tpu as pltpu

F32 = jnp.float32
BF16 = jnp.bfloat16
SDS = jax.ShapeDtypeStruct

NDEV = 8
D = 1024
S = 2048
BL = 2
T = BL * S
NH = 12
HD = 64
AW = NH * HD
CW = D - AW
INW = 3 * AW + 2 * CW
CK = 31
DFF = 2816
FK = 3
BLK = 128
NBUCKET = 32
BRANCHES = ((128, 1), (512, 4), (2048, 16))
ALPHA = 2.0 ** 0.25
LN_EPS = 1e-5
NEG_INF = -1e30
LR, B1, B2, AEPS, WD, STEP = 0.001, 0.9, 0.999, 1e-08, 0.01, 10

TM = 512
FT = 256
NFT = DFF // FT
PACK_LANES = 128
GRAD_WIRE = BF16

assert all(w // d == BLK for w, d in BRANCHES)


def _dot(a, b):
    return jnp.dot(a, b, preferred_element_type=F32)


def _dot_nt(a, b):
    return lax.dot_general(a, b, (((1,), (1,)), ((), ())), preferred_element_type=F32)


def _dot_tn(a, b):
    return lax.dot_general(a, b, (((0,), (0,)), ((), ())), preferred_element_type=F32)


def _rowmean(v):
    return jnp.mean(v, axis=-1, keepdims=True)


def _colsum(v):
    return jnp.sum(v, axis=0, keepdims=True)


def _sigmoid(v):
    return jax.nn.sigmoid(v)


def _exchange(items, name):
    n = len(items)
    arrs = [a for a, _ in items]
    kinds = [k for _, k in items]
    out_shapes = []
    for a, k in items:
        shp = (NDEV,) + tuple(a.shape) if k == "gather" else tuple(a.shape)
        out_shapes.append(SDS(shp, a.dtype))

    def body(*refs):
        ins = refs[:n]
        outs = refs[n:2 * n]
        send_sems, recv_sems, local_sems = refs[2 * n:]
        x, y, c = lax.axis_index("x"), lax.axis_index("y"), lax.axis_index("c")
        me = 4 * x + 2 * y + c

        def peer(k):
            px = 1 - x if k & 4 else x
            py = 1 - y if k & 2 else y
            pc = 1 - c if k & 1 else c
            return (px, py, pc), 4 * px + 2 * py + pc

        local = []
        for i in range(n):
            src = ins[i] if kinds[i] == "gather" else ins[i].at[me]
            cp = pltpu.make_async_copy(src, outs[i].at[me], local_sems.at[i])
            cp.start()
            local.append(cp)
        sends = []
        for k in range(1, NDEV):
            dev, pid = peer(k)
            for i in range(n):
                src = ins[i] if kinds[i] == "gather" else ins[i].at[pid]
                cp = pltpu.make_async_remote_copy(
                    src_ref=src, dst_ref=outs[i].at[me],
                    send_sem=send_sems.at[i, k - 1], recv_sem=recv_sems.at[i, k - 1],
                    device_id=dev, device_id_type=pl.DeviceIdType.MESH)
                cp.start()
                sends.append(cp)
        for k in range(1, NDEV):
            dev, pid = peer(k)
            for i in range(n):
                src = ins[i] if kinds[i] == "gather" else ins[i].at[pid]
                pltpu.make_async_remote_copy(
                    src_ref=src, dst_ref=outs[i].at[pid],
                    send_sem=send_sems.at[i, k - 1], recv_sem=recv_sems.at[i, k - 1],
                    device_id=dev, device_id_type=pl.DeviceIdType.MESH).wait_recv()
        for cp in sends:
            cp.wait_send()
        for cp in local:
            cp.wait()

    any_spec = pl.BlockSpec(memory_space=pl.ANY)
    return pl.pallas_call(
        body, name=name,
        out_shape=tuple(out_shapes),
        in_specs=[any_spec] * n,
        out_specs=tuple([any_spec] * n),
        scratch_shapes=[pltpu.SemaphoreType.DMA((n, NDEV - 1)),
                        pltpu.SemaphoreType.DMA((n, NDEV - 1)),
                        pltpu.SemaphoreType.DMA((n,))],
        compiler_params=pltpu.CompilerParams(has_side_effects=True),
    )(*arrs)


_HBM = pl.BlockSpec(memory_space=pltpu.HBM)
_SEM = pl.BlockSpec(memory_space=pltpu.SEMAPHORE)
_EFFECT = pltpu.SideEffectType.DATAFLOW_SIDE_EFFECTING


def _peer_of(k):
    x, y, c = lax.axis_index("x"), lax.axis_index("y"), lax.axis_index("c")
    px = 1 - x if k & 4 else x
    py = 1 - y if k & 2 else y
    pc = 1 - c if k & 1 else c
    return (px, py, pc), 4 * px + 2 * py + pc


def _split_copies(kinds, ins, lands, send_sems, recv_sems, started):
    me = 4 * lax.axis_index("x") + 2 * lax.axis_index("y") + lax.axis_index("c")
    out = []
    for k in range(1, NDEV):
        dev, pid = _peer_of(k)
        for i, kind in enumerate(kinds):
            src = ins[i] if kind == "gather" else ins[i].at[pid]
            dst = lands[i].at[me] if started else lands[i].at[pid]
            slot = i * (NDEV - 1) + k - 1
            out.append(pltpu.make_async_remote_copy(
                src_ref=src, dst_ref=dst, send_sem=send_sems.at[slot], recv_sem=recv_sems.at[slot],
                device_id=dev, device_id_type=pl.DeviceIdType.MESH))
    return out


def _exchange_start(items, name):
    n = len(items)
    kinds = [k for _, k in items]
    srcs = [pltpu.with_memory_space_constraint(a, pltpu.HBM) for a, _ in items]
    lands = []
    for a, k in items:
        shp = (NDEV,) + tuple(a.shape) if k == "gather" else tuple(a.shape)
        lands.append(pltpu.with_memory_space_constraint(lax.empty(shp, a.dtype), pltpu.HBM))

    def body(*refs):
        ins, land_refs = refs[:n], refs[n:2 * n]
        send_sems, recv_sems = refs[2 * n], refs[2 * n + 1]
        token = refs[-1]
        for cp in _split_copies(kinds, ins, land_refs, send_sems, recv_sems, True):
            cp.start()
        token[...] = jnp.zeros_like(token)

    sems = pltpu.SemaphoreType.DMA((n * (NDEV - 1),))
    res = pl.pallas_call(
        body, name=name,
        out_shape=(sems, sems, *[pltpu.HBM(a.shape, a.dtype) for a in srcs + lands], SDS((8, 128), F32)),
        in_specs=[_HBM] * (2 * n),
        out_specs=(_SEM, _SEM, *[_HBM] * (2 * n), pl.BlockSpec(memory_space=pltpu.VMEM)),
        input_output_aliases={i: 2 + i for i in range(2 * n)},
        compiler_params=pltpu.CompilerParams(has_side_effects=_EFFECT),
    )(*srcs, *lands)
    return (kinds, res[0], res[1], list(res[2:2 + n]), list(res[2 + n:2 + 2 * n])), res[-1][0, 0]


def _exchange_wait(state, after, name):
    kinds, send_sems, recv_sems, srcs, lands = state
    n = len(kinds)

    def body(*refs):
        ins, land_refs = refs[:n], refs[n:2 * n]
        s_sems, r_sems = refs[2 * n], refs[2 * n + 1]
        for cp in _split_copies(kinds, ins, land_refs, s_sems, r_sems, False):
            cp.wait_send()
            cp.wait_recv()

    res = pl.pallas_call(
        body, name=name,
        out_shape=tuple(pltpu.HBM(a.shape, a.dtype) for a in srcs + lands),
        in_specs=[_HBM] * (2 * n) + [_SEM, _SEM, pl.BlockSpec(memory_space=pl.ANY)],
        out_specs=tuple([_HBM] * (2 * n)),
        input_output_aliases={i: i for i in range(2 * n)},
        compiler_params=pltpu.CompilerParams(has_side_effects=_EFFECT),
    )(*srcs, *lands, send_sems, recv_sems, after)
    return list(res[n:])


def _own_slot(land, own, me):
    return lax.dynamic_update_slice_in_dim(land, own[None].astype(land.dtype), me, axis=0)


def _mm_nn_bias(a, b, bias, col_blk0, nblk, tn, out_dtype, name):
    m_, k_ = a.shape

    def body(a_ref, b_ref, bias_ref, o_ref):
        acc = _dot(a_ref[...].astype(BF16), b_ref[...])
        o_ref[...] = (acc + bias_ref[...]).astype(o_ref.dtype)

    return pl.pallas_call(
        body, name=name, grid=(nblk, m_ // TM),
        in_specs=[pl.BlockSpec((TM, k_), lambda n, m: (m, 0)),
                  pl.BlockSpec((k_, tn), lambda n, m: (0, col_blk0 + n)),
                  pl.BlockSpec((1, tn), lambda n, m: (0, col_blk0 + n))],
        out_specs=pl.BlockSpec((TM, tn), lambda n, m: (m, n)),
        out_shape=SDS((m_, nblk * tn), out_dtype),
    )(a, b, bias)


def _mm_nt(a, b, res, res_scale, name):
    m_, k_ = a.shape
    n_ = b.shape[0]
    has_res = res is not None

    def body(*refs):
        if has_res:
            a_ref, b_ref, r_ref, o_ref = refs
        else:
            a_ref, b_ref, o_ref = refs
        acc = _dot_nt(a_ref[...].astype(BF16), b_ref[...].astype(BF16))
        if has_res:
            acc = acc + res_scale * r_ref[...]
        o_ref[...] = acc

    in_specs = [pl.BlockSpec((TM, k_), lambda m: (m, 0)), pl.BlockSpec((n_, k_), lambda m: (0, 0))]
    args = [a, b]
    if has_res:
        in_specs.append(pl.BlockSpec((TM, n_), lambda m: (m, 0)))
        args.append(res)
    return pl.pallas_call(
        body, name=name, grid=(m_ // TM,), in_specs=in_specs,
        out_specs=pl.BlockSpec((TM, n_), lambda m: (m, 0)),
        out_shape=SDS((m_, n_), F32),
    )(*args)


def _mm_tn(a, b, tn, tk, name):
    t_, na = a.shape
    nb = b.shape[1]
    nk = t_ // tk

    def body(a_ref, b_ref, o_ref, acc):
        k = pl.program_id(1)

        @pl.when(k == 0)
        def _():
            acc[...] = jnp.zeros_like(acc)

        acc[...] += _dot_tn(a_ref[...].astype(BF16), b_ref[...].astype(BF16))

        @pl.when(k == nk - 1)
        def _():
            o_ref[...] = acc[...].astype(o_ref.dtype)

    return pl.pallas_call(
        body, name=name, grid=(na // tn, nk),
        in_specs=[pl.BlockSpec((tk, tn), lambda n, k: (k, n)),
                  pl.BlockSpec((tk, nb), lambda n, k: (k, 0))],
        out_specs=pl.BlockSpec((tn, nb), lambda n, k: (n, 0)),
        out_shape=SDS((na, nb), GRAD_WIRE),
        scratch_shapes=[pltpu.VMEM((tn, nb), F32)],
    )(a, b)


def _bucket_maps():
    qi = np.arange(BLK)[:, None]
    kj = np.arange(2 * BLK)[None, :]
    steps = np.maximum(qi + BLK - kj, 0)
    exact = NBUCKET // 2
    maps = []
    for _, dil in BRANCHES:
        dist = steps * dil
        d_f = np.maximum(dist, 1).astype(np.float32)
        large = exact + (np.log(d_f / np.float32(exact)) / np.float32(math.log(S / exact))
                         * np.float32(NBUCKET - exact)).astype(np.int32)
        large = np.minimum(large, NBUCKET - 1)
        maps.append(np.where(dist < exact, dist, large).astype(np.int32))
    return np.stack(maps)


def _bias_table(rel_table, buckets):
    def body(t_ref, b_ref, o_ref):
        bk = b_ref[0]
        for h in range(NH):
            acc = jnp.zeros((BLK, 2 * BLK), F32)
            for k in range(NBUCKET):
                acc = jnp.where(bk == k, t_ref[k, h], acc)
            o_ref[0, h] = acc

    return pl.pallas_call(
        body, name="bias_table", grid=(len(BRANCHES),),
        in_specs=[pl.BlockSpec(memory_space=pltpu.SMEM),
                  pl.BlockSpec((1, BLK, 2 * BLK), lambda i: (i, 0, 0))],
        out_specs=pl.BlockSpec((1, NH, BLK, 2 * BLK), lambda i: (i, 0, 0, 0)),
        out_shape=SDS((len(BRANCHES), NH, BLK, 2 * BLK), F32),
    )(rel_table, buckets)


def _rel_table_grad(dbias, buckets):
    def body(d_ref, b_ref, o_ref):
        h = pl.program_id(0)
        for k in range(NBUCKET):
            tot = jnp.zeros((1, 1), F32)
            for br in range(len(BRANCHES)):
                sel = jnp.where(b_ref[br] == k, d_ref[br, 0], 0.0)
                tot = tot + jnp.sum(jnp.sum(sel, axis=1, keepdims=True), axis=0, keepdims=True)
            o_ref[0, :, pl.ds(k, 1)] = tot

    out = pl.pallas_call(
        body, name="rel_table_grad", grid=(NH,),
        in_specs=[pl.BlockSpec((len(BRANCHES), 1, BLK, 2 * BLK), lambda h: (0, h, 0, 0)),
                  pl.BlockSpec((len(BRANCHES), BLK, 2 * BLK), lambda h: (0, 0, 0))],
        out_specs=pl.BlockSpec((1, 1, NBUCKET), lambda h: (h, 0, 0)),
        out_shape=SDS((NH, 1, NBUCKET), F32),
    )(dbias, buckets)
    return out.reshape(NH, NBUCKET).T


def _block_rows(br, i):
    _, dil = BRANCHES[br]
    nb = S // dil // BLK
    if nb == 16:
        r, nidx = 0, i
    elif nb == 4:
        r, nidx = lax.shift_right_logical(i, 2), lax.bitwise_and(i, 3)
    else:
        r, nidx = i, 0
    start = r + dil * BLK * nidx
    if nb == 1:
        return start, None, None
    prev = r + dil * BLK * jnp.maximum(nidx - 1, 0)
    return start, prev, nidx > 0


def _rows(start, dil):
    if dil == 1:
        return pl.ds(pl.multiple_of(start, BLK), BLK)
    return pl.ds(start, BLK, stride=dil)


def _attn_masks():
    lane = lax.broadcasted_iota(jnp.int32, (BLK, BLK), 1)
    qi = lax.broadcasted_iota(jnp.int32, (BLK, BLK), 0)
    head0 = lane < HD
    valid_cur = lane <= qi
    valid_prev = lane >= qi
    return head0, valid_cur, valid_prev


def _attn_fwd_v1(qkv, bias):
    scale = 1.0 / math.sqrt(HD)
    nbr = len(BRANCHES)

    def body(q_ref, k_ref, v_ref, bias_ref, o_ref, lse_ref, qf, kf, vf, ob, mb, lb):
        qf[...] = q_ref[...].astype(F32)
        kf[...] = k_ref[...].astype(F32)
        vf[...] = v_ref[...].astype(F32)
        head0, valid_cur, valid_prev = _attn_masks()

        for br in range(nbr):
            dil = BRANCHES[br][1]

            def blk(i, carry, br=br, dil=dil):
                start, prev, has_prev = _block_rows(br, i)
                rows = _rows(start, dil)
                q = qf[rows, :]
                kc = kf[rows, :].astype(BF16)
                vc = vf[rows, :].astype(BF16)
                if prev is not None:
                    prows = _rows(prev, dil)
                    kp = kf[prows, :].astype(BF16)
                    vp = vf[prows, :].astype(BF16)
                    ok_prev = jnp.logical_and(valid_prev, has_prev)
                o_acc = jnp.zeros((BLK, BLK), F32)
                m_acc = jnp.zeros((BLK, BLK), F32)
                l_acc = jnp.zeros((BLK, BLK), F32)
                for j in range(2):
                    mj = head0 if j == 0 else jnp.logical_not(head0)
                    qj = jnp.where(mj, q, 0.0).astype(BF16)
                    sc = _dot_nt(qj, kc) * scale + bias_ref[br, j, :, BLK:]
                    sc = jnp.where(valid_cur, sc, NEG_INF)
                    mx = jnp.max(sc, axis=-1, keepdims=True)
                    if prev is not None:
                        sp = _dot_nt(qj, kp) * scale + bias_ref[br, j, :, :BLK]
                        sp = jnp.where(ok_prev, sp, NEG_INF)
                        mx = jnp.maximum(mx, jnp.max(sp, axis=-1, keepdims=True))
                    pc = jnp.exp(sc - mx)
                    ls = jnp.sum(pc, axis=-1, keepdims=True)
                    o = _dot(pc.astype(BF16), vc)
                    if prev is not None:
                        pp = jnp.exp(sp - mx)
                        ls = ls + jnp.sum(pp, axis=-1, keepdims=True)
                        o = o + _dot(pp.astype(BF16), vp)
                    o_acc = jnp.where(mj, o, o_acc)
                    m_acc = jnp.where(mj, mx, m_acc)
                    l_acc = jnp.where(mj, ls, l_acc)
                ob[br, rows, :] = o_acc
                mb[br, rows, :] = m_acc
                lb[br, rows, :] = l_acc
                return carry

            lax.fori_loop(0, 16, blk, 0)

        def merge(i, carry):
            rows = pl.ds(pl.multiple_of(i * 256, 256), 256)
            m_all = jnp.maximum(jnp.maximum(mb[0, rows, :], mb[1, rows, :]), mb[2, rows, :])
            num = jnp.zeros((256, BLK), F32)
            den = jnp.zeros((256, BLK), F32)
            for br in range(nbr):
                c = jnp.exp(mb[br, rows, :] - m_all)
                num = num + ob[br, rows, :] * c
                den = den + lb[br, rows, :] * c
            o_ref[rows, :] = num / den
            lse_ref[rows, :] = m_all + jnp.log(den)
            return carry

        lax.fori_loop(0, S // 256, merge, 0)

    npair = NH // 2
    blk_spec = lambda off: pl.BlockSpec((S, BLK), lambda b, hp: (b, off + hp))
    return pl.pallas_call(
        body, name="attn_fwd", grid=(BL, npair),
        in_specs=[blk_spec(0), blk_spec(npair), blk_spec(2 * npair),
                  pl.BlockSpec((nbr, 2, BLK, 2 * BLK), lambda b, hp: (0, hp, 0, 0))],
        out_specs=(blk_spec(0), blk_spec(0)),
        out_shape=(SDS((T, AW), F32), SDS((T, AW), F32)),
        scratch_shapes=[pltpu.VMEM((S, BLK), F32)] * 3 + [pltpu.VMEM((nbr, S, BLK), F32)] * 3,
    )(qkv, qkv, qkv, bias)


def _attn_bwd_v1(qkv, attn, lse, dattn, bias):
    scale = 1.0 / math.sqrt(HD)
    nbr = len(BRANCHES)

    def body(q_ref, k_ref, v_ref, o_ref, lse_ref, do_ref, bias_ref,
             dq_ref, dk_ref, dv_ref, sq_ref, sk_ref, sv_ref, db_ref,
             qf, kf, vf, dl, dqa, dka, dva):
        b = pl.program_id(1)
        qf[...] = q_ref[...].astype(F32)
        kf[...] = k_ref[...].astype(F32)
        vf[...] = v_ref[...].astype(F32)
        dqa[...] = jnp.zeros_like(dqa)
        dka[...] = jnp.zeros_like(dka)
        dva[...] = jnp.zeros_like(dva)
        head0, valid_cur, valid_prev = _attn_masks()

        @pl.when(b == 0)
        def _():
            db_ref[...] = jnp.zeros_like(db_ref)
            sq_ref[...] = jnp.zeros_like(sq_ref)
            sk_ref[...] = jnp.zeros_like(sk_ref)
            sv_ref[...] = jnp.zeros_like(sv_ref)

        def delta(i, carry):
            rows = pl.ds(pl.multiple_of(i * 256, 256), 256)
            prod = do_ref[rows, :] * o_ref[rows, :]
            h0 = lax.broadcasted_iota(jnp.int32, (256, BLK), 1) < HD
            d0 = jnp.sum(jnp.where(h0, prod, 0.0), axis=-1, keepdims=True)
            d1 = jnp.sum(jnp.where(h0, 0.0, prod), axis=-1, keepdims=True)
            dl[rows, :] = jnp.where(h0, d0, d1)
            return carry

        lax.fori_loop(0, S // 256, delta, 0)

        for br in range(nbr):
            dil = BRANCHES[br][1]

            def blk(i, carry, br=br, dil=dil):
                start, prev, has_prev = _block_rows(br, i)
                rows = _rows(start, dil)
                q = qf[rows, :]
                kc = kf[rows, :].astype(BF16)
                vc = vf[rows, :].astype(BF16)
                do = do_ref[rows, :]
                lse_b = lse_ref[rows, :]
                dl_b = dl[rows, :]
                if prev is not None:
                    prows = _rows(prev, dil)
                    kp = kf[prows, :].astype(BF16)
                    vp = vf[prows, :].astype(BF16)
                    ok_prev = jnp.logical_and(valid_prev, has_prev)
                    dk_p = jnp.zeros((BLK, BLK), F32)
                    dv_p = jnp.zeros((BLK, BLK), F32)
                dq = jnp.zeros((BLK, BLK), F32)
                dk_c = jnp.zeros((BLK, BLK), F32)
                dv_c = jnp.zeros((BLK, BLK), F32)
                for j in range(2):
                    mj = head0 if j == 0 else jnp.logical_not(head0)
                    qj = jnp.where(mj, q, 0.0).astype(BF16)
                    doj = jnp.where(mj, do, 0.0).astype(BF16)
                    lse_j = lse_b[:, j * HD:j * HD + 1]
                    dl_j = dl_b[:, j * HD:j * HD + 1]
                    sc = _dot_nt(qj, kc) * scale + bias_ref[br, j, :, BLK:]
                    pc = jnp.where(valid_cur, jnp.exp(sc - lse_j), 0.0)
                    ds_c = pc * (_dot_nt(doj, vc) - dl_j)
                    db_ref[br, j, :, BLK:] += ds_c
                    dsb = (ds_c * scale).astype(BF16)
                    dqj = _dot(dsb, kc)
                    dk_c = dk_c + _dot_tn(dsb, qj)
                    dv_c = dv_c + _dot_tn(pc.astype(BF16), doj)
                    if prev is not None:
                        sp = _dot_nt(qj, kp) * scale + bias_ref[br, j, :, :BLK]
                        pp = jnp.where(ok_prev, jnp.exp(sp - lse_j), 0.0)
                        ds_p = pp * (_dot_nt(doj, vp) - dl_j)
                        db_ref[br, j, :, :BLK] += ds_p
                        dsbp = (ds_p * scale).astype(BF16)
                        dqj = dqj + _dot(dsbp, kp)
                        dk_p = dk_p + _dot_tn(dsbp, qj)
                        dv_p = dv_p + _dot_tn(pp.astype(BF16), doj)
                    dq = jnp.where(mj, dqj, dq)
                dqa[rows, :] = dqa[rows, :] + dq
                dka[rows, :] = dka[rows, :] + dk_c
                dva[rows, :] = dva[rows, :] + dv_c
                if prev is not None:
                    dka[prows, :] = dka[prows, :] + dk_p
                    dva[prows, :] = dva[prows, :] + dv_p
                return carry

            lax.fori_loop(0, 16, blk, 0)

        def flush(i, carry):
            rows = pl.ds(pl.multiple_of(i * 256, 256), 256)
            for acc, out, cs in ((dqa, dq_ref, sq_ref), (dka, dk_ref, sk_ref), (dva, dv_ref, sv_ref)):
                val = acc[rows, :]
                out[rows, :] = val.astype(BF16)
                cs[...] += _colsum(val)
            return carry

        lax.fori_loop(0, S // 256, flush, 0)

    npair = NH // 2
    blk_spec = lambda off: pl.BlockSpec((S, BLK), lambda hp, b: (b, off + hp))
    sum_spec = pl.BlockSpec((1, BLK), lambda hp, b: (0, hp))
    return pl.pallas_call(
        body, name="attn_bwd", grid=(npair, BL),
        in_specs=[blk_spec(0), blk_spec(npair), blk_spec(2 * npair), blk_spec(0), blk_spec(0), blk_spec(0),
                  pl.BlockSpec((nbr, 2, BLK, 2 * BLK), lambda hp, b: (0, hp, 0, 0))],
        out_specs=(blk_spec(0), blk_spec(0), blk_spec(0), sum_spec, sum_spec, sum_spec,
                   pl.BlockSpec((nbr, 2, BLK, 2 * BLK), lambda hp, b: (0, hp, 0, 0))),
        out_shape=(SDS((T, AW), BF16), SDS((T, AW), BF16), SDS((T, AW), BF16),
                   SDS((1, AW), F32), SDS((1, AW), F32), SDS((1, AW), F32),
                   SDS((nbr, NH, BLK, 2 * BLK), F32)),
        scratch_shapes=[pltpu.VMEM((S, BLK), F32)] * 7,
    )(qkv, qkv, qkv, attn, lse, dattn, bias)


PADK = BLK
SCALE = 1.0 / math.sqrt(HD)
ATTN_UNROLL = 8


def _branch_geometry(br):
    dil = BRANCHES[br][1]
    sub = S // dil
    return dil, sub, sub // BLK


def _token_rows(br, i):
    dil, _, nblk = _branch_geometry(br)
    if dil == 1:
        return pl.ds(pl.multiple_of(i * BLK, BLK), BLK), i
    r = lax.shift_right_logical(i, nblk.bit_length() - 1)
    n = lax.bitwise_and(i, nblk - 1)
    return pl.ds(r + dil * BLK * n, BLK, stride=dil), n


def _sub_layout_loop(br, step):
    dil, sub, _ = _branch_geometry(br)
    rows = min(sub, 256)
    nchunk = sub // rows

    def it_step(it, carry):
        if dil == 1:
            src = pl.ds(pl.multiple_of(it * rows, rows), rows)
        else:
            r = lax.shift_right_logical(it, nchunk.bit_length() - 1)
            src = pl.ds(r + dil * rows * lax.bitwise_and(it, nchunk - 1), rows, stride=dil)
        step(src, pl.multiple_of(it * rows, BLK), rows)
        return carry

    lax.fori_loop(0, dil * nchunk, it_step, 0)


def _masked_bias(bias_ref, bm):
    qi = lax.broadcasted_iota(jnp.int32, (BLK, 2 * BLK), 0)
    kj = lax.broadcasted_iota(jnp.int32, (BLK, 2 * BLK), 1)
    first = jnp.logical_and(kj >= BLK, kj - BLK <= qi)
    valid = jnp.logical_or(first, jnp.logical_and(kj < BLK, kj >= qi))
    for br in range(len(BRANCHES)):
        for j in range(2):
            b = bias_ref[br, j]
            bm[br, 1, pl.ds(j * BLK, BLK), :] = jnp.where(valid, b, NEG_INF)
            bm[br, 0, pl.ds(j * BLK, BLK), :] = jnp.where(first, b, NEG_INF)


def _head_split(fn):
    def split(t):
        h0 = lax.broadcasted_iota(jnp.int32, t.shape, 1) < HD
        t = fn(t)
        return jnp.where(h0, t, 0.0).astype(BF16), jnp.where(h0, 0.0, t).astype(BF16)
    return split


def _attn_fwd(qkv, bias):
    nbr = len(BRANCHES)

    def body(q_ref, k_ref, v_ref, bias_ref, o_ref, lse_ref, qf, kf, vf, qs0, qs1, ks, vs, bm, ob, mb, lb):
        qf[...] = q_ref[...].astype(F32)
        kf[...] = k_ref[...].astype(F32)
        vf[...] = v_ref[...].astype(F32)
        _masked_bias(bias_ref, bm)
        ks[pl.ds(0, PADK), :] = jnp.zeros((PADK, BLK), BF16)
        vs[pl.ds(0, PADK), :] = jnp.zeros((PADK, BLK), BF16)
        head0 = lax.broadcasted_iota(jnp.int32, (BLK, BLK), 1) < HD
        split_q = _head_split(lambda t: t * SCALE)

        for br in range(nbr):
            nblk = _branch_geometry(br)[2]

            def stage(src, off, rows):
                qs0[pl.ds(off, rows), :], qs1[pl.ds(off, rows), :] = split_q(qf[src, :])
                ks[pl.ds(PADK + off, rows), :] = kf[src, :].astype(BF16)
                vs[pl.ds(PADK + off, rows), :] = vf[src, :].astype(BF16)

            _sub_layout_loop(br, stage)

            def blk(i, carry, br=br, nblk=nblk):
                base = pl.multiple_of(i * BLK, BLK)
                rows, n = _token_rows(br, i)
                q01 = jnp.concatenate([qs0[pl.ds(base, BLK), :], qs1[pl.ds(base, BLK), :]], axis=0)
                if nblk > 1:
                    kcat = ks[pl.ds(base, 2 * BLK), :]
                    vcat = vs[pl.ds(base, 2 * BLK), :]
                    s = _dot_nt(q01, kcat) + bm[br, jnp.minimum(n, 1)]
                else:
                    kcat = ks[pl.ds(PADK + base, BLK), :]
                    vcat = vs[pl.ds(PADK + base, BLK), :]
                    s = _dot_nt(q01, kcat) + bm[br, 0, :, BLK:]
                mx = jnp.max(s, axis=-1, keepdims=True)
                p = jnp.exp(s - mx)
                ls = jnp.sum(p, axis=-1, keepdims=True)
                o = _dot(p.astype(BF16), vcat)
                ob[br, rows, :] = jnp.where(head0, o[:BLK], o[BLK:])
                mb[br, rows, :] = jnp.where(head0, mx[:BLK], mx[BLK:])
                lb[br, rows, :] = jnp.where(head0, ls[:BLK], ls[BLK:])
                return carry

            lax.fori_loop(0, 16, blk, 0, unroll=ATTN_UNROLL)

        def merge(i, carry):
            rows = pl.ds(pl.multiple_of(i * 256, 256), 256)
            m_all = jnp.maximum(jnp.maximum(mb[0, rows, :], mb[1, rows, :]), mb[2, rows, :])
            num = jnp.zeros((256, BLK), F32)
            den = jnp.zeros((256, BLK), F32)
            for br in range(nbr):
                c = jnp.exp(mb[br, rows, :] - m_all)
                num = num + ob[br, rows, :] * c
                den = den + lb[br, rows, :] * c
            o_ref[rows, :] = num / den
            lse_ref[rows, :] = m_all + jnp.log(den)
            return carry

        lax.fori_loop(0, S // 256, merge, 0)

    npair = NH // 2
    blk_spec = lambda off: pl.BlockSpec((S, BLK), lambda b, hp: (b, off + hp))
    return pl.pallas_call(
        body, name="attn_fwd", grid=(BL, npair),
        in_specs=[blk_spec(0), blk_spec(npair), blk_spec(2 * npair),
                  pl.BlockSpec((nbr, 2, BLK, 2 * BLK), lambda b, hp: (0, hp, 0, 0))],
        out_specs=(blk_spec(0), blk_spec(0)),
        out_shape=(SDS((T, AW), F32), SDS((T, AW), F32)),
        scratch_shapes=[pltpu.VMEM((S, BLK), F32)] * 3 + [pltpu.VMEM((S, BLK), BF16)] * 2
        + [pltpu.VMEM((PADK + S, BLK), BF16)] * 2 + [pltpu.VMEM((nbr, 2, 2 * BLK, 2 * BLK), F32)]
        + [pltpu.VMEM((nbr, S, BLK), F32)] * 3,
    )(qkv, qkv, qkv, bias)


def _attn_bwd(qkv, attn, lse, dattn, bias):
    nbr = len(BRANCHES)

    def body(q_ref, k_ref, v_ref, o_ref, lse_ref, do_ref, bias_ref,
             dq_ref, dk_ref, dv_ref, sq_ref, sk_ref, sv_ref, db_ref,
             qf, kf, vf, dl, dqa, dka, dva, qs0, qs1, ds0, ds1, ks, vs, dks, dvs, bm):
        b = pl.program_id(1)
        qf[...] = q_ref[...].astype(F32)
        kf[...] = k_ref[...].astype(F32)
        vf[...] = v_ref[...].astype(F32)
        dqa[...] = jnp.zeros_like(dqa)
        dka[...] = jnp.zeros_like(dka)
        dva[...] = jnp.zeros_like(dva)
        _masked_bias(bias_ref, bm)
        ks[pl.ds(0, PADK), :] = jnp.zeros((PADK, BLK), BF16)
        vs[pl.ds(0, PADK), :] = jnp.zeros((PADK, BLK), BF16)
        head0 = lax.broadcasted_iota(jnp.int32, (BLK, BLK), 1) < HD
        split_q = _head_split(lambda t: t * SCALE)
        split_do = _head_split(lambda t: t)

        @pl.when(b == 0)
        def _():
            db_ref[...] = jnp.zeros_like(db_ref)
            sq_ref[...] = jnp.zeros_like(sq_ref)
            sk_ref[...] = jnp.zeros_like(sk_ref)
            sv_ref[...] = jnp.zeros_like(sv_ref)

        def delta(i, carry):
            rows = pl.ds(pl.multiple_of(i * 256, 256), 256)
            prod = do_ref[rows, :] * o_ref[rows, :]
            h0 = lax.broadcasted_iota(jnp.int32, (256, BLK), 1) < HD
            d0 = jnp.sum(jnp.where(h0, prod, 0.0), axis=-1, keepdims=True)
            d1 = jnp.sum(jnp.where(h0, 0.0, prod), axis=-1, keepdims=True)
            dl[rows, :] = jnp.where(h0, d0, d1)
            return carry

        lax.fori_loop(0, S // 256, delta, 0)

        for br in range(nbr):
            nblk = _branch_geometry(br)[2]

            def stage(src, off, rows):
                qs0[pl.ds(off, rows), :], qs1[pl.ds(off, rows), :] = split_q(qf[src, :])
                ds0[pl.ds(off, rows), :], ds1[pl.ds(off, rows), :] = split_do(do_ref[src, :])
                ks[pl.ds(PADK + off, rows), :] = kf[src, :].astype(BF16)
                vs[pl.ds(PADK + off, rows), :] = vf[src, :].astype(BF16)

            _sub_layout_loop(br, stage)
            dks[...] = jnp.zeros_like(dks)
            dvs[...] = jnp.zeros_like(dvs)

            def blk(i, carry, br=br, nblk=nblk):
                base = pl.multiple_of(i * BLK, BLK)
                rows, n = _token_rows(br, i)
                q01 = jnp.concatenate([qs0[pl.ds(base, BLK), :], qs1[pl.ds(base, BLK), :]], axis=0)
                do01 = jnp.concatenate([ds0[pl.ds(base, BLK), :], ds1[pl.ds(base, BLK), :]], axis=0)
                lse_b = lse_ref[rows, :]
                dl_b = dl[rows, :]
                lse01 = jnp.concatenate([lse_b[:, 0:1], lse_b[:, HD:HD + 1]], axis=0)
                dl01 = jnp.concatenate([dl_b[:, 0:1], dl_b[:, HD:HD + 1]], axis=0)
                if nblk > 1:
                    krows = pl.ds(base, 2 * BLK)
                    bias_m = bm[br, jnp.minimum(n, 1)]
                else:
                    krows = pl.ds(PADK + base, BLK)
                    bias_m = bm[br, 0, :, BLK:]
                kcat = ks[krows, :]
                vcat = vs[krows, :]
                p = jnp.exp(_dot_nt(q01, kcat) + bias_m - lse01)
                dsv = p * (_dot_nt(do01, vcat) - dl01)
                if nblk > 1:
                    db_ref[br, 0] += dsv[:BLK]
                    db_ref[br, 1] += dsv[BLK:]
                else:
                    db_ref[br, 0, :, BLK:] += dsv[:BLK]
                    db_ref[br, 1, :, BLK:] += dsv[BLK:]
                dsb = dsv.astype(BF16)
                dq01 = _dot(dsb, kcat)
                dqa[rows, :] = dqa[rows, :] + jnp.where(head0, dq01[:BLK], dq01[BLK:])
                dks[krows, :] = dks[krows, :] + _dot_tn(dsb, q01)
                dvs[krows, :] = dvs[krows, :] + _dot_tn(p.astype(BF16), do01)
                return carry

            lax.fori_loop(0, 16, blk, 0, unroll=ATTN_UNROLL)

            def fold(src, off, rows):
                dka[src, :] = dka[src, :] + dks[pl.ds(PADK + off, rows), :]
                dva[src, :] = dva[src, :] + dvs[pl.ds(PADK + off, rows), :]

            _sub_layout_loop(br, fold)

        def flush(i, carry):
            rows = pl.ds(pl.multiple_of(i * 256, 256), 256)
            for acc, out, cs, mul in ((dqa, dq_ref, sq_ref, SCALE), (dka, dk_ref, sk_ref, 1.0), (dva, dv_ref, sv_ref, 1.0)):
                val = acc[rows, :] * mul
                out[rows, :] = val.astype(BF16)
                cs[...] += _colsum(val)
            return carry

        lax.fori_loop(0, S // 256, flush, 0)

    npair = NH // 2
    blk_spec = lambda off: pl.BlockSpec((S, BLK), lambda hp, b: (b, off + hp))
    sum_spec = pl.BlockSpec((1, BLK), lambda hp, b: (0, hp))
    return pl.pallas_call(
        body, name="attn_bwd", grid=(npair, BL),
        in_specs=[blk_spec(0), blk_spec(npair), blk_spec(2 * npair), blk_spec(0), blk_spec(0), blk_spec(0),
                  pl.BlockSpec((nbr, 2, BLK, 2 * BLK), lambda hp, b: (0, hp, 0, 0))],
        out_specs=(blk_spec(0), blk_spec(0), blk_spec(0), sum_spec, sum_spec, sum_spec,
                   pl.BlockSpec((nbr, 2, BLK, 2 * BLK), lambda hp, b: (0, hp, 0, 0))),
        out_shape=(SDS((T, AW), BF16), SDS((T, AW), BF16), SDS((T, AW), BF16),
                   SDS((1, AW), F32), SDS((1, AW), F32), SDS((1, AW), F32),
                   SDS((nbr, NH, BLK, 2 * BLK), F32)),
        scratch_shapes=[pltpu.VMEM((S, BLK), F32)] * 7 + [pltpu.VMEM((S, BLK), BF16)] * 4
        + [pltpu.VMEM((PADK + S, BLK), BF16)] * 2 + [pltpu.VMEM((PADK + S, BLK), F32)] * 2
        + [pltpu.VMEM((nbr, 2, 2 * BLK, 2 * BLK), F32)],
    )(qkv, qkv, qkv, attn, lse, dattn, bias)


CH = 256
PADR = 32


def _conv_fwd(ag, conv_w, conv_b):
    def body(ag_ref, w_ref, b_ref, u1_ref, u0p):
        u0p[pl.ds(0, PADR), :] = jnp.zeros((PADR, CW), F32)

        def glu(i, carry):
            t0 = pl.multiple_of(i * CH, CH)
            a = ag_ref[pl.ds(t0, CH), :CW]
            g = ag_ref[pl.ds(t0, CH), CW:]
            u0p[pl.ds(PADR + t0, CH), :] = a * _sigmoid(g)
            return carry

        lax.fori_loop(0, S // CH, glu, 0)

        def conv(i, carry):
            t0 = pl.multiple_of(i * CH, CH)
            win = u0p[pl.ds(t0, CH + PADR), :]
            acc = jnp.zeros((CH, CW), F32) + b_ref[...]
            for k in range(CK):
                off = PADR - (CK - 1) + k
                acc = acc + win[off:off + CH, :] * w_ref[k:k + 1, :]
            u1_ref[pl.ds(t0, CH), :] = acc
            return carry

        lax.fori_loop(0, S // CH, conv, 0)

    return pl.pallas_call(
        body, name="conv_fwd", grid=(BL,),
        in_specs=[pl.BlockSpec((S, 2 * CW), lambda b: (b, 0)),
                  pl.BlockSpec((CK, CW), lambda b: (0, 0)),
                  pl.BlockSpec((1, CW), lambda b: (0, 0))],
        out_specs=pl.BlockSpec((S, CW), lambda b: (b, 0)),
        out_shape=SDS((T, CW), F32),
        scratch_shapes=[pltpu.VMEM((S + PADR, CW), F32)],
    )(ag, conv_w, conv_b)


def _conv_post(u1, cg, cb):
    mu = _rowmean(u1)
    uc = u1 - mu
    rstd = lax.rsqrt(_rowmean(uc * uc) + LN_EPS)
    xh = uc * rstd
    u2 = xh * cg + cb
    sg = _sigmoid(u2)
    return xh, rstd, u2, sg, u2 * sg


def _mix_fwd(attn, u1, ga, gc, cg, cb):
    def body(a_ref, u_ref, ga_ref, gc_ref, cg_ref, cb_ref, o_ref):
        a = a_ref[...]
        ra = lax.rsqrt(_rowmean(a * a) + LN_EPS)
        o_ref[:, :AW] = (a * ra * ga_ref[...]).astype(BF16)
        _, _, _, _, u3 = _conv_post(u_ref[...], cg_ref[...], cb_ref[...])
        rc = lax.rsqrt(_rowmean(u3 * u3) + LN_EPS)
        o_ref[:, AW:] = (u3 * rc * gc_ref[...]).astype(BF16)

    vec = lambda w: pl.BlockSpec((1, w), lambda m: (0, 0))
    return pl.pallas_call(
        body, name="mix_fwd", grid=(T // TM,),
        in_specs=[pl.BlockSpec((TM, AW), lambda m: (m, 0)), pl.BlockSpec((TM, CW), lambda m: (m, 0)),
                  vec(AW), vec(CW), vec(CW), vec(CW)],
        out_specs=pl.BlockSpec((TM, D), lambda m: (m, 0)),
        out_shape=SDS((T, D), BF16),
    )(attn, u1, ga, gc, cg, cb)


def _mix_bwd(dmixed, attn, u1, ga, gc, cg, cb):
    def body(dm_ref, a_ref, u_ref, ga_ref, gc_ref, cg_ref, cb_ref,
             da_ref, du_ref, g_an, g_cn, g_lg, g_lb, g_cb):
        @pl.when(pl.program_id(0) == 0)
        def _():
            for r in (g_an, g_cn, g_lg, g_lb, g_cb):
                r[...] = jnp.zeros_like(r)

        a = a_ref[...]
        dna = dm_ref[:, :AW]
        ra = lax.rsqrt(_rowmean(a * a) + LN_EPS)
        g_an[...] += _colsum(dna * a * ra)
        dat = dna * ga_ref[...]
        da_ref[...] = ra * dat - a * (ra * ra * ra) * _rowmean(dat * a)

        xh, rstd, u2, sg, u3 = _conv_post(u_ref[...], cg_ref[...], cb_ref[...])
        dnc = dm_ref[:, AW:]
        rc = lax.rsqrt(_rowmean(u3 * u3) + LN_EPS)
        g_cn[...] += _colsum(dnc * u3 * rc)
        dut = dnc * gc_ref[...]
        du3 = rc * dut - u3 * (rc * rc * rc) * _rowmean(dut * u3)
        du2 = du3 * sg * (1.0 + u2 * (1.0 - sg))
        g_lg[...] += _colsum(du2 * xh)
        g_lb[...] += _colsum(du2)
        dxh = du2 * cg_ref[...]
        du1 = rstd * (dxh - _rowmean(dxh) - xh * _rowmean(dxh * xh))
        g_cb[...] += _colsum(du1)
        du_ref[...] = du1

    vec = lambda w: pl.BlockSpec((1, w), lambda m: (0, 0))
    return pl.pallas_call(
        body, name="mix_bwd", grid=(T // TM,),
        in_specs=[pl.BlockSpec((TM, D), lambda m: (m, 0)), pl.BlockSpec((TM, AW), lambda m: (m, 0)),
                  pl.BlockSpec((TM, CW), lambda m: (m, 0)), vec(AW), vec(CW), vec(CW), vec(CW)],
        out_specs=(pl.BlockSpec((TM, AW), lambda m: (m, 0)), pl.BlockSpec((TM, CW), lambda m: (m, 0)),
                   vec(AW), vec(CW), vec(CW), vec(CW), vec(CW)),
        out_shape=(SDS((T, AW), F32), SDS((T, CW), F32),
                   SDS((1, AW), F32), SDS((1, CW), F32), SDS((1, CW), F32), SDS((1, CW), F32), SDS((1, CW), F32)),
    )(dmixed, attn, u1, ga, gc, cg, cb)


def _conv_bwd(du1, ag, conv_w):
    def body(du_ref, ag_ref, w_ref, dag_ref, cs_ref, gw_ref, u0p, dup):
        @pl.when(pl.program_id(0) == 0)
        def _():
            cs_ref[...] = jnp.zeros_like(cs_ref)
            gw_ref[...] = jnp.zeros_like(gw_ref)

        u0p[pl.ds(0, PADR), :] = jnp.zeros((PADR, CW), F32)
        dup[pl.ds(S, PADR), :] = jnp.zeros((PADR, CW), F32)

        def fill(i, carry):
            t0 = pl.multiple_of(i * CH, CH)
            a = ag_ref[pl.ds(t0, CH), :CW]
            g = ag_ref[pl.ds(t0, CH), CW:]
            u0p[pl.ds(PADR + t0, CH), :] = a * _sigmoid(g)
            dup[pl.ds(t0, CH), :] = du_ref[pl.ds(t0, CH), :]
            return carry

        lax.fori_loop(0, S // CH, fill, 0)

        def chunk(i, carry):
            t0 = pl.multiple_of(i * CH, CH)
            d = dup[pl.ds(t0, CH), :]
            win_u = u0p[pl.ds(t0, CH + PADR), :]
            win_d = dup[pl.ds(t0, CH + PADR), :]
            du0 = jnp.zeros((CH, CW), F32)
            for k in range(CK):
                off = PADR - (CK - 1) + k
                gw_ref[k:k + 1, :] += _colsum(d * win_u[off:off + CH, :])
                fo = CK - 1 - k
                du0 = du0 + win_d[fo:fo + CH, :] * w_ref[k:k + 1, :]
            a = ag_ref[pl.ds(t0, CH), :CW]
            sg = _sigmoid(ag_ref[pl.ds(t0, CH), CW:])
            da = du0 * sg
            dg = du0 * a * sg * (1.0 - sg)
            dag_ref[pl.ds(t0, CH), :CW] = da.astype(BF16)
            dag_ref[pl.ds(t0, CH), CW:] = dg.astype(BF16)
            cs_ref[:, :CW] += _colsum(da)
            cs_ref[:, CW:] += _colsum(dg)
            return carry

        lax.fori_loop(0, S // CH, chunk, 0)

    return pl.pallas_call(
        body, name="conv_bwd", grid=(BL,),
        in_specs=[pl.BlockSpec((S, CW), lambda b: (b, 0)), pl.BlockSpec((S, 2 * CW), lambda b: (b, 0)),
                  pl.BlockSpec((CK, CW), lambda b: (0, 0))],
        out_specs=(pl.BlockSpec((S, 2 * CW), lambda b: (b, 0)),
                   pl.BlockSpec((1, 2 * CW), lambda b: (0, 0)),
                   pl.BlockSpec((PADR, CW), lambda b: (0, 0))),
        out_shape=(SDS((T, 2 * CW), BF16), SDS((1, 2 * CW), F32), SDS((PADR, CW), F32)),
        scratch_shapes=[pltpu.VMEM((S + PADR, CW), F32), pltpu.VMEM((S + PADR, CW), F32)],
    )(du1, ag, conv_w)


def _layer_norm_fwd(z):
    mu = _rowmean(z)
    zc = z - mu
    rstd = lax.rsqrt(_rowmean(zc * zc) + LN_EPS)
    return zc * rstd, rstd


def _layer_norm_bwd(dy, xh, rstd, g):
    dxh = dy * g
    return rstd * (dxh - _rowmean(dxh) - xh * _rowmean(dxh * xh))


def _out_proj_ln1(mixed, w_out, x2, g1, b1):
    def body(a_ref, w_ref, x_ref, g_ref, b_ref, xh_ref, rstd_ref, x1_ref):
        z = ALPHA * x_ref[...] + _dot(a_ref[...], w_ref[...])
        xh, rstd = _layer_norm_fwd(z)
        xh_ref[...] = xh
        rstd_ref[...] = rstd
        x1_ref[...] = (xh * g_ref[...] + b_ref[...]).astype(BF16)

    vec = pl.BlockSpec((1, D), lambda m: (0, 0))
    row = pl.BlockSpec((TM, D), lambda m: (m, 0))
    return pl.pallas_call(
        body, name="out_proj_ln1", grid=(T // TM,),
        in_specs=[row, pl.BlockSpec((D, D), lambda m: (0, 0)), row, vec, vec],
        out_specs=(row, pl.BlockSpec((TM, 1), lambda m: (m, 0)), row),
        out_shape=(SDS((T, D), F32), SDS((T, 1), F32), SDS((T, D), BF16)),
    )(mixed, w_out, x2, g1, b1)


def _seq_start(m):
    return lax.bitwise_and(m, S // TM - 1) == 0


def _causal3(ext, w_ref, b_ref):
    x0 = ext[pl.ds(8, TM), :]
    x1 = ext[pl.ds(7, TM), :]
    x2 = ext[pl.ds(6, TM), :]
    y = w_ref[2:3, :] * x0 + w_ref[1:2, :] * x1 + w_ref[0:1, :] * x2 + b_ref[...]
    return y, x0, x1, x2


def _shift_down(x, before, k):
    rolled = pltpu.roll(x, k, 0)
    row = lax.broadcasted_iota(jnp.int32, before.shape, 0)
    head = jnp.where(row < k, pltpu.roll(before, k, 0), rolled[:8])
    return jnp.concatenate([head, rolled[8:]], axis=0)


def _shift_up(x, after, k):
    n = x.shape[0]
    rolled = pltpu.roll(x, n - k, 0)
    row = lax.broadcasted_iota(jnp.int32, after.shape, 0)
    tail = jnp.where(row >= 8 - k, pltpu.roll(after, 8 - k, 0), rolled[n - 8:])
    return jnp.concatenate([rolled[:n - 8], tail], axis=0)


def _ffn_up(x1b, w_up, fcw, fcb):
    def body(x_ref, wg_ref, wv_ref, cwg_ref, cwv_ref, cbg_ref, cbv_ref, up_ref, gv_ref, act_ref, prev_g, prev_v):
        @pl.when(_seq_start(pl.program_id(1)))
        def _():
            prev_g[...] = jnp.zeros_like(prev_g)
            prev_v[...] = jnp.zeros_like(prev_v)

        x = x_ref[...]
        outs = []
        for w_ref, cw_ref, cb_ref, prev, lo in ((wg_ref, cwg_ref, cbg_ref, prev_g, 0), (wv_ref, cwv_ref, cbv_ref, prev_v, FT)):
            ub = _dot(x, w_ref[...]).astype(BF16)
            up_ref[:, lo:lo + FT] = ub
            u = ub.astype(F32)
            before = prev[...]
            y = (cw_ref[2:3, :] * u + cw_ref[1:2, :] * _shift_down(u, before, 1)
                 + cw_ref[0:1, :] * _shift_down(u, before, 2) + cb_ref[...])
            prev[...] = u[TM - 8:]
            yb = y.astype(BF16)
            gv_ref[:, lo:lo + FT] = yb
            outs.append(yb.astype(F32))
        gate, val = outs
        act_ref[...] = (gate * _sigmoid(gate) * val).astype(BF16)

    wspec = lambda off: pl.BlockSpec((D, FT), lambda n, m: (0, n + off))
    cwspec = lambda off: pl.BlockSpec((FK, FT), lambda n, m: (0, n + off))
    cbspec = lambda off: pl.BlockSpec((1, FT), lambda n, m: (0, n + off))
    pair = pl.BlockSpec((TM, 2 * FT), lambda n, m: (m, n))
    return pl.pallas_call(
        body, name="ffn_up", grid=(NFT, T // TM),
        in_specs=[pl.BlockSpec((TM, D), lambda n, m: (m, 0)), wspec(0), wspec(NFT),
                  cwspec(0), cwspec(NFT), cbspec(0), cbspec(NFT)],
        out_specs=(pair, pair, pl.BlockSpec((TM, FT), lambda n, m: (m, n))),
        out_shape=(SDS((T, 2 * DFF), BF16), SDS((T, 2 * DFF), BF16), SDS((T, DFF), BF16)),
        scratch_shapes=[pltpu.VMEM((8, FT), F32)] * 2,
    )(x1b, w_up, w_up, fcw, fcw, fcb, fcb)


def _ffn_down_loss(act, w_down, xh1, g1, b1, g2, b2, target):
    def body(a_ref, w_ref, xh1_ref, g1_ref, b1_ref, g2_ref, b2_ref, t_ref, dz_ref, loss_ref, gg_ref, gb_ref):
        @pl.when(pl.program_id(0) == 0)
        def _():
            loss_ref[...] = jnp.zeros_like(loss_ref)
            gg_ref[...] = jnp.zeros_like(gg_ref)
            gb_ref[...] = jnp.zeros_like(gb_ref)

        x1 = xh1_ref[...] * g1_ref[...] + b1_ref[...]
        z = ALPHA * x1 + _dot(a_ref[...], w_ref[...])
        xh, rstd = _layer_norm_fwd(z)
        diff = xh * g2_ref[...] + b2_ref[...] - t_ref[...]
        loss_ref[...] += 0.5 * _colsum(_rowmean(diff * diff))
        dout = diff * (1.0 / D)
        gg_ref[...] += _colsum(dout * xh)
        gb_ref[...] += _colsum(dout)
        dz_ref[...] = _layer_norm_bwd(dout, xh, rstd, g2_ref[...])

    vec = pl.BlockSpec((1, D), lambda m: (0, 0))
    row = pl.BlockSpec((TM, D), lambda m: (m, 0))
    return pl.pallas_call(
        body, name="ffn_down_loss", grid=(T // TM,),
        in_specs=[pl.BlockSpec((TM, DFF), lambda m: (m, 0)), pl.BlockSpec((DFF, D), lambda m: (0, 0)),
                  row, vec, vec, vec, vec, row],
        out_specs=(row, pl.BlockSpec((1, 1), lambda m: (0, 0)), vec, vec),
        out_shape=(SDS((T, D), F32), SDS((1, 1), F32), SDS((1, D), F32), SDS((1, D), F32)),
    )(act, w_down, xh1, g1, b1, g2, b2, target)


def _ffn_down_bwd(dz2, w_down, gv):
    def body(dz_ref, wd_ref, gv_ref, dup_ref, csg_ref, csv_ref):
        @pl.when(pl.program_id(1) == 0)
        def _():
            csg_ref[...] = jnp.zeros_like(csg_ref)
            csv_ref[...] = jnp.zeros_like(csv_ref)

        dact = _dot_nt(dz_ref[...].astype(BF16), wd_ref[...])
        gate = gv_ref[:, :FT].astype(F32)
        val = gv_ref[:, FT:].astype(F32)
        sg = _sigmoid(gate)
        gs = gate * sg
        dgate = dact * val * (sg + gs * (1.0 - sg))
        dval = dact * gs
        dup_ref[:, :FT] = dgate.astype(BF16)
        dup_ref[:, FT:] = dval.astype(BF16)
        csg_ref[...] += _colsum(dgate)
        csv_ref[...] += _colsum(dval)

    cs = pl.BlockSpec((1, FT), lambda n, m: (0, n))
    pair = pl.BlockSpec((TM, 2 * FT), lambda n, m: (m, n))
    return pl.pallas_call(
        body, name="ffn_down_bwd", grid=(NFT, T // TM),
        in_specs=[pl.BlockSpec((TM, D), lambda n, m: (m, 0)), pl.BlockSpec((FT, D), lambda n, m: (n, 0)), pair],
        out_specs=(pair, cs, cs),
        out_shape=(SDS((T, 2 * DFF), BF16), SDS((1, DFF), F32), SDS((1, DFF), F32)),
    )(dz2, w_down, gv)


HALO = 16


def _conv3_transpose(dup, up, fcw_il):
    tiles = T // TM

    def body(d_ref, h_ref, u_ref, w_ref, o_ref, gw_ref):
        m = pl.program_id(1)

        @pl.when(m == 0)
        def _():
            gw_ref[...] = jnp.zeros_like(gw_ref)

        d0 = d_ref[...].astype(F32)
        last = lax.bitwise_and(m + 1, S // TM - 1) == 0
        after = jnp.where(last, 0.0, h_ref[...].astype(F32)[:8])
        d1 = _shift_up(d0, after, 1)
        d2 = _shift_up(d0, after, 2)
        o_ref[...] = (w_ref[2:3, :] * d0 + w_ref[1:2, :] * d1 + w_ref[0:1, :] * d2).astype(BF16)
        u = u_ref[...].astype(F32)
        for k, dk in enumerate((d2, d1, d0)):
            gw_ref[k:k + 1, :] += _colsum(dk * u)

    pair = pl.BlockSpec((TM, 2 * FT), lambda n, m: (m, n))
    return pl.pallas_call(
        body, name="conv3_transpose", grid=(NFT, tiles),
        in_specs=[pair,
                  pl.BlockSpec((HALO, 2 * FT), lambda n, m: (jnp.minimum((m + 1) * (TM // HALO), T // HALO - 1), n)),
                  pair, pl.BlockSpec((FK, 2 * FT), lambda n, m: (0, n))],
        out_specs=(pair, pl.BlockSpec((FK, 2 * FT), lambda n, m: (0, n))),
        out_shape=(SDS((T, 2 * DFF), BF16), SDS((FK, 2 * DFF), F32)),
    )(dup, dup, up, fcw_il)


def _ffn_up_bwd_ln1(dpre, w_up, dz2, xh1, rstd1, g1):
    def body(a_ref, wg_ref, wv_ref, dz2_ref, xh_ref, rstd_ref, g_ref, dz1_ref, gg_ref, gb_ref, acc):
        m, kk = pl.program_id(0), pl.program_id(1)

        @pl.when(kk == 0)
        def _():
            acc[...] = jnp.zeros_like(acc)

        acc[...] += _dot_nt(a_ref[:, :FT], wg_ref[...]) + _dot_nt(a_ref[:, FT:], wv_ref[...])

        @pl.when(kk == NFT - 1)
        def _():
            @pl.when(m == 0)
            def _():
                gg_ref[...] = jnp.zeros_like(gg_ref)
                gb_ref[...] = jnp.zeros_like(gb_ref)

            dx1 = acc[...] + ALPHA * dz2_ref[...]
            xh = xh_ref[...]
            gg_ref[...] += _colsum(dx1 * xh)
            gb_ref[...] += _colsum(dx1)
            dz1_ref[...] = _layer_norm_bwd(dx1, xh, rstd_ref[...], g_ref[...])

    vec = pl.BlockSpec((1, D), lambda m, k: (0, 0))
    row = pl.BlockSpec((TM, D), lambda m, k: (m, 0))
    wspec = lambda off: pl.BlockSpec((D, FT), lambda m, k: (0, k + off))
    return pl.pallas_call(
        body, name="ffn_up_bwd_ln1", grid=(T // TM, NFT),
        in_specs=[pl.BlockSpec((TM, 2 * FT), lambda m, k: (m, k)), wspec(0), wspec(NFT),
                  row, row, pl.BlockSpec((TM, 1), lambda m, k: (m, 0)), vec],
        out_specs=(row, vec, vec),
        out_shape=(SDS((T, D), F32), SDS((1, D), F32), SDS((1, D), F32)),
        scratch_shapes=[pltpu.VMEM((TM, D), F32)],
    )(dpre, w_up, w_up, dz2, xh1, rstd1, g1)


def _grad_w_up(dpre, x1b):
    tk = 1024

    def body(a_ref, b_ref, o_ref, acc):
        k = pl.program_id(1)

        @pl.when(k == 0)
        def _():
            acc[...] = jnp.zeros_like(acc)

        acc[...] += _dot_tn(a_ref[...], b_ref[...])

        @pl.when(k == T // tk - 1)
        def _():
            o_ref[0] = acc[pl.ds(0, FT), :].astype(o_ref.dtype)
            o_ref[1] = acc[pl.ds(FT, FT), :].astype(o_ref.dtype)

    out = pl.pallas_call(
        body, name="grad_w_up", grid=(NFT, T // tk),
        in_specs=[pl.BlockSpec((tk, 2 * FT), lambda n, k: (k, n)), pl.BlockSpec((tk, D), lambda n, k: (k, 0))],
        out_specs=pl.BlockSpec((2, FT, D), lambda n, k: (0, n, 0)),
        out_shape=SDS((2, DFF, D), GRAD_WIRE),
        scratch_shapes=[pltpu.VMEM((2 * FT, D), F32)],
    )(dpre, x1b)
    return out.reshape(2 * DFF, D)


def _row_tile(rows, cols):
    if rows * cols * 4 <= (1 << 20) or rows % 8:
        return rows
    for t in (256, 176, 128, 88, 64, 32, 16, 8):
        if rows % t == 0 and t * cols * 4 <= (1 << 20):
            return t
    return 8


def _sum8(r, name):
    _, rows, cols = r.shape
    tr = _row_tile(rows, cols)

    def body(r_ref, o_ref):
        acc = r_ref[0].astype(F32)
        for p in range(1, NDEV):
            acc = acc + r_ref[p].astype(F32)
        o_ref[...] = acc

    return pl.pallas_call(
        body, name=name, grid=(rows // tr,),
        in_specs=[pl.BlockSpec((NDEV, tr, cols), lambda i: (0, i, 0))],
        out_specs=pl.BlockSpec((tr, cols), lambda i: (i, 0)),
        out_shape=SDS((rows, cols), F32),
    )(r)


def _adamw(w, g, m, v, name):
    rows, cols = w.shape
    tr = _row_tile(rows, cols)

    def body(w_ref, g_ref, m_ref, v_ref, d_ref, nm_ref, nv_ref):
        g_ = g_ref[...]
        m_ = B1 * m_ref[...] + (1.0 - B1) * g_
        v_ = B2 * v_ref[...] + (1.0 - B2) * jnp.square(g_)
        m_hat = m_ / (1.0 - B1 ** STEP)
        v_hat = v_ / (1.0 - B2 ** STEP)
        d_ref[...] = -LR * (m_hat / (jnp.sqrt(v_hat) + AEPS) + WD * w_ref[...])
        nm_ref[...] = m_
        nv_ref[...] = v_

    spec = pl.BlockSpec((tr, cols), lambda i: (i, 0))
    shp = SDS((rows, cols), F32)
    return pl.pallas_call(
        body, name=name, grid=(rows // tr,), in_specs=[spec] * 4, out_specs=(spec,) * 3,
        out_shape=(shp, shp, shp),
    )(w, g, m, v)


def _interleave(a):
    r = a.shape[0]
    return a.reshape(r, 2, NFT, FT).transpose(0, 2, 1, 3).reshape(r, 2 * DFF)


def _deinterleave(a):
    r = a.shape[0]
    return a.reshape(r, NFT, 2, FT).transpose(0, 2, 1, 3).reshape(r, 2 * DFF)


def _local_step(x2, target, rel_table, w_in, b_in, conv_w, conv_b, conv_ln_g, conv_ln_b, attn_norm_g,
                conv_norm_g, late_weights, ln1_g, ln1_b, ffn_conv_w, ffn_conv_b, ln2_g, ln2_b, ship_ffn_grads):
    buckets = jnp.asarray(_bucket_maps())
    bias = _bias_table(rel_table, buckets)

    qkv = _mm_nn_bias(x2, w_in, b_in, 0, 3, AW, BF16, "proj_qkv")
    ag = _mm_nn_bias(x2, w_in, b_in, 3 * AW // CW, 2, CW, F32, "proj_ag")
    attn, lse = _attn_fwd(qkv, bias)
    u1 = _conv_fwd(ag, conv_w, conv_b)
    mixed = _mix_fwd(attn, u1, attn_norm_g, conv_norm_g, conv_ln_g, conv_ln_b)
    w_out, w_up, w_down = late_weights(mixed)
    xh1, rstd1, x1b = _out_proj_ln1(mixed, w_out, x2, ln1_g, ln1_b)
    up, gv, act = _ffn_up(x1b, w_up, ffn_conv_w, ffn_conv_b)
    dz2, loss, g_ln2_g, g_ln2_b = _ffn_down_loss(act, w_down, xh1, ln1_g, ln1_b, ln2_g, ln2_b, target)

    dup, cs_g, cs_v = _ffn_down_bwd(dz2, w_down, gv)
    g_w_down = _mm_tn(act, dz2, DFF // 2, 512, "grad_w_down")
    dpre, gfw_il = _conv3_transpose(dup, up, _interleave(ffn_conv_w))
    dz1, g_ln1_g, g_ln1_b = _ffn_up_bwd_ln1(dpre, w_up, dz2, xh1, rstd1, ln1_g)
    zero = ship_ffn_grads(g_w_down, _grad_w_up(dpre, x1b))
    dmixed = _mm_nt(dz1, w_out, None, 0.0, "dmixed")
    g_w_out = _mm_tn(mixed, dz1, D, 512, "grad_w_out")
    dattn, du1, g_an, g_cn, g_clg, g_clb, g_cb = _mix_bwd(
        dmixed, attn, u1, attn_norm_g + zero, conv_norm_g, conv_ln_g, conv_ln_b)
    dag, cs_ag, g_conv_w = _conv_bwd(du1, ag, conv_w)
    dq, dk, dv, cs_q, cs_k, cs_v2, dbias = _attn_bwd(qkv, attn, lse, dattn, bias)
    g_rel = _rel_table_grad(dbias, buckets)
    dh = jnp.concatenate([dq, dk, dv, dag], axis=1)
    grad_x = _mm_nt(dh, w_in, dz1, ALPHA, "grad_x")
    g_w_in_t = _mm_tn(dh, x2, INW // 2, 512, "grad_w_in")

    grads = dict(
        rel_table=g_rel,
        b_in=jnp.concatenate([cs_q, cs_k, cs_v2, cs_ag], axis=1),
        conv_b=g_cb, conv_ln_g=g_clg, conv_ln_b=g_clb, attn_norm_g=g_an, conv_norm_g=g_cn,
        ln1_g=g_ln1_g, ln1_b=g_ln1_b,
        ffn_conv_b=jnp.concatenate([cs_g, cs_v], axis=1),
        ln2_g=g_ln2_g, ln2_b=g_ln2_b,
        conv_w=g_conv_w[:CK],
        ffn_conv_w=_deinterleave(gfw_il),
        w_in_t=g_w_in_t, w_out=g_w_out,
    )
    return loss, grad_x, grads


SMALL = (("rel_table", (NBUCKET, NH)), ("b_in", (1, INW)), ("conv_b", (1, CW)), ("conv_ln_g", (1, CW)),
         ("conv_ln_b", (1, CW)), ("attn_norm_g", (1, AW)), ("conv_norm_g", (1, CW)), ("ln1_g", (1, D)),
         ("ln1_b", (1, D)), ("ffn_conv_b", (1, 2 * DFF)), ("ln2_g", (1, D)), ("ln2_b", (1, D)))
SHARDED_SMALL = (("conv_w", (CK, CW)), ("ffn_conv_w", (FK, 2 * DFF)))


def _pack(parts):
    flat = jnp.concatenate([p.reshape(-1) for p in parts])
    tile = 8 * PACK_LANES
    pad = (-flat.shape[0]) % tile
    return jnp.pad(flat, (0, pad)).reshape(-1, PACK_LANES)


def _unpack(packed, specs):
    flat = packed.reshape(-1)
    out, off = {}, 0
    for name, shp in specs:
        size = int(np.prod(shp))
        out[name] = flat[off:off + size].reshape(shp)
        off += size
    return out


def kernel(x, rel_table, w_in, b_in, conv_w, conv_b, conv_ln_g, conv_ln_b, attn_norm_g, conv_norm_g, w_out, ln1_g, ln1_b, w_up, ffn_conv_w, ffn_conv_b, w_down, ln2_g, ln2_b, loss_target, m_rel_table, m_w_in, m_b_in, m_conv_w, m_conv_b, m_conv_ln_g, m_conv_ln_b, m_attn_norm_g, m_conv_norm_g, m_w_out, m_ln1_g, m_ln1_b, m_w_up, m_ffn_conv_w, m_ffn_conv_b, m_w_down, m_ln2_g, m_ln2_b, v_rel_table, v_w_in, v_b_in, v_conv_w, v_conv_b, v_conv_ln_g, v_conv_ln_b, v_attn_norm_g, v_conv_norm_g, v_w_out, v_ln1_g, v_ln1_b, v_w_up, v_ffn_conv_w, v_ffn_conv_b, v_w_down, v_ln2_g, v_ln2_b):
    given = dict(locals())
    me = 4 * lax.axis_index("x") + 2 * lax.axis_index("y") + lax.axis_index("c")

    cols = lambda a: a.transpose(1, 0, 2).reshape(a.shape[1], NDEV * a.shape[2])
    rows = lambda a: a.reshape(NDEV * a.shape[1], a.shape[2])
    stack = lambda a: a.reshape(NDEV, a.shape[0] // NDEV, a.shape[1])

    first = _exchange([(w_in[0].astype(BF16), "gather"), (conv_w[0], "gather"), (ffn_conv_w[0], "gather")],
                      "gather_first")
    w_in_f, conv_w_f, ffn_conv_w_f = cols(first[0]), cols(first[1]), cols(first[2])
    late_own = [w_out[0].astype(BF16), w_up[0].astype(BF16), w_down[0].astype(BF16)]
    late_state, zero1 = _exchange_start([(a, "gather") for a in late_own], "gather_late_start")

    def late_weights(after):
        lands = _exchange_wait(late_state, after, "gather_late_wait")
        w_out_g, w_up_g, w_down_g = [_own_slot(l, o, me) for l, o in zip(lands, late_own)]
        return rows(w_out_g), cols(w_up_g), rows(w_down_g)

    shipped = {}

    def ship_ffn_grads(g_w_down, g_w_up_t):
        shipped["own"] = [stack(g_w_down), stack(g_w_up_t)]
        shipped["state"], zero2 = _exchange_start([(a, "scatter") for a in shipped["own"]], "ffn_grads_start")
        return zero2

    loss, grad_x, g = _local_step(
        x.reshape(T, D), loss_target.reshape(T, D), rel_table, w_in_f, b_in + zero1, conv_w_f, conv_b, conv_ln_g,
        conv_ln_b, attn_norm_g, conv_norm_g, late_weights, ln1_g, ln1_b, ffn_conv_w_f, ffn_conv_b,
        ln2_g, ln2_b, ship_ffn_grads)

    small_specs = SMALL + SHARDED_SMALL
    packed = _pack([g[n] for n, _ in small_specs])
    got = _exchange([(stack(g["w_in_t"]), "scatter"), (stack(g["w_out"]), "scatter"), (packed, "gather")],
                    "exchange_grads")
    ffn_lands = _exchange_wait(shipped["state"], got[2], "ffn_grads_wait")
    got_down, got_up = [_own_slot(l, lax.dynamic_index_in_dim(o, me, 0, keepdims=False), me)
                        for l, o in zip(ffn_lands, shipped["own"])]

    grad = {}
    grad["w_in"] = _sum8(got[0], "sum_w_in").T[None]
    grad["w_out"] = _sum8(got[1], "sum_w_out")[None]
    grad["w_up"] = _sum8(got_up, "sum_w_up").T[None]
    grad["w_down"] = _sum8(got_down, "sum_w_down")[None]
    small = _unpack(_sum8(got[2], "sum_small"), small_specs)
    for n, _ in SMALL:
        grad[n] = small[n]
    grad["conv_w"] = lax.dynamic_slice_in_dim(small["conv_w"], me * (CW // NDEV), CW // NDEV, axis=1)[None]
    grad["ffn_conv_w"] = lax.dynamic_slice_in_dim(small["ffn_conv_w"], me * (2 * DFF // NDEV), 2 * DFF // NDEV, axis=1)[None]

    delta, new_m, new_v = {}, {}, {}
    for n in ("w_in", "w_out", "w_up", "w_down", "conv_w", "ffn_conv_w"):
        shp = given[n].shape
        two = lambda a: a.reshape(shp[-2], shp[-1])
        d_, m_, v_ = _adamw(two(given[n]), two(grad[n]), two(given["m_" + n]), two(given["v_" + n]), "adamw_" + n)
        delta[n], new_m[n], new_v[n] = d_.reshape(shp), m_.reshape(shp), v_.reshape(shp)
    wp = _pack([given[n] for n, _ in SMALL])
    gp = _pack([grad[n] for n, _ in SMALL])
    mp = _pack([given["m_" + n] for n, _ in SMALL])
    vp = _pack([given["v_" + n] for n, _ in SMALL])
    dp, nmp, nvp = _adamw(wp, gp, mp, vp, "adamw_small")
    for dst, src in ((delta, dp), (new_m, nmp), (new_v, nvp)):
        dst.update(_unpack(src, SMALL))

    order = ("rel_table", "w_in", "b_in", "conv_w", "conv_b", "conv_ln_g", "conv_ln_b", "attn_norm_g",
             "conv_norm_g", "w_out", "ln1_g", "ln1_b", "w_up", "ffn_conv_w", "ffn_conv_b", "w_down", "ln2_g", "ln2_b")
    total_loss = lax.psum(loss[0, 0], ("x", "y", "c"))
    return (total_loss, grad_x.reshape(BL, S, D), *[grad[n] for n in order], *[delta[n] for n in order],
            *[new_m[n] for n in order], *[new_v[n] for n in order])
```

```python
import functools
import math

import numpy as np
import jax
import jax.numpy as jnp
from jax import lax
from jax.experimental import pallas as pl
from jax.experimental.pallas import tpu as pltpu

F32 = jnp.float32
BF16 = jnp.bfloat16
SDS = jax.ShapeDtypeStruct

NDEV = 8
D = 1024
S = 2048
BL = 2
T = BL * S
NH = 12
HD = 64
AW = NH * HD
CW = D - AW
INW = 3 * AW + 2 * CW
CK = 31
DFF = 2816
FK = 3
BLK = 128
NBUCKET = 32
BRANCHES = ((128, 1), (512, 4), (2048, 16))
ALPHA = 2.0 ** 0.25
LN_EPS = 1e-5
NEG_INF = -1e30
LR, B1, B2, AEPS, WD, STEP = 0.001, 0.9, 0.999, 1e-08, 0.01, 10

TM = 512
FT = 1408
NFT = DFF // FT
TMF = 256
PACK_LANES = 128
GRAD_WIRE = BF16

assert all(w // d == BLK for w, d in BRANCHES)


def _dot(a, b):
    return jnp.dot(a, b, preferred_element_type=F32)


def _dot_nt(a, b):
    return lax.dot_general(a, b, (((1,), (1,)), ((), ())), preferred_element_type=F32)


def _dot_tn(a, b):
    return lax.dot_general(a, b, (((0,), (0,)), ((), ())), preferred_element_type=F32)


def _rowmean(v):
    return jnp.mean(v, axis=-1, keepdims=True)


def _colsum(v):
    return jnp.sum(v, axis=0, keepdims=True)


def _sigmoid(v):
    return jax.nn.sigmoid(v)


def _exchange(items, name):
    n = len(items)
    arrs = [a for a, _ in items]
    kinds = [k for _, k in items]
    out_shapes = []
    for a, k in items:
        shp = (NDEV,) + tuple(a.shape) if k == "gather" else tuple(a.shape)
        out_shapes.append(SDS(shp, a.dtype))

    def body(*refs):
        ins = refs[:n]
        outs = refs[n:2 * n]
        send_sems, recv_sems, local_sems = refs[2 * n:]
        x, y, c = lax.axis_index("x"), lax.axis_index("y"), lax.axis_index("c")
        me = 4 * x + 2 * y + c

        def peer(k):
            px = 1 - x if k & 4 else x
            py = 1 - y if k & 2 else y
            pc = 1 - c if k & 1 else c
            return (px, py, pc), 4 * px + 2 * py + pc

        local = []
        for i in range(n):
            src = ins[i] if kinds[i] == "gather" else ins[i].at[me]
            cp = pltpu.make_async_copy(src, outs[i].at[me], local_sems.at[i])
            cp.start()
            local.append(cp)
        sends = []
        for k in range(1, NDEV):
            dev, pid = peer(k)
            for i in range(n):
                src = ins[i] if kinds[i] == "gather" else ins[i].at[pid]
                cp = pltpu.make_async_remote_copy(
                    src_ref=src, dst_ref=outs[i].at[me],
                    send_sem=send_sems.at[i, k - 1], recv_sem=recv_sems.at[i, k - 1],
                    device_id=dev, device_id_type=pl.DeviceIdType.MESH)
                cp.start()
                sends.append(cp)
        for k in range(1, NDEV):
            dev, pid = peer(k)
            for i in range(n):
                src = ins[i] if kinds[i] == "gather" else ins[i].at[pid]
                pltpu.make_async_remote_copy(
                    src_ref=src, dst_ref=outs[i].at[pid],
                    send_sem=send_sems.at[i, k - 1], recv_sem=recv_sems.at[i, k - 1],
                    device_id=dev, device_id_type=pl.DeviceIdType.MESH).wait_recv()
        for cp in sends:
            cp.wait_send()
        for cp in local:
            cp.wait()

    any_spec = pl.BlockSpec(memory_space=pl.ANY)
    return pl.pallas_call(
        body, name=name,
        out_shape=tuple(out_shapes),
        in_specs=[any_spec] * n,
        out_specs=tuple([any_spec] * n),
        scratch_shapes=[pltpu.SemaphoreType.DMA((n, NDEV - 1)),
                        pltpu.SemaphoreType.DMA((n, NDEV - 1)),
                        pltpu.SemaphoreType.DMA((n,))],
        compiler_params=pltpu.CompilerParams(has_side_effects=True),
    )(*arrs)


_HBM = pl.BlockSpec(memory_space=pltpu.HBM)
_SEM = pl.BlockSpec(memory_space=pltpu.SEMAPHORE)
_EFFECT = pltpu.SideEffectType.DATAFLOW_SIDE_EFFECTING


def _peer_of(k):
    x, y, c = lax.axis_index("x"), lax.axis_index("y"), lax.axis_index("c")
    px = 1 - x if k & 4 else x
    py = 1 - y if k & 2 else y
    pc = 1 - c if k & 1 else c
    return (px, py, pc), 4 * px + 2 * py + pc


def _split_copies(kinds, ins, lands, send_sems, recv_sems, started):
    me = 4 * lax.axis_index("x") + 2 * lax.axis_index("y") + lax.axis_index("c")
    out = []
    for k in range(1, NDEV):
        dev, pid = _peer_of(k)
        for i, kind in enumerate(kinds):
            src = ins[i] if kind == "gather" else ins[i].at[pid]
            dst = lands[i].at[me] if started else lands[i].at[pid]
            slot = i * (NDEV - 1) + k - 1
            out.append(pltpu.make_async_remote_copy(
                src_ref=src, dst_ref=dst, send_sem=send_sems.at[slot], recv_sem=recv_sems.at[slot],
                device_id=dev, device_id_type=pl.DeviceIdType.MESH))
    return out


def _exchange_start(items, name):
    n = len(items)
    kinds = [k for _, k in items]
    srcs = [pltpu.with_memory_space_constraint(a, pltpu.HBM) for a, _ in items]
    lands = []
    for a, k in items:
        shp = (NDEV,) + tuple(a.shape) if k == "gather" else tuple(a.shape)
        lands.append(pltpu.with_memory_space_constraint(lax.empty(shp, a.dtype), pltpu.HBM))

    def body(*refs):
        ins, land_refs = refs[:n], refs[n:2 * n]
        send_sems, recv_sems = refs[2 * n], refs[2 * n + 1]
        token = refs[-1]
        for cp in _split_copies(kinds, ins, land_refs, send_sems, recv_sems, True):
            cp.start()
        token[...] = jnp.zeros_like(token)

    sems = pltpu.SemaphoreType.DMA((n * (NDEV - 1),))
    res = pl.pallas_call(
        body, name=name,
        out_shape=(sems, sems, *[pltpu.HBM(a.shape, a.dtype) for a in srcs + lands], SDS((8, 128), F32)),
        in_specs=[_HBM] * (2 * n),
        out_specs=(_SEM, _SEM, *[_HBM] * (2 * n), pl.BlockSpec(memory_space=pltpu.VMEM)),
        input_output_aliases={i: 2 + i for i in range(2 * n)},
        compiler_params=pltpu.CompilerParams(has_side_effects=_EFFECT),
    )(*srcs, *lands)
    return (kinds, res[0], res[1], list(res[2:2 + n]), list(res[2 + n:2 + 2 * n])), res[-1][0, 0]


def _exchange_wait(state, after, name):
    kinds, send_sems, recv_sems, srcs, lands = state
    n = len(kinds)

    def body(*refs):
        ins, land_refs = refs[:n], refs[n:2 * n]
        s_sems, r_sems = refs[2 * n], refs[2 * n + 1]
        for cp in _split_copies(kinds, ins, land_refs, s_sems, r_sems, False):
            cp.wait_send()
            cp.wait_recv()

    res = pl.pallas_call(
        body, name=name,
        out_shape=tuple(pltpu.HBM(a.shape, a.dtype) for a in srcs + lands),
        in_specs=[_HBM] * (2 * n) + [_SEM, _SEM, pl.BlockSpec(memory_space=pl.ANY)],
        out_specs=tuple([_HBM] * (2 * n)),
        input_output_aliases={i: i for i in range(2 * n)},
        compiler_params=pltpu.CompilerParams(has_side_effects=_EFFECT),
    )(*srcs, *lands, send_sems, recv_sems, after)
    return list(res[n:])


def _own_slot(land, own, me):
    return lax.dynamic_update_slice_in_dim(land, own[None].astype(land.dtype), me, axis=0)


def _mm_nn_bias(a, b, bias, col_blk0, nblk, tn, out_dtype, name):
    m_, k_ = a.shape

    def body(a_ref, b_ref, bias_ref, o_ref):
        acc = _dot(a_ref[...].astype(BF16), b_ref[...])
        o_ref[...] = (acc + bias_ref[...]).astype(o_ref.dtype)

    return pl.pallas_call(
        body, name=name, grid=(nblk, m_ // TM),
        in_specs=[pl.BlockSpec((TM, k_), lambda n, m: (m, 0)),
                  pl.BlockSpec((k_, tn), lambda n, m: (0, col_blk0 + n)),
                  pl.BlockSpec((1, tn), lambda n, m: (0, col_blk0 + n))],
        out_specs=pl.BlockSpec((TM, tn), lambda n, m: (m, n)),
        out_shape=SDS((m_, nblk * tn), out_dtype),
    )(a, b, bias)


def _mm_nt(a, b, res, res_scale, name):
    m_, k_ = a.shape
    n_ = b.shape[0]
    has_res = res is not None

    def body(*refs):
        if has_res:
            a_ref, b_ref, r_ref, o_ref = refs
        else:
            a_ref, b_ref, o_ref = refs
        acc = _dot_nt(a_ref[...].astype(BF16), b_ref[...].astype(BF16))
        if has_res:
            acc = acc + res_scale * r_ref[...]
        o_ref[...] = acc

    in_specs = [pl.BlockSpec((TM, k_), lambda m: (m, 0)), pl.BlockSpec((n_, k_), lambda m: (0, 0))]
    args = [a, b]
    if has_res:
        in_specs.append(pl.BlockSpec((TM, n_), lambda m: (m, 0)))
        args.append(res)
    return pl.pallas_call(
        body, name=name, grid=(m_ // TM,), in_specs=in_specs,
        out_specs=pl.BlockSpec((TM, n_), lambda m: (m, 0)),
        out_shape=SDS((m_, n_), F32),
    )(*args)


def _mm_tn(a, b, tn, tk, name):
    t_, na = a.shape
    nb = b.shape[1]
    nk = t_ // tk

    def body(a_ref, b_ref, o_ref, acc):
        k = pl.program_id(1)

        @pl.when(k == 0)
        def _():
            acc[...] = jnp.zeros_like(acc)

        acc[...] += _dot_tn(a_ref[...].astype(BF16), b_ref[...].astype(BF16))

        @pl.when(k == nk - 1)
        def _():
            o_ref[...] = acc[...].astype(o_ref.dtype)

    return pl.pallas_call(
        body, name=name, grid=(na // tn, nk),
        in_specs=[pl.BlockSpec((tk, tn), lambda n, k: (k, n)),
                  pl.BlockSpec((tk, nb), lambda n, k: (k, 0))],
        out_specs=pl.BlockSpec((tn, nb), lambda n, k: (n, 0)),
        out_shape=SDS((na, nb), GRAD_WIRE),
        scratch_shapes=[pltpu.VMEM((tn, nb), F32)],
    )(a, b)


def _bucket_maps():
    qi = np.arange(BLK)[:, None]
    kj = np.arange(2 * BLK)[None, :]
    steps = np.maximum(qi + BLK - kj, 0)
    exact = NBUCKET // 2
    maps = []
    for _, dil in BRANCHES:
        dist = steps * dil
        d_f = np.maximum(dist, 1).astype(np.float32)
        large = exact + (np.log(d_f / np.float32(exact)) / np.float32(math.log(S / exact))
                         * np.float32(NBUCKET - exact)).astype(np.int32)
        large = np.minimum(large, NBUCKET - 1)
        maps.append(np.where(dist < exact, dist, large).astype(np.int32))
    return np.stack(maps)


def _bias_table(rel_table, buckets):
    def body(t_ref, b_ref, o_ref):
        bk = b_ref[0]
        for h in range(NH):
            acc = jnp.zeros((BLK, 2 * BLK), F32)
            for k in range(NBUCKET):
                acc = jnp.where(bk == k, t_ref[k, h], acc)
            o_ref[0, h] = acc

    return pl.pallas_call(
        body, name="bias_table", grid=(len(BRANCHES),),
        in_specs=[pl.BlockSpec(memory_space=pltpu.SMEM),
                  pl.BlockSpec((1, BLK, 2 * BLK), lambda i: (i, 0, 0))],
        out_specs=pl.BlockSpec((1, NH, BLK, 2 * BLK), lambda i: (i, 0, 0, 0)),
        out_shape=SDS((len(BRANCHES), NH, BLK, 2 * BLK), F32),
    )(rel_table, buckets)


def _rel_table_grad(dbias, buckets):
    def body(d_ref, b_ref, o_ref):
        h = pl.program_id(0)
        for k in range(NBUCKET):
            tot = jnp.zeros((1, 1), F32)
            for br in range(len(BRANCHES)):
                sel = jnp.where(b_ref[br] == k, d_ref[br, 0], 0.0)
                tot = tot + jnp.sum(jnp.sum(sel, axis=1, keepdims=True), axis=0, keepdims=True)
            o_ref[0, :, pl.ds(k, 1)] = tot

    out = pl.pallas_call(
        body, name="rel_table_grad", grid=(NH,),
        in_specs=[pl.BlockSpec((len(BRANCHES), 1, BLK, 2 * BLK), lambda h: (0, h, 0, 0)),
                  pl.BlockSpec((len(BRANCHES), BLK, 2 * BLK), lambda h: (0, 0, 0))],
        out_specs=pl.BlockSpec((1, 1, NBUCKET), lambda h: (h, 0, 0)),
        out_shape=SDS((NH, 1, NBUCKET), F32),
    )(dbias, buckets)
    return out.reshape(NH, NBUCKET).T


def _block_rows(br, i):
    _, dil = BRANCHES[br]
    nb = S // dil // BLK
    if nb == 16:
        r, nidx = 0, i
    elif nb == 4:
        r, nidx = lax.shift_right_logical(i, 2), lax.bitwise_and(i, 3)
    else:
        r, nidx = i, 0
    start = r + dil * BLK * nidx
    if nb == 1:
        return start, None, None
    prev = r + dil * BLK * jnp.maximum(nidx - 1, 0)
    return start, prev, nidx > 0


def _rows(start, dil):
    if dil == 1:
        return pl.ds(pl.multiple_of(start, BLK), BLK)
    return pl.ds(start, BLK, stride=dil)


def _attn_masks():
    lane = lax.broadcasted_iota(jnp.int32, (BLK, BLK), 1)
    qi = lax.broadcasted_iota(jnp.int32, (BLK, BLK), 0)
    head0 = lane < HD
    valid_cur = lane <= qi
    valid_prev = lane >= qi
    return head0, valid_cur, valid_prev


def _attn_fwd_v1(qkv, bias):
    scale = 1.0 / math.sqrt(HD)
    nbr = len(BRANCHES)

    def body(q_ref, k_ref, v_ref, bias_ref, o_ref, lse_ref, qf, kf, vf, ob, mb, lb):
        qf[...] = q_ref[...].astype(F32)
        kf[...] = k_ref[...].astype(F32)
        vf[...] = v_ref[...].astype(F32)
        head0, valid_cur, valid_prev = _attn_masks()

        for br in range(nbr):
            dil = BRANCHES[br][1]

            def blk(i, carry, br=br, dil=dil):
                start, prev, has_prev = _block_rows(br, i)
                rows = _rows(start, dil)
                q = qf[rows, :]
                kc = kf[rows, :].astype(BF16)
                vc = vf[rows, :].astype(BF16)
                if prev is not None:
                    prows = _rows(prev, dil)
                    kp = kf[prows, :].astype(BF16)
                    vp = vf[prows, :].astype(BF16)
                    ok_prev = jnp.logical_and(valid_prev, has_prev)
                o_acc = jnp.zeros((BLK, BLK), F32)
                m_acc = jnp.zeros((BLK, BLK), F32)
                l_acc = jnp.zeros((BLK, BLK), F32)
                for j in range(2):
                    mj = head0 if j == 0 else jnp.logical_not(head0)
                    qj = jnp.where(mj, q, 0.0).astype(BF16)
                    sc = _dot_nt(qj, kc) * scale + bias_ref[br, j, :, BLK:]
                    sc = jnp.where(valid_cur, sc, NEG_INF)
                    mx = jnp.max(sc, axis=-1, keepdims=True)
                    if prev is not None:
                        sp = _dot_nt(qj, kp) * scale + bias_ref[br, j, :, :BLK]
                        sp = jnp.where(ok_prev, sp, NEG_INF)
                        mx = jnp.maximum(mx, jnp.max(sp, axis=-1, keepdims=True))
                    pc = jnp.exp(sc - mx)
                    ls = jnp.sum(pc, axis=-1, keepdims=True)
                    o = _dot(pc.astype(BF16), vc)
                    if prev is not None:
                        pp = jnp.exp(sp - mx)
                        ls = ls + jnp.sum(pp, axis=-1, keepdims=True)
                        o = o + _dot(pp.astype(BF16), vp)
                    o_acc = jnp.where(mj, o, o_acc)
                    m_acc = jnp.where(mj, mx, m_acc)
                    l_acc = jnp.where(mj, ls, l_acc)
                ob[br, rows, :] = o_acc
                mb[br, rows, :] = m_acc
                lb[br, rows, :] = l_acc
                return carry

            lax.fori_loop(0, 16, blk, 0)

        def merge(i, carry):
            rows = pl.ds(pl.multiple_of(i * 256, 256), 256)
            m_all = jnp.maximum(jnp.maximum(mb[0, rows, :], mb[1, rows, :]), mb[2, rows, :])
            num = jnp.zeros((256, BLK), F32)
            den = jnp.zeros((256, BLK), F32)
            for br in range(nbr):
                c = jnp.exp(mb[br, rows, :] - m_all)
                num = num + ob[br, rows, :] * c
                den = den + lb[br, rows, :] * c
            o_ref[rows, :] = num / den
            lse_ref[rows, :] = m_all + jnp.log(den)
            return carry

        lax.fori_loop(0, S // 256, merge, 0)

    npair = NH // 2
    blk_spec = lambda off: pl.BlockSpec((S, BLK), lambda b, hp: (b, off + hp))
    return pl.pallas_call(
        body, name="attn_fwd", grid=(BL, npair),
        in_specs=[blk_spec(0), blk_spec(npair), blk_spec(2 * npair),
                  pl.BlockSpec((nbr, 2, BLK, 2 * BLK), lambda b, hp: (0, hp, 0, 0))],
        out_specs=(blk_spec(0), blk_spec(0)),
        out_shape=(SDS((T, AW), F32), SDS((T, AW), F32)),
        scratch_shapes=[pltpu.VMEM((S, BLK), F32)] * 3 + [pltpu.VMEM((nbr, S, BLK), F32)] * 3,
    )(qkv, qkv, qkv, bias)


def _attn_bwd_v1(qkv, attn, lse, dattn, bias):
    scale = 1.0 / math.sqrt(HD)
    nbr = len(BRANCHES)

    def body(q_ref, k_ref, v_ref, o_ref, lse_ref, do_ref, bias_ref,
             dq_ref, dk_ref, dv_ref, sq_ref, sk_ref, sv_ref, db_ref,
             qf, kf, vf, dl, dqa, dka, dva):
        b = pl.program_id(1)
        qf[...] = q_ref[...].astype(F32)
        kf[...] = k_ref[...].astype(F32)
        vf[...] = v_ref[...].astype(F32)
        dqa[...] = jnp.zeros_like(dqa)
        dka[...] = jnp.zeros_like(dka)
        dva[...] = jnp.zeros_like(dva)
        head0, valid_cur, valid_prev = _attn_masks()

        @pl.when(b == 0)
        def _():
            db_ref[...] = jnp.zeros_like(db_ref)
            sq_ref[...] = jnp.zeros_like(sq_ref)
            sk_ref[...] = jnp.zeros_like(sk_ref)
            sv_ref[...] = jnp.zeros_like(sv_ref)

        def delta(i, carry):
            rows = pl.ds(pl.multiple_of(i * 256, 256), 256)
            prod = do_ref[rows, :] * o_ref[rows, :]
            h0 = lax.broadcasted_iota(jnp.int32, (256, BLK), 1) < HD
            d0 = jnp.sum(jnp.where(h0, prod, 0.0), axis=-1, keepdims=True)
            d1 = jnp.sum(jnp.where(h0, 0.0, prod), axis=-1, keepdims=True)
            dl[rows, :] = jnp.where(h0, d0, d1)
            return carry

        lax.fori_loop(0, S // 256, delta, 0)

        for br in range(nbr):
            dil = BRANCHES[br][1]

            def blk(i, carry, br=br, dil=dil):
                start, prev, has_prev = _block_rows(br, i)
                rows = _rows(start, dil)
                q = qf[rows, :]
                kc = kf[rows, :].astype(BF16)
                vc = vf[rows, :].astype(BF16)
                do = do_ref[rows, :]
                lse_b = lse_ref[rows, :]
                dl_b = dl[rows, :]
                if prev is not None:
                    prows = _rows(prev, dil)
                    kp = kf[prows, :].astype(BF16)
                    vp = vf[prows, :].astype(BF16)
                    ok_prev = jnp.logical_and(valid_prev, has_prev)
                    dk_p = jnp.zeros((BLK, BLK), F32)
                    dv_p = jnp.zeros((BLK, BLK), F32)
                dq = jnp.zeros((BLK, BLK), F32)
                dk_c = jnp.zeros((BLK, BLK), F32)
                dv_c = jnp.zeros((BLK, BLK), F32)
                for j in range(2):
                    mj = head0 if j == 0 else jnp.logical_not(head0)
                    qj = jnp.where(mj, q, 0.0).astype(BF16)
                    doj = jnp.where(mj, do, 0.0).astype(BF16)
                    lse_j = lse_b[:, j * HD:j * HD + 1]
                    dl_j = dl_b[:, j * HD:j * HD + 1]
                    sc = _dot_nt(qj, kc) * scale + bias_ref[br, j, :, BLK:]
                    pc = jnp.where(valid_cur, jnp.exp(sc - lse_j), 0.0)
                    ds_c = pc * (_dot_nt(doj, vc) - dl_j)
                    db_ref[br, j, :, BLK:] += ds_c
                    dsb = (ds_c * scale).astype(BF16)
                    dqj = _dot(dsb, kc)
                    dk_c = dk_c + _dot_tn(dsb, qj)
                    dv_c = dv_c + _dot_tn(pc.astype(BF16), doj)
                    if prev is not None:
                        sp = _dot_nt(qj, kp) * scale + bias_ref[br, j, :, :BLK]
                        pp = jnp.where(ok_prev, jnp.exp(sp - lse_j), 0.0)
                        ds_p = pp * (_dot_nt(doj, vp) - dl_j)
                        db_ref[br, j, :, :BLK] += ds_p
                        dsbp = (ds_p * scale).astype(BF16)
                        dqj = dqj + _dot(dsbp, kp)
                        dk_p = dk_p + _dot_tn(dsbp, qj)
                        dv_p = dv_p + _dot_tn(pp.astype(BF16), doj)
                    dq = jnp.where(mj, dqj, dq)
                dqa[rows, :] = dqa[rows, :] + dq
                dka[rows, :] = dka[rows, :] + dk_c
                dva[rows, :] = dva[rows, :] + dv_c
                if prev is not None:
                    dka[prows, :] = dka[prows, :] + dk_p
                    dva[prows, :] = dva[prows, :] + dv_p
                return carry

            lax.fori_loop(0, 16, blk, 0)

        def flush(i, carry):
            rows = pl.ds(pl.multiple_of(i * 256, 256), 256)
            for acc, out, cs in ((dqa, dq_ref, sq_ref), (dka, dk_ref, sk_ref), (dva, dv_ref, sv_ref)):
                val = acc[rows, :]
                out[rows, :] = val.astype(BF16)
                cs[...] += _colsum(val)
            return carry

        lax.fori_loop(0, S // 256, flush, 0)

    npair = NH // 2
    blk_spec = lambda off: pl.BlockSpec((S, BLK), lambda hp, b: (b, off + hp))
    sum_spec = pl.BlockSpec((1, BLK), lambda hp, b: (0, hp))
    return pl.pallas_call(
        body, name="attn_bwd", grid=(npair, BL),
        in_specs=[blk_spec(0), blk_spec(npair), blk_spec(2 * npair), blk_spec(0), blk_spec(0), blk_spec(0),
                  pl.BlockSpec((nbr, 2, BLK, 2 * BLK), lambda hp, b: (0, hp, 0, 0))],
        out_specs=(blk_spec(0), blk_spec(0), blk_spec(0), sum_spec, sum_spec, sum_spec,
                   pl.BlockSpec((nbr, 2, BLK, 2 * BLK), lambda hp, b: (0, hp, 0, 0))),
        out_shape=(SDS((T, AW), BF16), SDS((T, AW), BF16), SDS((T, AW), BF16),
                   SDS((1, AW), F32), SDS((1, AW), F32), SDS((1, AW), F32),
                   SDS((nbr, NH, BLK, 2 * BLK), F32)),
        scratch_shapes=[pltpu.VMEM((S, BLK), F32)] * 7,
    )(qkv, qkv, qkv, attn, lse, dattn, bias)


PADK = BLK
SCALE = 1.0 / math.sqrt(HD)
ATTN_UNROLL = 8


def _branch_geometry(br):
    dil = BRANCHES[br][1]
    sub = S // dil
    return dil, sub, sub // BLK


def _token_rows(br, i):
    dil, _, nblk = _branch_geometry(br)
    if dil == 1:
        return pl.ds(pl.multiple_of(i * BLK, BLK), BLK), i
    r = lax.shift_right_logical(i, nblk.bit_length() - 1)
    n = lax.bitwise_and(i, nblk - 1)
    return pl.ds(r + dil * BLK * n, BLK, stride=dil), n


def _sub_layout_loop(br, step):
    dil, sub, _ = _branch_geometry(br)
    rows = min(sub, 256)
    nchunk = sub // rows

    def it_step(it, carry):
        if dil == 1:
            src = pl.ds(pl.multiple_of(it * rows, rows), rows)
        else:
            r = lax.shift_right_logical(it, nchunk.bit_length() - 1)
            src = pl.ds(r + dil * rows * lax.bitwise_and(it, nchunk - 1), rows, stride=dil)
        step(src, pl.multiple_of(it * rows, BLK), rows)
        return carry

    lax.fori_loop(0, dil * nchunk, it_step, 0)


def _masked_bias(bias_ref, bm):
    qi = lax.broadcasted_iota(jnp.int32, (BLK, 2 * BLK), 0)
    kj = lax.broadcasted_iota(jnp.int32, (BLK, 2 * BLK), 1)
    first = jnp.logical_and(kj >= BLK, kj - BLK <= qi)
    valid = jnp.logical_or(first, jnp.logical_and(kj < BLK, kj >= qi))
    for br in range(len(BRANCHES)):
        for j in range(2):
            b = bias_ref[br, j]
            bm[br, 1, pl.ds(j * BLK, BLK), :] = jnp.where(valid, b, NEG_INF)
            bm[br, 0, pl.ds(j * BLK, BLK), :] = jnp.where(first, b, NEG_INF)


def _head_split(fn):
    def split(t):
        h0 = lax.broadcasted_iota(jnp.int32, t.shape, 1) < HD
        t = fn(t)
        return jnp.where(h0, t, 0.0).astype(BF16), jnp.where(h0, 0.0, t).astype(BF16)
    return split


def _attn_fwd(qkv, bias):
    nbr = len(BRANCHES)

    def body(q_ref, k_ref, v_ref, bias_ref, o_ref, lse_ref, qf, kf, vf, qs0, qs1, ks, vs, bm, ob, mb, lb):
        qf[...] = q_ref[...].astype(F32)
        kf[...] = k_ref[...].astype(F32)
        vf[...] = v_ref[...].astype(F32)
        _masked_bias(bias_ref, bm)
        ks[pl.ds(0, PADK), :] = jnp.zeros((PADK, BLK), BF16)
        vs[pl.ds(0, PADK), :] = jnp.zeros((PADK, BLK), BF16)
        head0 = lax.broadcasted_iota(jnp.int32, (BLK, BLK), 1) < HD
        split_q = _head_split(lambda t: t * SCALE)

        for br in range(nbr):
            nblk = _branch_geometry(br)[2]

            def stage(src, off, rows):
                qs0[pl.ds(off, rows), :], qs1[pl.ds(off, rows), :] = split_q(qf[src, :])
                ks[pl.ds(PADK + off, rows), :] = kf[src, :].astype(BF16)
                vs[pl.ds(PADK + off, rows), :] = vf[src, :].astype(BF16)

            _sub_layout_loop(br, stage)

            def blk(i, carry, br=br, nblk=nblk):
                base = pl.multiple_of(i * BLK, BLK)
                rows, n = _token_rows(br, i)
                q01 = jnp.concatenate([qs0[pl.ds(base, BLK), :], qs1[pl.ds(base, BLK), :]], axis=0)
                if nblk > 1:
                    kcat = ks[pl.ds(base, 2 * BLK), :]
                    vcat = vs[pl.ds(base, 2 * BLK), :]
                    s = _dot_nt(q01, kcat) + bm[br, jnp.minimum(n, 1)]
                else:
                    kcat = ks[pl.ds(PADK + base, BLK), :]
                    vcat = vs[pl.ds(PADK + base, BLK), :]
                    s = _dot_nt(q01, kcat) + bm[br, 0, :, BLK:]
                mx = jnp.max(s, axis=-1, keepdims=True)
                p = jnp.exp(s - mx)
                ls = jnp.sum(p, axis=-1, keepdims=True)
                o = _dot(p.astype(BF16), vcat)
                ob[br, rows, :] = jnp.where(head0, o[:BLK], o[BLK:])
                mb[br, rows, :] = jnp.where(head0, mx[:BLK], mx[BLK:])
                lb[br, rows, :] = jnp.where(head0, ls[:BLK], ls[BLK:])
                return carry

            lax.fori_loop(0, 16, blk, 0, unroll=ATTN_UNROLL)

        def merge(i, carry):
            rows = pl.ds(pl.multiple_of(i * 256, 256), 256)
            m_all = jnp.maximum(jnp.maximum(mb[0, rows, :], mb[1, rows, :]), mb[2, rows, :])
            num = jnp.zeros((256, BLK), F32)
            den = jnp.zeros((256, BLK), F32)
            for br in range(nbr):
                c = jnp.exp(mb[br, rows, :] - m_all)
                num = num + ob[br, rows, :] * c
                den = den + lb[br, rows, :] * c
            o_ref[rows, :] = num / den
            lse_ref[rows, :] = m_all + jnp.log(den)
            return carry

        lax.fori_loop(0, S // 256, merge, 0)

    npair = NH // 2
    blk_spec = lambda off: pl.BlockSpec((S, BLK), lambda b, hp: (b, off + hp))
    return pl.pallas_call(
        body, name="attn_fwd", grid=(BL, npair),
        in_specs=[blk_spec(0), blk_spec(npair), blk_spec(2 * npair),
                  pl.BlockSpec((nbr, 2, BLK, 2 * BLK), lambda b, hp: (0, hp, 0, 0))],
        out_specs=(blk_spec(0), blk_spec(0)),
        out_shape=(SDS((T, AW), F32), SDS((T, AW), F32)),
        scratch_shapes=[pltpu.VMEM((S, BLK), F32)] * 3 + [pltpu.VMEM((S, BLK), BF16)] * 2
        + [pltpu.VMEM((PADK + S, BLK), BF16)] * 2 + [pltpu.VMEM((nbr, 2, 2 * BLK, 2 * BLK), F32)]
        + [pltpu.VMEM((nbr, S, BLK), F32)] * 3,
    )(qkv, qkv, qkv, bias)


def _attn_bwd(qkv, attn, lse, dattn, bias):
    nbr = len(BRANCHES)

    def body(q_ref, k_ref, v_ref, o_ref, lse_ref, do_ref, bias_ref,
             dq_ref, dk_ref, dv_ref, sq_ref, sk_ref, sv_ref, db_ref,
             qf, kf, vf, dl, dqa, dka, dva, qs0, qs1, ds0, ds1, ks, vs, dks, dvs, bm):
        b = pl.program_id(1)
        qf[...] = q_ref[...].astype(F32)
        kf[...] = k_ref[...].astype(F32)
        vf[...] = v_ref[...].astype(F32)
        dqa[...] = jnp.zeros_like(dqa)
        dka[...] = jnp.zeros_like(dka)
        dva[...] = jnp.zeros_like(dva)
        _masked_bias(bias_ref, bm)
        ks[pl.ds(0, PADK), :] = jnp.zeros((PADK, BLK), BF16)
        vs[pl.ds(0, PADK), :] = jnp.zeros((PADK, BLK), BF16)
        head0 = lax.broadcasted_iota(jnp.int32, (BLK, BLK), 1) < HD
        split_q = _head_split(lambda t: t * SCALE)
        split_do = _head_split(lambda t: t)

        @pl.when(b == 0)
        def _():
            db_ref[...] = jnp.zeros_like(db_ref)
            sq_ref[...] = jnp.zeros_like(sq_ref)
            sk_ref[...] = jnp.zeros_like(sk_ref)
            sv_ref[...] = jnp.zeros_like(sv_ref)

        def delta(i, carry):
            rows = pl.ds(pl.multiple_of(i * 256, 256), 256)
            prod = do_ref[rows, :] * o_ref[rows, :]
            h0 = lax.broadcasted_iota(jnp.int32, (256, BLK), 1) < HD
            d0 = jnp.sum(jnp.where(h0, prod, 0.0), axis=-1, keepdims=True)
            d1 = jnp.sum(jnp.where(h0, 0.0, prod), axis=-1, keepdims=True)
            dl[rows, :] = jnp.where(h0, d0, d1)
            return carry

        lax.fori_loop(0, S // 256, delta, 0)

        for br in range(nbr):
            nblk = _branch_geometry(br)[2]

            def stage(src, off, rows):
                qs0[pl.ds(off, rows), :], qs1[pl.ds(off, rows), :] = split_q(qf[src, :])
                ds0[pl.ds(off, rows), :], ds1[pl.ds(off, rows), :] = split_do(do_ref[src, :])
                ks[pl.ds(PADK + off, rows), :] = kf[src, :].astype(BF16)
                vs[pl.ds(PADK + off, rows), :] = vf[src, :].astype(BF16)

            _sub_layout_loop(br, stage)
            dks[...] = jnp.zeros_like(dks)
            dvs[...] = jnp.zeros_like(dvs)

            def blk(i, carry, br=br, nblk=nblk):
                base = pl.multiple_of(i * BLK, BLK)
                rows, n = _token_rows(br, i)
                q01 = jnp.concatenate([qs0[pl.ds(base, BLK), :], qs1[pl.ds(base, BLK), :]], axis=0)
                do01 = jnp.concatenate([ds0[pl.ds(base, BLK), :], ds1[pl.ds(base, BLK), :]], axis=0)
                lse_b = lse_ref[rows, :]
                dl_b = dl[rows, :]
                lse01 = jnp.concatenate([lse_b[:, 0:1], lse_b[:, HD:HD + 1]], axis=0)
                dl01 = jnp.concatenate([dl_b[:, 0:1], dl_b[:, HD:HD + 1]], axis=0)
                if nblk > 1:
                    krows = pl.ds(base, 2 * BLK)
                    bias_m = bm[br, jnp.minimum(n, 1)]
                else:
                    krows = pl.ds(PADK + base, BLK)
                    bias_m = bm[br, 0, :, BLK:]
                kcat = ks[krows, :]
                vcat = vs[krows, :]
                p = jnp.exp(_dot_nt(q01, kcat) + bias_m - lse01)
                dsv = p * (_dot_nt(do01, vcat) - dl01)
                if nblk > 1:
                    db_ref[br, 0] += dsv[:BLK]
                    db_ref[br, 1] += dsv[BLK:]
                else:
                    db_ref[br, 0, :, BLK:] += dsv[:BLK]
                    db_ref[br, 1, :, BLK:] += dsv[BLK:]
                dsb = dsv.astype(BF16)
                dq01 = _dot(dsb, kcat)
                dqa[rows, :] = dqa[rows, :] + jnp.where(head0, dq01[:BLK], dq01[BLK:])
                dks[krows, :] = dks[krows, :] + _dot_tn(dsb, q01)
                dvs[krows, :] = dvs[krows, :] + _dot_tn(p.astype(BF16), do01)
                return carry

            lax.fori_loop(0, 16, blk, 0, unroll=ATTN_UNROLL)

            def fold(src, off, rows):
                dka[src, :] = dka[src, :] + dks[pl.ds(PADK + off, rows), :]
                dva[src, :] = dva[src, :] + dvs[pl.ds(PADK + off, rows), :]

            _sub_layout_loop(br, fold)

        def flush(i, carry):
            rows = pl.ds(pl.multiple_of(i * 256, 256), 256)
            for acc, out, cs, mul in ((dqa, dq_ref, sq_ref, SCALE), (dka, dk_ref, sk_ref, 1.0), (dva, dv_ref, sv_ref, 1.0)):
                val = acc[rows, :] * mul
                out[rows, :] = val.astype(BF16)
                cs[...] += _colsum(val)
            return carry

        lax.fori_loop(0, S // 256, flush, 0)

    npair = NH // 2
    blk_spec = lambda off: pl.BlockSpec((S, BLK), lambda hp, b: (b, off + hp))
    sum_spec = pl.BlockSpec((1, BLK), lambda hp, b: (0, hp))
    return pl.pallas_call(
        body, name="attn_bwd", grid=(npair, BL),
        in_specs=[blk_spec(0), blk_spec(npair), blk_spec(2 * npair), blk_spec(0), blk_spec(0), blk_spec(0),
                  pl.BlockSpec((nbr, 2, BLK, 2 * BLK), lambda hp, b: (0, hp, 0, 0))],
        out_specs=(blk_spec(0), blk_spec(0), blk_spec(0), sum_spec, sum_spec, sum_spec,
                   pl.BlockSpec((nbr, 2, BLK, 2 * BLK), lambda hp, b: (0, hp, 0, 0))),
        out_shape=(SDS((T, AW), BF16), SDS((T, AW), BF16), SDS((T, AW), BF16),
                   SDS((1, AW), F32), SDS((1, AW), F32), SDS((1, AW), F32),
                   SDS((nbr, NH, BLK, 2 * BLK), F32)),
        scratch_shapes=[pltpu.VMEM((S, BLK), F32)] * 7 + [pltpu.VMEM((S, BLK), BF16)] * 4
        + [pltpu.VMEM((PADK + S, BLK), BF16)] * 2 + [pltpu.VMEM((PADK + S, BLK), F32)] * 2
        + [pltpu.VMEM((nbr, 2, 2 * BLK, 2 * BLK), F32)],
    )(qkv, qkv, qkv, attn, lse, dattn, bias)


CH = 256
PADR = 32


def _conv_fwd(ag, conv_w, conv_b):
    def body(ag_ref, w_ref, b_ref, u1_ref, u0p):
        u0p[pl.ds(0, PADR), :] = jnp.zeros((PADR, CW), F32)

        def glu(i, carry):
            t0 = pl.multiple_of(i * CH, CH)
            a = ag_ref[pl.ds(t0, CH), :CW]
            g = ag_ref[pl.ds(t0, CH), CW:]
            u0p[pl.ds(PADR + t0, CH), :] = a * _sigmoid(g)
            return carry

        lax.fori_loop(0, S // CH, glu, 0)

        def conv(i, carry):
            t0 = pl.multiple_of(i * CH, CH)
            win = u0p[pl.ds(t0, CH + PADR), :]
            acc = jnp.zeros((CH, CW), F32) + b_ref[...]
            for k in range(CK):
                off = PADR - (CK - 1) + k
                acc = acc + win[off:off + CH, :] * w_ref[k:k + 1, :]
            u1_ref[pl.ds(t0, CH), :] = acc
            return carry

        lax.fori_loop(0, S // CH, conv, 0)

    return pl.pallas_call(
        body, name="conv_fwd", grid=(BL,),
        in_specs=[pl.BlockSpec((S, 2 * CW), lambda b: (b, 0)),
                  pl.BlockSpec((CK, CW), lambda b: (0, 0)),
                  pl.BlockSpec((1, CW), lambda b: (0, 0))],
        out_specs=pl.BlockSpec((S, CW), lambda b: (b, 0)),
        out_shape=SDS((T, CW), F32),
        scratch_shapes=[pltpu.VMEM((S + PADR, CW), F32)],
    )(ag, conv_w, conv_b)


def _conv_post(u1, cg, cb):
    mu = _rowmean(u1)
    uc = u1 - mu
    rstd = lax.rsqrt(_rowmean(uc * uc) + LN_EPS)
    xh = uc * rstd
    u2 = xh * cg + cb
    sg = _sigmoid(u2)
    return xh, rstd, u2, sg, u2 * sg


def _mix_fwd(attn, u1, ga, gc, cg, cb):
    def body(a_ref, u_ref, ga_ref, gc_ref, cg_ref, cb_ref, o_ref):
        a = a_ref[...]
        ra = lax.rsqrt(_rowmean(a * a) + LN_EPS)
        o_ref[:, :AW] = (a * ra * ga_ref[...]).astype(BF16)
        _, _, _, _, u3 = _conv_post(u_ref[...], cg_ref[...], cb_ref[...])
        rc = lax.rsqrt(_rowmean(u3 * u3) + LN_EPS)
        o_ref[:, AW:] = (u3 * rc * gc_ref[...]).astype(BF16)

    vec = lambda w: pl.BlockSpec((1, w), lambda m: (0, 0))
    return pl.pallas_call(
        body, name="mix_fwd", grid=(T // TM,),
        in_specs=[pl.BlockSpec((TM, AW), lambda m: (m, 0)), pl.BlockSpec((TM, CW), lambda m: (m, 0)),
                  vec(AW), vec(CW), vec(CW), vec(CW)],
        out_specs=pl.BlockSpec((TM, D), lambda m: (m, 0)),
        out_shape=SDS((T, D), BF16),
    )(attn, u1, ga, gc, cg, cb)


def _mix_bwd(dmixed, attn, u1, ga, gc, cg, cb):
    def body(dm_ref, a_ref, u_ref, ga_ref, gc_ref, cg_ref, cb_ref,
             da_ref, du_ref, g_an, g_cn, g_lg, g_lb, g_cb):
        @pl.when(pl.program_id(0) == 0)
        def _():
            for r in (g_an, g_cn, g_lg, g_lb, g_cb):
                r[...] = jnp.zeros_like(r)

        a = a_ref[...]
        dna = dm_ref[:, :AW]
        ra = lax.rsqrt(_rowmean(a * a) + LN_EPS)
        g_an[...] += _colsum(dna * a * ra)
        dat = dna * ga_ref[...]
        da_ref[...] = ra * dat - a * (ra * ra * ra) * _rowmean(dat * a)

        xh, rstd, u2, sg, u3 = _conv_post(u_ref[...], cg_ref[...], cb_ref[...])
        dnc = dm_ref[:, AW:]
        rc = lax.rsqrt(_rowmean(u3 * u3) + LN_EPS)
        g_cn[...] += _colsum(dnc * u3 * rc)
        dut = dnc * gc_ref[...]
        du3 = rc * dut - u3 * (rc * rc * rc) * _rowmean(dut * u3)
        du2 = du3 * sg * (1.0 + u2 * (1.0 - sg))
        g_lg[...] += _colsum(du2 * xh)
        g_lb[...] += _colsum(du2)
        dxh = du2 * cg_ref[...]
        du1 = rstd * (dxh - _rowmean(dxh) - xh * _rowmean(dxh * xh))
        g_cb[...] += _colsum(du1)
        du_ref[...] = du1

    vec = lambda w: pl.BlockSpec((1, w), lambda m: (0, 0))
    return pl.pallas_call(
        body, name="mix_bwd", grid=(T // TM,),
        in_specs=[pl.BlockSpec((TM, D), lambda m: (m, 0)), pl.BlockSpec((TM, AW), lambda m: (m, 0)),
                  pl.BlockSpec((TM, CW), lambda m: (m, 0)), vec(AW), vec(CW), vec(CW), vec(CW)],
        out_specs=(pl.BlockSpec((TM, AW), lambda m: (m, 0)), pl.BlockSpec((TM, CW), lambda m: (m, 0)),
                   vec(AW), vec(CW), vec(CW), vec(CW), vec(CW)),
        out_shape=(SDS((T, AW), F32), SDS((T, CW), F32),
                   SDS((1, AW), F32), SDS((1, CW), F32), SDS((1, CW), F32), SDS((1, CW), F32), SDS((1, CW), F32)),
    )(dmixed, attn, u1, ga, gc, cg, cb)


def _conv_bwd(du1, ag, conv_w):
    def body(du_ref, ag_ref, w_ref, dag_ref, cs_ref, gw_ref, u0p, dup):
        @pl.when(pl.program_id(0) == 0)
        def _():
            cs_ref[...] = jnp.zeros_like(cs_ref)
            gw_ref[...] = jnp.zeros_like(gw_ref)

        u0p[pl.ds(0, PADR), :] = jnp.zeros((PADR, CW), F32)
        dup[pl.ds(S, PADR), :] = jnp.zeros((PADR, CW), F32)

        def fill(i, carry):
            t0 = pl.multiple_of(i * CH, CH)
            a = ag_ref[pl.ds(t0, CH), :CW]
            g = ag_ref[pl.ds(t0, CH), CW:]
            u0p[pl.ds(PADR + t0, CH), :] = a * _sigmoid(g)
            dup[pl.ds(t0, CH), :] = du_ref[pl.ds(t0, CH), :]
            return carry

        lax.fori_loop(0, S // CH, fill, 0)

        def chunk(i, carry):
            t0 = pl.multiple_of(i * CH, CH)
            d = dup[pl.ds(t0, CH), :]
            win_u = u0p[pl.ds(t0, CH + PADR), :]
            win_d = dup[pl.ds(t0, CH + PADR), :]
            du0 = jnp.zeros((CH, CW), F32)
            for k in range(CK):
                off = PADR - (CK - 1) + k
                gw_ref[k:k + 1, :] += _colsum(d * win_u[off:off + CH, :])
                fo = CK - 1 - k
                du0 = du0 + win_d[fo:fo + CH, :] * w_ref[k:k + 1, :]
            a = ag_ref[pl.ds(t0, CH), :CW]
            sg = _sigmoid(ag_ref[pl.ds(t0, CH), CW:])
            da = du0 * sg
            dg = du0 * a * sg * (1.0 - sg)
            dag_ref[pl.ds(t0, CH), :CW] = da.astype(BF16)
            dag_ref[pl.ds(t0, CH), CW:] = dg.astype(BF16)
            cs_ref[:, :CW] += _colsum(da)
            cs_ref[:, CW:] += _colsum(dg)
            return carry

        lax.fori_loop(0, S // CH, chunk, 0)

    return pl.pallas_call(
        body, name="conv_bwd", grid=(BL,),
        in_specs=[pl.BlockSpec((S, CW), lambda b: (b, 0)), pl.BlockSpec((S, 2 * CW), lambda b: (b, 0)),
                  pl.BlockSpec((CK, CW), lambda b: (0, 0))],
        out_specs=(pl.BlockSpec((S, 2 * CW), lambda b: (b, 0)),
                   pl.BlockSpec((1, 2 * CW), lambda b: (0, 0)),
                   pl.BlockSpec((PADR, CW), lambda b: (0, 0))),
        out_shape=(SDS((T, 2 * CW), BF16), SDS((1, 2 * CW), F32), SDS((PADR, CW), F32)),
        scratch_shapes=[pltpu.VMEM((S + PADR, CW), F32), pltpu.VMEM((S + PADR, CW), F32)],
    )(du1, ag, conv_w)


def _layer_norm_fwd(z):
    mu = _rowmean(z)
    zc = z - mu
    rstd = lax.rsqrt(_rowmean(zc * zc) + LN_EPS)
    return zc * rstd, rstd


def _layer_norm_bwd(dy, xh, rstd, g):
    dxh = dy * g
    return rstd * (dxh - _rowmean(dxh) - xh * _rowmean(dxh * xh))


def _out_proj_ln1(mixed, w_out, x2, g1, b1):
    def body(a_ref, w_ref, x_ref, g_ref, b_ref, xh_ref, rstd_ref, x1_ref):
        z = ALPHA * x_ref[...] + _dot(a_ref[...], w_ref[...])
        xh, rstd = _layer_norm_fwd(z)
        xh_ref[...] = xh
        rstd_ref[...] = rstd
        x1_ref[...] = (xh * g_ref[...] + b_ref[...]).astype(BF16)

    vec = pl.BlockSpec((1, D), lambda m: (0, 0))
    row = pl.BlockSpec((TM, D), lambda m: (m, 0))
    return pl.pallas_call(
        body, name="out_proj_ln1", grid=(T // TM,),
        in_specs=[row, pl.BlockSpec((D, D), lambda m: (0, 0)), row, vec, vec],
        out_specs=(row, pl.BlockSpec((TM, 1), lambda m: (m, 0)), row),
        out_shape=(SDS((T, D), F32), SDS((T, 1), F32), SDS((T, D), BF16)),
    )(mixed, w_out, x2, g1, b1)


def _seq_start(m):
    return lax.bitwise_and(m, S // TMF - 1) == 0


def _causal3(ext, w_ref, b_ref):
    x0 = ext[pl.ds(8, TM), :]
    x1 = ext[pl.ds(7, TM), :]
    x2 = ext[pl.ds(6, TM), :]
    y = w_ref[2:3, :] * x0 + w_ref[1:2, :] * x1 + w_ref[0:1, :] * x2 + b_ref[...]
    return y, x0, x1, x2


def _shift_down(x, before, k):
    rolled = pltpu.roll(x, k, 0)
    row = lax.broadcasted_iota(jnp.int32, before.shape, 0)
    head = jnp.where(row < k, pltpu.roll(before, k, 0), rolled[:8])
    return jnp.concatenate([head, rolled[8:]], axis=0)


def _shift_up(x, after, k):
    n = x.shape[0]
    rolled = pltpu.roll(x, n - k, 0)
    row = lax.broadcasted_iota(jnp.int32, after.shape, 0)
    tail = jnp.where(row >= 8 - k, pltpu.roll(after, 8 - k, 0), rolled[n - 8:])
    return jnp.concatenate([rolled[:n - 8], tail], axis=0)


def _ffn_up(x1b, w_up, fcw, fcb):
    def body(x_ref, wg_ref, wv_ref, cwg_ref, cwv_ref, cbg_ref, cbv_ref, up_ref, gv_ref, act_ref, prev_g, prev_v):
        @pl.when(_seq_start(pl.program_id(1)))
        def _():
            prev_g[...] = jnp.zeros_like(prev_g)
            prev_v[...] = jnp.zeros_like(prev_v)

        x = x_ref[...]
        outs = []
        for w_ref, cw_ref, cb_ref, prev, lo in ((wg_ref, cwg_ref, cbg_ref, prev_g, 0), (wv_ref, cwv_ref, cbv_ref, prev_v, FT)):
            ub = _dot(x, w_ref[...]).astype(BF16)
            up_ref[:, lo:lo + FT] = ub
            u = ub.astype(F32)
            before = prev[...]
            y = (cw_ref[2:3, :] * u + cw_ref[1:2, :] * _shift_down(u, before, 1)
                 + cw_ref[0:1, :] * _shift_down(u, before, 2) + cb_ref[...])
            prev[...] = u[TMF - 8:]
            yb = y.astype(BF16)
            gv_ref[:, lo:lo + FT] = yb
            outs.append(yb.astype(F32))
        gate, val = outs
        act_ref[...] = (gate * _sigmoid(gate) * val).astype(BF16)

    wspec = lambda off: pl.BlockSpec((D, FT), lambda n, m: (0, n + off))
    cwspec = lambda off: pl.BlockSpec((FK, FT), lambda n, m: (0, n + off))
    cbspec = lambda off: pl.BlockSpec((1, FT), lambda n, m: (0, n + off))
    pair = pl.BlockSpec((TMF, 2 * FT), lambda n, m: (m, n))
    return pl.pallas_call(
        body, name="ffn_up", grid=(NFT, T // TMF),
        in_specs=[pl.BlockSpec((TMF, D), lambda n, m: (m, 0)), wspec(0), wspec(NFT),
                  cwspec(0), cwspec(NFT), cbspec(0), cbspec(NFT)],
        out_specs=(pair, pair, pl.BlockSpec((TMF, FT), lambda n, m: (m, n))),
        out_shape=(SDS((T, 2 * DFF), BF16), SDS((T, 2 * DFF), BF16), SDS((T, DFF), BF16)),
        scratch_shapes=[pltpu.VMEM((8, FT), F32)] * 2,
    )(x1b, w_up, w_up, fcw, fcw, fcb, fcb)


def _ffn_down_loss(act, w_down, xh1, g1, b1, g2, b2, target):
    def body(a_ref, w_ref, xh1_ref, g1_ref, b1_ref, g2_ref, b2_ref, t_ref, dz_ref, loss_ref, gg_ref, gb_ref):
        @pl.when(pl.program_id(0) == 0)
        def _():
            loss_ref[...] = jnp.zeros_like(loss_ref)
            gg_ref[...] = jnp.zeros_like(gg_ref)
            gb_ref[...] = jnp.zeros_like(gb_ref)

        x1 = xh1_ref[...] * g1_ref[...] + b1_ref[...]
        z = ALPHA * x1 + _dot(a_ref[...], w_ref[...])
        xh, rstd = _layer_norm_fwd(z)
        diff = xh * g2_ref[...] + b2_ref[...] - t_ref[...]
        loss_ref[...] += 0.5 * _colsum(_rowmean(diff * diff))
        dout = diff * (1.0 / D)
        gg_ref[...] += _colsum(dout * xh)
        gb_ref[...] += _colsum(dout)
        dz_ref[...] = _layer_norm_bwd(dout, xh, rstd, g2_ref[...])

    vec = pl.BlockSpec((1, D), lambda m: (0, 0))
    row = pl.BlockSpec((TM, D), lambda m: (m, 0))
    return pl.pallas_call(
        body, name="ffn_down_loss", grid=(T // TM,),
        in_specs=[pl.BlockSpec((TM, DFF), lambda m: (m, 0)), pl.BlockSpec((DFF, D), lambda m: (0, 0)),
                  row, vec, vec, vec, vec, row],
        out_specs=(row, pl.BlockSpec((1, 1), lambda m: (0, 0)), vec, vec),
        out_shape=(SDS((T, D), F32), SDS((1, 1), F32), SDS((1, D), F32), SDS((1, D), F32)),
    )(act, w_down, xh1, g1, b1, g2, b2, target)


def _ffn_down_bwd(dz2, w_down, gv):
    def body(dz_ref, wd_ref, gv_ref, dup_ref, csg_ref, csv_ref):
        @pl.when(pl.program_id(1) == 0)
        def _():
            csg_ref[...] = jnp.zeros_like(csg_ref)
            csv_ref[...] = jnp.zeros_like(csv_ref)

        dact = _dot_nt(dz_ref[...].astype(BF16), wd_ref[...])
        gate = gv_ref[:, :FT].astype(F32)
        val = gv_ref[:, FT:].astype(F32)
        sg = _sigmoid(gate)
        gs = gate * sg
        dgate = dact * val * (sg + gs * (1.0 - sg))
        dval = dact * gs
        dup_ref[:, :FT] = dgate.astype(BF16)
        dup_ref[:, FT:] = dval.astype(BF16)
        csg_ref[...] += _colsum(dgate)
        csv_ref[...] += _colsum(dval)

    cs = pl.BlockSpec((1, FT), lambda n, m: (0, n))
    pair = pl.BlockSpec((TMF, 2 * FT), lambda n, m: (m, n))
    return pl.pallas_call(
        body, name="ffn_down_bwd", grid=(NFT, T // TMF),
        in_specs=[pl.BlockSpec((TMF, D), lambda n, m: (m, 0)), pl.BlockSpec((FT, D), lambda n, m: (n, 0)), pair],
        out_specs=(pair, cs, cs),
        out_shape=(SDS((T, 2 * DFF), BF16), SDS((1, DFF), F32), SDS((1, DFF), F32)),
    )(dz2, w_down, gv)


HALO = 16


def _conv3_transpose(dup, up, fcw_il):
    tiles = T // TMF

    def body(d_ref, h_ref, u_ref, w_ref, o_ref, gw_ref):
        m = pl.program_id(1)

        @pl.when(m == 0)
        def _():
            gw_ref[...] = jnp.zeros_like(gw_ref)

        d0 = d_ref[...].astype(F32)
        last = lax.bitwise_and(m + 1, S // TMF - 1) == 0
        after = jnp.where(last, 0.0, h_ref[...].astype(F32)[:8])
        d1 = _shift_up(d0, after, 1)
        d2 = _shift_up(d0, after, 2)
        o_ref[...] = (w_ref[2:3, :] * d0 + w_ref[1:2, :] * d1 + w_ref[0:1, :] * d2).astype(BF16)
        u = u_ref[...].astype(F32)
        for k, dk in enumerate((d2, d1, d0)):
            gw_ref[k:k + 1, :] += _colsum(dk * u)

    pair = pl.BlockSpec((TMF, 2 * FT), lambda n, m: (m, n))
    return pl.pallas_call(
        body, name="conv3_transpose", grid=(NFT, tiles),
        in_specs=[pair,
                  pl.BlockSpec((HALO, 2 * FT), lambda n, m: (jnp.minimum((m + 1) * (TMF // HALO), T // HALO - 1), n)),
                  pair, pl.BlockSpec((FK, 2 * FT), lambda n, m: (0, n))],
        out_specs=(pair, pl.BlockSpec((FK, 2 * FT), lambda n, m: (0, n))),
        out_shape=(SDS((T, 2 * DFF), BF16), SDS((FK, 2 * DFF), F32)),
    )(dup, dup, up, fcw_il)


def _ffn_up_bwd_ln1(dpre, w_up, dz2, xh1, rstd1, g1):
    def body(a_ref, w_ref, dz2_ref, xh_ref, rstd_ref, g_ref, dz1_ref, gg_ref, gb_ref):
        @pl.when(pl.program_id(0) == 0)
        def _():
            gg_ref[...] = jnp.zeros_like(gg_ref)
            gb_ref[...] = jnp.zeros_like(gb_ref)

        dx1 = ALPHA * dz2_ref[...]
        for n in range(NFT):
            for half in range(2):
                a = a_ref[:, (2 * n + half) * FT:(2 * n + half + 1) * FT]
                w = w_ref[:, (half * NFT + n) * FT:(half * NFT + n + 1) * FT]
                dx1 = dx1 + _dot_nt(a, w)
        xh = xh_ref[...]
        gg_ref[...] += _colsum(dx1 * xh)
        gb_ref[...] += _colsum(dx1)
        dz1_ref[...] = _layer_norm_bwd(dx1, xh, rstd_ref[...], g_ref[...])

    vec = pl.BlockSpec((1, D), lambda m: (0, 0))
    row = pl.BlockSpec((TMF, D), lambda m: (m, 0))
    return pl.pallas_call(
        body, name="ffn_up_bwd_ln1", grid=(T // TMF,),
        in_specs=[pl.BlockSpec((TMF, 2 * DFF), lambda m: (m, 0)), pl.BlockSpec((D, 2 * DFF), lambda m: (0, 0)),
                  row, row, pl.BlockSpec((TMF, 1), lambda m: (m, 0)), vec],
        out_specs=(row, vec, vec),
        out_shape=(SDS((T, D), F32), SDS((1, D), F32), SDS((1, D), F32)),
    )(dpre, w_up, dz2, xh1, rstd1, g1)


def _grad_w_up(dpre, x1b):
    tk = 1024

    def body(a_ref, b_ref, o_ref, acc):
        k = pl.program_id(1)

        @pl.when(k == 0)
        def _():
            acc[...] = jnp.zeros_like(acc)

        acc[...] += _dot_tn(a_ref[...], b_ref[...])

        @pl.when(k == T // tk - 1)
        def _():
            o_ref[0] = acc[pl.ds(0, FT), :].astype(o_ref.dtype)
            o_ref[1] = acc[pl.ds(FT, FT), :].astype(o_ref.dtype)

    out = pl.pallas_call(
        body, name="grad_w_up", grid=(NFT, T // tk),
        in_specs=[pl.BlockSpec((tk, 2 * FT), lambda n, k: (k, n)), pl.BlockSpec((tk, D), lambda n, k: (k, 0))],
        out_specs=pl.BlockSpec((2, FT, D), lambda n, k: (0, n, 0)),
        out_shape=SDS((2, DFF, D), GRAD_WIRE),
        scratch_shapes=[pltpu.VMEM((2 * FT, D), F32)],
    )(dpre, x1b)
    return out.reshape(2 * DFF, D)


def _row_tile(rows, cols):
    if rows * cols * 4 <= (1 << 20) or rows % 8:
        return rows
    for t in (256, 176, 128, 88, 64, 32, 16, 8):
        if rows % t == 0 and t * cols * 4 <= (1 << 20):
            return t
    return 8


def _sum8(r, name):
    _, rows, cols = r.shape
    tr = _row_tile(rows, cols)

    def body(r_ref, o_ref):
        acc = r_ref[0].astype(F32)
        for p in range(1, NDEV):
            acc = acc + r_ref[p].astype(F32)
        o_ref[...] = acc

    return pl.pallas_call(
        body, name=name, grid=(rows // tr,),
        in_specs=[pl.BlockSpec((NDEV, tr, cols), lambda i: (0, i, 0))],
        out_specs=pl.BlockSpec((tr, cols), lambda i: (i, 0)),
        out_shape=SDS((rows, cols), F32),
    )(r)


def _adamw(w, g, m, v, name):
    rows, cols = w.shape
    tr = _row_tile(rows, cols)

    def body(w_ref, g_ref, m_ref, v_ref, d_ref, nm_ref, nv_ref):
        g_ = g_ref[...]
        m_ = B1 * m_ref[...] + (1.0 - B1) * g_
        v_ = B2 * v_ref[...] + (1.0 - B2) * jnp.square(g_)
        m_hat = m_ / (1.0 - B1 ** STEP)
        v_hat = v_ / (1.0 - B2 ** STEP)
        d_ref[...] = -LR * (m_hat / (jnp.sqrt(v_hat) + AEPS) + WD * w_ref[...])
        nm_ref[...] = m_
        nv_ref[...] = v_

    spec = pl.BlockSpec((tr, cols), lambda i: (i, 0))
    shp = SDS((rows, cols), F32)
    return pl.pallas_call(
        body, name=name, grid=(rows // tr,), in_specs=[spec] * 4, out_specs=(spec,) * 3,
        out_shape=(shp, shp, shp),
    )(w, g, m, v)


def _interleave(a):
    r = a.shape[0]
    return a.reshape(r, 2, NFT, FT).transpose(0, 2, 1, 3).reshape(r, 2 * DFF)


def _deinterleave(a):
    r = a.shape[0]
    return a.reshape(r, NFT, 2, FT).transpose(0, 2, 1, 3).reshape(r, 2 * DFF)


def _local_step(x2, target, rel_table, w_in, b_in, conv_w, conv_b, conv_ln_g, conv_ln_b, attn_norm_g,
                conv_norm_g, late_weights, ln1_g, ln1_b, ffn_conv_w, ffn_conv_b, ln2_g, ln2_b, ship_ffn_grads):
    buckets = jnp.asarray(_bucket_maps())
    bias = _bias_table(rel_table, buckets)

    qkv = _mm_nn_bias(x2, w_in, b_in, 0, 3, AW, BF16, "proj_qkv")
    ag = _mm_nn_bias(x2, w_in, b_in, 3 * AW // CW, 2, CW, F32, "proj_ag")
    attn, lse = _attn_fwd(qkv, bias)
    u1 = _conv_fwd(ag, conv_w, conv_b)
    mixed = _mix_fwd(attn, u1, attn_norm_g, conv_norm_g, conv_ln_g, conv_ln_b)
    w_out, w_up, w_down = late_weights(mixed)
    xh1, rstd1, x1b = _out_proj_ln1(mixed, w_out, x2, ln1_g, ln1_b)
    up, gv, act = _ffn_up(x1b, w_up, ffn_conv_w, ffn_conv_b)
    dz2, loss, g_ln2_g, g_ln2_b = _ffn_down_loss(act, w_down, xh1, ln1_g, ln1_b, ln2_g, ln2_b, target)

    dup, cs_g, cs_v = _ffn_down_bwd(dz2, w_down, gv)
    g_w_down = _mm_tn(act, dz2, DFF // 2, 512, "grad_w_down")
    dpre, gfw_il = _conv3_transpose(dup, up, _interleave(ffn_conv_w))
    dz1, g_ln1_g, g_ln1_b = _ffn_up_bwd_ln1(dpre, w_up, dz2, xh1, rstd1, ln1_g)
    zero = ship_ffn_grads(g_w_down, _grad_w_up(dpre, x1b))
    dmixed = _mm_nt(dz1, w_out, None, 0.0, "dmixed")
    g_w_out = _mm_tn(mixed, dz1, D, 512, "grad_w_out")
    dattn, du1, g_an, g_cn, g_clg, g_clb, g_cb = _mix_bwd(
        dmixed, attn, u1, attn_norm_g + zero, conv_norm_g, conv_ln_g, conv_ln_b)
    dag, cs_ag, g_conv_w = _conv_bwd(du1, ag, conv_w)
    dq, dk, dv, cs_q, cs_k, cs_v2, dbias = _attn_bwd(qkv, attn, lse, dattn, bias)
    g_rel = _rel_table_grad(dbias, buckets)
    dh = jnp.concatenate([dq, dk, dv, dag], axis=1)
    grad_x = _mm_nt(dh, w_in, dz1, ALPHA, "grad_x")
    g_w_in_t = _mm_tn(dh, x2, INW // 2, 512, "grad_w_in")

    grads = dict(
        rel_table=g_rel,
        b_in=jnp.concatenate([cs_q, cs_k, cs_v2, cs_ag], axis=1),
        conv_b=g_cb, conv_ln_g=g_clg, conv_ln_b=g_clb, attn_norm_g=g_an, conv_norm_g=g_cn,
        ln1_g=g_ln1_g, ln1_b=g_ln1_b,
        ffn_conv_b=jnp.concatenate([cs_g, cs_v], axis=1),
        ln2_g=g_ln2_g, ln2_b=g_ln2_b,
        conv_w=g_conv_w[:CK],
        ffn_conv_w=_deinterleave(gfw_il),
        w_in_t=g_w_in_t, w_out=g_w_out,
    )
    return loss, grad_x, grads


SMALL = (("rel_table", (NBUCKET, NH)), ("b_in", (1, INW)), ("conv_b", (1, CW)), ("conv_ln_g", (1, CW)),
         ("conv_ln_b", (1, CW)), ("attn_norm_g", (1, AW)), ("conv_norm_g", (1, CW)), ("ln1_g", (1, D)),
         ("ln1_b", (1, D)), ("ffn_conv_b", (1, 2 * DFF)), ("ln2_g", (1, D)), ("ln2_b", (1, D)))
SHARDED_SMALL = (("conv_w", (CK, CW)), ("ffn_conv_w", (FK, 2 * DFF)))


def _pack(parts):
    flat = jnp.concatenate([p.reshape(-1) for p in parts])
    tile = 8 * PACK_LANES
    pad = (-flat.shape[0]) % tile
    return jnp.pad(flat, (0, pad)).reshape(-1, PACK_LANES)


def _unpack(packed, specs):
    flat = packed.reshape(-1)
    out, off = {}, 0
    for name, shp in specs:
        size = int(np.prod(shp))
        out[name] = flat[off:off + size].reshape(shp)
        off += size
    return out


def kernel(x, rel_table, w_in, b_in, conv_w, conv_b, conv_ln_g, conv_ln_b, attn_norm_g, conv_norm_g, w_out, ln1_g, ln1_b, w_up, ffn_conv_w, ffn_conv_b, w_down, ln2_g, ln2_b, loss_target, m_rel_table, m_w_in, m_b_in, m_conv_w, m_conv_b, m_conv_ln_g, m_conv_ln_b, m_attn_norm_g, m_conv_norm_g, m_w_out, m_ln1_g, m_ln1_b, m_w_up, m_ffn_conv_w, m_ffn_conv_b, m_w_down, m_ln2_g, m_ln2_b, v_rel_table, v_w_in, v_b_in, v_conv_w, v_conv_b, v_conv_ln_g, v_conv_ln_b, v_attn_norm_g, v_conv_norm_g, v_w_out, v_ln1_g, v_ln1_b, v_w_up, v_ffn_conv_w, v_ffn_conv_b, v_w_down, v_ln2_g, v_ln2_b):
    given = dict(locals())
    me = 4 * lax.axis_index("x") + 2 * lax.axis_index("y") + lax.axis_index("c")

    cols = lambda a: a.transpose(1, 0, 2).reshape(a.shape[1], NDEV * a.shape[2])
    rows = lambda a: a.reshape(NDEV * a.shape[1], a.shape[2])
    stack = lambda a: a.reshape(NDEV, a.shape[0] // NDEV, a.shape[1])

    first = _exchange([(w_in[0].astype(BF16), "gather"), (conv_w[0], "gather"), (ffn_conv_w[0], "gather")],
                      "gather_first")
    w_in_f, conv_w_f, ffn_conv_w_f = cols(first[0]), cols(first[1]), cols(first[2])
    late_own = [w_out[0].astype(BF16), w_up[0].astype(BF16), w_down[0].astype(BF16)]
    late_state, zero1 = _exchange_start([(a, "gather") for a in late_own], "gather_late_start")

    def late_weights(after):
        lands = _exchange_wait(late_state, after, "gather_late_wait")
        w_out_g, w_up_g, w_down_g = [_own_slot(l, o, me) for l, o in zip(lands, late_own)]
        return rows(w_out_g), cols(w_up_g), rows(w_down_g)

    shipped = {}

    def ship_ffn_grads(g_w_down, g_w_up_t):
        shipped["own"] = [stack(g_w_down), stack(g_w_up_t)]
        shipped["state"], zero2 = _exchange_start([(a, "scatter") for a in shipped["own"]], "ffn_grads_start")
        return zero2

    loss, grad_x, g = _local_step(
        x.reshape(T, D), loss_target.reshape(T, D), rel_table, w_in_f, b_in + zero1, conv_w_f, conv_b, conv_ln_g,
        conv_ln_b, attn_norm_g, conv_norm_g, late_weights, ln1_g, ln1_b, ffn_conv_w_f, ffn_conv_b,
        ln2_g, ln2_b, ship_ffn_grads)

    small_specs = SMALL + SHARDED_SMALL
    packed = _pack([g[n] for n, _ in small_specs])
    got = _exchange([(stack(g["w_in_t"]), "scatter"), (stack(g["w_out"]), "scatter"), (packed, "gather")],
                    "exchange_grads")
    ffn_lands = _exchange_wait(shipped["state"], got[2], "ffn_grads_wait")
    got_down, got_up = [_own_slot(l, lax.dynamic_index_in_dim(o, me, 0, keepdims=False), me)
                        for l, o in zip(ffn_lands, shipped["own"])]

    grad = {}
    grad["w_in"] = _sum8(got[0], "sum_w_in").T[None]
    grad["w_out"] = _sum8(got[1], "sum_w_out")[None]
    grad["w_up"] = _sum8(got_up, "sum_w_up").T[None]
    grad["w_down"] = _sum8(got_down, "sum_w_down")[None]
    small = _unpack(_sum8(got[2], "sum_small"), small_specs)
    for n, _ in SMALL:
        grad[n] = small[n]
    grad["conv_w"] = lax.dynamic_slice_in_dim(small["conv_w"], me * (CW // NDEV), CW // NDEV, axis=1)[None]
    grad["ffn_conv_w"] = lax.dynamic_slice_in_dim(small["ffn_conv_w"], me * (2 * DFF // NDEV), 2 * DFF // NDEV, axis=1)[None]

    delta, new_m, new_v = {}, {}, {}
    for n in ("w_in", "w_out", "w_up", "w_down", "conv_w", "ffn_conv_w"):
        shp = given[n].shape
        two = lambda a: a.reshape(shp[-2], shp[-1])
        d_, m_, v_ = _adamw(two(given[n]), two(grad[n]), two(given["m_" + n]), two(given["v_" + n]), "adamw_" + n)
        delta[n], new_m[n], new_v[n] = d_.reshape(shp), m_.reshape(shp), v_.reshape(shp)
    wp = _pack([given[n] for n, _ in SMALL])
    gp = _pack([grad[n] for n, _ in SMALL])
    mp = _pack([given["m_" + n] for n, _ in SMALL])
    vp = _pack([given["v_" + n] for n, _ in SMALL])
    dp, nmp, nvp = _adamw(wp, gp, mp, vp, "adamw_small")
    for dst, src in ((delta, dp), (new_m, nmp), (new_v, nvp)):
        dst.update(_unpack(src, SMALL))

    order = ("rel_table", "w_in", "b_in", "conv_w", "conv_b", "conv_ln_g", "conv_ln_b", "attn_norm_g",
             "conv_norm_g", "w_out", "ln1_g", "ln1_b", "w_up", "ffn_conv_w", "ffn_conv_b", "w_down", "ln2_g", "ln2_b")
    total_loss = lax.psum(loss[0, 0], ("x", "y", "c"))
    return (total_loss, grad_x.reshape(BL, S, D), *[grad[n] for n in order], *[delta[n] for n in order],
            *[new_m[n] for n in order], *[new_v[n] for n in order])
```

```python
import functools
import math

import numpy as np
import jax
import jax.numpy as jnp
from jax import lax
from jax.experimental import pallas as pl
from jax.experimental.pallas import tpu as pltpu

F32 = jnp.float32
BF16 = jnp.bfloat16
SDS = jax.ShapeDtypeStruct

NDEV = 8
D = 1024
S = 2048
BL = 2
T = BL * S
NH = 12
HD = 64
AW = NH * HD
CW = D - AW
INW = 3 * AW + 2 * CW
CK = 31
DFF = 2816
FK = 3
BLK = 128
NBUCKET = 32
BRANCHES = ((128, 1), (512, 4), (2048, 16))
ALPHA = 2.0 ** 0.25
LN_EPS = 1e-5
NEG_INF = -1e30
LR, B1, B2, AEPS, WD, STEP = 0.001, 0.9, 0.999, 1e-08, 0.01, 10

TM = 512
FT = 1408
NFT = DFF // FT
TMF = 256
PACK_LANES = 128
GRAD_WIRE = BF16

assert all(w // d == BLK for w, d in BRANCHES)


def _dot(a, b):
    return jnp.dot(a, b, preferred_element_type=F32)


def _dot_nt(a, b):
    return lax.dot_general(a, b, (((1,), (1,)), ((), ())), preferred_element_type=F32)


def _dot_tn(a, b):
    return lax.dot_general(a, b, (((0,), (0,)), ((), ())), preferred_element_type=F32)


def _rowmean(v):
    return jnp.mean(v, axis=-1, keepdims=True)


def _colsum(v):
    return jnp.sum(v, axis=0, keepdims=True)


def _sigmoid(v):
    return jax.nn.sigmoid(v)


def _exchange(items, name):
    n = len(items)
    arrs = [a for a, _ in items]
    kinds = [k for _, k in items]
    out_shapes = []
    for a, k in items:
        shp = (NDEV,) + tuple(a.shape) if k == "gather" else tuple(a.shape)
        out_shapes.append(SDS(shp, a.dtype))

    def body(*refs):
        ins = refs[:n]
        outs = refs[n:2 * n]
        send_sems, recv_sems, local_sems = refs[2 * n:]
        x, y, c = lax.axis_index("x"), lax.axis_index("y"), lax.axis_index("c")
        me = 4 * x + 2 * y + c

        def peer(k):
            px = 1 - x if k & 4 else x
            py = 1 - y if k & 2 else y
            pc = 1 - c if k & 1 else c
            return (px, py, pc), 4 * px + 2 * py + pc

        local = []
        for i in range(n):
            src = ins[i] if kinds[i] == "gather" else ins[i].at[me]
            cp = pltpu.make_async_copy(src, outs[i].at[me], local_sems.at[i])
            cp.start()
            local.append(cp)
        sends = []
        for k in range(1, NDEV):
            dev, pid = peer(k)
            for i in range(n):
                src = ins[i] if kinds[i] == "gather" else ins[i].at[pid]
                cp = pltpu.make_async_remote_copy(
                    src_ref=src, dst_ref=outs[i].at[me],
                    send_sem=send_sems.at[i, k - 1], recv_sem=recv_sems.at[i, k - 1],
                    device_id=dev, device_id_type=pl.DeviceIdType.MESH)
                cp.start()
                sends.append(cp)
        for k in range(1, NDEV):
            dev, pid = peer(k)
            for i in range(n):
                src = ins[i] if kinds[i] == "gather" else ins[i].at[pid]
                pltpu.make_async_remote_copy(
                    src_ref=src, dst_ref=outs[i].at[pid],
                    send_sem=send_sems.at[i, k - 1], recv_sem=recv_sems.at[i, k - 1],
                    device_id=dev, device_id_type=pl.DeviceIdType.MESH).wait_recv()
        for cp in sends:
            cp.wait_send()
        for cp in local:
            cp.wait()

    any_spec = pl.BlockSpec(memory_space=pl.ANY)
    return pl.pallas_call(
        body, name=name,
        out_shape=tuple(out_shapes),
        in_specs=[any_spec] * n,
        out_specs=tuple([any_spec] * n),
        scratch_shapes=[pltpu.SemaphoreType.DMA((n, NDEV - 1)),
                        pltpu.SemaphoreType.DMA((n, NDEV - 1)),
                        pltpu.SemaphoreType.DMA((n,))],
        compiler_params=pltpu.CompilerParams(has_side_effects=True),
    )(*arrs)


_HBM = pl.BlockSpec(memory_space=pltpu.HBM)
_SEM = pl.BlockSpec(memory_space=pltpu.SEMAPHORE)
_EFFECT = pltpu.SideEffectType.DATAFLOW_SIDE_EFFECTING


def _peer_of(k):
    x, y, c = lax.axis_index("x"), lax.axis_index("y"), lax.axis_index("c")
    px = 1 - x if k & 4 else x
    py = 1 - y if k & 2 else y
    pc = 1 - c if k & 1 else c
    return (px, py, pc), 4 * px + 2 * py + pc


def _split_copies(kinds, ins, lands, send_sems, recv_sems, started):
    me = 4 * lax.axis_index("x") + 2 * lax.axis_index("y") + lax.axis_index("c")
    out = []
    for k in range(1, NDEV):
        dev, pid = _peer_of(k)
        for i, kind in enumerate(kinds):
            src = ins[i] if kind == "gather" else ins[i].at[pid]
            dst = lands[i].at[me] if started else lands[i].at[pid]
            slot = i * (NDEV - 1) + k - 1
            out.append(pltpu.make_async_remote_copy(
                src_ref=src, dst_ref=dst, send_sem=send_sems.at[slot], recv_sem=recv_sems.at[slot],
                device_id=dev, device_id_type=pl.DeviceIdType.MESH))
    return out


def _exchange_start(items, name):
    n = len(items)
    kinds = [k for _, k in items]
    srcs = [pltpu.with_memory_space_constraint(a, pltpu.HBM) for a, _ in items]
    lands = []
    for a, k in items:
        shp = (NDEV,) + tuple(a.shape) if k == "gather" else tuple(a.shape)
        lands.append(pltpu.with_memory_space_constraint(lax.empty(shp, a.dtype), pltpu.HBM))

    def body(*refs):
        ins, land_refs = refs[:n], refs[n:2 * n]
        send_sems, recv_sems = refs[2 * n], refs[2 * n + 1]
        token = refs[-1]
        for cp in _split_copies(kinds, ins, land_refs, send_sems, recv_sems, True):
            cp.start()
        token[...] = jnp.zeros_like(token)

    sems = pltpu.SemaphoreType.DMA((n * (NDEV - 1),))
    res = pl.pallas_call(
        body, name=name,
        out_shape=(sems, sems, *[pltpu.HBM(a.shape, a.dtype) for a in srcs + lands], SDS((8, 128), F32)),
        in_specs=[_HBM] * (2 * n),
        out_specs=(_SEM, _SEM, *[_HBM] * (2 * n), pl.BlockSpec(memory_space=pltpu.VMEM)),
        input_output_aliases={i: 2 + i for i in range(2 * n)},
        compiler_params=pltpu.CompilerParams(has_side_effects=_EFFECT),
    )(*srcs, *lands)
    return (kinds, res[0], res[1], list(res[2:2 + n]), list(res[2 + n:2 + 2 * n])), res[-1][0, 0]


def _exchange_wait(state, after, name):
    kinds, send_sems, recv_sems, srcs, lands = state
    n = len(kinds)

    def body(*refs):
        ins, land_refs = refs[:n], refs[n:2 * n]
        s_sems, r_sems = refs[2 * n], refs[2 * n + 1]
        for cp in _split_copies(kinds, ins, land_refs, s_sems, r_sems, False):
            cp.wait_send()
            cp.wait_recv()

    res = pl.pallas_call(
        body, name=name,
        out_shape=tuple(pltpu.HBM(a.shape, a.dtype) for a in srcs + lands),
        in_specs=[_HBM] * (2 * n) + [_SEM, _SEM, pl.BlockSpec(memory_space=pl.ANY)],
        out_specs=tuple([_HBM] * (2 * n)),
        input_output_aliases={i: i for i in range(2 * n)},
        compiler_params=pltpu.CompilerParams(has_side_effects=_EFFECT),
    )(*srcs, *lands, send_sems, recv_sems, after)
    return list(res[n:])


def _own_slot(land, own, me):
    return lax.dynamic_update_slice_in_dim(land, own[None].astype(land.dtype), me, axis=0)


def _mm_nt_bias(a, bt, bias, row_blk0, nblk, tn, out_dtype, name):
    m_, k_ = a.shape

    def body(a_ref, b_ref, bias_ref, o_ref):
        acc = _dot_nt(a_ref[...].astype(BF16), b_ref[...])
        o_ref[...] = (acc + bias_ref[...]).astype(o_ref.dtype)

    return pl.pallas_call(
        body, name=name, grid=(nblk, m_ // TM),
        in_specs=[pl.BlockSpec((TM, k_), lambda n, m: (m, 0)),
                  pl.BlockSpec((tn, k_), lambda n, m: (row_blk0 + n, 0)),
                  pl.BlockSpec((1, tn), lambda n, m: (0, row_blk0 + n))],
        out_specs=pl.BlockSpec((TM, tn), lambda n, m: (m, n)),
        out_shape=SDS((m_, nblk * tn), out_dtype),
    )(a, bt, bias)


def _mm_nt(a, b, name):
    m_, k_ = a.shape
    n_ = b.shape[0]

    def body(a_ref, b_ref, o_ref):
        o_ref[...] = _dot_nt(a_ref[...].astype(BF16), b_ref[...].astype(BF16))

    return pl.pallas_call(
        body, name=name, grid=(m_ // TM,),
        in_specs=[pl.BlockSpec((TM, k_), lambda m: (m, 0)), pl.BlockSpec((n_, k_), lambda m: (0, 0))],
        out_specs=pl.BlockSpec((TM, n_), lambda m: (m, 0)),
        out_shape=SDS((m_, n_), F32),
    )(a, b)


def _grad_x(pieces, w_in_t, dz1, zero):
    widths = [p.shape[1] for p in pieces]

    def body(*refs):
        p_refs = refs[:len(pieces)]
        w_ref, dz_ref, z_ref, o_ref = refs[len(pieces):]
        acc = ALPHA * dz_ref[...] + z_ref[...]
        r0 = 0
        for p_ref, wd in zip(p_refs, widths):
            acc = acc + _dot(p_ref[...], w_ref[pl.ds(r0, wd), :])
            r0 += wd
        o_ref[...] = acc

    row = pl.BlockSpec((TM, D), lambda m: (m, 0))
    return pl.pallas_call(
        body, name="grad_x", grid=(T // TM,),
        in_specs=[pl.BlockSpec((TM, wd), lambda m: (m, 0)) for wd in widths]
        + [pl.BlockSpec((INW, D), lambda m: (0, 0)), row, pl.BlockSpec((1, 1), lambda m: (0, 0))],
        out_specs=row,
        out_shape=SDS((T, D), F32),
    )(*pieces, w_in_t, dz1, zero)


def _grad_w_in(pieces, x2):
    widths = [p.shape[1] for p in pieces]
    tk = 512
    nk = T // tk

    def body(*refs):
        p_refs = refs[:len(pieces)]
        x_ref, o_ref, acc = refs[len(pieces):]
        k = pl.program_id(0)

        @pl.when(k == 0)
        def _():
            acc[...] = jnp.zeros_like(acc)

        xb = x_ref[...].astype(BF16)
        r0 = 0
        for p_ref, wd in zip(p_refs, widths):
            acc[pl.ds(r0, wd), :] += _dot_tn(p_ref[...], xb)
            r0 += wd

        @pl.when(k == nk - 1)
        def _():
            o_ref[...] = acc[...].astype(o_ref.dtype)

    return pl.pallas_call(
        body, name="grad_w_in", grid=(nk,),
        in_specs=[pl.BlockSpec((tk, wd), lambda k: (k, 0)) for wd in widths] + [pl.BlockSpec((tk, D), lambda k: (k, 0))],
        out_specs=pl.BlockSpec((INW, D), lambda k: (0, 0)),
        out_shape=SDS((INW, D), GRAD_WIRE),
        scratch_shapes=[pltpu.VMEM((INW, D), F32)],
    )(*pieces, x2)


def _mm_tn(a, b, tn, tk, name):
    t_, na = a.shape
    nb = b.shape[1]
    nk = t_ // tk

    def body(a_ref, b_ref, o_ref, acc):
        k = pl.program_id(1)

        @pl.when(k == 0)
        def _():
            acc[...] = jnp.zeros_like(acc)

        acc[...] += _dot_tn(a_ref[...].astype(BF16), b_ref[...].astype(BF16))

        @pl.when(k == nk - 1)
        def _():
            o_ref[...] = acc[...].astype(o_ref.dtype)

    return pl.pallas_call(
        body, name=name, grid=(na // tn, nk),
        in_specs=[pl.BlockSpec((tk, tn), lambda n, k: (k, n)),
                  pl.BlockSpec((tk, nb), lambda n, k: (k, 0))],
        out_specs=pl.BlockSpec((tn, nb), lambda n, k: (n, 0)),
        out_shape=SDS((na, nb), GRAD_WIRE),
        scratch_shapes=[pltpu.VMEM((tn, nb), F32)],
    )(a, b)


def _bucket_maps():
    qi = np.arange(BLK)[:, None]
    kj = np.arange(2 * BLK)[None, :]
    steps = np.maximum(qi + BLK - kj, 0)
    exact = NBUCKET // 2
    maps = []
    for _, dil in BRANCHES:
        dist = steps * dil
        d_f = np.maximum(dist, 1).astype(np.float32)
        large = exact + (np.log(d_f / np.float32(exact)) / np.float32(math.log(S / exact))
                         * np.float32(NBUCKET - exact)).astype(np.int32)
        large = np.minimum(large, NBUCKET - 1)
        maps.append(np.where(dist < exact, dist, large).astype(np.int32))
    return np.stack(maps)


def _bias_table(rel_table, buckets):
    def body(t_ref, b_ref, o_ref):
        bk = b_ref[0]
        for h in range(NH):
            acc = jnp.zeros((BLK, 2 * BLK), F32)
            for k in range(NBUCKET):
                acc = jnp.where(bk == k, t_ref[k, h], acc)
            o_ref[0, h] = acc

    return pl.pallas_call(
        body, name="bias_table", grid=(len(BRANCHES),),
        in_specs=[pl.BlockSpec(memory_space=pltpu.SMEM),
                  pl.BlockSpec((1, BLK, 2 * BLK), lambda i: (i, 0, 0))],
        out_specs=pl.BlockSpec((1, NH, BLK, 2 * BLK), lambda i: (i, 0, 0, 0)),
        out_shape=SDS((len(BRANCHES), NH, BLK, 2 * BLK), F32),
    )(rel_table, buckets)


def _rel_table_grad(dbias, buckets):
    def body(d_ref, b_ref, o_ref):
        h = pl.program_id(0)
        for k in range(NBUCKET):
            tot = jnp.zeros((1, 1), F32)
            for br in range(len(BRANCHES)):
                sel = jnp.where(b_ref[br] == k, d_ref[br, 0], 0.0)
                tot = tot + jnp.sum(jnp.sum(sel, axis=1, keepdims=True), axis=0, keepdims=True)
            o_ref[0, :, pl.ds(k, 1)] = tot

    out = pl.pallas_call(
        body, name="rel_table_grad", grid=(NH,),
        in_specs=[pl.BlockSpec((len(BRANCHES), 1, BLK, 2 * BLK), lambda h: (0, h, 0, 0)),
                  pl.BlockSpec((len(BRANCHES), BLK, 2 * BLK), lambda h: (0, 0, 0))],
        out_specs=pl.BlockSpec((1, 1, NBUCKET), lambda h: (h, 0, 0)),
        out_shape=SDS((NH, 1, NBUCKET), F32),
    )(dbias, buckets)
    return out.reshape(NH, NBUCKET).T


def _block_rows(br, i):
    _, dil = BRANCHES[br]
    nb = S // dil // BLK
    if nb == 16:
        r, nidx = 0, i
    elif nb == 4:
        r, nidx = lax.shift_right_logical(i, 2), lax.bitwise_and(i, 3)
    else:
        r, nidx = i, 0
    start = r + dil * BLK * nidx
    if nb == 1:
        return start, None, None
    prev = r + dil * BLK * jnp.maximum(nidx - 1, 0)
    return start, prev, nidx > 0


def _rows(start, dil):
    if dil == 1:
        return pl.ds(pl.multiple_of(start, BLK), BLK)
    return pl.ds(start, BLK, stride=dil)


def _attn_masks():
    lane = lax.broadcasted_iota(jnp.int32, (BLK, BLK), 1)
    qi = lax.broadcasted_iota(jnp.int32, (BLK, BLK), 0)
    head0 = lane < HD
    valid_cur = lane <= qi
    valid_prev = lane >= qi
    return head0, valid_cur, valid_prev


def _attn_fwd_v1(qkv, bias):
    scale = 1.0 / math.sqrt(HD)
    nbr = len(BRANCHES)

    def body(q_ref, k_ref, v_ref, bias_ref, o_ref, lse_ref, qf, kf, vf, ob, mb, lb):
        qf[...] = q_ref[...].astype(F32)
        kf[...] = k_ref[...].astype(F32)
        vf[...] = v_ref[...].astype(F32)
        head0, valid_cur, valid_prev = _attn_masks()

        for br in range(nbr):
            dil = BRANCHES[br][1]

            def blk(i, carry, br=br, dil=dil):
                start, prev, has_prev = _block_rows(br, i)
                rows = _rows(start, dil)
                q = qf[rows, :]
                kc = kf[rows, :].astype(BF16)
                vc = vf[rows, :].astype(BF16)
                if prev is not None:
                    prows = _rows(prev, dil)
                    kp = kf[prows, :].astype(BF16)
                    vp = vf[prows, :].astype(BF16)
                    ok_prev = jnp.logical_and(valid_prev, has_prev)
                o_acc = jnp.zeros((BLK, BLK), F32)
                m_acc = jnp.zeros((BLK, BLK), F32)
                l_acc = jnp.zeros((BLK, BLK), F32)
                for j in range(2):
                    mj = head0 if j == 0 else jnp.logical_not(head0)
                    qj = jnp.where(mj, q, 0.0).astype(BF16)
                    sc = _dot_nt(qj, kc) * scale + bias_ref[br, j, :, BLK:]
                    sc = jnp.where(valid_cur, sc, NEG_INF)
                    mx = jnp.max(sc, axis=-1, keepdims=True)
                    if prev is not None:
                        sp = _dot_nt(qj, kp) * scale + bias_ref[br, j, :, :BLK]
                        sp = jnp.where(ok_prev, sp, NEG_INF)
                        mx = jnp.maximum(mx, jnp.max(sp, axis=-1, keepdims=True))
                    pc = jnp.exp(sc - mx)
                    ls = jnp.sum(pc, axis=-1, keepdims=True)
                    o = _dot(pc.astype(BF16), vc)
                    if prev is not None:
                        pp = jnp.exp(sp - mx)
                        ls = ls + jnp.sum(pp, axis=-1, keepdims=True)
                        o = o + _dot(pp.astype(BF16), vp)
                    o_acc = jnp.where(mj, o, o_acc)
                    m_acc = jnp.where(mj, mx, m_acc)
                    l_acc = jnp.where(mj, ls, l_acc)
                ob[br, rows, :] = o_acc
                mb[br, rows, :] = m_acc
                lb[br, rows, :] = l_acc
                return carry

            lax.fori_loop(0, 16, blk, 0)

        def merge(i, carry):
            rows = pl.ds(pl.multiple_of(i * 256, 256), 256)
            m_all = jnp.maximum(jnp.maximum(mb[0, rows, :], mb[1, rows, :]), mb[2, rows, :])
            num = jnp.zeros((256, BLK), F32)
            den = jnp.zeros((256, BLK), F32)
            for br in range(nbr):
                c = jnp.exp(mb[br, rows, :] - m_all)
                num = num + ob[br, rows, :] * c
                den = den + lb[br, rows, :] * c
            o_ref[rows, :] = num / den
            lse_ref[rows, :] = m_all + jnp.log(den)
            return carry

        lax.fori_loop(0, S // 256, merge, 0)

    npair = NH // 2
    blk_spec = lambda off: pl.BlockSpec((S, BLK), lambda b, hp: (b, off + hp))
    return pl.pallas_call(
        body, name="attn_fwd", grid=(BL, npair),
        in_specs=[blk_spec(0), blk_spec(npair), blk_spec(2 * npair),
                  pl.BlockSpec((nbr, 2, BLK, 2 * BLK), lambda b, hp: (0, hp, 0, 0))],
        out_specs=(blk_spec(0), blk_spec(0)),
        out_shape=(SDS((T, AW), F32), SDS((T, AW), F32)),
        scratch_shapes=[pltpu.VMEM((S, BLK), F32)] * 3 + [pltpu.VMEM((nbr, S, BLK), F32)] * 3,
    )(qkv, qkv, qkv, bias)


def _attn_bwd_v1(qkv, attn, lse, dattn, bias):
    scale = 1.0 / math.sqrt(HD)
    nbr = len(BRANCHES)

    def body(q_ref, k_ref, v_ref, o_ref, lse_ref, do_ref, bias_ref,
             dq_ref, dk_ref, dv_ref, sq_ref, sk_ref, sv_ref, db_ref,
             qf, kf, vf, dl, dqa, dka, dva):
        b = pl.program_id(1)
        qf[...] = q_ref[...].astype(F32)
        kf[...] = k_ref[...].astype(F32)
        vf[...] = v_ref[...].astype(F32)
        dqa[...] = jnp.zeros_like(dqa)
        dka[...] = jnp.zeros_like(dka)
        dva[...] = jnp.zeros_like(dva)
        head0, valid_cur, valid_prev = _attn_masks()

        @pl.when(b == 0)
        def _():
            db_ref[...] = jnp.zeros_like(db_ref)
            sq_ref[...] = jnp.zeros_like(sq_ref)
            sk_ref[...] = jnp.zeros_like(sk_ref)
            sv_ref[...] = jnp.zeros_like(sv_ref)

        def delta(i, carry):
            rows = pl.ds(pl.multiple_of(i * 256, 256), 256)
            prod = do_ref[rows, :] * o_ref[rows, :]
            h0 = lax.broadcasted_iota(jnp.int32, (256, BLK), 1) < HD
            d0 = jnp.sum(jnp.where(h0, prod, 0.0), axis=-1, keepdims=True)
            d1 = jnp.sum(jnp.where(h0, 0.0, prod), axis=-1, keepdims=True)
            dl[rows, :] = jnp.where(h0, d0, d1)
            return carry

        lax.fori_loop(0, S // 256, delta, 0)

        for br in range(nbr):
            dil = BRANCHES[br][1]

            def blk(i, carry, br=br, dil=dil):
                start, prev, has_prev = _block_rows(br, i)
                rows = _rows(start, dil)
                q = qf[rows, :]
                kc = kf[rows, :].astype(BF16)
                vc = vf[rows, :].astype(BF16)
                do = do_ref[rows, :]
                lse_b = lse_ref[rows, :]
                dl_b = dl[rows, :]
                if prev is not None:
                    prows = _rows(prev, dil)
                    kp = kf[prows, :].astype(BF16)
                    vp = vf[prows, :].astype(BF16)
                    ok_prev = jnp.logical_and(valid_prev, has_prev)
                    dk_p = jnp.zeros((BLK, BLK), F32)
                    dv_p = jnp.zeros((BLK, BLK), F32)
                dq = jnp.zeros((BLK, BLK), F32)
                dk_c = jnp.zeros((BLK, BLK), F32)
                dv_c = jnp.zeros((BLK, BLK), F32)
                for j in range(2):
                    mj = head0 if j == 0 else jnp.logical_not(head0)
                    qj = jnp.where(mj, q, 0.0).astype(BF16)
                    doj = jnp.where(mj, do, 0.0).astype(BF16)
                    lse_j = lse_b[:, j * HD:j * HD + 1]
                    dl_j = dl_b[:, j * HD:j * HD + 1]
                    sc = _dot_nt(qj, kc) * scale + bias_ref[br, j, :, BLK:]
                    pc = jnp.where(valid_cur, jnp.exp(sc - lse_j), 0.0)
                    ds_c = pc * (_dot_nt(doj, vc) - dl_j)
                    db_ref[br, j, :, BLK:] += ds_c
                    dsb = (ds_c * scale).astype(BF16)
                    dqj = _dot(dsb, kc)
                    dk_c = dk_c + _dot_tn(dsb, qj)
                    dv_c = dv_c + _dot_tn(pc.astype(BF16), doj)
                    if prev is not None:
                        sp = _dot_nt(qj, kp) * scale + bias_ref[br, j, :, :BLK]
                        pp = jnp.where(ok_prev, jnp.exp(sp - lse_j), 0.0)
                        ds_p = pp * (_dot_nt(doj, vp) - dl_j)
                        db_ref[br, j, :, :BLK] += ds_p
                        dsbp = (ds_p * scale).astype(BF16)
                        dqj = dqj + _dot(dsbp, kp)
                        dk_p = dk_p + _dot_tn(dsbp, qj)
                        dv_p = dv_p + _dot_tn(pp.astype(BF16), doj)
                    dq = jnp.where(mj, dqj, dq)
                dqa[rows, :] = dqa[rows, :] + dq
                dka[rows, :] = dka[rows, :] + dk_c
                dva[rows, :] = dva[rows, :] + dv_c
                if prev is not None:
                    dka[prows, :] = dka[prows, :] + dk_p
                    dva[prows, :] = dva[prows, :] + dv_p
                return carry

            lax.fori_loop(0, 16, blk, 0)

        def flush(i, carry):
            rows = pl.ds(pl.multiple_of(i * 256, 256), 256)
            for acc, out, cs in ((dqa, dq_ref, sq_ref), (dka, dk_ref, sk_ref), (dva, dv_ref, sv_ref)):
                val = acc[rows, :]
                out[rows, :] = val.astype(BF16)
                cs[...] += _colsum(val)
            return carry

        lax.fori_loop(0, S // 256, flush, 0)

    npair = NH // 2
    blk_spec = lambda off: pl.BlockSpec((S, BLK), lambda hp, b: (b, off + hp))
    sum_spec = pl.BlockSpec((1, BLK), lambda hp, b: (0, hp))
    return pl.pallas_call(
        body, name="attn_bwd", grid=(npair, BL),
        in_specs=[blk_spec(0), blk_spec(npair), blk_spec(2 * npair), blk_spec(0), blk_spec(0), blk_spec(0),
                  pl.BlockSpec((nbr, 2, BLK, 2 * BLK), lambda hp, b: (0, hp, 0, 0))],
        out_specs=(blk_spec(0), blk_spec(0), blk_spec(0), sum_spec, sum_spec, sum_spec,
                   pl.BlockSpec((nbr, 2, BLK, 2 * BLK), lambda hp, b: (0, hp, 0, 0))),
        out_shape=(SDS((T, AW), BF16), SDS((T, AW), BF16), SDS((T, AW), BF16),
                   SDS((1, AW), F32), SDS((1, AW), F32), SDS((1, AW), F32),
                   SDS((nbr, NH, BLK, 2 * BLK), F32)),
        scratch_shapes=[pltpu.VMEM((S, BLK), F32)] * 7,
    )(qkv, qkv, qkv, attn, lse, dattn, bias)


PADK = BLK
SCALE = 1.0 / math.sqrt(HD)
ATTN_UNROLL = 8


def _branch_geometry(br):
    dil = BRANCHES[br][1]
    sub = S // dil
    return dil, sub, sub // BLK


def _token_rows(br, i):
    dil, _, nblk = _branch_geometry(br)
    if dil == 1:
        return pl.ds(pl.multiple_of(i * BLK, BLK), BLK), i
    r = lax.shift_right_logical(i, nblk.bit_length() - 1)
    n = lax.bitwise_and(i, nblk - 1)
    return pl.ds(r + dil * BLK * n, BLK, stride=dil), n


def _sub_layout_loop(br, step):
    dil, sub, _ = _branch_geometry(br)
    rows = min(sub, 256)
    nchunk = sub // rows

    def it_step(it, carry):
        if dil == 1:
            src = pl.ds(pl.multiple_of(it * rows, rows), rows)
        else:
            r = lax.shift_right_logical(it, nchunk.bit_length() - 1)
            src = pl.ds(r + dil * rows * lax.bitwise_and(it, nchunk - 1), rows, stride=dil)
        step(src, pl.multiple_of(it * rows, BLK), rows)
        return carry

    lax.fori_loop(0, dil * nchunk, it_step, 0)


def _masked_bias(bias_ref, bm):
    qi = lax.broadcasted_iota(jnp.int32, (BLK, 2 * BLK), 0)
    kj = lax.broadcasted_iota(jnp.int32, (BLK, 2 * BLK), 1)
    first = jnp.logical_and(kj >= BLK, kj - BLK <= qi)
    valid = jnp.logical_or(first, jnp.logical_and(kj < BLK, kj >= qi))
    for br in range(len(BRANCHES)):
        for j in range(2):
            b = bias_ref[br, j]
            bm[br, 1, pl.ds(j * BLK, BLK), :] = jnp.where(valid, b, NEG_INF)
            bm[br, 0, pl.ds(j * BLK, BLK), :] = jnp.where(first, b, NEG_INF)


def _head_split(fn):
    def split(t):
        h0 = lax.broadcasted_iota(jnp.int32, t.shape, 1) < HD
        t = fn(t)
        return jnp.where(h0, t, 0.0).astype(BF16), jnp.where(h0, 0.0, t).astype(BF16)
    return split


def _attn_fwd(qkv, bias):
    nbr = len(BRANCHES)

    def body(q_ref, k_ref, v_ref, bias_ref, o_ref, lse_ref, qf, kf, vf, qs0, qs1, ks, vs, bm, ob, mb, lb):
        qf[...] = q_ref[...].astype(F32)
        kf[...] = k_ref[...].astype(F32)
        vf[...] = v_ref[...].astype(F32)
        _masked_bias(bias_ref, bm)
        ks[pl.ds(0, PADK), :] = jnp.zeros((PADK, BLK), BF16)
        vs[pl.ds(0, PADK), :] = jnp.zeros((PADK, BLK), BF16)
        head0 = lax.broadcasted_iota(jnp.int32, (BLK, BLK), 1) < HD
        split_q = _head_split(lambda t: t * SCALE)

        for br in range(nbr):
            nblk = _branch_geometry(br)[2]

            def stage(src, off, rows):
                qs0[pl.ds(off, rows), :], qs1[pl.ds(off, rows), :] = split_q(qf[src, :])
                ks[pl.ds(PADK + off, rows), :] = kf[src, :].astype(BF16)
                vs[pl.ds(PADK + off, rows), :] = vf[src, :].astype(BF16)

            _sub_layout_loop(br, stage)

            def blk(i, carry, br=br, nblk=nblk):
                base = pl.multiple_of(i * BLK, BLK)
                rows, n = _token_rows(br, i)
                q01 = jnp.concatenate([qs0[pl.ds(base, BLK), :], qs1[pl.ds(base, BLK), :]], axis=0)
                if nblk > 1:
                    kcat = ks[pl.ds(base, 2 * BLK), :]
                    vcat = vs[pl.ds(base, 2 * BLK), :]
                    s = _dot_nt(q01, kcat) + bm[br, jnp.minimum(n, 1)]
                else:
                    kcat = ks[pl.ds(PADK + base, BLK), :]
                    vcat = vs[pl.ds(PADK + base, BLK), :]
                    s = _dot_nt(q01, kcat) + bm[br, 0, :, BLK:]
                mx = jnp.max(s, axis=-1, keepdims=True)
                p = jnp.exp(s - mx)
                ls = jnp.sum(p, axis=-1, keepdims=True)
                o = _dot(p.astype(BF16), vcat)
                ob[br, rows, :] = jnp.where(head0, o[:BLK], o[BLK:])
                mb[br, rows, :] = jnp.where(head0, mx[:BLK], mx[BLK:])
                lb[br, rows, :] = jnp.where(head0, ls[:BLK], ls[BLK:])
                return carry

            lax.fori_loop(0, 16, blk, 0, unroll=ATTN_UNROLL)

        def merge(i, carry):
            rows = pl.ds(pl.multiple_of(i * 256, 256), 256)
            m_all = jnp.maximum(jnp.maximum(mb[0, rows, :], mb[1, rows, :]), mb[2, rows, :])
            num = jnp.zeros((256, BLK), F32)
            den = jnp.zeros((256, BLK), F32)
            for br in range(nbr):
                c = jnp.exp(mb[br, rows, :] - m_all)
                num = num + ob[br, rows, :] * c
                den = den + lb[br, rows, :] * c
            o_ref[rows, :] = num / den
            lse_ref[rows, :] = m_all + jnp.log(den)
            return carry

        lax.fori_loop(0, S // 256, merge, 0)

    npair = NH // 2
    blk_spec = lambda off: pl.BlockSpec((S, BLK), lambda b, hp: (b, off + hp))
    return pl.pallas_call(
        body, name="attn_fwd", grid=(BL, npair),
        in_specs=[blk_spec(0), blk_spec(npair), blk_spec(2 * npair),
                  pl.BlockSpec((nbr, 2, BLK, 2 * BLK), lambda b, hp: (0, hp, 0, 0))],
        out_specs=(blk_spec(0), blk_spec(0)),
        out_shape=(SDS((T, AW), F32), SDS((T, AW), F32)),
        scratch_shapes=[pltpu.VMEM((S, BLK), F32)] * 3 + [pltpu.VMEM((S, BLK), BF16)] * 2
        + [pltpu.VMEM((PADK + S, BLK), BF16)] * 2 + [pltpu.VMEM((nbr, 2, 2 * BLK, 2 * BLK), F32)]
        + [pltpu.VMEM((nbr, S, BLK), F32)] * 3,
    )(qkv, qkv, qkv, bias)


def _attn_bwd(qkv, attn, lse, dattn, bias):
    nbr = len(BRANCHES)

    def body(q_ref, k_ref, v_ref, o_ref, lse_ref, do_ref, bias_ref,
             dq_ref, dk_ref, dv_ref, sq_ref, sk_ref, sv_ref, db_ref,
             qf, kf, vf, dl, dqa, dka, dva, qs0, qs1, ds0, ds1, ks, vs, dks, dvs, bm):
        b = pl.program_id(1)
        qf[...] = q_ref[...].astype(F32)
        kf[...] = k_ref[...].astype(F32)
        vf[...] = v_ref[...].astype(F32)
        dqa[...] = jnp.zeros_like(dqa)
        dka[...] = jnp.zeros_like(dka)
        dva[...] = jnp.zeros_like(dva)
        _masked_bias(bias_ref, bm)
        ks[pl.ds(0, PADK), :] = jnp.zeros((PADK, BLK), BF16)
        vs[pl.ds(0, PADK), :] = jnp.zeros((PADK, BLK), BF16)
        head0 = lax.broadcasted_iota(jnp.int32, (BLK, BLK), 1) < HD
        split_q = _head_split(lambda t: t * SCALE)
        split_do = _head_split(lambda t: t)

        @pl.when(b == 0)
        def _():
            db_ref[...] = jnp.zeros_like(db_ref)
            sq_ref[...] = jnp.zeros_like(sq_ref)
            sk_ref[...] = jnp.zeros_like(sk_ref)
            sv_ref[...] = jnp.zeros_like(sv_ref)

        def delta(i, carry):
            rows = pl.ds(pl.multiple_of(i * 256, 256), 256)
            prod = do_ref[rows, :] * o_ref[rows, :]
            h0 = lax.broadcasted_iota(jnp.int32, (256, BLK), 1) < HD
            d0 = jnp.sum(jnp.where(h0, prod, 0.0), axis=-1, keepdims=True)
            d1 = jnp.sum(jnp.where(h0, 0.0, prod), axis=-1, keepdims=True)
            dl[rows, :] = jnp.where(h0, d0, d1)
            return carry

        lax.fori_loop(0, S // 256, delta, 0)

        for br in range(nbr):
            nblk = _branch_geometry(br)[2]

            def stage(src, off, rows):
                qs0[pl.ds(off, rows), :], qs1[pl.ds(off, rows), :] = split_q(qf[src, :])
                ds0[pl.ds(off, rows), :], ds1[pl.ds(off, rows), :] = split_do(do_ref[src, :])
                ks[pl.ds(PADK + off, rows), :] = kf[src, :].astype(BF16)
                vs[pl.ds(PADK + off, rows), :] = vf[src, :].astype(BF16)

            _sub_layout_loop(br, stage)
            dks[...] = jnp.zeros_like(dks)
            dvs[...] = jnp.zeros_like(dvs)

            def blk(i, carry, br=br, nblk=nblk):
                base = pl.multiple_of(i * BLK, BLK)
                rows, n = _token_rows(br, i)
                q01 = jnp.concatenate([qs0[pl.ds(base, BLK), :], qs1[pl.ds(base, BLK), :]], axis=0)
                do01 = jnp.concatenate([ds0[pl.ds(base, BLK), :], ds1[pl.ds(base, BLK), :]], axis=0)
                lse_b = lse_ref[rows, :]
                dl_b = dl[rows, :]
                lse01 = jnp.concatenate([lse_b[:, 0:1], lse_b[:, HD:HD + 1]], axis=0)
                dl01 = jnp.concatenate([dl_b[:, 0:1], dl_b[:, HD:HD + 1]], axis=0)
                if nblk > 1:
                    krows = pl.ds(base, 2 * BLK)
                    bias_m = bm[br, jnp.minimum(n, 1)]
                else:
                    krows = pl.ds(PADK + base, BLK)
                    bias_m = bm[br, 0, :, BLK:]
                kcat = ks[krows, :]
                vcat = vs[krows, :]
                p = jnp.exp(_dot_nt(q01, kcat) + bias_m - lse01)
                dsv = p * (_dot_nt(do01, vcat) - dl01)
                if nblk > 1:
                    db_ref[br, 0] += dsv[:BLK]
                    db_ref[br, 1] += dsv[BLK:]
                else:
                    db_ref[br, 0, :, BLK:] += dsv[:BLK]
                    db_ref[br, 1, :, BLK:] += dsv[BLK:]
                dsb = dsv.astype(BF16)
                dq01 = _dot(dsb, kcat)
                dqa[rows, :] = dqa[rows, :] + jnp.where(head0, dq01[:BLK], dq01[BLK:])
                dks[krows, :] = dks[krows, :] + _dot_tn(dsb, q01)
                dvs[krows, :] = dvs[krows, :] + _dot_tn(p.astype(BF16), do01)
                return carry

            lax.fori_loop(0, 16, blk, 0, unroll=ATTN_UNROLL)

            def fold(src, off, rows):
                dka[src, :] = dka[src, :] + dks[pl.ds(PADK + off, rows), :]
                dva[src, :] = dva[src, :] + dvs[pl.ds(PADK + off, rows), :]

            _sub_layout_loop(br, fold)

        def flush(i, carry):
            rows = pl.ds(pl.multiple_of(i * 256, 256), 256)
            for acc, out, cs, mul in ((dqa, dq_ref, sq_ref, SCALE), (dka, dk_ref, sk_ref, 1.0), (dva, dv_ref, sv_ref, 1.0)):
                val = acc[rows, :] * mul
                out[rows, :] = val.astype(BF16)
                cs[...] += _colsum(val)
            return carry

        lax.fori_loop(0, S // 256, flush, 0)

    npair = NH // 2
    blk_spec = lambda off: pl.BlockSpec((S, BLK), lambda hp, b: (b, off + hp))
    sum_spec = pl.BlockSpec((1, BLK), lambda hp, b: (0, hp))
    return pl.pallas_call(
        body, name="attn_bwd", grid=(npair, BL),
        in_specs=[blk_spec(0), blk_spec(npair), blk_spec(2 * npair), blk_spec(0), blk_spec(0), blk_spec(0),
                  pl.BlockSpec((nbr, 2, BLK, 2 * BLK), lambda hp, b: (0, hp, 0, 0))],
        out_specs=(blk_spec(0), blk_spec(0), blk_spec(0), sum_spec, sum_spec, sum_spec,
                   pl.BlockSpec((nbr, 2, BLK, 2 * BLK), lambda hp, b: (0, hp, 0, 0))),
        out_shape=(SDS((T, AW), BF16), SDS((T, AW), BF16), SDS((T, AW), BF16),
                   SDS((1, AW), F32), SDS((1, AW), F32), SDS((1, AW), F32),
                   SDS((nbr, NH, BLK, 2 * BLK), F32)),
        scratch_shapes=[pltpu.VMEM((S, BLK), F32)] * 7 + [pltpu.VMEM((S, BLK), BF16)] * 4
        + [pltpu.VMEM((PADK + S, BLK), BF16)] * 2 + [pltpu.VMEM((PADK + S, BLK), F32)] * 2
        + [pltpu.VMEM((nbr, 2, 2 * BLK, 2 * BLK), F32)],
    )(qkv, qkv, qkv, attn, lse, dattn, bias)


CH = 256
PADR = 32


def _conv_fwd(ag, conv_w, conv_b):
    def body(ag_ref, w_ref, b_ref, u1_ref, u0p):
        u0p[pl.ds(0, PADR), :] = jnp.zeros((PADR, CW), F32)

        def glu(i, carry):
            t0 = pl.multiple_of(i * CH, CH)
            a = ag_ref[pl.ds(t0, CH), :CW]
            g = ag_ref[pl.ds(t0, CH), CW:]
            u0p[pl.ds(PADR + t0, CH), :] = a * _sigmoid(g)
            return carry

        lax.fori_loop(0, S // CH, glu, 0)

        def conv(i, carry):
            t0 = pl.multiple_of(i * CH, CH)
            win = u0p[pl.ds(t0, CH + PADR), :]
            acc = jnp.zeros((CH, CW), F32) + b_ref[...]
            for k in range(CK):
                off = PADR - (CK - 1) + k
                acc = acc + win[off:off + CH, :] * w_ref[k:k + 1, :]
            u1_ref[pl.ds(t0, CH), :] = acc
            return carry

        lax.fori_loop(0, S // CH, conv, 0)

    return pl.pallas_call(
        body, name="conv_fwd", grid=(BL,),
        in_specs=[pl.BlockSpec((S, 2 * CW), lambda b: (b, 0)),
                  pl.BlockSpec((CK, CW), lambda b: (0, 0)),
                  pl.BlockSpec((1, CW), lambda b: (0, 0))],
        out_specs=pl.BlockSpec((S, CW), lambda b: (b, 0)),
        out_shape=SDS((T, CW), F32),
        scratch_shapes=[pltpu.VMEM((S + PADR, CW), F32)],
    )(ag, conv_w, conv_b)


def _conv_post(u1, cg, cb):
    mu = _rowmean(u1)
    uc = u1 - mu
    rstd = lax.rsqrt(_rowmean(uc * uc) + LN_EPS)
    xh = uc * rstd
    u2 = xh * cg + cb
    sg = _sigmoid(u2)
    return xh, rstd, u2, sg, u2 * sg


def _mix_fwd(attn, u1, ga, gc, cg, cb):
    def body(a_ref, u_ref, ga_ref, gc_ref, cg_ref, cb_ref, o_ref):
        a = a_ref[...]
        ra = lax.rsqrt(_rowmean(a * a) + LN_EPS)
        o_ref[:, :AW] = (a * ra * ga_ref[...]).astype(BF16)
        _, _, _, _, u3 = _conv_post(u_ref[...], cg_ref[...], cb_ref[...])
        rc = lax.rsqrt(_rowmean(u3 * u3) + LN_EPS)
        o_ref[:, AW:] = (u3 * rc * gc_ref[...]).astype(BF16)

    vec = lambda w: pl.BlockSpec((1, w), lambda m: (0, 0))
    return pl.pallas_call(
        body, name="mix_fwd", grid=(T // TM,),
        in_specs=[pl.BlockSpec((TM, AW), lambda m: (m, 0)), pl.BlockSpec((TM, CW), lambda m: (m, 0)),
                  vec(AW), vec(CW), vec(CW), vec(CW)],
        out_specs=pl.BlockSpec((TM, D), lambda m: (m, 0)),
        out_shape=SDS((T, D), BF16),
    )(attn, u1, ga, gc, cg, cb)


def _mix_bwd(dmixed, attn, u1, ga, gc, cg, cb):
    def body(dm_ref, a_ref, u_ref, ga_ref, gc_ref, cg_ref, cb_ref,
             da_ref, du_ref, g_an, g_cn, g_lg, g_lb, g_cb):
        @pl.when(pl.program_id(0) == 0)
        def _():
            for r in (g_an, g_cn, g_lg, g_lb, g_cb):
                r[...] = jnp.zeros_like(r)

        a = a_ref[...]
        dna = dm_ref[:, :AW]
        ra = lax.rsqrt(_rowmean(a * a) + LN_EPS)
        g_an[...] += _colsum(dna * a * ra)
        dat = dna * ga_ref[...]
        da_ref[...] = ra * dat - a * (ra * ra * ra) * _rowmean(dat * a)

        xh, rstd, u2, sg, u3 = _conv_post(u_ref[...], cg_ref[...], cb_ref[...])
        dnc = dm_ref[:, AW:]
        rc = lax.rsqrt(_rowmean(u3 * u3) + LN_EPS)
        g_cn[...] += _colsum(dnc * u3 * rc)
        dut = dnc * gc_ref[...]
        du3 = rc * dut - u3 * (rc * rc * rc) * _rowmean(dut * u3)
        du2 = du3 * sg * (1.0 + u2 * (1.0 - sg))
        g_lg[...] += _colsum(du2 * xh)
        g_lb[...] += _colsum(du2)
        dxh = du2 * cg_ref[...]
        du1 = rstd * (dxh - _rowmean(dxh) - xh * _rowmean(dxh * xh))
        g_cb[...] += _colsum(du1)
        du_ref[...] = du1

    vec = lambda w: pl.BlockSpec((1, w), lambda m: (0, 0))
    return pl.pallas_call(
        body, name="mix_bwd", grid=(T // TM,),
        in_specs=[pl.BlockSpec((TM, D), lambda m: (m, 0)), pl.BlockSpec((TM, AW), lambda m: (m, 0)),
                  pl.BlockSpec((TM, CW), lambda m: (m, 0)), vec(AW), vec(CW), vec(CW), vec(CW)],
        out_specs=(pl.BlockSpec((TM, AW), lambda m: (m, 0)), pl.BlockSpec((TM, CW), lambda m: (m, 0)),
                   vec(AW), vec(CW), vec(CW), vec(CW), vec(CW)),
        out_shape=(SDS((T, AW), F32), SDS((T, CW), F32),
                   SDS((1, AW), F32), SDS((1, CW), F32), SDS((1, CW), F32), SDS((1, CW), F32), SDS((1, CW), F32)),
    )(dmixed, attn, u1, ga, gc, cg, cb)


def _conv_bwd(du1, ag, conv_w):
    def body(du_ref, ag_ref, w_ref, dag_ref, cs_ref, gw_ref, u0p, dup):
        @pl.when(pl.program_id(0) == 0)
        def _():
            cs_ref[...] = jnp.zeros_like(cs_ref)
            gw_ref[...] = jnp.zeros_like(gw_ref)

        u0p[pl.ds(0, PADR), :] = jnp.zeros((PADR, CW), F32)
        dup[pl.ds(S, PADR), :] = jnp.zeros((PADR, CW), F32)

        def fill(i, carry):
            t0 = pl.multiple_of(i * CH, CH)
            a = ag_ref[pl.ds(t0, CH), :CW]
            g = ag_ref[pl.ds(t0, CH), CW:]
            u0p[pl.ds(PADR + t0, CH), :] = a * _sigmoid(g)
            dup[pl.ds(t0, CH), :] = du_ref[pl.ds(t0, CH), :]
            return carry

        lax.fori_loop(0, S // CH, fill, 0)

        def chunk(i, carry):
            t0 = pl.multiple_of(i * CH, CH)
            d = dup[pl.ds(t0, CH), :]
            win_u = u0p[pl.ds(t0, CH + PADR), :]
            win_d = dup[pl.ds(t0, CH + PADR), :]
            du0 = jnp.zeros((CH, CW), F32)
            for k in range(CK):
                off = PADR - (CK - 1) + k
                gw_ref[k:k + 1, :] += _colsum(d * win_u[off:off + CH, :])
                fo = CK - 1 - k
                du0 = du0 + win_d[fo:fo + CH, :] * w_ref[k:k + 1, :]
            a = ag_ref[pl.ds(t0, CH), :CW]
            sg = _sigmoid(ag_ref[pl.ds(t0, CH), CW:])
            da = du0 * sg
            dg = du0 * a * sg * (1.0 - sg)
            dag_ref[pl.ds(t0, CH), :CW] = da.astype(BF16)
            dag_ref[pl.ds(t0, CH), CW:] = dg.astype(BF16)
            cs_ref[:, :CW] += _colsum(da)
            cs_ref[:, CW:] += _colsum(dg)
            return carry

        lax.fori_loop(0, S // CH, chunk, 0)

    return pl.pallas_call(
        body, name="conv_bwd", grid=(BL,),
        in_specs=[pl.BlockSpec((S, CW), lambda b: (b, 0)), pl.BlockSpec((S, 2 * CW), lambda b: (b, 0)),
                  pl.BlockSpec((CK, CW), lambda b: (0, 0))],
        out_specs=(pl.BlockSpec((S, 2 * CW), lambda b: (b, 0)),
                   pl.BlockSpec((1, 2 * CW), lambda b: (0, 0)),
                   pl.BlockSpec((PADR, CW), lambda b: (0, 0))),
        out_shape=(SDS((T, 2 * CW), BF16), SDS((1, 2 * CW), F32), SDS((PADR, CW), F32)),
        scratch_shapes=[pltpu.VMEM((S + PADR, CW), F32), pltpu.VMEM((S + PADR, CW), F32)],
    )(du1, ag, conv_w)


def _layer_norm_fwd(z):
    mu = _rowmean(z)
    zc = z - mu
    rstd = lax.rsqrt(_rowmean(zc * zc) + LN_EPS)
    return zc * rstd, rstd


def _layer_norm_bwd(dy, xh, rstd, g):
    dxh = dy * g
    return rstd * (dxh - _rowmean(dxh) - xh * _rowmean(dxh * xh))


def _out_proj_ln1(mixed, w_out, x2, g1, b1):
    def body(a_ref, w_ref, x_ref, g_ref, b_ref, xh_ref, rstd_ref, x1_ref):
        z = ALPHA * x_ref[...] + _dot(a_ref[...], w_ref[...])
        xh, rstd = _layer_norm_fwd(z)
        xh_ref[...] = xh
        rstd_ref[...] = rstd
        x1_ref[...] = (xh * g_ref[...] + b_ref[...]).astype(BF16)

    vec = pl.BlockSpec((1, D), lambda m: (0, 0))
    row = pl.BlockSpec((TM, D), lambda m: (m, 0))
    return pl.pallas_call(
        body, name="out_proj_ln1", grid=(T // TM,),
        in_specs=[row, pl.BlockSpec((D, D), lambda m: (0, 0)), row, vec, vec],
        out_specs=(row, pl.BlockSpec((TM, 1), lambda m: (m, 0)), row),
        out_shape=(SDS((T, D), F32), SDS((T, 1), F32), SDS((T, D), BF16)),
    )(mixed, w_out, x2, g1, b1)


def _seq_start(m):
    return lax.bitwise_and(m, S // TMF - 1) == 0


def _causal3(ext, w_ref, b_ref):
    x0 = ext[pl.ds(8, TM), :]
    x1 = ext[pl.ds(7, TM), :]
    x2 = ext[pl.ds(6, TM), :]
    y = w_ref[2:3, :] * x0 + w_ref[1:2, :] * x1 + w_ref[0:1, :] * x2 + b_ref[...]
    return y, x0, x1, x2


def _shift_down(x, before, k):
    rolled = pltpu.roll(x, k, 0)
    row = lax.broadcasted_iota(jnp.int32, before.shape, 0)
    head = jnp.where(row < k, pltpu.roll(before, k, 0), rolled[:8])
    return jnp.concatenate([head, rolled[8:]], axis=0)


def _shift_up(x, after, k):
    n = x.shape[0]
    rolled = pltpu.roll(x, n - k, 0)
    row = lax.broadcasted_iota(jnp.int32, after.shape, 0)
    tail = jnp.where(row >= 8 - k, pltpu.roll(after, 8 - k, 0), rolled[n - 8:])
    return jnp.concatenate([rolled[:n - 8], tail], axis=0)


def _ffn_up(x1b, w_up, fcw, fcb):
    def body(x_ref, wg_ref, wv_ref, cwg_ref, cwv_ref, cbg_ref, cbv_ref, up_ref, gv_ref, act_ref, prev_g, prev_v):
        @pl.when(_seq_start(pl.program_id(1)))
        def _():
            prev_g[...] = jnp.zeros_like(prev_g)
            prev_v[...] = jnp.zeros_like(prev_v)

        x = x_ref[...]
        outs = []
        for w_ref, cw_ref, cb_ref, prev, lo in ((wg_ref, cwg_ref, cbg_ref, prev_g, 0), (wv_ref, cwv_ref, cbv_ref, prev_v, FT)):
            ub = _dot_nt(x, w_ref[...]).astype(BF16)
            up_ref[:, lo:lo + FT] = ub
            u = ub.astype(F32)
            before = prev[...]
            y = (cw_ref[2:3, :] * u + cw_ref[1:2, :] * _shift_down(u, before, 1)
                 + cw_ref[0:1, :] * _shift_down(u, before, 2) + cb_ref[...])
            prev[...] = u[TMF - 8:]
            yb = y.astype(BF16)
            gv_ref[:, lo:lo + FT] = yb
            outs.append(yb.astype(F32))
        gate, val = outs
        act_ref[...] = (gate * _sigmoid(gate) * val).astype(BF16)

    wspec = lambda off: pl.BlockSpec((FT, D), lambda n, m: (n + off, 0))
    cwspec = lambda off: pl.BlockSpec((FK, FT), lambda n, m: (0, n + off))
    cbspec = lambda off: pl.BlockSpec((1, FT), lambda n, m: (0, n + off))
    pair = pl.BlockSpec((TMF, 2 * FT), lambda n, m: (m, n))
    return pl.pallas_call(
        body, name="ffn_up", grid=(NFT, T // TMF),
        in_specs=[pl.BlockSpec((TMF, D), lambda n, m: (m, 0)), wspec(0), wspec(NFT),
                  cwspec(0), cwspec(NFT), cbspec(0), cbspec(NFT)],
        out_specs=(pair, pair, pl.BlockSpec((TMF, FT), lambda n, m: (m, n))),
        out_shape=(SDS((T, 2 * DFF), BF16), SDS((T, 2 * DFF), BF16), SDS((T, DFF), BF16)),
        scratch_shapes=[pltpu.VMEM((8, FT), F32)] * 2,
    )(x1b, w_up, w_up, fcw, fcw, fcb, fcb)


def _ffn_down_loss(act, w_down, xh1, g1, b1, g2, b2, target):
    def body(a_ref, w_ref, xh1_ref, g1_ref, b1_ref, g2_ref, b2_ref, t_ref, dz_ref, loss_ref, gg_ref, gb_ref):
        @pl.when(pl.program_id(0) == 0)
        def _():
            loss_ref[...] = jnp.zeros_like(loss_ref)
            gg_ref[...] = jnp.zeros_like(gg_ref)
            gb_ref[...] = jnp.zeros_like(gb_ref)

        x1 = xh1_ref[...] * g1_ref[...] + b1_ref[...]
        z = ALPHA * x1 + _dot(a_ref[...], w_ref[...])
        xh, rstd = _layer_norm_fwd(z)
        diff = xh * g2_ref[...] + b2_ref[...] - t_ref[...]
        loss_ref[...] += 0.5 * _colsum(_rowmean(diff * diff))
        dout = diff * (1.0 / D)
        gg_ref[...] += _colsum(dout * xh)
        gb_ref[...] += _colsum(dout)
        dz_ref[...] = _layer_norm_bwd(dout, xh, rstd, g2_ref[...])

    vec = pl.BlockSpec((1, D), lambda m: (0, 0))
    row = pl.BlockSpec((TM, D), lambda m: (m, 0))
    return pl.pallas_call(
        body, name="ffn_down_loss", grid=(T // TM,),
        in_specs=[pl.BlockSpec((TM, DFF), lambda m: (m, 0)), pl.BlockSpec((DFF, D), lambda m: (0, 0)),
                  row, vec, vec, vec, vec, row],
        out_specs=(row, pl.BlockSpec((1, 1), lambda m: (0, 0)), vec, vec),
        out_shape=(SDS((T, D), F32), SDS((1, 1), F32), SDS((1, D), F32), SDS((1, D), F32)),
    )(act, w_down, xh1, g1, b1, g2, b2, target)


def _ffn_down_bwd(dz2, w_down, gv):
    def body(dz_ref, wd_ref, gv_ref, dup_ref, csg_ref, csv_ref):
        @pl.when(pl.program_id(1) == 0)
        def _():
            csg_ref[...] = jnp.zeros_like(csg_ref)
            csv_ref[...] = jnp.zeros_like(csv_ref)

        dact = _dot_nt(dz_ref[...].astype(BF16), wd_ref[...])
        gate = gv_ref[:, :FT].astype(F32)
        val = gv_ref[:, FT:].astype(F32)
        sg = _sigmoid(gate)
        gs = gate * sg
        dgate = dact * val * (sg + gs * (1.0 - sg))
        dval = dact * gs
        dup_ref[:, :FT] = dgate.astype(BF16)
        dup_ref[:, FT:] = dval.astype(BF16)
        csg_ref[...] += _colsum(dgate)
        csv_ref[...] += _colsum(dval)

    cs = pl.BlockSpec((1, FT), lambda n, m: (0, n))
    pair = pl.BlockSpec((TMF, 2 * FT), lambda n, m: (m, n))
    return pl.pallas_call(
        body, name="ffn_down_bwd", grid=(NFT, T // TMF),
        in_specs=[pl.BlockSpec((TMF, D), lambda n, m: (m, 0)), pl.BlockSpec((FT, D), lambda n, m: (n, 0)), pair],
        out_specs=(pair, cs, cs),
        out_shape=(SDS((T, 2 * DFF), BF16), SDS((1, DFF), F32), SDS((1, DFF), F32)),
    )(dz2, w_down, gv)


HALO = 16


def _conv3_transpose(dup, up, fcw_il):
    tiles = T // TMF

    def body(d_ref, h_ref, u_ref, w_ref, o_ref, gw_ref):
        m = pl.program_id(1)

        @pl.when(m == 0)
        def _():
            gw_ref[...] = jnp.zeros_like(gw_ref)

        d0 = d_ref[...].astype(F32)
        last = lax.bitwise_and(m + 1, S // TMF - 1) == 0
        after = jnp.where(last, 0.0, h_ref[...].astype(F32)[:8])
        d1 = _shift_up(d0, after, 1)
        d2 = _shift_up(d0, after, 2)
        o_ref[...] = (w_ref[2:3, :] * d0 + w_ref[1:2, :] * d1 + w_ref[0:1, :] * d2).astype(BF16)
        u = u_ref[...].astype(F32)
        for k, dk in enumerate((d2, d1, d0)):
            gw_ref[k:k + 1, :] += _colsum(dk * u)

    pair = pl.BlockSpec((TMF, 2 * FT), lambda n, m: (m, n))
    return pl.pallas_call(
        body, name="conv3_transpose", grid=(NFT, tiles),
        in_specs=[pair,
                  pl.BlockSpec((HALO, 2 * FT), lambda n, m: (jnp.minimum((m + 1) * (TMF // HALO), T // HALO - 1), n)),
                  pair, pl.BlockSpec((FK, 2 * FT), lambda n, m: (0, n))],
        out_specs=(pair, pl.BlockSpec((FK, 2 * FT), lambda n, m: (0, n))),
        out_shape=(SDS((T, 2 * DFF), BF16), SDS((FK, 2 * DFF), F32)),
    )(dup, dup, up, fcw_il)


def _ffn_up_bwd_ln1(dpre, w_up, dz2, xh1, rstd1, g1):
    def body(a_ref, w_ref, dz2_ref, xh_ref, rstd_ref, g_ref, dz1_ref, gg_ref, gb_ref):
        @pl.when(pl.program_id(0) == 0)
        def _():
            gg_ref[...] = jnp.zeros_like(gg_ref)
            gb_ref[...] = jnp.zeros_like(gb_ref)

        dx1 = ALPHA * dz2_ref[...]
        for n in range(NFT):
            for half in range(2):
                a = a_ref[:, (2 * n + half) * FT:(2 * n + half + 1) * FT]
                w = w_ref[pl.ds((half * NFT + n) * FT, FT), :]
                dx1 = dx1 + _dot(a, w)
        xh = xh_ref[...]
        gg_ref[...] += _colsum(dx1 * xh)
        gb_ref[...] += _colsum(dx1)
        dz1_ref[...] = _layer_norm_bwd(dx1, xh, rstd_ref[...], g_ref[...])

    vec = pl.BlockSpec((1, D), lambda m: (0, 0))
    row = pl.BlockSpec((TMF, D), lambda m: (m, 0))
    return pl.pallas_call(
        body, name="ffn_up_bwd_ln1", grid=(T // TMF,),
        in_specs=[pl.BlockSpec((TMF, 2 * DFF), lambda m: (m, 0)), pl.BlockSpec((2 * DFF, D), lambda m: (0, 0)),
                  row, row, pl.BlockSpec((TMF, 1), lambda m: (m, 0)), vec],
        out_specs=(row, vec, vec),
        out_shape=(SDS((T, D), F32), SDS((1, D), F32), SDS((1, D), F32)),
    )(dpre, w_up, dz2, xh1, rstd1, g1)


def _grad_w_up(dpre, x1b):
    tk = 1024

    def body(a_ref, b_ref, o_ref, acc):
        k = pl.program_id(1)

        @pl.when(k == 0)
        def _():
            acc[...] = jnp.zeros_like(acc)

        acc[...] += _dot_tn(a_ref[...], b_ref[...])

        @pl.when(k == T // tk - 1)
        def _():
            o_ref[0] = acc[pl.ds(0, FT), :].astype(o_ref.dtype)
            o_ref[1] = acc[pl.ds(FT, FT), :].astype(o_ref.dtype)

    out = pl.pallas_call(
        body, name="grad_w_up", grid=(NFT, T // tk),
        in_specs=[pl.BlockSpec((tk, 2 * FT), lambda n, k: (k, n)), pl.BlockSpec((tk, D), lambda n, k: (k, 0))],
        out_specs=pl.BlockSpec((2, FT, D), lambda n, k: (0, n, 0)),
        out_shape=SDS((2, DFF, D), GRAD_WIRE),
        scratch_shapes=[pltpu.VMEM((2 * FT, D), F32)],
    )(dpre, x1b)
    return out.reshape(2 * DFF, D)


def _row_tile(rows, cols):
    if rows * cols * 4 <= (1 << 20) or rows % 8:
        return rows
    for t in (256, 176, 128, 88, 64, 32, 16, 8):
        if rows % t == 0 and t * cols * 4 <= (1 << 20):
            return t
    return 8


def _sum8(r, name):
    _, rows, cols = r.shape
    tr = _row_tile(rows, cols)

    def body(r_ref, o_ref):
        acc = r_ref[0].astype(F32)
        for p in range(1, NDEV):
            acc = acc + r_ref[p].astype(F32)
        o_ref[...] = acc

    return pl.pallas_call(
        body, name=name, grid=(rows // tr,),
        in_specs=[pl.BlockSpec((NDEV, tr, cols), lambda i: (0, i, 0))],
        out_specs=pl.BlockSpec((tr, cols), lambda i: (i, 0)),
        out_shape=SDS((rows, cols), F32),
    )(r)


def _adamw(w, g, m, v, name):
    rows, cols = w.shape
    tr = _row_tile(rows, cols)

    def body(w_ref, g_ref, m_ref, v_ref, d_ref, nm_ref, nv_ref):
        g_ = g_ref[...]
        m_ = B1 * m_ref[...] + (1.0 - B1) * g_
        v_ = B2 * v_ref[...] + (1.0 - B2) * jnp.square(g_)
        m_hat = m_ / (1.0 - B1 ** STEP)
        v_hat = v_ / (1.0 - B2 ** STEP)
        d_ref[...] = -LR * (m_hat / (jnp.sqrt(v_hat) + AEPS) + WD * w_ref[...])
        nm_ref[...] = m_
        nv_ref[...] = v_

    spec = pl.BlockSpec((tr, cols), lambda i: (i, 0))
    shp = SDS((rows, cols), F32)
    return pl.pallas_call(
        body, name=name, grid=(rows // tr,), in_specs=[spec] * 4, out_specs=(spec,) * 3,
        out_shape=(shp, shp, shp),
    )(w, g, m, v)


def _adamw_many(ws, gs, ms, vs, name):
    n = len(ws)

    def body(*refs):
        for i in range(n):
            w_ref, g_ref, m_ref, v_ref, d_ref, nm_ref, nv_ref = refs[i::n]
            g_ = g_ref[...]
            m_ = B1 * m_ref[...] + (1.0 - B1) * g_
            v_ = B2 * v_ref[...] + (1.0 - B2) * jnp.square(g_)
            m_hat = m_ / (1.0 - B1 ** STEP)
            v_hat = v_ / (1.0 - B2 ** STEP)
            d_ref[...] = -LR * (m_hat / (jnp.sqrt(v_hat) + AEPS) + WD * w_ref[...])
            nm_ref[...] = m_
            nv_ref[...] = v_

    shapes = tuple(SDS(w.shape, F32) for w in ws)
    res = pl.pallas_call(body, name=name, out_shape=shapes * 3)(*ws, *gs, *ms, *vs)
    return res[:n], res[n:2 * n], res[2 * n:]


def _interleave(a):
    r = a.shape[0]
    return a.reshape(r, 2, NFT, FT).transpose(0, 2, 1, 3).reshape(r, 2 * DFF)


def _deinterleave(a):
    r = a.shape[0]
    return a.reshape(r, NFT, 2, FT).transpose(0, 2, 1, 3).reshape(r, 2 * DFF)


def _local_step(x2, target, rel_table, w_in_t, b_in, conv_w, conv_b, conv_ln_g, conv_ln_b, attn_norm_g,
                conv_norm_g, late_weights, ln1_g, ln1_b, ffn_conv_w, ffn_conv_b, ln2_g, ln2_b, ship_ffn_grads, ship_tail):
    buckets = jnp.asarray(_bucket_maps())
    bias = _bias_table(rel_table, buckets)

    qkv = _mm_nt_bias(x2, w_in_t, b_in, 0, 3, AW, BF16, "proj_qkv")
    ag = _mm_nt_bias(x2, w_in_t, b_in, 3 * AW // CW, 2, CW, F32, "proj_ag")
    attn, lse = _attn_fwd(qkv, bias)
    u1 = _conv_fwd(ag, conv_w, conv_b)
    mixed = _mix_fwd(attn, u1, attn_norm_g, conv_norm_g, conv_ln_g, conv_ln_b)
    w_out, w_up, w_down = late_weights(mixed)
    xh1, rstd1, x1b = _out_proj_ln1(mixed, w_out, x2, ln1_g, ln1_b)
    up, gv, act = _ffn_up(x1b, w_up, ffn_conv_w, ffn_conv_b)
    dz2, loss, g_ln2_g, g_ln2_b = _ffn_down_loss(act, w_down, xh1, ln1_g, ln1_b, ln2_g, ln2_b, target)

    dup, cs_g, cs_v = _ffn_down_bwd(dz2, w_down, gv)
    g_w_down = _mm_tn(act, dz2, DFF // 2, 512, "grad_w_down")
    dpre, gfw_il = _conv3_transpose(dup, up, _interleave(ffn_conv_w))
    dz1, g_ln1_g, g_ln1_b = _ffn_up_bwd_ln1(dpre, w_up, dz2, xh1, rstd1, ln1_g)
    g_w_out = _mm_tn(mixed, dz1, D, 512, "grad_w_out")
    zero = ship_ffn_grads(g_w_down, _grad_w_up(dpre, x1b), g_w_out)
    dmixed = _mm_nt(dz1, w_out, "dmixed")
    dattn, du1, g_an, g_cn, g_clg, g_clb, g_cb = _mix_bwd(
        dmixed, attn, u1, attn_norm_g + zero, conv_norm_g, conv_ln_g, conv_ln_b)
    dag, cs_ag, g_conv_w = _conv_bwd(du1, ag, conv_w)
    dq, dk, dv, cs_q, cs_k, cs_v2, dbias = _attn_bwd(qkv, attn, lse, dattn, bias)
    g_rel = _rel_table_grad(dbias, buckets)
    pieces = [dq, dk, dv, dag]
    g_w_in_t = _grad_w_in(pieces, x2)

    grads = dict(
        rel_table=g_rel,
        b_in=jnp.concatenate([cs_q, cs_k, cs_v2, cs_ag], axis=1),
        conv_b=g_cb, conv_ln_g=g_clg, conv_ln_b=g_clb, attn_norm_g=g_an, conv_norm_g=g_cn,
        ln1_g=g_ln1_g, ln1_b=g_ln1_b,
        ffn_conv_b=jnp.concatenate([cs_g, cs_v], axis=1),
        ln2_g=g_ln2_g, ln2_b=g_ln2_b,
        conv_w=g_conv_w[:CK],
        ffn_conv_w=_deinterleave(gfw_il),
    )
    zero11 = ship_tail(g_w_in_t, grads)
    grad_x = _grad_x(pieces, w_in_t, dz1, zero11)
    return loss, grad_x


SMALL = (("rel_table", (NBUCKET, NH)), ("b_in", (1, INW)), ("conv_b", (1, CW)), ("conv_ln_g", (1, CW)),
         ("conv_ln_b", (1, CW)), ("attn_norm_g", (1, AW)), ("conv_norm_g", (1, CW)), ("ln1_g", (1, D)),
         ("ln1_b", (1, D)), ("ffn_conv_b", (1, 2 * DFF)), ("ln2_g", (1, D)), ("ln2_b", (1, D)))
SHARDED_SMALL = (("conv_w", (CK, CW)), ("ffn_conv_w", (FK, 2 * DFF)))


def _pack(parts):
    flat = jnp.concatenate([p.reshape(-1) for p in parts])
    tile = 8 * PACK_LANES
    pad = (-flat.shape[0]) % tile
    return jnp.pad(flat, (0, pad)).reshape(-1, PACK_LANES)


def _unpack(packed, specs):
    flat = packed.reshape(-1)
    out, off = {}, 0
    for name, shp in specs:
        size = int(np.prod(shp))
        out[name] = flat[off:off + size].reshape(shp)
        off += size
    return out


def kernel(x, rel_table, w_in, b_in, conv_w, conv_b, conv_ln_g, conv_ln_b, attn_norm_g, conv_norm_g, w_out, ln1_g, ln1_b, w_up, ffn_conv_w, ffn_conv_b, w_down, ln2_g, ln2_b, loss_target, m_rel_table, m_w_in, m_b_in, m_conv_w, m_conv_b, m_conv_ln_g, m_conv_ln_b, m_attn_norm_g, m_conv_norm_g, m_w_out, m_ln1_g, m_ln1_b, m_w_up, m_ffn_conv_w, m_ffn_conv_b, m_w_down, m_ln2_g, m_ln2_b, v_rel_table, v_w_in, v_b_in, v_conv_w, v_conv_b, v_conv_ln_g, v_conv_ln_b, v_attn_norm_g, v_conv_norm_g, v_w_out, v_ln1_g, v_ln1_b, v_w_up, v_ffn_conv_w, v_ffn_conv_b, v_w_down, v_ln2_g, v_ln2_b):
    given = dict(locals())
    me = 4 * lax.axis_index("x") + 2 * lax.axis_index("y") + lax.axis_index("c")

    cols = lambda a: a.transpose(1, 0, 2).reshape(a.shape[1], NDEV * a.shape[2])
    rows = lambda a: a.reshape(NDEV * a.shape[1], a.shape[2])
    stack = lambda a: a.reshape(NDEV, a.shape[0] // NDEV, a.shape[1])

    small_specs = SMALL + SHARDED_SMALL
    grad, delta, new_m, new_v = {}, {}, {}, {}

    def adamw_big(n, g2d):
        shp = given[n].shape
        two = lambda a: a.reshape(shp[-2], shp[-1])
        grad[n] = g2d.reshape(shp)
        d_, m_, v_ = _adamw(two(given[n]), g2d, two(given["m_" + n]), two(given["v_" + n]), "adamw_" + n)
        delta[n], new_m[n], new_v[n] = d_.reshape(shp), m_.reshape(shp), v_.reshape(shp)
        return d_

    def own_slab(a):
        return lax.dynamic_index_in_dim(a, me, 0, keepdims=False)

    first = _exchange([(w_in[0].T.astype(BF16), "gather"), (conv_w[0], "gather"), (ffn_conv_w[0], "gather")],
                      "gather_first")
    w_in_t, conv_w_f, ffn_conv_w_f = rows(first[0]), cols(first[1]), cols(first[2])
    late_own = [w_out[0].astype(BF16), w_up[0].T.astype(BF16), w_down[0].astype(BF16)]
    late_state, zero1 = _exchange_start([(a, "gather") for a in late_own], "gather_late_start")

    def late_weights(after):
        lands = _exchange_wait(late_state, after, "gather_late_wait")
        return [rows(_own_slot(l, o, me)) for l, o in zip(lands, late_own)]

    shipped = {}

    def ship_ffn_grads(g_w_down, g_w_up_t, g_w_out):
        shipped["ffn_own"] = [stack(g_w_down), stack(g_w_up_t), stack(g_w_out)]
        shipped["ffn"], zero2 = _exchange_start([(a, "scatter") for a in shipped["ffn_own"]], "ffn_grads_start")
        return zero2

    def ship_tail(g_w_in_t, small_grads):
        shipped["tail_own"] = [stack(g_w_in_t), _pack([small_grads[n] for n, _ in small_specs])]
        shipped["tail"], zero3 = _exchange_start(
            [(shipped["tail_own"][0], "scatter"), (shipped["tail_own"][1], "gather")], "tail_grads_start")
        return zero3.reshape(1, 1)

    loss, grad_x = _local_step(
        x.reshape(T, D), loss_target.reshape(T, D), rel_table, w_in_t, b_in + zero1, conv_w_f, conv_b, conv_ln_g,
        conv_ln_b, attn_norm_g, conv_norm_g, late_weights, ln1_g, ln1_b, ffn_conv_w_f, ffn_conv_b,
        ln2_g, ln2_b, ship_ffn_grads, ship_tail)

    ffn_lands = _exchange_wait(shipped["ffn"], grad_x, "ffn_grads_wait")
    got_down, got_up, got_out = [_own_slot(l, own_slab(o), me) for l, o in zip(ffn_lands, shipped["ffn_own"])]
    adamw_big("w_down", _sum8(got_down, "sum_w_down"))
    adamw_big("w_up", _sum8(got_up, "sum_w_up").T)
    last = adamw_big("w_out", _sum8(got_out, "sum_w_out"))

    tail_lands = _exchange_wait(shipped["tail"], last, "tail_grads_wait")
    got_in = _own_slot(tail_lands[0], own_slab(shipped["tail_own"][0]), me)
    got_small = _own_slot(tail_lands[1], shipped["tail_own"][1], me)
    adamw_big("w_in", _sum8(got_in, "sum_w_in").T)
    small = _unpack(_sum8(got_small, "sum_small"), small_specs)
    small["conv_w"] = lax.dynamic_slice_in_dim(small["conv_w"], me * (CW // NDEV), CW // NDEV, axis=1)
    small["ffn_conv_w"] = lax.dynamic_slice_in_dim(small["ffn_conv_w"], me * (2 * DFF // NDEV), 2 * DFF // NDEV, axis=1)
    names = [n for n, _ in small_specs]
    two = lambda a: a.reshape(a.shape[-2], a.shape[-1])
    ds, nms, nvs = _adamw_many([two(given[n]) for n in names], [small[n] for n in names],
                               [two(given["m_" + n]) for n in names], [two(given["v_" + n]) for n in names], "adamw_small")
    for n, d_, m_, v_ in zip(names, ds, nms, nvs):
        shp = given[n].shape
        grad[n], delta[n], new_m[n], new_v[n] = small[n].reshape(shp), d_.reshape(shp), m_.reshape(shp), v_.reshape(shp)

    order = ("rel_table", "w_in", "b_in", "conv_w", "conv_b", "conv_ln_g", "conv_ln_b", "attn_norm_g",
             "conv_norm_g", "w_out", "ln1_g", "ln1_b", "w_up", "ffn_conv_w", "ffn_conv_b", "w_down", "ln2_g", "ln2_b")
    total_loss = lax.psum(loss[0, 0], ("x", "y", "c"))
    return (total_loss, grad_x.reshape(BL, S, D), *[grad[n] for n in order], *[delta[n] for n in order],
            *[new_m[n] for n in order], *[new_v[n] for n in order])
```

```python
import functools
import math

import numpy as np
import jax
import jax.numpy as jnp
from jax import lax
from jax.experimental import pallas as pl
from jax.experimental.pallas import tpu as pltpu

F32 = jnp.float32
BF16 = jnp.bfloat16
SDS = jax.ShapeDtypeStruct

NDEV = 8
D = 1024
S = 2048
BL = 2
T = BL * S
NH = 12
HD = 64
AW = NH * HD
CW = D - AW
INW = 3 * AW + 2 * CW
CK = 31
DFF = 2816
FK = 3
BLK = 128
NBUCKET = 32
BRANCHES = ((128, 1), (512, 4), (2048, 16))
ALPHA = 2.0 ** 0.25
LN_EPS = 1e-5
NEG_INF = -1e30
LR, B1, B2, AEPS, WD, STEP = 0.001, 0.9, 0.999, 1e-08, 0.01, 10

TM = 512
FT = 1408
NFT = DFF // FT
TMF = 256
PACK_LANES = 128
GRAD_WIRE = BF16

assert all(w // d == BLK for w, d in BRANCHES)


def _dot(a, b):
    return jnp.dot(a, b, preferred_element_type=F32)


def _dot_nt(a, b):
    return lax.dot_general(a, b, (((1,), (1,)), ((), ())), preferred_element_type=F32)


def _dot_tn(a, b):
    return lax.dot_general(a, b, (((0,), (0,)), ((), ())), preferred_element_type=F32)


def _rowmean(v):
    return jnp.mean(v, axis=-1, keepdims=True)


def _colsum(v):
    return jnp.sum(v, axis=0, keepdims=True)


def _sigmoid(v):
    return jax.nn.sigmoid(v)


def _exchange(items, name):
    n = len(items)
    arrs = [a for a, _ in items]
    kinds = [k for _, k in items]
    out_shapes = []
    for a, k in items:
        shp = (NDEV,) + tuple(a.shape) if k == "gather" else tuple(a.shape)
        out_shapes.append(SDS(shp, a.dtype))

    def body(*refs):
        ins = refs[:n]
        outs = refs[n:2 * n]
        send_sems, recv_sems, local_sems = refs[2 * n:]
        x, y, c = lax.axis_index("x"), lax.axis_index("y"), lax.axis_index("c")
        me = 4 * x + 2 * y + c

        def peer(k):
            px = 1 - x if k & 4 else x
            py = 1 - y if k & 2 else y
            pc = 1 - c if k & 1 else c
            return (px, py, pc), 4 * px + 2 * py + pc

        local = []
        for i in range(n):
            src = ins[i] if kinds[i] == "gather" else ins[i].at[me]
            cp = pltpu.make_async_copy(src, outs[i].at[me], local_sems.at[i])
            cp.start()
            local.append(cp)
        sends = []
        for k in range(1, NDEV):
            dev, pid = peer(k)
            for i in range(n):
                src = ins[i] if kinds[i] == "gather" else ins[i].at[pid]
                cp = pltpu.make_async_remote_copy(
                    src_ref=src, dst_ref=outs[i].at[me],
                    send_sem=send_sems.at[i, k - 1], recv_sem=recv_sems.at[i, k - 1],
                    device_id=dev, device_id_type=pl.DeviceIdType.MESH)
                cp.start()
                sends.append(cp)
        for k in range(1, NDEV):
            dev, pid = peer(k)
            for i in range(n):
                src = ins[i] if kinds[i] == "gather" else ins[i].at[pid]
                pltpu.make_async_remote_copy(
                    src_ref=src, dst_ref=outs[i].at[pid],
                    send_sem=send_sems.at[i, k - 1], recv_sem=recv_sems.at[i, k - 1],
                    device_id=dev, device_id_type=pl.DeviceIdType.MESH).wait_recv()
        for cp in sends:
            cp.wait_send()
        for cp in local:
            cp.wait()

    any_spec = pl.BlockSpec(memory_space=pl.ANY)
    return pl.pallas_call(
        body, name=name,
        out_shape=tuple(out_shapes),
        in_specs=[any_spec] * n,
        out_specs=tuple([any_spec] * n),
        scratch_shapes=[pltpu.SemaphoreType.DMA((n, NDEV - 1)),
                        pltpu.SemaphoreType.DMA((n, NDEV - 1)),
                        pltpu.SemaphoreType.DMA((n,))],
        compiler_params=pltpu.CompilerParams(has_side_effects=True),
    )(*arrs)


_HBM = pl.BlockSpec(memory_space=pltpu.HBM)
_SEM = pl.BlockSpec(memory_space=pltpu.SEMAPHORE)
_EFFECT = pltpu.SideEffectType.DATAFLOW_SIDE_EFFECTING


def _peer_of(k):
    x, y, c = lax.axis_index("x"), lax.axis_index("y"), lax.axis_index("c")
    px = 1 - x if k & 4 else x
    py = 1 - y if k & 2 else y
    pc = 1 - c if k & 1 else c
    return (px, py, pc), 4 * px + 2 * py + pc


def _split_copies(kinds, ins, lands, send_sems, recv_sems, started):
    me = 4 * lax.axis_index("x") + 2 * lax.axis_index("y") + lax.axis_index("c")
    out = []
    for k in range(1, NDEV):
        dev, pid = _peer_of(k)
        for i, kind in enumerate(kinds):
            src = ins[i] if kind == "gather" else ins[i].at[pid]
            dst = lands[i].at[me] if started else lands[i].at[pid]
            slot = i * (NDEV - 1) + k - 1
            out.append(pltpu.make_async_remote_copy(
                src_ref=src, dst_ref=dst, send_sem=send_sems.at[slot], recv_sem=recv_sems.at[slot],
                device_id=dev, device_id_type=pl.DeviceIdType.MESH))
    return out


def _exchange_start(items, name):
    n = len(items)
    kinds = [k for _, k in items]
    srcs = [pltpu.with_memory_space_constraint(a, pltpu.HBM) for a, _ in items]
    lands = []
    for a, k in items:
        shp = (NDEV,) + tuple(a.shape) if k == "gather" else tuple(a.shape)
        lands.append(pltpu.with_memory_space_constraint(lax.empty(shp, a.dtype), pltpu.HBM))

    def body(*refs):
        ins, land_refs = refs[:n], refs[n:2 * n]
        send_sems, recv_sems, own_sems = refs[2 * n:2 * n + 3]
        token = refs[-1]
        for cp in _own_copies(kinds, ins, land_refs, own_sems):
            cp.start()
        for cp in _split_copies(kinds, ins, land_refs, send_sems, recv_sems, True):
            cp.start()
        token[...] = jnp.zeros_like(token)

    sems = pltpu.SemaphoreType.DMA((n * (NDEV - 1),))
    res = pl.pallas_call(
        body, name=name,
        out_shape=(sems, sems, pltpu.SemaphoreType.DMA((n,)),
                   *[pltpu.HBM(a.shape, a.dtype) for a in srcs + lands], SDS((8, 128), F32)),
        in_specs=[_HBM] * (2 * n),
        out_specs=(_SEM, _SEM, _SEM, *[_HBM] * (2 * n), pl.BlockSpec(memory_space=pltpu.VMEM)),
        input_output_aliases={i: 3 + i for i in range(2 * n)},
        compiler_params=pltpu.CompilerParams(has_side_effects=_EFFECT),
    )(*srcs, *lands)
    return (kinds, res[0], res[1], res[2], list(res[3:3 + n]), list(res[3 + n:3 + 2 * n])), res[-1][0, 0]


def _own_copies(kinds, ins, lands, own_sems):
    me = 4 * lax.axis_index("x") + 2 * lax.axis_index("y") + lax.axis_index("c")
    return [pltpu.make_async_copy(ins[i] if kind == "gather" else ins[i].at[me], lands[i].at[me], own_sems.at[i])
            for i, kind in enumerate(kinds)]


def _exchange_wait(state, after, name):
    kinds, send_sems, recv_sems, own_sems, srcs, lands = state
    n = len(kinds)

    def body(*refs):
        ins, land_refs = refs[:n], refs[n:2 * n]
        s_sems, r_sems, o_sems = refs[2 * n:2 * n + 3]
        for cp in _split_copies(kinds, ins, land_refs, s_sems, r_sems, False):
            cp.wait_send()
            cp.wait_recv()
        for cp in _own_copies(kinds, ins, land_refs, o_sems):
            cp.wait()

    res = pl.pallas_call(
        body, name=name,
        out_shape=tuple(pltpu.HBM(a.shape, a.dtype) for a in srcs + lands),
        in_specs=[_HBM] * (2 * n) + [_SEM, _SEM, _SEM, pl.BlockSpec(memory_space=pl.ANY)],
        out_specs=tuple([_HBM] * (2 * n)),
        input_output_aliases={i: i for i in range(2 * n)},
        compiler_params=pltpu.CompilerParams(has_side_effects=_EFFECT),
    )(*srcs, *lands, send_sems, recv_sems, own_sems, after)
    return list(res[n:])


def _mm_nt_bias(a, bt, bias, row_blk0, nblk, tn, out_dtype, name):
    m_, k_ = a.shape

    def body(a_ref, b_ref, bias_ref, o_ref):
        acc = _dot_nt(a_ref[...].astype(BF16), b_ref[...])
        o_ref[...] = (acc + bias_ref[...]).astype(o_ref.dtype)

    return pl.pallas_call(
        body, name=name, grid=(nblk, m_ // TM),
        in_specs=[pl.BlockSpec((TM, k_), lambda n, m: (m, 0)),
                  pl.BlockSpec((tn, k_), lambda n, m: (row_blk0 + n, 0)),
                  pl.BlockSpec((1, tn), lambda n, m: (0, row_blk0 + n))],
        out_specs=pl.BlockSpec((TM, tn), lambda n, m: (m, n)),
        out_shape=SDS((m_, nblk * tn), out_dtype),
    )(a, bt, bias)


def _mm_nt(a, b, name):
    m_, k_ = a.shape
    n_ = b.shape[0]

    def body(a_ref, b_ref, o_ref):
        o_ref[...] = _dot_nt(a_ref[...].astype(BF16), b_ref[...].astype(BF16))

    return pl.pallas_call(
        body, name=name, grid=(m_ // TM,),
        in_specs=[pl.BlockSpec((TM, k_), lambda m: (m, 0)), pl.BlockSpec((n_, k_), lambda m: (0, 0))],
        out_specs=pl.BlockSpec((TM, n_), lambda m: (m, 0)),
        out_shape=SDS((m_, n_), F32),
    )(a, b)


def _grad_x(pieces, w_in_t, dz1, zero):
    widths = [p.shape[1] for p in pieces]

    def body(*refs):
        p_refs = refs[:len(pieces)]
        w_ref, dz_ref, z_ref, o_ref = refs[len(pieces):]
        acc = ALPHA * dz_ref[...] + z_ref[...]
        r0 = 0
        for p_ref, wd in zip(p_refs, widths):
            acc = acc + _dot(p_ref[...], w_ref[pl.ds(r0, wd), :])
            r0 += wd
        o_ref[...] = acc

    row = pl.BlockSpec((TM, D), lambda m: (m, 0))
    return pl.pallas_call(
        body, name="grad_x", grid=(T // TM,),
        in_specs=[pl.BlockSpec((TM, wd), lambda m: (m, 0)) for wd in widths]
        + [pl.BlockSpec((INW, D), lambda m: (0, 0)), row, pl.BlockSpec((1, 1), lambda m: (0, 0))],
        out_specs=row,
        out_shape=SDS((T, D), F32),
    )(*pieces, w_in_t, dz1, zero)


def _grad_w_in(pieces, x2):
    widths = [p.shape[1] for p in pieces]
    tk = 512
    nk = T // tk

    def body(*refs):
        p_refs = refs[:len(pieces)]
        x_ref, o_ref, acc = refs[len(pieces):]
        k = pl.program_id(0)

        @pl.when(k == 0)
        def _():
            acc[...] = jnp.zeros_like(acc)

        xb = x_ref[...].astype(BF16)
        r0 = 0
        for p_ref, wd in zip(p_refs, widths):
            acc[pl.ds(r0, wd), :] += _dot_tn(p_ref[...], xb)
            r0 += wd

        @pl.when(k == nk - 1)
        def _():
            o_ref[...] = acc[...].astype(o_ref.dtype)

    return pl.pallas_call(
        body, name="grad_w_in", grid=(nk,),
        in_specs=[pl.BlockSpec((tk, wd), lambda k: (k, 0)) for wd in widths] + [pl.BlockSpec((tk, D), lambda k: (k, 0))],
        out_specs=pl.BlockSpec((INW, D), lambda k: (0, 0)),
        out_shape=SDS((INW, D), GRAD_WIRE),
        scratch_shapes=[pltpu.VMEM((INW, D), F32)],
    )(*pieces, x2)


def _mm_tn(a, b, tn, tk, name):
    t_, na = a.shape
    nb = b.shape[1]
    nk = t_ // tk

    def body(a_ref, b_ref, o_ref, acc):
        k = pl.program_id(1)

        @pl.when(k == 0)
        def _():
            acc[...] = jnp.zeros_like(acc)

        acc[...] += _dot_tn(a_ref[...].astype(BF16), b_ref[...].astype(BF16))

        @pl.when(k == nk - 1)
        def _():
            o_ref[...] = acc[...].astype(o_ref.dtype)

    return pl.pallas_call(
        body, name=name, grid=(na // tn, nk),
        in_specs=[pl.BlockSpec((tk, tn), lambda n, k: (k, n)),
                  pl.BlockSpec((tk, nb), lambda n, k: (k, 0))],
        out_specs=pl.BlockSpec((tn, nb), lambda n, k: (n, 0)),
        out_shape=SDS((na, nb), GRAD_WIRE),
        scratch_shapes=[pltpu.VMEM((tn, nb), F32)],
    )(a, b)


def _bucket_maps():
    qi = np.arange(BLK)[:, None]
    kj = np.arange(2 * BLK)[None, :]
    steps = np.maximum(qi + BLK - kj, 0)
    exact = NBUCKET // 2
    maps = []
    for _, dil in BRANCHES:
        dist = steps * dil
        d_f = np.maximum(dist, 1).astype(np.float32)
        large = exact + (np.log(d_f / np.float32(exact)) / np.float32(math.log(S / exact))
                         * np.float32(NBUCKET - exact)).astype(np.int32)
        large = np.minimum(large, NBUCKET - 1)
        maps.append(np.where(dist < exact, dist, large).astype(np.int32))
    return np.stack(maps)


def _bias_table(rel_table, buckets):
    def body(t_ref, b_ref, o_ref):
        bk = b_ref[0]
        for h in range(NH):
            acc = jnp.zeros((BLK, 2 * BLK), F32)
            for k in range(NBUCKET):
                acc = jnp.where(bk == k, t_ref[k, h], acc)
            o_ref[0, h] = acc

    return pl.pallas_call(
        body, name="bias_table", grid=(len(BRANCHES),),
        in_specs=[pl.BlockSpec(memory_space=pltpu.SMEM),
                  pl.BlockSpec((1, BLK, 2 * BLK), lambda i: (i, 0, 0))],
        out_specs=pl.BlockSpec((1, NH, BLK, 2 * BLK), lambda i: (i, 0, 0, 0)),
        out_shape=SDS((len(BRANCHES), NH, BLK, 2 * BLK), F32),
    )(rel_table, buckets)


def _rel_table_grad(dbias, buckets):
    def body(d_ref, b_ref, o_ref):
        h = pl.program_id(0)
        for k in range(NBUCKET):
            tot = jnp.zeros((1, 1), F32)
            for br in range(len(BRANCHES)):
                sel = jnp.where(b_ref[br] == k, d_ref[br, 0], 0.0)
                tot = tot + jnp.sum(jnp.sum(sel, axis=1, keepdims=True), axis=0, keepdims=True)
            o_ref[0, :, pl.ds(k, 1)] = tot

    out = pl.pallas_call(
        body, name="rel_table_grad", grid=(NH,),
        in_specs=[pl.BlockSpec((len(BRANCHES), 1, BLK, 2 * BLK), lambda h: (0, h, 0, 0)),
                  pl.BlockSpec((len(BRANCHES), BLK, 2 * BLK), lambda h: (0, 0, 0))],
        out_specs=pl.BlockSpec((1, 1, NBUCKET), lambda h: (h, 0, 0)),
        out_shape=SDS((NH, 1, NBUCKET), F32),
    )(dbias, buckets)
    return out.reshape(NH, NBUCKET).T


def _block_rows(br, i):
    _, dil = BRANCHES[br]
    nb = S // dil // BLK
    if nb == 16:
        r, nidx = 0, i
    elif nb == 4:
        r, nidx = lax.shift_right_logical(i, 2), lax.bitwise_and(i, 3)
    else:
        r, nidx = i, 0
    start = r + dil * BLK * nidx
    if nb == 1:
        return start, None, None
    prev = r + dil * BLK * jnp.maximum(nidx - 1, 0)
    return start, prev, nidx > 0


def _rows(start, dil):
    if dil == 1:
        return pl.ds(pl.multiple_of(start, BLK), BLK)
    return pl.ds(start, BLK, stride=dil)


def _attn_masks():
    lane = lax.broadcasted_iota(jnp.int32, (BLK, BLK), 1)
    qi = lax.broadcasted_iota(jnp.int32, (BLK, BLK), 0)
    head0 = lane < HD
    valid_cur = lane <= qi
    valid_prev = lane >= qi
    return head0, valid_cur, valid_prev


def _attn_fwd_v1(qkv, bias):
    scale = 1.0 / math.sqrt(HD)
    nbr = len(BRANCHES)

    def body(q_ref, k_ref, v_ref, bias_ref, o_ref, lse_ref, qf, kf, vf, ob, mb, lb):
        qf[...] = q_ref[...].astype(F32)
        kf[...] = k_ref[...].astype(F32)
        vf[...] = v_ref[...].astype(F32)
        head0, valid_cur, valid_prev = _attn_masks()

        for br in range(nbr):
            dil = BRANCHES[br][1]

            def blk(i, carry, br=br, dil=dil):
                start, prev, has_prev = _block_rows(br, i)
                rows = _rows(start, dil)
                q = qf[rows, :]
                kc = kf[rows, :].astype(BF16)
                vc = vf[rows, :].astype(BF16)
                if prev is not None:
                    prows = _rows(prev, dil)
                    kp = kf[prows, :].astype(BF16)
                    vp = vf[prows, :].astype(BF16)
                    ok_prev = jnp.logical_and(valid_prev, has_prev)
                o_acc = jnp.zeros((BLK, BLK), F32)
                m_acc = jnp.zeros((BLK, BLK), F32)
                l_acc = jnp.zeros((BLK, BLK), F32)
                for j in range(2):
                    mj = head0 if j == 0 else jnp.logical_not(head0)
                    qj = jnp.where(mj, q, 0.0).astype(BF16)
                    sc = _dot_nt(qj, kc) * scale + bias_ref[br, j, :, BLK:]
                    sc = jnp.where(valid_cur, sc, NEG_INF)
                    mx = jnp.max(sc, axis=-1, keepdims=True)
                    if prev is not None:
                        sp = _dot_nt(qj, kp) * scale + bias_ref[br, j, :, :BLK]
                        sp = jnp.where(ok_prev, sp, NEG_INF)
                        mx = jnp.maximum(mx, jnp.max(sp, axis=-1, keepdims=True))
                    pc = jnp.exp(sc - mx)
                    ls = jnp.sum(pc, axis=-1, keepdims=True)
                    o = _dot(pc.astype(BF16), vc)
                    if prev is not None:
                        pp = jnp.exp(sp - mx)
                        ls = ls + jnp.sum(pp, axis=-1, keepdims=True)
                        o = o + _dot(pp.astype(BF16), vp)
                    o_acc = jnp.where(mj, o, o_acc)
                    m_acc = jnp.where(mj, mx, m_acc)
                    l_acc = jnp.where(mj, ls, l_acc)
                ob[br, rows, :] = o_acc
                mb[br, rows, :] = m_acc
                lb[br, rows, :] = l_acc
                return carry

            lax.fori_loop(0, 16, blk, 0)

        def merge(i, carry):
            rows = pl.ds(pl.multiple_of(i * 256, 256), 256)
            m_all = jnp.maximum(jnp.maximum(mb[0, rows, :], mb[1, rows, :]), mb[2, rows, :])
            num = jnp.zeros((256, BLK), F32)
            den = jnp.zeros((256, BLK), F32)
            for br in range(nbr):
                c = jnp.exp(mb[br, rows, :] - m_all)
                num = num + ob[br, rows, :] * c
                den = den + lb[br, rows, :] * c
            o_ref[rows, :] = num / den
            lse_ref[rows, :] = m_all + jnp.log(den)
            return carry

        lax.fori_loop(0, S // 256, merge, 0)

    npair = NH // 2
    blk_spec = lambda off: pl.BlockSpec((S, BLK), lambda b, hp: (b, off + hp))
    return pl.pallas_call(
        body, name="attn_fwd", grid=(BL, npair),
        in_specs=[blk_spec(0), blk_spec(npair), blk_spec(2 * npair),
                  pl.BlockSpec((nbr, 2, BLK, 2 * BLK), lambda b, hp: (0, hp, 0, 0))],
        out_specs=(blk_spec(0), blk_spec(0)),
        out_shape=(SDS((T, AW), F32), SDS((T, AW), F32)),
        scratch_shapes=[pltpu.VMEM((S, BLK), F32)] * 3 + [pltpu.VMEM((nbr, S, BLK), F32)] * 3,
    )(qkv, qkv, qkv, bias)


def _attn_bwd_v1(qkv, attn, lse, dattn, bias):
    scale = 1.0 / math.sqrt(HD)
    nbr = len(BRANCHES)

    def body(q_ref, k_ref, v_ref, o_ref, lse_ref, do_ref, bias_ref,
             dq_ref, dk_ref, dv_ref, sq_ref, sk_ref, sv_ref, db_ref,
             qf, kf, vf, dl, dqa, dka, dva):
        b = pl.program_id(1)
        qf[...] = q_ref[...].astype(F32)
        kf[...] = k_ref[...].astype(F32)
        vf[...] = v_ref[...].astype(F32)
        dqa[...] = jnp.zeros_like(dqa)
        dka[...] = jnp.zeros_like(dka)
        dva[...] = jnp.zeros_like(dva)
        head0, valid_cur, valid_prev = _attn_masks()

        @pl.when(b == 0)
        def _():
            db_ref[...] = jnp.zeros_like(db_ref)
            sq_ref[...] = jnp.zeros_like(sq_ref)
            sk_ref[...] = jnp.zeros_like(sk_ref)
            sv_ref[...] = jnp.zeros_like(sv_ref)

        def delta(i, carry):
            rows = pl.ds(pl.multiple_of(i * 256, 256), 256)
            prod = do_ref[rows, :] * o_ref[rows, :]
            h0 = lax.broadcasted_iota(jnp.int32, (256, BLK), 1) < HD
            d0 = jnp.sum(jnp.where(h0, prod, 0.0), axis=-1, keepdims=True)
            d1 = jnp.sum(jnp.where(h0, 0.0, prod), axis=-1, keepdims=True)
            dl[rows, :] = jnp.where(h0, d0, d1)
            return carry

        lax.fori_loop(0, S // 256, delta, 0)

        for br in range(nbr):
            dil = BRANCHES[br][1]

            def blk(i, carry, br=br, dil=dil):
                start, prev, has_prev = _block_rows(br, i)
                rows = _rows(start, dil)
                q = qf[rows, :]
                kc = kf[rows, :].astype(BF16)
                vc = vf[rows, :].astype(BF16)
                do = do_ref[rows, :]
                lse_b = lse_ref[rows, :]
                dl_b = dl[rows, :]
                if prev is not None:
                    prows = _rows(prev, dil)
                    kp = kf[prows, :].astype(BF16)
                    vp = vf[prows, :].astype(BF16)
                    ok_prev = jnp.logical_and(valid_prev, has_prev)
                    dk_p = jnp.zeros((BLK, BLK), F32)
                    dv_p = jnp.zeros((BLK, BLK), F32)
                dq = jnp.zeros((BLK, BLK), F32)
                dk_c = jnp.zeros((BLK, BLK), F32)
                dv_c = jnp.zeros((BLK, BLK), F32)
                for j in range(2):
                    mj = head0 if j == 0 else jnp.logical_not(head0)
                    qj = jnp.where(mj, q, 0.0).astype(BF16)
                    doj = jnp.where(mj, do, 0.0).astype(BF16)
                    lse_j = lse_b[:, j * HD:j * HD + 1]
                    dl_j = dl_b[:, j * HD:j * HD + 1]
                    sc = _dot_nt(qj, kc) * scale + bias_ref[br, j, :, BLK:]
                    pc = jnp.where(valid_cur, jnp.exp(sc - lse_j), 0.0)
                    ds_c = pc * (_dot_nt(doj, vc) - dl_j)
                    db_ref[br, j, :, BLK:] += ds_c
                    dsb = (ds_c * scale).astype(BF16)
                    dqj = _dot(dsb, kc)
                    dk_c = dk_c + _dot_tn(dsb, qj)
                    dv_c = dv_c + _dot_tn(pc.astype(BF16), doj)
                    if prev is not None:
                        sp = _dot_nt(qj, kp) * scale + bias_ref[br, j, :, :BLK]
                        pp = jnp.where(ok_prev, jnp.exp(sp - lse_j), 0.0)
                        ds_p = pp * (_dot_nt(doj, vp) - dl_j)
                        db_ref[br, j, :, :BLK] += ds_p
                        dsbp = (ds_p * scale).astype(BF16)
                        dqj = dqj + _dot(dsbp, kp)
                        dk_p = dk_p + _dot_tn(dsbp, qj)
                        dv_p = dv_p + _dot_tn(pp.astype(BF16), doj)
                    dq = jnp.where(mj, dqj, dq)
                dqa[rows, :] = dqa[rows, :] + dq
                dka[rows, :] = dka[rows, :] + dk_c
                dva[rows, :] = dva[rows, :] + dv_c
                if prev is not None:
                    dka[prows, :] = dka[prows, :] + dk_p
                    dva[prows, :] = dva[prows, :] + dv_p
                return carry

            lax.fori_loop(0, 16, blk, 0)

        def flush(i, carry):
            rows = pl.ds(pl.multiple_of(i * 256, 256), 256)
            for acc, out, cs in ((dqa, dq_ref, sq_ref), (dka, dk_ref, sk_ref), (dva, dv_ref, sv_ref)):
                val = acc[rows, :]
                out[rows, :] = val.astype(BF16)
                cs[...] += _colsum(val)
            return carry

        lax.fori_loop(0, S // 256, flush, 0)

    npair = NH // 2
    blk_spec = lambda off: pl.BlockSpec((S, BLK), lambda hp, b: (b, off + hp))
    sum_spec = pl.BlockSpec((1, BLK), lambda hp, b: (0, hp))
    return pl.pallas_call(
        body, name="attn_bwd", grid=(npair, BL),
        in_specs=[blk_spec(0), blk_spec(npair), blk_spec(2 * npair), blk_spec(0), blk_spec(0), blk_spec(0),
                  pl.BlockSpec((nbr, 2, BLK, 2 * BLK), lambda hp, b: (0, hp, 0, 0))],
        out_specs=(blk_spec(0), blk_spec(0), blk_spec(0), sum_spec, sum_spec, sum_spec,
                   pl.BlockSpec((nbr, 2, BLK, 2 * BLK), lambda hp, b: (0, hp, 0, 0))),
        out_shape=(SDS((T, AW), BF16), SDS((T, AW), BF16), SDS((T, AW), BF16),
                   SDS((1, AW), F32), SDS((1, AW), F32), SDS((1, AW), F32),
                   SDS((nbr, NH, BLK, 2 * BLK), F32)),
        scratch_shapes=[pltpu.VMEM((S, BLK), F32)] * 7,
    )(qkv, qkv, qkv, attn, lse, dattn, bias)


PADK = BLK
SCALE = 1.0 / math.sqrt(HD)
ATTN_UNROLL = 8


def _branch_geometry(br):
    dil = BRANCHES[br][1]
    sub = S // dil
    return dil, sub, sub // BLK


def _token_rows(br, i):
    dil, _, nblk = _branch_geometry(br)
    if dil == 1:
        return pl.ds(pl.multiple_of(i * BLK, BLK), BLK), i
    r = lax.shift_right_logical(i, nblk.bit_length() - 1)
    n = lax.bitwise_and(i, nblk - 1)
    return pl.ds(r + dil * BLK * n, BLK, stride=dil), n


def _sub_layout_loop(br, step):
    dil, sub, _ = _branch_geometry(br)
    rows = min(sub, 256)
    nchunk = sub // rows

    def it_step(it, carry):
        if dil == 1:
            src = pl.ds(pl.multiple_of(it * rows, rows), rows)
        else:
            r = lax.shift_right_logical(it, nchunk.bit_length() - 1)
            src = pl.ds(r + dil * rows * lax.bitwise_and(it, nchunk - 1), rows, stride=dil)
        step(src, pl.multiple_of(it * rows, BLK), rows)
        return carry

    lax.fori_loop(0, dil * nchunk, it_step, 0)


def _masked_bias(bias_ref, bm):
    qi = lax.broadcasted_iota(jnp.int32, (BLK, 2 * BLK), 0)
    kj = lax.broadcasted_iota(jnp.int32, (BLK, 2 * BLK), 1)
    first = jnp.logical_and(kj >= BLK, kj - BLK <= qi)
    valid = jnp.logical_or(first, jnp.logical_and(kj < BLK, kj >= qi))
    for br in range(len(BRANCHES)):
        for j in range(2):
            b = bias_ref[br, j]
            bm[br, 1, pl.ds(j * BLK, BLK), :] = jnp.where(valid, b, NEG_INF)
            bm[br, 0, pl.ds(j * BLK, BLK), :] = jnp.where(first, b, NEG_INF)


def _head_split(fn):
    def split(t):
        h0 = lax.broadcasted_iota(jnp.int32, t.shape, 1) < HD
        t = fn(t)
        return jnp.where(h0, t, 0.0).astype(BF16), jnp.where(h0, 0.0, t).astype(BF16)
    return split


def _attn_fwd(qkv, bias):
    nbr = len(BRANCHES)

    def body(q_ref, k_ref, v_ref, bias_ref, o_ref, lse_ref, qf, kf, vf, qs0, qs1, ks, vs, bm, ob, mb, lb):
        qf[...] = q_ref[...].astype(F32)
        kf[...] = k_ref[...].astype(F32)
        vf[...] = v_ref[...].astype(F32)
        _masked_bias(bias_ref, bm)
        ks[pl.ds(0, PADK), :] = jnp.zeros((PADK, BLK), BF16)
        vs[pl.ds(0, PADK), :] = jnp.zeros((PADK, BLK), BF16)
        head0 = lax.broadcasted_iota(jnp.int32, (BLK, BLK), 1) < HD
        split_q = _head_split(lambda t: t * SCALE)

        for br in range(nbr):
            nblk = _branch_geometry(br)[2]

            def stage(src, off, rows):
                qs0[pl.ds(off, rows), :], qs1[pl.ds(off, rows), :] = split_q(qf[src, :])
                ks[pl.ds(PADK + off, rows), :] = kf[src, :].astype(BF16)
                vs[pl.ds(PADK + off, rows), :] = vf[src, :].astype(BF16)

            _sub_layout_loop(br, stage)

            def blk(i, carry, br=br, nblk=nblk):
                base = pl.multiple_of(i * BLK, BLK)
                rows, n = _token_rows(br, i)
                q01 = jnp.concatenate([qs0[pl.ds(base, BLK), :], qs1[pl.ds(base, BLK), :]], axis=0)
                if nblk > 1:
                    kcat = ks[pl.ds(base, 2 * BLK), :]
                    vcat = vs[pl.ds(base, 2 * BLK), :]
                    s = _dot_nt(q01, kcat) + bm[br, jnp.minimum(n, 1)]
                else:
                    kcat = ks[pl.ds(PADK + base, BLK), :]
                    vcat = vs[pl.ds(PADK + base, BLK), :]
                    s = _dot_nt(q01, kcat) + bm[br, 0, :, BLK:]
                mx = jnp.max(s, axis=-1, keepdims=True)
                p = jnp.exp(s - mx)
                ls = jnp.sum(p, axis=-1, keepdims=True)
                o = _dot(p.astype(BF16), vcat)
                ob[br, rows, :] = jnp.where(head0, o[:BLK], o[BLK:])
                mb[br, rows, :] = jnp.where(head0, mx[:BLK], mx[BLK:])
                lb[br, rows, :] = jnp.where(head0, ls[:BLK], ls[BLK:])
                return carry

            lax.fori_loop(0, 16, blk, 0, unroll=ATTN_UNROLL)

        def merge(i, carry):
            rows = pl.ds(pl.multiple_of(i * 256, 256), 256)
            m_all = jnp.maximum(jnp.maximum(mb[0, rows, :], mb[1, rows, :]), mb[2, rows, :])
            num = jnp.zeros((256, BLK), F32)
            den = jnp.zeros((256, BLK), F32)
            for br in range(nbr):
                c = jnp.exp(mb[br, rows, :] - m_all)
                num = num + ob[br, rows, :] * c
                den = den + lb[br, rows, :] * c
            o_ref[rows, :] = num / den
            lse_ref[rows, :] = m_all + jnp.log(den)
            return carry

        lax.fori_loop(0, S // 256, merge, 0)

    npair = NH // 2
    blk_spec = lambda off: pl.BlockSpec((S, BLK), lambda b, hp: (b, off + hp))
    return pl.pallas_call(
        body, name="attn_fwd", grid=(BL, npair),
        in_specs=[blk_spec(0), blk_spec(npair), blk_spec(2 * npair),
                  pl.BlockSpec((nbr, 2, BLK, 2 * BLK), lambda b, hp: (0, hp, 0, 0))],
        out_specs=(blk_spec(0), blk_spec(0)),
        out_shape=(SDS((T, AW), F32), SDS((T, AW), F32)),
        scratch_shapes=[pltpu.VMEM((S, BLK), F32)] * 3 + [pltpu.VMEM((S, BLK), BF16)] * 2
        + [pltpu.VMEM((PADK + S, BLK), BF16)] * 2 + [pltpu.VMEM((nbr, 2, 2 * BLK, 2 * BLK), F32)]
        + [pltpu.VMEM((nbr, S, BLK), F32)] * 3,
    )(qkv, qkv, qkv, bias)


def _attn_bwd(qkv, attn, lse, dattn, bias):
    nbr = len(BRANCHES)

    def body(q_ref, k_ref, v_ref, o_ref, lse_ref, do_ref, bias_ref,
             dq_ref, dk_ref, dv_ref, sq_ref, sk_ref, sv_ref, db_ref,
             qf, kf, vf, dl, dqa, dka, dva, qs0, qs1, ds0, ds1, ks, vs, dks, dvs, bm):
        b = pl.program_id(1)
        qf[...] = q_ref[...].astype(F32)
        kf[...] = k_ref[...].astype(F32)
        vf[...] = v_ref[...].astype(F32)
        dqa[...] = jnp.zeros_like(dqa)
        dka[...] = jnp.zeros_like(dka)
        dva[...] = jnp.zeros_like(dva)
        _masked_bias(bias_ref, bm)
        ks[pl.ds(0, PADK), :] = jnp.zeros((PADK, BLK), BF16)
        vs[pl.ds(0, PADK), :] = jnp.zeros((PADK, BLK), BF16)
        head0 = lax.broadcasted_iota(jnp.int32, (BLK, BLK), 1) < HD
        split_q = _head_split(lambda t: t * SCALE)
        split_do = _head_split(lambda t: t)

        @pl.when(b == 0)
        def _():
            db_ref[...] = jnp.zeros_like(db_ref)
            sq_ref[...] = jnp.zeros_like(sq_ref)
            sk_ref[...] = jnp.zeros_like(sk_ref)
            sv_ref[...] = jnp.zeros_like(sv_ref)

        def delta(i, carry):
            rows = pl.ds(pl.multiple_of(i * 256, 256), 256)
            prod = do_ref[rows, :] * o_ref[rows, :]
            h0 = lax.broadcasted_iota(jnp.int32, (256, BLK), 1) < HD
            d0 = jnp.sum(jnp.where(h0, prod, 0.0), axis=-1, keepdims=True)
            d1 = jnp.sum(jnp.where(h0, 0.0, prod), axis=-1, keepdims=True)
            dl[rows, :] = jnp.where(h0, d0, d1)
            return carry

        lax.fori_loop(0, S // 256, delta, 0)

        for br in range(nbr):
            nblk = _branch_geometry(br)[2]

            def stage(src, off, rows):
                qs0[pl.ds(off, rows), :], qs1[pl.ds(off, rows), :] = split_q(qf[src, :])
                ds0[pl.ds(off, rows), :], ds1[pl.ds(off, rows), :] = split_do(do_ref[src, :])
                ks[pl.ds(PADK + off, rows), :] = kf[src, :].astype(BF16)
                vs[pl.ds(PADK + off, rows), :] = vf[src, :].astype(BF16)

            _sub_layout_loop(br, stage)
            dks[...] = jnp.zeros_like(dks)
            dvs[...] = jnp.zeros_like(dvs)

            def blk(i, carry, br=br, nblk=nblk):
                base = pl.multiple_of(i * BLK, BLK)
                rows, n = _token_rows(br, i)
                q01 = jnp.concatenate([qs0[pl.ds(base, BLK), :], qs1[pl.ds(base, BLK), :]], axis=0)
                do01 = jnp.concatenate([ds0[pl.ds(base, BLK), :], ds1[pl.ds(base, BLK), :]], axis=0)
                lse_b = lse_ref[rows, :]
                dl_b = dl[rows, :]
                lse01 = jnp.concatenate([lse_b[:, 0:1], lse_b[:, HD:HD + 1]], axis=0)
                dl01 = jnp.concatenate([dl_b[:, 0:1], dl_b[:, HD:HD + 1]], axis=0)
                if nblk > 1:
                    krows = pl.ds(base, 2 * BLK)
                    bias_m = bm[br, jnp.minimum(n, 1)]
                else:
                    krows = pl.ds(PADK + base, BLK)
                    bias_m = bm[br, 0, :, BLK:]
                kcat = ks[krows, :]
                vcat = vs[krows, :]
                p = jnp.exp(_dot_nt(q01, kcat) + bias_m - lse01)
                dsv = p * (_dot_nt(do01, vcat) - dl01)
                if nblk > 1:
                    db_ref[br, 0] += dsv[:BLK]
                    db_ref[br, 1] += dsv[BLK:]
                else:
                    db_ref[br, 0, :, BLK:] += dsv[:BLK]
                    db_ref[br, 1, :, BLK:] += dsv[BLK:]
                dsb = dsv.astype(BF16)
                dq01 = _dot(dsb, kcat)
                dqa[rows, :] = dqa[rows, :] + jnp.where(head0, dq01[:BLK], dq01[BLK:])
                dks[krows, :] = dks[krows, :] + _dot_tn(dsb, q01)
                dvs[krows, :] = dvs[krows, :] + _dot_tn(p.astype(BF16), do01)
                return carry

            lax.fori_loop(0, 16, blk, 0, unroll=ATTN_UNROLL)

            def fold(src, off, rows):
                dka[src, :] = dka[src, :] + dks[pl.ds(PADK + off, rows), :]
                dva[src, :] = dva[src, :] + dvs[pl.ds(PADK + off, rows), :]

            _sub_layout_loop(br, fold)

        def flush(i, carry):
            rows = pl.ds(pl.multiple_of(i * 256, 256), 256)
            for acc, out, cs, mul in ((dqa, dq_ref, sq_ref, SCALE), (dka, dk_ref, sk_ref, 1.0), (dva, dv_ref, sv_ref, 1.0)):
                val = acc[rows, :] * mul
                out[rows, :] = val.astype(BF16)
                cs[...] += _colsum(val)
            return carry

        lax.fori_loop(0, S // 256, flush, 0)

    npair = NH // 2
    blk_spec = lambda off: pl.BlockSpec((S, BLK), lambda hp, b: (b, off + hp))
    sum_spec = pl.BlockSpec((1, BLK), lambda hp, b: (0, hp))
    return pl.pallas_call(
        body, name="attn_bwd", grid=(npair, BL),
        in_specs=[blk_spec(0), blk_spec(npair), blk_spec(2 * npair), blk_spec(0), blk_spec(0), blk_spec(0),
                  pl.BlockSpec((nbr, 2, BLK, 2 * BLK), lambda hp, b: (0, hp, 0, 0))],
        out_specs=(blk_spec(0), blk_spec(0), blk_spec(0), sum_spec, sum_spec, sum_spec,
                   pl.BlockSpec((nbr, 2, BLK, 2 * BLK), lambda hp, b: (0, hp, 0, 0))),
        out_shape=(SDS((T, AW), BF16), SDS((T, AW), BF16), SDS((T, AW), BF16),
                   SDS((1, AW), F32), SDS((1, AW), F32), SDS((1, AW), F32),
                   SDS((nbr, NH, BLK, 2 * BLK), F32)),
        scratch_shapes=[pltpu.VMEM((S, BLK), F32)] * 7 + [pltpu.VMEM((S, BLK), BF16)] * 4
        + [pltpu.VMEM((PADK + S, BLK), BF16)] * 2 + [pltpu.VMEM((PADK + S, BLK), F32)] * 2
        + [pltpu.VMEM((nbr, 2, 2 * BLK, 2 * BLK), F32)],
    )(qkv, qkv, qkv, attn, lse, dattn, bias)


CH = 256
PADR = 32


def _conv_fwd(ag, conv_w, conv_b):
    def body(ag_ref, w_ref, b_ref, u1_ref, u0p):
        u0p[pl.ds(0, PADR), :] = jnp.zeros((PADR, CW), F32)

        def glu(i, carry):
            t0 = pl.multiple_of(i * CH, CH)
            a = ag_ref[pl.ds(t0, CH), :CW]
            g = ag_ref[pl.ds(t0, CH), CW:]
            u0p[pl.ds(PADR + t0, CH), :] = a * _sigmoid(g)
            return carry

        lax.fori_loop(0, S // CH, glu, 0)

        def conv(i, carry):
            t0 = pl.multiple_of(i * CH, CH)
            win = u0p[pl.ds(t0, CH + PADR), :]
            acc = jnp.zeros((CH, CW), F32) + b_ref[...]
            for k in range(CK):
                off = PADR - (CK - 1) + k
                acc = acc + win[off:off + CH, :] * w_ref[k:k + 1, :]
            u1_ref[pl.ds(t0, CH), :] = acc
            return carry

        lax.fori_loop(0, S // CH, conv, 0)

    return pl.pallas_call(
        body, name="conv_fwd", grid=(BL,),
        in_specs=[pl.BlockSpec((S, 2 * CW), lambda b: (b, 0)),
                  pl.BlockSpec((CK, CW), lambda b: (0, 0)),
                  pl.BlockSpec((1, CW), lambda b: (0, 0))],
        out_specs=pl.BlockSpec((S, CW), lambda b: (b, 0)),
        out_shape=SDS((T, CW), F32),
        scratch_shapes=[pltpu.VMEM((S + PADR, CW), F32)],
    )(ag, conv_w, conv_b)


def _conv_post(u1, cg, cb):
    mu = _rowmean(u1)
    uc = u1 - mu
    rstd = lax.rsqrt(_rowmean(uc * uc) + LN_EPS)
    xh = uc * rstd
    u2 = xh * cg + cb
    sg = _sigmoid(u2)
    return xh, rstd, u2, sg, u2 * sg


def _mix_fwd(attn, u1, ga, gc, cg, cb):
    def body(a_ref, u_ref, ga_ref, gc_ref, cg_ref, cb_ref, o_ref):
        a = a_ref[...]
        ra = lax.rsqrt(_rowmean(a * a) + LN_EPS)
        o_ref[:, :AW] = (a * ra * ga_ref[...]).astype(BF16)
        _, _, _, _, u3 = _conv_post(u_ref[...], cg_ref[...], cb_ref[...])
        rc = lax.rsqrt(_rowmean(u3 * u3) + LN_EPS)
        o_ref[:, AW:] = (u3 * rc * gc_ref[...]).astype(BF16)

    vec = lambda w: pl.BlockSpec((1, w), lambda m: (0, 0))
    return pl.pallas_call(
        body, name="mix_fwd", grid=(T // TM,),
        in_specs=[pl.BlockSpec((TM, AW), lambda m: (m, 0)), pl.BlockSpec((TM, CW), lambda m: (m, 0)),
                  vec(AW), vec(CW), vec(CW), vec(CW)],
        out_specs=pl.BlockSpec((TM, D), lambda m: (m, 0)),
        out_shape=SDS((T, D), BF16),
    )(attn, u1, ga, gc, cg, cb)


def _mix_bwd(dmixed, attn, u1, ga, gc, cg, cb):
    def body(dm_ref, a_ref, u_ref, ga_ref, gc_ref, cg_ref, cb_ref,
             da_ref, du_ref, g_an, g_cn, g_lg, g_lb, g_cb):
        @pl.when(pl.program_id(0) == 0)
        def _():
            for r in (g_an, g_cn, g_lg, g_lb, g_cb):
                r[...] = jnp.zeros_like(r)

        a = a_ref[...]
        dna = dm_ref[:, :AW]
        ra = lax.rsqrt(_rowmean(a * a) + LN_EPS)
        g_an[...] += _colsum(dna * a * ra)
        dat = dna * ga_ref[...]
        da_ref[...] = ra * dat - a * (ra * ra * ra) * _rowmean(dat * a)

        xh, rstd, u2, sg, u3 = _conv_post(u_ref[...], cg_ref[...], cb_ref[...])
        dnc = dm_ref[:, AW:]
        rc = lax.rsqrt(_rowmean(u3 * u3) + LN_EPS)
        g_cn[...] += _colsum(dnc * u3 * rc)
        dut = dnc * gc_ref[...]
        du3 = rc * dut - u3 * (rc * rc * rc) * _rowmean(dut * u3)
        du2 = du3 * sg * (1.0 + u2 * (1.0 - sg))
        g_lg[...] += _colsum(du2 * xh)
        g_lb[...] += _colsum(du2)
        dxh = du2 * cg_ref[...]
        du1 = rstd * (dxh - _rowmean(dxh) - xh * _rowmean(dxh * xh))
        g_cb[...] += _colsum(du1)
        du_ref[...] = du1

    vec = lambda w: pl.BlockSpec((1, w), lambda m: (0, 0))
    return pl.pallas_call(
        body, name="mix_bwd", grid=(T // TM,),
        in_specs=[pl.BlockSpec((TM, D), lambda m: (m, 0)), pl.BlockSpec((TM, AW), lambda m: (m, 0)),
                  pl.BlockSpec((TM, CW), lambda m: (m, 0)), vec(AW), vec(CW), vec(CW), vec(CW)],
        out_specs=(pl.BlockSpec((TM, AW), lambda m: (m, 0)), pl.BlockSpec((TM, CW), lambda m: (m, 0)),
                   vec(AW), vec(CW), vec(CW), vec(CW), vec(CW)),
        out_shape=(SDS((T, AW), F32), SDS((T, CW), F32),
                   SDS((1, AW), F32), SDS((1, CW), F32), SDS((1, CW), F32), SDS((1, CW), F32), SDS((1, CW), F32)),
    )(dmixed, attn, u1, ga, gc, cg, cb)


def _conv_bwd(du1, ag, conv_w):
    def body(du_ref, ag_ref, w_ref, dag_ref, cs_ref, gw_ref, u0p, dup):
        @pl.when(pl.program_id(0) == 0)
        def _():
            cs_ref[...] = jnp.zeros_like(cs_ref)
            gw_ref[...] = jnp.zeros_like(gw_ref)

        u0p[pl.ds(0, PADR), :] = jnp.zeros((PADR, CW), F32)
        dup[pl.ds(S, PADR), :] = jnp.zeros((PADR, CW), F32)

        def fill(i, carry):
            t0 = pl.multiple_of(i * CH, CH)
            a = ag_ref[pl.ds(t0, CH), :CW]
            g = ag_ref[pl.ds(t0, CH), CW:]
            u0p[pl.ds(PADR + t0, CH), :] = a * _sigmoid(g)
            dup[pl.ds(t0, CH), :] = du_ref[pl.ds(t0, CH), :]
            return carry

        lax.fori_loop(0, S // CH, fill, 0)

        def chunk(i, carry):
            t0 = pl.multiple_of(i * CH, CH)
            d = dup[pl.ds(t0, CH), :]
            win_u = u0p[pl.ds(t0, CH + PADR), :]
            win_d = dup[pl.ds(t0, CH + PADR), :]
            du0 = jnp.zeros((CH, CW), F32)
            for k in range(CK):
                off = PADR - (CK - 1) + k
                gw_ref[k:k + 1, :] += _colsum(d * win_u[off:off + CH, :])
                fo = CK - 1 - k
                du0 = du0 + win_d[fo:fo + CH, :] * w_ref[k:k + 1, :]
            a = ag_ref[pl.ds(t0, CH), :CW]
            sg = _sigmoid(ag_ref[pl.ds(t0, CH), CW:])
            da = du0 * sg
            dg = du0 * a * sg * (1.0 - sg)
            dag_ref[pl.ds(t0, CH), :CW] = da.astype(BF16)
            dag_ref[pl.ds(t0, CH), CW:] = dg.astype(BF16)
            cs_ref[:, :CW] += _colsum(da)
            cs_ref[:, CW:] += _colsum(dg)
            return carry

        lax.fori_loop(0, S // CH, chunk, 0)

    return pl.pallas_call(
        body, name="conv_bwd", grid=(BL,),
        in_specs=[pl.BlockSpec((S, CW), lambda b: (b, 0)), pl.BlockSpec((S, 2 * CW), lambda b: (b, 0)),
                  pl.BlockSpec((CK, CW), lambda b: (0, 0))],
        out_specs=(pl.BlockSpec((S, 2 * CW), lambda b: (b, 0)),
                   pl.BlockSpec((1, 2 * CW), lambda b: (0, 0)),
                   pl.BlockSpec((PADR, CW), lambda b: (0, 0))),
        out_shape=(SDS((T, 2 * CW), BF16), SDS((1, 2 * CW), F32), SDS((PADR, CW), F32)),
        scratch_shapes=[pltpu.VMEM((S + PADR, CW), F32), pltpu.VMEM((S + PADR, CW), F32)],
    )(du1, ag, conv_w)


def _layer_norm_fwd(z):
    mu = _rowmean(z)
    zc = z - mu
    rstd = lax.rsqrt(_rowmean(zc * zc) + LN_EPS)
    return zc * rstd, rstd


def _layer_norm_bwd(dy, xh, rstd, g):
    dxh = dy * g
    return rstd * (dxh - _rowmean(dxh) - xh * _rowmean(dxh * xh))


def _out_proj_ln1(mixed, w_out, x2, g1, b1):
    def body(a_ref, w_ref, x_ref, g_ref, b_ref, xh_ref, rstd_ref, x1_ref):
        z = ALPHA * x_ref[...] + _dot(a_ref[...], w_ref[...])
        xh, rstd = _layer_norm_fwd(z)
        xh_ref[...] = xh
        rstd_ref[...] = rstd
        x1_ref[...] = (xh * g_ref[...] + b_ref[...]).astype(BF16)

    vec = pl.BlockSpec((1, D), lambda m: (0, 0))
    row = pl.BlockSpec((TM, D), lambda m: (m, 0))
    return pl.pallas_call(
        body, name="out_proj_ln1", grid=(T // TM,),
        in_specs=[row, pl.BlockSpec((D, D), lambda m: (0, 0)), row, vec, vec],
        out_specs=(row, pl.BlockSpec((TM, 1), lambda m: (m, 0)), row),
        out_shape=(SDS((T, D), F32), SDS((T, 1), F32), SDS((T, D), BF16)),
    )(mixed, w_out, x2, g1, b1)


def _seq_start(m):
    return lax.bitwise_and(m, S // TMF - 1) == 0


def _causal3(ext, w_ref, b_ref):
    x0 = ext[pl.ds(8, TM), :]
    x1 = ext[pl.ds(7, TM), :]
    x2 = ext[pl.ds(6, TM), :]
    y = w_ref[2:3, :] * x0 + w_ref[1:2, :] * x1 + w_ref[0:1, :] * x2 + b_ref[...]
    return y, x0, x1, x2


def _shift_down(x, before, k):
    rolled = pltpu.roll(x, k, 0)
    row = lax.broadcasted_iota(jnp.int32, before.shape, 0)
    head = jnp.where(row < k, pltpu.roll(before, k, 0), rolled[:8])
    return jnp.concatenate([head, rolled[8:]], axis=0)


def _shift_up(x, after, k):
    n = x.shape[0]
    rolled = pltpu.roll(x, n - k, 0)
    row = lax.broadcasted_iota(jnp.int32, after.shape, 0)
    tail = jnp.where(row >= 8 - k, pltpu.roll(after, 8 - k, 0), rolled[n - 8:])
    return jnp.concatenate([rolled[:n - 8], tail], axis=0)


def _ffn_up(x1b, w_up, fcw, fcb):
    def body(x_ref, wg_ref, wv_ref, cwg_ref, cwv_ref, cbg_ref, cbv_ref, up_ref, gv_ref, act_ref, prev_g, prev_v):
        @pl.when(_seq_start(pl.program_id(1)))
        def _():
            prev_g[...] = jnp.zeros_like(prev_g)
            prev_v[...] = jnp.zeros_like(prev_v)

        x = x_ref[...]
        outs = []
        for w_ref, cw_ref, cb_ref, prev, lo in ((wg_ref, cwg_ref, cbg_ref, prev_g, 0), (wv_ref, cwv_ref, cbv_ref, prev_v, FT)):
            ub = _dot_nt(x, w_ref[...]).astype(BF16)
            up_ref[:, lo:lo + FT] = ub
            u = ub.astype(F32)
            before = prev[...]
            y = (cw_ref[2:3, :] * u + cw_ref[1:2, :] * _shift_down(u, before, 1)
                 + cw_ref[0:1, :] * _shift_down(u, before, 2) + cb_ref[...])
            prev[...] = u[TMF - 8:]
            yb = y.astype(BF16)
            gv_ref[:, lo:lo + FT] = yb
            outs.append(yb.astype(F32))
        gate, val = outs
        act_ref[...] = (gate * _sigmoid(gate) * val).astype(BF16)

    wspec = lambda off: pl.BlockSpec((FT, D), lambda n, m: (n + off, 0))
    cwspec = lambda off: pl.BlockSpec((FK, FT), lambda n, m: (0, n + off))
    cbspec = lambda off: pl.BlockSpec((1, FT), lambda n, m: (0, n + off))
    pair = pl.BlockSpec((TMF, 2 * FT), lambda n, m: (m, n))
    return pl.pallas_call(
        body, name="ffn_up", grid=(NFT, T // TMF),
        in_specs=[pl.BlockSpec((TMF, D), lambda n, m: (m, 0)), wspec(0), wspec(NFT),
                  cwspec(0), cwspec(NFT), cbspec(0), cbspec(NFT)],
        out_specs=(pair, pair, pl.BlockSpec((TMF, FT), lambda n, m: (m, n))),
        out_shape=(SDS((T, 2 * DFF), BF16), SDS((T, 2 * DFF), BF16), SDS((T, DFF), BF16)),
        scratch_shapes=[pltpu.VMEM((8, FT), F32)] * 2,
    )(x1b, w_up, w_up, fcw, fcw, fcb, fcb)


def _ffn_down_loss(act, w_down, xh1, g1, b1, g2, b2, target):
    def body(a_ref, w_ref, xh1_ref, g1_ref, b1_ref, g2_ref, b2_ref, t_ref, dz_ref, loss_ref, gg_ref, gb_ref):
        @pl.when(pl.program_id(0) == 0)
        def _():
            loss_ref[...] = jnp.zeros_like(loss_ref)
            gg_ref[...] = jnp.zeros_like(gg_ref)
            gb_ref[...] = jnp.zeros_like(gb_ref)

        x1 = xh1_ref[...] * g1_ref[...] + b1_ref[...]
        z = ALPHA * x1 + _dot(a_ref[...], w_ref[...])
        xh, rstd = _layer_norm_fwd(z)
        diff = xh * g2_ref[...] + b2_ref[...] - t_ref[...]
        loss_ref[...] += 0.5 * _colsum(_rowmean(diff * diff))
        dout = diff * (1.0 / D)
        gg_ref[...] += _colsum(dout * xh)
        gb_ref[...] += _colsum(dout)
        dz_ref[...] = _layer_norm_bwd(dout, xh, rstd, g2_ref[...])

    vec = pl.BlockSpec((1, D), lambda m: (0, 0))
    row = pl.BlockSpec((TM, D), lambda m: (m, 0))
    return pl.pallas_call(
        body, name="ffn_down_loss", grid=(T // TM,),
        in_specs=[pl.BlockSpec((TM, DFF), lambda m: (m, 0)), pl.BlockSpec((DFF, D), lambda m: (0, 0)),
                  row, vec, vec, vec, vec, row],
        out_specs=(row, pl.BlockSpec((1, 1), lambda m: (0, 0)), vec, vec),
        out_shape=(SDS((T, D), F32), SDS((1, 1), F32), SDS((1, D), F32), SDS((1, D), F32)),
    )(act, w_down, xh1, g1, b1, g2, b2, target)


def _ffn_down_bwd(dz2, w_down, gv):
    def body(dz_ref, wd_ref, gv_ref, dup_ref, csg_ref, csv_ref):
        @pl.when(pl.program_id(1) == 0)
        def _():
            csg_ref[...] = jnp.zeros_like(csg_ref)
            csv_ref[...] = jnp.zeros_like(csv_ref)

        dact = _dot_nt(dz_ref[...].astype(BF16), wd_ref[...])
        gate = gv_ref[:, :FT].astype(F32)
        val = gv_ref[:, FT:].astype(F32)
        sg = _sigmoid(gate)
        gs = gate * sg
        dgate = dact * val * (sg + gs * (1.0 - sg))
        dval = dact * gs
        dup_ref[:, :FT] = dgate.astype(BF16)
        dup_ref[:, FT:] = dval.astype(BF16)
        csg_ref[...] += _colsum(dgate)
        csv_ref[...] += _colsum(dval)

    cs = pl.BlockSpec((1, FT), lambda n, m: (0, n))
    pair = pl.BlockSpec((TMF, 2 * FT), lambda n, m: (m, n))
    return pl.pallas_call(
        body, name="ffn_down_bwd", grid=(NFT, T // TMF),
        in_specs=[pl.BlockSpec((TMF, D), lambda n, m: (m, 0)), pl.BlockSpec((FT, D), lambda n, m: (n, 0)), pair],
        out_specs=(pair, cs, cs),
        out_shape=(SDS((T, 2 * DFF), BF16), SDS((1, DFF), F32), SDS((1, DFF), F32)),
    )(dz2, w_down, gv)


HALO = 16


def _conv3_transpose(dup, up, fcw_il):
    tiles = T // TMF

    def body(d_ref, h_ref, u_ref, w_ref, o_ref, gw_ref):
        m = pl.program_id(1)

        @pl.when(m == 0)
        def _():
            gw_ref[...] = jnp.zeros_like(gw_ref)

        d0 = d_ref[...].astype(F32)
        last = lax.bitwise_and(m + 1, S // TMF - 1) == 0
        after = jnp.where(last, 0.0, h_ref[...].astype(F32)[:8])
        d1 = _shift_up(d0, after, 1)
        d2 = _shift_up(d0, after, 2)
        o_ref[...] = (w_ref[2:3, :] * d0 + w_ref[1:2, :] * d1 + w_ref[0:1, :] * d2).astype(BF16)
        u = u_ref[...].astype(F32)
        for k, dk in enumerate((d2, d1, d0)):
            gw_ref[k:k + 1, :] += _colsum(dk * u)

    pair = pl.BlockSpec((TMF, 2 * FT), lambda n, m: (m, n))
    return pl.pallas_call(
        body, name="conv3_transpose", grid=(NFT, tiles),
        in_specs=[pair,
                  pl.BlockSpec((HALO, 2 * FT), lambda n, m: (jnp.minimum((m + 1) * (TMF // HALO), T // HALO - 1), n)),
                  pair, pl.BlockSpec((FK, 2 * FT), lambda n, m: (0, n))],
        out_specs=(pair, pl.BlockSpec((FK, 2 * FT), lambda n, m: (0, n))),
        out_shape=(SDS((T, 2 * DFF), BF16), SDS((FK, 2 * DFF), F32)),
    )(dup, dup, up, fcw_il)


def _ffn_up_bwd_ln1(dpre, w_up, dz2, xh1, rstd1, g1):
    def body(a_ref, w_ref, dz2_ref, xh_ref, rstd_ref, g_ref, dz1_ref, gg_ref, gb_ref):
        @pl.when(pl.program_id(0) == 0)
        def _():
            gg_ref[...] = jnp.zeros_like(gg_ref)
            gb_ref[...] = jnp.zeros_like(gb_ref)

        dx1 = ALPHA * dz2_ref[...]
        for n in range(NFT):
            for half in range(2):
                a = a_ref[:, (2 * n + half) * FT:(2 * n + half + 1) * FT]
                w = w_ref[pl.ds((half * NFT + n) * FT, FT), :]
                dx1 = dx1 + _dot(a, w)
        xh = xh_ref[...]
        gg_ref[...] += _colsum(dx1 * xh)
        gb_ref[...] += _colsum(dx1)
        dz1_ref[...] = _layer_norm_bwd(dx1, xh, rstd_ref[...], g_ref[...])

    vec = pl.BlockSpec((1, D), lambda m: (0, 0))
    row = pl.BlockSpec((TMF, D), lambda m: (m, 0))
    return pl.pallas_call(
        body, name="ffn_up_bwd_ln1", grid=(T // TMF,),
        in_specs=[pl.BlockSpec((TMF, 2 * DFF), lambda m: (m, 0)), pl.BlockSpec((2 * DFF, D), lambda m: (0, 0)),
                  row, row, pl.BlockSpec((TMF, 1), lambda m: (m, 0)), vec],
        out_specs=(row, vec, vec),
        out_shape=(SDS((T, D), F32), SDS((1, D), F32), SDS((1, D), F32)),
    )(dpre, w_up, dz2, xh1, rstd1, g1)


def _grad_w_up(dpre, x1b):
    tk = 1024

    def body(a_ref, b_ref, o_ref, acc):
        k = pl.program_id(1)

        @pl.when(k == 0)
        def _():
            acc[...] = jnp.zeros_like(acc)

        acc[...] += _dot_tn(a_ref[...], b_ref[...])

        @pl.when(k == T // tk - 1)
        def _():
            o_ref[0] = acc[pl.ds(0, FT), :].astype(o_ref.dtype)
            o_ref[1] = acc[pl.ds(FT, FT), :].astype(o_ref.dtype)

    out = pl.pallas_call(
        body, name="grad_w_up", grid=(NFT, T // tk),
        in_specs=[pl.BlockSpec((tk, 2 * FT), lambda n, k: (k, n)), pl.BlockSpec((tk, D), lambda n, k: (k, 0))],
        out_specs=pl.BlockSpec((2, FT, D), lambda n, k: (0, n, 0)),
        out_shape=SDS((2, DFF, D), GRAD_WIRE),
        scratch_shapes=[pltpu.VMEM((2 * FT, D), F32)],
    )(dpre, x1b)
    return out.reshape(2 * DFF, D)


def _row_tile(rows, cols):
    if rows * cols * 4 <= (1 << 20) or rows % 8:
        return rows
    for t in (256, 176, 128, 88, 64, 32, 16, 8):
        if rows % t == 0 and t * cols * 4 <= (1 << 20):
            return t
    return 8


def _sum8(r, name):
    _, rows, cols = r.shape
    tr = _row_tile(rows, cols)

    def body(r_ref, o_ref):
        acc = r_ref[0].astype(F32)
        for p in range(1, NDEV):
            acc = acc + r_ref[p].astype(F32)
        o_ref[...] = acc

    return pl.pallas_call(
        body, name=name, grid=(rows // tr,),
        in_specs=[pl.BlockSpec((NDEV, tr, cols), lambda i: (0, i, 0))],
        out_specs=pl.BlockSpec((tr, cols), lambda i: (i, 0)),
        out_shape=SDS((rows, cols), F32),
    )(r)


def _adamw(w, g, m, v, name):
    rows, cols = w.shape
    tr = _row_tile(rows, cols)

    def body(w_ref, g_ref, m_ref, v_ref, d_ref, nm_ref, nv_ref):
        g_ = g_ref[...]
        m_ = B1 * m_ref[...] + (1.0 - B1) * g_
        v_ = B2 * v_ref[...] + (1.0 - B2) * jnp.square(g_)
        m_hat = m_ / (1.0 - B1 ** STEP)
        v_hat = v_ / (1.0 - B2 ** STEP)
        d_ref[...] = -LR * (m_hat / (jnp.sqrt(v_hat) + AEPS) + WD * w_ref[...])
        nm_ref[...] = m_
        nv_ref[...] = v_

    spec = pl.BlockSpec((tr, cols), lambda i: (i, 0))
    shp = SDS((rows, cols), F32)
    return pl.pallas_call(
        body, name=name, grid=(rows // tr,), in_specs=[spec] * 4, out_specs=(spec,) * 3,
        out_shape=(shp, shp, shp),
    )(w, g, m, v)


def _adamw_many(ws, gs, ms, vs, name):
    n = len(ws)

    def body(*refs):
        for i in range(n):
            w_ref, g_ref, m_ref, v_ref, d_ref, nm_ref, nv_ref = refs[i::n]
            g_ = g_ref[...]
            m_ = B1 * m_ref[...] + (1.0 - B1) * g_
            v_ = B2 * v_ref[...] + (1.0 - B2) * jnp.square(g_)
            m_hat = m_ / (1.0 - B1 ** STEP)
            v_hat = v_ / (1.0 - B2 ** STEP)
            d_ref[...] = -LR * (m_hat / (jnp.sqrt(v_hat) + AEPS) + WD * w_ref[...])
            nm_ref[...] = m_
            nv_ref[...] = v_

    shapes = tuple(SDS(w.shape, F32) for w in ws)
    res = pl.pallas_call(body, name=name, out_shape=shapes * 3)(*ws, *gs, *ms, *vs)
    return res[:n], res[n:2 * n], res[2 * n:]


def _interleave(a):
    r = a.shape[0]
    return a.reshape(r, 2, NFT, FT).transpose(0, 2, 1, 3).reshape(r, 2 * DFF)


def _deinterleave(a):
    r = a.shape[0]
    return a.reshape(r, NFT, 2, FT).transpose(0, 2, 1, 3).reshape(r, 2 * DFF)


def _local_step(x2, target, rel_table, w_in_t, b_in, conv_w, conv_b, conv_ln_g, conv_ln_b, attn_norm_g,
                conv_norm_g, late_weights, ln1_g, ln1_b, ffn_conv_w, ffn_conv_b, ln2_g, ln2_b, ship_ffn_grads, ship_tail):
    buckets = jnp.asarray(_bucket_maps())
    bias = _bias_table(rel_table, buckets)

    qkv = _mm_nt_bias(x2, w_in_t, b_in, 0, 3, AW, BF16, "proj_qkv")
    ag = _mm_nt_bias(x2, w_in_t, b_in, 3 * AW // CW, 2, CW, F32, "proj_ag")
    attn, lse = _attn_fwd(qkv, bias)
    u1 = _conv_fwd(ag, conv_w, conv_b)
    mixed = _mix_fwd(attn, u1, attn_norm_g, conv_norm_g, conv_ln_g, conv_ln_b)
    w_out, w_up, w_down = late_weights(mixed)
    xh1, rstd1, x1b = _out_proj_ln1(mixed, w_out, x2, ln1_g, ln1_b)
    up, gv, act = _ffn_up(x1b, w_up, ffn_conv_w, ffn_conv_b)
    dz2, loss, g_ln2_g, g_ln2_b = _ffn_down_loss(act, w_down, xh1, ln1_g, ln1_b, ln2_g, ln2_b, target)

    dup, cs_g, cs_v = _ffn_down_bwd(dz2, w_down, gv)
    g_w_down = _mm_tn(act, dz2, DFF // 2, 512, "grad_w_down")
    dpre, gfw_il = _conv3_transpose(dup, up, _interleave(ffn_conv_w))
    dz1, g_ln1_g, g_ln1_b = _ffn_up_bwd_ln1(dpre, w_up, dz2, xh1, rstd1, ln1_g)
    g_w_out = _mm_tn(mixed, dz1, D, 512, "grad_w_out")
    zero = ship_ffn_grads(g_w_down, _grad_w_up(dpre, x1b), g_w_out)
    dmixed = _mm_nt(dz1, w_out, "dmixed")
    dattn, du1, g_an, g_cn, g_clg, g_clb, g_cb = _mix_bwd(
        dmixed, attn, u1, attn_norm_g + zero, conv_norm_g, conv_ln_g, conv_ln_b)
    dag, cs_ag, g_conv_w = _conv_bwd(du1, ag, conv_w)
    dq, dk, dv, cs_q, cs_k, cs_v2, dbias = _attn_bwd(qkv, attn, lse, dattn, bias)
    g_rel = _rel_table_grad(dbias, buckets)
    pieces = [dq, dk, dv, dag]
    g_w_in_t = _grad_w_in(pieces, x2)

    grads = dict(
        rel_table=g_rel,
        b_in=jnp.concatenate([cs_q, cs_k, cs_v2, cs_ag], axis=1),
        conv_b=g_cb, conv_ln_g=g_clg, conv_ln_b=g_clb, attn_norm_g=g_an, conv_norm_g=g_cn,
        ln1_g=g_ln1_g, ln1_b=g_ln1_b,
        ffn_conv_b=jnp.concatenate([cs_g, cs_v], axis=1),
        ln2_g=g_ln2_g, ln2_b=g_ln2_b,
        conv_w=g_conv_w[:CK],
        ffn_conv_w=_deinterleave(gfw_il),
    )
    grads["loss"] = loss
    zero11 = ship_tail(g_w_in_t, grads)
    grad_x = _grad_x(pieces, w_in_t, dz1, zero11)
    return loss, grad_x


SMALL = (("rel_table", (NBUCKET, NH)), ("b_in", (1, INW)), ("conv_b", (1, CW)), ("conv_ln_g", (1, CW)),
         ("conv_ln_b", (1, CW)), ("attn_norm_g", (1, AW)), ("conv_norm_g", (1, CW)), ("ln1_g", (1, D)),
         ("ln1_b", (1, D)), ("ffn_conv_b", (1, 2 * DFF)), ("ln2_g", (1, D)), ("ln2_b", (1, D)))
SHARDED_SMALL = (("conv_w", (CK, CW)), ("ffn_conv_w", (FK, 2 * DFF)))


def _pack(parts):
    flat = jnp.concatenate([p.reshape(-1) for p in parts])
    tile = 8 * PACK_LANES
    pad = (-flat.shape[0]) % tile
    return jnp.pad(flat, (0, pad)).reshape(-1, PACK_LANES)


def _unpack(packed, specs):
    flat = packed.reshape(-1)
    out, off = {}, 0
    for name, shp in specs:
        size = int(np.prod(shp))
        out[name] = flat[off:off + size].reshape(shp)
        off += size
    return out


def kernel(x, rel_table, w_in, b_in, conv_w, conv_b, conv_ln_g, conv_ln_b, attn_norm_g, conv_norm_g, w_out, ln1_g, ln1_b, w_up, ffn_conv_w, ffn_conv_b, w_down, ln2_g, ln2_b, loss_target, m_rel_table, m_w_in, m_b_in, m_conv_w, m_conv_b, m_conv_ln_g, m_conv_ln_b, m_attn_norm_g, m_conv_norm_g, m_w_out, m_ln1_g, m_ln1_b, m_w_up, m_ffn_conv_w, m_ffn_conv_b, m_w_down, m_ln2_g, m_ln2_b, v_rel_table, v_w_in, v_b_in, v_conv_w, v_conv_b, v_conv_ln_g, v_conv_ln_b, v_attn_norm_g, v_conv_norm_g, v_w_out, v_ln1_g, v_ln1_b, v_w_up, v_ffn_conv_w, v_ffn_conv_b, v_w_down, v_ln2_g, v_ln2_b):
    given = dict(locals())
    me = 4 * lax.axis_index("x") + 2 * lax.axis_index("y") + lax.axis_index("c")

    cols = lambda a: a.transpose(1, 0, 2).reshape(a.shape[1], NDEV * a.shape[2])
    rows = lambda a: a.reshape(NDEV * a.shape[1], a.shape[2])
    stack = lambda a: a.reshape(NDEV, a.shape[0] // NDEV, a.shape[1])

    small_specs = SMALL + SHARDED_SMALL
    packed_specs = small_specs + (("loss", (1, 1)),)
    grad, delta, new_m, new_v = {}, {}, {}, {}

    def adamw_big(n, g2d):
        shp = given[n].shape
        two = lambda a: a.reshape(shp[-2], shp[-1])
        grad[n] = g2d.reshape(shp)
        d_, m_, v_ = _adamw(two(given[n]), g2d, two(given["m_" + n]), two(given["v_" + n]), "adamw_" + n)
        delta[n], new_m[n], new_v[n] = d_.reshape(shp), m_.reshape(shp), v_.reshape(shp)
        return d_

    def own_slab(a):
        return lax.dynamic_index_in_dim(a, me, 0, keepdims=False)

    first = _exchange([(w_in[0].T.astype(BF16), "gather"), (conv_w[0], "gather"), (ffn_conv_w[0], "gather")],
                      "gather_first")
    w_in_t, conv_w_f, ffn_conv_w_f = rows(first[0]), cols(first[1]), cols(first[2])
    late_own = [w_out[0].astype(BF16), w_up[0].T.astype(BF16), w_down[0].astype(BF16)]
    late_state, zero1 = _exchange_start([(a, "gather") for a in late_own], "gather_late_start")

    def late_weights(after):
        lands = _exchange_wait(late_state, after, "gather_late_wait")
        return [rows(l) for l in lands]

    shipped = {}

    def ship_ffn_grads(g_w_down, g_w_up_t, g_w_out):
        shipped["ffn"], zero2 = _exchange_start(
            [(stack(a), "scatter") for a in (g_w_down, g_w_up_t, g_w_out)], "ffn_grads_start")
        return zero2

    def ship_tail(g_w_in_t, small_grads):
        shipped["tail"], zero3 = _exchange_start(
            [(stack(g_w_in_t), "scatter"), (_pack([small_grads[n] for n, _ in packed_specs]), "gather")],
            "tail_grads_start")
        return zero3.reshape(1, 1)

    loss, grad_x = _local_step(
        x.reshape(T, D), loss_target.reshape(T, D), rel_table, w_in_t, b_in + zero1, conv_w_f, conv_b, conv_ln_g,
        conv_ln_b, attn_norm_g, conv_norm_g, late_weights, ln1_g, ln1_b, ffn_conv_w_f, ffn_conv_b,
        ln2_g, ln2_b, ship_ffn_grads, ship_tail)

    got_down, got_up, got_out = _exchange_wait(shipped["ffn"], grad_x, "ffn_grads_wait")
    adamw_big("w_down", _sum8(got_down, "sum_w_down"))
    adamw_big("w_up", _sum8(got_up, "sum_w_up").T)
    last = adamw_big("w_out", _sum8(got_out, "sum_w_out"))

    got_in, got_small = _exchange_wait(shipped["tail"], last, "tail_grads_wait")
    adamw_big("w_in", _sum8(got_in, "sum_w_in").T)
    small = _unpack(_sum8(got_small, "sum_small"), packed_specs)
    small["conv_w"] = lax.dynamic_slice_in_dim(small["conv_w"], me * (CW // NDEV), CW // NDEV, axis=1)
    small["ffn_conv_w"] = lax.dynamic_slice_in_dim(small["ffn_conv_w"], me * (2 * DFF // NDEV), 2 * DFF // NDEV, axis=1)
    names = [n for n, _ in small_specs]
    two = lambda a: a.reshape(a.shape[-2], a.shape[-1])
    ds, nms, nvs = _adamw_many([two(given[n]) for n in names], [small[n] for n in names],
                               [two(given["m_" + n]) for n in names], [two(given["v_" + n]) for n in names], "adamw_small")
    for n, d_, m_, v_ in zip(names, ds, nms, nvs):
        shp = given[n].shape
        grad[n], delta[n], new_m[n], new_v[n] = small[n].reshape(shp), d_.reshape(shp), m_.reshape(shp), v_.reshape(shp)

    order = ("rel_table", "w_in", "b_in", "conv_w", "conv_b", "conv_ln_g", "conv_ln_b", "attn_norm_g",
             "conv_norm_g", "w_out", "ln1_g", "ln1_b", "w_up", "ffn_conv_w", "ffn_conv_b", "w_down", "ln2_g", "ln2_b")
    return (small["loss"][0, 0], grad_x.reshape(BL, S, D), *[grad[n] for n in order], *[delta[n] for n in order],
            *[new_m[n] for n in order], *[new_v[n] for n in order])
```

```python
import functools
import math

import numpy as np
import jax
import jax.numpy as jnp
from jax import lax
from jax.experimental import pallas as pl
from jax.experimental.pallas import tpu as pltpu

F32 = jnp.float32
BF16 = jnp.bfloat16
SDS = jax.ShapeDtypeStruct

NDEV = 8
D = 1024
S = 2048
BL = 2
T = BL * S
NH = 12
HD = 64
AW = NH * HD
CW = D - AW
INW = 3 * AW + 2 * CW
CK = 31
DFF = 2816
FK = 3
BLK = 128
NBUCKET = 32
BRANCHES = ((128, 1), (512, 4), (2048, 16))
ALPHA = 2.0 ** 0.25
LN_EPS = 1e-5
NEG_INF = -1e30
LR, B1, B2, AEPS, WD, STEP = 0.001, 0.9, 0.999, 1e-08, 0.01, 10

TM = 512
FT = 1408
NFT = DFF // FT
TMF = 256
PACK_LANES = 128
GRAD_WIRE = BF16

assert all(w // d == BLK for w, d in BRANCHES)


def _dot(a, b):
    return jnp.dot(a, b, preferred_element_type=F32)


def _dot_nt(a, b):
    return lax.dot_general(a, b, (((1,), (1,)), ((), ())), preferred_element_type=F32)


def _dot_tn(a, b):
    return lax.dot_general(a, b, (((0,), (0,)), ((), ())), preferred_element_type=F32)


def _rowmean(v):
    return jnp.mean(v, axis=-1, keepdims=True)


def _colsum(v):
    return jnp.sum(v, axis=0, keepdims=True)


def _sigmoid(v):
    return jax.nn.sigmoid(v)


def _exchange(items, name):
    n = len(items)
    arrs = [a for a, _ in items]
    kinds = [k for _, k in items]
    out_shapes = []
    for a, k in items:
        shp = (NDEV,) + tuple(a.shape) if k == "gather" else tuple(a.shape)
        out_shapes.append(SDS(shp, a.dtype))

    def body(*refs):
        ins = refs[:n]
        outs = refs[n:2 * n]
        send_sems, recv_sems, local_sems = refs[2 * n:]
        x, y, c = lax.axis_index("x"), lax.axis_index("y"), lax.axis_index("c")
        me = 4 * x + 2 * y + c

        def peer(k):
            px = 1 - x if k & 4 else x
            py = 1 - y if k & 2 else y
            pc = 1 - c if k & 1 else c
            return (px, py, pc), 4 * px + 2 * py + pc

        local = []
        for i in range(n):
            src = ins[i] if kinds[i] == "gather" else ins[i].at[me]
            cp = pltpu.make_async_copy(src, outs[i].at[me], local_sems.at[i])
            cp.start()
            local.append(cp)
        sends = []
        for k in range(1, NDEV):
            dev, pid = peer(k)
            for i in range(n):
                src = ins[i] if kinds[i] == "gather" else ins[i].at[pid]
                cp = pltpu.make_async_remote_copy(
                    src_ref=src, dst_ref=outs[i].at[me],
                    send_sem=send_sems.at[i, k - 1], recv_sem=recv_sems.at[i, k - 1],
                    device_id=dev, device_id_type=pl.DeviceIdType.MESH)
                cp.start()
                sends.append(cp)
        for k in range(1, NDEV):
            dev, pid = peer(k)
            for i in range(n):
                src = ins[i] if kinds[i] == "gather" else ins[i].at[pid]
                pltpu.make_async_remote_copy(
                    src_ref=src, dst_ref=outs[i].at[pid],
                    send_sem=send_sems.at[i, k - 1], recv_sem=recv_sems.at[i, k - 1],
                    device_id=dev, device_id_type=pl.DeviceIdType.MESH).wait_recv()
        for cp in sends:
            cp.wait_send()
        for cp in local:
            cp.wait()

    any_spec = pl.BlockSpec(memory_space=pl.ANY)
    return pl.pallas_call(
        body, name=name,
        out_shape=tuple(out_shapes),
        in_specs=[any_spec] * n,
        out_specs=tuple([any_spec] * n),
        scratch_shapes=[pltpu.SemaphoreType.DMA((n, NDEV - 1)),
                        pltpu.SemaphoreType.DMA((n, NDEV - 1)),
                        pltpu.SemaphoreType.DMA((n,))],
        compiler_params=pltpu.CompilerParams(has_side_effects=True),
    )(*arrs)


_HBM = pl.BlockSpec(memory_space=pltpu.HBM)
_SEM = pl.BlockSpec(memory_space=pltpu.SEMAPHORE)
_EFFECT = pltpu.SideEffectType.DATAFLOW_SIDE_EFFECTING


def _peer_of(k):
    x, y, c = lax.axis_index("x"), lax.axis_index("y"), lax.axis_index("c")
    px = 1 - x if k & 4 else x
    py = 1 - y if k & 2 else y
    pc = 1 - c if k & 1 else c
    return (px, py, pc), 4 * px + 2 * py + pc


def _split_copies(kinds, ins, lands, send_sems, recv_sems, started):
    me = 4 * lax.axis_index("x") + 2 * lax.axis_index("y") + lax.axis_index("c")
    out = []
    for k in range(1, NDEV):
        dev, pid = _peer_of(k)
        for i, kind in enumerate(kinds):
            src = ins[i] if kind == "gather" else ins[i].at[pid]
            dst = lands[i].at[me] if started else lands[i].at[pid]
            slot = i * (NDEV - 1) + k - 1
            out.append(pltpu.make_async_remote_copy(
                src_ref=src, dst_ref=dst, send_sem=send_sems.at[slot], recv_sem=recv_sems.at[slot],
                device_id=dev, device_id_type=pl.DeviceIdType.MESH))
    return out


def _exchange_start(items, name):
    n = len(items)
    kinds = [k for _, k in items]
    srcs = [pltpu.with_memory_space_constraint(a, pltpu.HBM) for a, _ in items]
    lands = []
    for a, k in items:
        shp = (NDEV,) + tuple(a.shape) if k == "gather" else tuple(a.shape)
        lands.append(pltpu.with_memory_space_constraint(lax.empty(shp, a.dtype), pltpu.HBM))

    def body(*refs):
        ins, land_refs = refs[:n], refs[n:2 * n]
        send_sems, recv_sems, own_sems = refs[2 * n:2 * n + 3]
        token = refs[-1]
        for cp in _own_copies(kinds, ins, land_refs, own_sems):
            cp.start()
        for cp in _split_copies(kinds, ins, land_refs, send_sems, recv_sems, True):
            cp.start()
        token[...] = jnp.zeros_like(token)

    sems = pltpu.SemaphoreType.DMA((n * (NDEV - 1),))
    res = pl.pallas_call(
        body, name=name,
        out_shape=(sems, sems, pltpu.SemaphoreType.DMA((n,)),
                   *[pltpu.HBM(a.shape, a.dtype) for a in srcs + lands], SDS((8, 128), F32)),
        in_specs=[_HBM] * (2 * n),
        out_specs=(_SEM, _SEM, _SEM, *[_HBM] * (2 * n), pl.BlockSpec(memory_space=pltpu.VMEM)),
        input_output_aliases={i: 3 + i for i in range(2 * n)},
        compiler_params=pltpu.CompilerParams(has_side_effects=_EFFECT),
    )(*srcs, *lands)
    return (kinds, res[0], res[1], res[2], list(res[3:3 + n]), list(res[3 + n:3 + 2 * n])), res[-1][0, 0]


def _own_copies(kinds, ins, lands, own_sems):
    me = 4 * lax.axis_index("x") + 2 * lax.axis_index("y") + lax.axis_index("c")
    return [pltpu.make_async_copy(ins[i] if kind == "gather" else ins[i].at[me], lands[i].at[me], own_sems.at[i])
            for i, kind in enumerate(kinds)]


def _exchange_wait(state, after, name):
    kinds, send_sems, recv_sems, own_sems, srcs, lands = state
    n = len(kinds)

    def body(*refs):
        ins, land_refs = refs[:n], refs[n:2 * n]
        s_sems, r_sems, o_sems = refs[2 * n:2 * n + 3]
        for cp in _split_copies(kinds, ins, land_refs, s_sems, r_sems, False):
            cp.wait_send()
            cp.wait_recv()
        for cp in _own_copies(kinds, ins, land_refs, o_sems):
            cp.wait()

    res = pl.pallas_call(
        body, name=name,
        out_shape=tuple(pltpu.HBM(a.shape, a.dtype) for a in srcs + lands),
        in_specs=[_HBM] * (2 * n) + [_SEM, _SEM, _SEM, pl.BlockSpec(memory_space=pl.ANY)],
        out_specs=tuple([_HBM] * (2 * n)),
        input_output_aliases={i: i for i in range(2 * n)},
        compiler_params=pltpu.CompilerParams(has_side_effects=_EFFECT),
    )(*srcs, *lands, send_sems, recv_sems, own_sems, after)
    return list(res[n:])


def _mm_nt_bias(a, bt, bias, row_blk0, nblk, tn, out_dtype, name):
    m_, k_ = a.shape

    def body(a_ref, b_ref, bias_ref, o_ref):
        acc = _dot_nt(a_ref[...].astype(BF16), b_ref[...])
        o_ref[...] = (acc + bias_ref[...]).astype(o_ref.dtype)

    return pl.pallas_call(
        body, name=name, grid=(nblk, m_ // TM),
        in_specs=[pl.BlockSpec((TM, k_), lambda n, m: (m, 0)),
                  pl.BlockSpec((tn, k_), lambda n, m: (row_blk0 + n, 0)),
                  pl.BlockSpec((1, tn), lambda n, m: (0, row_blk0 + n))],
        out_specs=pl.BlockSpec((TM, tn), lambda n, m: (m, n)),
        out_shape=SDS((m_, nblk * tn), out_dtype),
    )(a, bt, bias)


def _mm_nt(a, b, name):
    m_, k_ = a.shape
    n_ = b.shape[0]

    def body(a_ref, b_ref, o_ref):
        o_ref[...] = _dot_nt(a_ref[...].astype(BF16), b_ref[...].astype(BF16))

    return pl.pallas_call(
        body, name=name, grid=(m_ // TM,),
        in_specs=[pl.BlockSpec((TM, k_), lambda m: (m, 0)), pl.BlockSpec((n_, k_), lambda m: (0, 0))],
        out_specs=pl.BlockSpec((TM, n_), lambda m: (m, 0)),
        out_shape=SDS((m_, n_), F32),
    )(a, b)


def _grad_x(pieces, w_in_t, dz1, zero):
    widths = [p.shape[1] for p in pieces]

    def body(*refs):
        p_refs = refs[:len(pieces)]
        w_ref, dz_ref, z_ref, o_ref = refs[len(pieces):]
        acc = ALPHA * dz_ref[...] + z_ref[...]
        r0 = 0
        for p_ref, wd in zip(p_refs, widths):
            acc = acc + _dot(p_ref[...], w_ref[pl.ds(r0, wd), :])
            r0 += wd
        o_ref[...] = acc

    row = pl.BlockSpec((TM, D), lambda m: (m, 0))
    return pl.pallas_call(
        body, name="grad_x", grid=(T // TM,),
        in_specs=[pl.BlockSpec((TM, wd), lambda m: (m, 0)) for wd in widths]
        + [pl.BlockSpec((INW, D), lambda m: (0, 0)), row, pl.BlockSpec((1, 1), lambda m: (0, 0))],
        out_specs=row,
        out_shape=SDS((T, D), F32),
    )(*pieces, w_in_t, dz1, zero)


def _grad_w_in(pieces, x2):
    widths = [p.shape[1] for p in pieces]
    tk = 512
    nk = T // tk

    def body(*refs):
        p_refs = refs[:len(pieces)]
        x_ref, o_ref, acc = refs[len(pieces):]
        k = pl.program_id(0)

        @pl.when(k == 0)
        def _():
            acc[...] = jnp.zeros_like(acc)

        xb = x_ref[...].astype(BF16)
        r0 = 0
        for p_ref, wd in zip(p_refs, widths):
            acc[pl.ds(r0, wd), :] += _dot_tn(p_ref[...], xb)
            r0 += wd

        @pl.when(k == nk - 1)
        def _():
            o_ref[...] = acc[...].astype(o_ref.dtype)

    return pl.pallas_call(
        body, name="grad_w_in", grid=(nk,),
        in_specs=[pl.BlockSpec((tk, wd), lambda k: (k, 0)) for wd in widths] + [pl.BlockSpec((tk, D), lambda k: (k, 0))],
        out_specs=pl.BlockSpec((INW, D), lambda k: (0, 0)),
        out_shape=SDS((INW, D), GRAD_WIRE),
        scratch_shapes=[pltpu.VMEM((INW, D), F32)],
    )(*pieces, x2)


def _mm_tn(a, b, tn, tk, name):
    t_, na = a.shape
    nb = b.shape[1]
    nk = t_ // tk

    def body(a_ref, b_ref, o_ref, acc):
        k = pl.program_id(1)

        @pl.when(k == 0)
        def _():
            acc[...] = jnp.zeros_like(acc)

        acc[...] += _dot_tn(a_ref[...].astype(BF16), b_ref[...].astype(BF16))

        @pl.when(k == nk - 1)
        def _():
            o_ref[...] = acc[...].astype(o_ref.dtype)

    return pl.pallas_call(
        body, name=name, grid=(na // tn, nk),
        in_specs=[pl.BlockSpec((tk, tn), lambda n, k: (k, n)),
                  pl.BlockSpec((tk, nb), lambda n, k: (k, 0))],
        out_specs=pl.BlockSpec((tn, nb), lambda n, k: (n, 0)),
        out_shape=SDS((na, nb), GRAD_WIRE),
        scratch_shapes=[pltpu.VMEM((tn, nb), F32)],
    )(a, b)


def _bucket_maps():
    qi = np.arange(BLK)[:, None]
    kj = np.arange(2 * BLK)[None, :]
    steps = np.maximum(qi + BLK - kj, 0)
    exact = NBUCKET // 2
    maps = []
    for _, dil in BRANCHES:
        dist = steps * dil
        d_f = np.maximum(dist, 1).astype(np.float32)
        large = exact + (np.log(d_f / np.float32(exact)) / np.float32(math.log(S / exact))
                         * np.float32(NBUCKET - exact)).astype(np.int32)
        large = np.minimum(large, NBUCKET - 1)
        maps.append(np.where(dist < exact, dist, large).astype(np.int32))
    return np.stack(maps)


def _bias_table(rel_table, buckets):
    def body(t_ref, b_ref, o_ref):
        bk = b_ref[0]
        for h in range(NH):
            acc = jnp.zeros((BLK, 2 * BLK), F32)
            for k in range(NBUCKET):
                acc = jnp.where(bk == k, t_ref[k, h], acc)
            o_ref[0, h] = acc

    return pl.pallas_call(
        body, name="bias_table", grid=(len(BRANCHES),),
        in_specs=[pl.BlockSpec(memory_space=pltpu.SMEM),
                  pl.BlockSpec((1, BLK, 2 * BLK), lambda i: (i, 0, 0))],
        out_specs=pl.BlockSpec((1, NH, BLK, 2 * BLK), lambda i: (i, 0, 0, 0)),
        out_shape=SDS((len(BRANCHES), NH, BLK, 2 * BLK), F32),
    )(rel_table, buckets)


def _rel_table_grad(dbias, buckets):
    def body(d_ref, b_ref, o_ref):
        h = pl.program_id(0)
        for k in range(NBUCKET):
            tot = jnp.zeros((1, 1), F32)
            for br in range(len(BRANCHES)):
                sel = jnp.where(b_ref[br] == k, d_ref[br, 0], 0.0)
                tot = tot + jnp.sum(jnp.sum(sel, axis=1, keepdims=True), axis=0, keepdims=True)
            o_ref[0, :, pl.ds(k, 1)] = tot

    out = pl.pallas_call(
        body, name="rel_table_grad", grid=(NH,),
        in_specs=[pl.BlockSpec((len(BRANCHES), 1, BLK, 2 * BLK), lambda h: (0, h, 0, 0)),
                  pl.BlockSpec((len(BRANCHES), BLK, 2 * BLK), lambda h: (0, 0, 0))],
        out_specs=pl.BlockSpec((1, 1, NBUCKET), lambda h: (h, 0, 0)),
        out_shape=SDS((NH, 1, NBUCKET), F32),
    )(dbias, buckets)
    return out.reshape(NH, NBUCKET).T


def _block_rows(br, i):
    _, dil = BRANCHES[br]
    nb = S // dil // BLK
    if nb == 16:
        r, nidx = 0, i
    elif nb == 4:
        r, nidx = lax.shift_right_logical(i, 2), lax.bitwise_and(i, 3)
    else:
        r, nidx = i, 0
    start = r + dil * BLK * nidx
    if nb == 1:
        return start, None, None
    prev = r + dil * BLK * jnp.maximum(nidx - 1, 0)
    return start, prev, nidx > 0


def _rows(start, dil):
    if dil == 1:
        return pl.ds(pl.multiple_of(start, BLK), BLK)
    return pl.ds(start, BLK, stride=dil)


def _attn_masks():
    lane = lax.broadcasted_iota(jnp.int32, (BLK, BLK), 1)
    qi = lax.broadcasted_iota(jnp.int32, (BLK, BLK), 0)
    head0 = lane < HD
    valid_cur = lane <= qi
    valid_prev = lane >= qi
    return head0, valid_cur, valid_prev


def _attn_fwd_v1(qkv, bias):
    scale = 1.0 / math.sqrt(HD)
    nbr = len(BRANCHES)

    def body(q_ref, k_ref, v_ref, bias_ref, o_ref, lse_ref, qf, kf, vf, ob, mb, lb):
        qf[...] = q_ref[...].astype(F32)
        kf[...] = k_ref[...].astype(F32)
        vf[...] = v_ref[...].astype(F32)
        head0, valid_cur, valid_prev = _attn_masks()

        for br in range(nbr):
            dil = BRANCHES[br][1]

            def blk(i, carry, br=br, dil=dil):
                start, prev, has_prev = _block_rows(br, i)
                rows = _rows(start, dil)
                q = qf[rows, :]
                kc = kf[rows, :].astype(BF16)
                vc = vf[rows, :].astype(BF16)
                if prev is not None:
                    prows = _rows(prev, dil)
                    kp = kf[prows, :].astype(BF16)
                    vp = vf[prows, :].astype(BF16)
                    ok_prev = jnp.logical_and(valid_prev, has_prev)
                o_acc = jnp.zeros((BLK, BLK), F32)
                m_acc = jnp.zeros((BLK, BLK), F32)
                l_acc = jnp.zeros((BLK, BLK), F32)
                for j in range(2):
                    mj = head0 if j == 0 else jnp.logical_not(head0)
                    qj = jnp.where(mj, q, 0.0).astype(BF16)
                    sc = _dot_nt(qj, kc) * scale + bias_ref[br, j, :, BLK:]
                    sc = jnp.where(valid_cur, sc, NEG_INF)
                    mx = jnp.max(sc, axis=-1, keepdims=True)
                    if prev is not None:
                        sp = _dot_nt(qj, kp) * scale + bias_ref[br, j, :, :BLK]
                        sp = jnp.where(ok_prev, sp, NEG_INF)
                        mx = jnp.maximum(mx, jnp.max(sp, axis=-1, keepdims=True))
                    pc = jnp.exp(sc - mx)
                    ls = jnp.sum(pc, axis=-1, keepdims=True)
                    o = _dot(pc.astype(BF16), vc)
                    if prev is not None:
                        pp = jnp.exp(sp - mx)
                        ls = ls + jnp.sum(pp, axis=-1, keepdims=True)
                        o = o + _dot(pp.astype(BF16), vp)
                    o_acc = jnp.where(mj, o, o_acc)
                    m_acc = jnp.where(mj, mx, m_acc)
                    l_acc = jnp.where(mj, ls, l_acc)
                ob[br, rows, :] = o_acc
                mb[br, rows, :] = m_acc
                lb[br, rows, :] = l_acc
                return carry

            lax.fori_loop(0, 16, blk, 0)

        def merge(i, carry):
            rows = pl.ds(pl.multiple_of(i * 256, 256), 256)
            m_all = jnp.maximum(jnp.maximum(mb[0, rows, :], mb[1, rows, :]), mb[2, rows, :])
            num = jnp.zeros((256, BLK), F32)
            den = jnp.zeros((256, BLK), F32)
            for br in range(nbr):
                c = jnp.exp(mb[br, rows, :] - m_all)
                num = num + ob[br, rows, :] * c
                den = den + lb[br, rows, :] * c
            o_ref[rows, :] = num / den
            lse_ref[rows, :] = m_all + jnp.log(den)
            return carry

        lax.fori_loop(0, S // 256, merge, 0)

    npair = NH // 2
    blk_spec = lambda off: pl.BlockSpec((S, BLK), lambda b, hp: (b, off + hp))
    return pl.pallas_call(
        body, name="attn_fwd", grid=(BL, npair),
        in_specs=[blk_spec(0), blk_spec(npair), blk_spec(2 * npair),
                  pl.BlockSpec((nbr, 2, BLK, 2 * BLK), lambda b, hp: (0, hp, 0, 0))],
        out_specs=(blk_spec(0), blk_spec(0)),
        out_shape=(SDS((T, AW), F32), SDS((T, AW), F32)),
        scratch_shapes=[pltpu.VMEM((S, BLK), F32)] * 3 + [pltpu.VMEM((nbr, S, BLK), F32)] * 3,
    )(qkv, qkv, qkv, bias)


def _attn_bwd_v1(qkv, attn, lse, dattn, bias):
    scale = 1.0 / math.sqrt(HD)
    nbr = len(BRANCHES)

    def body(q_ref, k_ref, v_ref, o_ref, lse_ref, do_ref, bias_ref,
             dq_ref, dk_ref, dv_ref, sq_ref, sk_ref, sv_ref, db_ref,
             qf, kf, vf, dl, dqa, dka, dva):
        b = pl.program_id(1)
        qf[...] = q_ref[...].astype(F32)
        kf[...] = k_ref[...].astype(F32)
        vf[...] = v_ref[...].astype(F32)
        dqa[...] = jnp.zeros_like(dqa)
        dka[...] = jnp.zeros_like(dka)
        dva[...] = jnp.zeros_like(dva)
        head0, valid_cur, valid_prev = _attn_masks()

        @pl.when(b == 0)
        def _():
            db_ref[...] = jnp.zeros_like(db_ref)
            sq_ref[...] = jnp.zeros_like(sq_ref)
            sk_ref[...] = jnp.zeros_like(sk_ref)
            sv_ref[...] = jnp.zeros_like(sv_ref)

        def delta(i, carry):
            rows = pl.ds(pl.multiple_of(i * 256, 256), 256)
            prod = do_ref[rows, :] * o_ref[rows, :]
            h0 = lax.broadcasted_iota(jnp.int32, (256, BLK), 1) < HD
            d0 = jnp.sum(jnp.where(h0, prod, 0.0), axis=-1, keepdims=True)
            d1 = jnp.sum(jnp.where(h0, 0.0, prod), axis=-1, keepdims=True)
            dl[rows, :] = jnp.where(h0, d0, d1)
            return carry

        lax.fori_loop(0, S // 256, delta, 0)

        for br in range(nbr):
            dil = BRANCHES[br][1]

            def blk(i, carry, br=br, dil=dil):
                start, prev, has_prev = _block_rows(br, i)
                rows = _rows(start, dil)
                q = qf[rows, :]
                kc = kf[rows, :].astype(BF16)
                vc = vf[rows, :].astype(BF16)
                do = do_ref[rows, :]
                lse_b = lse_ref[rows, :]
                dl_b = dl[rows, :]
                if prev is not None:
                    prows = _rows(prev, dil)
                    kp = kf[prows, :].astype(BF16)
                    vp = vf[prows, :].astype(BF16)
                    ok_prev = jnp.logical_and(valid_prev, has_prev)
                    dk_p = jnp.zeros((BLK, BLK), F32)
                    dv_p = jnp.zeros((BLK, BLK), F32)
                dq = jnp.zeros((BLK, BLK), F32)
                dk_c = jnp.zeros((BLK, BLK), F32)
                dv_c = jnp.zeros((BLK, BLK), F32)
                for j in range(2):
                    mj = head0 if j == 0 else jnp.logical_not(head0)
                    qj = jnp.where(mj, q, 0.0).astype(BF16)
                    doj = jnp.where(mj, do, 0.0).astype(BF16)
                    lse_j = lse_b[:, j * HD:j * HD + 1]
                    dl_j = dl_b[:, j * HD:j * HD + 1]
                    sc = _dot_nt(qj, kc) * scale + bias_ref[br, j, :, BLK:]
                    pc = jnp.where(valid_cur, jnp.exp(sc - lse_j), 0.0)
                    ds_c = pc * (_dot_nt(doj, vc) - dl_j)
                    db_ref[br, j, :, BLK:] += ds_c
                    dsb = (ds_c * scale).astype(BF16)
                    dqj = _dot(dsb, kc)
                    dk_c = dk_c + _dot_tn(dsb, qj)
                    dv_c = dv_c + _dot_tn(pc.astype(BF16), doj)
                    if prev is not None:
                        sp = _dot_nt(qj, kp) * scale + bias_ref[br, j, :, :BLK]
                        pp = jnp.where(ok_prev, jnp.exp(sp - lse_j), 0.0)
                        ds_p = pp * (_dot_nt(doj, vp) - dl_j)
                        db_ref[br, j, :, :BLK] += ds_p
                        dsbp = (ds_p * scale).astype(BF16)
                        dqj = dqj + _dot(dsbp, kp)
                        dk_p = dk_p + _dot_tn(dsbp, qj)
                        dv_p = dv_p + _dot_tn(pp.astype(BF16), doj)
                    dq = jnp.where(mj, dqj, dq)
                dqa[rows, :] = dqa[rows, :] + dq
                dka[rows, :] = dka[rows, :] + dk_c
                dva[rows, :] = dva[rows, :] + dv_c
                if prev is not None:
                    dka[prows, :] = dka[prows, :] + dk_p
                    dva[prows, :] = dva[prows, :] + dv_p
                return carry

            lax.fori_loop(0, 16, blk, 0)

        def flush(i, carry):
            rows = pl.ds(pl.multiple_of(i * 256, 256), 256)
            for acc, out, cs in ((dqa, dq_ref, sq_ref), (dka, dk_ref, sk_ref), (dva, dv_ref, sv_ref)):
                val = acc[rows, :]
                out[rows, :] = val.astype(BF16)
                cs[...] += _colsum(val)
            return carry

        lax.fori_loop(0, S // 256, flush, 0)

    npair = NH // 2
    blk_spec = lambda off: pl.BlockSpec((S, BLK), lambda hp, b: (b, off + hp))
    sum_spec = pl.BlockSpec((1, BLK), lambda hp, b: (0, hp))
    return pl.pallas_call(
        body, name="attn_bwd", grid=(npair, BL),
        in_specs=[blk_spec(0), blk_spec(npair), blk_spec(2 * npair), blk_spec(0), blk_spec(0), blk_spec(0),
                  pl.BlockSpec((nbr, 2, BLK, 2 * BLK), lambda hp, b: (0, hp, 0, 0))],
        out_specs=(blk_spec(0), blk_spec(0), blk_spec(0), sum_spec, sum_spec, sum_spec,
                   pl.BlockSpec((nbr, 2, BLK, 2 * BLK), lambda hp, b: (0, hp, 0, 0))),
        out_shape=(SDS((T, AW), BF16), SDS((T, AW), BF16), SDS((T, AW), BF16),
                   SDS((1, AW), F32), SDS((1, AW), F32), SDS((1, AW), F32),
                   SDS((nbr, NH, BLK, 2 * BLK), F32)),
        scratch_shapes=[pltpu.VMEM((S, BLK), F32)] * 7,
    )(qkv, qkv, qkv, attn, lse, dattn, bias)


PADK = BLK
SCALE = 1.0 / math.sqrt(HD)
ATTN_UNROLL = 8


def _branch_geometry(br):
    dil = BRANCHES[br][1]
    sub = S // dil
    return dil, sub, sub // BLK


def _token_rows(br, i):
    dil, _, nblk = _branch_geometry(br)
    if dil == 1:
        return pl.ds(pl.multiple_of(i * BLK, BLK), BLK), i
    r = lax.shift_right_logical(i, nblk.bit_length() - 1)
    n = lax.bitwise_and(i, nblk - 1)
    return pl.ds(r + dil * BLK * n, BLK, stride=dil), n


def _sub_layout_loop(br, step):
    dil, sub, _ = _branch_geometry(br)
    rows = min(sub, 256)
    nchunk = sub // rows

    def it_step(it, carry):
        if dil == 1:
            src = pl.ds(pl.multiple_of(it * rows, rows), rows)
        else:
            r = lax.shift_right_logical(it, nchunk.bit_length() - 1)
            src = pl.ds(r + dil * rows * lax.bitwise_and(it, nchunk - 1), rows, stride=dil)
        step(src, pl.multiple_of(it * rows, BLK), rows)
        return carry

    lax.fori_loop(0, dil * nchunk, it_step, 0)


def _masked_bias(bias_ref, bm):
    qi = lax.broadcasted_iota(jnp.int32, (BLK, 2 * BLK), 0)
    kj = lax.broadcasted_iota(jnp.int32, (BLK, 2 * BLK), 1)
    first = jnp.logical_and(kj >= BLK, kj - BLK <= qi)
    valid = jnp.logical_or(first, jnp.logical_and(kj < BLK, kj >= qi))
    for br in range(len(BRANCHES)):
        for j in range(2):
            b = bias_ref[br, j]
            bm[br, 1, pl.ds(j * BLK, BLK), :] = jnp.where(valid, b, NEG_INF)
            bm[br, 0, pl.ds(j * BLK, BLK), :] = jnp.where(first, b, NEG_INF)


def _head_split(fn):
    def split(t):
        h0 = lax.broadcasted_iota(jnp.int32, t.shape, 1) < HD
        t = fn(t)
        return jnp.where(h0, t, 0.0).astype(BF16), jnp.where(h0, 0.0, t).astype(BF16)
    return split


def _attn_fwd(qkv, bias):
    nbr = len(BRANCHES)

    def body(q_ref, k_ref, v_ref, bias_ref, o_ref, lse_ref, qf, kf, vf, qs0, qs1, ks, vs, bm, ob, mb, lb):
        qf[...] = q_ref[...].astype(F32)
        kf[...] = k_ref[...].astype(F32)
        vf[...] = v_ref[...].astype(F32)
        _masked_bias(bias_ref, bm)
        ks[pl.ds(0, PADK), :] = jnp.zeros((PADK, BLK), BF16)
        vs[pl.ds(0, PADK), :] = jnp.zeros((PADK, BLK), BF16)
        head0 = lax.broadcasted_iota(jnp.int32, (BLK, BLK), 1) < HD
        split_q = _head_split(lambda t: t * SCALE)

        for br in range(nbr):
            nblk = _branch_geometry(br)[2]

            def stage(src, off, rows):
                qs0[pl.ds(off, rows), :], qs1[pl.ds(off, rows), :] = split_q(qf[src, :])
                ks[pl.ds(PADK + off, rows), :] = kf[src, :].astype(BF16)
                vs[pl.ds(PADK + off, rows), :] = vf[src, :].astype(BF16)

            _sub_layout_loop(br, stage)

            def blk(i, carry, br=br, nblk=nblk):
                base = pl.multiple_of(i * BLK, BLK)
                rows, n = _token_rows(br, i)
                q01 = jnp.concatenate([qs0[pl.ds(base, BLK), :], qs1[pl.ds(base, BLK), :]], axis=0)
                if nblk > 1:
                    kcat = ks[pl.ds(base, 2 * BLK), :]
                    vcat = vs[pl.ds(base, 2 * BLK), :]
                    s = _dot_nt(q01, kcat) + bm[br, jnp.minimum(n, 1)]
                else:
                    kcat = ks[pl.ds(PADK + base, BLK), :]
                    vcat = vs[pl.ds(PADK + base, BLK), :]
                    s = _dot_nt(q01, kcat) + bm[br, 0, :, BLK:]
                mx = jnp.max(s, axis=-1, keepdims=True)
                p = jnp.exp(s - mx)
                ls = jnp.sum(p, axis=-1, keepdims=True)
                o = _dot(p.astype(BF16), vcat)
                ob[br, rows, :] = jnp.where(head0, o[:BLK], o[BLK:])
                mb[br, rows, :] = jnp.where(head0, mx[:BLK], mx[BLK:])
                lb[br, rows, :] = jnp.where(head0, ls[:BLK], ls[BLK:])
                return carry

            lax.fori_loop(0, 16, blk, 0, unroll=ATTN_UNROLL)

        def merge(i, carry):
            rows = pl.ds(pl.multiple_of(i * 256, 256), 256)
            m_all = jnp.maximum(jnp.maximum(mb[0, rows, :], mb[1, rows, :]), mb[2, rows, :])
            num = jnp.zeros((256, BLK), F32)
            den = jnp.zeros((256, BLK), F32)
            for br in range(nbr):
                c = jnp.exp(mb[br, rows, :] - m_all)
                num = num + ob[br, rows, :] * c
                den = den + lb[br, rows, :] * c
            o_ref[rows, :] = num / den
            lse_ref[rows, :] = m_all + jnp.log(den)
            return carry

        lax.fori_loop(0, S // 256, merge, 0)

    npair = NH // 2
    blk_spec = lambda off: pl.BlockSpec((S, BLK), lambda b, hp: (b, off + hp))
    return pl.pallas_call(
        body, name="attn_fwd", grid=(BL, npair),
        in_specs=[blk_spec(0), blk_spec(npair), blk_spec(2 * npair),
                  pl.BlockSpec((nbr, 2, BLK, 2 * BLK), lambda b, hp: (0, hp, 0, 0))],
        out_specs=(blk_spec(0), blk_spec(0)),
        out_shape=(SDS((T, AW), F32), SDS((T, AW), F32)),
        scratch_shapes=[pltpu.VMEM((S, BLK), F32)] * 3 + [pltpu.VMEM((S, BLK), BF16)] * 2
        + [pltpu.VMEM((PADK + S, BLK), BF16)] * 2 + [pltpu.VMEM((nbr, 2, 2 * BLK, 2 * BLK), F32)]
        + [pltpu.VMEM((nbr, S, BLK), F32)] * 3,
    )(qkv, qkv, qkv, bias)


def _attn_bwd(qkv, attn, lse, dattn, bias):
    nbr = len(BRANCHES)

    def body(q_ref, k_ref, v_ref, o_ref, lse_ref, do_ref, bias_ref,
             dq_ref, dk_ref, dv_ref, sq_ref, sk_ref, sv_ref, db_ref,
             qf, kf, vf, dl, dqa, dka, dva, qs0, qs1, ds0, ds1, ks, vs, dks, dvs, bm):
        b = pl.program_id(1)
        qf[...] = q_ref[...].astype(F32)
        kf[...] = k_ref[...].astype(F32)
        vf[...] = v_ref[...].astype(F32)
        dqa[...] = jnp.zeros_like(dqa)
        dka[...] = jnp.zeros_like(dka)
        dva[...] = jnp.zeros_like(dva)
        _masked_bias(bias_ref, bm)
        ks[pl.ds(0, PADK), :] = jnp.zeros((PADK, BLK), BF16)
        vs[pl.ds(0, PADK), :] = jnp.zeros((PADK, BLK), BF16)
        head0 = lax.broadcasted_iota(jnp.int32, (BLK, BLK), 1) < HD
        split_q = _head_split(lambda t: t * SCALE)
        split_do = _head_split(lambda t: t)

        @pl.when(b == 0)
        def _():
            db_ref[...] = jnp.zeros_like(db_ref)
            sq_ref[...] = jnp.zeros_like(sq_ref)
            sk_ref[...] = jnp.zeros_like(sk_ref)
            sv_ref[...] = jnp.zeros_like(sv_ref)

        def delta(i, carry):
            rows = pl.ds(pl.multiple_of(i * 256, 256), 256)
            prod = do_ref[rows, :] * o_ref[rows, :]
            h0 = lax.broadcasted_iota(jnp.int32, (256, BLK), 1) < HD
            d0 = jnp.sum(jnp.where(h0, prod, 0.0), axis=-1, keepdims=True)
            d1 = jnp.sum(jnp.where(h0, 0.0, prod), axis=-1, keepdims=True)
            dl[rows, :] = jnp.where(h0, d0, d1)
            return carry

        lax.fori_loop(0, S // 256, delta, 0)

        for br in range(nbr):
            nblk = _branch_geometry(br)[2]

            def stage(src, off, rows):
                qs0[pl.ds(off, rows), :], qs1[pl.ds(off, rows), :] = split_q(qf[src, :])
                ds0[pl.ds(off, rows), :], ds1[pl.ds(off, rows), :] = split_do(do_ref[src, :])
                ks[pl.ds(PADK + off, rows), :] = kf[src, :].astype(BF16)
                vs[pl.ds(PADK + off, rows), :] = vf[src, :].astype(BF16)

            _sub_layout_loop(br, stage)
            dks[...] = jnp.zeros_like(dks)
            dvs[...] = jnp.zeros_like(dvs)

            def blk(i, carry, br=br, nblk=nblk):
                base = pl.multiple_of(i * BLK, BLK)
                rows, n = _token_rows(br, i)
                q01 = jnp.concatenate([qs0[pl.ds(base, BLK), :], qs1[pl.ds(base, BLK), :]], axis=0)
                do01 = jnp.concatenate([ds0[pl.ds(base, BLK), :], ds1[pl.ds(base, BLK), :]], axis=0)
                lse_b = lse_ref[rows, :]
                dl_b = dl[rows, :]
                lse01 = jnp.concatenate([lse_b[:, 0:1], lse_b[:, HD:HD + 1]], axis=0)
                dl01 = jnp.concatenate([dl_b[:, 0:1], dl_b[:, HD:HD + 1]], axis=0)
                if nblk > 1:
                    krows = pl.ds(base, 2 * BLK)
                    bias_m = bm[br, jnp.minimum(n, 1)]
                else:
                    krows = pl.ds(PADK + base, BLK)
                    bias_m = bm[br, 0, :, BLK:]
                kcat = ks[krows, :]
                vcat = vs[krows, :]
                p = jnp.exp(_dot_nt(q01, kcat) + bias_m - lse01)
                dsv = p * (_dot_nt(do01, vcat) - dl01)
                if nblk > 1:
                    db_ref[br, 0] += dsv[:BLK]
                    db_ref[br, 1] += dsv[BLK:]
                else:
                    db_ref[br, 0, :, BLK:] += dsv[:BLK]
                    db_ref[br, 1, :, BLK:] += dsv[BLK:]
                dsb = dsv.astype(BF16)
                dq01 = _dot(dsb, kcat)
                dqa[rows, :] = dqa[rows, :] + jnp.where(head0, dq01[:BLK], dq01[BLK:])
                dks[krows, :] = dks[krows, :] + _dot_tn(dsb, q01)
                dvs[krows, :] = dvs[krows, :] + _dot_tn(p.astype(BF16), do01)
                return carry

            lax.fori_loop(0, 16, blk, 0, unroll=ATTN_UNROLL)

            def fold(src, off, rows):
                dka[src, :] = dka[src, :] + dks[pl.ds(PADK + off, rows), :]
                dva[src, :] = dva[src, :] + dvs[pl.ds(PADK + off, rows), :]

            _sub_layout_loop(br, fold)

        def flush(i, carry):
            rows = pl.ds(pl.multiple_of(i * 256, 256), 256)
            for acc, out, cs, mul in ((dqa, dq_ref, sq_ref, SCALE), (dka, dk_ref, sk_ref, 1.0), (dva, dv_ref, sv_ref, 1.0)):
                val = acc[rows, :] * mul
                out[rows, :] = val.astype(BF16)
                cs[...] += _colsum(val)
            return carry

        lax.fori_loop(0, S // 256, flush, 0)

    npair = NH // 2
    blk_spec = lambda off: pl.BlockSpec((S, BLK), lambda hp, b: (b, off + hp))
    sum_spec = pl.BlockSpec((1, BLK), lambda hp, b: (0, hp))
    return pl.pallas_call(
        body, name="attn_bwd", grid=(npair, BL),
        in_specs=[blk_spec(0), blk_spec(npair), blk_spec(2 * npair), blk_spec(0), blk_spec(0), blk_spec(0),
                  pl.BlockSpec((nbr, 2, BLK, 2 * BLK), lambda hp, b: (0, hp, 0, 0))],
        out_specs=(blk_spec(0), blk_spec(0), blk_spec(0), sum_spec, sum_spec, sum_spec,
                   pl.BlockSpec((nbr, 2, BLK, 2 * BLK), lambda hp, b: (0, hp, 0, 0))),
        out_shape=(SDS((T, AW), BF16), SDS((T, AW), BF16), SDS((T, AW), BF16),
                   SDS((1, AW), F32), SDS((1, AW), F32), SDS((1, AW), F32),
                   SDS((nbr, NH, BLK, 2 * BLK), F32)),
        scratch_shapes=[pltpu.VMEM((S, BLK), F32)] * 7 + [pltpu.VMEM((S, BLK), BF16)] * 4
        + [pltpu.VMEM((PADK + S, BLK), BF16)] * 2 + [pltpu.VMEM((PADK + S, BLK), F32)] * 2
        + [pltpu.VMEM((nbr, 2, 2 * BLK, 2 * BLK), F32)],
    )(qkv, qkv, qkv, attn, lse, dattn, bias)


CH = 256
PADR = 32


def _tap_phases(offset_of_tap):
    taps = sorted((offset_of_tap(k) % 8, offset_of_tap(k) - offset_of_tap(k) % 8, k) for k in range(CK))
    assert all(lo + CH + ph <= CH + PADR for ph, lo, _ in taps)
    return taps


def _rows_up(win):
    made = {0: win}

    def get(phase):
        if phase not in made:
            made[phase] = pltpu.roll(win, win.shape[0] - phase, 0)
        return made[phase]
    return get


def _conv_fwd(ag, conv_w, conv_b):
    def body(ag_ref, w_ref, b_ref, u1_ref, u0p):
        u0p[pl.ds(0, PADR), :] = jnp.zeros((PADR, CW), F32)

        def glu(i, carry):
            t0 = pl.multiple_of(i * CH, CH)
            a = ag_ref[pl.ds(t0, CH), :CW]
            g = ag_ref[pl.ds(t0, CH), CW:]
            u0p[pl.ds(PADR + t0, CH), :] = a * _sigmoid(g)
            return carry

        lax.fori_loop(0, S // CH, glu, 0)

        def conv(i, carry):
            t0 = pl.multiple_of(i * CH, CH)
            win = u0p[pl.ds(t0, CH + PADR), :]
            acc = jnp.zeros((CH, CW), F32) + b_ref[...]
            up = _rows_up(win)
            for phase, lo, k in _tap_phases(lambda k: PADR - (CK - 1) + k):
                acc = acc + up(phase)[lo:lo + CH, :] * w_ref[k:k + 1, :]
            u1_ref[pl.ds(t0, CH), :] = acc
            return carry

        lax.fori_loop(0, S // CH, conv, 0)

    return pl.pallas_call(
        body, name="conv_fwd", grid=(BL,),
        in_specs=[pl.BlockSpec((S, 2 * CW), lambda b: (b, 0)),
                  pl.BlockSpec((CK, CW), lambda b: (0, 0)),
                  pl.BlockSpec((1, CW), lambda b: (0, 0))],
        out_specs=pl.BlockSpec((S, CW), lambda b: (b, 0)),
        out_shape=SDS((T, CW), F32),
        scratch_shapes=[pltpu.VMEM((S + PADR, CW), F32)],
    )(ag, conv_w, conv_b)


def _conv_post(u1, cg, cb):
    mu = _rowmean(u1)
    uc = u1 - mu
    rstd = lax.rsqrt(_rowmean(uc * uc) + LN_EPS)
    xh = uc * rstd
    u2 = xh * cg + cb
    sg = _sigmoid(u2)
    return xh, rstd, u2, sg, u2 * sg


def _mix_fwd(attn, u1, ga, gc, cg, cb):
    def body(a_ref, u_ref, ga_ref, gc_ref, cg_ref, cb_ref, o_ref):
        a = a_ref[...]
        ra = lax.rsqrt(_rowmean(a * a) + LN_EPS)
        o_ref[:, :AW] = (a * ra * ga_ref[...]).astype(BF16)
        _, _, _, _, u3 = _conv_post(u_ref[...], cg_ref[...], cb_ref[...])
        rc = lax.rsqrt(_rowmean(u3 * u3) + LN_EPS)
        o_ref[:, AW:] = (u3 * rc * gc_ref[...]).astype(BF16)

    vec = lambda w: pl.BlockSpec((1, w), lambda m: (0, 0))
    return pl.pallas_call(
        body, name="mix_fwd", grid=(T // TM,),
        in_specs=[pl.BlockSpec((TM, AW), lambda m: (m, 0)), pl.BlockSpec((TM, CW), lambda m: (m, 0)),
                  vec(AW), vec(CW), vec(CW), vec(CW)],
        out_specs=pl.BlockSpec((TM, D), lambda m: (m, 0)),
        out_shape=SDS((T, D), BF16),
    )(attn, u1, ga, gc, cg, cb)


def _mix_bwd(dz1, w_out, attn, u1, ga, gc, cg, cb):
    def body(dz_ref, w_ref, a_ref, u_ref, ga_ref, gc_ref, cg_ref, cb_ref,
             da_ref, du_ref, g_an, g_cn, g_lg, g_lb, g_cb):
        @pl.when(pl.program_id(0) == 0)
        def _():
            for r in (g_an, g_cn, g_lg, g_lb, g_cb):
                r[...] = jnp.zeros_like(r)

        dm = _dot_nt(dz_ref[...].astype(BF16), w_ref[...])
        a = a_ref[...]
        dna = dm[:, :AW]
        ra = lax.rsqrt(_rowmean(a * a) + LN_EPS)
        g_an[...] += _colsum(dna * a * ra)
        dat = dna * ga_ref[...]
        da_ref[...] = ra * dat - a * (ra * ra * ra) * _rowmean(dat * a)

        xh, rstd, u2, sg, u3 = _conv_post(u_ref[...], cg_ref[...], cb_ref[...])
        dnc = dm[:, AW:]
        rc = lax.rsqrt(_rowmean(u3 * u3) + LN_EPS)
        g_cn[...] += _colsum(dnc * u3 * rc)
        dut = dnc * gc_ref[...]
        du3 = rc * dut - u3 * (rc * rc * rc) * _rowmean(dut * u3)
        du2 = du3 * sg * (1.0 + u2 * (1.0 - sg))
        g_lg[...] += _colsum(du2 * xh)
        g_lb[...] += _colsum(du2)
        dxh = du2 * cg_ref[...]
        du1 = rstd * (dxh - _rowmean(dxh) - xh * _rowmean(dxh * xh))
        g_cb[...] += _colsum(du1)
        du_ref[...] = du1

    vec = lambda w: pl.BlockSpec((1, w), lambda m: (0, 0))
    return pl.pallas_call(
        body, name="mix_bwd", grid=(T // TM,),
        in_specs=[pl.BlockSpec((TM, D), lambda m: (m, 0)), pl.BlockSpec((D, D), lambda m: (0, 0)),
                  pl.BlockSpec((TM, AW), lambda m: (m, 0)),
                  pl.BlockSpec((TM, CW), lambda m: (m, 0)), vec(AW), vec(CW), vec(CW), vec(CW)],
        out_specs=(pl.BlockSpec((TM, AW), lambda m: (m, 0)), pl.BlockSpec((TM, CW), lambda m: (m, 0)),
                   vec(AW), vec(CW), vec(CW), vec(CW), vec(CW)),
        out_shape=(SDS((T, AW), F32), SDS((T, CW), F32),
                   SDS((1, AW), F32), SDS((1, CW), F32), SDS((1, CW), F32), SDS((1, CW), F32), SDS((1, CW), F32)),
    )(dz1, w_out, attn, u1, ga, gc, cg, cb)


def _conv_bwd(du1, ag, conv_w):
    def body(du_ref, ag_ref, w_ref, dag_ref, cs_ref, gw_ref, u0p, dup):
        @pl.when(pl.program_id(0) == 0)
        def _():
            cs_ref[...] = jnp.zeros_like(cs_ref)
            gw_ref[...] = jnp.zeros_like(gw_ref)

        u0p[pl.ds(0, PADR), :] = jnp.zeros((PADR, CW), F32)
        dup[pl.ds(S, PADR), :] = jnp.zeros((PADR, CW), F32)

        def fill(i, carry):
            t0 = pl.multiple_of(i * CH, CH)
            a = ag_ref[pl.ds(t0, CH), :CW]
            g = ag_ref[pl.ds(t0, CH), CW:]
            u0p[pl.ds(PADR + t0, CH), :] = a * _sigmoid(g)
            dup[pl.ds(t0, CH), :] = du_ref[pl.ds(t0, CH), :]
            return carry

        lax.fori_loop(0, S // CH, fill, 0)

        def chunk(i, carry):
            t0 = pl.multiple_of(i * CH, CH)
            d = dup[pl.ds(t0, CH), :]
            win_u = u0p[pl.ds(t0, CH + PADR), :]
            win_d = dup[pl.ds(t0, CH + PADR), :]
            du0 = jnp.zeros((CH, CW), F32)
            up_u = _rows_up(win_u)
            for phase, lo, k in _tap_phases(lambda k: PADR - (CK - 1) + k):
                gw_ref[k:k + 1, :] += _colsum(d * up_u(phase)[lo:lo + CH, :])
            up_d = _rows_up(win_d)
            for phase, lo, k in _tap_phases(lambda k: CK - 1 - k):
                du0 = du0 + up_d(phase)[lo:lo + CH, :] * w_ref[k:k + 1, :]
            a = ag_ref[pl.ds(t0, CH), :CW]
            sg = _sigmoid(ag_ref[pl.ds(t0, CH), CW:])
            da = du0 * sg
            dg = du0 * a * sg * (1.0 - sg)
            dag_ref[pl.ds(t0, CH), :CW] = da.astype(BF16)
            dag_ref[pl.ds(t0, CH), CW:] = dg.astype(BF16)
            cs_ref[:, :CW] += _colsum(da)
            cs_ref[:, CW:] += _colsum(dg)
            return carry

        lax.fori_loop(0, S // CH, chunk, 0)

    return pl.pallas_call(
        body, name="conv_bwd", grid=(BL,),
        in_specs=[pl.BlockSpec((S, CW), lambda b: (b, 0)), pl.BlockSpec((S, 2 * CW), lambda b: (b, 0)),
                  pl.BlockSpec((CK, CW), lambda b: (0, 0))],
        out_specs=(pl.BlockSpec((S, 2 * CW), lambda b: (b, 0)),
                   pl.BlockSpec((1, 2 * CW), lambda b: (0, 0)),
                   pl.BlockSpec((PADR, CW), lambda b: (0, 0))),
        out_shape=(SDS((T, 2 * CW), BF16), SDS((1, 2 * CW), F32), SDS((PADR, CW), F32)),
        scratch_shapes=[pltpu.VMEM((S + PADR, CW), F32), pltpu.VMEM((S + PADR, CW), F32)],
    )(du1, ag, conv_w)


def _layer_norm_fwd(z):
    mu = _rowmean(z)
    zc = z - mu
    rstd = lax.rsqrt(_rowmean(zc * zc) + LN_EPS)
    return zc * rstd, rstd


def _layer_norm_bwd(dy, xh, rstd, g):
    dxh = dy * g
    return rstd * (dxh - _rowmean(dxh) - xh * _rowmean(dxh * xh))


def _out_proj_ln1(mixed, w_out, x2, g1, b1):
    def body(a_ref, w_ref, x_ref, g_ref, b_ref, xh_ref, rstd_ref, x1_ref):
        z = ALPHA * x_ref[...] + _dot(a_ref[...], w_ref[...])
        xh, rstd = _layer_norm_fwd(z)
        xh_ref[...] = xh
        rstd_ref[...] = rstd
        x1_ref[...] = (xh * g_ref[...] + b_ref[...]).astype(BF16)

    vec = pl.BlockSpec((1, D), lambda m: (0, 0))
    row = pl.BlockSpec((TM, D), lambda m: (m, 0))
    return pl.pallas_call(
        body, name="out_proj_ln1", grid=(T // TM,),
        in_specs=[row, pl.BlockSpec((D, D), lambda m: (0, 0)), row, vec, vec],
        out_specs=(row, pl.BlockSpec((TM, 1), lambda m: (m, 0)), row),
        out_shape=(SDS((T, D), F32), SDS((T, 1), F32), SDS((T, D), BF16)),
    )(mixed, w_out, x2, g1, b1)


def _seq_start(m):
    return lax.bitwise_and(m, S // TMF - 1) == 0


def _causal3(ext, w_ref, b_ref):
    x0 = ext[pl.ds(8, TM), :]
    x1 = ext[pl.ds(7, TM), :]
    x2 = ext[pl.ds(6, TM), :]
    y = w_ref[2:3, :] * x0 + w_ref[1:2, :] * x1 + w_ref[0:1, :] * x2 + b_ref[...]
    return y, x0, x1, x2


def _shift_down(x, before, k):
    rolled = pltpu.roll(x, k, 0)
    row = lax.broadcasted_iota(jnp.int32, before.shape, 0)
    head = jnp.where(row < k, pltpu.roll(before, k, 0), rolled[:8])
    return jnp.concatenate([head, rolled[8:]], axis=0)


def _shift_up(x, after, k):
    n = x.shape[0]
    rolled = pltpu.roll(x, n - k, 0)
    row = lax.broadcasted_iota(jnp.int32, after.shape, 0)
    tail = jnp.where(row >= 8 - k, pltpu.roll(after, 8 - k, 0), rolled[n - 8:])
    return jnp.concatenate([rolled[:n - 8], tail], axis=0)


def _ffn_up(x1b, w_up, fcw, fcb):
    def body(x_ref, wg_ref, wv_ref, cwg_ref, cwv_ref, cbg_ref, cbv_ref, up_ref, gv_ref, act_ref, prev_g, prev_v):
        @pl.when(_seq_start(pl.program_id(1)))
        def _():
            prev_g[...] = jnp.zeros_like(prev_g)
            prev_v[...] = jnp.zeros_like(prev_v)

        x = x_ref[...]
        outs = []
        for w_ref, cw_ref, cb_ref, prev, lo in ((wg_ref, cwg_ref, cbg_ref, prev_g, 0), (wv_ref, cwv_ref, cbv_ref, prev_v, FT)):
            ub = _dot_nt(x, w_ref[...]).astype(BF16)
            up_ref[:, lo:lo + FT] = ub
            u = ub.astype(F32)
            before = prev[...]
            y = (cw_ref[2:3, :] * u + cw_ref[1:2, :] * _shift_down(u, before, 1)
                 + cw_ref[0:1, :] * _shift_down(u, before, 2) + cb_ref[...])
            prev[...] = u[TMF - 8:]
            yb = y.astype(BF16)
            gv_ref[:, lo:lo + FT] = yb
            outs.append(yb.astype(F32))
        gate, val = outs
        act_ref[...] = (gate * _sigmoid(gate) * val).astype(BF16)

    wspec = lambda off: pl.BlockSpec((FT, D), lambda n, m: (n + off, 0))
    cwspec = lambda off: pl.BlockSpec((FK, FT), lambda n, m: (0, n + off))
    cbspec = lambda off: pl.BlockSpec((1, FT), lambda n, m: (0, n + off))
    pair = pl.BlockSpec((TMF, 2 * FT), lambda n, m: (m, n))
    return pl.pallas_call(
        body, name="ffn_up", grid=(NFT, T // TMF),
        in_specs=[pl.BlockSpec((TMF, D), lambda n, m: (m, 0)), wspec(0), wspec(NFT),
                  cwspec(0), cwspec(NFT), cbspec(0), cbspec(NFT)],
        out_specs=(pair, pair, pl.BlockSpec((TMF, FT), lambda n, m: (m, n))),
        out_shape=(SDS((T, 2 * DFF), BF16), SDS((T, 2 * DFF), BF16), SDS((T, DFF), BF16)),
        scratch_shapes=[pltpu.VMEM((8, FT), F32)] * 2,
    )(x1b, w_up, w_up, fcw, fcw, fcb, fcb)


def _ffn_down_loss(act, w_down, xh1, g1, b1, g2, b2, target):
    def body(a_ref, w_ref, xh1_ref, g1_ref, b1_ref, g2_ref, b2_ref, t_ref, dz_ref, loss_ref, gg_ref, gb_ref):
        @pl.when(pl.program_id(0) == 0)
        def _():
            loss_ref[...] = jnp.zeros_like(loss_ref)
            gg_ref[...] = jnp.zeros_like(gg_ref)
            gb_ref[...] = jnp.zeros_like(gb_ref)

        x1 = xh1_ref[...] * g1_ref[...] + b1_ref[...]
        z = ALPHA * x1 + _dot(a_ref[...], w_ref[...])
        xh, rstd = _layer_norm_fwd(z)
        diff = xh * g2_ref[...] + b2_ref[...] - t_ref[...]
        loss_ref[...] += 0.5 * _colsum(_rowmean(diff * diff))
        dout = diff * (1.0 / D)
        gg_ref[...] += _colsum(dout * xh)
        gb_ref[...] += _colsum(dout)
        dz_ref[...] = _layer_norm_bwd(dout, xh, rstd, g2_ref[...])

    vec = pl.BlockSpec((1, D), lambda m: (0, 0))
    row = pl.BlockSpec((TM, D), lambda m: (m, 0))
    return pl.pallas_call(
        body, name="ffn_down_loss", grid=(T // TM,),
        in_specs=[pl.BlockSpec((TM, DFF), lambda m: (m, 0)), pl.BlockSpec((DFF, D), lambda m: (0, 0)),
                  row, vec, vec, vec, vec, row],
        out_specs=(row, pl.BlockSpec((1, 1), lambda m: (0, 0)), vec, vec),
        out_shape=(SDS((T, D), F32), SDS((1, 1), F32), SDS((1, D), F32), SDS((1, D), F32)),
    )(act, w_down, xh1, g1, b1, g2, b2, target)


def _ffn_down_bwd(dz2, w_down, gv):
    def body(dz_ref, wd_ref, gv_ref, dup_ref, csg_ref, csv_ref):
        @pl.when(pl.program_id(1) == 0)
        def _():
            csg_ref[...] = jnp.zeros_like(csg_ref)
            csv_ref[...] = jnp.zeros_like(csv_ref)

        dact = _dot_nt(dz_ref[...].astype(BF16), wd_ref[...])
        gate = gv_ref[:, :FT].astype(F32)
        val = gv_ref[:, FT:].astype(F32)
        sg = _sigmoid(gate)
        gs = gate * sg
        dgate = dact * val * (sg + gs * (1.0 - sg))
        dval = dact * gs
        dup_ref[:, :FT] = dgate.astype(BF16)
        dup_ref[:, FT:] = dval.astype(BF16)
        csg_ref[...] += _colsum(dgate)
        csv_ref[...] += _colsum(dval)

    cs = pl.BlockSpec((1, FT), lambda n, m: (0, n))
    pair = pl.BlockSpec((TMF, 2 * FT), lambda n, m: (m, n))
    return pl.pallas_call(
        body, name="ffn_down_bwd", grid=(NFT, T // TMF),
        in_specs=[pl.BlockSpec((TMF, D), lambda n, m: (m, 0)), pl.BlockSpec((FT, D), lambda n, m: (n, 0)), pair],
        out_specs=(pair, cs, cs),
        out_shape=(SDS((T, 2 * DFF), BF16), SDS((1, DFF), F32), SDS((1, DFF), F32)),
    )(dz2, w_down, gv)


HALO = 16


def _conv3_transpose(dup, up, fcw_il):
    tiles = T // TMF

    def body(d_ref, h_ref, u_ref, w_ref, o_ref, gw_ref):
        m = pl.program_id(1)

        @pl.when(m == 0)
        def _():
            gw_ref[...] = jnp.zeros_like(gw_ref)

        d0 = d_ref[...].astype(F32)
        last = lax.bitwise_and(m + 1, S // TMF - 1) == 0
        after = jnp.where(last, 0.0, h_ref[...].astype(F32)[:8])
        d1 = _shift_up(d0, after, 1)
        d2 = _shift_up(d0, after, 2)
        o_ref[...] = (w_ref[2:3, :] * d0 + w_ref[1:2, :] * d1 + w_ref[0:1, :] * d2).astype(BF16)
        u = u_ref[...].astype(F32)
        for k, dk in enumerate((d2, d1, d0)):
            gw_ref[k:k + 1, :] += _colsum(dk * u)

    pair = pl.BlockSpec((TMF, 2 * FT), lambda n, m: (m, n))
    return pl.pallas_call(
        body, name="conv3_transpose", grid=(NFT, tiles),
        in_specs=[pair,
                  pl.BlockSpec((HALO, 2 * FT), lambda n, m: (jnp.minimum((m + 1) * (TMF // HALO), T // HALO - 1), n)),
                  pair, pl.BlockSpec((FK, 2 * FT), lambda n, m: (0, n))],
        out_specs=(pair, pl.BlockSpec((FK, 2 * FT), lambda n, m: (0, n))),
        out_shape=(SDS((T, 2 * DFF), BF16), SDS((FK, 2 * DFF), F32)),
    )(dup, dup, up, fcw_il)


def _ffn_up_bwd_ln1(dpre, w_up, dz2, xh1, rstd1, g1):
    def body(a_ref, w_ref, dz2_ref, xh_ref, rstd_ref, g_ref, dz1_ref, gg_ref, gb_ref):
        @pl.when(pl.program_id(0) == 0)
        def _():
            gg_ref[...] = jnp.zeros_like(gg_ref)
            gb_ref[...] = jnp.zeros_like(gb_ref)

        dx1 = ALPHA * dz2_ref[...]
        for n in range(NFT):
            for half in range(2):
                a = a_ref[:, (2 * n + half) * FT:(2 * n + half + 1) * FT]
                w = w_ref[pl.ds((half * NFT + n) * FT, FT), :]
                dx1 = dx1 + _dot(a, w)
        xh = xh_ref[...]
        gg_ref[...] += _colsum(dx1 * xh)
        gb_ref[...] += _colsum(dx1)
        dz1_ref[...] = _layer_norm_bwd(dx1, xh, rstd_ref[...], g_ref[...])

    vec = pl.BlockSpec((1, D), lambda m: (0, 0))
    row = pl.BlockSpec((TMF, D), lambda m: (m, 0))
    return pl.pallas_call(
        body, name="ffn_up_bwd_ln1", grid=(T // TMF,),
        in_specs=[pl.BlockSpec((TMF, 2 * DFF), lambda m: (m, 0)), pl.BlockSpec((2 * DFF, D), lambda m: (0, 0)),
                  row, row, pl.BlockSpec((TMF, 1), lambda m: (m, 0)), vec],
        out_specs=(row, vec, vec),
        out_shape=(SDS((T, D), F32), SDS((1, D), F32), SDS((1, D), F32)),
    )(dpre, w_up, dz2, xh1, rstd1, g1)


def _grad_w_up(dpre, x1b):
    tk = 1024

    def body(a_ref, b_ref, o_ref, acc):
        k = pl.program_id(1)

        @pl.when(k == 0)
        def _():
            acc[...] = jnp.zeros_like(acc)

        acc[...] += _dot_tn(a_ref[...], b_ref[...])

        @pl.when(k == T // tk - 1)
        def _():
            o_ref[0] = acc[pl.ds(0, FT), :].astype(o_ref.dtype)
            o_ref[1] = acc[pl.ds(FT, FT), :].astype(o_ref.dtype)

    out = pl.pallas_call(
        body, name="grad_w_up", grid=(NFT, T // tk),
        in_specs=[pl.BlockSpec((tk, 2 * FT), lambda n, k: (k, n)), pl.BlockSpec((tk, D), lambda n, k: (k, 0))],
        out_specs=pl.BlockSpec((2, FT, D), lambda n, k: (0, n, 0)),
        out_shape=SDS((2, DFF, D), GRAD_WIRE),
        scratch_shapes=[pltpu.VMEM((2 * FT, D), F32)],
    )(dpre, x1b)
    return out.reshape(2 * DFF, D)


def _row_tile(rows, cols):
    if rows * cols * 4 <= (1 << 20) or rows % 8:
        return rows
    for t in (256, 176, 128, 88, 64, 32, 16, 8):
        if rows % t == 0 and t * cols * 4 <= (1 << 20):
            return t
    return 8


def _sum8(r, name):
    _, rows, cols = r.shape
    tr = _row_tile(rows, cols)

    def body(r_ref, o_ref):
        acc = r_ref[0].astype(F32)
        for p in range(1, NDEV):
            acc = acc + r_ref[p].astype(F32)
        o_ref[...] = acc

    return pl.pallas_call(
        body, name=name, grid=(rows // tr,),
        in_specs=[pl.BlockSpec((NDEV, tr, cols), lambda i: (0, i, 0))],
        out_specs=pl.BlockSpec((tr, cols), lambda i: (i, 0)),
        out_shape=SDS((rows, cols), F32),
    )(r)


def _adamw(w, g, m, v, name):
    rows, cols = w.shape
    tr = _row_tile(rows, cols)

    def body(w_ref, g_ref, m_ref, v_ref, d_ref, nm_ref, nv_ref):
        g_ = g_ref[...]
        m_ = B1 * m_ref[...] + (1.0 - B1) * g_
        v_ = B2 * v_ref[...] + (1.0 - B2) * jnp.square(g_)
        m_hat = m_ / (1.0 - B1 ** STEP)
        v_hat = v_ / (1.0 - B2 ** STEP)
        d_ref[...] = -LR * (m_hat / (jnp.sqrt(v_hat) + AEPS) + WD * w_ref[...])
        nm_ref[...] = m_
        nv_ref[...] = v_

    spec = pl.BlockSpec((tr, cols), lambda i: (i, 0))
    shp = SDS((rows, cols), F32)
    return pl.pallas_call(
        body, name=name, grid=(rows // tr,), in_specs=[spec] * 4, out_specs=(spec,) * 3,
        out_shape=(shp, shp, shp),
    )(w, g, m, v)


def _adamw_many(ws, gs, ms, vs, name):
    n = len(ws)

    def body(*refs):
        for i in range(n):
            w_ref, g_ref, m_ref, v_ref, d_ref, nm_ref, nv_ref = refs[i::n]
            g_ = g_ref[...]
            m_ = B1 * m_ref[...] + (1.0 - B1) * g_
            v_ = B2 * v_ref[...] + (1.0 - B2) * jnp.square(g_)
            m_hat = m_ / (1.0 - B1 ** STEP)
            v_hat = v_ / (1.0 - B2 ** STEP)
            d_ref[...] = -LR * (m_hat / (jnp.sqrt(v_hat) + AEPS) + WD * w_ref[...])
            nm_ref[...] = m_
            nv_ref[...] = v_

    shapes = tuple(SDS(w.shape, F32) for w in ws)
    res = pl.pallas_call(body, name=name, out_shape=shapes * 3)(*ws, *gs, *ms, *vs)
    return res[:n], res[n:2 * n], res[2 * n:]


def _interleave(a):
    r = a.shape[0]
    return a.reshape(r, 2, NFT, FT).transpose(0, 2, 1, 3).reshape(r, 2 * DFF)


def _deinterleave(a):
    r = a.shape[0]
    return a.reshape(r, NFT, 2, FT).transpose(0, 2, 1, 3).reshape(r, 2 * DFF)


def _local_step(x2, target, rel_table, first_weights, b_in, conv_b, conv_ln_g, conv_ln_b, attn_norm_g,
                conv_norm_g, late_weights, ln1_g, ln1_b, ffn_conv_b, ln2_g, ln2_b, ship_ffn_grads, ship_tail):
    buckets = jnp.asarray(_bucket_maps())
    bias = _bias_table(rel_table, buckets)
    w_in_t, conv_w, ffn_conv_w = first_weights(bias)

    qkv = _mm_nt_bias(x2, w_in_t, b_in, 0, 3, AW, BF16, "proj_qkv")
    ag = _mm_nt_bias(x2, w_in_t, b_in, 3 * AW // CW, 2, CW, F32, "proj_ag")
    attn, lse = _attn_fwd(qkv, bias)
    u1 = _conv_fwd(ag, conv_w, conv_b)
    mixed = _mix_fwd(attn, u1, attn_norm_g, conv_norm_g, conv_ln_g, conv_ln_b)
    w_out, w_up, w_down = late_weights(mixed)
    xh1, rstd1, x1b = _out_proj_ln1(mixed, w_out, x2, ln1_g, ln1_b)
    up, gv, act = _ffn_up(x1b, w_up, ffn_conv_w, ffn_conv_b)
    dz2, loss, g_ln2_g, g_ln2_b = _ffn_down_loss(act, w_down, xh1, ln1_g, ln1_b, ln2_g, ln2_b, target)

    dup, cs_g, cs_v = _ffn_down_bwd(dz2, w_down, gv)
    g_w_down = _mm_tn(act, dz2, DFF // 2, 512, "grad_w_down")
    dpre, gfw_il = _conv3_transpose(dup, up, _interleave(ffn_conv_w))
    dz1, g_ln1_g, g_ln1_b = _ffn_up_bwd_ln1(dpre, w_up, dz2, xh1, rstd1, ln1_g)
    g_w_out = _mm_tn(mixed, dz1, D, 512, "grad_w_out")
    zero = ship_ffn_grads(g_w_down, _grad_w_up(dpre, x1b), g_w_out)
    dattn, du1, g_an, g_cn, g_clg, g_clb, g_cb = _mix_bwd(
        dz1, w_out, attn, u1, attn_norm_g + zero, conv_norm_g, conv_ln_g, conv_ln_b)
    dag, cs_ag, g_conv_w = _conv_bwd(du1, ag, conv_w)
    dq, dk, dv, cs_q, cs_k, cs_v2, dbias = _attn_bwd(qkv, attn, lse, dattn, bias)
    g_rel = _rel_table_grad(dbias, buckets)
    pieces = [dq, dk, dv, dag]
    g_w_in_t = _grad_w_in(pieces, x2)

    grads = dict(
        rel_table=g_rel,
        b_in=jnp.concatenate([cs_q, cs_k, cs_v2, cs_ag], axis=1),
        conv_b=g_cb, conv_ln_g=g_clg, conv_ln_b=g_clb, attn_norm_g=g_an, conv_norm_g=g_cn,
        ln1_g=g_ln1_g, ln1_b=g_ln1_b,
        ffn_conv_b=jnp.concatenate([cs_g, cs_v], axis=1),
        ln2_g=g_ln2_g, ln2_b=g_ln2_b,
        conv_w=g_conv_w[:CK],
        ffn_conv_w=_deinterleave(gfw_il),
    )
    grads["loss"] = loss
    zero11 = ship_tail(g_w_in_t, grads)
    grad_x = _grad_x(pieces, w_in_t, dz1, zero11)
    return loss, grad_x


SMALL = (("rel_table", (NBUCKET, NH)), ("b_in", (1, INW)), ("conv_b", (1, CW)), ("conv_ln_g", (1, CW)),
         ("conv_ln_b", (1, CW)), ("attn_norm_g", (1, AW)), ("conv_norm_g", (1, CW)), ("ln1_g", (1, D)),
         ("ln1_b", (1, D)), ("ffn_conv_b", (1, 2 * DFF)), ("ln2_g", (1, D)), ("ln2_b", (1, D)))
SHARDED_SMALL = (("conv_w", (CK, CW)), ("ffn_conv_w", (FK, 2 * DFF)))


def _pack(parts):
    flat = jnp.concatenate([p.reshape(-1) for p in parts])
    tile = 8 * PACK_LANES
    pad = (-flat.shape[0]) % tile
    return jnp.pad(flat, (0, pad)).reshape(-1, PACK_LANES)


def _unpack(packed, specs):
    flat = packed.reshape(-1)
    out, off = {}, 0
    for name, shp in specs:
        size = int(np.prod(shp))
        out[name] = flat[off:off + size].reshape(shp)
        off += size
    return out


def kernel(x, rel_table, w_in, b_in, conv_w, conv_b, conv_ln_g, conv_ln_b, attn_norm_g, conv_norm_g, w_out, ln1_g, ln1_b, w_up, ffn_conv_w, ffn_conv_b, w_down, ln2_g, ln2_b, loss_target, m_rel_table, m_w_in, m_b_in, m_conv_w, m_conv_b, m_conv_ln_g, m_conv_ln_b, m_attn_norm_g, m_conv_norm_g, m_w_out, m_ln1_g, m_ln1_b, m_w_up, m_ffn_conv_w, m_ffn_conv_b, m_w_down, m_ln2_g, m_ln2_b, v_rel_table, v_w_in, v_b_in, v_conv_w, v_conv_b, v_conv_ln_g, v_conv_ln_b, v_attn_norm_g, v_conv_norm_g, v_w_out, v_ln1_g, v_ln1_b, v_w_up, v_ffn_conv_w, v_ffn_conv_b, v_w_down, v_ln2_g, v_ln2_b):
    given = dict(locals())
    me = 4 * lax.axis_index("x") + 2 * lax.axis_index("y") + lax.axis_index("c")

    cols = lambda a: a.transpose(1, 0, 2).reshape(a.shape[1], NDEV * a.shape[2])
    rows = lambda a: a.reshape(NDEV * a.shape[1], a.shape[2])
    stack = lambda a: a.reshape(NDEV, a.shape[0] // NDEV, a.shape[1])

    small_specs = SMALL + SHARDED_SMALL
    packed_specs = small_specs + (("loss", (1, 1)),)
    grad, delta, new_m, new_v = {}, {}, {}, {}

    def adamw_big(n, g2d):
        shp = given[n].shape
        two = lambda a: a.reshape(shp[-2], shp[-1])
        grad[n] = g2d.reshape(shp)
        d_, m_, v_ = _adamw(two(given[n]), g2d, two(given["m_" + n]), two(given["v_" + n]), "adamw_" + n)
        delta[n], new_m[n], new_v[n] = d_.reshape(shp), m_.reshape(shp), v_.reshape(shp)
        return d_

    def own_slab(a):
        return lax.dynamic_index_in_dim(a, me, 0, keepdims=False)

    first_state, zero0 = _exchange_start(
        [(w_in[0].T.astype(BF16), "gather"), (conv_w[0], "gather"), (ffn_conv_w[0], "gather")], "gather_first_start")

    def first_weights(after):
        lands = _exchange_wait(first_state, after, "gather_first_wait")
        return rows(lands[0]), cols(lands[1]), cols(lands[2])

    late_state, zero1 = _exchange_start(
        [(w_out[0].astype(BF16), "gather"), (w_up[0].T.astype(BF16) + zero0.astype(BF16), "gather"),
         (w_down[0].astype(BF16), "gather")], "gather_late_start")

    def late_weights(after):
        return [rows(l) for l in _exchange_wait(late_state, after, "gather_late_wait")]

    shipped = {}

    def ship_ffn_grads(g_w_down, g_w_up_t, g_w_out):
        shipped["ffn"], zero2 = _exchange_start(
            [(stack(a), "scatter") for a in (g_w_down, g_w_up_t, g_w_out)], "ffn_grads_start")
        return zero2

    def ship_tail(g_w_in_t, small_grads):
        shipped["tail"], zero3 = _exchange_start(
            [(stack(g_w_in_t), "scatter"), (_pack([small_grads[n] for n, _ in packed_specs]), "gather")],
            "tail_grads_start")
        return zero3.reshape(1, 1)

    loss, grad_x = _local_step(
        x.reshape(T, D), loss_target.reshape(T, D), rel_table + zero1, first_weights, b_in, conv_b, conv_ln_g,
        conv_ln_b, attn_norm_g, conv_norm_g, late_weights, ln1_g, ln1_b, ffn_conv_b,
        ln2_g, ln2_b, ship_ffn_grads, ship_tail)

    got_down, got_up, got_out = _exchange_wait(shipped["ffn"], grad_x, "ffn_grads_wait")
    adamw_big("w_down", _sum8(got_down, "sum_w_down"))
    adamw_big("w_up", _sum8(got_up, "sum_w_up").T)
    last = adamw_big("w_out", _sum8(got_out, "sum_w_out"))

    got_in, got_small = _exchange_wait(shipped["tail"], last, "tail_grads_wait")
    adamw_big("w_in", _sum8(got_in, "sum_w_in").T)
    small = _unpack(_sum8(got_small, "sum_small"), packed_specs)
    small["conv_w"] = lax.dynamic_slice_in_dim(small["conv_w"], me * (CW // NDEV), CW // NDEV, axis=1)
    small["ffn_conv_w"] = lax.dynamic_slice_in_dim(small["ffn_conv_w"], me * (2 * DFF // NDEV), 2 * DFF // NDEV, axis=1)
    names = [n for n, _ in small_specs]
    two = lambda a: a.reshape(a.shape[-2], a.shape[-1])
    ds, nms, nvs = _adamw_many([two(given[n]) for n in names], [small[n] for n in names],
                               [two(given["m_" + n]) for n in names], [two(given["v_" + n]) for n in names], "adamw_small")
    for n, d_, m_, v_ in zip(names, ds, nms, nvs):
        shp = given[n].shape
        grad[n], delta[n], new_m[n], new_v[n] = small[n].reshape(shp), d_.reshape(shp), m_.reshape(shp), v_.reshape(shp)

    order = ("rel_table", "w_in", "b_in", "conv_w", "conv_b", "conv_ln_g", "conv_ln_b", "attn_norm_g",
             "conv_norm_g", "w_out", "ln1_g", "ln1_b", "w_up", "ffn_conv_w", "ffn_conv_b", "w_down", "ln2_g", "ln2_b")
    return (small["loss"][0, 0], grad_x.reshape(BL, S, D), *[grad[n] for n in order], *[delta[n] for n in order],
            *[new_m[n] for n in order], *[new_v[n] for n in order])
```

```python
import math

import numpy as np
import jax
import jax.numpy as jnp
from jax import lax
from jax.experimental import pallas as pl
from jax.experimental.pallas import tpu as pltpu

F32 = jnp.float32
BF16 = jnp.bfloat16
SDS = jax.ShapeDtypeStruct

NDEV = 8
D = 1024
S = 2048
BL = 2
T = BL * S
NH = 12
HD = 64
AW = NH * HD
CW = D - AW
INW = 3 * AW + 2 * CW
CK = 31
DFF = 2816
FK = 3
BLK = 128
NBUCKET = 32
BRANCHES = ((128, 1), (512, 4), (2048, 16))
ALPHA = 2.0 ** 0.25
LN_EPS = 1e-5
NEG_INF = -1e30
LR, B1, B2, AEPS, WD, STEP = 0.001, 0.9, 0.999, 1e-08, 0.01, 10

TM = 512
FT = 1408
NFT = DFF // FT
TMF = 256
PACK_LANES = 128
GRAD_WIRE = BF16

assert all(w // d == BLK for w, d in BRANCHES)


def _dot(a, b):
    return jnp.dot(a, b, preferred_element_type=F32)


def _dot_nt(a, b):
    return lax.dot_general(a, b, (((1,), (1,)), ((), ())), preferred_element_type=F32)


def _dot_tn(a, b):
    return lax.dot_general(a, b, (((0,), (0,)), ((), ())), preferred_element_type=F32)


def _rowmean(v):
    return jnp.mean(v, axis=-1, keepdims=True)


def _colsum(v):
    return jnp.sum(v, axis=0, keepdims=True)


def _sigmoid(v):
    return jax.nn.sigmoid(v)


_HBM = pl.BlockSpec(memory_space=pltpu.HBM)
_SEM = pl.BlockSpec(memory_space=pltpu.SEMAPHORE)
_EFFECT = pltpu.SideEffectType.DATAFLOW_SIDE_EFFECTING


def _peer_of(k):
    x, y, c = lax.axis_index("x"), lax.axis_index("y"), lax.axis_index("c")
    px = 1 - x if k & 4 else x
    py = 1 - y if k & 2 else y
    pc = 1 - c if k & 1 else c
    return (px, py, pc), 4 * px + 2 * py + pc


def _split_copies(kinds, ins, lands, send_sems, recv_sems, started):
    me = 4 * lax.axis_index("x") + 2 * lax.axis_index("y") + lax.axis_index("c")
    out = []
    for k in range(1, NDEV):
        dev, pid = _peer_of(k)
        for i, kind in enumerate(kinds):
            src = ins[i] if kind == "gather" else ins[i].at[pid]
            dst = lands[i].at[me] if started else lands[i].at[pid]
            slot = i * (NDEV - 1) + k - 1
            out.append(pltpu.make_async_remote_copy(
                src_ref=src, dst_ref=dst, send_sem=send_sems.at[slot], recv_sem=recv_sems.at[slot],
                device_id=dev, device_id_type=pl.DeviceIdType.MESH))
    return out


def _exchange_start(items, name):
    n = len(items)
    kinds = [k for _, k in items]
    srcs = [pltpu.with_memory_space_constraint(a, pltpu.HBM) for a, _ in items]
    lands = []
    for a, k in items:
        shp = (NDEV,) + tuple(a.shape) if k == "gather" else tuple(a.shape)
        lands.append(pltpu.with_memory_space_constraint(lax.empty(shp, a.dtype), pltpu.HBM))

    def body(*refs):
        ins, land_refs = refs[:n], refs[n:2 * n]
        send_sems, recv_sems, own_sems = refs[2 * n:2 * n + 3]
        token = refs[-1]
        for cp in _own_copies(kinds, ins, land_refs, own_sems):
            cp.start()
        for cp in _split_copies(kinds, ins, land_refs, send_sems, recv_sems, True):
            cp.start()
        token[...] = jnp.zeros_like(token)

    sems = pltpu.SemaphoreType.DMA((n * (NDEV - 1),))
    res = pl.pallas_call(
        body, name=name,
        out_shape=(sems, sems, pltpu.SemaphoreType.DMA((n,)),
                   *[pltpu.HBM(a.shape, a.dtype) for a in srcs + lands], SDS((8, 128), F32)),
        in_specs=[_HBM] * (2 * n),
        out_specs=(_SEM, _SEM, _SEM, *[_HBM] * (2 * n), pl.BlockSpec(memory_space=pltpu.VMEM)),
        input_output_aliases={i: 3 + i for i in range(2 * n)},
        compiler_params=pltpu.CompilerParams(has_side_effects=_EFFECT),
    )(*srcs, *lands)
    return (kinds, res[0], res[1], res[2], list(res[3:3 + n]), list(res[3 + n:3 + 2 * n])), res[-1][0, 0]


def _own_copies(kinds, ins, lands, own_sems):
    me = 4 * lax.axis_index("x") + 2 * lax.axis_index("y") + lax.axis_index("c")
    return [pltpu.make_async_copy(ins[i] if kind == "gather" else ins[i].at[me], lands[i].at[me], own_sems.at[i])
            for i, kind in enumerate(kinds)]


def _exchange_wait(state, after, name):
    kinds, send_sems, recv_sems, own_sems, srcs, lands = state
    n = len(kinds)

    def body(*refs):
        ins, land_refs = refs[:n], refs[n:2 * n]
        s_sems, r_sems, o_sems = refs[2 * n:2 * n + 3]
        for cp in _split_copies(kinds, ins, land_refs, s_sems, r_sems, False):
            cp.wait_send()
            cp.wait_recv()
        for cp in _own_copies(kinds, ins, land_refs, o_sems):
            cp.wait()

    res = pl.pallas_call(
        body, name=name,
        out_shape=tuple(pltpu.HBM(a.shape, a.dtype) for a in srcs + lands),
        in_specs=[_HBM] * (2 * n) + [_SEM, _SEM, _SEM, pl.BlockSpec(memory_space=pl.ANY)],
        out_specs=tuple([_HBM] * (2 * n)),
        input_output_aliases={i: i for i in range(2 * n)},
        compiler_params=pltpu.CompilerParams(has_side_effects=_EFFECT),
    )(*srcs, *lands, send_sems, recv_sems, own_sems, after)
    return list(res[n:])


def _proj_in(x2, w_in_t, b_in):
    nq = 3 * AW

    def body(x_ref, w_ref, b_ref, qkv_ref, ag_ref):
        xb = x_ref[...].astype(BF16)
        qkv_ref[...] = (_dot_nt(xb, w_ref[pl.ds(0, nq), :]) + b_ref[:, :nq]).astype(BF16)
        ag_ref[...] = _dot_nt(xb, w_ref[pl.ds(nq, 2 * CW), :]) + b_ref[:, nq:]

    return pl.pallas_call(
        body, name="proj_in", grid=(T // TM,),
        in_specs=[pl.BlockSpec((TM, D), lambda m: (m, 0)), pl.BlockSpec((INW, D), lambda m: (0, 0)),
                  pl.BlockSpec((1, INW), lambda m: (0, 0))],
        out_specs=(pl.BlockSpec((TM, nq), lambda m: (m, 0)), pl.BlockSpec((TM, 2 * CW), lambda m: (m, 0))),
        out_shape=(SDS((T, nq), BF16), SDS((T, 2 * CW), F32)),
    )(x2, w_in_t, b_in)


def _grad_x(pieces, w_in_t, dz1, zero):
    widths = [p.shape[1] for p in pieces]

    def body(*refs):
        p_refs = refs[:len(pieces)]
        w_ref, dz_ref, z_ref, o_ref = refs[len(pieces):]
        acc = ALPHA * dz_ref[...] + z_ref[...]
        r0 = 0
        for p_ref, wd in zip(p_refs, widths):
            acc = acc + _dot(p_ref[...], w_ref[pl.ds(r0, wd), :])
            r0 += wd
        o_ref[...] = acc

    row = pl.BlockSpec((TM, D), lambda m: (m, 0))
    return pl.pallas_call(
        body, name="grad_x", grid=(T // TM,),
        in_specs=[pl.BlockSpec((TM, wd), lambda m: (m, 0)) for wd in widths]
        + [pl.BlockSpec((INW, D), lambda m: (0, 0)), row, pl.BlockSpec((1, 1), lambda m: (0, 0))],
        out_specs=row,
        out_shape=SDS((T, D), F32),
    )(*pieces, w_in_t, dz1, zero)


def _grad_w_in(pieces, x2):
    widths = [p.shape[1] for p in pieces]
    tk = 512
    nk = T // tk

    def body(*refs):
        p_refs = refs[:len(pieces)]
        x_ref, o_ref, acc = refs[len(pieces):]
        k = pl.program_id(0)

        @pl.when(k == 0)
        def _():
            acc[...] = jnp.zeros_like(acc)

        xb = x_ref[...].astype(BF16)
        r0 = 0
        for p_ref, wd in zip(p_refs, widths):
            acc[pl.ds(r0, wd), :] += _dot_tn(p_ref[...], xb)
            r0 += wd

        @pl.when(k == nk - 1)
        def _():
            o_ref[...] = acc[...].astype(o_ref.dtype)

    return pl.pallas_call(
        body, name="grad_w_in", grid=(nk,),
        in_specs=[pl.BlockSpec((tk, wd), lambda k: (k, 0)) for wd in widths] + [pl.BlockSpec((tk, D), lambda k: (k, 0))],
        out_specs=pl.BlockSpec((INW, D), lambda k: (0, 0)),
        out_shape=SDS((INW, D), GRAD_WIRE),
        scratch_shapes=[pltpu.VMEM((INW, D), F32)],
    )(*pieces, x2)


def _mm_tn(a, b, tn, tk, name):
    t_, na = a.shape
    nb = b.shape[1]
    nk = t_ // tk

    def body(a_ref, b_ref, o_ref, acc):
        k = pl.program_id(1)

        @pl.when(k == 0)
        def _():
            acc[...] = jnp.zeros_like(acc)

        acc[...] += _dot_tn(a_ref[...].astype(BF16), b_ref[...].astype(BF16))

        @pl.when(k == nk - 1)
        def _():
            o_ref[...] = acc[...].astype(o_ref.dtype)

    return pl.pallas_call(
        body, name=name, grid=(na // tn, nk),
        in_specs=[pl.BlockSpec((tk, tn), lambda n, k: (k, n)),
                  pl.BlockSpec((tk, nb), lambda n, k: (k, 0))],
        out_specs=pl.BlockSpec((tn, nb), lambda n, k: (n, 0)),
        out_shape=SDS((na, nb), GRAD_WIRE),
        scratch_shapes=[pltpu.VMEM((tn, nb), F32)],
    )(a, b)


def _bucket_maps():
    qi = np.arange(BLK)[:, None]
    kj = np.arange(2 * BLK)[None, :]
    steps = np.maximum(qi + BLK - kj, 0)
    exact = NBUCKET // 2
    maps = []
    for _, dil in BRANCHES:
        dist = steps * dil
        d_f = np.maximum(dist, 1).astype(np.float32)
        large = exact + (np.log(d_f / np.float32(exact)) / np.float32(math.log(S / exact))
                         * np.float32(NBUCKET - exact)).astype(np.int32)
        large = np.minimum(large, NBUCKET - 1)
        maps.append(np.where(dist < exact, dist, large).astype(np.int32))
    return np.stack(maps)


def _bias_table(rel_table, buckets):
    def body(t_ref, b_ref, o_ref):
        bk = b_ref[0]
        for h in range(NH):
            acc = jnp.zeros((BLK, 2 * BLK), F32)
            for k in range(NBUCKET):
                acc = jnp.where(bk == k, t_ref[k, h], acc)
            o_ref[0, h] = acc

    return pl.pallas_call(
        body, name="bias_table", grid=(len(BRANCHES),),
        in_specs=[pl.BlockSpec(memory_space=pltpu.SMEM),
                  pl.BlockSpec((1, BLK, 2 * BLK), lambda i: (i, 0, 0))],
        out_specs=pl.BlockSpec((1, NH, BLK, 2 * BLK), lambda i: (i, 0, 0, 0)),
        out_shape=SDS((len(BRANCHES), NH, BLK, 2 * BLK), F32),
    )(rel_table, buckets)


def _rel_table_grad(dbias, buckets):
    def body(d_ref, b_ref, o_ref):
        h = pl.program_id(0)
        for k in range(NBUCKET):
            tot = jnp.zeros((1, 1), F32)
            for br in range(len(BRANCHES)):
                sel = jnp.where(b_ref[br] == k, d_ref[br, 0], 0.0)
                tot = tot + jnp.sum(jnp.sum(sel, axis=1, keepdims=True), axis=0, keepdims=True)
            o_ref[0, :, pl.ds(k, 1)] = tot

    out = pl.pallas_call(
        body, name="rel_table_grad", grid=(NH,),
        in_specs=[pl.BlockSpec((len(BRANCHES), 1, BLK, 2 * BLK), lambda h: (0, h, 0, 0)),
                  pl.BlockSpec((len(BRANCHES), BLK, 2 * BLK), lambda h: (0, 0, 0))],
        out_specs=pl.BlockSpec((1, 1, NBUCKET), lambda h: (h, 0, 0)),
        out_shape=SDS((NH, 1, NBUCKET), F32),
    )(dbias, buckets)
    return out.reshape(NH, NBUCKET).T


PADK = BLK
SCALE = 1.0 / math.sqrt(HD)
ATTN_UNROLL = 8


def _branch_geometry(br):
    dil = BRANCHES[br][1]
    sub = S // dil
    return dil, sub, sub // BLK


def _token_rows(br, i):
    dil, _, nblk = _branch_geometry(br)
    if dil == 1:
        return pl.ds(pl.multiple_of(i * BLK, BLK), BLK), i
    r = lax.shift_right_logical(i, nblk.bit_length() - 1)
    n = lax.bitwise_and(i, nblk - 1)
    return pl.ds(r + dil * BLK * n, BLK, stride=dil), n


def _sub_layout_loop(br, step):
    dil, sub, _ = _branch_geometry(br)
    rows = min(sub, 256)
    nchunk = sub // rows

    def it_step(it, carry):
        if dil == 1:
            src = pl.ds(pl.multiple_of(it * rows, rows), rows)
        else:
            r = lax.shift_right_logical(it, nchunk.bit_length() - 1)
            src = pl.ds(r + dil * rows * lax.bitwise_and(it, nchunk - 1), rows, stride=dil)
        step(src, pl.multiple_of(it * rows, BLK), rows)
        return carry

    lax.fori_loop(0, dil * nchunk, it_step, 0)


def _masked_bias(bias_ref, bm):
    qi = lax.broadcasted_iota(jnp.int32, (BLK, 2 * BLK), 0)
    kj = lax.broadcasted_iota(jnp.int32, (BLK, 2 * BLK), 1)
    first = jnp.logical_and(kj >= BLK, kj - BLK <= qi)
    valid = jnp.logical_or(first, jnp.logical_and(kj < BLK, kj >= qi))
    for br in range(len(BRANCHES)):
        for j in range(2):
            b = bias_ref[br, j]
            bm[br, 1, pl.ds(j * BLK, BLK), :] = jnp.where(valid, b, NEG_INF)
            bm[br, 0, pl.ds(j * BLK, BLK), :] = jnp.where(first, b, NEG_INF)


def _head_split(fn):
    def split(t):
        h0 = lax.broadcasted_iota(jnp.int32, t.shape, 1) < HD
        t = fn(t)
        return jnp.where(h0, t, 0.0).astype(BF16), jnp.where(h0, 0.0, t).astype(BF16)
    return split


def _attn_fwd(qkv, bias):
    nbr = len(BRANCHES)

    def body(q_ref, k_ref, v_ref, bias_ref, o_ref, lse_ref, qf, kf, vf, qs0, qs1, ks, vs, bm, ob, mb, lb):
        qf[...] = q_ref[...].astype(F32)
        kf[...] = k_ref[...].astype(F32)
        vf[...] = v_ref[...].astype(F32)
        _masked_bias(bias_ref, bm)
        ks[pl.ds(0, PADK), :] = jnp.zeros((PADK, BLK), BF16)
        vs[pl.ds(0, PADK), :] = jnp.zeros((PADK, BLK), BF16)
        head0 = lax.broadcasted_iota(jnp.int32, (BLK, BLK), 1) < HD
        split_q = _head_split(lambda t: t * SCALE)

        for br in range(nbr):
            nblk = _branch_geometry(br)[2]

            def stage(src, off, rows):
                qs0[pl.ds(off, rows), :], qs1[pl.ds(off, rows), :] = split_q(qf[src, :])
                ks[pl.ds(PADK + off, rows), :] = kf[src, :].astype(BF16)
                vs[pl.ds(PADK + off, rows), :] = vf[src, :].astype(BF16)

            _sub_layout_loop(br, stage)

            def blk(i, carry, br=br, nblk=nblk):
                base = pl.multiple_of(i * BLK, BLK)
                rows, n = _token_rows(br, i)
                q01 = jnp.concatenate([qs0[pl.ds(base, BLK), :], qs1[pl.ds(base, BLK), :]], axis=0)
                if nblk > 1:
                    kcat = ks[pl.ds(base, 2 * BLK), :]
                    vcat = vs[pl.ds(base, 2 * BLK), :]
                    s = _dot_nt(q01, kcat) + bm[br, jnp.minimum(n, 1)]
                else:
                    kcat = ks[pl.ds(PADK + base, BLK), :]
                    vcat = vs[pl.ds(PADK + base, BLK), :]
                    s = _dot_nt(q01, kcat) + bm[br, 0, :, BLK:]
                mx = jnp.max(s, axis=-1, keepdims=True)
                p = jnp.exp(s - mx)
                ls = jnp.sum(p, axis=-1, keepdims=True)
                o = _dot(p.astype(BF16), vcat)
                ob[br, rows, :] = jnp.where(head0, o[:BLK], o[BLK:])
                mb[br, rows, :] = jnp.where(head0, mx[:BLK], mx[BLK:])
                lb[br, rows, :] = jnp.where(head0, ls[:BLK], ls[BLK:])
                return carry

            lax.fori_loop(0, 16, blk, 0, unroll=ATTN_UNROLL)

        def merge(i, carry):
            rows = pl.ds(pl.multiple_of(i * 256, 256), 256)
            m_all = jnp.maximum(jnp.maximum(mb[0, rows, :], mb[1, rows, :]), mb[2, rows, :])
            num = jnp.zeros((256, BLK), F32)
            den = jnp.zeros((256, BLK), F32)
            for br in range(nbr):
                c = jnp.exp(mb[br, rows, :] - m_all)
                num = num + ob[br, rows, :] * c
                den = den + lb[br, rows, :] * c
            o_ref[rows, :] = num / den
            lse_ref[rows, :] = m_all + jnp.log(den)
            return carry

        lax.fori_loop(0, S // 256, merge, 0)

    npair = NH // 2
    blk_spec = lambda off: pl.BlockSpec((S, BLK), lambda b, hp: (b, off + hp))
    return pl.pallas_call(
        body, name="attn_fwd", grid=(BL, npair),
        in_specs=[blk_spec(0), blk_spec(npair), blk_spec(2 * npair),
                  pl.BlockSpec((nbr, 2, BLK, 2 * BLK), lambda b, hp: (0, hp, 0, 0))],
        out_specs=(blk_spec(0), blk_spec(0)),
        out_shape=(SDS((T, AW), F32), SDS((T, AW), F32)),
        scratch_shapes=[pltpu.VMEM((S, BLK), F32)] * 3 + [pltpu.VMEM((S, BLK), BF16)] * 2
        + [pltpu.VMEM((PADK + S, BLK), BF16)] * 2 + [pltpu.VMEM((nbr, 2, 2 * BLK, 2 * BLK), F32)]
        + [pltpu.VMEM((nbr, S, BLK), F32)] * 3,
    )(qkv, qkv, qkv, bias)


def _attn_bwd(qkv, attn, lse, dattn, bias):
    nbr = len(BRANCHES)

    def body(q_ref, k_ref, v_ref, o_ref, lse_ref, do_ref, bias_ref,
             dq_ref, dk_ref, dv_ref, sq_ref, sk_ref, sv_ref, db_ref,
             qf, kf, vf, dl, dqa, dka, dva, qs0, qs1, ds0, ds1, ks, vs, dks, dvs, bm):
        b = pl.program_id(1)
        qf[...] = q_ref[...].astype(F32)
        kf[...] = k_ref[...].astype(F32)
        vf[...] = v_ref[...].astype(F32)
        dqa[...] = jnp.zeros_like(dqa)
        dka[...] = jnp.zeros_like(dka)
        dva[...] = jnp.zeros_like(dva)
        _masked_bias(bias_ref, bm)
        ks[pl.ds(0, PADK), :] = jnp.zeros((PADK, BLK), BF16)
        vs[pl.ds(0, PADK), :] = jnp.zeros((PADK, BLK), BF16)
        head0 = lax.broadcasted_iota(jnp.int32, (BLK, BLK), 1) < HD
        split_q = _head_split(lambda t: t * SCALE)
        split_do = _head_split(lambda t: t)

        @pl.when(b == 0)
        def _():
            db_ref[...] = jnp.zeros_like(db_ref)
            sq_ref[...] = jnp.zeros_like(sq_ref)
            sk_ref[...] = jnp.zeros_like(sk_ref)
            sv_ref[...] = jnp.zeros_like(sv_ref)

        def delta(i, carry):
            rows = pl.ds(pl.multiple_of(i * 256, 256), 256)
            prod = do_ref[rows, :] * o_ref[rows, :]
            h0 = lax.broadcasted_iota(jnp.int32, (256, BLK), 1) < HD
            d0 = jnp.sum(jnp.where(h0, prod, 0.0), axis=-1, keepdims=True)
            d1 = jnp.sum(jnp.where(h0, 0.0, prod), axis=-1, keepdims=True)
            dl[rows, :] = jnp.where(h0, d0, d1)
            return carry

        lax.fori_loop(0, S // 256, delta, 0)

        for br in range(nbr):
            nblk = _branch_geometry(br)[2]

            def stage(src, off, rows):
                qs0[pl.ds(off, rows), :], qs1[pl.ds(off, rows), :] = split_q(qf[src, :])
                ds0[pl.ds(off, rows), :], ds1[pl.ds(off, rows), :] = split_do(do_ref[src, :])
                ks[pl.ds(PADK + off, rows), :] = kf[src, :].astype(BF16)
                vs[pl.ds(PADK + off, rows), :] = vf[src, :].astype(BF16)

            _sub_layout_loop(br, stage)
            dks[...] = jnp.zeros_like(dks)
            dvs[...] = jnp.zeros_like(dvs)

            def blk(i, carry, br=br, nblk=nblk):
                base = pl.multiple_of(i * BLK, BLK)
                rows, n = _token_rows(br, i)
                q01 = jnp.concatenate([qs0[pl.ds(base, BLK), :], qs1[pl.ds(base, BLK), :]], axis=0)
                do01 = jnp.concatenate([ds0[pl.ds(base, BLK), :], ds1[pl.ds(base, BLK), :]], axis=0)
                lse_b = lse_ref[rows, :]
                dl_b = dl[rows, :]
                lse01 = jnp.concatenate([lse_b[:, 0:1], lse_b[:, HD:HD + 1]], axis=0)
                dl01 = jnp.concatenate([dl_b[:, 0:1], dl_b[:, HD:HD + 1]], axis=0)
                if nblk > 1:
                    krows = pl.ds(base, 2 * BLK)
                    bias_m = bm[br, jnp.minimum(n, 1)]
                else:
                    krows = pl.ds(PADK + base, BLK)
                    bias_m = bm[br, 0, :, BLK:]
                kcat = ks[krows, :]
                vcat = vs[krows, :]
                p = jnp.exp(_dot_nt(q01, kcat) + bias_m - lse01)
                dsv = p * (_dot_nt(do01, vcat) - dl01)
                if nblk > 1:
                    db_ref[br, 0] += dsv[:BLK]
                    db_ref[br, 1] += dsv[BLK:]
                else:
                    db_ref[br, 0, :, BLK:] += dsv[:BLK]
                    db_ref[br, 1, :, BLK:] += dsv[BLK:]
                dsb = dsv.astype(BF16)
                dq01 = _dot(dsb, kcat)
                dqa[rows, :] = dqa[rows, :] + jnp.where(head0, dq01[:BLK], dq01[BLK:])
                dks[krows, :] = dks[krows, :] + _dot_tn(dsb, q01)
                dvs[krows, :] = dvs[krows, :] + _dot_tn(p.astype(BF16), do01)
                return carry

            lax.fori_loop(0, 16, blk, 0, unroll=ATTN_UNROLL)

            def fold(src, off, rows):
                dka[src, :] = dka[src, :] + dks[pl.ds(PADK + off, rows), :]
                dva[src, :] = dva[src, :] + dvs[pl.ds(PADK + off, rows), :]

            _sub_layout_loop(br, fold)

        def flush(i, carry):
            rows = pl.ds(pl.multiple_of(i * 256, 256), 256)
            for acc, out, cs, mul in ((dqa, dq_ref, sq_ref, SCALE), (dka, dk_ref, sk_ref, 1.0), (dva, dv_ref, sv_ref, 1.0)):
                val = acc[rows, :] * mul
                out[rows, :] = val.astype(BF16)
                cs[...] += _colsum(val)
            return carry

        lax.fori_loop(0, S // 256, flush, 0)

    npair = NH // 2
    blk_spec = lambda off: pl.BlockSpec((S, BLK), lambda hp, b: (b, off + hp))
    sum_spec = pl.BlockSpec((1, BLK), lambda hp, b: (0, hp))
    return pl.pallas_call(
        body, name="attn_bwd", grid=(npair, BL),
        in_specs=[blk_spec(0), blk_spec(npair), blk_spec(2 * npair), blk_spec(0), blk_spec(0), blk_spec(0),
                  pl.BlockSpec((nbr, 2, BLK, 2 * BLK), lambda hp, b: (0, hp, 0, 0))],
        out_specs=(blk_spec(0), blk_spec(0), blk_spec(0), sum_spec, sum_spec, sum_spec,
                   pl.BlockSpec((nbr, 2, BLK, 2 * BLK), lambda hp, b: (0, hp, 0, 0))),
        out_shape=(SDS((T, AW), BF16), SDS((T, AW), BF16), SDS((T, AW), BF16),
                   SDS((1, AW), F32), SDS((1, AW), F32), SDS((1, AW), F32),
                   SDS((nbr, NH, BLK, 2 * BLK), F32)),
        scratch_shapes=[pltpu.VMEM((S, BLK), F32)] * 7 + [pltpu.VMEM((S, BLK), BF16)] * 4
        + [pltpu.VMEM((PADK + S, BLK), BF16)] * 2 + [pltpu.VMEM((PADK + S, BLK), F32)] * 2
        + [pltpu.VMEM((nbr, 2, 2 * BLK, 2 * BLK), F32)],
    )(qkv, qkv, qkv, attn, lse, dattn, bias)


CH = 256
PADR = 32


def _tap_phases(offset_of_tap):
    taps = sorted((offset_of_tap(k) % 8, offset_of_tap(k) - offset_of_tap(k) % 8, k) for k in range(CK))
    assert all(lo + CH + ph <= CH + PADR for ph, lo, _ in taps)
    return taps


def _rows_up(win):
    made = {0: win}

    def get(phase):
        if phase not in made:
            made[phase] = pltpu.roll(win, win.shape[0] - phase, 0)
        return made[phase]
    return get


def _conv_fwd(ag, conv_w, conv_b):
    def body(ag_ref, w_ref, b_ref, u1_ref, u0p):
        u0p[pl.ds(0, PADR), :] = jnp.zeros((PADR, CW), F32)

        def glu(i, carry):
            t0 = pl.multiple_of(i * CH, CH)
            a = ag_ref[pl.ds(t0, CH), :CW]
            g = ag_ref[pl.ds(t0, CH), CW:]
            u0p[pl.ds(PADR + t0, CH), :] = a * _sigmoid(g)
            return carry

        lax.fori_loop(0, S // CH, glu, 0)

        def conv(i, carry):
            t0 = pl.multiple_of(i * CH, CH)
            win = u0p[pl.ds(t0, CH + PADR), :]
            acc = jnp.zeros((CH, CW), F32) + b_ref[...]
            up = _rows_up(win)
            for phase, lo, k in _tap_phases(lambda k: PADR - (CK - 1) + k):
                acc = acc + up(phase)[lo:lo + CH, :] * w_ref[k:k + 1, :]
            u1_ref[pl.ds(t0, CH), :] = acc
            return carry

        lax.fori_loop(0, S // CH, conv, 0)

    return pl.pallas_call(
        body, name="conv_fwd", grid=(BL,),
        in_specs=[pl.BlockSpec((S, 2 * CW), lambda b: (b, 0)),
                  pl.BlockSpec((CK, CW), lambda b: (0, 0)),
                  pl.BlockSpec((1, CW), lambda b: (0, 0))],
        out_specs=pl.BlockSpec((S, CW), lambda b: (b, 0)),
        out_shape=SDS((T, CW), F32),
        scratch_shapes=[pltpu.VMEM((S + PADR, CW), F32)],
    )(ag, conv_w, conv_b)


def _conv_post(u1, cg, cb):
    mu = _rowmean(u1)
    uc = u1 - mu
    rstd = lax.rsqrt(_rowmean(uc * uc) + LN_EPS)
    xh = uc * rstd
    u2 = xh * cg + cb
    sg = _sigmoid(u2)
    return xh, rstd, u2, sg, u2 * sg


def _mix_fwd(attn, u1, ga, gc, cg, cb):
    def body(a_ref, u_ref, ga_ref, gc_ref, cg_ref, cb_ref, o_ref):
        a = a_ref[...]
        ra = lax.rsqrt(_rowmean(a * a) + LN_EPS)
        o_ref[:, :AW] = (a * ra * ga_ref[...]).astype(BF16)
        _, _, _, _, u3 = _conv_post(u_ref[...], cg_ref[...], cb_ref[...])
        rc = lax.rsqrt(_rowmean(u3 * u3) + LN_EPS)
        o_ref[:, AW:] = (u3 * rc * gc_ref[...]).astype(BF16)

    vec = lambda w: pl.BlockSpec((1, w), lambda m: (0, 0))
    return pl.pallas_call(
        body, name="mix_fwd", grid=(T // TM,),
        in_specs=[pl.BlockSpec((TM, AW), lambda m: (m, 0)), pl.BlockSpec((TM, CW), lambda m: (m, 0)),
                  vec(AW), vec(CW), vec(CW), vec(CW)],
        out_specs=pl.BlockSpec((TM, D), lambda m: (m, 0)),
        out_shape=SDS((T, D), BF16),
    )(attn, u1, ga, gc, cg, cb)


def _mix_bwd(dz1, w_out, attn, u1, ga, gc, cg, cb):
    def body(dz_ref, w_ref, a_ref, u_ref, ga_ref, gc_ref, cg_ref, cb_ref,
             da_ref, du_ref, g_an, g_cn, g_lg, g_lb, g_cb):
        @pl.when(pl.program_id(0) == 0)
        def _():
            for r in (g_an, g_cn, g_lg, g_lb, g_cb):
                r[...] = jnp.zeros_like(r)

        dm = _dot_nt(dz_ref[...].astype(BF16), w_ref[...])
        a = a_ref[...]
        dna = dm[:, :AW]
        ra = lax.rsqrt(_rowmean(a * a) + LN_EPS)
        g_an[...] += _colsum(dna * a * ra)
        dat = dna * ga_ref[...]
        da_ref[...] = ra * dat - a * (ra * ra * ra) * _rowmean(dat * a)

        xh, rstd, u2, sg, u3 = _conv_post(u_ref[...], cg_ref[...], cb_ref[...])
        dnc = dm[:, AW:]
        rc = lax.rsqrt(_rowmean(u3 * u3) + LN_EPS)
        g_cn[...] += _colsum(dnc * u3 * rc)
        dut = dnc * gc_ref[...]
        du3 = rc * dut - u3 * (rc * rc * rc) * _rowmean(dut * u3)
        du2 = du3 * sg * (1.0 + u2 * (1.0 - sg))
        g_lg[...] += _colsum(du2 * xh)
        g_lb[...] += _colsum(du2)
        dxh = du2 * cg_ref[...]
        du1 = rstd * (dxh - _rowmean(dxh) - xh * _rowmean(dxh * xh))
        g_cb[...] += _colsum(du1)
        du_ref[...] = du1

    vec = lambda w: pl.BlockSpec((1, w), lambda m: (0, 0))
    return pl.pallas_call(
        body, name="mix_bwd", grid=(T // TM,),
        in_specs=[pl.BlockSpec((TM, D), lambda m: (m, 0)), pl.BlockSpec((D, D), lambda m: (0, 0)),
                  pl.BlockSpec((TM, AW), lambda m: (m, 0)),
                  pl.BlockSpec((TM, CW), lambda m: (m, 0)), vec(AW), vec(CW), vec(CW), vec(CW)],
        out_specs=(pl.BlockSpec((TM, AW), lambda m: (m, 0)), pl.BlockSpec((TM, CW), lambda m: (m, 0)),
                   vec(AW), vec(CW), vec(CW), vec(CW), vec(CW)),
        out_shape=(SDS((T, AW), F32), SDS((T, CW), F32),
                   SDS((1, AW), F32), SDS((1, CW), F32), SDS((1, CW), F32), SDS((1, CW), F32), SDS((1, CW), F32)),
    )(dz1, w_out, attn, u1, ga, gc, cg, cb)


def _conv_bwd(du1, ag, conv_w):
    def body(du_ref, ag_ref, w_ref, dag_ref, cs_ref, gw_ref, u0p, dup):
        @pl.when(pl.program_id(0) == 0)
        def _():
            cs_ref[...] = jnp.zeros_like(cs_ref)
            gw_ref[...] = jnp.zeros_like(gw_ref)

        u0p[pl.ds(0, PADR), :] = jnp.zeros((PADR, CW), F32)
        dup[pl.ds(S, PADR), :] = jnp.zeros((PADR, CW), F32)

        def fill(i, carry):
            t0 = pl.multiple_of(i * CH, CH)
            a = ag_ref[pl.ds(t0, CH), :CW]
            g = ag_ref[pl.ds(t0, CH), CW:]
            u0p[pl.ds(PADR + t0, CH), :] = a * _sigmoid(g)
            dup[pl.ds(t0, CH), :] = du_ref[pl.ds(t0, CH), :]
            return carry

        lax.fori_loop(0, S // CH, fill, 0)

        def chunk(i, carry):
            t0 = pl.multiple_of(i * CH, CH)
            d = dup[pl.ds(t0, CH), :]
            win_u = u0p[pl.ds(t0, CH + PADR), :]
            win_d = dup[pl.ds(t0, CH + PADR), :]
            du0 = jnp.zeros((CH, CW), F32)
            up_u = _rows_up(win_u)
            for phase, lo, k in _tap_phases(lambda k: PADR - (CK - 1) + k):
                gw_ref[k:k + 1, :] += _colsum(d * up_u(phase)[lo:lo + CH, :])
            up_d = _rows_up(win_d)
            for phase, lo, k in _tap_phases(lambda k: CK - 1 - k):
                du0 = du0 + up_d(phase)[lo:lo + CH, :] * w_ref[k:k + 1, :]
            a = ag_ref[pl.ds(t0, CH), :CW]
            sg = _sigmoid(ag_ref[pl.ds(t0, CH), CW:])
            da = du0 * sg
            dg = du0 * a * sg * (1.0 - sg)
            dag_ref[pl.ds(t0, CH), :CW] = da.astype(BF16)
            dag_ref[pl.ds(t0, CH), CW:] = dg.astype(BF16)
            cs_ref[:, :CW] += _colsum(da)
            cs_ref[:, CW:] += _colsum(dg)
            return carry

        lax.fori_loop(0, S // CH, chunk, 0)

    return pl.pallas_call(
        body, name="conv_bwd", grid=(BL,),
        in_specs=[pl.BlockSpec((S, CW), lambda b: (b, 0)), pl.BlockSpec((S, 2 * CW), lambda b: (b, 0)),
                  pl.BlockSpec((CK, CW), lambda b: (0, 0))],
        out_specs=(pl.BlockSpec((S, 2 * CW), lambda b: (b, 0)),
                   pl.BlockSpec((1, 2 * CW), lambda b: (0, 0)),
                   pl.BlockSpec((PADR, CW), lambda b: (0, 0))),
        out_shape=(SDS((T, 2 * CW), BF16), SDS((1, 2 * CW), F32), SDS((PADR, CW), F32)),
        scratch_shapes=[pltpu.VMEM((S + PADR, CW), F32), pltpu.VMEM((S + PADR, CW), F32)],
    )(du1, ag, conv_w)


def _layer_norm_fwd(z):
    mu = _rowmean(z)
    zc = z - mu
    rstd = lax.rsqrt(_rowmean(zc * zc) + LN_EPS)
    return zc * rstd, rstd


def _layer_norm_bwd(dy, xh, rstd, g):
    dxh = dy * g
    return rstd * (dxh - _rowmean(dxh) - xh * _rowmean(dxh * xh))


def _out_proj_ln1(mixed, w_out, x2, g1, b1):
    def body(a_ref, w_ref, x_ref, g_ref, b_ref, xh_ref, rstd_ref, x1_ref):
        z = ALPHA * x_ref[...] + _dot(a_ref[...], w_ref[...])
        xh, rstd = _layer_norm_fwd(z)
        xh_ref[...] = xh
        rstd_ref[...] = rstd
        x1_ref[...] = (xh * g_ref[...] + b_ref[...]).astype(BF16)

    vec = pl.BlockSpec((1, D), lambda m: (0, 0))
    row = pl.BlockSpec((TM, D), lambda m: (m, 0))
    return pl.pallas_call(
        body, name="out_proj_ln1", grid=(T // TM,),
        in_specs=[row, pl.BlockSpec((D, D), lambda m: (0, 0)), row, vec, vec],
        out_specs=(row, pl.BlockSpec((TM, 1), lambda m: (m, 0)), row),
        out_shape=(SDS((T, D), F32), SDS((T, 1), F32), SDS((T, D), BF16)),
    )(mixed, w_out, x2, g1, b1)


def _seq_start(m):
    return lax.bitwise_and(m, S // TMF - 1) == 0


def _shift_down(x, before, k):
    rolled = pltpu.roll(x, k, 0)
    row = lax.broadcasted_iota(jnp.int32, before.shape, 0)
    head = jnp.where(row < k, pltpu.roll(before, k, 0), rolled[:8])
    return jnp.concatenate([head, rolled[8:]], axis=0)


def _shift_up(x, after, k):
    n = x.shape[0]
    rolled = pltpu.roll(x, n - k, 0)
    row = lax.broadcasted_iota(jnp.int32, after.shape, 0)
    tail = jnp.where(row >= 8 - k, pltpu.roll(after, 8 - k, 0), rolled[n - 8:])
    return jnp.concatenate([rolled[:n - 8], tail], axis=0)


def _ffn_up(x1b, w_up, fcw, fcb):
    def body(x_ref, wg_ref, wv_ref, cwg_ref, cwv_ref, cbg_ref, cbv_ref, up_ref, gv_ref, act_ref, prev_g, prev_v):
        @pl.when(_seq_start(pl.program_id(1)))
        def _():
            prev_g[...] = jnp.zeros_like(prev_g)
            prev_v[...] = jnp.zeros_like(prev_v)

        x = x_ref[...]
        outs = []
        for w_ref, cw_ref, cb_ref, prev, lo in ((wg_ref, cwg_ref, cbg_ref, prev_g, 0), (wv_ref, cwv_ref, cbv_ref, prev_v, FT)):
            ub = _dot_nt(x, w_ref[...]).astype(BF16)
            up_ref[:, lo:lo + FT] = ub
            u = ub.astype(F32)
            before = prev[...]
            y = (cw_ref[2:3, :] * u + cw_ref[1:2, :] * _shift_down(u, before, 1)
                 + cw_ref[0:1, :] * _shift_down(u, before, 2) + cb_ref[...])
            prev[...] = u[TMF - 8:]
            yb = y.astype(BF16)
            gv_ref[:, lo:lo + FT] = yb
            outs.append(yb.astype(F32))
        gate, val = outs
        act_ref[...] = (gate * _sigmoid(gate) * val).astype(BF16)

    wspec = lambda off: pl.BlockSpec((FT, D), lambda n, m: (n + off, 0))
    cwspec = lambda off: pl.BlockSpec((FK, FT), lambda n, m: (0, n + off))
    cbspec = lambda off: pl.BlockSpec((1, FT), lambda n, m: (0, n + off))
    pair = pl.BlockSpec((TMF, 2 * FT), lambda n, m: (m, n))
    return pl.pallas_call(
        body, name="ffn_up", grid=(NFT, T // TMF),
        in_specs=[pl.BlockSpec((TMF, D), lambda n, m: (m, 0)), wspec(0), wspec(NFT),
                  cwspec(0), cwspec(NFT), cbspec(0), cbspec(NFT)],
        out_specs=(pair, pair, pl.BlockSpec((TMF, FT), lambda n, m: (m, n))),
        out_shape=(SDS((T, 2 * DFF), BF16), SDS((T, 2 * DFF), BF16), SDS((T, DFF), BF16)),
        scratch_shapes=[pltpu.VMEM((8, FT), F32)] * 2,
    )(x1b, w_up, w_up, fcw, fcw, fcb, fcb)


def _ffn_down_loss(act, w_down, xh1, g1, b1, g2, b2, target):
    def body(a_ref, w_ref, xh1_ref, g1_ref, b1_ref, g2_ref, b2_ref, t_ref, dz_ref, loss_ref, gg_ref, gb_ref):
        @pl.when(pl.program_id(0) == 0)
        def _():
            loss_ref[...] = jnp.zeros_like(loss_ref)
            gg_ref[...] = jnp.zeros_like(gg_ref)
            gb_ref[...] = jnp.zeros_like(gb_ref)

        x1 = xh1_ref[...] * g1_ref[...] + b1_ref[...]
        z = ALPHA * x1 + _dot(a_ref[...], w_ref[...])
        xh, rstd = _layer_norm_fwd(z)
        diff = xh * g2_ref[...] + b2_ref[...] - t_ref[...]
        loss_ref[...] += 0.5 * _colsum(_rowmean(diff * diff))
        dout = diff * (1.0 / D)
        gg_ref[...] += _colsum(dout * xh)
        gb_ref[...] += _colsum(dout)
        dz_ref[...] = _layer_norm_bwd(dout, xh, rstd, g2_ref[...])

    vec = pl.BlockSpec((1, D), lambda m: (0, 0))
    row = pl.BlockSpec((TM, D), lambda m: (m, 0))
    return pl.pallas_call(
        body, name="ffn_down_loss", grid=(T // TM,),
        in_specs=[pl.BlockSpec((TM, DFF), lambda m: (m, 0)), pl.BlockSpec((DFF, D), lambda m: (0, 0)),
                  row, vec, vec, vec, vec, row],
        out_specs=(row, pl.BlockSpec((1, 1), lambda m: (0, 0)), vec, vec),
        out_shape=(SDS((T, D), F32), SDS((1, 1), F32), SDS((1, D), F32), SDS((1, D), F32)),
    )(act, w_down, xh1, g1, b1, g2, b2, target)


def _ffn_down_bwd(dz2, w_down, gv):
    def body(dz_ref, wd_ref, gv_ref, dup_ref, csg_ref, csv_ref):
        @pl.when(pl.program_id(1) == 0)
        def _():
            csg_ref[...] = jnp.zeros_like(csg_ref)
            csv_ref[...] = jnp.zeros_like(csv_ref)

        dact = _dot_nt(dz_ref[...].astype(BF16), wd_ref[...])
        gate = gv_ref[:, :FT].astype(F32)
        val = gv_ref[:, FT:].astype(F32)
        sg = _sigmoid(gate)
        gs = gate * sg
        dgate = dact * val * (sg + gs * (1.0 - sg))
        dval = dact * gs
        dup_ref[:, :FT] = dgate.astype(BF16)
        dup_ref[:, FT:] = dval.astype(BF16)
        csg_ref[...] += _colsum(dgate)
        csv_ref[...] += _colsum(dval)

    cs = pl.BlockSpec((1, FT), lambda n, m: (0, n))
    pair = pl.BlockSpec((TMF, 2 * FT), lambda n, m: (m, n))
    return pl.pallas_call(
        body, name="ffn_down_bwd", grid=(NFT, T // TMF),
        in_specs=[pl.BlockSpec((TMF, D), lambda n, m: (m, 0)), pl.BlockSpec((FT, D), lambda n, m: (n, 0)), pair],
        out_specs=(pair, cs, cs),
        out_shape=(SDS((T, 2 * DFF), BF16), SDS((1, DFF), F32), SDS((1, DFF), F32)),
    )(dz2, w_down, gv)


HALO = 16


def _conv3_transpose(dup, up, fcw_il):
    tiles = T // TMF

    def body(d_ref, h_ref, u_ref, w_ref, o_ref, gw_ref):
        m = pl.program_id(1)

        @pl.when(m == 0)
        def _():
            gw_ref[...] = jnp.zeros_like(gw_ref)

        d0 = d_ref[...].astype(F32)
        last = lax.bitwise_and(m + 1, S // TMF - 1) == 0
        after = jnp.where(last, 0.0, h_ref[...].astype(F32)[:8])
        d1 = _shift_up(d0, after, 1)
        d2 = _shift_up(d0, after, 2)
        o_ref[...] = (w_ref[2:3, :] * d0 + w_ref[1:2, :] * d1 + w_ref[0:1, :] * d2).astype(BF16)
        u = u_ref[...].astype(F32)
        for k, dk in enumerate((d2, d1, d0)):
            gw_ref[k:k + 1, :] += _colsum(dk * u)

    pair = pl.BlockSpec((TMF, 2 * FT), lambda n, m: (m, n))
    return pl.pallas_call(
        body, name="conv3_transpose", grid=(NFT, tiles),
        in_specs=[pair,
                  pl.BlockSpec((HALO, 2 * FT), lambda n, m: (jnp.minimum((m + 1) * (TMF // HALO), T // HALO - 1), n)),
                  pair, pl.BlockSpec((FK, 2 * FT), lambda n, m: (0, n))],
        out_specs=(pair, pl.BlockSpec((FK, 2 * FT), lambda n, m: (0, n))),
        out_shape=(SDS((T, 2 * DFF), BF16), SDS((FK, 2 * DFF), F32)),
    )(dup, dup, up, fcw_il)


def _ffn_up_bwd_ln1(dpre, w_up, dz2, xh1, rstd1, g1):
    def body(a_ref, w_ref, dz2_ref, xh_ref, rstd_ref, g_ref, dz1_ref, gg_ref, gb_ref):
        @pl.when(pl.program_id(0) == 0)
        def _():
            gg_ref[...] = jnp.zeros_like(gg_ref)
            gb_ref[...] = jnp.zeros_like(gb_ref)

        dx1 = ALPHA * dz2_ref[...]
        for n in range(NFT):
            for half in range(2):
                a = a_ref[:, (2 * n + half) * FT:(2 * n + half + 1) * FT]
                w = w_ref[pl.ds((half * NFT + n) * FT, FT), :]
                dx1 = dx1 + _dot(a, w)
        xh = xh_ref[...]
        gg_ref[...] += _colsum(dx1 * xh)
        gb_ref[...] += _colsum(dx1)
        dz1_ref[...] = _layer_norm_bwd(dx1, xh, rstd_ref[...], g_ref[...])

    vec = pl.BlockSpec((1, D), lambda m: (0, 0))
    row = pl.BlockSpec((TMF, D), lambda m: (m, 0))
    return pl.pallas_call(
        body, name="ffn_up_bwd_ln1", grid=(T // TMF,),
        in_specs=[pl.BlockSpec((TMF, 2 * DFF), lambda m: (m, 0)), pl.BlockSpec((2 * DFF, D), lambda m: (0, 0)),
                  row, row, pl.BlockSpec((TMF, 1), lambda m: (m, 0)), vec],
        out_specs=(row, vec, vec),
        out_shape=(SDS((T, D), F32), SDS((1, D), F32), SDS((1, D), F32)),
    )(dpre, w_up, dz2, xh1, rstd1, g1)


def _grad_w_up(dpre, x1b):
    tk = 1024

    def body(a_ref, b_ref, o_ref, acc):
        k = pl.program_id(1)

        @pl.when(k == 0)
        def _():
            acc[...] = jnp.zeros_like(acc)

        acc[...] += _dot_tn(a_ref[...], b_ref[...])

        @pl.when(k == T // tk - 1)
        def _():
            o_ref[0] = acc[pl.ds(0, FT), :].astype(o_ref.dtype)
            o_ref[1] = acc[pl.ds(FT, FT), :].astype(o_ref.dtype)

    out = pl.pallas_call(
        body, name="grad_w_up", grid=(NFT, T // tk),
        in_specs=[pl.BlockSpec((tk, 2 * FT), lambda n, k: (k, n)), pl.BlockSpec((tk, D), lambda n, k: (k, 0))],
        out_specs=pl.BlockSpec((2, FT, D), lambda n, k: (0, n, 0)),
        out_shape=SDS((2, DFF, D), GRAD_WIRE),
        scratch_shapes=[pltpu.VMEM((2 * FT, D), F32)],
    )(dpre, x1b)
    return out.reshape(2 * DFF, D)


def _row_tile(rows, cols):
    if rows * cols * 4 <= (1 << 20) or rows % 8:
        return rows
    for t in (256, 176, 128, 88, 64, 32, 16, 8):
        if rows % t == 0 and t * cols * 4 <= (1 << 20):
            return t
    return 8


def _sum8(r, name):
    _, rows, cols = r.shape
    tr = _row_tile(rows, cols)

    def body(r_ref, o_ref):
        acc = r_ref[0].astype(F32)
        for p in range(1, NDEV):
            acc = acc + r_ref[p].astype(F32)
        o_ref[...] = acc

    return pl.pallas_call(
        body, name=name, grid=(rows // tr,),
        in_specs=[pl.BlockSpec((NDEV, tr, cols), lambda i: (0, i, 0))],
        out_specs=pl.BlockSpec((tr, cols), lambda i: (i, 0)),
        out_shape=SDS((rows, cols), F32),
    )(r)


def _adamw(w, g, m, v, name):
    rows, cols = w.shape
    tr = _row_tile(rows, cols)

    def body(w_ref, g_ref, m_ref, v_ref, d_ref, nm_ref, nv_ref):
        g_ = g_ref[...]
        m_ = B1 * m_ref[...] + (1.0 - B1) * g_
        v_ = B2 * v_ref[...] + (1.0 - B2) * jnp.square(g_)
        m_hat = m_ / (1.0 - B1 ** STEP)
        v_hat = v_ / (1.0 - B2 ** STEP)
        d_ref[...] = -LR * (m_hat / (jnp.sqrt(v_hat) + AEPS) + WD * w_ref[...])
        nm_ref[...] = m_
        nv_ref[...] = v_

    spec = pl.BlockSpec((tr, cols), lambda i: (i, 0))
    shp = SDS((rows, cols), F32)
    return pl.pallas_call(
        body, name=name, grid=(rows // tr,), in_specs=[spec] * 4, out_specs=(spec,) * 3,
        out_shape=(shp, shp, shp),
    )(w, g, m, v)


def _adamw_many(ws, gs, ms, vs, name):
    n = len(ws)

    def body(*refs):
        for i in range(n):
            w_ref, g_ref, m_ref, v_ref, d_ref, nm_ref, nv_ref = refs[i::n]
            g_ = g_ref[...]
            m_ = B1 * m_ref[...] + (1.0 - B1) * g_
            v_ = B2 * v_ref[...] + (1.0 - B2) * jnp.square(g_)
            m_hat = m_ / (1.0 - B1 ** STEP)
            v_hat = v_ / (1.0 - B2 ** STEP)
            d_ref[...] = -LR * (m_hat / (jnp.sqrt(v_hat) + AEPS) + WD * w_ref[...])
            nm_ref[...] = m_
            nv_ref[...] = v_

    shapes = tuple(SDS(w.shape, F32) for w in ws)
    res = pl.pallas_call(body, name=name, out_shape=shapes * 3)(*ws, *gs, *ms, *vs)
    return res[:n], res[n:2 * n], res[2 * n:]


def _interleave(a):
    r = a.shape[0]
    return a.reshape(r, 2, NFT, FT).transpose(0, 2, 1, 3).reshape(r, 2 * DFF)


def _deinterleave(a):
    r = a.shape[0]
    return a.reshape(r, NFT, 2, FT).transpose(0, 2, 1, 3).reshape(r, 2 * DFF)


def _local_step(x2, target, rel_table, first_weights, b_in, conv_b, conv_ln_g, conv_ln_b, attn_norm_g,
                conv_norm_g, late_weights, ln1_g, ln1_b, ffn_conv_b, ln2_g, ln2_b, ship_ffn_grads, ship_tail):
    buckets = jnp.asarray(_bucket_maps())
    bias = _bias_table(rel_table, buckets)
    w_in_t, conv_w, ffn_conv_w = first_weights(bias)

    qkv, ag = _proj_in(x2, w_in_t, b_in)
    attn, lse = _attn_fwd(qkv, bias)
    u1 = _conv_fwd(ag, conv_w, conv_b)
    mixed = _mix_fwd(attn, u1, attn_norm_g, conv_norm_g, conv_ln_g, conv_ln_b)
    w_out, w_up, w_down = late_weights(mixed)
    xh1, rstd1, x1b = _out_proj_ln1(mixed, w_out, x2, ln1_g, ln1_b)
    up, gv, act = _ffn_up(x1b, w_up, ffn_conv_w, ffn_conv_b)
    dz2, loss, g_ln2_g, g_ln2_b = _ffn_down_loss(act, w_down, xh1, ln1_g, ln1_b, ln2_g, ln2_b, target)

    dup, cs_g, cs_v = _ffn_down_bwd(dz2, w_down, gv)
    g_w_down = _mm_tn(act, dz2, DFF // 2, 512, "grad_w_down")
    dpre, gfw_il = _conv3_transpose(dup, up, _interleave(ffn_conv_w))
    dz1, g_ln1_g, g_ln1_b = _ffn_up_bwd_ln1(dpre, w_up, dz2, xh1, rstd1, ln1_g)
    g_w_out = _mm_tn(mixed, dz1, D, 512, "grad_w_out")
    zero = ship_ffn_grads(g_w_down, _grad_w_up(dpre, x1b), g_w_out)
    dattn, du1, g_an, g_cn, g_clg, g_clb, g_cb = _mix_bwd(
        dz1, w_out, attn, u1, attn_norm_g + zero, conv_norm_g, conv_ln_g, conv_ln_b)
    dag, cs_ag, g_conv_w = _conv_bwd(du1, ag, conv_w)
    dq, dk, dv, cs_q, cs_k, cs_v2, dbias = _attn_bwd(qkv, attn, lse, dattn, bias)
    g_rel = _rel_table_grad(dbias, buckets)
    pieces = [dq, dk, dv, dag]
    g_w_in_t = _grad_w_in(pieces, x2)

    grads = dict(
        rel_table=g_rel,
        b_in=jnp.concatenate([cs_q, cs_k, cs_v2, cs_ag], axis=1),
        conv_b=g_cb, conv_ln_g=g_clg, conv_ln_b=g_clb, attn_norm_g=g_an, conv_norm_g=g_cn,
        ln1_g=g_ln1_g, ln1_b=g_ln1_b,
        ffn_conv_b=jnp.concatenate([cs_g, cs_v], axis=1),
        ln2_g=g_ln2_g, ln2_b=g_ln2_b,
        conv_w=g_conv_w[:CK],
        ffn_conv_w=_deinterleave(gfw_il),
    )
    grads["loss"] = loss
    zero11 = ship_tail(g_w_in_t, grads)
    grad_x = _grad_x(pieces, w_in_t, dz1, zero11)
    return loss, grad_x


SMALL = (("rel_table", (NBUCKET, NH)), ("b_in", (1, INW)), ("conv_b", (1, CW)), ("conv_ln_g", (1, CW)),
         ("conv_ln_b", (1, CW)), ("attn_norm_g", (1, AW)), ("conv_norm_g", (1, CW)), ("ln1_g", (1, D)),
         ("ln1_b", (1, D)), ("ffn_conv_b", (1, 2 * DFF)), ("ln2_g", (1, D)), ("ln2_b", (1, D)))
SHARDED_SMALL = (("conv_w", (CK, CW)), ("ffn_conv_w", (FK, 2 * DFF)))


def _pack(parts):
    flat = jnp.concatenate([p.reshape(-1) for p in parts])
    tile = 8 * PACK_LANES
    pad = (-flat.shape[0]) % tile
    return jnp.pad(flat, (0, pad)).reshape(-1, PACK_LANES)


def _unpack(packed, specs):
    flat = packed.reshape(-1)
    out, off = {}, 0
    for name, shp in specs:
        size = int(np.prod(shp))
        out[name] = flat[off:off + size].reshape(shp)
        off += size
    return out


def kernel(x, rel_table, w_in, b_in, conv_w, conv_b, conv_ln_g, conv_ln_b, attn_norm_g, conv_norm_g, w_out, ln1_g, ln1_b, w_up, ffn_conv_w, ffn_conv_b, w_down, ln2_g, ln2_b, loss_target, m_rel_table, m_w_in, m_b_in, m_conv_w, m_conv_b, m_conv_ln_g, m_conv_ln_b, m_attn_norm_g, m_conv_norm_g, m_w_out, m_ln1_g, m_ln1_b, m_w_up, m_ffn_conv_w, m_ffn_conv_b, m_w_down, m_ln2_g, m_ln2_b, v_rel_table, v_w_in, v_b_in, v_conv_w, v_conv_b, v_conv_ln_g, v_conv_ln_b, v_attn_norm_g, v_conv_norm_g, v_w_out, v_ln1_g, v_ln1_b, v_w_up, v_ffn_conv_w, v_ffn_conv_b, v_w_down, v_ln2_g, v_ln2_b):
    given = dict(locals())
    me = 4 * lax.axis_index("x") + 2 * lax.axis_index("y") + lax.axis_index("c")

    cols = lambda a: a.transpose(1, 0, 2).reshape(a.shape[1], NDEV * a.shape[2])
    rows = lambda a: a.reshape(NDEV * a.shape[1], a.shape[2])
    stack = lambda a: a.reshape(NDEV, a.shape[0] // NDEV, a.shape[1])

    small_specs = SMALL + SHARDED_SMALL
    packed_specs = small_specs + (("loss", (1, 1)),)
    grad, delta, new_m, new_v = {}, {}, {}, {}

    def adamw_big(n, g2d):
        shp = given[n].shape
        two = lambda a: a.reshape(shp[-2], shp[-1])
        grad[n] = g2d.reshape(shp)
        d_, m_, v_ = _adamw(two(given[n]), g2d, two(given["m_" + n]), two(given["v_" + n]), "adamw_" + n)
        delta[n], new_m[n], new_v[n] = d_.reshape(shp), m_.reshape(shp), v_.reshape(shp)
        return d_

    first_state, zero0 = _exchange_start(
        [(w_in[0].T.astype(BF16), "gather"), (conv_w[0], "gather"), (ffn_conv_w[0], "gather")], "gather_first_start")

    def first_weights(after):
        lands = _exchange_wait(first_state, after, "gather_first_wait")
        return rows(lands[0]), cols(lands[1]), cols(lands[2])

    late_state, zero1 = _exchange_start(
        [(w_out[0].astype(BF16), "gather"), (w_up[0].T.astype(BF16) + zero0.astype(BF16), "gather"),
         (w_down[0].astype(BF16), "gather")], "gather_late_start")

    def late_weights(after):
        return [rows(l) for l in _exchange_wait(late_state, after, "gather_late_wait")]

    shipped = {}

    def ship_ffn_grads(g_w_down, g_w_up_t, g_w_out):
        shipped["ffn"], zero2 = _exchange_start(
            [(stack(a), "scatter") for a in (g_w_down, g_w_up_t, g_w_out)], "ffn_grads_start")
        return zero2

    def ship_tail(g_w_in_t, small_grads):
        shipped["tail"], zero3 = _exchange_start(
            [(stack(g_w_in_t), "scatter"), (_pack([small_grads[n] for n, _ in packed_specs]), "gather")],
            "tail_grads_start")
        return zero3.reshape(1, 1)

    loss, grad_x = _local_step(
        x.reshape(T, D), loss_target.reshape(T, D), rel_table + zero1, first_weights, b_in, conv_b, conv_ln_g,
        conv_ln_b, attn_norm_g, conv_norm_g, late_weights, ln1_g, ln1_b, ffn_conv_b,
        ln2_g, ln2_b, ship_ffn_grads, ship_tail)

    got_down, got_up, got_out = _exchange_wait(shipped["ffn"], grad_x, "ffn_grads_wait")
    adamw_big("w_down", _sum8(got_down, "sum_w_down"))
    adamw_big("w_up", _sum8(got_up, "sum_w_up").T)
    last = adamw_big("w_out", _sum8(got_out, "sum_w_out"))

    got_in, got_small = _exchange_wait(shipped["tail"], last, "tail_grads_wait")
    adamw_big("w_in", _sum8(got_in, "sum_w_in").T)
    small = _unpack(_sum8(got_small, "sum_small"), packed_specs)
    small["conv_w"] = lax.dynamic_slice_in_dim(small["conv_w"], me * (CW // NDEV), CW // NDEV, axis=1)
    small["ffn_conv_w"] = lax.dynamic_slice_in_dim(small["ffn_conv_w"], me * (2 * DFF // NDEV), 2 * DFF // NDEV, axis=1)
    names = [n for n, _ in small_specs]
    two = lambda a: a.reshape(a.shape[-2], a.shape[-1])
    ds, nms, nvs = _adamw_many([two(given[n]) for n in names], [small[n] for n in names],
                               [two(given["m_" + n]) for n in names], [two(given["v_" + n]) for n in names], "adamw_small")
    for n, d_, m_, v_ in zip(names, ds, nms, nvs):
        shp = given[n].shape
        grad[n], delta[n], new_m[n], new_v[n] = small[n].reshape(shp), d_.reshape(shp), m_.reshape(shp), v_.reshape(shp)

    order = ("rel_table", "w_in", "b_in", "conv_w", "conv_b", "conv_ln_g", "conv_ln_b", "attn_norm_g",
             "conv_norm_g", "w_out", "ln1_g", "ln1_b", "w_up", "ffn_conv_w", "ffn_conv_b", "w_down", "ln2_g", "ln2_b")
    return (small["loss"][0, 0], grad_x.reshape(BL, S, D), *[grad[n] for n in order], *[delta[n] for n in order],
            *[new_m[n] for n in order], *[new_v[n] for n in order])
```

```python
import math

import numpy as np
import jax
import jax.numpy as jnp
from jax import lax
from jax.experimental import pallas as pl
from jax.experimental.pallas import tpu as pltpu

F32 = jnp.float32
BF16 = jnp.bfloat16
SDS = jax.ShapeDtypeStruct

NDEV = 8
D = 1024
S = 2048
BL = 2
T = BL * S
NH = 12
HD = 64
AW = NH * HD
CW = D - AW
INW = 3 * AW + 2 * CW
CK = 31
DFF = 2816
FK = 3
BLK = 128
NBUCKET = 32
BRANCHES = ((128, 1), (512, 4), (2048, 16))
ALPHA = 2.0 ** 0.25
LN_EPS = 1e-5
NEG_INF = -1e30
LR, B1, B2, AEPS, WD, STEP = 0.001, 0.9, 0.999, 1e-08, 0.01, 10

TM = 512
FT = 1408
NFT = DFF // FT
TMF = 256
PACK_LANES = 128
GRAD_WIRE = BF16

assert all(w // d == BLK for w, d in BRANCHES)


def _dot(a, b):
    return jnp.dot(a, b, preferred_element_type=F32)


def _dot_nt(a, b):
    return lax.dot_general(a, b, (((1,), (1,)), ((), ())), preferred_element_type=F32)


def _dot_tn(a, b):
    return lax.dot_general(a, b, (((0,), (0,)), ((), ())), preferred_element_type=F32)


def _rowmean(v):
    return jnp.mean(v, axis=-1, keepdims=True)


def _colsum(v):
    return jnp.sum(v, axis=0, keepdims=True)


def _sigmoid(v):
    return jax.nn.sigmoid(v)


_HBM = pl.BlockSpec(memory_space=pltpu.HBM)
_SEM = pl.BlockSpec(memory_space=pltpu.SEMAPHORE)
_EFFECT = pltpu.SideEffectType.DATAFLOW_SIDE_EFFECTING


def _peer_of(k):
    x, y, c = lax.axis_index("x"), lax.axis_index("y"), lax.axis_index("c")
    px = 1 - x if k & 4 else x
    py = 1 - y if k & 2 else y
    pc = 1 - c if k & 1 else c
    return (px, py, pc), 4 * px + 2 * py + pc


def _split_copies(kinds, ins, lands, send_sems, recv_sems, started):
    me = 4 * lax.axis_index("x") + 2 * lax.axis_index("y") + lax.axis_index("c")
    out = []
    for k in range(1, NDEV):
        dev, pid = _peer_of(k)
        for i, kind in enumerate(kinds):
            src = ins[i] if kind == "gather" else ins[i].at[pid]
            dst = lands[i].at[me] if started else lands[i].at[pid]
            slot = i * (NDEV - 1) + k - 1
            out.append(pltpu.make_async_remote_copy(
                src_ref=src, dst_ref=dst, send_sem=send_sems.at[slot], recv_sem=recv_sems.at[slot],
                device_id=dev, device_id_type=pl.DeviceIdType.MESH))
    return out


def _exchange_start(items, name):
    n = len(items)
    kinds = [k for _, k in items]
    srcs = [pltpu.with_memory_space_constraint(a, pltpu.HBM) for a, _ in items]
    lands = []
    for a, k in items:
        shp = (NDEV,) + tuple(a.shape) if k == "gather" else tuple(a.shape)
        lands.append(pltpu.with_memory_space_constraint(lax.empty(shp, a.dtype), pltpu.HBM))

    def body(*refs):
        ins, land_refs = refs[:n], refs[n:2 * n]
        send_sems, recv_sems, own_sems = refs[2 * n:2 * n + 3]
        token = refs[-1]
        for cp in _own_copies(kinds, ins, land_refs, own_sems):
            cp.start()
        for cp in _split_copies(kinds, ins, land_refs, send_sems, recv_sems, True):
            cp.start()
        token[...] = jnp.zeros_like(token)

    sems = pltpu.SemaphoreType.DMA((n * (NDEV - 1),))
    res = pl.pallas_call(
        body, name=name,
        out_shape=(sems, sems, pltpu.SemaphoreType.DMA((n,)),
                   *[pltpu.HBM(a.shape, a.dtype) for a in srcs + lands], SDS((8, 128), F32)),
        in_specs=[_HBM] * (2 * n),
        out_specs=(_SEM, _SEM, _SEM, *[_HBM] * (2 * n), pl.BlockSpec(memory_space=pltpu.VMEM)),
        input_output_aliases={i: 3 + i for i in range(2 * n)},
        compiler_params=pltpu.CompilerParams(has_side_effects=_EFFECT),
    )(*srcs, *lands)
    return (kinds, res[0], res[1], res[2], list(res[3:3 + n]), list(res[3 + n:3 + 2 * n])), res[-1][0, 0]


def _own_copies(kinds, ins, lands, own_sems):
    me = 4 * lax.axis_index("x") + 2 * lax.axis_index("y") + lax.axis_index("c")
    return [pltpu.make_async_copy(ins[i] if kind == "gather" else ins[i].at[me], lands[i].at[me], own_sems.at[i])
            for i, kind in enumerate(kinds)]


def _exchange_wait(state, after, name):
    kinds, send_sems, recv_sems, own_sems, srcs, lands = state
    n = len(kinds)

    def body(*refs):
        ins, land_refs = refs[:n], refs[n:2 * n]
        s_sems, r_sems, o_sems = refs[2 * n:2 * n + 3]
        for cp in _split_copies(kinds, ins, land_refs, s_sems, r_sems, False):
            cp.wait_send()
            cp.wait_recv()
        for cp in _own_copies(kinds, ins, land_refs, o_sems):
            cp.wait()

    res = pl.pallas_call(
        body, name=name,
        out_shape=tuple(pltpu.HBM(a.shape, a.dtype) for a in srcs + lands),
        in_specs=[_HBM] * (2 * n) + [_SEM, _SEM, _SEM, pl.BlockSpec(memory_space=pl.ANY)],
        out_specs=tuple([_HBM] * (2 * n)),
        input_output_aliases={i: i for i in range(2 * n)},
        compiler_params=pltpu.CompilerParams(has_side_effects=_EFFECT),
    )(*srcs, *lands, send_sems, recv_sems, own_sems, after)
    return list(res[n:])


def _proj_in(x2, w_in_t, b_in):
    nq = 3 * AW

    def body(x_ref, w_ref, b_ref, qkv_ref, ag_ref):
        xb = x_ref[...].astype(BF16)
        qkv_ref[...] = (_dot_nt(xb, w_ref[pl.ds(0, nq), :]) + b_ref[:, :nq]).astype(BF16)
        ag_ref[...] = _dot_nt(xb, w_ref[pl.ds(nq, 2 * CW), :]) + b_ref[:, nq:]

    return pl.pallas_call(
        body, name="proj_in", grid=(T // TM,),
        in_specs=[pl.BlockSpec((TM, D), lambda m: (m, 0)), pl.BlockSpec((INW, D), lambda m: (0, 0)),
                  pl.BlockSpec((1, INW), lambda m: (0, 0))],
        out_specs=(pl.BlockSpec((TM, nq), lambda m: (m, 0)), pl.BlockSpec((TM, 2 * CW), lambda m: (m, 0))),
        out_shape=(SDS((T, nq), BF16), SDS((T, 2 * CW), F32)),
    )(x2, w_in_t, b_in)


def _grad_x(pieces, w_in_t, dz1, zero):
    widths = [p.shape[1] for p in pieces]

    def body(*refs):
        p_refs = refs[:len(pieces)]
        w_ref, dz_ref, z_ref, o_ref = refs[len(pieces):]
        acc = ALPHA * dz_ref[...] + z_ref[...]
        r0 = 0
        for p_ref, wd in zip(p_refs, widths):
            acc = acc + _dot(p_ref[...], w_ref[pl.ds(r0, wd), :])
            r0 += wd
        o_ref[...] = acc

    row = pl.BlockSpec((TM, D), lambda m: (m, 0))
    return pl.pallas_call(
        body, name="grad_x", grid=(T // TM,),
        in_specs=[pl.BlockSpec((TM, wd), lambda m: (m, 0)) for wd in widths]
        + [pl.BlockSpec((INW, D), lambda m: (0, 0)), row, pl.BlockSpec((1, 1), lambda m: (0, 0))],
        out_specs=row,
        out_shape=SDS((T, D), F32),
    )(*pieces, w_in_t, dz1, zero)


def _grad_w_in(pieces, x2):
    widths = [p.shape[1] for p in pieces]
    tk = 512
    nk = T // tk

    def body(*refs):
        p_refs = refs[:len(pieces)]
        x_ref, o_ref, acc = refs[len(pieces):]
        k = pl.program_id(0)

        @pl.when(k == 0)
        def _():
            acc[...] = jnp.zeros_like(acc)

        xb = x_ref[...].astype(BF16)
        r0 = 0
        for p_ref, wd in zip(p_refs, widths):
            acc[pl.ds(r0, wd), :] += _dot_tn(p_ref[...], xb)
            r0 += wd

        @pl.when(k == nk - 1)
        def _():
            o_ref[...] = acc[...].astype(o_ref.dtype)

    return pl.pallas_call(
        body, name="grad_w_in", grid=(nk,),
        in_specs=[pl.BlockSpec((tk, wd), lambda k: (k, 0)) for wd in widths] + [pl.BlockSpec((tk, D), lambda k: (k, 0))],
        out_specs=pl.BlockSpec((INW, D), lambda k: (0, 0)),
        out_shape=SDS((INW, D), GRAD_WIRE),
        scratch_shapes=[pltpu.VMEM((INW, D), F32)],
    )(*pieces, x2)


def _mm_tn(a, b, tn, tk, name):
    t_, na = a.shape
    nb = b.shape[1]
    nk = t_ // tk

    def body(a_ref, b_ref, o_ref, acc):
        k = pl.program_id(1)

        @pl.when(k == 0)
        def _():
            acc[...] = jnp.zeros_like(acc)

        acc[...] += _dot_tn(a_ref[...].astype(BF16), b_ref[...].astype(BF16))

        @pl.when(k == nk - 1)
        def _():
            o_ref[...] = acc[...].astype(o_ref.dtype)

    return pl.pallas_call(
        body, name=name, grid=(na // tn, nk),
        in_specs=[pl.BlockSpec((tk, tn), lambda n, k: (k, n)),
                  pl.BlockSpec((tk, nb), lambda n, k: (k, 0))],
        out_specs=pl.BlockSpec((tn, nb), lambda n, k: (n, 0)),
        out_shape=SDS((na, nb), GRAD_WIRE),
        scratch_shapes=[pltpu.VMEM((tn, nb), F32)],
    )(a, b)


def _bucket_maps():
    qi = np.arange(BLK)[:, None]
    kj = np.arange(2 * BLK)[None, :]
    steps = np.maximum(qi + BLK - kj, 0)
    exact = NBUCKET // 2
    maps = []
    for _, dil in BRANCHES:
        dist = steps * dil
        d_f = np.maximum(dist, 1).astype(np.float32)
        large = exact + (np.log(d_f / np.float32(exact)) / np.float32(math.log(S / exact))
                         * np.float32(NBUCKET - exact)).astype(np.int32)
        large = np.minimum(large, NBUCKET - 1)
        maps.append(np.where(dist < exact, dist, large).astype(np.int32))
    return np.stack(maps)


def _bias_table(rel_table, buckets):
    def body(t_ref, b_ref, o_ref):
        bk = b_ref[0]
        for h in range(NH):
            acc = jnp.zeros((BLK, 2 * BLK), F32)
            for k in range(NBUCKET):
                acc = jnp.where(bk == k, t_ref[k, h], acc)
            o_ref[0, h] = acc

    return pl.pallas_call(
        body, name="bias_table", grid=(len(BRANCHES),),
        in_specs=[pl.BlockSpec(memory_space=pltpu.SMEM),
                  pl.BlockSpec((1, BLK, 2 * BLK), lambda i: (i, 0, 0))],
        out_specs=pl.BlockSpec((1, NH, BLK, 2 * BLK), lambda i: (i, 0, 0, 0)),
        out_shape=SDS((len(BRANCHES), NH, BLK, 2 * BLK), F32),
    )(rel_table, buckets)


def _rel_table_grad(dbias, buckets):
    def body(d_ref, b_ref, o_ref):
        h = pl.program_id(0)
        for k in range(NBUCKET):
            tot = jnp.zeros((1, 1), F32)
            for br in range(len(BRANCHES)):
                sel = jnp.where(b_ref[br] == k, d_ref[br, 0], 0.0)
                tot = tot + jnp.sum(jnp.sum(sel, axis=1, keepdims=True), axis=0, keepdims=True)
            o_ref[0, :, pl.ds(k, 1)] = tot

    out = pl.pallas_call(
        body, name="rel_table_grad", grid=(NH,),
        in_specs=[pl.BlockSpec((len(BRANCHES), 1, BLK, 2 * BLK), lambda h: (0, h, 0, 0)),
                  pl.BlockSpec((len(BRANCHES), BLK, 2 * BLK), lambda h: (0, 0, 0))],
        out_specs=pl.BlockSpec((1, 1, NBUCKET), lambda h: (h, 0, 0)),
        out_shape=SDS((NH, 1, NBUCKET), F32),
    )(dbias, buckets)
    return out.reshape(NH, NBUCKET).T


PADK = BLK
SCALE = 1.0 / math.sqrt(HD)
ATTN_UNROLL = 16


def _branch_geometry(br):
    dil = BRANCHES[br][1]
    sub = S // dil
    return dil, sub, sub // BLK


def _token_rows(br, i):
    dil, _, nblk = _branch_geometry(br)
    if dil == 1:
        return pl.ds(pl.multiple_of(i * BLK, BLK), BLK), i
    r = lax.shift_right_logical(i, nblk.bit_length() - 1)
    n = lax.bitwise_and(i, nblk - 1)
    return pl.ds(r + dil * BLK * n, BLK, stride=dil), n


def _sub_layout_loop(br, step):
    dil, sub, _ = _branch_geometry(br)
    rows = min(sub, 256)
    nchunk = sub // rows

    def it_step(it, carry):
        if dil == 1:
            src = pl.ds(pl.multiple_of(it * rows, rows), rows)
        else:
            r = lax.shift_right_logical(it, nchunk.bit_length() - 1)
            src = pl.ds(r + dil * rows * lax.bitwise_and(it, nchunk - 1), rows, stride=dil)
        step(src, pl.multiple_of(it * rows, BLK), rows)
        return carry

    lax.fori_loop(0, dil * nchunk, it_step, 0)


def _masked_bias(bias_ref, bm):
    qi = lax.broadcasted_iota(jnp.int32, (BLK, 2 * BLK), 0)
    kj = lax.broadcasted_iota(jnp.int32, (BLK, 2 * BLK), 1)
    first = jnp.logical_and(kj >= BLK, kj - BLK <= qi)
    valid = jnp.logical_or(first, jnp.logical_and(kj < BLK, kj >= qi))
    for br in range(len(BRANCHES)):
        for j in range(2):
            b = bias_ref[br, j]
            bm[br, 1, pl.ds(j * BLK, BLK), :] = jnp.where(valid, b, NEG_INF)
            bm[br, 0, pl.ds(j * BLK, BLK), :] = jnp.where(first, b, NEG_INF)


def _head_split(fn):
    def split(t):
        h0 = lax.broadcasted_iota(jnp.int32, t.shape, 1) < HD
        t = fn(t)
        return jnp.where(h0, t, 0.0).astype(BF16), jnp.where(h0, 0.0, t).astype(BF16)
    return split


def _attn_fwd(qkv, bias):
    nbr = len(BRANCHES)

    def body(q_ref, k_ref, v_ref, bias_ref, o_ref, lse_ref, qf, kf, vf, qs0, qs1, ks, vs, bm, ob, mb, lb):
        qf[...] = q_ref[...].astype(F32)
        kf[...] = k_ref[...].astype(F32)
        vf[...] = v_ref[...].astype(F32)
        _masked_bias(bias_ref, bm)
        ks[pl.ds(0, PADK), :] = jnp.zeros((PADK, BLK), BF16)
        vs[pl.ds(0, PADK), :] = jnp.zeros((PADK, BLK), BF16)
        head0 = lax.broadcasted_iota(jnp.int32, (BLK, BLK), 1) < HD
        split_q = _head_split(lambda t: t * SCALE)

        for br in range(nbr):
            nblk = _branch_geometry(br)[2]

            def stage(src, off, rows):
                qs0[pl.ds(off, rows), :], qs1[pl.ds(off, rows), :] = split_q(qf[src, :])
                ks[pl.ds(PADK + off, rows), :] = kf[src, :].astype(BF16)
                vs[pl.ds(PADK + off, rows), :] = vf[src, :].astype(BF16)

            _sub_layout_loop(br, stage)

            def blk(i, carry, br=br, nblk=nblk):
                base = pl.multiple_of(i * BLK, BLK)
                rows, n = _token_rows(br, i)
                q01 = jnp.concatenate([qs0[pl.ds(base, BLK), :], qs1[pl.ds(base, BLK), :]], axis=0)
                if nblk > 1:
                    kcat = ks[pl.ds(base, 2 * BLK), :]
                    vcat = vs[pl.ds(base, 2 * BLK), :]
                    s = _dot_nt(q01, kcat) + bm[br, jnp.minimum(n, 1)]
                else:
                    kcat = ks[pl.ds(PADK + base, BLK), :]
                    vcat = vs[pl.ds(PADK + base, BLK), :]
                    s = _dot_nt(q01, kcat) + bm[br, 0, :, BLK:]
                mx = jnp.max(s, axis=-1, keepdims=True)
                p = jnp.exp(s - mx)
                ls = jnp.sum(p, axis=-1, keepdims=True)
                o = _dot(p.astype(BF16), vcat)
                ob[br, rows, :] = jnp.where(head0, o[:BLK], o[BLK:])
                mb[br, rows, :] = jnp.where(head0, mx[:BLK], mx[BLK:])
                lb[br, rows, :] = jnp.where(head0, ls[:BLK], ls[BLK:])
                return carry

            lax.fori_loop(0, 16, blk, 0, unroll=ATTN_UNROLL)

        def merge(i, carry):
            rows = pl.ds(pl.multiple_of(i * 256, 256), 256)
            m_all = jnp.maximum(jnp.maximum(mb[0, rows, :], mb[1, rows, :]), mb[2, rows, :])
            num = jnp.zeros((256, BLK), F32)
            den = jnp.zeros((256, BLK), F32)
            for br in range(nbr):
                c = jnp.exp(mb[br, rows, :] - m_all)
                num = num + ob[br, rows, :] * c
                den = den + lb[br, rows, :] * c
            o_ref[rows, :] = num / den
            lse_ref[rows, :] = m_all + jnp.log(den)
            return carry

        lax.fori_loop(0, S // 256, merge, 0)

    npair = NH // 2
    blk_spec = lambda off: pl.BlockSpec((S, BLK), lambda b, hp: (b, off + hp))
    return pl.pallas_call(
        body, name="attn_fwd", grid=(BL, npair),
        in_specs=[blk_spec(0), blk_spec(npair), blk_spec(2 * npair),
                  pl.BlockSpec((nbr, 2, BLK, 2 * BLK), lambda b, hp: (0, hp, 0, 0))],
        out_specs=(blk_spec(0), blk_spec(0)),
        out_shape=(SDS((T, AW), F32), SDS((T, AW), F32)),
        scratch_shapes=[pltpu.VMEM((S, BLK), F32)] * 3 + [pltpu.VMEM((S, BLK), BF16)] * 2
        + [pltpu.VMEM((PADK + S, BLK), BF16)] * 2 + [pltpu.VMEM((nbr, 2, 2 * BLK, 2 * BLK), F32)]
        + [pltpu.VMEM((nbr, S, BLK), F32)] * 3,
    )(qkv, qkv, qkv, bias)


def _attn_bwd(qkv, attn, lse, dattn, bias):
    nbr = len(BRANCHES)

    def body(q_ref, k_ref, v_ref, o_ref, lse_ref, do_ref, bias_ref,
             dq_ref, dk_ref, dv_ref, sq_ref, sk_ref, sv_ref, db_ref,
             qf, kf, vf, dl, dqa, dka, dva, qs0, qs1, ds0, ds1, ks, vs, dks, dvs, bm):
        b = pl.program_id(1)
        qf[...] = q_ref[...].astype(F32)
        kf[...] = k_ref[...].astype(F32)
        vf[...] = v_ref[...].astype(F32)
        dqa[...] = jnp.zeros_like(dqa)
        dka[...] = jnp.zeros_like(dka)
        dva[...] = jnp.zeros_like(dva)
        _masked_bias(bias_ref, bm)
        ks[pl.ds(0, PADK), :] = jnp.zeros((PADK, BLK), BF16)
        vs[pl.ds(0, PADK), :] = jnp.zeros((PADK, BLK), BF16)
        head0 = lax.broadcasted_iota(jnp.int32, (BLK, BLK), 1) < HD
        split_q = _head_split(lambda t: t * SCALE)
        split_do = _head_split(lambda t: t)

        @pl.when(b == 0)
        def _():
            db_ref[...] = jnp.zeros_like(db_ref)
            sq_ref[...] = jnp.zeros_like(sq_ref)
            sk_ref[...] = jnp.zeros_like(sk_ref)
            sv_ref[...] = jnp.zeros_like(sv_ref)

        def delta(i, carry):
            rows = pl.ds(pl.multiple_of(i * 256, 256), 256)
            prod = do_ref[rows, :] * o_ref[rows, :]
            h0 = lax.broadcasted_iota(jnp.int32, (256, BLK), 1) < HD
            d0 = jnp.sum(jnp.where(h0, prod, 0.0), axis=-1, keepdims=True)
            d1 = jnp.sum(jnp.where(h0, 0.0, prod), axis=-1, keepdims=True)
            dl[rows, :] = jnp.where(h0, d0, d1)
            return carry

        lax.fori_loop(0, S // 256, delta, 0)

        for br in range(nbr):
            nblk = _branch_geometry(br)[2]

            def stage(src, off, rows):
                qs0[pl.ds(off, rows), :], qs1[pl.ds(off, rows), :] = split_q(qf[src, :])
                ds0[pl.ds(off, rows), :], ds1[pl.ds(off, rows), :] = split_do(do_ref[src, :])
                ks[pl.ds(PADK + off, rows), :] = kf[src, :].astype(BF16)
                vs[pl.ds(PADK + off, rows), :] = vf[src, :].astype(BF16)

            _sub_layout_loop(br, stage)
            dks[...] = jnp.zeros_like(dks)
            dvs[...] = jnp.zeros_like(dvs)

            def blk(i, carry, br=br, nblk=nblk):
                base = pl.multiple_of(i * BLK, BLK)
                rows, n = _token_rows(br, i)
                q01 = jnp.concatenate([qs0[pl.ds(base, BLK), :], qs1[pl.ds(base, BLK), :]], axis=0)
                do01 = jnp.concatenate([ds0[pl.ds(base, BLK), :], ds1[pl.ds(base, BLK), :]], axis=0)
                lse_b = lse_ref[rows, :]
                dl_b = dl[rows, :]
                lse01 = jnp.concatenate([lse_b[:, 0:1], lse_b[:, HD:HD + 1]], axis=0)
                dl01 = jnp.concatenate([dl_b[:, 0:1], dl_b[:, HD:HD + 1]], axis=0)
                if nblk > 1:
                    krows = pl.ds(base, 2 * BLK)
                    bias_m = bm[br, jnp.minimum(n, 1)]
                else:
                    krows = pl.ds(PADK + base, BLK)
                    bias_m = bm[br, 0, :, BLK:]
                kcat = ks[krows, :]
                vcat = vs[krows, :]
                p = jnp.exp(_dot_nt(q01, kcat) + bias_m - lse01)
                dsv = p * (_dot_nt(do01, vcat) - dl01)
                if nblk > 1:
                    db_ref[br, 0] += dsv[:BLK]
                    db_ref[br, 1] += dsv[BLK:]
                else:
                    db_ref[br, 0, :, BLK:] += dsv[:BLK]
                    db_ref[br, 1, :, BLK:] += dsv[BLK:]
                dsb = dsv.astype(BF16)
                dq01 = _dot(dsb, kcat)
                dqa[rows, :] = dqa[rows, :] + jnp.where(head0, dq01[:BLK], dq01[BLK:])
                dks[krows, :] = dks[krows, :] + _dot_tn(dsb, q01)
                dvs[krows, :] = dvs[krows, :] + _dot_tn(p.astype(BF16), do01)
                return carry

            lax.fori_loop(0, 16, blk, 0, unroll=ATTN_UNROLL)

            def fold(src, off, rows):
                dka[src, :] = dka[src, :] + dks[pl.ds(PADK + off, rows), :]
                dva[src, :] = dva[src, :] + dvs[pl.ds(PADK + off, rows), :]

            _sub_layout_loop(br, fold)

        def flush(i, carry):
            rows = pl.ds(pl.multiple_of(i * 256, 256), 256)
            for acc, out, cs, mul in ((dqa, dq_ref, sq_ref, SCALE), (dka, dk_ref, sk_ref, 1.0), (dva, dv_ref, sv_ref, 1.0)):
                val = acc[rows, :] * mul
                out[rows, :] = val.astype(BF16)
                cs[...] += _colsum(val)
            return carry

        lax.fori_loop(0, S // 256, flush, 0)

    npair = NH // 2
    blk_spec = lambda off: pl.BlockSpec((S, BLK), lambda hp, b: (b, off + hp))
    sum_spec = pl.BlockSpec((1, BLK), lambda hp, b: (0, hp))
    return pl.pallas_call(
        body, name="attn_bwd", grid=(npair, BL),
        in_specs=[blk_spec(0), blk_spec(npair), blk_spec(2 * npair), blk_spec(0), blk_spec(0), blk_spec(0),
                  pl.BlockSpec((nbr, 2, BLK, 2 * BLK), lambda hp, b: (0, hp, 0, 0))],
        out_specs=(blk_spec(0), blk_spec(0), blk_spec(0), sum_spec, sum_spec, sum_spec,
                   pl.BlockSpec((nbr, 2, BLK, 2 * BLK), lambda hp, b: (0, hp, 0, 0))),
        out_shape=(SDS((T, AW), BF16), SDS((T, AW), BF16), SDS((T, AW), BF16),
                   SDS((1, AW), F32), SDS((1, AW), F32), SDS((1, AW), F32),
                   SDS((nbr, NH, BLK, 2 * BLK), F32)),
        scratch_shapes=[pltpu.VMEM((S, BLK), F32)] * 7 + [pltpu.VMEM((S, BLK), BF16)] * 4
        + [pltpu.VMEM((PADK + S, BLK), BF16)] * 2 + [pltpu.VMEM((PADK + S, BLK), F32)] * 2
        + [pltpu.VMEM((nbr, 2, 2 * BLK, 2 * BLK), F32)],
    )(qkv, qkv, qkv, attn, lse, dattn, bias)


CH = 256
PADR = 32


def _tap_phases(offset_of_tap):
    taps = sorted((offset_of_tap(k) % 8, offset_of_tap(k) - offset_of_tap(k) % 8, k) for k in range(CK))
    assert all(lo + CH + ph <= CH + PADR for ph, lo, _ in taps)
    return taps


def _rows_up(win):
    made = {0: win}

    def get(phase):
        if phase not in made:
            made[phase] = pltpu.roll(win, win.shape[0] - phase, 0)
        return made[phase]
    return get


def _conv_fwd(ag, conv_w, conv_b):
    def body(ag_ref, w_ref, b_ref, u1_ref, u0p):
        u0p[pl.ds(0, PADR), :] = jnp.zeros((PADR, CW), F32)

        def glu(i, carry):
            t0 = pl.multiple_of(i * CH, CH)
            a = ag_ref[pl.ds(t0, CH), :CW]
            g = ag_ref[pl.ds(t0, CH), CW:]
            u0p[pl.ds(PADR + t0, CH), :] = a * _sigmoid(g)
            return carry

        lax.fori_loop(0, S // CH, glu, 0)

        def conv(i, carry):
            t0 = pl.multiple_of(i * CH, CH)
            win = u0p[pl.ds(t0, CH + PADR), :]
            acc = jnp.zeros((CH, CW), F32) + b_ref[...]
            up = _rows_up(win)
            for phase, lo, k in _tap_phases(lambda k: PADR - (CK - 1) + k):
                acc = acc + up(phase)[lo:lo + CH, :] * w_ref[k:k + 1, :]
            u1_ref[pl.ds(t0, CH), :] = acc
            return carry

        lax.fori_loop(0, S // CH, conv, 0)

    return pl.pallas_call(
        body, name="conv_fwd", grid=(BL,),
        in_specs=[pl.BlockSpec((S, 2 * CW), lambda b: (b, 0)),
                  pl.BlockSpec((CK, CW), lambda b: (0, 0)),
                  pl.BlockSpec((1, CW), lambda b: (0, 0))],
        out_specs=pl.BlockSpec((S, CW), lambda b: (b, 0)),
        out_shape=SDS((T, CW), F32),
        scratch_shapes=[pltpu.VMEM((S + PADR, CW), F32)],
    )(ag, conv_w, conv_b)


def _conv_post(u1, cg, cb):
    mu = _rowmean(u1)
    uc = u1 - mu
    rstd = lax.rsqrt(_rowmean(uc * uc) + LN_EPS)
    xh = uc * rstd
    u2 = xh * cg + cb
    sg = _sigmoid(u2)
    return xh, rstd, u2, sg, u2 * sg


def _mix_fwd(attn, u1, ga, gc, cg, cb):
    def body(a_ref, u_ref, ga_ref, gc_ref, cg_ref, cb_ref, o_ref):
        a = a_ref[...]
        ra = lax.rsqrt(_rowmean(a * a) + LN_EPS)
        o_ref[:, :AW] = (a * ra * ga_ref[...]).astype(BF16)
        _, _, _, _, u3 = _conv_post(u_ref[...], cg_ref[...], cb_ref[...])
        rc = lax.rsqrt(_rowmean(u3 * u3) + LN_EPS)
        o_ref[:, AW:] = (u3 * rc * gc_ref[...]).astype(BF16)

    vec = lambda w: pl.BlockSpec((1, w), lambda m: (0, 0))
    return pl.pallas_call(
        body, name="mix_fwd", grid=(T // TM,),
        in_specs=[pl.BlockSpec((TM, AW), lambda m: (m, 0)), pl.BlockSpec((TM, CW), lambda m: (m, 0)),
                  vec(AW), vec(CW), vec(CW), vec(CW)],
        out_specs=pl.BlockSpec((TM, D), lambda m: (m, 0)),
        out_shape=SDS((T, D), BF16),
    )(attn, u1, ga, gc, cg, cb)


def _mix_bwd(dz1, w_out, attn, u1, ga, gc, cg, cb):
    def body(dz_ref, w_ref, a_ref, u_ref, ga_ref, gc_ref, cg_ref, cb_ref,
             da_ref, du_ref, g_an, g_cn, g_lg, g_lb, g_cb):
        @pl.when(pl.program_id(0) == 0)
        def _():
            for r in (g_an, g_cn, g_lg, g_lb, g_cb):
                r[...] = jnp.zeros_like(r)

        dm = _dot_nt(dz_ref[...].astype(BF16), w_ref[...])
        a = a_ref[...]
        dna = dm[:, :AW]
        ra = lax.rsqrt(_rowmean(a * a) + LN_EPS)
        g_an[...] += _colsum(dna * a * ra)
        dat = dna * ga_ref[...]
        da_ref[...] = ra * dat - a * (ra * ra * ra) * _rowmean(dat * a)

        xh, rstd, u2, sg, u3 = _conv_post(u_ref[...], cg_ref[...], cb_ref[...])
        dnc = dm[:, AW:]
        rc = lax.rsqrt(_rowmean(u3 * u3) + LN_EPS)
        g_cn[...] += _colsum(dnc * u3 * rc)
        dut = dnc * gc_ref[...]
        du3 = rc * dut - u3 * (rc * rc * rc) * _rowmean(dut * u3)
        du2 = du3 * sg * (1.0 + u2 * (1.0 - sg))
        g_lg[...] += _colsum(du2 * xh)
        g_lb[...] += _colsum(du2)
        dxh = du2 * cg_ref[...]
        du1 = rstd * (dxh - _rowmean(dxh) - xh * _rowmean(dxh * xh))
        g_cb[...] += _colsum(du1)
        du_ref[...] = du1

    vec = lambda w: pl.BlockSpec((1, w), lambda m: (0, 0))
    return pl.pallas_call(
        body, name="mix_bwd", grid=(T // TM,),
        in_specs=[pl.BlockSpec((TM, D), lambda m: (m, 0)), pl.BlockSpec((D, D), lambda m: (0, 0)),
                  pl.BlockSpec((TM, AW), lambda m: (m, 0)),
                  pl.BlockSpec((TM, CW), lambda m: (m, 0)), vec(AW), vec(CW), vec(CW), vec(CW)],
        out_specs=(pl.BlockSpec((TM, AW), lambda m: (m, 0)), pl.BlockSpec((TM, CW), lambda m: (m, 0)),
                   vec(AW), vec(CW), vec(CW), vec(CW), vec(CW)),
        out_shape=(SDS((T, AW), F32), SDS((T, CW), F32),
                   SDS((1, AW), F32), SDS((1, CW), F32), SDS((1, CW), F32), SDS((1, CW), F32), SDS((1, CW), F32)),
    )(dz1, w_out, attn, u1, ga, gc, cg, cb)


def _conv_bwd(du1, ag, conv_w):
    def body(du_ref, ag_ref, w_ref, dag_ref, cs_ref, gw_ref, u0p, dup):
        @pl.when(pl.program_id(0) == 0)
        def _():
            cs_ref[...] = jnp.zeros_like(cs_ref)
            gw_ref[...] = jnp.zeros_like(gw_ref)

        u0p[pl.ds(0, PADR), :] = jnp.zeros((PADR, CW), F32)
        dup[pl.ds(S, PADR), :] = jnp.zeros((PADR, CW), F32)

        def fill(i, carry):
            t0 = pl.multiple_of(i * CH, CH)
            a = ag_ref[pl.ds(t0, CH), :CW]
            g = ag_ref[pl.ds(t0, CH), CW:]
            u0p[pl.ds(PADR + t0, CH), :] = a * _sigmoid(g)
            dup[pl.ds(t0, CH), :] = du_ref[pl.ds(t0, CH), :]
            return carry

        lax.fori_loop(0, S // CH, fill, 0)

        def chunk(i, carry):
            t0 = pl.multiple_of(i * CH, CH)
            d = dup[pl.ds(t0, CH), :]
            win_u = u0p[pl.ds(t0, CH + PADR), :]
            win_d = dup[pl.ds(t0, CH + PADR), :]
            du0 = jnp.zeros((CH, CW), F32)
            up_u = _rows_up(win_u)
            for phase, lo, k in _tap_phases(lambda k: PADR - (CK - 1) + k):
                gw_ref[k:k + 1, :] += _colsum(d * up_u(phase)[lo:lo + CH, :])
            up_d = _rows_up(win_d)
            for phase, lo, k in _tap_phases(lambda k: CK - 1 - k):
                du0 = du0 + up_d(phase)[lo:lo + CH, :] * w_ref[k:k + 1, :]
            a = ag_ref[pl.ds(t0, CH), :CW]
            sg = _sigmoid(ag_ref[pl.ds(t0, CH), CW:])
            da = du0 * sg
            dg = du0 * a * sg * (1.0 - sg)
            dag_ref[pl.ds(t0, CH), :CW] = da.astype(BF16)
            dag_ref[pl.ds(t0, CH), CW:] = dg.astype(BF16)
            cs_ref[:, :CW] += _colsum(da)
            cs_ref[:, CW:] += _colsum(dg)
            return carry

        lax.fori_loop(0, S // CH, chunk, 0)

    return pl.pallas_call(
        body, name="conv_bwd", grid=(BL,),
        in_specs=[pl.BlockSpec((S, CW), lambda b: (b, 0)), pl.BlockSpec((S, 2 * CW), lambda b: (b, 0)),
                  pl.BlockSpec((CK, CW), lambda b: (0, 0))],
        out_specs=(pl.BlockSpec((S, 2 * CW), lambda b: (b, 0)),
                   pl.BlockSpec((1, 2 * CW), lambda b: (0, 0)),
                   pl.BlockSpec((PADR, CW), lambda b: (0, 0))),
        out_shape=(SDS((T, 2 * CW), BF16), SDS((1, 2 * CW), F32), SDS((PADR, CW), F32)),
        scratch_shapes=[pltpu.VMEM((S + PADR, CW), F32), pltpu.VMEM((S + PADR, CW), F32)],
    )(du1, ag, conv_w)


def _layer_norm_fwd(z):
    mu = _rowmean(z)
    zc = z - mu
    rstd = lax.rsqrt(_rowmean(zc * zc) + LN_EPS)
    return zc * rstd, rstd


def _layer_norm_bwd(dy, xh, rstd, g):
    dxh = dy * g
    return rstd * (dxh - _rowmean(dxh) - xh * _rowmean(dxh * xh))


def _out_proj_ln1(mixed, w_out, x2, g1, b1):
    def body(a_ref, w_ref, x_ref, g_ref, b_ref, xh_ref, rstd_ref, x1_ref):
        z = ALPHA * x_ref[...] + _dot(a_ref[...], w_ref[...])
        xh, rstd = _layer_norm_fwd(z)
        xh_ref[...] = xh
        rstd_ref[...] = rstd
        x1_ref[...] = (xh * g_ref[...] + b_ref[...]).astype(BF16)

    vec = pl.BlockSpec((1, D), lambda m: (0, 0))
    row = pl.BlockSpec((TM, D), lambda m: (m, 0))
    return pl.pallas_call(
        body, name="out_proj_ln1", grid=(T // TM,),
        in_specs=[row, pl.BlockSpec((D, D), lambda m: (0, 0)), row, vec, vec],
        out_specs=(row, pl.BlockSpec((TM, 1), lambda m: (m, 0)), row),
        out_shape=(SDS((T, D), F32), SDS((T, 1), F32), SDS((T, D), BF16)),
    )(mixed, w_out, x2, g1, b1)


def _seq_start(m):
    return lax.bitwise_and(m, S // TMF - 1) == 0


def _shift_down(x, before, k):
    rolled = pltpu.roll(x, k, 0)
    row = lax.broadcasted_iota(jnp.int32, before.shape, 0)
    head = jnp.where(row < k, pltpu.roll(before, k, 0), rolled[:8])
    return jnp.concatenate([head, rolled[8:]], axis=0)


def _shift_up(x, after, k):
    n = x.shape[0]
    rolled = pltpu.roll(x, n - k, 0)
    row = lax.broadcasted_iota(jnp.int32, after.shape, 0)
    tail = jnp.where(row >= 8 - k, pltpu.roll(after, 8 - k, 0), rolled[n - 8:])
    return jnp.concatenate([rolled[:n - 8], tail], axis=0)


def _ffn_up(x1b, w_up, fcw, fcb):
    def body(x_ref, wg_ref, wv_ref, cwg_ref, cwv_ref, cbg_ref, cbv_ref, up_ref, gv_ref, act_ref, prev_g, prev_v):
        @pl.when(_seq_start(pl.program_id(1)))
        def _():
            prev_g[...] = jnp.zeros_like(prev_g)
            prev_v[...] = jnp.zeros_like(prev_v)

        x = x_ref[...]
        outs = []
        for w_ref, cw_ref, cb_ref, prev, lo in ((wg_ref, cwg_ref, cbg_ref, prev_g, 0), (wv_ref, cwv_ref, cbv_ref, prev_v, FT)):
            ub = _dot_nt(x, w_ref[...]).astype(BF16)
            up_ref[:, lo:lo + FT] = ub
            u = ub.astype(F32)
            before = prev[...]
            y = (cw_ref[2:3, :] * u + cw_ref[1:2, :] * _shift_down(u, before, 1)
                 + cw_ref[0:1, :] * _shift_down(u, before, 2) + cb_ref[...])
            prev[...] = u[TMF - 8:]
            yb = y.astype(BF16)
            gv_ref[:, lo:lo + FT] = yb
            outs.append(yb.astype(F32))
        gate, val = outs
        act_ref[...] = (gate * _sigmoid(gate) * val).astype(BF16)

    wspec = lambda off: pl.BlockSpec((FT, D), lambda n, m: (n + off, 0))
    cwspec = lambda off: pl.BlockSpec((FK, FT), lambda n, m: (0, n + off))
    cbspec = lambda off: pl.BlockSpec((1, FT), lambda n, m: (0, n + off))
    pair = pl.BlockSpec((TMF, 2 * FT), lambda n, m: (m, n))
    return pl.pallas_call(
        body, name="ffn_up", grid=(NFT, T // TMF),
        in_specs=[pl.BlockSpec((TMF, D), lambda n, m: (m, 0)), wspec(0), wspec(NFT),
                  cwspec(0), cwspec(NFT), cbspec(0), cbspec(NFT)],
        out_specs=(pair, pair, pl.BlockSpec((TMF, FT), lambda n, m: (m, n))),
        out_shape=(SDS((T, 2 * DFF), BF16), SDS((T, 2 * DFF), BF16), SDS((T, DFF), BF16)),
        scratch_shapes=[pltpu.VMEM((8, FT), F32)] * 2,
    )(x1b, w_up, w_up, fcw, fcw, fcb, fcb)


def _ffn_down_loss(act, w_down, xh1, g1, b1, g2, b2, target):
    def body(a_ref, w_ref, xh1_ref, g1_ref, b1_ref, g2_ref, b2_ref, t_ref, dz_ref, loss_ref, gg_ref, gb_ref):
        @pl.when(pl.program_id(0) == 0)
        def _():
            loss_ref[...] = jnp.zeros_like(loss_ref)
            gg_ref[...] = jnp.zeros_like(gg_ref)
            gb_ref[...] = jnp.zeros_like(gb_ref)

        x1 = xh1_ref[...] * g1_ref[...] + b1_ref[...]
        z = ALPHA * x1 + _dot(a_ref[...], w_ref[...])
        xh, rstd = _layer_norm_fwd(z)
        diff = xh * g2_ref[...] + b2_ref[...] - t_ref[...]
        loss_ref[...] += 0.5 * _colsum(_rowmean(diff * diff))
        dout = diff * (1.0 / D)
        gg_ref[...] += _colsum(dout * xh)
        gb_ref[...] += _colsum(dout)
        dz_ref[...] = _layer_norm_bwd(dout, xh, rstd, g2_ref[...])

    vec = pl.BlockSpec((1, D), lambda m: (0, 0))
    row = pl.BlockSpec((TM, D), lambda m: (m, 0))
    return pl.pallas_call(
        body, name="ffn_down_loss", grid=(T // TM,),
        in_specs=[pl.BlockSpec((TM, DFF), lambda m: (m, 0)), pl.BlockSpec((DFF, D), lambda m: (0, 0)),
                  row, vec, vec, vec, vec, row],
        out_specs=(row, pl.BlockSpec((1, 1), lambda m: (0, 0)), vec, vec),
        out_shape=(SDS((T, D), F32), SDS((1, 1), F32), SDS((1, D), F32), SDS((1, D), F32)),
    )(act, w_down, xh1, g1, b1, g2, b2, target)


def _ffn_down_bwd(dz2, w_down, gv, up, fcw):
    tiles = T // TMF

    def body(dz_ref, wd_ref, gv_ref, up_ref, cwg_ref, cwv_ref,
             dpre_ref, csg_ref, csv_ref, gwg_ref, gwv_ref, next_g, next_v):
        step = pl.program_id(1)
        tile = tiles - 1 - step

        @pl.when(step == 0)
        def _():
            for r in (csg_ref, csv_ref, gwg_ref, gwv_ref, next_g, next_v):
                r[...] = jnp.zeros_like(r)

        seq_end = lax.bitwise_and(tile + 1, S // TMF - 1) == 0
        dact = _dot_nt(dz_ref[...].astype(BF16), wd_ref[...])
        gate = gv_ref[:, :FT].astype(F32)
        val = gv_ref[:, FT:].astype(F32)
        sg = _sigmoid(gate)
        gs = gate * sg
        halves = ((dact * val * (sg + gs * (1.0 - sg)), cwg_ref, csg_ref, gwg_ref, next_g, 0),
                  (dact * gs, cwv_ref, csv_ref, gwv_ref, next_v, FT))
        for d0, cw_ref, cs_ref, gw_ref, nxt, lo in halves:
            after = jnp.where(seq_end, 0.0, nxt[...])
            d1 = _shift_up(d0, after, 1)
            d2 = _shift_up(d0, after, 2)
            nxt[...] = d0[:8]
            dpre_ref[:, lo:lo + FT] = (cw_ref[2:3, :] * d0 + cw_ref[1:2, :] * d1 + cw_ref[0:1, :] * d2).astype(BF16)
            cs_ref[...] += _colsum(d0)
            u = up_ref[:, lo:lo + FT].astype(F32)
            for k, dk in enumerate((d2, d1, d0)):
                gw_ref[k:k + 1, :] += _colsum(dk * u)

    cs = pl.BlockSpec((1, FT), lambda n, m: (0, n))
    gw = pl.BlockSpec((FK, FT), lambda n, m: (0, n))
    cwspec = lambda off: pl.BlockSpec((FK, FT), lambda n, m: (0, n + off))
    pair = pl.BlockSpec((TMF, 2 * FT), lambda n, m: (tiles - 1 - m, n))
    return pl.pallas_call(
        body, name="ffn_down_bwd", grid=(NFT, tiles),
        in_specs=[pl.BlockSpec((TMF, D), lambda n, m: (tiles - 1 - m, 0)), pl.BlockSpec((FT, D), lambda n, m: (n, 0)),
                  pair, pair, cwspec(0), cwspec(NFT)],
        out_specs=(pair, cs, cs, gw, gw),
        out_shape=(SDS((T, 2 * DFF), BF16), SDS((1, DFF), F32), SDS((1, DFF), F32),
                   SDS((FK, DFF), F32), SDS((FK, DFF), F32)),
        scratch_shapes=[pltpu.VMEM((8, FT), F32)] * 2,
    )(dz2, w_down, gv, up, fcw, fcw)


def _ffn_up_bwd_ln1(dpre, w_up, dz2, xh1, rstd1, g1):
    def body(a_ref, w_ref, dz2_ref, xh_ref, rstd_ref, g_ref, dz1_ref, gg_ref, gb_ref):
        @pl.when(pl.program_id(0) == 0)
        def _():
            gg_ref[...] = jnp.zeros_like(gg_ref)
            gb_ref[...] = jnp.zeros_like(gb_ref)

        dx1 = ALPHA * dz2_ref[...]
        for n in range(NFT):
            for half in range(2):
                a = a_ref[:, (2 * n + half) * FT:(2 * n + half + 1) * FT]
                w = w_ref[pl.ds((half * NFT + n) * FT, FT), :]
                dx1 = dx1 + _dot(a, w)
        xh = xh_ref[...]
        gg_ref[...] += _colsum(dx1 * xh)
        gb_ref[...] += _colsum(dx1)
        dz1_ref[...] = _layer_norm_bwd(dx1, xh, rstd_ref[...], g_ref[...])

    vec = pl.BlockSpec((1, D), lambda m: (0, 0))
    row = pl.BlockSpec((TMF, D), lambda m: (m, 0))
    return pl.pallas_call(
        body, name="ffn_up_bwd_ln1", grid=(T // TMF,),
        in_specs=[pl.BlockSpec((TMF, 2 * DFF), lambda m: (m, 0)), pl.BlockSpec((2 * DFF, D), lambda m: (0, 0)),
                  row, row, pl.BlockSpec((TMF, 1), lambda m: (m, 0)), vec],
        out_specs=(row, vec, vec),
        out_shape=(SDS((T, D), F32), SDS((1, D), F32), SDS((1, D), F32)),
    )(dpre, w_up, dz2, xh1, rstd1, g1)


def _grad_w_up(dpre, x1b):
    tk = 1024

    def body(a_ref, b_ref, o_ref, acc):
        k = pl.program_id(1)

        @pl.when(k == 0)
        def _():
            acc[...] = jnp.zeros_like(acc)

        acc[...] += _dot_tn(a_ref[...], b_ref[...])

        @pl.when(k == T // tk - 1)
        def _():
            o_ref[0] = acc[pl.ds(0, FT), :].astype(o_ref.dtype)
            o_ref[1] = acc[pl.ds(FT, FT), :].astype(o_ref.dtype)

    out = pl.pallas_call(
        body, name="grad_w_up", grid=(NFT, T // tk),
        in_specs=[pl.BlockSpec((tk, 2 * FT), lambda n, k: (k, n)), pl.BlockSpec((tk, D), lambda n, k: (k, 0))],
        out_specs=pl.BlockSpec((2, FT, D), lambda n, k: (0, n, 0)),
        out_shape=SDS((2, DFF, D), GRAD_WIRE),
        scratch_shapes=[pltpu.VMEM((2 * FT, D), F32)],
    )(dpre, x1b)
    return out.reshape(2 * DFF, D)


def _row_tile(rows, cols):
    if rows * cols * 4 <= (1 << 20) or rows % 8:
        return rows
    for t in (256, 176, 128, 88, 64, 32, 16, 8):
        if rows % t == 0 and t * cols * 4 <= (1 << 20):
            return t
    return 8


def _sum8(r, name):
    _, rows, cols = r.shape
    tr = _row_tile(rows, cols)

    def body(r_ref, o_ref):
        acc = r_ref[0].astype(F32)
        for p in range(1, NDEV):
            acc = acc + r_ref[p].astype(F32)
        o_ref[...] = acc

    return pl.pallas_call(
        body, name=name, grid=(rows // tr,),
        in_specs=[pl.BlockSpec((NDEV, tr, cols), lambda i: (0, i, 0))],
        out_specs=pl.BlockSpec((tr, cols), lambda i: (i, 0)),
        out_shape=SDS((rows, cols), F32),
    )(r)


def _adamw(w, g, m, v, name):
    rows, cols = w.shape
    tr = _row_tile(rows, cols)

    def body(w_ref, g_ref, m_ref, v_ref, d_ref, nm_ref, nv_ref):
        g_ = g_ref[...]
        m_ = B1 * m_ref[...] + (1.0 - B1) * g_
        v_ = B2 * v_ref[...] + (1.0 - B2) * jnp.square(g_)
        m_hat = m_ / (1.0 - B1 ** STEP)
        v_hat = v_ / (1.0 - B2 ** STEP)
        d_ref[...] = -LR * (m_hat / (jnp.sqrt(v_hat) + AEPS) + WD * w_ref[...])
        nm_ref[...] = m_
        nv_ref[...] = v_

    spec = pl.BlockSpec((tr, cols), lambda i: (i, 0))
    shp = SDS((rows, cols), F32)
    return pl.pallas_call(
        body, name=name, grid=(rows // tr,), in_specs=[spec] * 4, out_specs=(spec,) * 3,
        out_shape=(shp, shp, shp),
    )(w, g, m, v)


def _adamw_many(ws, gs, ms, vs, name):
    n = len(ws)

    def body(*refs):
        for i in range(n):
            w_ref, g_ref, m_ref, v_ref, d_ref, nm_ref, nv_ref = refs[i::n]
            g_ = g_ref[...]
            m_ = B1 * m_ref[...] + (1.0 - B1) * g_
            v_ = B2 * v_ref[...] + (1.0 - B2) * jnp.square(g_)
            m_hat = m_ / (1.0 - B1 ** STEP)
            v_hat = v_ / (1.0 - B2 ** STEP)
            d_ref[...] = -LR * (m_hat / (jnp.sqrt(v_hat) + AEPS) + WD * w_ref[...])
            nm_ref[...] = m_
            nv_ref[...] = v_

    shapes = tuple(SDS(w.shape, F32) for w in ws)
    res = pl.pallas_call(body, name=name, out_shape=shapes * 3)(*ws, *gs, *ms, *vs)
    return res[:n], res[n:2 * n], res[2 * n:]


def _local_step(x2, target, rel_table, first_weights, b_in, conv_b, conv_ln_g, conv_ln_b, attn_norm_g,
                conv_norm_g, late_weights, ln1_g, ln1_b, ffn_conv_b, ln2_g, ln2_b, ship_ffn_grads, ship_tail):
    buckets = jnp.asarray(_bucket_maps())
    bias = _bias_table(rel_table, buckets)
    w_in_t, conv_w, ffn_conv_w = first_weights(bias)

    qkv, ag = _proj_in(x2, w_in_t, b_in)
    attn, lse = _attn_fwd(qkv, bias)
    u1 = _conv_fwd(ag, conv_w, conv_b)
    mixed = _mix_fwd(attn, u1, attn_norm_g, conv_norm_g, conv_ln_g, conv_ln_b)
    w_out, w_up, w_down = late_weights(mixed)
    xh1, rstd1, x1b = _out_proj_ln1(mixed, w_out, x2, ln1_g, ln1_b)
    up, gv, act = _ffn_up(x1b, w_up, ffn_conv_w, ffn_conv_b)
    dz2, loss, g_ln2_g, g_ln2_b = _ffn_down_loss(act, w_down, xh1, ln1_g, ln1_b, ln2_g, ln2_b, target)

    dpre, cs_g, cs_v, gfw_g, gfw_v = _ffn_down_bwd(dz2, w_down, gv, up, ffn_conv_w)
    g_w_down = _mm_tn(act, dz2, DFF // 2, 512, "grad_w_down")
    dz1, g_ln1_g, g_ln1_b = _ffn_up_bwd_ln1(dpre, w_up, dz2, xh1, rstd1, ln1_g)
    g_w_out = _mm_tn(mixed, dz1, D, 512, "grad_w_out")
    zero = ship_ffn_grads(g_w_down, _grad_w_up(dpre, x1b), g_w_out)
    dattn, du1, g_an, g_cn, g_clg, g_clb, g_cb = _mix_bwd(
        dz1, w_out, attn, u1, attn_norm_g + zero, conv_norm_g, conv_ln_g, conv_ln_b)
    dag, cs_ag, g_conv_w = _conv_bwd(du1, ag, conv_w)
    dq, dk, dv, cs_q, cs_k, cs_v2, dbias = _attn_bwd(qkv, attn, lse, dattn, bias)
    g_rel = _rel_table_grad(dbias, buckets)
    pieces = [dq, dk, dv, dag]
    g_w_in_t = _grad_w_in(pieces, x2)

    grads = dict(
        rel_table=g_rel,
        b_in=jnp.concatenate([cs_q, cs_k, cs_v2, cs_ag], axis=1),
        conv_b=g_cb, conv_ln_g=g_clg, conv_ln_b=g_clb, attn_norm_g=g_an, conv_norm_g=g_cn,
        ln1_g=g_ln1_g, ln1_b=g_ln1_b,
        ffn_conv_b=jnp.concatenate([cs_g, cs_v], axis=1),
        ln2_g=g_ln2_g, ln2_b=g_ln2_b,
        conv_w=g_conv_w[:CK],
        ffn_conv_w=jnp.concatenate([gfw_g, gfw_v], axis=1),
    )
    grads["loss"] = loss
    zero11 = ship_tail(g_w_in_t, grads)
    grad_x = _grad_x(pieces, w_in_t, dz1, zero11)
    return loss, grad_x


SMALL = (("rel_table", (NBUCKET, NH)), ("b_in", (1, INW)), ("conv_b", (1, CW)), ("conv_ln_g", (1, CW)),
         ("conv_ln_b", (1, CW)), ("attn_norm_g", (1, AW)), ("conv_norm_g", (1, CW)), ("ln1_g", (1, D)),
         ("ln1_b", (1, D)), ("ffn_conv_b", (1, 2 * DFF)), ("ln2_g", (1, D)), ("ln2_b", (1, D)))
SHARDED_SMALL = (("conv_w", (CK, CW)), ("ffn_conv_w", (FK, 2 * DFF)))


def _pack(parts):
    flat = jnp.concatenate([p.reshape(-1) for p in parts])
    tile = 8 * PACK_LANES
    pad = (-flat.shape[0]) % tile
    return jnp.pad(flat, (0, pad)).reshape(-1, PACK_LANES)


def _unpack(packed, specs):
    flat = packed.reshape(-1)
    out, off = {}, 0
    for name, shp in specs:
        size = int(np.prod(shp))
        out[name] = flat[off:off + size].reshape(shp)
        off += size
    return out


def kernel(x, rel_table, w_in, b_in, conv_w, conv_b, conv_ln_g, conv_ln_b, attn_norm_g, conv_norm_g, w_out, ln1_g, ln1_b, w_up, ffn_conv_w, ffn_conv_b, w_down, ln2_g, ln2_b, loss_target, m_rel_table, m_w_in, m_b_in, m_conv_w, m_conv_b, m_conv_ln_g, m_conv_ln_b, m_attn_norm_g, m_conv_norm_g, m_w_out, m_ln1_g, m_ln1_b, m_w_up, m_ffn_conv_w, m_ffn_conv_b, m_w_down, m_ln2_g, m_ln2_b, v_rel_table, v_w_in, v_b_in, v_conv_w, v_conv_b, v_conv_ln_g, v_conv_ln_b, v_attn_norm_g, v_conv_norm_g, v_w_out, v_ln1_g, v_ln1_b, v_w_up, v_ffn_conv_w, v_ffn_conv_b, v_w_down, v_ln2_g, v_ln2_b):
    given = dict(locals())
    me = 4 * lax.axis_index("x") + 2 * lax.axis_index("y") + lax.axis_index("c")

    cols = lambda a: a.transpose(1, 0, 2).reshape(a.shape[1], NDEV * a.shape[2])
    rows = lambda a: a.reshape(NDEV * a.shape[1], a.shape[2])
    stack = lambda a: a.reshape(NDEV, a.shape[0] // NDEV, a.shape[1])

    small_specs = SMALL + SHARDED_SMALL
    packed_specs = small_specs + (("loss", (1, 1)),)
    grad, delta, new_m, new_v = {}, {}, {}, {}

    def adamw_big(n, g2d):
        shp = given[n].shape
        two = lambda a: a.reshape(shp[-2], shp[-1])
        grad[n] = g2d.reshape(shp)
        d_, m_, v_ = _adamw(two(given[n]), g2d, two(given["m_" + n]), two(given["v_" + n]), "adamw_" + n)
        delta[n], new_m[n], new_v[n] = d_.reshape(shp), m_.reshape(shp), v_.reshape(shp)
        return d_

    first_state, zero0 = _exchange_start(
        [(w_in[0].T.astype(BF16), "gather"), (conv_w[0], "gather"), (ffn_conv_w[0], "gather")], "gather_first_start")

    def first_weights(after):
        lands = _exchange_wait(first_state, after, "gather_first_wait")
        return rows(lands[0]), cols(lands[1]), cols(lands[2])

    late_state, zero1 = _exchange_start(
        [(w_out[0].astype(BF16), "gather"), (w_up[0].T.astype(BF16) + zero0.astype(BF16), "gather"),
         (w_down[0].astype(BF16), "gather")], "gather_late_start")

    def late_weights(after):
        return [rows(l) for l in _exchange_wait(late_state, after, "gather_late_wait")]

    shipped = {}

    def ship_ffn_grads(g_w_down, g_w_up_t, g_w_out):
        shipped["ffn"], zero2 = _exchange_start(
            [(stack(a), "scatter") for a in (g_w_down, g_w_up_t, g_w_out)], "ffn_grads_start")
        return zero2

    def ship_tail(g_w_in_t, small_grads):
        shipped["tail"], zero3 = _exchange_start(
            [(stack(g_w_in_t), "scatter"), (_pack([small_grads[n] for n, _ in packed_specs]), "gather")],
            "tail_grads_start")
        return zero3.reshape(1, 1)

    loss, grad_x = _local_step(
        x.reshape(T, D), loss_target.reshape(T, D), rel_table + zero1, first_weights, b_in, conv_b, conv_ln_g,
        conv_ln_b, attn_norm_g, conv_norm_g, late_weights, ln1_g, ln1_b, ffn_conv_b,
        ln2_g, ln2_b, ship_ffn_grads, ship_tail)

    got_down, got_up, got_out = _exchange_wait(shipped["ffn"], grad_x, "ffn_grads_wait")
    adamw_big("w_down", _sum8(got_down, "sum_w_down"))
    adamw_big("w_up", _sum8(got_up, "sum_w_up").T)
    last = adamw_big("w_out", _sum8(got_out, "sum_w_out"))

    got_in, got_small = _exchange_wait(shipped["tail"], last, "tail_grads_wait")
    adamw_big("w_in", _sum8(got_in, "sum_w_in").T)
    small = _unpack(_sum8(got_small, "sum_small"), packed_specs)
    small["conv_w"] = lax.dynamic_slice_in_dim(small["conv_w"], me * (CW // NDEV), CW // NDEV, axis=1)
    small["ffn_conv_w"] = lax.dynamic_slice_in_dim(small["ffn_conv_w"], me * (2 * DFF // NDEV), 2 * DFF // NDEV, axis=1)
    names = [n for n, _ in small_specs]
    two = lambda a: a.reshape(a.shape[-2], a.shape[-1])
    ds, nms, nvs = _adamw_many([two(given[n]) for n in names], [small[n] for n in names],
                               [two(given["m_" + n]) for n in names], [two(given["v_" + n]) for n in names], "adamw_small")
    for n, d_, m_, v_ in zip(names, ds, nms, nvs):
        shp = given[n].shape
        grad[n], delta[n], new_m[n], new_v[n] = small[n].reshape(shp), d_.reshape(shp), m_.reshape(shp), v_.reshape(shp)

    order = ("rel_table", "w_in", "b_in", "conv_w", "conv_b", "conv_ln_g", "conv_ln_b", "attn_norm_g",
             "conv_norm_g", "w_out", "ln1_g", "ln1_b", "w_up", "ffn_conv_w", "ffn_conv_b", "w_down", "ln2_g", "ln2_b")
    return (small["loss"][0, 0], grad_x.reshape(BL, S, D), *[grad[n] for n in order], *[delta[n] for n in order],
            *[new_m[n] for n in order], *[new_v[n] for n in order])
```

```python
import math

import numpy as np
import jax
import jax.numpy as jnp
from jax import lax
from jax.experimental import pallas as pl
from jax.experimental.pallas import tpu as pltpu

F32 = jnp.float32
BF16 = jnp.bfloat16
SDS = jax.ShapeDtypeStruct

NDEV = 8
D = 1024
S = 2048
BL = 2
T = BL * S
NH = 12
HD = 64
AW = NH * HD
CW = D - AW
INW = 3 * AW + 2 * CW
CK = 31
DFF = 2816
FK = 3
BLK = 128
NBUCKET = 32
BRANCHES = ((128, 1), (512, 4), (2048, 16))
ALPHA = 2.0 ** 0.25
LN_EPS = 1e-5
NEG_INF = -1e30
LR, B1, B2, AEPS, WD, STEP = 0.001, 0.9, 0.999, 1e-08, 0.01, 10

TM = 512
FT = 1408
NFT = DFF // FT
TMF = 256
PACK_LANES = 128
GRAD_WIRE = BF16

assert all(w // d == BLK for w, d in BRANCHES)


def _dot(a, b):
    return jnp.dot(a, b, preferred_element_type=F32)


def _dot_nt(a, b):
    return lax.dot_general(a, b, (((1,), (1,)), ((), ())), preferred_element_type=F32)


def _dot_tn(a, b):
    return lax.dot_general(a, b, (((0,), (0,)), ((), ())), preferred_element_type=F32)


def _rowmean(v):
    return jnp.mean(v, axis=-1, keepdims=True)


def _colsum(v):
    return jnp.sum(v, axis=0, keepdims=True)


def _sigmoid(v):
    return jax.nn.sigmoid(v)


_HBM = pl.BlockSpec(memory_space=pltpu.HBM)
_SEM = pl.BlockSpec(memory_space=pltpu.SEMAPHORE)
_EFFECT = pltpu.SideEffectType.DATAFLOW_SIDE_EFFECTING


def _peer_of(k):
    x, y, c = lax.axis_index("x"), lax.axis_index("y"), lax.axis_index("c")
    px = 1 - x if k & 4 else x
    py = 1 - y if k & 2 else y
    pc = 1 - c if k & 1 else c
    return (px, py, pc), 4 * px + 2 * py + pc


def _split_copies(kinds, ins, lands, send_sems, recv_sems, started):
    me = 4 * lax.axis_index("x") + 2 * lax.axis_index("y") + lax.axis_index("c")
    out = []
    for k in range(1, NDEV):
        dev, pid = _peer_of(k)
        for i, kind in enumerate(kinds):
            src = ins[i] if kind == "gather" else ins[i].at[pid]
            dst = lands[i].at[me] if started else lands[i].at[pid]
            slot = i * (NDEV - 1) + k - 1
            out.append(pltpu.make_async_remote_copy(
                src_ref=src, dst_ref=dst, send_sem=send_sems.at[slot], recv_sem=recv_sems.at[slot],
                device_id=dev, device_id_type=pl.DeviceIdType.MESH))
    return out


def _exchange_start(items, name):
    n = len(items)
    kinds = [k for _, k in items]
    srcs = [pltpu.with_memory_space_constraint(a, pltpu.HBM) for a, _ in items]
    lands = []
    for a, k in items:
        shp = (NDEV,) + tuple(a.shape) if k == "gather" else tuple(a.shape)
        lands.append(pltpu.with_memory_space_constraint(lax.empty(shp, a.dtype), pltpu.HBM))

    def body(*refs):
        ins, land_refs = refs[:n], refs[n:2 * n]
        send_sems, recv_sems, own_sems = refs[2 * n:2 * n + 3]
        token = refs[-1]
        for cp in _own_copies(kinds, ins, land_refs, own_sems):
            cp.start()
        for cp in _split_copies(kinds, ins, land_refs, send_sems, recv_sems, True):
            cp.start()
        token[...] = jnp.zeros_like(token)

    sems = pltpu.SemaphoreType.DMA((n * (NDEV - 1),))
    res = pl.pallas_call(
        body, name=name,
        out_shape=(sems, sems, pltpu.SemaphoreType.DMA((n,)),
                   *[pltpu.HBM(a.shape, a.dtype) for a in srcs + lands], SDS((8, 128), F32)),
        in_specs=[_HBM] * (2 * n),
        out_specs=(_SEM, _SEM, _SEM, *[_HBM] * (2 * n), pl.BlockSpec(memory_space=pltpu.VMEM)),
        input_output_aliases={i: 3 + i for i in range(2 * n)},
        compiler_params=pltpu.CompilerParams(has_side_effects=_EFFECT),
    )(*srcs, *lands)
    return (kinds, res[0], res[1], res[2], list(res[3:3 + n]), list(res[3 + n:3 + 2 * n])), res[-1][0, 0]


def _own_copies(kinds, ins, lands, own_sems):
    me = 4 * lax.axis_index("x") + 2 * lax.axis_index("y") + lax.axis_index("c")
    return [pltpu.make_async_copy(ins[i] if kind == "gather" else ins[i].at[me], lands[i].at[me], own_sems.at[i])
            for i, kind in enumerate(kinds)]


def _exchange_wait(state, after, name):
    kinds, send_sems, recv_sems, own_sems, srcs, lands = state
    n = len(kinds)

    def body(*refs):
        ins, land_refs = refs[:n], refs[n:2 * n]
        s_sems, r_sems, o_sems = refs[2 * n:2 * n + 3]
        for cp in _split_copies(kinds, ins, land_refs, s_sems, r_sems, False):
            cp.wait_send()
            cp.wait_recv()
        for cp in _own_copies(kinds, ins, land_refs, o_sems):
            cp.wait()

    res = pl.pallas_call(
        body, name=name,
        out_shape=tuple(pltpu.HBM(a.shape, a.dtype) for a in srcs + lands),
        in_specs=[_HBM] * (2 * n) + [_SEM, _SEM, _SEM, pl.BlockSpec(memory_space=pl.ANY)],
        out_specs=tuple([_HBM] * (2 * n)),
        input_output_aliases={i: i for i in range(2 * n)},
        compiler_params=pltpu.CompilerParams(has_side_effects=_EFFECT),
    )(*srcs, *lands, send_sems, recv_sems, own_sems, after)
    return list(res[n:])


def _proj_in(x2, w_in_t, b_in):
    nq = 3 * AW

    def body(x_ref, w_ref, b_ref, qkv_ref, ag_ref):
        xb = x_ref[...].astype(BF16)
        qkv_ref[...] = (_dot_nt(xb, w_ref[pl.ds(0, nq), :]) + b_ref[:, :nq]).astype(BF16)
        ag_ref[...] = _dot_nt(xb, w_ref[pl.ds(nq, 2 * CW), :]) + b_ref[:, nq:]

    return pl.pallas_call(
        body, name="proj_in", grid=(T // TM,),
        in_specs=[pl.BlockSpec((TM, D), lambda m: (m, 0)), pl.BlockSpec((INW, D), lambda m: (0, 0)),
                  pl.BlockSpec((1, INW), lambda m: (0, 0))],
        out_specs=(pl.BlockSpec((TM, nq), lambda m: (m, 0)), pl.BlockSpec((TM, 2 * CW), lambda m: (m, 0))),
        out_shape=(SDS((T, nq), BF16), SDS((T, 2 * CW), F32)),
    )(x2, w_in_t, b_in)


def _grad_x(pieces, w_in_t, dz1, zero):
    widths = [p.shape[1] for p in pieces]

    def body(*refs):
        p_refs = refs[:len(pieces)]
        w_ref, dz_ref, z_ref, o_ref = refs[len(pieces):]
        acc = ALPHA * dz_ref[...] + z_ref[...]
        r0 = 0
        for p_ref, wd in zip(p_refs, widths):
            acc = acc + _dot(p_ref[...], w_ref[pl.ds(r0, wd), :])
            r0 += wd
        o_ref[...] = acc

    row = pl.BlockSpec((TM, D), lambda m: (m, 0))
    return pl.pallas_call(
        body, name="grad_x", grid=(T // TM,),
        in_specs=[pl.BlockSpec((TM, wd), lambda m: (m, 0)) for wd in widths]
        + [pl.BlockSpec((INW, D), lambda m: (0, 0)), row, pl.BlockSpec((1, 1), lambda m: (0, 0))],
        out_specs=row,
        out_shape=SDS((T, D), F32),
    )(*pieces, w_in_t, dz1, zero)


def _grad_w_in(pieces, x2):
    widths = [p.shape[1] for p in pieces]
    tk = 512
    nk = T // tk

    def body(*refs):
        p_refs = refs[:len(pieces)]
        x_ref, o_ref, acc = refs[len(pieces):]
        k = pl.program_id(0)

        @pl.when(k == 0)
        def _():
            acc[...] = jnp.zeros_like(acc)

        xb = x_ref[...].astype(BF16)
        r0 = 0
        for p_ref, wd in zip(p_refs, widths):
            acc[pl.ds(r0, wd), :] += _dot_tn(p_ref[...], xb)
            r0 += wd

        @pl.when(k == nk - 1)
        def _():
            o_ref[...] = acc[...].astype(o_ref.dtype)

    return pl.pallas_call(
        body, name="grad_w_in", grid=(nk,),
        in_specs=[pl.BlockSpec((tk, wd), lambda k: (k, 0)) for wd in widths] + [pl.BlockSpec((tk, D), lambda k: (k, 0))],
        out_specs=pl.BlockSpec((INW, D), lambda k: (0, 0)),
        out_shape=SDS((INW, D), GRAD_WIRE),
        scratch_shapes=[pltpu.VMEM((INW, D), F32)],
    )(*pieces, x2)


def _mm_tn(a, b, tn, tk, name):
    t_, na = a.shape
    nb = b.shape[1]
    nk = t_ // tk

    def body(a_ref, b_ref, o_ref, acc):
        k = pl.program_id(1)

        @pl.when(k == 0)
        def _():
            acc[...] = jnp.zeros_like(acc)

        acc[...] += _dot_tn(a_ref[...].astype(BF16), b_ref[...].astype(BF16))

        @pl.when(k == nk - 1)
        def _():
            o_ref[...] = acc[...].astype(o_ref.dtype)

    return pl.pallas_call(
        body, name=name, grid=(na // tn, nk),
        in_specs=[pl.BlockSpec((tk, tn), lambda n, k: (k, n)),
                  pl.BlockSpec((tk, nb), lambda n, k: (k, 0))],
        out_specs=pl.BlockSpec((tn, nb), lambda n, k: (n, 0)),
        out_shape=SDS((na, nb), GRAD_WIRE),
        scratch_shapes=[pltpu.VMEM((tn, nb), F32)],
    )(a, b)


def _bucket_maps():
    qi = np.arange(BLK)[:, None]
    kj = np.arange(2 * BLK)[None, :]
    steps = np.maximum(qi + BLK - kj, 0)
    exact = NBUCKET // 2
    maps = []
    for _, dil in BRANCHES:
        dist = steps * dil
        d_f = np.maximum(dist, 1).astype(np.float32)
        large = exact + (np.log(d_f / np.float32(exact)) / np.float32(math.log(S / exact))
                         * np.float32(NBUCKET - exact)).astype(np.int32)
        large = np.minimum(large, NBUCKET - 1)
        maps.append(np.where(dist < exact, dist, large).astype(np.int32))
    return np.stack(maps)


def _bias_table(rel_table, buckets):
    def body(t_ref, b_ref, o_ref):
        bk = b_ref[0]
        for h in range(NH):
            acc = jnp.zeros((BLK, 2 * BLK), F32)
            for k in range(NBUCKET):
                acc = jnp.where(bk == k, t_ref[k, h], acc)
            o_ref[0, h] = acc

    return pl.pallas_call(
        body, name="bias_table", grid=(len(BRANCHES),),
        in_specs=[pl.BlockSpec(memory_space=pltpu.SMEM),
                  pl.BlockSpec((1, BLK, 2 * BLK), lambda i: (i, 0, 0))],
        out_specs=pl.BlockSpec((1, NH, BLK, 2 * BLK), lambda i: (i, 0, 0, 0)),
        out_shape=SDS((len(BRANCHES), NH, BLK, 2 * BLK), F32),
    )(rel_table, buckets)


def _rel_table_grad(dbias, buckets):
    def body(d_ref, b_ref, o_ref):
        h = pl.program_id(0)
        for k in range(NBUCKET):
            tot = jnp.zeros((1, 1), F32)
            for br in range(len(BRANCHES)):
                sel = jnp.where(b_ref[br] == k, d_ref[br, 0], 0.0)
                tot = tot + jnp.sum(jnp.sum(sel, axis=1, keepdims=True), axis=0, keepdims=True)
            o_ref[0, :, pl.ds(k, 1)] = tot

    out = pl.pallas_call(
        body, name="rel_table_grad", grid=(NH,),
        in_specs=[pl.BlockSpec((len(BRANCHES), 1, BLK, 2 * BLK), lambda h: (0, h, 0, 0)),
                  pl.BlockSpec((len(BRANCHES), BLK, 2 * BLK), lambda h: (0, 0, 0))],
        out_specs=pl.BlockSpec((1, 1, NBUCKET), lambda h: (h, 0, 0)),
        out_shape=SDS((NH, 1, NBUCKET), F32),
    )(dbias, buckets)
    return out.reshape(NH, NBUCKET).T


PADK = BLK
SCALE = 1.0 / math.sqrt(HD)
ATTN_UNROLL = 16


def _branch_geometry(br):
    dil = BRANCHES[br][1]
    sub = S // dil
    return dil, sub, sub // BLK


def _token_rows(br, i):
    dil, _, nblk = _branch_geometry(br)
    if dil == 1:
        return pl.ds(pl.multiple_of(i * BLK, BLK), BLK), i
    r = lax.shift_right_logical(i, nblk.bit_length() - 1)
    n = lax.bitwise_and(i, nblk - 1)
    return pl.ds(r + dil * BLK * n, BLK, stride=dil), n


def _sub_layout_loop(br, step):
    dil, sub, _ = _branch_geometry(br)
    rows = min(sub, 256)
    nchunk = sub // rows

    def it_step(it, carry):
        if dil == 1:
            src = pl.ds(pl.multiple_of(it * rows, rows), rows)
        else:
            r = lax.shift_right_logical(it, nchunk.bit_length() - 1)
            src = pl.ds(r + dil * rows * lax.bitwise_and(it, nchunk - 1), rows, stride=dil)
        step(src, pl.multiple_of(it * rows, BLK), rows)
        return carry

    lax.fori_loop(0, dil * nchunk, it_step, 0)


def _masked_bias(bias_ref, bm):
    qi = lax.broadcasted_iota(jnp.int32, (BLK, 2 * BLK), 0)
    kj = lax.broadcasted_iota(jnp.int32, (BLK, 2 * BLK), 1)
    first = jnp.logical_and(kj >= BLK, kj - BLK <= qi)
    valid = jnp.logical_or(first, jnp.logical_and(kj < BLK, kj >= qi))
    for br in range(len(BRANCHES)):
        for j in range(2):
            b = bias_ref[br, j]
            bm[br, 1, pl.ds(j * BLK, BLK), :] = jnp.where(valid, b, NEG_INF)
            bm[br, 0, pl.ds(j * BLK, BLK), :] = jnp.where(first, b, NEG_INF)


def _head_split(fn):
    def split(t):
        h0 = lax.broadcasted_iota(jnp.int32, t.shape, 1) < HD
        t = fn(t)
        return jnp.where(h0, t, 0.0).astype(BF16), jnp.where(h0, 0.0, t).astype(BF16)
    return split


def _attn_fwd(qkv, bias):
    nbr = len(BRANCHES)

    def body(q_ref, k_ref, v_ref, bias_ref, o_ref, lse_ref, qf, kf, vf, qs0, qs1, ks, vs, bm, ob, mb, lb):
        qf[...] = q_ref[...].astype(F32)
        kf[...] = k_ref[...].astype(F32)
        vf[...] = v_ref[...].astype(F32)
        _masked_bias(bias_ref, bm)
        ks[pl.ds(0, PADK), :] = jnp.zeros((PADK, BLK), BF16)
        vs[pl.ds(0, PADK), :] = jnp.zeros((PADK, BLK), BF16)
        head0 = lax.broadcasted_iota(jnp.int32, (BLK, BLK), 1) < HD
        split_q = _head_split(lambda t: t * SCALE)

        for br in range(nbr):
            nblk = _branch_geometry(br)[2]

            def stage(src, off, rows):
                qs0[pl.ds(off, rows), :], qs1[pl.ds(off, rows), :] = split_q(qf[src, :])
                ks[pl.ds(PADK + off, rows), :] = kf[src, :].astype(BF16)
                vs[pl.ds(PADK + off, rows), :] = vf[src, :].astype(BF16)

            _sub_layout_loop(br, stage)

            def blk(i, carry, br=br, nblk=nblk):
                base = pl.multiple_of(i * BLK, BLK)
                rows, n = _token_rows(br, i)
                q01 = jnp.concatenate([qs0[pl.ds(base, BLK), :], qs1[pl.ds(base, BLK), :]], axis=0)
                if nblk > 1:
                    kcat = ks[pl.ds(base, 2 * BLK), :]
                    vcat = vs[pl.ds(base, 2 * BLK), :]
                    s = _dot_nt(q01, kcat) + bm[br, jnp.minimum(n, 1)]
                else:
                    kcat = ks[pl.ds(PADK + base, BLK), :]
                    vcat = vs[pl.ds(PADK + base, BLK), :]
                    s = _dot_nt(q01, kcat) + bm[br, 0, :, BLK:]
                mx = jnp.max(s, axis=-1, keepdims=True)
                p = jnp.exp(s - mx)
                ls = jnp.sum(p, axis=-1, keepdims=True)
                o = _dot(p.astype(BF16), vcat)
                ob[br, rows, :] = jnp.where(head0, o[:BLK], o[BLK:])
                mb[br, rows, :] = jnp.where(head0, mx[:BLK], mx[BLK:])
                lb[br, rows, :] = jnp.where(head0, ls[:BLK], ls[BLK:])
                return carry

            lax.fori_loop(0, 16, blk, 0, unroll=ATTN_UNROLL)

        def merge(i, carry):
            rows = pl.ds(pl.multiple_of(i * 256, 256), 256)
            m_all = jnp.maximum(jnp.maximum(mb[0, rows, :], mb[1, rows, :]), mb[2, rows, :])
            num = jnp.zeros((256, BLK), F32)
            den = jnp.zeros((256, BLK), F32)
            for br in range(nbr):
                c = jnp.exp(mb[br, rows, :] - m_all)
                num = num + ob[br, rows, :] * c
                den = den + lb[br, rows, :] * c
            o_ref[rows, :] = num / den
            lse_ref[rows, :] = m_all + jnp.log(den)
            return carry

        lax.fori_loop(0, S // 256, merge, 0)

    npair = NH // 2
    blk_spec = lambda off: pl.BlockSpec((S, BLK), lambda b, hp: (b, off + hp))
    return pl.pallas_call(
        body, name="attn_fwd", grid=(BL, npair),
        in_specs=[blk_spec(0), blk_spec(npair), blk_spec(2 * npair),
                  pl.BlockSpec((nbr, 2, BLK, 2 * BLK), lambda b, hp: (0, hp, 0, 0))],
        out_specs=(blk_spec(0), blk_spec(0)),
        out_shape=(SDS((T, AW), F32), SDS((T, AW), F32)),
        scratch_shapes=[pltpu.VMEM((S, BLK), F32)] * 3 + [pltpu.VMEM((S, BLK), BF16)] * 2
        + [pltpu.VMEM((PADK + S, BLK), BF16)] * 2 + [pltpu.VMEM((nbr, 2, 2 * BLK, 2 * BLK), F32)]
        + [pltpu.VMEM((nbr, S, BLK), F32)] * 3,
    )(qkv, qkv, qkv, bias)


def _attn_bwd(qkv, attn, lse, dattn, bias):
    nbr = len(BRANCHES)

    def body(q_ref, k_ref, v_ref, o_ref, lse_ref, do_ref, bias_ref,
             dq_ref, dk_ref, dv_ref, sq_ref, sk_ref, sv_ref, db_ref,
             qf, kf, vf, dl, dqa, dka, dva, qs0, qs1, ds0, ds1, ks, vs, dks, dvs, bm):
        b = pl.program_id(1)
        qf[...] = q_ref[...].astype(F32)
        kf[...] = k_ref[...].astype(F32)
        vf[...] = v_ref[...].astype(F32)
        dqa[...] = jnp.zeros_like(dqa)
        dka[...] = jnp.zeros_like(dka)
        dva[...] = jnp.zeros_like(dva)
        _masked_bias(bias_ref, bm)
        ks[pl.ds(0, PADK), :] = jnp.zeros((PADK, BLK), BF16)
        vs[pl.ds(0, PADK), :] = jnp.zeros((PADK, BLK), BF16)
        head0 = lax.broadcasted_iota(jnp.int32, (BLK, BLK), 1) < HD
        split_q = _head_split(lambda t: t * SCALE)
        split_do = _head_split(lambda t: t)

        @pl.when(b == 0)
        def _():
            db_ref[...] = jnp.zeros_like(db_ref)
            sq_ref[...] = jnp.zeros_like(sq_ref)
            sk_ref[...] = jnp.zeros_like(sk_ref)
            sv_ref[...] = jnp.zeros_like(sv_ref)

        def delta(i, carry):
            rows = pl.ds(pl.multiple_of(i * 256, 256), 256)
            prod = do_ref[rows, :] * o_ref[rows, :]
            h0 = lax.broadcasted_iota(jnp.int32, (256, BLK), 1) < HD
            d0 = jnp.sum(jnp.where(h0, prod, 0.0), axis=-1, keepdims=True)
            d1 = jnp.sum(jnp.where(h0, 0.0, prod), axis=-1, keepdims=True)
            dl[rows, :] = jnp.where(h0, d0, d1)
            return carry

        lax.fori_loop(0, S // 256, delta, 0)

        for br in range(nbr):
            nblk = _branch_geometry(br)[2]

            def stage(src, off, rows):
                qs0[pl.ds(off, rows), :], qs1[pl.ds(off, rows), :] = split_q(qf[src, :])
                ds0[pl.ds(off, rows), :], ds1[pl.ds(off, rows), :] = split_do(do_ref[src, :])
                ks[pl.ds(PADK + off, rows), :] = kf[src, :].astype(BF16)
                vs[pl.ds(PADK + off, rows), :] = vf[src, :].astype(BF16)

            _sub_layout_loop(br, stage)
            dks[...] = jnp.zeros_like(dks)
            dvs[...] = jnp.zeros_like(dvs)

            def blk(i, carry, br=br, nblk=nblk):
                base = pl.multiple_of(i * BLK, BLK)
                rows, n = _token_rows(br, i)
                q01 = jnp.concatenate([qs0[pl.ds(base, BLK), :], qs1[pl.ds(base, BLK), :]], axis=0)
                do01 = jnp.concatenate([ds0[pl.ds(base, BLK), :], ds1[pl.ds(base, BLK), :]], axis=0)
                lse_b = lse_ref[rows, :]
                dl_b = dl[rows, :]
                lse01 = jnp.concatenate([lse_b[:, 0:1], lse_b[:, HD:HD + 1]], axis=0)
                dl01 = jnp.concatenate([dl_b[:, 0:1], dl_b[:, HD:HD + 1]], axis=0)
                if nblk > 1:
                    krows = pl.ds(base, 2 * BLK)
                    bias_m = bm[br, jnp.minimum(n, 1)]
                else:
                    krows = pl.ds(PADK + base, BLK)
                    bias_m = bm[br, 0, :, BLK:]
                kcat = ks[krows, :]
                vcat = vs[krows, :]
                p = jnp.exp(_dot_nt(q01, kcat) + bias_m - lse01)
                dsv = p * (_dot_nt(do01, vcat) - dl01)
                if nblk > 1:
                    db_ref[br, 0] += dsv[:BLK]
                    db_ref[br, 1] += dsv[BLK:]
                else:
                    db_ref[br, 0, :, BLK:] += dsv[:BLK]
                    db_ref[br, 1, :, BLK:] += dsv[BLK:]
                dsb = dsv.astype(BF16)
                dq01 = _dot(dsb, kcat)
                dqa[rows, :] = dqa[rows, :] + jnp.where(head0, dq01[:BLK], dq01[BLK:])
                dks[krows, :] = dks[krows, :] + _dot_tn(dsb, q01)
                dvs[krows, :] = dvs[krows, :] + _dot_tn(p.astype(BF16), do01)
                return carry

            lax.fori_loop(0, 16, blk, 0, unroll=ATTN_UNROLL)

            def fold(src, off, rows):
                dka[src, :] = dka[src, :] + dks[pl.ds(PADK + off, rows), :]
                dva[src, :] = dva[src, :] + dvs[pl.ds(PADK + off, rows), :]

            _sub_layout_loop(br, fold)

        def flush(i, carry):
            rows = pl.ds(pl.multiple_of(i * 256, 256), 256)
            for acc, out, cs, mul in ((dqa, dq_ref, sq_ref, SCALE), (dka, dk_ref, sk_ref, 1.0), (dva, dv_ref, sv_ref, 1.0)):
                val = acc[rows, :] * mul
                out[rows, :] = val.astype(BF16)
                cs[...] += _colsum(val)
            return carry

        lax.fori_loop(0, S // 256, flush, 0)

    npair = NH // 2
    blk_spec = lambda off: pl.BlockSpec((S, BLK), lambda hp, b: (b, off + hp))
    sum_spec = pl.BlockSpec((1, BLK), lambda hp, b: (0, hp))
    return pl.pallas_call(
        body, name="attn_bwd", grid=(npair, BL),
        in_specs=[blk_spec(0), blk_spec(npair), blk_spec(2 * npair), blk_spec(0), blk_spec(0), blk_spec(0),
                  pl.BlockSpec((nbr, 2, BLK, 2 * BLK), lambda hp, b: (0, hp, 0, 0))],
        out_specs=(blk_spec(0), blk_spec(0), blk_spec(0), sum_spec, sum_spec, sum_spec,
                   pl.BlockSpec((nbr, 2, BLK, 2 * BLK), lambda hp, b: (0, hp, 0, 0))),
        out_shape=(SDS((T, AW), BF16), SDS((T, AW), BF16), SDS((T, AW), BF16),
                   SDS((1, AW), F32), SDS((1, AW), F32), SDS((1, AW), F32),
                   SDS((nbr, NH, BLK, 2 * BLK), F32)),
        scratch_shapes=[pltpu.VMEM((S, BLK), F32)] * 7 + [pltpu.VMEM((S, BLK), BF16)] * 4
        + [pltpu.VMEM((PADK + S, BLK), BF16)] * 2 + [pltpu.VMEM((PADK + S, BLK), F32)] * 2
        + [pltpu.VMEM((nbr, 2, 2 * BLK, 2 * BLK), F32)],
    )(qkv, qkv, qkv, attn, lse, dattn, bias)


CH = 256
PADR = 32


def _tap_phases(offset_of_tap):
    taps = sorted((offset_of_tap(k) % 8, offset_of_tap(k) - offset_of_tap(k) % 8, k) for k in range(CK))
    assert all(lo + CH + ph <= CH + PADR for ph, lo, _ in taps)
    return taps


def _rows_up(win):
    made = {0: win}

    def get(phase):
        if phase not in made:
            made[phase] = pltpu.roll(win, win.shape[0] - phase, 0)
        return made[phase]
    return get


def _conv_fwd(ag, conv_w, conv_b):
    def body(ag_ref, w_ref, b_ref, u1_ref, u0p):
        u0p[pl.ds(0, PADR), :] = jnp.zeros((PADR, CW), F32)

        def glu(i, carry):
            t0 = pl.multiple_of(i * CH, CH)
            a = ag_ref[pl.ds(t0, CH), :CW]
            g = ag_ref[pl.ds(t0, CH), CW:]
            u0p[pl.ds(PADR + t0, CH), :] = a * _sigmoid(g)
            return carry

        lax.fori_loop(0, S // CH, glu, 0)

        def conv(i, carry):
            t0 = pl.multiple_of(i * CH, CH)
            win = u0p[pl.ds(t0, CH + PADR), :]
            acc = jnp.zeros((CH, CW), F32) + b_ref[...]
            up = _rows_up(win)
            for phase, lo, k in _tap_phases(lambda k: PADR - (CK - 1) + k):
                acc = acc + up(phase)[lo:lo + CH, :] * w_ref[k:k + 1, :]
            u1_ref[pl.ds(t0, CH), :] = acc
            return carry

        lax.fori_loop(0, S // CH, conv, 0)

    return pl.pallas_call(
        body, name="conv_fwd", grid=(BL,),
        in_specs=[pl.BlockSpec((S, 2 * CW), lambda b: (b, 0)),
                  pl.BlockSpec((CK, CW), lambda b: (0, 0)),
                  pl.BlockSpec((1, CW), lambda b: (0, 0))],
        out_specs=pl.BlockSpec((S, CW), lambda b: (b, 0)),
        out_shape=SDS((T, CW), F32),
        scratch_shapes=[pltpu.VMEM((S + PADR, CW), F32)],
    )(ag, conv_w, conv_b)


def _conv_post(u1, cg, cb):
    mu = _rowmean(u1)
    uc = u1 - mu
    rstd = lax.rsqrt(_rowmean(uc * uc) + LN_EPS)
    xh = uc * rstd
    u2 = xh * cg + cb
    sg = _sigmoid(u2)
    return xh, rstd, u2, sg, u2 * sg


def _mix_fwd(attn, u1, ga, gc, cg, cb):
    def body(a_ref, u_ref, ga_ref, gc_ref, cg_ref, cb_ref, o_ref):
        a = a_ref[...]
        ra = lax.rsqrt(_rowmean(a * a) + LN_EPS)
        o_ref[:, :AW] = (a * ra * ga_ref[...]).astype(BF16)
        _, _, _, _, u3 = _conv_post(u_ref[...], cg_ref[...], cb_ref[...])
        rc = lax.rsqrt(_rowmean(u3 * u3) + LN_EPS)
        o_ref[:, AW:] = (u3 * rc * gc_ref[...]).astype(BF16)

    vec = lambda w: pl.BlockSpec((1, w), lambda m: (0, 0))
    return pl.pallas_call(
        body, name="mix_fwd", grid=(T // TM,),
        in_specs=[pl.BlockSpec((TM, AW), lambda m: (m, 0)), pl.BlockSpec((TM, CW), lambda m: (m, 0)),
                  vec(AW), vec(CW), vec(CW), vec(CW)],
        out_specs=pl.BlockSpec((TM, D), lambda m: (m, 0)),
        out_shape=SDS((T, D), BF16),
    )(attn, u1, ga, gc, cg, cb)


def _mix_bwd(dz1, w_out, attn, u1, ga, gc, cg, cb):
    def body(dz_ref, w_ref, a_ref, u_ref, ga_ref, gc_ref, cg_ref, cb_ref,
             da_ref, du_ref, g_an, g_cn, g_lg, g_lb, g_cb):
        @pl.when(pl.program_id(0) == 0)
        def _():
            for r in (g_an, g_cn, g_lg, g_lb, g_cb):
                r[...] = jnp.zeros_like(r)

        dm = _dot_nt(dz_ref[...].astype(BF16), w_ref[...])
        a = a_ref[...]
        dna = dm[:, :AW]
        ra = lax.rsqrt(_rowmean(a * a) + LN_EPS)
        g_an[...] += _colsum(dna * a * ra)
        dat = dna * ga_ref[...]
        da_ref[...] = ra * dat - a * (ra * ra * ra) * _rowmean(dat * a)

        xh, rstd, u2, sg, u3 = _conv_post(u_ref[...], cg_ref[...], cb_ref[...])
        dnc = dm[:, AW:]
        rc = lax.rsqrt(_rowmean(u3 * u3) + LN_EPS)
        g_cn[...] += _colsum(dnc * u3 * rc)
        dut = dnc * gc_ref[...]
        du3 = rc * dut - u3 * (rc * rc * rc) * _rowmean(dut * u3)
        du2 = du3 * sg * (1.0 + u2 * (1.0 - sg))
        g_lg[...] += _colsum(du2 * xh)
        g_lb[...] += _colsum(du2)
        dxh = du2 * cg_ref[...]
        du1 = rstd * (dxh - _rowmean(dxh) - xh * _rowmean(dxh * xh))
        g_cb[...] += _colsum(du1)
        du_ref[...] = du1

    vec = lambda w: pl.BlockSpec((1, w), lambda m: (0, 0))
    return pl.pallas_call(
        body, name="mix_bwd", grid=(T // TM,),
        in_specs=[pl.BlockSpec((TM, D), lambda m: (m, 0)), pl.BlockSpec((D, D), lambda m: (0, 0)),
                  pl.BlockSpec((TM, AW), lambda m: (m, 0)),
                  pl.BlockSpec((TM, CW), lambda m: (m, 0)), vec(AW), vec(CW), vec(CW), vec(CW)],
        out_specs=(pl.BlockSpec((TM, AW), lambda m: (m, 0)), pl.BlockSpec((TM, CW), lambda m: (m, 0)),
                   vec(AW), vec(CW), vec(CW), vec(CW), vec(CW)),
        out_shape=(SDS((T, AW), F32), SDS((T, CW), F32),
                   SDS((1, AW), F32), SDS((1, CW), F32), SDS((1, CW), F32), SDS((1, CW), F32), SDS((1, CW), F32)),
    )(dz1, w_out, attn, u1, ga, gc, cg, cb)


def _conv_bwd(du1, ag, conv_w):
    def body(du_ref, ag_ref, w_ref, dag_ref, cs_ref, gw_ref, u0p, dup):
        @pl.when(pl.program_id(0) == 0)
        def _():
            cs_ref[...] = jnp.zeros_like(cs_ref)
            gw_ref[...] = jnp.zeros_like(gw_ref)

        u0p[pl.ds(0, PADR), :] = jnp.zeros((PADR, CW), F32)
        dup[pl.ds(S, PADR), :] = jnp.zeros((PADR, CW), F32)

        def fill(i, carry):
            t0 = pl.multiple_of(i * CH, CH)
            a = ag_ref[pl.ds(t0, CH), :CW]
            g = ag_ref[pl.ds(t0, CH), CW:]
            u0p[pl.ds(PADR + t0, CH), :] = a * _sigmoid(g)
            dup[pl.ds(t0, CH), :] = du_ref[pl.ds(t0, CH), :]
            return carry

        lax.fori_loop(0, S // CH, fill, 0)

        def chunk(i, carry):
            t0 = pl.multiple_of(i * CH, CH)
            d = dup[pl.ds(t0, CH), :]
            win_u = u0p[pl.ds(t0, CH + PADR), :]
            win_d = dup[pl.ds(t0, CH + PADR), :]
            du0 = jnp.zeros((CH, CW), F32)
            up_u = _rows_up(win_u)
            for phase, lo, k in _tap_phases(lambda k: PADR - (CK - 1) + k):
                gw_ref[k:k + 1, :] += _colsum(d * up_u(phase)[lo:lo + CH, :])
            up_d = _rows_up(win_d)
            for phase, lo, k in _tap_phases(lambda k: CK - 1 - k):
                du0 = du0 + up_d(phase)[lo:lo + CH, :] * w_ref[k:k + 1, :]
            a = ag_ref[pl.ds(t0, CH), :CW]
            sg = _sigmoid(ag_ref[pl.ds(t0, CH), CW:])
            da = du0 * sg
            dg = du0 * a * sg * (1.0 - sg)
            dag_ref[pl.ds(t0, CH), :CW] = da.astype(BF16)
            dag_ref[pl.ds(t0, CH), CW:] = dg.astype(BF16)
            cs_ref[:, :CW] += _colsum(da)
            cs_ref[:, CW:] += _colsum(dg)
            return carry

        lax.fori_loop(0, S // CH, chunk, 0)

    return pl.pallas_call(
        body, name="conv_bwd", grid=(BL,),
        in_specs=[pl.BlockSpec((S, CW), lambda b: (b, 0)), pl.BlockSpec((S, 2 * CW), lambda b: (b, 0)),
                  pl.BlockSpec((CK, CW), lambda b: (0, 0))],
        out_specs=(pl.BlockSpec((S, 2 * CW), lambda b: (b, 0)),
                   pl.BlockSpec((1, 2 * CW), lambda b: (0, 0)),
                   pl.BlockSpec((PADR, CW), lambda b: (0, 0))),
        out_shape=(SDS((T, 2 * CW), BF16), SDS((1, 2 * CW), F32), SDS((PADR, CW), F32)),
        scratch_shapes=[pltpu.VMEM((S + PADR, CW), F32), pltpu.VMEM((S + PADR, CW), F32)],
    )(du1, ag, conv_w)


def _layer_norm_fwd(z):
    mu = _rowmean(z)
    zc = z - mu
    rstd = lax.rsqrt(_rowmean(zc * zc) + LN_EPS)
    return zc * rstd, rstd


def _layer_norm_bwd(dy, xh, rstd, g):
    dxh = dy * g
    return rstd * (dxh - _rowmean(dxh) - xh * _rowmean(dxh * xh))


def _out_proj_ln1(mixed, w_out, x2, g1, b1):
    def body(a_ref, w_ref, x_ref, g_ref, b_ref, xh_ref, rstd_ref, x1_ref):
        z = ALPHA * x_ref[...] + _dot(a_ref[...], w_ref[...])
        xh, rstd = _layer_norm_fwd(z)
        xh_ref[...] = xh
        rstd_ref[...] = rstd
        x1_ref[...] = (xh * g_ref[...] + b_ref[...]).astype(BF16)

    vec = pl.BlockSpec((1, D), lambda m: (0, 0))
    row = pl.BlockSpec((TM, D), lambda m: (m, 0))
    return pl.pallas_call(
        body, name="out_proj_ln1", grid=(T // TM,),
        in_specs=[row, pl.BlockSpec((D, D), lambda m: (0, 0)), row, vec, vec],
        out_specs=(row, pl.BlockSpec((TM, 1), lambda m: (m, 0)), row),
        out_shape=(SDS((T, D), F32), SDS((T, 1), F32), SDS((T, D), BF16)),
    )(mixed, w_out, x2, g1, b1)


def _seq_start(m):
    return lax.bitwise_and(m, S // TMF - 1) == 0


def _shift_down(x, before, k):
    rolled = pltpu.roll(x, k, 0)
    row = lax.broadcasted_iota(jnp.int32, before.shape, 0)
    head = jnp.where(row < k, pltpu.roll(before, k, 0), rolled[:8])
    return jnp.concatenate([head, rolled[8:]], axis=0)


def _shift_up(x, after, k):
    n = x.shape[0]
    rolled = pltpu.roll(x, n - k, 0)
    row = lax.broadcasted_iota(jnp.int32, after.shape, 0)
    tail = jnp.where(row >= 8 - k, pltpu.roll(after, 8 - k, 0), rolled[n - 8:])
    return jnp.concatenate([rolled[:n - 8], tail], axis=0)


def _ffn_up(x1b, w_up, fcw, fcb):
    def body(x_ref, wg_ref, wv_ref, cwg_ref, cwv_ref, cbg_ref, cbv_ref, up_ref, gv_ref, act_ref, prev_g, prev_v):
        @pl.when(_seq_start(pl.program_id(1)))
        def _():
            prev_g[...] = jnp.zeros_like(prev_g)
            prev_v[...] = jnp.zeros_like(prev_v)

        x = x_ref[...]
        outs = []
        for w_ref, cw_ref, cb_ref, prev, lo in ((wg_ref, cwg_ref, cbg_ref, prev_g, 0), (wv_ref, cwv_ref, cbv_ref, prev_v, FT)):
            ub = _dot_nt(x, w_ref[...]).astype(BF16)
            up_ref[:, lo:lo + FT] = ub
            u = ub.astype(F32)
            before = prev[...]
            y = (cw_ref[2:3, :] * u + cw_ref[1:2, :] * _shift_down(u, before, 1)
                 + cw_ref[0:1, :] * _shift_down(u, before, 2) + cb_ref[...])
            prev[...] = u[TMF - 8:]
            yb = y.astype(BF16)
            gv_ref[:, lo:lo + FT] = yb
            outs.append(yb.astype(F32))
        gate, val = outs
        act_ref[...] = (gate * _sigmoid(gate) * val).astype(BF16)

    wspec = lambda off: pl.BlockSpec((FT, D), lambda n, m: (n + off, 0))
    cwspec = lambda off: pl.BlockSpec((FK, FT), lambda n, m: (0, n + off))
    cbspec = lambda off: pl.BlockSpec((1, FT), lambda n, m: (0, n + off))
    pair = pl.BlockSpec((TMF, 2 * FT), lambda n, m: (m, n))
    return pl.pallas_call(
        body, name="ffn_up", grid=(NFT, T // TMF),
        in_specs=[pl.BlockSpec((TMF, D), lambda n, m: (m, 0)), wspec(0), wspec(NFT),
                  cwspec(0), cwspec(NFT), cbspec(0), cbspec(NFT)],
        out_specs=(pair, pair, pl.BlockSpec((TMF, FT), lambda n, m: (m, n))),
        out_shape=(SDS((T, 2 * DFF), BF16), SDS((T, 2 * DFF), BF16), SDS((T, DFF), BF16)),
        scratch_shapes=[pltpu.VMEM((8, FT), F32)] * 2,
    )(x1b, w_up, w_up, fcw, fcw, fcb, fcb)


def _ffn_down_loss(act, w_down, xh1, g1, b1, g2, b2, target):
    def body(a_ref, w_ref, xh1_ref, g1_ref, b1_ref, g2_ref, b2_ref, t_ref, dz_ref, loss_ref, gg_ref, gb_ref):
        @pl.when(pl.program_id(0) == 0)
        def _():
            loss_ref[...] = jnp.zeros_like(loss_ref)
            gg_ref[...] = jnp.zeros_like(gg_ref)
            gb_ref[...] = jnp.zeros_like(gb_ref)

        x1 = xh1_ref[...] * g1_ref[...] + b1_ref[...]
        z = ALPHA * x1 + _dot(a_ref[...], w_ref[...])
        xh, rstd = _layer_norm_fwd(z)
        diff = xh * g2_ref[...] + b2_ref[...] - t_ref[...]
        loss_ref[...] += 0.5 * _colsum(_rowmean(diff * diff))
        dout = diff * (1.0 / D)
        gg_ref[...] += _colsum(dout * xh)
        gb_ref[...] += _colsum(dout)
        dz_ref[...] = _layer_norm_bwd(dout, xh, rstd, g2_ref[...])

    vec = pl.BlockSpec((1, D), lambda m: (0, 0))
    row = pl.BlockSpec((TM, D), lambda m: (m, 0))
    return pl.pallas_call(
        body, name="ffn_down_loss", grid=(T // TM,),
        in_specs=[pl.BlockSpec((TM, DFF), lambda m: (m, 0)), pl.BlockSpec((DFF, D), lambda m: (0, 0)),
                  row, vec, vec, vec, vec, row],
        out_specs=(row, pl.BlockSpec((1, 1), lambda m: (0, 0)), vec, vec),
        out_shape=(SDS((T, D), F32), SDS((1, 1), F32), SDS((1, D), F32), SDS((1, D), F32)),
    )(act, w_down, xh1, g1, b1, g2, b2, target)


def _ffn_down_bwd(dz2, w_down, gv, up, fcw):
    tiles = T // TMF

    def body(dz_ref, wd_ref, gv_ref, up_ref, cwg_ref, cwv_ref,
             dpre_ref, csg_ref, csv_ref, gwg_ref, gwv_ref, next_g, next_v):
        step = pl.program_id(1)
        tile = tiles - 1 - step

        @pl.when(step == 0)
        def _():
            for r in (csg_ref, csv_ref, gwg_ref, gwv_ref, next_g, next_v):
                r[...] = jnp.zeros_like(r)

        seq_end = lax.bitwise_and(tile + 1, S // TMF - 1) == 0
        dact = _dot_nt(dz_ref[...].astype(BF16), wd_ref[...])
        gate = gv_ref[:, :FT].astype(F32)
        val = gv_ref[:, FT:].astype(F32)
        sg = _sigmoid(gate)
        gs = gate * sg
        halves = ((dact * val * (sg + gs * (1.0 - sg)), cwg_ref, csg_ref, gwg_ref, next_g, 0),
                  (dact * gs, cwv_ref, csv_ref, gwv_ref, next_v, FT))
        for d0, cw_ref, cs_ref, gw_ref, nxt, lo in halves:
            after = jnp.where(seq_end, 0.0, nxt[...])
            d1 = _shift_up(d0, after, 1)
            d2 = _shift_up(d0, after, 2)
            nxt[...] = d0[:8]
            dpre_ref[:, lo:lo + FT] = (cw_ref[2:3, :] * d0 + cw_ref[1:2, :] * d1 + cw_ref[0:1, :] * d2).astype(BF16)
            cs_ref[...] += _colsum(d0)
            u = up_ref[:, lo:lo + FT].astype(F32)
            for k, dk in enumerate((d2, d1, d0)):
                gw_ref[k:k + 1, :] += _colsum(dk * u)

    cs = pl.BlockSpec((1, FT), lambda n, m: (0, n))
    gw = pl.BlockSpec((FK, FT), lambda n, m: (0, n))
    cwspec = lambda off: pl.BlockSpec((FK, FT), lambda n, m: (0, n + off))
    pair = pl.BlockSpec((TMF, 2 * FT), lambda n, m: (tiles - 1 - m, n))
    return pl.pallas_call(
        body, name="ffn_down_bwd", grid=(NFT, tiles),
        in_specs=[pl.BlockSpec((TMF, D), lambda n, m: (tiles - 1 - m, 0)), pl.BlockSpec((FT, D), lambda n, m: (n, 0)),
                  pair, pair, cwspec(0), cwspec(NFT)],
        out_specs=(pair, cs, cs, gw, gw),
        out_shape=(SDS((T, 2 * DFF), BF16), SDS((1, DFF), F32), SDS((1, DFF), F32),
                   SDS((FK, DFF), F32), SDS((FK, DFF), F32)),
        scratch_shapes=[pltpu.VMEM((8, FT), F32)] * 2,
    )(dz2, w_down, gv, up, fcw, fcw)


def _ffn_up_bwd_ln1(dpre, w_up, dz2, xh1, rstd1, g1):
    def body(a_ref, w_ref, dz2_ref, xh_ref, rstd_ref, g_ref, dz1_ref, gg_ref, gb_ref):
        @pl.when(pl.program_id(0) == 0)
        def _():
            gg_ref[...] = jnp.zeros_like(gg_ref)
            gb_ref[...] = jnp.zeros_like(gb_ref)

        dx1 = ALPHA * dz2_ref[...]
        for n in range(NFT):
            for half in range(2):
                a = a_ref[:, (2 * n + half) * FT:(2 * n + half + 1) * FT]
                w = w_ref[pl.ds((half * NFT + n) * FT, FT), :]
                dx1 = dx1 + _dot(a, w)
        xh = xh_ref[...]
        gg_ref[...] += _colsum(dx1 * xh)
        gb_ref[...] += _colsum(dx1)
        dz1_ref[...] = _layer_norm_bwd(dx1, xh, rstd_ref[...], g_ref[...])

    vec = pl.BlockSpec((1, D), lambda m: (0, 0))
    row = pl.BlockSpec((TMF, D), lambda m: (m, 0))
    return pl.pallas_call(
        body, name="ffn_up_bwd_ln1", grid=(T // TMF,),
        in_specs=[pl.BlockSpec((TMF, 2 * DFF), lambda m: (m, 0)), pl.BlockSpec((2 * DFF, D), lambda m: (0, 0)),
                  row, row, pl.BlockSpec((TMF, 1), lambda m: (m, 0)), vec],
        out_specs=(row, vec, vec),
        out_shape=(SDS((T, D), F32), SDS((1, D), F32), SDS((1, D), F32)),
    )(dpre, w_up, dz2, xh1, rstd1, g1)


def _grad_w_up(dpre, x1b):
    tk = 1024

    def body(a_ref, b_ref, o_ref, acc):
        k = pl.program_id(1)

        @pl.when(k == 0)
        def _():
            acc[...] = jnp.zeros_like(acc)

        acc[...] += _dot_tn(a_ref[...], b_ref[...])

        @pl.when(k == T // tk - 1)
        def _():
            o_ref[0] = acc[pl.ds(0, FT), :].astype(o_ref.dtype)
            o_ref[1] = acc[pl.ds(FT, FT), :].astype(o_ref.dtype)

    out = pl.pallas_call(
        body, name="grad_w_up", grid=(NFT, T // tk),
        in_specs=[pl.BlockSpec((tk, 2 * FT), lambda n, k: (k, n)), pl.BlockSpec((tk, D), lambda n, k: (k, 0))],
        out_specs=pl.BlockSpec((2, FT, D), lambda n, k: (0, n, 0)),
        out_shape=SDS((2, DFF, D), GRAD_WIRE),
        scratch_shapes=[pltpu.VMEM((2 * FT, D), F32)],
    )(dpre, x1b)
    return out.reshape(2 * DFF, D)


def _row_tile(rows, cols):
    if rows * cols * 4 <= (1 << 20) or rows % 8:
        return rows
    for t in (256, 176, 128, 88, 64, 32, 16, 8):
        if rows % t == 0 and t * cols * 4 <= (1 << 20):
            return t
    return 8


def _sum8(r, name):
    _, rows, cols = r.shape
    tr = _row_tile(rows, cols)

    def body(r_ref, o_ref):
        acc = r_ref[0].astype(F32)
        for p in range(1, NDEV):
            acc = acc + r_ref[p].astype(F32)
        o_ref[...] = acc

    return pl.pallas_call(
        body, name=name, grid=(rows // tr,),
        in_specs=[pl.BlockSpec((NDEV, tr, cols), lambda i: (0, i, 0))],
        out_specs=pl.BlockSpec((tr, cols), lambda i: (i, 0)),
        out_shape=SDS((rows, cols), F32),
    )(r)


def _adamw(w, g, m, v, name):
    rows, cols = w.shape
    tr = _row_tile(rows, cols)

    def body(w_ref, g_ref, m_ref, v_ref, d_ref, nm_ref, nv_ref):
        g_ = g_ref[...]
        m_ = B1 * m_ref[...] + (1.0 - B1) * g_
        v_ = B2 * v_ref[...] + (1.0 - B2) * jnp.square(g_)
        m_hat = m_ / (1.0 - B1 ** STEP)
        v_hat = v_ / (1.0 - B2 ** STEP)
        d_ref[...] = -LR * (m_hat / (jnp.sqrt(v_hat) + AEPS) + WD * w_ref[...])
        nm_ref[...] = m_
        nv_ref[...] = v_

    spec = pl.BlockSpec((tr, cols), lambda i: (i, 0))
    shp = SDS((rows, cols), F32)
    return pl.pallas_call(
        body, name=name, grid=(rows // tr,), in_specs=[spec] * 4, out_specs=(spec,) * 3,
        out_shape=(shp, shp, shp),
    )(w, g, m, v)


def _adamw_many(ws, gs, ms, vs, name):
    n = len(ws)

    def body(*refs):
        for i in range(n):
            w_ref, g_ref, m_ref, v_ref, d_ref, nm_ref, nv_ref = refs[i::n]
            g_ = g_ref[...]
            m_ = B1 * m_ref[...] + (1.0 - B1) * g_
            v_ = B2 * v_ref[...] + (1.0 - B2) * jnp.square(g_)
            m_hat = m_ / (1.0 - B1 ** STEP)
            v_hat = v_ / (1.0 - B2 ** STEP)
            d_ref[...] = -LR * (m_hat / (jnp.sqrt(v_hat) + AEPS) + WD * w_ref[...])
            nm_ref[...] = m_
            nv_ref[...] = v_

    shapes = tuple(SDS(w.shape, F32) for w in ws)
    res = pl.pallas_call(body, name=name, out_shape=shapes * 3)(*ws, *gs, *ms, *vs)
    return res[:n], res[n:2 * n], res[2 * n:]


def _local_step(x2, target, rel_table, first_weights, b_in, conv_b, conv_ln_g, conv_ln_b, attn_norm_g,
                conv_norm_g, late_weights, ln1_g, ln1_b, ffn_conv_b, ln2_g, ln2_b, ship_ffn_grads, ship_tail):
    buckets = jnp.asarray(_bucket_maps())
    bias = _bias_table(rel_table, buckets)
    w_in_t, conv_w, ffn_conv_w = first_weights(bias)

    qkv, ag = _proj_in(x2, w_in_t, b_in)
    attn, lse = _attn_fwd(qkv, bias)
    u1 = _conv_fwd(ag, conv_w, conv_b)
    mixed = _mix_fwd(attn, u1, attn_norm_g, conv_norm_g, conv_ln_g, conv_ln_b)
    w_out, w_up, w_down = late_weights(mixed)
    xh1, rstd1, x1b = _out_proj_ln1(mixed, w_out, x2, ln1_g, ln1_b)
    up, gv, act = _ffn_up(x1b, w_up, ffn_conv_w, ffn_conv_b)
    dz2, loss, g_ln2_g, g_ln2_b = _ffn_down_loss(act, w_down, xh1, ln1_g, ln1_b, ln2_g, ln2_b, target)

    dpre, cs_g, cs_v, gfw_g, gfw_v = _ffn_down_bwd(dz2, w_down, gv, up, ffn_conv_w)
    g_w_down = _mm_tn(act, dz2, DFF // 2, 512, "grad_w_down")
    dz1, g_ln1_g, g_ln1_b = _ffn_up_bwd_ln1(dpre, w_up, dz2, xh1, rstd1, ln1_g)
    g_w_out = _mm_tn(mixed, dz1, D, 512, "grad_w_out")
    zero = ship_ffn_grads(g_w_down, _grad_w_up(dpre, x1b), g_w_out)
    dattn, du1, g_an, g_cn, g_clg, g_clb, g_cb = _mix_bwd(
        dz1, w_out, attn, u1, attn_norm_g + zero, conv_norm_g, conv_ln_g, conv_ln_b)
    dag, cs_ag, g_conv_w = _conv_bwd(du1, ag, conv_w)
    dq, dk, dv, cs_q, cs_k, cs_v2, dbias = _attn_bwd(qkv, attn, lse, dattn, bias)
    g_rel = _rel_table_grad(dbias, buckets)
    pieces = [dq, dk, dv, dag]
    g_w_in_t = _grad_w_in(pieces, x2)

    grads = dict(
        rel_table=g_rel,
        b_in=jnp.concatenate([cs_q, cs_k, cs_v2, cs_ag], axis=1),
        conv_b=g_cb, conv_ln_g=g_clg, conv_ln_b=g_clb, attn_norm_g=g_an, conv_norm_g=g_cn,
        ln1_g=g_ln1_g, ln1_b=g_ln1_b,
        ffn_conv_b=jnp.concatenate([cs_g, cs_v], axis=1),
        ln2_g=g_ln2_g, ln2_b=g_ln2_b,
        conv_w=g_conv_w[:CK],
        ffn_conv_w=jnp.concatenate([gfw_g, gfw_v], axis=1),
    )
    grads["loss"] = loss
    zero11 = ship_tail(g_w_in_t, grads)
    grad_x = _grad_x(pieces, w_in_t, dz1, zero11)
    return loss, grad_x


SMALL = (("rel_table", (NBUCKET, NH)), ("b_in", (1, INW)), ("conv_b", (1, CW)), ("conv_ln_g", (1, CW)),
         ("conv_ln_b", (1, CW)), ("attn_norm_g", (1, AW)), ("conv_norm_g", (1, CW)), ("ln1_g", (1, D)),
         ("ln1_b", (1, D)), ("ffn_conv_b", (1, 2 * DFF)), ("ln2_g", (1, D)), ("ln2_b", (1, D)))
SHARDED_SMALL = (("conv_w", (CK, CW)), ("ffn_conv_w", (FK, 2 * DFF)))


def _pack(parts):
    flat = jnp.concatenate([p.reshape(-1) for p in parts])
    tile = 8 * PACK_LANES
    pad = (-flat.shape[0]) % tile
    return jnp.pad(flat, (0, pad)).reshape(-1, PACK_LANES)


def _unpack(packed, specs):
    flat = packed.reshape(-1)
    out, off = {}, 0
    for name, shp in specs:
        size = int(np.prod(shp))
        out[name] = flat[off:off + size].reshape(shp)
        off += size
    return out


def kernel(x, rel_table, w_in, b_in, conv_w, conv_b, conv_ln_g, conv_ln_b, attn_norm_g, conv_norm_g, w_out, ln1_g, ln1_b, w_up, ffn_conv_w, ffn_conv_b, w_down, ln2_g, ln2_b, loss_target, m_rel_table, m_w_in, m_b_in, m_conv_w, m_conv_b, m_conv_ln_g, m_conv_ln_b, m_attn_norm_g, m_conv_norm_g, m_w_out, m_ln1_g, m_ln1_b, m_w_up, m_ffn_conv_w, m_ffn_conv_b, m_w_down, m_ln2_g, m_ln2_b, v_rel_table, v_w_in, v_b_in, v_conv_w, v_conv_b, v_conv_ln_g, v_conv_ln_b, v_attn_norm_g, v_conv_norm_g, v_w_out, v_ln1_g, v_ln1_b, v_w_up, v_ffn_conv_w, v_ffn_conv_b, v_w_down, v_ln2_g, v_ln2_b):
    given = dict(locals())
    me = 4 * lax.axis_index("x") + 2 * lax.axis_index("y") + lax.axis_index("c")

    cols = lambda a: a.transpose(1, 0, 2).reshape(a.shape[1], NDEV * a.shape[2])
    rows = lambda a: a.reshape(NDEV * a.shape[1], a.shape[2])
    stack = lambda a: a.reshape(NDEV, a.shape[0] // NDEV, a.shape[1])

    small_specs = SMALL + SHARDED_SMALL
    packed_specs = small_specs + (("loss", (1, 1)),)
    grad, delta, new_m, new_v = {}, {}, {}, {}

    def adamw_big(n, g2d, transposed=False):
        shp = given[n].shape
        to2d = (lambda a: a.reshape(shp[-2], shp[-1]).T) if transposed else (lambda a: a.reshape(shp[-2], shp[-1]))
        back = (lambda a: a.T.reshape(shp)) if transposed else (lambda a: a.reshape(shp))
        d_, m_, v_ = _adamw(to2d(given[n]), g2d, to2d(given["m_" + n]), to2d(given["v_" + n]), "adamw_" + n)
        grad[n], delta[n], new_m[n], new_v[n] = back(g2d), back(d_), back(m_), back(v_)
        return d_

    first_state, zero0 = _exchange_start(
        [(w_in[0].T.astype(BF16), "gather"), (conv_w[0], "gather"), (ffn_conv_w[0], "gather")], "gather_first_start")

    def first_weights(after):
        lands = _exchange_wait(first_state, after, "gather_first_wait")
        return rows(lands[0]), cols(lands[1]), cols(lands[2])

    late_state, zero1 = _exchange_start(
        [(w_out[0].astype(BF16), "gather"), (w_up[0].T.astype(BF16) + zero0.astype(BF16), "gather"),
         (w_down[0].astype(BF16), "gather")], "gather_late_start")

    def late_weights(after):
        return [rows(l) for l in _exchange_wait(late_state, after, "gather_late_wait")]

    shipped = {}

    def ship_ffn_grads(g_w_down, g_w_up_t, g_w_out):
        shipped["ffn"], zero2 = _exchange_start(
            [(stack(a), "scatter") for a in (g_w_down, g_w_up_t, g_w_out)], "ffn_grads_start")
        return zero2

    def ship_tail(g_w_in_t, small_grads):
        shipped["tail"], zero3 = _exchange_start(
            [(stack(g_w_in_t), "scatter"), (_pack([small_grads[n] for n, _ in packed_specs]), "gather")],
            "tail_grads_start")
        return zero3.reshape(1, 1)

    loss, grad_x = _local_step(
        x.reshape(T, D), loss_target.reshape(T, D), rel_table + zero1, first_weights, b_in, conv_b, conv_ln_g,
        conv_ln_b, attn_norm_g, conv_norm_g, late_weights, ln1_g, ln1_b, ffn_conv_b,
        ln2_g, ln2_b, ship_ffn_grads, ship_tail)

    got_down, got_up, got_out = _exchange_wait(shipped["ffn"], grad_x, "ffn_grads_wait")
    adamw_big("w_down", _sum8(got_down, "sum_w_down"))
    adamw_big("w_up", _sum8(got_up, "sum_w_up"), transposed=True)
    last = adamw_big("w_out", _sum8(got_out, "sum_w_out"))

    got_in, got_small = _exchange_wait(shipped["tail"], last, "tail_grads_wait")
    adamw_big("w_in", _sum8(got_in, "sum_w_in"), transposed=True)
    small = _unpack(_sum8(got_small, "sum_small"), packed_specs)
    small["conv_w"] = lax.dynamic_slice_in_dim(small["conv_w"], me * (CW // NDEV), CW // NDEV, axis=1)
    small["ffn_conv_w"] = lax.dynamic_slice_in_dim(small["ffn_conv_w"], me * (2 * DFF // NDEV), 2 * DFF // NDEV, axis=1)
    names = [n for n, _ in small_specs]
    two = lambda a: a.reshape(a.shape[-2], a.shape[-1])
    ds, nms, nvs = _adamw_many([two(given[n]) for n in names], [small[n] for n in names],
                               [two(given["m_" + n]) for n in names], [two(given["v_" + n]) for n in names], "adamw_small")
    for n, d_, m_, v_ in zip(names, ds, nms, nvs):
        shp = given[n].shape
        grad[n], delta[n], new_m[n], new_v[n] = small[n].reshape(shp), d_.reshape(shp), m_.reshape(shp), v_.reshape(shp)

    order = ("rel_table", "w_in", "b_in", "conv_w", "conv_b", "conv_ln_g", "conv_ln_b", "attn_norm_g",
             "conv_norm_g", "w_out", "ln1_g", "ln1_b", "w_up", "ffn_conv_w", "ffn_conv_b", "w_down", "ln2_g", "ln2_b")
    return (small["loss"][0, 0], grad_x.reshape(BL, S, D), *[grad[n] for n in order], *[delta[n] for n in order],
            *[new_m[n] for n in order], *[new_v[n] for n in order])
```

```python
import math

import numpy as np
import jax
import jax.numpy as jnp
from jax import lax
from jax.experimental import pallas as pl
from jax.experimental.pallas import tpu as pltpu

F32 = jnp.float32
BF16 = jnp.bfloat16
SDS = jax.ShapeDtypeStruct

NDEV = 8
D = 1024
S = 2048
BL = 2
T = BL * S
NH = 12
HD = 64
AW = NH * HD
CW = D - AW
INW = 3 * AW + 2 * CW
CK = 31
DFF = 2816
FK = 3
BLK = 128
NBUCKET = 32
BRANCHES = ((128, 1), (512, 4), (2048, 16))
ALPHA = 2.0 ** 0.25
LN_EPS = 1e-5
NEG_INF = -1e30
LR, B1, B2, AEPS, WD, STEP = 0.001, 0.9, 0.999, 1e-08, 0.01, 10

TM = 512
FT = 1408
NFT = DFF // FT
TMF = 256
PACK_LANES = 128
GRAD_WIRE = BF16

assert all(w // d == BLK for w, d in BRANCHES)


def _dot(a, b):
    return jnp.dot(a, b, preferred_element_type=F32)


def _dot_nt(a, b):
    return lax.dot_general(a, b, (((1,), (1,)), ((), ())), preferred_element_type=F32)


def _dot_tn(a, b):
    return lax.dot_general(a, b, (((0,), (0,)), ((), ())), preferred_element_type=F32)


def _rowmean(v):
    return jnp.mean(v, axis=-1, keepdims=True)


def _colsum(v):
    return jnp.sum(v, axis=0, keepdims=True)


def _sigmoid(v):
    return jax.nn.sigmoid(v)


_HBM = pl.BlockSpec(memory_space=pltpu.HBM)
_SEM = pl.BlockSpec(memory_space=pltpu.SEMAPHORE)
_EFFECT = pltpu.SideEffectType.DATAFLOW_SIDE_EFFECTING


def _peer_of(k):
    x, y, c = lax.axis_index("x"), lax.axis_index("y"), lax.axis_index("c")
    px = 1 - x if k & 4 else x
    py = 1 - y if k & 2 else y
    pc = 1 - c if k & 1 else c
    return (px, py, pc), 4 * px + 2 * py + pc


def _split_copies(kinds, ins, lands, send_sems, recv_sems, started):
    me = 4 * lax.axis_index("x") + 2 * lax.axis_index("y") + lax.axis_index("c")
    out = []
    for k in range(1, NDEV):
        dev, pid = _peer_of(k)
        for i, kind in enumerate(kinds):
            src = ins[i] if kind == "gather" else ins[i].at[pid]
            dst = lands[i].at[me] if started else lands[i].at[pid]
            slot = i * (NDEV - 1) + k - 1
            out.append(pltpu.make_async_remote_copy(
                src_ref=src, dst_ref=dst, send_sem=send_sems.at[slot], recv_sem=recv_sems.at[slot],
                device_id=dev, device_id_type=pl.DeviceIdType.MESH))
    return out


def _exchange_start(items, name):
    n = len(items)
    kinds = [k for _, k in items]
    srcs = [pltpu.with_memory_space_constraint(a, pltpu.HBM) for a, _ in items]
    lands = []
    for a, k in items:
        shp = (NDEV,) + tuple(a.shape) if k == "gather" else tuple(a.shape)
        lands.append(pltpu.with_memory_space_constraint(lax.empty(shp, a.dtype), pltpu.HBM))

    def body(*refs):
        ins, land_refs = refs[:n], refs[n:2 * n]
        send_sems, recv_sems, own_sems = refs[2 * n:2 * n + 3]
        token = refs[-1]
        for cp in _own_copies(kinds, ins, land_refs, own_sems):
            cp.start()
        for cp in _split_copies(kinds, ins, land_refs, send_sems, recv_sems, True):
            cp.start()
        token[...] = jnp.zeros_like(token)

    sems = pltpu.SemaphoreType.DMA((n * (NDEV - 1),))
    res = pl.pallas_call(
        body, name=name,
        out_shape=(sems, sems, pltpu.SemaphoreType.DMA((n,)),
                   *[pltpu.HBM(a.shape, a.dtype) for a in srcs + lands], SDS((8, 128), F32)),
        in_specs=[_HBM] * (2 * n),
        out_specs=(_SEM, _SEM, _SEM, *[_HBM] * (2 * n), pl.BlockSpec(memory_space=pltpu.VMEM)),
        input_output_aliases={i: 3 + i for i in range(2 * n)},
        compiler_params=pltpu.CompilerParams(has_side_effects=_EFFECT),
    )(*srcs, *lands)
    return (kinds, res[0], res[1], res[2], list(res[3:3 + n]), list(res[3 + n:3 + 2 * n])), res[-1][0, 0]


def _own_copies(kinds, ins, lands, own_sems):
    me = 4 * lax.axis_index("x") + 2 * lax.axis_index("y") + lax.axis_index("c")
    return [pltpu.make_async_copy(ins[i] if kind == "gather" else ins[i].at[me], lands[i].at[me], own_sems.at[i])
            for i, kind in enumerate(kinds)]


def _exchange_wait(state, after, name):
    kinds, send_sems, recv_sems, own_sems, srcs, lands = state
    n = len(kinds)

    def body(*refs):
        ins, land_refs = refs[:n], refs[n:2 * n]
        s_sems, r_sems, o_sems = refs[2 * n:2 * n + 3]
        for cp in _split_copies(kinds, ins, land_refs, s_sems, r_sems, False):
            cp.wait_send()
            cp.wait_recv()
        for cp in _own_copies(kinds, ins, land_refs, o_sems):
            cp.wait()

    res = pl.pallas_call(
        body, name=name,
        out_shape=tuple(pltpu.HBM(a.shape, a.dtype) for a in srcs + lands),
        in_specs=[_HBM] * (2 * n) + [_SEM, _SEM, _SEM, pl.BlockSpec(memory_space=pl.ANY)],
        out_specs=tuple([_HBM] * (2 * n)),
        input_output_aliases={i: i for i in range(2 * n)},
        compiler_params=pltpu.CompilerParams(has_side_effects=_EFFECT),
    )(*srcs, *lands, send_sems, recv_sems, own_sems, after)
    return list(res[n:])


def _proj_in(x2, w_in_t, b_in):
    nq = 3 * AW

    def body(x_ref, w_ref, b_ref, qkv_ref, ag_ref):
        xb = x_ref[...].astype(BF16)
        qkv_ref[...] = (_dot_nt(xb, w_ref[pl.ds(0, nq), :]) + b_ref[:, :nq]).astype(BF16)
        ag_ref[...] = _dot_nt(xb, w_ref[pl.ds(nq, 2 * CW), :]) + b_ref[:, nq:]

    return pl.pallas_call(
        body, name="proj_in", grid=(T // TM,),
        in_specs=[pl.BlockSpec((TM, D), lambda m: (m, 0)), pl.BlockSpec((INW, D), lambda m: (0, 0)),
                  pl.BlockSpec((1, INW), lambda m: (0, 0))],
        out_specs=(pl.BlockSpec((TM, nq), lambda m: (m, 0)), pl.BlockSpec((TM, 2 * CW), lambda m: (m, 0))),
        out_shape=(SDS((T, nq), BF16), SDS((T, 2 * CW), F32)),
    )(x2, w_in_t, b_in)


def _grad_x(pieces, w_in_t, dz1, zero):
    widths = [p.shape[1] for p in pieces]

    def body(*refs):
        p_refs = refs[:len(pieces)]
        w_ref, dz_ref, z_ref, o_ref = refs[len(pieces):]
        acc = ALPHA * dz_ref[...] + z_ref[...]
        r0 = 0
        for p_ref, wd in zip(p_refs, widths):
            acc = acc + _dot(p_ref[...], w_ref[pl.ds(r0, wd), :])
            r0 += wd
        o_ref[...] = acc

    row = pl.BlockSpec((TM, D), lambda m: (m, 0))
    return pl.pallas_call(
        body, name="grad_x", grid=(T // TM,),
        in_specs=[pl.BlockSpec((TM, wd), lambda m: (m, 0)) for wd in widths]
        + [pl.BlockSpec((INW, D), lambda m: (0, 0)), row, pl.BlockSpec((1, 1), lambda m: (0, 0))],
        out_specs=row,
        out_shape=SDS((T, D), F32),
    )(*pieces, w_in_t, dz1, zero)


def _grad_w_in(pieces, x2):
    widths = [p.shape[1] for p in pieces]
    tk = 512
    nk = T // tk

    def body(*refs):
        p_refs = refs[:len(pieces)]
        x_ref, o_ref, acc = refs[len(pieces):]
        k = pl.program_id(0)

        @pl.when(k == 0)
        def _():
            acc[...] = jnp.zeros_like(acc)

        xb = x_ref[...].astype(BF16)
        r0 = 0
        for p_ref, wd in zip(p_refs, widths):
            acc[pl.ds(r0, wd), :] += _dot_tn(p_ref[...], xb)
            r0 += wd

        @pl.when(k == nk - 1)
        def _():
            o_ref[...] = acc[...].astype(o_ref.dtype)

    return pl.pallas_call(
        body, name="grad_w_in", grid=(nk,),
        in_specs=[pl.BlockSpec((tk, wd), lambda k: (k, 0)) for wd in widths] + [pl.BlockSpec((tk, D), lambda k: (k, 0))],
        out_specs=pl.BlockSpec((INW, D), lambda k: (0, 0)),
        out_shape=SDS((INW, D), GRAD_WIRE),
        scratch_shapes=[pltpu.VMEM((INW, D), F32)],
    )(*pieces, x2)


def _mm_tn(a, b, tn, tk, name):
    t_, na = a.shape
    nb = b.shape[1]
    nk = t_ // tk

    def body(a_ref, b_ref, o_ref, acc):
        k = pl.program_id(1)

        @pl.when(k == 0)
        def _():
            acc[...] = jnp.zeros_like(acc)

        acc[...] += _dot_tn(a_ref[...].astype(BF16), b_ref[...].astype(BF16))

        @pl.when(k == nk - 1)
        def _():
            o_ref[...] = acc[...].astype(o_ref.dtype)

    return pl.pallas_call(
        body, name=name, grid=(na // tn, nk),
        in_specs=[pl.BlockSpec((tk, tn), lambda n, k: (k, n)),
                  pl.BlockSpec((tk, nb), lambda n, k: (k, 0))],
        out_specs=pl.BlockSpec((tn, nb), lambda n, k: (n, 0)),
        out_shape=SDS((na, nb), GRAD_WIRE),
        scratch_shapes=[pltpu.VMEM((tn, nb), F32)],
    )(a, b)


def _bucket_maps():
    qi = np.arange(BLK)[:, None]
    kj = np.arange(2 * BLK)[None, :]
    steps = np.maximum(qi + BLK - kj, 0)
    exact = NBUCKET // 2
    maps = []
    for _, dil in BRANCHES:
        dist = steps * dil
        d_f = np.maximum(dist, 1).astype(np.float32)
        large = exact + (np.log(d_f / np.float32(exact)) / np.float32(math.log(S / exact))
                         * np.float32(NBUCKET - exact)).astype(np.int32)
        large = np.minimum(large, NBUCKET - 1)
        maps.append(np.where(dist < exact, dist, large).astype(np.int32))
    return np.stack(maps)


def _bias_table(rel_table, buckets):
    def body(t_ref, b_ref, o_ref):
        bk = b_ref[0]
        for h in range(NH):
            acc = jnp.zeros((BLK, 2 * BLK), F32)
            for k in range(NBUCKET):
                acc = jnp.where(bk == k, t_ref[k, h], acc)
            o_ref[0, h] = acc

    return pl.pallas_call(
        body, name="bias_table", grid=(len(BRANCHES),),
        in_specs=[pl.BlockSpec(memory_space=pltpu.SMEM),
                  pl.BlockSpec((1, BLK, 2 * BLK), lambda i: (i, 0, 0))],
        out_specs=pl.BlockSpec((1, NH, BLK, 2 * BLK), lambda i: (i, 0, 0, 0)),
        out_shape=SDS((len(BRANCHES), NH, BLK, 2 * BLK), F32),
    )(rel_table, buckets)


def _rel_table_grad(dbias, buckets):
    def body(d_ref, b_ref, o_ref):
        h = pl.program_id(0)
        for k in range(NBUCKET):
            tot = jnp.zeros((1, 1), F32)
            for br in range(len(BRANCHES)):
                sel = jnp.where(b_ref[br] == k, d_ref[br, 0], 0.0)
                tot = tot + jnp.sum(jnp.sum(sel, axis=1, keepdims=True), axis=0, keepdims=True)
            o_ref[0, :, pl.ds(k, 1)] = tot

    out = pl.pallas_call(
        body, name="rel_table_grad", grid=(NH,),
        in_specs=[pl.BlockSpec((len(BRANCHES), 1, BLK, 2 * BLK), lambda h: (0, h, 0, 0)),
                  pl.BlockSpec((len(BRANCHES), BLK, 2 * BLK), lambda h: (0, 0, 0))],
        out_specs=pl.BlockSpec((1, 1, NBUCKET), lambda h: (h, 0, 0)),
        out_shape=SDS((NH, 1, NBUCKET), F32),
    )(dbias, buckets)
    return out.reshape(NH, NBUCKET).T


PADK = BLK
SCALE = 1.0 / math.sqrt(HD)
ATTN_UNROLL = 16


def _branch_geometry(br):
    dil = BRANCHES[br][1]
    sub = S // dil
    return dil, sub, sub // BLK


def _token_rows(br, i):
    dil, _, nblk = _branch_geometry(br)
    if dil == 1:
        return pl.ds(pl.multiple_of(i * BLK, BLK), BLK), i
    r = lax.shift_right_logical(i, nblk.bit_length() - 1)
    n = lax.bitwise_and(i, nblk - 1)
    return pl.ds(r + dil * BLK * n, BLK, stride=dil), n


def _sub_layout_loop(br, step):
    dil, sub, _ = _branch_geometry(br)
    rows = min(sub, 256)
    nchunk = sub // rows

    def it_step(it, carry):
        if dil == 1:
            src = pl.ds(pl.multiple_of(it * rows, rows), rows)
        else:
            r = lax.shift_right_logical(it, nchunk.bit_length() - 1)
            src = pl.ds(r + dil * rows * lax.bitwise_and(it, nchunk - 1), rows, stride=dil)
        step(src, pl.multiple_of(it * rows, BLK), rows)
        return carry

    lax.fori_loop(0, dil * nchunk, it_step, 0)


def _masked_bias(bias_ref, bm):
    qi = lax.broadcasted_iota(jnp.int32, (BLK, 2 * BLK), 0)
    kj = lax.broadcasted_iota(jnp.int32, (BLK, 2 * BLK), 1)
    first = jnp.logical_and(kj >= BLK, kj - BLK <= qi)
    valid = jnp.logical_or(first, jnp.logical_and(kj < BLK, kj >= qi))
    for br in range(len(BRANCHES)):
        for j in range(2):
            b = bias_ref[br, j]
            bm[br, 1, pl.ds(j * BLK, BLK), :] = jnp.where(valid, b, NEG_INF)
            bm[br, 0, pl.ds(j * BLK, BLK), :] = jnp.where(first, b, NEG_INF)


def _head_split(fn):
    def split(t):
        h0 = lax.broadcasted_iota(jnp.int32, t.shape, 1) < HD
        t = fn(t)
        return jnp.where(h0, t, 0.0).astype(BF16), jnp.where(h0, 0.0, t).astype(BF16)
    return split


def _attn_fwd(qkv, bias):
    nbr = len(BRANCHES)

    def body(q_ref, k_ref, v_ref, bias_ref, o_ref, lse_ref, qf, kf, vf, qs0, qs1, ks, vs, bm, ob, mb, lb):
        qf[...] = q_ref[...].astype(F32)
        kf[...] = k_ref[...].astype(F32)
        vf[...] = v_ref[...].astype(F32)
        _masked_bias(bias_ref, bm)
        ks[pl.ds(0, PADK), :] = jnp.zeros((PADK, BLK), BF16)
        vs[pl.ds(0, PADK), :] = jnp.zeros((PADK, BLK), BF16)
        head0 = lax.broadcasted_iota(jnp.int32, (BLK, BLK), 1) < HD
        split_q = _head_split(lambda t: t * SCALE)

        for br in range(nbr):
            nblk = _branch_geometry(br)[2]

            def stage(src, off, rows):
                qs0[pl.ds(off, rows), :], qs1[pl.ds(off, rows), :] = split_q(qf[src, :])
                ks[pl.ds(PADK + off, rows), :] = kf[src, :].astype(BF16)
                vs[pl.ds(PADK + off, rows), :] = vf[src, :].astype(BF16)

            _sub_layout_loop(br, stage)

            def blk(i, carry, br=br, nblk=nblk):
                base = pl.multiple_of(i * BLK, BLK)
                rows, n = _token_rows(br, i)
                q01 = jnp.concatenate([qs0[pl.ds(base, BLK), :], qs1[pl.ds(base, BLK), :]], axis=0)
                if nblk > 1:
                    kcat = ks[pl.ds(base, 2 * BLK), :]
                    vcat = vs[pl.ds(base, 2 * BLK), :]
                    s = _dot_nt(q01, kcat) + bm[br, jnp.minimum(n, 1)]
                else:
                    kcat = ks[pl.ds(PADK + base, BLK), :]
                    vcat = vs[pl.ds(PADK + base, BLK), :]
                    s = _dot_nt(q01, kcat) + bm[br, 0, :, BLK:]
                mx = jnp.max(s, axis=-1, keepdims=True)
                p = jnp.exp(s - mx)
                ls = jnp.sum(p, axis=-1, keepdims=True)
                o = _dot(p.astype(BF16), vcat)
                ob[br, rows, :] = jnp.where(head0, o[:BLK], o[BLK:])
                mb[br, rows, :] = jnp.where(head0, mx[:BLK], mx[BLK:])
                lb[br, rows, :] = jnp.where(head0, ls[:BLK], ls[BLK:])
                return carry

            lax.fori_loop(0, 16, blk, 0, unroll=ATTN_UNROLL)

        def merge(i, carry):
            rows = pl.ds(pl.multiple_of(i * 256, 256), 256)
            m_all = jnp.maximum(jnp.maximum(mb[0, rows, :], mb[1, rows, :]), mb[2, rows, :])
            num = jnp.zeros((256, BLK), F32)
            den = jnp.zeros((256, BLK), F32)
            for br in range(nbr):
                c = jnp.exp(mb[br, rows, :] - m_all)
                num = num + ob[br, rows, :] * c
                den = den + lb[br, rows, :] * c
            o_ref[rows, :] = num / den
            lse_ref[rows, :] = m_all + jnp.log(den)
            return carry

        lax.fori_loop(0, S // 256, merge, 0)

    npair = NH // 2
    blk_spec = lambda off: pl.BlockSpec((S, BLK), lambda b, hp: (b, off + hp))
    return pl.pallas_call(
        body, name="attn_fwd", grid=(BL, npair),
        in_specs=[blk_spec(0), blk_spec(npair), blk_spec(2 * npair),
                  pl.BlockSpec((nbr, 2, BLK, 2 * BLK), lambda b, hp: (0, hp, 0, 0))],
        out_specs=(blk_spec(0), blk_spec(0)),
        out_shape=(SDS((T, AW), F32), SDS((T, AW), F32)),
        scratch_shapes=[pltpu.VMEM((S, BLK), F32)] * 3 + [pltpu.VMEM((S, BLK), BF16)] * 2
        + [pltpu.VMEM((PADK + S, BLK), BF16)] * 2 + [pltpu.VMEM((nbr, 2, 2 * BLK, 2 * BLK), F32)]
        + [pltpu.VMEM((nbr, S, BLK), F32)] * 3,
    )(qkv, qkv, qkv, bias)


def _attn_bwd(qkv, attn, lse, dattn, bias):
    nbr = len(BRANCHES)

    def body(q_ref, k_ref, v_ref, o_ref, lse_ref, do_ref, bias_ref,
             dq_ref, dk_ref, dv_ref, sq_ref, sk_ref, sv_ref, db_ref,
             qf, kf, vf, dl, dqa, dka, dva, qs0, qs1, ds0, ds1, ks, vs, dks, dvs, bm):
        b = pl.program_id(1)
        qf[...] = q_ref[...].astype(F32)
        kf[...] = k_ref[...].astype(F32)
        vf[...] = v_ref[...].astype(F32)
        dqa[...] = jnp.zeros_like(dqa)
        dka[...] = jnp.zeros_like(dka)
        dva[...] = jnp.zeros_like(dva)
        _masked_bias(bias_ref, bm)
        ks[pl.ds(0, PADK), :] = jnp.zeros((PADK, BLK), BF16)
        vs[pl.ds(0, PADK), :] = jnp.zeros((PADK, BLK), BF16)
        head0 = lax.broadcasted_iota(jnp.int32, (BLK, BLK), 1) < HD
        split_q = _head_split(lambda t: t * SCALE)
        split_do = _head_split(lambda t: t)

        @pl.when(b == 0)
        def _():
            db_ref[...] = jnp.zeros_like(db_ref)
            sq_ref[...] = jnp.zeros_like(sq_ref)
            sk_ref[...] = jnp.zeros_like(sk_ref)
            sv_ref[...] = jnp.zeros_like(sv_ref)

        def delta(i, carry):
            rows = pl.ds(pl.multiple_of(i * 256, 256), 256)
            prod = do_ref[rows, :] * o_ref[rows, :]
            h0 = lax.broadcasted_iota(jnp.int32, (256, BLK), 1) < HD
            d0 = jnp.sum(jnp.where(h0, prod, 0.0), axis=-1, keepdims=True)
            d1 = jnp.sum(jnp.where(h0, 0.0, prod), axis=-1, keepdims=True)
            dl[rows, :] = jnp.where(h0, d0, d1)
            return carry

        lax.fori_loop(0, S // 256, delta, 0)

        for br in range(nbr):
            nblk = _branch_geometry(br)[2]

            def stage(src, off, rows):
                qs0[pl.ds(off, rows), :], qs1[pl.ds(off, rows), :] = split_q(qf[src, :])
                ds0[pl.ds(off, rows), :], ds1[pl.ds(off, rows), :] = split_do(do_ref[src, :])
                ks[pl.ds(PADK + off, rows), :] = kf[src, :].astype(BF16)
                vs[pl.ds(PADK + off, rows), :] = vf[src, :].astype(BF16)

            _sub_layout_loop(br, stage)
            dks[...] = jnp.zeros_like(dks)
            dvs[...] = jnp.zeros_like(dvs)

            def blk(i, carry, br=br, nblk=nblk):
                base = pl.multiple_of(i * BLK, BLK)
                rows, n = _token_rows(br, i)
                q01 = jnp.concatenate([qs0[pl.ds(base, BLK), :], qs1[pl.ds(base, BLK), :]], axis=0)
                do01 = jnp.concatenate([ds0[pl.ds(base, BLK), :], ds1[pl.ds(base, BLK), :]], axis=0)
                lse_b = lse_ref[rows, :]
                dl_b = dl[rows, :]
                lse01 = jnp.concatenate([lse_b[:, 0:1], lse_b[:, HD:HD + 1]], axis=0)
                dl01 = jnp.concatenate([dl_b[:, 0:1], dl_b[:, HD:HD + 1]], axis=0)
                if nblk > 1:
                    krows = pl.ds(base, 2 * BLK)
                    bias_m = bm[br, jnp.minimum(n, 1)]
                else:
                    krows = pl.ds(PADK + base, BLK)
                    bias_m = bm[br, 0, :, BLK:]
                kcat = ks[krows, :]
                vcat = vs[krows, :]
                p = jnp.exp(_dot_nt(q01, kcat) + bias_m - lse01)
                dsv = p * (_dot_nt(do01, vcat) - dl01)
                if nblk > 1:
                    db_ref[br, 0] += dsv[:BLK]
                    db_ref[br, 1] += dsv[BLK:]
                else:
                    db_ref[br, 0, :, BLK:] += dsv[:BLK]
                    db_ref[br, 1, :, BLK:] += dsv[BLK:]
                dsb = dsv.astype(BF16)
                dq01 = _dot(dsb, kcat)
                dqa[rows, :] = dqa[rows, :] + jnp.where(head0, dq01[:BLK], dq01[BLK:])
                dks[krows, :] = dks[krows, :] + _dot_tn(dsb, q01)
                dvs[krows, :] = dvs[krows, :] + _dot_tn(p.astype(BF16), do01)
                return carry

            lax.fori_loop(0, 16, blk, 0, unroll=ATTN_UNROLL)

            def fold(src, off, rows):
                dka[src, :] = dka[src, :] + dks[pl.ds(PADK + off, rows), :]
                dva[src, :] = dva[src, :] + dvs[pl.ds(PADK + off, rows), :]

            _sub_layout_loop(br, fold)

        def flush(i, carry):
            rows = pl.ds(pl.multiple_of(i * 256, 256), 256)
            for acc, out, cs, mul in ((dqa, dq_ref, sq_ref, SCALE), (dka, dk_ref, sk_ref, 1.0), (dva, dv_ref, sv_ref, 1.0)):
                val = acc[rows, :] * mul
                out[rows, :] = val.astype(BF16)
                cs[...] += _colsum(val)
            return carry

        lax.fori_loop(0, S // 256, flush, 0)

    npair = NH // 2
    blk_spec = lambda off: pl.BlockSpec((S, BLK), lambda hp, b: (b, off + hp))
    sum_spec = pl.BlockSpec((1, BLK), lambda hp, b: (0, hp))
    return pl.pallas_call(
        body, name="attn_bwd", grid=(npair, BL),
        in_specs=[blk_spec(0), blk_spec(npair), blk_spec(2 * npair), blk_spec(0), blk_spec(0), blk_spec(0),
                  pl.BlockSpec((nbr, 2, BLK, 2 * BLK), lambda hp, b: (0, hp, 0, 0))],
        out_specs=(blk_spec(0), blk_spec(0), blk_spec(0), sum_spec, sum_spec, sum_spec,
                   pl.BlockSpec((nbr, 2, BLK, 2 * BLK), lambda hp, b: (0, hp, 0, 0))),
        out_shape=(SDS((T, AW), BF16), SDS((T, AW), BF16), SDS((T, AW), BF16),
                   SDS((1, AW), F32), SDS((1, AW), F32), SDS((1, AW), F32),
                   SDS((nbr, NH, BLK, 2 * BLK), F32)),
        scratch_shapes=[pltpu.VMEM((S, BLK), F32)] * 7 + [pltpu.VMEM((S, BLK), BF16)] * 4
        + [pltpu.VMEM((PADK + S, BLK), BF16)] * 2 + [pltpu.VMEM((PADK + S, BLK), F32)] * 2
        + [pltpu.VMEM((nbr, 2, 2 * BLK, 2 * BLK), F32)],
    )(qkv, qkv, qkv, attn, lse, dattn, bias)


CH = 256
PADR = 32


def _tap_phases(offset_of_tap):
    taps = sorted((offset_of_tap(k) % 8, offset_of_tap(k) - offset_of_tap(k) % 8, k) for k in range(CK))
    assert all(lo + CH + ph <= CH + PADR for ph, lo, _ in taps)
    return taps


def _rows_up(win):
    made = {0: win}

    def get(phase):
        if phase not in made:
            made[phase] = pltpu.roll(win, win.shape[0] - phase, 0)
        return made[phase]
    return get


def _conv_fwd(ag, conv_w, conv_b):
    def body(ag_ref, w_ref, b_ref, u1_ref, u0p):
        u0p[pl.ds(0, PADR), :] = jnp.zeros((PADR, CW), F32)

        def glu(i, carry):
            t0 = pl.multiple_of(i * CH, CH)
            a = ag_ref[pl.ds(t0, CH), :CW]
            g = ag_ref[pl.ds(t0, CH), CW:]
            u0p[pl.ds(PADR + t0, CH), :] = a * _sigmoid(g)
            return carry

        lax.fori_loop(0, S // CH, glu, 0)

        def conv(i, carry):
            t0 = pl.multiple_of(i * CH, CH)
            win = u0p[pl.ds(t0, CH + PADR), :]
            acc = jnp.zeros((CH, CW), F32) + b_ref[...]
            up = _rows_up(win)
            for phase, lo, k in _tap_phases(lambda k: PADR - (CK - 1) + k):
                acc = acc + up(phase)[lo:lo + CH, :] * w_ref[k:k + 1, :]
            u1_ref[pl.ds(t0, CH), :] = acc
            return carry

        lax.fori_loop(0, S // CH, conv, 0)

    return pl.pallas_call(
        body, name="conv_fwd", grid=(BL,),
        in_specs=[pl.BlockSpec((S, 2 * CW), lambda b: (b, 0)),
                  pl.BlockSpec((CK, CW), lambda b: (0, 0)),
                  pl.BlockSpec((1, CW), lambda b: (0, 0))],
        out_specs=pl.BlockSpec((S, CW), lambda b: (b, 0)),
        out_shape=SDS((T, CW), F32),
        scratch_shapes=[pltpu.VMEM((S + PADR, CW), F32)],
    )(ag, conv_w, conv_b)


def _conv_post(u1, cg, cb):
    mu = _rowmean(u1)
    uc = u1 - mu
    rstd = lax.rsqrt(_rowmean(uc * uc) + LN_EPS)
    xh = uc * rstd
    u2 = xh * cg + cb
    sg = _sigmoid(u2)
    return xh, rstd, u2, sg, u2 * sg


def _mix_fwd(attn, u1, ga, gc, cg, cb):
    def body(a_ref, u_ref, ga_ref, gc_ref, cg_ref, cb_ref, o_ref):
        a = a_ref[...]
        ra = lax.rsqrt(_rowmean(a * a) + LN_EPS)
        o_ref[:, :AW] = (a * ra * ga_ref[...]).astype(BF16)
        _, _, _, _, u3 = _conv_post(u_ref[...], cg_ref[...], cb_ref[...])
        rc = lax.rsqrt(_rowmean(u3 * u3) + LN_EPS)
        o_ref[:, AW:] = (u3 * rc * gc_ref[...]).astype(BF16)

    vec = lambda w: pl.BlockSpec((1, w), lambda m: (0, 0))
    return pl.pallas_call(
        body, name="mix_fwd", grid=(T // TM,),
        in_specs=[pl.BlockSpec((TM, AW), lambda m: (m, 0)), pl.BlockSpec((TM, CW), lambda m: (m, 0)),
                  vec(AW), vec(CW), vec(CW), vec(CW)],
        out_specs=pl.BlockSpec((TM, D), lambda m: (m, 0)),
        out_shape=SDS((T, D), BF16),
    )(attn, u1, ga, gc, cg, cb)


def _mix_bwd(dz1, w_out, attn, u1, ga, gc, cg, cb):
    def body(dz_ref, w_ref, a_ref, u_ref, ga_ref, gc_ref, cg_ref, cb_ref,
             da_ref, du_ref, g_an, g_cn, g_lg, g_lb, g_cb):
        @pl.when(pl.program_id(0) == 0)
        def _():
            for r in (g_an, g_cn, g_lg, g_lb, g_cb):
                r[...] = jnp.zeros_like(r)

        dm = _dot_nt(dz_ref[...].astype(BF16), w_ref[...])
        a = a_ref[...]
        dna = dm[:, :AW]
        ra = lax.rsqrt(_rowmean(a * a) + LN_EPS)
        g_an[...] += _colsum(dna * a * ra)
        dat = dna * ga_ref[...]
        da_ref[...] = ra * dat - a * (ra * ra * ra) * _rowmean(dat * a)

        xh, rstd, u2, sg, u3 = _conv_post(u_ref[...], cg_ref[...], cb_ref[...])
        dnc = dm[:, AW:]
        rc = lax.rsqrt(_rowmean(u3 * u3) + LN_EPS)
        g_cn[...] += _colsum(dnc * u3 * rc)
        dut = dnc * gc_ref[...]
        du3 = rc * dut - u3 * (rc * rc * rc) * _rowmean(dut * u3)
        du2 = du3 * sg * (1.0 + u2 * (1.0 - sg))
        g_lg[...] += _colsum(du2 * xh)
        g_lb[...] += _colsum(du2)
        dxh = du2 * cg_ref[...]
        du1 = rstd * (dxh - _rowmean(dxh) - xh * _rowmean(dxh * xh))
        g_cb[...] += _colsum(du1)
        du_ref[...] = du1

    vec = lambda w: pl.BlockSpec((1, w), lambda m: (0, 0))
    return pl.pallas_call(
        body, name="mix_bwd", grid=(T // TM,),
        in_specs=[pl.BlockSpec((TM, D), lambda m: (m, 0)), pl.BlockSpec((D, D), lambda m: (0, 0)),
                  pl.BlockSpec((TM, AW), lambda m: (m, 0)),
                  pl.BlockSpec((TM, CW), lambda m: (m, 0)), vec(AW), vec(CW), vec(CW), vec(CW)],
        out_specs=(pl.BlockSpec((TM, AW), lambda m: (m, 0)), pl.BlockSpec((TM, CW), lambda m: (m, 0)),
                   vec(AW), vec(CW), vec(CW), vec(CW), vec(CW)),
        out_shape=(SDS((T, AW), F32), SDS((T, CW), F32),
                   SDS((1, AW), F32), SDS((1, CW), F32), SDS((1, CW), F32), SDS((1, CW), F32), SDS((1, CW), F32)),
    )(dz1, w_out, attn, u1, ga, gc, cg, cb)


def _conv_bwd(du1, ag, conv_w):
    def body(du_ref, ag_ref, w_ref, dag_ref, cs_ref, gw_ref, u0p, dup):
        @pl.when(pl.program_id(0) == 0)
        def _():
            cs_ref[...] = jnp.zeros_like(cs_ref)
            gw_ref[...] = jnp.zeros_like(gw_ref)

        u0p[pl.ds(0, PADR), :] = jnp.zeros((PADR, CW), F32)
        dup[pl.ds(S, PADR), :] = jnp.zeros((PADR, CW), F32)

        def fill(i, carry):
            t0 = pl.multiple_of(i * CH, CH)
            a = ag_ref[pl.ds(t0, CH), :CW]
            g = ag_ref[pl.ds(t0, CH), CW:]
            u0p[pl.ds(PADR + t0, CH), :] = a * _sigmoid(g)
            dup[pl.ds(t0, CH), :] = du_ref[pl.ds(t0, CH), :]
            return carry

        lax.fori_loop(0, S // CH, fill, 0)

        def chunk(i, carry):
            t0 = pl.multiple_of(i * CH, CH)
            d = dup[pl.ds(t0, CH), :]
            win_u = u0p[pl.ds(t0, CH + PADR), :]
            win_d = dup[pl.ds(t0, CH + PADR), :]
            du0 = jnp.zeros((CH, CW), F32)
            up_u = _rows_up(win_u)
            for phase, lo, k in _tap_phases(lambda k: PADR - (CK - 1) + k):
                gw_ref[k:k + 1, :] += _colsum(d * up_u(phase)[lo:lo + CH, :])
            up_d = _rows_up(win_d)
            for phase, lo, k in _tap_phases(lambda k: CK - 1 - k):
                du0 = du0 + up_d(phase)[lo:lo + CH, :] * w_ref[k:k + 1, :]
            a = ag_ref[pl.ds(t0, CH), :CW]
            sg = _sigmoid(ag_ref[pl.ds(t0, CH), CW:])
            da = du0 * sg
            dg = du0 * a * sg * (1.0 - sg)
            dag_ref[pl.ds(t0, CH), :CW] = da.astype(BF16)
            dag_ref[pl.ds(t0, CH), CW:] = dg.astype(BF16)
            cs_ref[:, :CW] += _colsum(da)
            cs_ref[:, CW:] += _colsum(dg)
            return carry

        lax.fori_loop(0, S // CH, chunk, 0)

    return pl.pallas_call(
        body, name="conv_bwd", grid=(BL,),
        in_specs=[pl.BlockSpec((S, CW), lambda b: (b, 0)), pl.BlockSpec((S, 2 * CW), lambda b: (b, 0)),
                  pl.BlockSpec((CK, CW), lambda b: (0, 0))],
        out_specs=(pl.BlockSpec((S, 2 * CW), lambda b: (b, 0)),
                   pl.BlockSpec((1, 2 * CW), lambda b: (0, 0)),
                   pl.BlockSpec((PADR, CW), lambda b: (0, 0))),
        out_shape=(SDS((T, 2 * CW), BF16), SDS((1, 2 * CW), F32), SDS((PADR, CW), F32)),
        scratch_shapes=[pltpu.VMEM((S + PADR, CW), F32), pltpu.VMEM((S + PADR, CW), F32)],
    )(du1, ag, conv_w)


def _layer_norm_fwd(z):
    mu = _rowmean(z)
    zc = z - mu
    rstd = lax.rsqrt(_rowmean(zc * zc) + LN_EPS)
    return zc * rstd, rstd


def _layer_norm_bwd(dy, xh, rstd, g):
    dxh = dy * g
    return rstd * (dxh - _rowmean(dxh) - xh * _rowmean(dxh * xh))


def _out_proj_ln1(mixed, w_out, x2, g1, b1):
    def body(a_ref, w_ref, x_ref, g_ref, b_ref, xh_ref, rstd_ref, x1_ref):
        z = ALPHA * x_ref[...] + _dot(a_ref[...], w_ref[...])
        xh, rstd = _layer_norm_fwd(z)
        xh_ref[...] = xh
        rstd_ref[...] = rstd
        x1_ref[...] = (xh * g_ref[...] + b_ref[...]).astype(BF16)

    vec = pl.BlockSpec((1, D), lambda m: (0, 0))
    row = pl.BlockSpec((TM, D), lambda m: (m, 0))
    return pl.pallas_call(
        body, name="out_proj_ln1", grid=(T // TM,),
        in_specs=[row, pl.BlockSpec((D, D), lambda m: (0, 0)), row, vec, vec],
        out_specs=(row, pl.BlockSpec((TM, 1), lambda m: (m, 0)), row),
        out_shape=(SDS((T, D), F32), SDS((T, 1), F32), SDS((T, D), BF16)),
    )(mixed, w_out, x2, g1, b1)


def _seq_start(m):
    return lax.bitwise_and(m, S // TMF - 1) == 0


def _shift_down(x, before, k):
    rolled = pltpu.roll(x, k, 0)
    row = lax.broadcasted_iota(jnp.int32, before.shape, 0)
    head = jnp.where(row < k, pltpu.roll(before, k, 0), rolled[:8])
    return jnp.concatenate([head, rolled[8:]], axis=0)


def _shift_up(x, after, k):
    n = x.shape[0]
    rolled = pltpu.roll(x, n - k, 0)
    row = lax.broadcasted_iota(jnp.int32, after.shape, 0)
    tail = jnp.where(row >= 8 - k, pltpu.roll(after, 8 - k, 0), rolled[n - 8:])
    return jnp.concatenate([rolled[:n - 8], tail], axis=0)


def _ffn_up(x1b, w_up, fcw, fcb):
    def body(x_ref, wg_ref, wv_ref, cwg_ref, cwv_ref, cbg_ref, cbv_ref, up_ref, gv_ref, act_ref, prev_g, prev_v):
        @pl.when(_seq_start(pl.program_id(1)))
        def _():
            prev_g[...] = jnp.zeros_like(prev_g)
            prev_v[...] = jnp.zeros_like(prev_v)

        x = x_ref[...]
        outs = []
        for w_ref, cw_ref, cb_ref, prev, lo in ((wg_ref, cwg_ref, cbg_ref, prev_g, 0), (wv_ref, cwv_ref, cbv_ref, prev_v, FT)):
            u = _dot_nt(x, w_ref[...])
            up_ref[:, lo:lo + FT] = u.astype(BF16)
            before = prev[...]
            y = (cw_ref[2:3, :] * u + cw_ref[1:2, :] * _shift_down(u, before, 1)
                 + cw_ref[0:1, :] * _shift_down(u, before, 2) + cb_ref[...])
            prev[...] = u[TMF - 8:]
            gv_ref[:, lo:lo + FT] = y.astype(BF16)
            outs.append(y)
        gate, val = outs
        act_ref[...] = (gate * _sigmoid(gate) * val).astype(BF16)

    wspec = lambda off: pl.BlockSpec((FT, D), lambda n, m: (n + off, 0))
    cwspec = lambda off: pl.BlockSpec((FK, FT), lambda n, m: (0, n + off))
    cbspec = lambda off: pl.BlockSpec((1, FT), lambda n, m: (0, n + off))
    pair = pl.BlockSpec((TMF, 2 * FT), lambda n, m: (m, n))
    return pl.pallas_call(
        body, name="ffn_up", grid=(NFT, T // TMF),
        in_specs=[pl.BlockSpec((TMF, D), lambda n, m: (m, 0)), wspec(0), wspec(NFT),
                  cwspec(0), cwspec(NFT), cbspec(0), cbspec(NFT)],
        out_specs=(pair, pair, pl.BlockSpec((TMF, FT), lambda n, m: (m, n))),
        out_shape=(SDS((T, 2 * DFF), BF16), SDS((T, 2 * DFF), BF16), SDS((T, DFF), BF16)),
        scratch_shapes=[pltpu.VMEM((8, FT), F32)] * 2,
    )(x1b, w_up, w_up, fcw, fcw, fcb, fcb)


def _ffn_down_loss(act, w_down, xh1, g1, b1, g2, b2, target):
    def body(a_ref, w_ref, xh1_ref, g1_ref, b1_ref, g2_ref, b2_ref, t_ref, dz_ref, loss_ref, gg_ref, gb_ref):
        @pl.when(pl.program_id(0) == 0)
        def _():
            loss_ref[...] = jnp.zeros_like(loss_ref)
            gg_ref[...] = jnp.zeros_like(gg_ref)
            gb_ref[...] = jnp.zeros_like(gb_ref)

        x1 = xh1_ref[...] * g1_ref[...] + b1_ref[...]
        z = ALPHA * x1 + _dot(a_ref[...], w_ref[...])
        xh, rstd = _layer_norm_fwd(z)
        diff = xh * g2_ref[...] + b2_ref[...] - t_ref[...]
        loss_ref[...] += 0.5 * _colsum(_rowmean(diff * diff))
        dout = diff * (1.0 / D)
        gg_ref[...] += _colsum(dout * xh)
        gb_ref[...] += _colsum(dout)
        dz_ref[...] = _layer_norm_bwd(dout, xh, rstd, g2_ref[...])

    vec = pl.BlockSpec((1, D), lambda m: (0, 0))
    row = pl.BlockSpec((TM, D), lambda m: (m, 0))
    return pl.pallas_call(
        body, name="ffn_down_loss", grid=(T // TM,),
        in_specs=[pl.BlockSpec((TM, DFF), lambda m: (m, 0)), pl.BlockSpec((DFF, D), lambda m: (0, 0)),
                  row, vec, vec, vec, vec, row],
        out_specs=(row, pl.BlockSpec((1, 1), lambda m: (0, 0)), vec, vec),
        out_shape=(SDS((T, D), F32), SDS((1, 1), F32), SDS((1, D), F32), SDS((1, D), F32)),
    )(act, w_down, xh1, g1, b1, g2, b2, target)


def _ffn_down_bwd(dz2, w_down, gv, up, fcw):
    tiles = T // TMF

    def body(dz_ref, wd_ref, gv_ref, up_ref, cwg_ref, cwv_ref,
             dpre_ref, csg_ref, csv_ref, gwg_ref, gwv_ref, next_g, next_v):
        step = pl.program_id(1)
        tile = tiles - 1 - step

        @pl.when(step == 0)
        def _():
            for r in (csg_ref, csv_ref, gwg_ref, gwv_ref, next_g, next_v):
                r[...] = jnp.zeros_like(r)

        seq_end = lax.bitwise_and(tile + 1, S // TMF - 1) == 0
        dact = _dot_nt(dz_ref[...].astype(BF16), wd_ref[...])
        gate = gv_ref[:, :FT].astype(F32)
        val = gv_ref[:, FT:].astype(F32)
        sg = _sigmoid(gate)
        gs = gate * sg
        halves = ((dact * val * (sg + gs * (1.0 - sg)), cwg_ref, csg_ref, gwg_ref, next_g, 0),
                  (dact * gs, cwv_ref, csv_ref, gwv_ref, next_v, FT))
        for d0, cw_ref, cs_ref, gw_ref, nxt, lo in halves:
            after = jnp.where(seq_end, 0.0, nxt[...])
            d1 = _shift_up(d0, after, 1)
            d2 = _shift_up(d0, after, 2)
            nxt[...] = d0[:8]
            dpre_ref[:, lo:lo + FT] = (cw_ref[2:3, :] * d0 + cw_ref[1:2, :] * d1 + cw_ref[0:1, :] * d2).astype(BF16)
            cs_ref[...] += _colsum(d0)
            u = up_ref[:, lo:lo + FT].astype(F32)
            for k, dk in enumerate((d2, d1, d0)):
                gw_ref[k:k + 1, :] += _colsum(dk * u)

    cs = pl.BlockSpec((1, FT), lambda n, m: (0, n))
    gw = pl.BlockSpec((FK, FT), lambda n, m: (0, n))
    cwspec = lambda off: pl.BlockSpec((FK, FT), lambda n, m: (0, n + off))
    pair = pl.BlockSpec((TMF, 2 * FT), lambda n, m: (tiles - 1 - m, n))
    return pl.pallas_call(
        body, name="ffn_down_bwd", grid=(NFT, tiles),
        in_specs=[pl.BlockSpec((TMF, D), lambda n, m: (tiles - 1 - m, 0)), pl.BlockSpec((FT, D), lambda n, m: (n, 0)),
                  pair, pair, cwspec(0), cwspec(NFT)],
        out_specs=(pair, cs, cs, gw, gw),
        out_shape=(SDS((T, 2 * DFF), BF16), SDS((1, DFF), F32), SDS((1, DFF), F32),
                   SDS((FK, DFF), F32), SDS((FK, DFF), F32)),
        scratch_shapes=[pltpu.VMEM((8, FT), F32)] * 2,
    )(dz2, w_down, gv, up, fcw, fcw)


def _ffn_up_bwd_ln1(dpre, w_up, dz2, xh1, rstd1, g1):
    def body(a_ref, w_ref, dz2_ref, xh_ref, rstd_ref, g_ref, dz1_ref, gg_ref, gb_ref):
        @pl.when(pl.program_id(0) == 0)
        def _():
            gg_ref[...] = jnp.zeros_like(gg_ref)
            gb_ref[...] = jnp.zeros_like(gb_ref)

        for sub in range(TM // TMF):
            rows = pl.ds(sub * TMF, TMF)
            dx1 = ALPHA * dz2_ref[rows, :]
            for n in range(NFT):
                for half in range(2):
                    a = a_ref[rows, (2 * n + half) * FT:(2 * n + half + 1) * FT]
                    w = w_ref[pl.ds((half * NFT + n) * FT, FT), :]
                    dx1 = dx1 + _dot(a, w)
            xh = xh_ref[rows, :]
            gg_ref[...] += _colsum(dx1 * xh)
            gb_ref[...] += _colsum(dx1)
            dz1_ref[rows, :] = _layer_norm_bwd(dx1, xh, rstd_ref[rows, :], g_ref[...])

    vec = pl.BlockSpec((1, D), lambda m: (0, 0))
    row = pl.BlockSpec((TM, D), lambda m: (m, 0))
    return pl.pallas_call(
        body, name="ffn_up_bwd_ln1", grid=(T // TM,),
        in_specs=[pl.BlockSpec((TM, 2 * DFF), lambda m: (m, 0)),
                  pl.BlockSpec((2 * DFF, D), lambda m: (0, 0), pipeline_mode=pl.Buffered(1)),
                  row, row, pl.BlockSpec((TM, 1), lambda m: (m, 0)), vec],
        out_specs=(row, vec, vec),
        out_shape=(SDS((T, D), F32), SDS((1, D), F32), SDS((1, D), F32)),
    )(dpre, w_up, dz2, xh1, rstd1, g1)


def _grad_w_up(dpre, x1b):
    tk = 1024

    def body(a_ref, b_ref, o_ref, acc):
        k = pl.program_id(1)

        @pl.when(k == 0)
        def _():
            acc[...] = jnp.zeros_like(acc)

        acc[...] += _dot_tn(a_ref[...], b_ref[...])

        @pl.when(k == T // tk - 1)
        def _():
            o_ref[0] = acc[pl.ds(0, FT), :].astype(o_ref.dtype)
            o_ref[1] = acc[pl.ds(FT, FT), :].astype(o_ref.dtype)

    out = pl.pallas_call(
        body, name="grad_w_up", grid=(NFT, T // tk),
        in_specs=[pl.BlockSpec((tk, 2 * FT), lambda n, k: (k, n)), pl.BlockSpec((tk, D), lambda n, k: (k, 0))],
        out_specs=pl.BlockSpec((2, FT, D), lambda n, k: (0, n, 0)),
        out_shape=SDS((2, DFF, D), GRAD_WIRE),
        scratch_shapes=[pltpu.VMEM((2 * FT, D), F32)],
    )(dpre, x1b)
    return out.reshape(2 * DFF, D)


def _row_tile(rows, cols):
    if rows * cols * 4 <= (1 << 20) or rows % 8:
        return rows
    for t in (256, 176, 128, 88, 64, 32, 16, 8):
        if rows % t == 0 and t * cols * 4 <= (1 << 20):
            return t
    return 8


def _sum8(r, name):
    _, rows, cols = r.shape
    tr = _row_tile(rows, cols)

    def body(r_ref, o_ref):
        acc = r_ref[0].astype(F32)
        for p in range(1, NDEV):
            acc = acc + r_ref[p].astype(F32)
        o_ref[...] = acc

    return pl.pallas_call(
        body, name=name, grid=(rows // tr,),
        in_specs=[pl.BlockSpec((NDEV, tr, cols), lambda i: (0, i, 0))],
        out_specs=pl.BlockSpec((tr, cols), lambda i: (i, 0)),
        out_shape=SDS((rows, cols), F32),
    )(r)


def _sum8_adamw(r, w, m, v, name):
    rows, cols = w.shape
    tr = _row_tile(rows, cols)

    def body(r_ref, w_ref, m_ref, v_ref, g_out, d_ref, nm_ref, nv_ref):
        g_ = r_ref[0].astype(F32)
        for p in range(1, NDEV):
            g_ = g_ + r_ref[p].astype(F32)
        m_ = B1 * m_ref[...] + (1.0 - B1) * g_
        v_ = B2 * v_ref[...] + (1.0 - B2) * jnp.square(g_)
        m_hat = m_ / (1.0 - B1 ** STEP)
        v_hat = v_ / (1.0 - B2 ** STEP)
        g_out[...] = g_
        d_ref[...] = -LR * (m_hat / (jnp.sqrt(v_hat) + AEPS) + WD * w_ref[...])
        nm_ref[...] = m_
        nv_ref[...] = v_

    spec = pl.BlockSpec((tr, cols), lambda i: (i, 0))
    shp = SDS((rows, cols), F32)
    return pl.pallas_call(
        body, name=name, grid=(rows // tr,),
        in_specs=[pl.BlockSpec((NDEV, tr, cols), lambda i: (0, i, 0))] + [spec] * 3, out_specs=(spec,) * 4,
        out_shape=(shp,) * 4,
    )(r, w, m, v)


def _adamw_many(ws, gs, ms, vs, name):
    n = len(ws)

    def body(*refs):
        for i in range(n):
            w_ref, g_ref, m_ref, v_ref, d_ref, nm_ref, nv_ref = refs[i::n]
            g_ = g_ref[...]
            m_ = B1 * m_ref[...] + (1.0 - B1) * g_
            v_ = B2 * v_ref[...] + (1.0 - B2) * jnp.square(g_)
            m_hat = m_ / (1.0 - B1 ** STEP)
            v_hat = v_ / (1.0 - B2 ** STEP)
            d_ref[...] = -LR * (m_hat / (jnp.sqrt(v_hat) + AEPS) + WD * w_ref[...])
            nm_ref[...] = m_
            nv_ref[...] = v_

    shapes = tuple(SDS(w.shape, F32) for w in ws)
    res = pl.pallas_call(body, name=name, out_shape=shapes * 3)(*ws, *gs, *ms, *vs)
    return res[:n], res[n:2 * n], res[2 * n:]


def _local_step(x2, target, rel_table, first_weights, b_in, conv_b, conv_ln_g, conv_ln_b, attn_norm_g,
                conv_norm_g, late_weights, ln1_g, ln1_b, ffn_conv_b, ln2_g, ln2_b, ship_ffn_grads, ship_tail):
    buckets = jnp.asarray(_bucket_maps())
    bias = _bias_table(rel_table, buckets)
    w_in_t, conv_w, ffn_conv_w = first_weights(bias)

    qkv, ag = _proj_in(x2, w_in_t, b_in)
    attn, lse = _attn_fwd(qkv, bias)
    u1 = _conv_fwd(ag, conv_w, conv_b)
    mixed = _mix_fwd(attn, u1, attn_norm_g, conv_norm_g, conv_ln_g, conv_ln_b)
    w_out, w_up, w_down = late_weights(mixed)
    xh1, rstd1, x1b = _out_proj_ln1(mixed, w_out, x2, ln1_g, ln1_b)
    up, gv, act = _ffn_up(x1b, w_up, ffn_conv_w, ffn_conv_b)
    dz2, loss, g_ln2_g, g_ln2_b = _ffn_down_loss(act, w_down, xh1, ln1_g, ln1_b, ln2_g, ln2_b, target)

    dpre, cs_g, cs_v, gfw_g, gfw_v = _ffn_down_bwd(dz2, w_down, gv, up, ffn_conv_w)
    g_w_down = _mm_tn(act, dz2, DFF // 2, 512, "grad_w_down")
    dz1, g_ln1_g, g_ln1_b = _ffn_up_bwd_ln1(dpre, w_up, dz2, xh1, rstd1, ln1_g)
    g_w_out = _mm_tn(mixed, dz1, D, 512, "grad_w_out")
    zero = ship_ffn_grads(g_w_down, _grad_w_up(dpre, x1b), g_w_out)
    dattn, du1, g_an, g_cn, g_clg, g_clb, g_cb = _mix_bwd(
        dz1, w_out, attn, u1, attn_norm_g + zero, conv_norm_g, conv_ln_g, conv_ln_b)
    dag, cs_ag, g_conv_w = _conv_bwd(du1, ag, conv_w)
    dq, dk, dv, cs_q, cs_k, cs_v2, dbias = _attn_bwd(qkv, attn, lse, dattn, bias)
    g_rel = _rel_table_grad(dbias, buckets)
    pieces = [dq, dk, dv, dag]
    g_w_in_t = _grad_w_in(pieces, x2)

    grads = dict(
        rel_table=g_rel,
        b_in=jnp.concatenate([cs_q, cs_k, cs_v2, cs_ag], axis=1),
        conv_b=g_cb, conv_ln_g=g_clg, conv_ln_b=g_clb, attn_norm_g=g_an, conv_norm_g=g_cn,
        ln1_g=g_ln1_g, ln1_b=g_ln1_b,
        ffn_conv_b=jnp.concatenate([cs_g, cs_v], axis=1),
        ln2_g=g_ln2_g, ln2_b=g_ln2_b,
        conv_w=g_conv_w[:CK],
        ffn_conv_w=jnp.concatenate([gfw_g, gfw_v], axis=1),
    )
    grads["loss"] = loss
    zero11 = ship_tail(g_w_in_t, grads)
    grad_x = _grad_x(pieces, w_in_t, dz1, zero11)
    return loss, grad_x


SMALL = (("rel_table", (NBUCKET, NH)), ("b_in", (1, INW)), ("conv_b", (1, CW)), ("conv_ln_g", (1, CW)),
         ("conv_ln_b", (1, CW)), ("attn_norm_g", (1, AW)), ("conv_norm_g", (1, CW)), ("ln1_g", (1, D)),
         ("ln1_b", (1, D)), ("ffn_conv_b", (1, 2 * DFF)), ("ln2_g", (1, D)), ("ln2_b", (1, D)))
SHARDED_SMALL = (("conv_w", (CK, CW)), ("ffn_conv_w", (FK, 2 * DFF)))


def _pack(parts):
    flat = jnp.concatenate([p.reshape(-1) for p in parts])
    tile = 8 * PACK_LANES
    pad = (-flat.shape[0]) % tile
    return jnp.pad(flat, (0, pad)).reshape(-1, PACK_LANES)


def _unpack(packed, specs):
    flat = packed.reshape(-1)
    out, off = {}, 0
    for name, shp in specs:
        size = int(np.prod(shp))
        out[name] = flat[off:off + size].reshape(shp)
        off += size
    return out


def kernel(x, rel_table, w_in, b_in, conv_w, conv_b, conv_ln_g, conv_ln_b, attn_norm_g, conv_norm_g, w_out, ln1_g, ln1_b, w_up, ffn_conv_w, ffn_conv_b, w_down, ln2_g, ln2_b, loss_target, m_rel_table, m_w_in, m_b_in, m_conv_w, m_conv_b, m_conv_ln_g, m_conv_ln_b, m_attn_norm_g, m_conv_norm_g, m_w_out, m_ln1_g, m_ln1_b, m_w_up, m_ffn_conv_w, m_ffn_conv_b, m_w_down, m_ln2_g, m_ln2_b, v_rel_table, v_w_in, v_b_in, v_conv_w, v_conv_b, v_conv_ln_g, v_conv_ln_b, v_attn_norm_g, v_conv_norm_g, v_w_out, v_ln1_g, v_ln1_b, v_w_up, v_ffn_conv_w, v_ffn_conv_b, v_w_down, v_ln2_g, v_ln2_b):
    given = dict(locals())
    me = 4 * lax.axis_index("x") + 2 * lax.axis_index("y") + lax.axis_index("c")

    cols = lambda a: a.transpose(1, 0, 2).reshape(a.shape[1], NDEV * a.shape[2])
    rows = lambda a: a.reshape(NDEV * a.shape[1], a.shape[2])
    stack = lambda a: a.reshape(NDEV, a.shape[0] // NDEV, a.shape[1])

    small_specs = SMALL + SHARDED_SMALL
    packed_specs = small_specs + (("loss", (1, 1)),)
    grad, delta, new_m, new_v = {}, {}, {}, {}

    def adamw_big(n, partials, transposed=False):
        shp = given[n].shape
        to2d = (lambda a: a.reshape(shp[-2], shp[-1]).T) if transposed else (lambda a: a.reshape(shp[-2], shp[-1]))
        back = (lambda a: a.T.reshape(shp)) if transposed else (lambda a: a.reshape(shp))
        g_, d_, m_, v_ = _sum8_adamw(partials, to2d(given[n]), to2d(given["m_" + n]), to2d(given["v_" + n]), "adamw_" + n)
        grad[n], delta[n], new_m[n], new_v[n] = back(g_), back(d_), back(m_), back(v_)
        return d_

    first_state, zero0 = _exchange_start(
        [(w_in[0].T.astype(BF16), "gather"), (conv_w[0], "gather"), (ffn_conv_w[0], "gather")], "gather_first_start")

    def first_weights(after):
        lands = _exchange_wait(first_state, after, "gather_first_wait")
        return rows(lands[0]), cols(lands[1]), cols(lands[2])

    late_state, zero1 = _exchange_start(
        [(w_out[0].astype(BF16), "gather"), (w_up[0].T.astype(BF16) + zero0.astype(BF16), "gather"),
         (w_down[0].astype(BF16), "gather")], "gather_late_start")

    def late_weights(after):
        return [rows(l) for l in _exchange_wait(late_state, after, "gather_late_wait")]

    shipped = {}

    def ship_ffn_grads(g_w_down, g_w_up_t, g_w_out):
        shipped["ffn"], zero2 = _exchange_start(
            [(stack(a), "scatter") for a in (g_w_down, g_w_up_t, g_w_out)], "ffn_grads_start")
        return zero2

    def ship_tail(g_w_in_t, small_grads):
        shipped["tail"], zero3 = _exchange_start(
            [(stack(g_w_in_t), "scatter"), (_pack([small_grads[n] for n, _ in packed_specs]), "gather")],
            "tail_grads_start")
        return zero3.reshape(1, 1)

    loss, grad_x = _local_step(
        x.reshape(T, D), loss_target.reshape(T, D), rel_table + zero1, first_weights, b_in, conv_b, conv_ln_g,
        conv_ln_b, attn_norm_g, conv_norm_g, late_weights, ln1_g, ln1_b, ffn_conv_b,
        ln2_g, ln2_b, ship_ffn_grads, ship_tail)

    got_down, got_up, got_out = _exchange_wait(shipped["ffn"], grad_x, "ffn_grads_wait")
    adamw_big("w_down", got_down)
    adamw_big("w_up", got_up, transposed=True)
    last = adamw_big("w_out", got_out)

    got_in, got_small = _exchange_wait(shipped["tail"], last, "tail_grads_wait")
    adamw_big("w_in", got_in, transposed=True)
    small = _unpack(_sum8(got_small, "sum_small"), packed_specs)
    small["conv_w"] = lax.dynamic_slice_in_dim(small["conv_w"], me * (CW // NDEV), CW // NDEV, axis=1)
    small["ffn_conv_w"] = lax.dynamic_slice_in_dim(small["ffn_conv_w"], me * (2 * DFF // NDEV), 2 * DFF // NDEV, axis=1)
    names = [n for n, _ in small_specs]
    two = lambda a: a.reshape(a.shape[-2], a.shape[-1])
    ds, nms, nvs = _adamw_many([two(given[n]) for n in names], [small[n] for n in names],
                               [two(given["m_" + n]) for n in names], [two(given["v_" + n]) for n in names], "adamw_small")
    for n, d_, m_, v_ in zip(names, ds, nms, nvs):
        shp = given[n].shape
        grad[n], delta[n], new_m[n], new_v[n] = small[n].reshape(shp), d_.reshape(shp), m_.reshape(shp), v_.reshape(shp)

    order = ("rel_table", "w_in", "b_in", "conv_w", "conv_b", "conv_ln_g", "conv_ln_b", "attn_norm_g",
             "conv_norm_g", "w_out", "ln1_g", "ln1_b", "w_up", "ffn_conv_w", "ffn_conv_b", "w_down", "ln2_g", "ln2_b")
    return (small["loss"][0, 0], grad_x.reshape(BL, S, D), *[grad[n] for n in order], *[delta[n] for n in order],
            *[new_m[n] for n in order], *[new_v[n] for n in order])
```

```python
import math

import numpy as np
import jax
import jax.numpy as jnp
from jax import lax
from jax.experimental import pallas as pl
from jax.experimental.pallas import tpu as pltpu

F32 = jnp.float32
BF16 = jnp.bfloat16
SDS = jax.ShapeDtypeStruct

NDEV = 8
D = 1024
S = 2048
BL = 2
T = BL * S
NH = 12
HD = 64
AW = NH * HD
CW = D - AW
INW = 3 * AW + 2 * CW
CK = 31
DFF = 2816
FK = 3
BLK = 128
NBUCKET = 32
BRANCHES = ((128, 1), (512, 4), (2048, 16))
ALPHA = 2.0 ** 0.25
LN_EPS = 1e-5
NEG_INF = -1e30
LR, B1, B2, AEPS, WD, STEP = 0.001, 0.9, 0.999, 1e-08, 0.01, 10

TM = 512
FT = 1408
NFT = DFF // FT
TMF = 256
PACK_LANES = 128
GRAD_WIRE = BF16

assert all(w // d == BLK for w, d in BRANCHES)


def _dot(a, b):
    return jnp.dot(a, b, preferred_element_type=F32)


def _dot_nt(a, b):
    return lax.dot_general(a, b, (((1,), (1,)), ((), ())), preferred_element_type=F32)


def _dot_tn(a, b):
    return lax.dot_general(a, b, (((0,), (0,)), ((), ())), preferred_element_type=F32)


def _rowmean(v):
    return jnp.mean(v, axis=-1, keepdims=True)


def _colsum(v):
    return jnp.sum(v, axis=0, keepdims=True)


def _sigmoid(v):
    return jax.nn.sigmoid(v)


_HBM = pl.BlockSpec(memory_space=pltpu.HBM)
_SEM = pl.BlockSpec(memory_space=pltpu.SEMAPHORE)
_EFFECT = pltpu.SideEffectType.DATAFLOW_SIDE_EFFECTING


def _peer_of(k):
    x, y, c = lax.axis_index("x"), lax.axis_index("y"), lax.axis_index("c")
    px = 1 - x if k & 4 else x
    py = 1 - y if k & 2 else y
    pc = 1 - c if k & 1 else c
    return (px, py, pc), 4 * px + 2 * py + pc


def _split_copies(kinds, ins, lands, send_sems, recv_sems, started):
    me = 4 * lax.axis_index("x") + 2 * lax.axis_index("y") + lax.axis_index("c")
    out = []
    for i, kind in enumerate(kinds):
        for k in range(1, NDEV):
            dev, pid = _peer_of(k)
            src = ins[i] if kind == "gather" else ins[i].at[pid]
            dst = lands[i].at[me] if started else lands[i].at[pid]
            slot = i * (NDEV - 1) + k - 1
            out.append(pltpu.make_async_remote_copy(
                src_ref=src, dst_ref=dst, send_sem=send_sems.at[slot], recv_sem=recv_sems.at[slot],
                device_id=dev, device_id_type=pl.DeviceIdType.MESH))
    return out


def _exchange_start(items, name):
    n = len(items)
    kinds = [k for _, k in items]
    srcs = [pltpu.with_memory_space_constraint(a, pltpu.HBM) for a, _ in items]
    lands = []
    for a, k in items:
        shp = (NDEV,) + tuple(a.shape) if k == "gather" else tuple(a.shape)
        lands.append(pltpu.with_memory_space_constraint(lax.empty(shp, a.dtype), pltpu.HBM))

    def body(*refs):
        ins, land_refs = refs[:n], refs[n:2 * n]
        send_sems, recv_sems, own_sems = refs[2 * n:2 * n + 3]
        token = refs[-1]
        for cp in _own_copies(kinds, ins, land_refs, own_sems):
            cp.start()
        for cp in _split_copies(kinds, ins, land_refs, send_sems, recv_sems, True):
            cp.start()
        token[...] = jnp.zeros_like(token)

    sems = pltpu.SemaphoreType.DMA((n * (NDEV - 1),))
    res = pl.pallas_call(
        body, name=name,
        out_shape=(sems, sems, pltpu.SemaphoreType.DMA((n,)),
                   *[pltpu.HBM(a.shape, a.dtype) for a in srcs + lands], SDS((8, 128), F32)),
        in_specs=[_HBM] * (2 * n),
        out_specs=(_SEM, _SEM, _SEM, *[_HBM] * (2 * n), pl.BlockSpec(memory_space=pltpu.VMEM)),
        input_output_aliases={i: 3 + i for i in range(2 * n)},
        compiler_params=pltpu.CompilerParams(has_side_effects=_EFFECT),
    )(*srcs, *lands)
    return (kinds, res[0], res[1], res[2], list(res[3:3 + n]), list(res[3 + n:3 + 2 * n])), res[-1][0, 0]


def _own_copies(kinds, ins, lands, own_sems):
    me = 4 * lax.axis_index("x") + 2 * lax.axis_index("y") + lax.axis_index("c")
    return [pltpu.make_async_copy(ins[i] if kind == "gather" else ins[i].at[me], lands[i].at[me], own_sems.at[i])
            for i, kind in enumerate(kinds)]


def _exchange_wait(state, after, name, only=None):
    kinds, send_sems, recv_sems, own_sems, srcs, lands = state
    n = len(kinds)
    chosen = range(n) if only is None else only

    def body(*refs):
        ins, land_refs = refs[:n], refs[n:2 * n]
        s_sems, r_sems, o_sems = refs[2 * n:2 * n + 3]
        remote = _split_copies(kinds, ins, land_refs, s_sems, r_sems, False)
        own = _own_copies(kinds, ins, land_refs, o_sems)
        for i in chosen:
            for cp in remote[i * (NDEV - 1):(i + 1) * (NDEV - 1)]:
                cp.wait_send()
                cp.wait_recv()
        for i in chosen:
            own[i].wait()

    res = pl.pallas_call(
        body, name=name,
        out_shape=tuple(pltpu.HBM(a.shape, a.dtype) for a in srcs + lands),
        in_specs=[_HBM] * (2 * n) + [_SEM, _SEM, _SEM, pl.BlockSpec(memory_space=pl.ANY)],
        out_specs=tuple([_HBM] * (2 * n)),
        input_output_aliases={i: i for i in range(2 * n)},
        compiler_params=pltpu.CompilerParams(has_side_effects=_EFFECT),
    )(*srcs, *lands, send_sems, recv_sems, own_sems, after)
    srcs[:], lands[:] = res[:n], res[n:]
    return list(lands)


def _proj_in(x2, w_in_t, b_in):
    nq = 3 * AW

    def body(x_ref, w_ref, b_ref, qkv_ref, ag_ref):
        xb = x_ref[...].astype(BF16)
        qkv_ref[...] = (_dot_nt(xb, w_ref[pl.ds(0, nq), :]) + b_ref[:, :nq]).astype(BF16)
        ag_ref[...] = _dot_nt(xb, w_ref[pl.ds(nq, 2 * CW), :]) + b_ref[:, nq:]

    return pl.pallas_call(
        body, name="proj_in", grid=(T // TM,),
        in_specs=[pl.BlockSpec((TM, D), lambda m: (m, 0)), pl.BlockSpec((INW, D), lambda m: (0, 0)),
                  pl.BlockSpec((1, INW), lambda m: (0, 0))],
        out_specs=(pl.BlockSpec((TM, nq), lambda m: (m, 0)), pl.BlockSpec((TM, 2 * CW), lambda m: (m, 0))),
        out_shape=(SDS((T, nq), BF16), SDS((T, 2 * CW), F32)),
    )(x2, w_in_t, b_in)


def _grad_x(pieces, w_in_t, dz1, zero):
    widths = [p.shape[1] for p in pieces]

    def body(*refs):
        p_refs = refs[:len(pieces)]
        w_ref, dz_ref, z_ref, o_ref = refs[len(pieces):]
        acc = ALPHA * dz_ref[...] + z_ref[...]
        r0 = 0
        for p_ref, wd in zip(p_refs, widths):
            acc = acc + _dot(p_ref[...], w_ref[pl.ds(r0, wd), :])
            r0 += wd
        o_ref[...] = acc

    row = pl.BlockSpec((TM, D), lambda m: (m, 0))
    return pl.pallas_call(
        body, name="grad_x", grid=(T // TM,),
        in_specs=[pl.BlockSpec((TM, wd), lambda m: (m, 0)) for wd in widths]
        + [pl.BlockSpec((INW, D), lambda m: (0, 0)), row, pl.BlockSpec((1, 1), lambda m: (0, 0))],
        out_specs=row,
        out_shape=SDS((T, D), F32),
    )(*pieces, w_in_t, dz1, zero)


def _grad_w_in(pieces, x2):
    widths = [p.shape[1] for p in pieces]
    tk = 512
    nk = T // tk

    def body(*refs):
        p_refs = refs[:len(pieces)]
        x_ref, o_ref, acc = refs[len(pieces):]
        k = pl.program_id(0)

        @pl.when(k == 0)
        def _():
            acc[...] = jnp.zeros_like(acc)

        xb = x_ref[...].astype(BF16)
        r0 = 0
        for p_ref, wd in zip(p_refs, widths):
            acc[pl.ds(r0, wd), :] += _dot_tn(p_ref[...], xb)
            r0 += wd

        @pl.when(k == nk - 1)
        def _():
            o_ref[...] = acc[...].astype(o_ref.dtype)

    return pl.pallas_call(
        body, name="grad_w_in", grid=(nk,),
        in_specs=[pl.BlockSpec((tk, wd), lambda k: (k, 0)) for wd in widths] + [pl.BlockSpec((tk, D), lambda k: (k, 0))],
        out_specs=pl.BlockSpec((INW, D), lambda k: (0, 0)),
        out_shape=SDS((INW, D), GRAD_WIRE),
        scratch_shapes=[pltpu.VMEM((INW, D), F32)],
    )(*pieces, x2)


def _mm_tn(a, b, tn, tk, name):
    t_, na = a.shape
    nb = b.shape[1]
    nk = t_ // tk

    def body(a_ref, b_ref, o_ref, acc):
        k = pl.program_id(1)

        @pl.when(k == 0)
        def _():
            acc[...] = jnp.zeros_like(acc)

        acc[...] += _dot_tn(a_ref[...].astype(BF16), b_ref[...].astype(BF16))

        @pl.when(k == nk - 1)
        def _():
            o_ref[...] = acc[...].astype(o_ref.dtype)

    return pl.pallas_call(
        body, name=name, grid=(na // tn, nk),
        in_specs=[pl.BlockSpec((tk, tn), lambda n, k: (k, n)),
                  pl.BlockSpec((tk, nb), lambda n, k: (k, 0))],
        out_specs=pl.BlockSpec((tn, nb), lambda n, k: (n, 0)),
        out_shape=SDS((na, nb), GRAD_WIRE),
        scratch_shapes=[pltpu.VMEM((tn, nb), F32)],
    )(a, b)


def _bucket_maps():
    qi = np.arange(BLK)[:, None]
    kj = np.arange(2 * BLK)[None, :]
    steps = np.maximum(qi + BLK - kj, 0)
    exact = NBUCKET // 2
    maps = []
    for _, dil in BRANCHES:
        dist = steps * dil
        d_f = np.maximum(dist, 1).astype(np.float32)
        large = exact + (np.log(d_f / np.float32(exact)) / np.float32(math.log(S / exact))
                         * np.float32(NBUCKET - exact)).astype(np.int32)
        large = np.minimum(large, NBUCKET - 1)
        maps.append(np.where(dist < exact, dist, large).astype(np.int32))
    return np.stack(maps)


def _bias_table(rel_table, buckets):
    def body(t_ref, b_ref, o_ref):
        bk = b_ref[0]
        for h in range(NH):
            acc = jnp.zeros((BLK, 2 * BLK), F32)
            for k in range(NBUCKET):
                acc = jnp.where(bk == k, t_ref[k, h], acc)
            o_ref[0, h] = acc

    return pl.pallas_call(
        body, name="bias_table", grid=(len(BRANCHES),),
        in_specs=[pl.BlockSpec(memory_space=pltpu.SMEM),
                  pl.BlockSpec((1, BLK, 2 * BLK), lambda i: (i, 0, 0))],
        out_specs=pl.BlockSpec((1, NH, BLK, 2 * BLK), lambda i: (i, 0, 0, 0)),
        out_shape=SDS((len(BRANCHES), NH, BLK, 2 * BLK), F32),
    )(rel_table, buckets)


def _rel_table_grad(dbias, buckets):
    def body(d_ref, b_ref, o_ref):
        h = pl.program_id(0)
        for k in range(NBUCKET):
            tot = jnp.zeros((1, 1), F32)
            for br in range(len(BRANCHES)):
                sel = jnp.where(b_ref[br] == k, d_ref[br, 0], 0.0)
                tot = tot + jnp.sum(jnp.sum(sel, axis=1, keepdims=True), axis=0, keepdims=True)
            o_ref[0, :, pl.ds(k, 1)] = tot

    out = pl.pallas_call(
        body, name="rel_table_grad", grid=(NH,),
        in_specs=[pl.BlockSpec((len(BRANCHES), 1, BLK, 2 * BLK), lambda h: (0, h, 0, 0)),
                  pl.BlockSpec((len(BRANCHES), BLK, 2 * BLK), lambda h: (0, 0, 0))],
        out_specs=pl.BlockSpec((1, 1, NBUCKET), lambda h: (h, 0, 0)),
        out_shape=SDS((NH, 1, NBUCKET), F32),
    )(dbias, buckets)
    return out.reshape(NH, NBUCKET).T


PADK = BLK
SCALE = 1.0 / math.sqrt(HD)
ATTN_UNROLL = 16


def _branch_geometry(br):
    dil = BRANCHES[br][1]
    sub = S // dil
    return dil, sub, sub // BLK


def _token_rows(br, i):
    dil, _, nblk = _branch_geometry(br)
    if dil == 1:
        return pl.ds(pl.multiple_of(i * BLK, BLK), BLK), i
    r = lax.shift_right_logical(i, nblk.bit_length() - 1)
    n = lax.bitwise_and(i, nblk - 1)
    return pl.ds(r + dil * BLK * n, BLK, stride=dil), n


def _sub_layout_loop(br, step):
    dil, sub, _ = _branch_geometry(br)
    rows = min(sub, 256)
    nchunk = sub // rows

    def it_step(it, carry):
        if dil == 1:
            src = pl.ds(pl.multiple_of(it * rows, rows), rows)
        else:
            r = lax.shift_right_logical(it, nchunk.bit_length() - 1)
            src = pl.ds(r + dil * rows * lax.bitwise_and(it, nchunk - 1), rows, stride=dil)
        step(src, pl.multiple_of(it * rows, BLK), rows)
        return carry

    lax.fori_loop(0, dil * nchunk, it_step, 0)


def _masked_bias(bias_ref, bm):
    qi = lax.broadcasted_iota(jnp.int32, (BLK, 2 * BLK), 0)
    kj = lax.broadcasted_iota(jnp.int32, (BLK, 2 * BLK), 1)
    first = jnp.logical_and(kj >= BLK, kj - BLK <= qi)
    valid = jnp.logical_or(first, jnp.logical_and(kj < BLK, kj >= qi))
    for br in range(len(BRANCHES)):
        for j in range(2):
            b = bias_ref[br, j]
            bm[br, 1, pl.ds(j * BLK, BLK), :] = jnp.where(valid, b, NEG_INF)
            bm[br, 0, pl.ds(j * BLK, BLK), :] = jnp.where(first, b, NEG_INF)


def _head_split(fn):
    def split(t):
        h0 = lax.broadcasted_iota(jnp.int32, t.shape, 1) < HD
        t = fn(t)
        return jnp.where(h0, t, 0.0).astype(BF16), jnp.where(h0, 0.0, t).astype(BF16)
    return split


def _attn_fwd(qkv, bias):
    nbr = len(BRANCHES)

    def body(q_ref, k_ref, v_ref, bias_ref, o_ref, lse_ref, qf, kf, vf, qs0, qs1, ks, vs, bm, ob, mb, lb):
        qf[...] = q_ref[...].astype(F32)
        kf[...] = k_ref[...].astype(F32)
        vf[...] = v_ref[...].astype(F32)
        _masked_bias(bias_ref, bm)
        ks[pl.ds(0, PADK), :] = jnp.zeros((PADK, BLK), BF16)
        vs[pl.ds(0, PADK), :] = jnp.zeros((PADK, BLK), BF16)
        head0 = lax.broadcasted_iota(jnp.int32, (BLK, BLK), 1) < HD
        split_q = _head_split(lambda t: t * SCALE)

        for br in range(nbr):
            nblk = _branch_geometry(br)[2]

            def stage(src, off, rows):
                qs0[pl.ds(off, rows), :], qs1[pl.ds(off, rows), :] = split_q(qf[src, :])
                ks[pl.ds(PADK + off, rows), :] = kf[src, :].astype(BF16)
                vs[pl.ds(PADK + off, rows), :] = vf[src, :].astype(BF16)

            _sub_layout_loop(br, stage)

            def blk(i, carry, br=br, nblk=nblk):
                base = pl.multiple_of(i * BLK, BLK)
                rows, n = _token_rows(br, i)
                q01 = jnp.concatenate([qs0[pl.ds(base, BLK), :], qs1[pl.ds(base, BLK), :]], axis=0)
                if nblk > 1:
                    kcat = ks[pl.ds(base, 2 * BLK), :]
                    vcat = vs[pl.ds(base, 2 * BLK), :]
                    s = _dot_nt(q01, kcat) + bm[br, jnp.minimum(n, 1)]
                else:
                    kcat = ks[pl.ds(PADK + base, BLK), :]
                    vcat = vs[pl.ds(PADK + base, BLK), :]
                    s = _dot_nt(q01, kcat) + bm[br, 0, :, BLK:]
                mx = jnp.max(s, axis=-1, keepdims=True)
                p = jnp.exp(s - mx)
                ls = jnp.sum(p, axis=-1, keepdims=True)
                o = _dot(p.astype(BF16), vcat)
                ob[br, rows, :] = jnp.where(head0, o[:BLK], o[BLK:])
                mb[br, rows, :] = jnp.where(head0, mx[:BLK], mx[BLK:])
                lb[br, rows, :] = jnp.where(head0, ls[:BLK], ls[BLK:])
                return carry

            lax.fori_loop(0, 16, blk, 0, unroll=ATTN_UNROLL)

        def merge(i, carry):
            rows = pl.ds(pl.multiple_of(i * 256, 256), 256)
            m_all = jnp.maximum(jnp.maximum(mb[0, rows, :], mb[1, rows, :]), mb[2, rows, :])
            num = jnp.zeros((256, BLK), F32)
            den = jnp.zeros((256, BLK), F32)
            for br in range(nbr):
                c = jnp.exp(mb[br, rows, :] - m_all)
                num = num + ob[br, rows, :] * c
                den = den + lb[br, rows, :] * c
            o_ref[rows, :] = num / den
            lse_ref[rows, :] = m_all + jnp.log(den)
            return carry

        lax.fori_loop(0, S // 256, merge, 0)

    npair = NH // 2
    blk_spec = lambda off: pl.BlockSpec((S, BLK), lambda b, hp: (b, off + hp))
    return pl.pallas_call(
        body, name="attn_fwd", grid=(BL, npair),
        in_specs=[blk_spec(0), blk_spec(npair), blk_spec(2 * npair),
                  pl.BlockSpec((nbr, 2, BLK, 2 * BLK), lambda b, hp: (0, hp, 0, 0))],
        out_specs=(blk_spec(0), blk_spec(0)),
        out_shape=(SDS((T, AW), F32), SDS((T, AW), F32)),
        scratch_shapes=[pltpu.VMEM((S, BLK), F32)] * 3 + [pltpu.VMEM((S, BLK), BF16)] * 2
        + [pltpu.VMEM((PADK + S, BLK), BF16)] * 2 + [pltpu.VMEM((nbr, 2, 2 * BLK, 2 * BLK), F32)]
        + [pltpu.VMEM((nbr, S, BLK), F32)] * 3,
    )(qkv, qkv, qkv, bias)


def _attn_bwd(qkv, attn, lse, dattn, bias):
    nbr = len(BRANCHES)

    def body(q_ref, k_ref, v_ref, o_ref, lse_ref, do_ref, bias_ref,
             dq_ref, dk_ref, dv_ref, sq_ref, sk_ref, sv_ref, db_ref,
             qf, kf, vf, dl, dqa, dka, dva, qs0, qs1, ds0, ds1, ks, vs, dks, dvs, bm):
        b = pl.program_id(1)
        qf[...] = q_ref[...].astype(F32)
        kf[...] = k_ref[...].astype(F32)
        vf[...] = v_ref[...].astype(F32)
        dqa[...] = jnp.zeros_like(dqa)
        dka[...] = jnp.zeros_like(dka)
        dva[...] = jnp.zeros_like(dva)
        _masked_bias(bias_ref, bm)
        ks[pl.ds(0, PADK), :] = jnp.zeros((PADK, BLK), BF16)
        vs[pl.ds(0, PADK), :] = jnp.zeros((PADK, BLK), BF16)
        head0 = lax.broadcasted_iota(jnp.int32, (BLK, BLK), 1) < HD
        split_q = _head_split(lambda t: t * SCALE)
        split_do = _head_split(lambda t: t)

        @pl.when(b == 0)
        def _():
            db_ref[...] = jnp.zeros_like(db_ref)
            sq_ref[...] = jnp.zeros_like(sq_ref)
            sk_ref[...] = jnp.zeros_like(sk_ref)
            sv_ref[...] = jnp.zeros_like(sv_ref)

        def delta(i, carry):
            rows = pl.ds(pl.multiple_of(i * 256, 256), 256)
            prod = do_ref[rows, :] * o_ref[rows, :]
            h0 = lax.broadcasted_iota(jnp.int32, (256, BLK), 1) < HD
            d0 = jnp.sum(jnp.where(h0, prod, 0.0), axis=-1, keepdims=True)
            d1 = jnp.sum(jnp.where(h0, 0.0, prod), axis=-1, keepdims=True)
            dl[rows, :] = jnp.where(h0, d0, d1)
            return carry

        lax.fori_loop(0, S // 256, delta, 0)

        for br in range(nbr):
            nblk = _branch_geometry(br)[2]

            def stage(src, off, rows):
                qs0[pl.ds(off, rows), :], qs1[pl.ds(off, rows), :] = split_q(qf[src, :])
                ds0[pl.ds(off, rows), :], ds1[pl.ds(off, rows), :] = split_do(do_ref[src, :])
                ks[pl.ds(PADK + off, rows), :] = kf[src, :].astype(BF16)
                vs[pl.ds(PADK + off, rows), :] = vf[src, :].astype(BF16)

            _sub_layout_loop(br, stage)
            dks[...] = jnp.zeros_like(dks)
            dvs[...] = jnp.zeros_like(dvs)

            def blk(i, carry, br=br, nblk=nblk):
                base = pl.multiple_of(i * BLK, BLK)
                rows, n = _token_rows(br, i)
                q01 = jnp.concatenate([qs0[pl.ds(base, BLK), :], qs1[pl.ds(base, BLK), :]], axis=0)
                do01 = jnp.concatenate([ds0[pl.ds(base, BLK), :], ds1[pl.ds(base, BLK), :]], axis=0)
                lse_b = lse_ref[rows, :]
                dl_b = dl[rows, :]
                lse01 = jnp.concatenate([lse_b[:, 0:1], lse_b[:, HD:HD + 1]], axis=0)
                dl01 = jnp.concatenate([dl_b[:, 0:1], dl_b[:, HD:HD + 1]], axis=0)
                if nblk > 1:
                    krows = pl.ds(base, 2 * BLK)
                    bias_m = bm[br, jnp.minimum(n, 1)]
                else:
                    krows = pl.ds(PADK + base, BLK)
                    bias_m = bm[br, 0, :, BLK:]
                kcat = ks[krows, :]
                vcat = vs[krows, :]
                p = jnp.exp(_dot_nt(q01, kcat) + bias_m - lse01)
                dsv = p * (_dot_nt(do01, vcat) - dl01)
                if nblk > 1:
                    db_ref[br, 0] += dsv[:BLK]
                    db_ref[br, 1] += dsv[BLK:]
                else:
                    db_ref[br, 0, :, BLK:] += dsv[:BLK]
                    db_ref[br, 1, :, BLK:] += dsv[BLK:]
                dsb = dsv.astype(BF16)
                dq01 = _dot(dsb, kcat)
                dqa[rows, :] = dqa[rows, :] + jnp.where(head0, dq01[:BLK], dq01[BLK:])
                dks[krows, :] = dks[krows, :] + _dot_tn(dsb, q01)
                dvs[krows, :] = dvs[krows, :] + _dot_tn(p.astype(BF16), do01)
                return carry

            lax.fori_loop(0, 16, blk, 0, unroll=ATTN_UNROLL)

            def fold(src, off, rows):
                dka[src, :] = dka[src, :] + dks[pl.ds(PADK + off, rows), :]
                dva[src, :] = dva[src, :] + dvs[pl.ds(PADK + off, rows), :]

            _sub_layout_loop(br, fold)

        def flush(i, carry):
            rows = pl.ds(pl.multiple_of(i * 256, 256), 256)
            for acc, out, cs, mul in ((dqa, dq_ref, sq_ref, SCALE), (dka, dk_ref, sk_ref, 1.0), (dva, dv_ref, sv_ref, 1.0)):
                val = acc[rows, :] * mul
                out[rows, :] = val.astype(BF16)
                cs[...] += _colsum(val)
            return carry

        lax.fori_loop(0, S // 256, flush, 0)

    npair = NH // 2
    blk_spec = lambda off: pl.BlockSpec((S, BLK), lambda hp, b: (b, off + hp))
    sum_spec = pl.BlockSpec((1, BLK), lambda hp, b: (0, hp))
    return pl.pallas_call(
        body, name="attn_bwd", grid=(npair, BL),
        in_specs=[blk_spec(0), blk_spec(npair), blk_spec(2 * npair), blk_spec(0), blk_spec(0), blk_spec(0),
                  pl.BlockSpec((nbr, 2, BLK, 2 * BLK), lambda hp, b: (0, hp, 0, 0))],
        out_specs=(blk_spec(0), blk_spec(0), blk_spec(0), sum_spec, sum_spec, sum_spec,
                   pl.BlockSpec((nbr, 2, BLK, 2 * BLK), lambda hp, b: (0, hp, 0, 0))),
        out_shape=(SDS((T, AW), BF16), SDS((T, AW), BF16), SDS((T, AW), BF16),
                   SDS((1, AW), F32), SDS((1, AW), F32), SDS((1, AW), F32),
                   SDS((nbr, NH, BLK, 2 * BLK), F32)),
        scratch_shapes=[pltpu.VMEM((S, BLK), F32)] * 7 + [pltpu.VMEM((S, BLK), BF16)] * 4
        + [pltpu.VMEM((PADK + S, BLK), BF16)] * 2 + [pltpu.VMEM((PADK + S, BLK), F32)] * 2
        + [pltpu.VMEM((nbr, 2, 2 * BLK, 2 * BLK), F32)],
    )(qkv, qkv, qkv, attn, lse, dattn, bias)


CH = 256
PADR = 32


def _tap_phases(offset_of_tap):
    taps = sorted((offset_of_tap(k) % 8, offset_of_tap(k) - offset_of_tap(k) % 8, k) for k in range(CK))
    assert all(lo + CH + ph <= CH + PADR for ph, lo, _ in taps)
    return taps


def _rows_up(win):
    made = {0: win}

    def get(phase):
        if phase not in made:
            made[phase] = pltpu.roll(win, win.shape[0] - phase, 0)
        return made[phase]
    return get


def _conv_fwd(ag, conv_w, conv_b):
    def body(ag_ref, w_ref, b_ref, u1_ref, u0p):
        u0p[pl.ds(0, PADR), :] = jnp.zeros((PADR, CW), F32)

        def glu(i, carry):
            t0 = pl.multiple_of(i * CH, CH)
            a = ag_ref[pl.ds(t0, CH), :CW]
            g = ag_ref[pl.ds(t0, CH), CW:]
            u0p[pl.ds(PADR + t0, CH), :] = a * _sigmoid(g)
            return carry

        lax.fori_loop(0, S // CH, glu, 0)

        def conv(i, carry):
            t0 = pl.multiple_of(i * CH, CH)
            win = u0p[pl.ds(t0, CH + PADR), :]
            acc = jnp.zeros((CH, CW), F32) + b_ref[...]
            up = _rows_up(win)
            for phase, lo, k in _tap_phases(lambda k: PADR - (CK - 1) + k):
                acc = acc + up(phase)[lo:lo + CH, :] * w_ref[k:k + 1, :]
            u1_ref[pl.ds(t0, CH), :] = acc
            return carry

        lax.fori_loop(0, S // CH, conv, 0)

    return pl.pallas_call(
        body, name="conv_fwd", grid=(BL,),
        in_specs=[pl.BlockSpec((S, 2 * CW), lambda b: (b, 0)),
                  pl.BlockSpec((CK, CW), lambda b: (0, 0)),
                  pl.BlockSpec((1, CW), lambda b: (0, 0))],
        out_specs=pl.BlockSpec((S, CW), lambda b: (b, 0)),
        out_shape=SDS((T, CW), F32),
        scratch_shapes=[pltpu.VMEM((S + PADR, CW), F32)],
    )(ag, conv_w, conv_b)


def _conv_post(u1, cg, cb):
    mu = _rowmean(u1)
    uc = u1 - mu
    rstd = lax.rsqrt(_rowmean(uc * uc) + LN_EPS)
    xh = uc * rstd
    u2 = xh * cg + cb
    sg = _sigmoid(u2)
    return xh, rstd, u2, sg, u2 * sg


def _mix_fwd(attn, u1, ga, gc, cg, cb):
    def body(a_ref, u_ref, ga_ref, gc_ref, cg_ref, cb_ref, o_ref):
        a = a_ref[...]
        ra = lax.rsqrt(_rowmean(a * a) + LN_EPS)
        o_ref[:, :AW] = (a * ra * ga_ref[...]).astype(BF16)
        _, _, _, _, u3 = _conv_post(u_ref[...], cg_ref[...], cb_ref[...])
        rc = lax.rsqrt(_rowmean(u3 * u3) + LN_EPS)
        o_ref[:, AW:] = (u3 * rc * gc_ref[...]).astype(BF16)

    vec = lambda w: pl.BlockSpec((1, w), lambda m: (0, 0))
    return pl.pallas_call(
        body, name="mix_fwd", grid=(T // TM,),
        in_specs=[pl.BlockSpec((TM, AW), lambda m: (m, 0)), pl.BlockSpec((TM, CW), lambda m: (m, 0)),
                  vec(AW), vec(CW), vec(CW), vec(CW)],
        out_specs=pl.BlockSpec((TM, D), lambda m: (m, 0)),
        out_shape=SDS((T, D), BF16),
    )(attn, u1, ga, gc, cg, cb)


def _mix_bwd(dz1, w_out, attn, u1, ga, gc, cg, cb):
    def body(dz_ref, w_ref, a_ref, u_ref, ga_ref, gc_ref, cg_ref, cb_ref,
             da_ref, du_ref, g_an, g_cn, g_lg, g_lb, g_cb):
        @pl.when(pl.program_id(0) == 0)
        def _():
            for r in (g_an, g_cn, g_lg, g_lb, g_cb):
                r[...] = jnp.zeros_like(r)

        dm = _dot_nt(dz_ref[...].astype(BF16), w_ref[...])
        a = a_ref[...]
        dna = dm[:, :AW]
        ra = lax.rsqrt(_rowmean(a * a) + LN_EPS)
        g_an[...] += _colsum(dna * a * ra)
        dat = dna * ga_ref[...]
        da_ref[...] = ra * dat - a * (ra * ra * ra) * _rowmean(dat * a)

        xh, rstd, u2, sg, u3 = _conv_post(u_ref[...], cg_ref[...], cb_ref[...])
        dnc = dm[:, AW:]
        rc = lax.rsqrt(_rowmean(u3 * u3) + LN_EPS)
        g_cn[...] += _colsum(dnc * u3 * rc)
        dut = dnc * gc_ref[...]
        du3 = rc * dut - u3 * (rc * rc * rc) * _rowmean(dut * u3)
        du2 = du3 * sg * (1.0 + u2 * (1.0 - sg))
        g_lg[...] += _colsum(du2 * xh)
        g_lb[...] += _colsum(du2)
        dxh = du2 * cg_ref[...]
        du1 = rstd * (dxh - _rowmean(dxh) - xh * _rowmean(dxh * xh))
        g_cb[...] += _colsum(du1)
        du_ref[...] = du1

    vec = lambda w: pl.BlockSpec((1, w), lambda m: (0, 0))
    return pl.pallas_call(
        body, name="mix_bwd", grid=(T // TM,),
        in_specs=[pl.BlockSpec((TM, D), lambda m: (m, 0)), pl.BlockSpec((D, D), lambda m: (0, 0)),
                  pl.BlockSpec((TM, AW), lambda m: (m, 0)),
                  pl.BlockSpec((TM, CW), lambda m: (m, 0)), vec(AW), vec(CW), vec(CW), vec(CW)],
        out_specs=(pl.BlockSpec((TM, AW), lambda m: (m, 0)), pl.BlockSpec((TM, CW), lambda m: (m, 0)),
                   vec(AW), vec(CW), vec(CW), vec(CW), vec(CW)),
        out_shape=(SDS((T, AW), F32), SDS((T, CW), F32),
                   SDS((1, AW), F32), SDS((1, CW), F32), SDS((1, CW), F32), SDS((1, CW), F32), SDS((1, CW), F32)),
    )(dz1, w_out, attn, u1, ga, gc, cg, cb)


def _conv_bwd(du1, ag, conv_w):
    def body(du_ref, ag_ref, w_ref, dag_ref, cs_ref, gw_ref, u0p, dup):
        @pl.when(pl.program_id(0) == 0)
        def _():
            cs_ref[...] = jnp.zeros_like(cs_ref)
            gw_ref[...] = jnp.zeros_like(gw_ref)

        u0p[pl.ds(0, PADR), :] = jnp.zeros((PADR, CW), F32)
        dup[pl.ds(S, PADR), :] = jnp.zeros((PADR, CW), F32)

        def fill(i, carry):
            t0 = pl.multiple_of(i * CH, CH)
            a = ag_ref[pl.ds(t0, CH), :CW]
            g = ag_ref[pl.ds(t0, CH), CW:]
            u0p[pl.ds(PADR + t0, CH), :] = a * _sigmoid(g)
            dup[pl.ds(t0, CH), :] = du_ref[pl.ds(t0, CH), :]
            return carry

        lax.fori_loop(0, S // CH, fill, 0)

        def chunk(i, carry):
            t0 = pl.multiple_of(i * CH, CH)
            d = dup[pl.ds(t0, CH), :]
            win_u = u0p[pl.ds(t0, CH + PADR), :]
            win_d = dup[pl.ds(t0, CH + PADR), :]
            du0 = jnp.zeros((CH, CW), F32)
            up_u = _rows_up(win_u)
            for phase, lo, k in _tap_phases(lambda k: PADR - (CK - 1) + k):
                gw_ref[k:k + 1, :] += _colsum(d * up_u(phase)[lo:lo + CH, :])
            up_d = _rows_up(win_d)
            for phase, lo, k in _tap_phases(lambda k: CK - 1 - k):
                du0 = du0 + up_d(phase)[lo:lo + CH, :] * w_ref[k:k + 1, :]
            a = ag_ref[pl.ds(t0, CH), :CW]
            sg = _sigmoid(ag_ref[pl.ds(t0, CH), CW:])
            da = du0 * sg
            dg = du0 * a * sg * (1.0 - sg)
            dag_ref[pl.ds(t0, CH), :CW] = da.astype(BF16)
            dag_ref[pl.ds(t0, CH), CW:] = dg.astype(BF16)
            cs_ref[:, :CW] += _colsum(da)
            cs_ref[:, CW:] += _colsum(dg)
            return carry

        lax.fori_loop(0, S // CH, chunk, 0)

    return pl.pallas_call(
        body, name="conv_bwd", grid=(BL,),
        in_specs=[pl.BlockSpec((S, CW), lambda b: (b, 0)), pl.BlockSpec((S, 2 * CW), lambda b: (b, 0)),
                  pl.BlockSpec((CK, CW), lambda b: (0, 0))],
        out_specs=(pl.BlockSpec((S, 2 * CW), lambda b: (b, 0)),
                   pl.BlockSpec((1, 2 * CW), lambda b: (0, 0)),
                   pl.BlockSpec((PADR, CW), lambda b: (0, 0))),
        out_shape=(SDS((T, 2 * CW), BF16), SDS((1, 2 * CW), F32), SDS((PADR, CW), F32)),
        scratch_shapes=[pltpu.VMEM((S + PADR, CW), F32), pltpu.VMEM((S + PADR, CW), F32)],
    )(du1, ag, conv_w)


def _layer_norm_fwd(z):
    mu = _rowmean(z)
    zc = z - mu
    rstd = lax.rsqrt(_rowmean(zc * zc) + LN_EPS)
    return zc * rstd, rstd


def _layer_norm_bwd(dy, xh, rstd, g):
    dxh = dy * g
    return rstd * (dxh - _rowmean(dxh) - xh * _rowmean(dxh * xh))


def _out_proj_ln1(mixed, w_out, x2, g1, b1):
    def body(a_ref, w_ref, x_ref, g_ref, b_ref, xh_ref, rstd_ref, x1_ref):
        z = ALPHA * x_ref[...] + _dot(a_ref[...], w_ref[...])
        xh, rstd = _layer_norm_fwd(z)
        xh_ref[...] = xh
        rstd_ref[...] = rstd
        x1_ref[...] = (xh * g_ref[...] + b_ref[...]).astype(BF16)

    vec = pl.BlockSpec((1, D), lambda m: (0, 0))
    row = pl.BlockSpec((TM, D), lambda m: (m, 0))
    return pl.pallas_call(
        body, name="out_proj_ln1", grid=(T // TM,),
        in_specs=[row, pl.BlockSpec((D, D), lambda m: (0, 0)), row, vec, vec],
        out_specs=(row, pl.BlockSpec((TM, 1), lambda m: (m, 0)), row),
        out_shape=(SDS((T, D), F32), SDS((T, 1), F32), SDS((T, D), BF16)),
    )(mixed, w_out, x2, g1, b1)


def _seq_start(m):
    return lax.bitwise_and(m, S // TMF - 1) == 0


def _shift_down(x, before, k):
    rolled = pltpu.roll(x, k, 0)
    row = lax.broadcasted_iota(jnp.int32, before.shape, 0)
    head = jnp.where(row < k, pltpu.roll(before, k, 0), rolled[:8])
    return jnp.concatenate([head, rolled[8:]], axis=0)


def _shift_up(x, after, k):
    n = x.shape[0]
    rolled = pltpu.roll(x, n - k, 0)
    row = lax.broadcasted_iota(jnp.int32, after.shape, 0)
    tail = jnp.where(row >= 8 - k, pltpu.roll(after, 8 - k, 0), rolled[n - 8:])
    return jnp.concatenate([rolled[:n - 8], tail], axis=0)


def _ffn_up(x1b, w_up, fcw, fcb):
    def body(x_ref, wg_ref, wv_ref, cwg_ref, cwv_ref, cbg_ref, cbv_ref, up_ref, gv_ref, act_ref, prev_g, prev_v):
        @pl.when(_seq_start(pl.program_id(1)))
        def _():
            prev_g[...] = jnp.zeros_like(prev_g)
            prev_v[...] = jnp.zeros_like(prev_v)

        x = x_ref[...]
        outs = []
        for w_ref, cw_ref, cb_ref, prev, lo in ((wg_ref, cwg_ref, cbg_ref, prev_g, 0), (wv_ref, cwv_ref, cbv_ref, prev_v, FT)):
            u = _dot_nt(x, w_ref[...])
            up_ref[:, lo:lo + FT] = u.astype(BF16)
            before = prev[...]
            y = (cw_ref[2:3, :] * u + cw_ref[1:2, :] * _shift_down(u, before, 1)
                 + cw_ref[0:1, :] * _shift_down(u, before, 2) + cb_ref[...])
            prev[...] = u[TMF - 8:]
            gv_ref[:, lo:lo + FT] = y.astype(BF16)
            outs.append(y)
        gate, val = outs
        act_ref[...] = (gate * _sigmoid(gate) * val).astype(BF16)

    wspec = lambda off: pl.BlockSpec((FT, D), lambda n, m: (n + off, 0))
    cwspec = lambda off: pl.BlockSpec((FK, FT), lambda n, m: (0, n + off))
    cbspec = lambda off: pl.BlockSpec((1, FT), lambda n, m: (0, n + off))
    pair = pl.BlockSpec((TMF, 2 * FT), lambda n, m: (m, n))
    return pl.pallas_call(
        body, name="ffn_up", grid=(NFT, T // TMF),
        in_specs=[pl.BlockSpec((TMF, D), lambda n, m: (m, 0)), wspec(0), wspec(NFT),
                  cwspec(0), cwspec(NFT), cbspec(0), cbspec(NFT)],
        out_specs=(pair, pair, pl.BlockSpec((TMF, FT), lambda n, m: (m, n))),
        out_shape=(SDS((T, 2 * DFF), BF16), SDS((T, 2 * DFF), BF16), SDS((T, DFF), BF16)),
        scratch_shapes=[pltpu.VMEM((8, FT), F32)] * 2,
    )(x1b, w_up, w_up, fcw, fcw, fcb, fcb)


def _ffn_down_loss(act, w_down, xh1, g1, b1, g2, b2, target):
    def body(a_ref, w_ref, xh1_ref, g1_ref, b1_ref, g2_ref, b2_ref, t_ref, dz_ref, loss_ref, gg_ref, gb_ref):
        @pl.when(pl.program_id(0) == 0)
        def _():
            loss_ref[...] = jnp.zeros_like(loss_ref)
            gg_ref[...] = jnp.zeros_like(gg_ref)
            gb_ref[...] = jnp.zeros_like(gb_ref)

        x1 = xh1_ref[...] * g1_ref[...] + b1_ref[...]
        z = ALPHA * x1 + _dot(a_ref[...], w_ref[...])
        xh, rstd = _layer_norm_fwd(z)
        diff = xh * g2_ref[...] + b2_ref[...] - t_ref[...]
        loss_ref[...] += 0.5 * _colsum(_rowmean(diff * diff))
        dout = diff * (1.0 / D)
        gg_ref[...] += _colsum(dout * xh)
        gb_ref[...] += _colsum(dout)
        dz_ref[...] = _layer_norm_bwd(dout, xh, rstd, g2_ref[...])

    vec = pl.BlockSpec((1, D), lambda m: (0, 0))
    row = pl.BlockSpec((TM, D), lambda m: (m, 0))
    return pl.pallas_call(
        body, name="ffn_down_loss", grid=(T // TM,),
        in_specs=[pl.BlockSpec((TM, DFF), lambda m: (m, 0)), pl.BlockSpec((DFF, D), lambda m: (0, 0)),
                  row, vec, vec, vec, vec, row],
        out_specs=(row, pl.BlockSpec((1, 1), lambda m: (0, 0)), vec, vec),
        out_shape=(SDS((T, D), F32), SDS((1, 1), F32), SDS((1, D), F32), SDS((1, D), F32)),
    )(act, w_down, xh1, g1, b1, g2, b2, target)


def _ffn_down_bwd(dz2, w_down, gv, up, fcw):
    tiles = T // TMF

    def body(dz_ref, wd_ref, gv_ref, up_ref, cwg_ref, cwv_ref,
             dpre_ref, csg_ref, csv_ref, gwg_ref, gwv_ref, next_g, next_v):
        step = pl.program_id(1)
        tile = tiles - 1 - step

        @pl.when(step == 0)
        def _():
            for r in (csg_ref, csv_ref, gwg_ref, gwv_ref, next_g, next_v):
                r[...] = jnp.zeros_like(r)

        seq_end = lax.bitwise_and(tile + 1, S // TMF - 1) == 0
        dact = _dot_nt(dz_ref[...].astype(BF16), wd_ref[...])
        gate = gv_ref[:, :FT].astype(F32)
        val = gv_ref[:, FT:].astype(F32)
        sg = _sigmoid(gate)
        gs = gate * sg
        halves = ((dact * val * (sg + gs * (1.0 - sg)), cwg_ref, csg_ref, gwg_ref, next_g, 0),
                  (dact * gs, cwv_ref, csv_ref, gwv_ref, next_v, FT))
        for d0, cw_ref, cs_ref, gw_ref, nxt, lo in halves:
            after = jnp.where(seq_end, 0.0, nxt[...])
            d1 = _shift_up(d0, after, 1)
            d2 = _shift_up(d0, after, 2)
            nxt[...] = d0[:8]
            dpre_ref[:, lo:lo + FT] = (cw_ref[2:3, :] * d0 + cw_ref[1:2, :] * d1 + cw_ref[0:1, :] * d2).astype(BF16)
            cs_ref[...] += _colsum(d0)
            u = up_ref[:, lo:lo + FT].astype(F32)
            for k, dk in enumerate((d2, d1, d0)):
                gw_ref[k:k + 1, :] += _colsum(dk * u)

    cs = pl.BlockSpec((1, FT), lambda n, m: (0, n))
    gw = pl.BlockSpec((FK, FT), lambda n, m: (0, n))
    cwspec = lambda off: pl.BlockSpec((FK, FT), lambda n, m: (0, n + off))
    pair = pl.BlockSpec((TMF, 2 * FT), lambda n, m: (tiles - 1 - m, n))
    return pl.pallas_call(
        body, name="ffn_down_bwd", grid=(NFT, tiles),
        in_specs=[pl.BlockSpec((TMF, D), lambda n, m: (tiles - 1 - m, 0)), pl.BlockSpec((FT, D), lambda n, m: (n, 0)),
                  pair, pair, cwspec(0), cwspec(NFT)],
        out_specs=(pair, cs, cs, gw, gw),
        out_shape=(SDS((T, 2 * DFF), BF16), SDS((1, DFF), F32), SDS((1, DFF), F32),
                   SDS((FK, DFF), F32), SDS((FK, DFF), F32)),
        scratch_shapes=[pltpu.VMEM((8, FT), F32)] * 2,
    )(dz2, w_down, gv, up, fcw, fcw)


def _ffn_up_bwd_ln1(dpre, w_up, dz2, xh1, rstd1, g1):
    def body(a_ref, w_ref, dz2_ref, xh_ref, rstd_ref, g_ref, dz1_ref, gg_ref, gb_ref):
        @pl.when(pl.program_id(0) == 0)
        def _():
            gg_ref[...] = jnp.zeros_like(gg_ref)
            gb_ref[...] = jnp.zeros_like(gb_ref)

        for sub in range(TM // TMF):
            rows = pl.ds(sub * TMF, TMF)
            dx1 = ALPHA * dz2_ref[rows, :]
            for n in range(NFT):
                for half in range(2):
                    a = a_ref[rows, (2 * n + half) * FT:(2 * n + half + 1) * FT]
                    w = w_ref[pl.ds((half * NFT + n) * FT, FT), :]
                    dx1 = dx1 + _dot(a, w)
            xh = xh_ref[rows, :]
            gg_ref[...] += _colsum(dx1 * xh)
            gb_ref[...] += _colsum(dx1)
            dz1_ref[rows, :] = _layer_norm_bwd(dx1, xh, rstd_ref[rows, :], g_ref[...])

    vec = pl.BlockSpec((1, D), lambda m: (0, 0))
    row = pl.BlockSpec((TM, D), lambda m: (m, 0))
    return pl.pallas_call(
        body, name="ffn_up_bwd_ln1", grid=(T // TM,),
        in_specs=[pl.BlockSpec((TM, 2 * DFF), lambda m: (m, 0)),
                  pl.BlockSpec((2 * DFF, D), lambda m: (0, 0), pipeline_mode=pl.Buffered(1)),
                  row, row, pl.BlockSpec((TM, 1), lambda m: (m, 0)), vec],
        out_specs=(row, vec, vec),
        out_shape=(SDS((T, D), F32), SDS((1, D), F32), SDS((1, D), F32)),
    )(dpre, w_up, dz2, xh1, rstd1, g1)


def _grad_w_up(dpre, x1b):
    tk = 1024

    def body(a_ref, b_ref, o_ref, acc):
        k = pl.program_id(1)

        @pl.when(k == 0)
        def _():
            acc[...] = jnp.zeros_like(acc)

        acc[...] += _dot_tn(a_ref[...], b_ref[...])

        @pl.when(k == T // tk - 1)
        def _():
            o_ref[0] = acc[pl.ds(0, FT), :].astype(o_ref.dtype)
            o_ref[1] = acc[pl.ds(FT, FT), :].astype(o_ref.dtype)

    out = pl.pallas_call(
        body, name="grad_w_up", grid=(NFT, T // tk),
        in_specs=[pl.BlockSpec((tk, 2 * FT), lambda n, k: (k, n)), pl.BlockSpec((tk, D), lambda n, k: (k, 0))],
        out_specs=pl.BlockSpec((2, FT, D), lambda n, k: (0, n, 0)),
        out_shape=SDS((2, DFF, D), GRAD_WIRE),
        scratch_shapes=[pltpu.VMEM((2 * FT, D), F32)],
    )(dpre, x1b)
    return out.reshape(2 * DFF, D)


def _row_tile(rows, cols):
    if rows * cols * 4 <= (1 << 20) or rows % 8:
        return rows
    for t in (256, 176, 128, 88, 64, 32, 16, 8):
        if rows % t == 0 and t * cols * 4 <= (1 << 20):
            return t
    return 8


def _sum8(r, name):
    _, rows, cols = r.shape
    tr = _row_tile(rows, cols)

    def body(r_ref, o_ref):
        acc = r_ref[0].astype(F32)
        for p in range(1, NDEV):
            acc = acc + r_ref[p].astype(F32)
        o_ref[...] = acc

    return pl.pallas_call(
        body, name=name, grid=(rows // tr,),
        in_specs=[pl.BlockSpec((NDEV, tr, cols), lambda i: (0, i, 0))],
        out_specs=pl.BlockSpec((tr, cols), lambda i: (i, 0)),
        out_shape=SDS((rows, cols), F32),
    )(r)


def _sum8_adamw(r, w, m, v, name):
    rows, cols = w.shape
    tr = _row_tile(rows, cols)

    def body(r_ref, w_ref, m_ref, v_ref, g_out, d_ref, nm_ref, nv_ref):
        g_ = r_ref[0].astype(F32)
        for p in range(1, NDEV):
            g_ = g_ + r_ref[p].astype(F32)
        m_ = B1 * m_ref[...] + (1.0 - B1) * g_
        v_ = B2 * v_ref[...] + (1.0 - B2) * jnp.square(g_)
        m_hat = m_ / (1.0 - B1 ** STEP)
        v_hat = v_ / (1.0 - B2 ** STEP)
        g_out[...] = g_
        d_ref[...] = -LR * (m_hat / (jnp.sqrt(v_hat) + AEPS) + WD * w_ref[...])
        nm_ref[...] = m_
        nv_ref[...] = v_

    spec = pl.BlockSpec((tr, cols), lambda i: (i, 0))
    shp = SDS((rows, cols), F32)
    return pl.pallas_call(
        body, name=name, grid=(rows // tr,),
        in_specs=[pl.BlockSpec((NDEV, tr, cols), lambda i: (0, i, 0))] + [spec] * 3, out_specs=(spec,) * 4,
        out_shape=(shp,) * 4,
    )(r, w, m, v)


def _adamw_many(ws, gs, ms, vs, name):
    n = len(ws)

    def body(*refs):
        for i in range(n):
            w_ref, g_ref, m_ref, v_ref, d_ref, nm_ref, nv_ref = refs[i::n]
            g_ = g_ref[...]
            m_ = B1 * m_ref[...] + (1.0 - B1) * g_
            v_ = B2 * v_ref[...] + (1.0 - B2) * jnp.square(g_)
            m_hat = m_ / (1.0 - B1 ** STEP)
            v_hat = v_ / (1.0 - B2 ** STEP)
            d_ref[...] = -LR * (m_hat / (jnp.sqrt(v_hat) + AEPS) + WD * w_ref[...])
            nm_ref[...] = m_
            nv_ref[...] = v_

    shapes = tuple(SDS(w.shape, F32) for w in ws)
    res = pl.pallas_call(body, name=name, out_shape=shapes * 3)(*ws, *gs, *ms, *vs)
    return res[:n], res[n:2 * n], res[2 * n:]


def _local_step(x2, target, rel_table, first_weights, b_in, conv_b, conv_ln_g, conv_ln_b, attn_norm_g,
                conv_norm_g, late_weights, ln1_g, ln1_b, ffn_conv_b, ln2_g, ln2_b, ship_ffn_grads, ship_tail):
    buckets = jnp.asarray(_bucket_maps())
    bias = _bias_table(rel_table, buckets)
    w_in_t, conv_w, ffn_conv_w = first_weights(bias)

    qkv, ag = _proj_in(x2, w_in_t, b_in)
    attn, lse = _attn_fwd(qkv, bias)
    u1 = _conv_fwd(ag, conv_w, conv_b)
    mixed = _mix_fwd(attn, u1, attn_norm_g, conv_norm_g, conv_ln_g, conv_ln_b)
    w_out = late_weights(0, mixed)
    xh1, rstd1, x1b = _out_proj_ln1(mixed, w_out, x2, ln1_g, ln1_b)
    w_up = late_weights(1, x1b)
    up, gv, act = _ffn_up(x1b, w_up, ffn_conv_w, ffn_conv_b)
    w_down = late_weights(2, act)
    dz2, loss, g_ln2_g, g_ln2_b = _ffn_down_loss(act, w_down, xh1, ln1_g, ln1_b, ln2_g, ln2_b, target)

    dpre, cs_g, cs_v, gfw_g, gfw_v = _ffn_down_bwd(dz2, w_down, gv, up, ffn_conv_w)
    g_w_down = _mm_tn(act, dz2, DFF // 2, 512, "grad_w_down")
    dz1, g_ln1_g, g_ln1_b = _ffn_up_bwd_ln1(dpre, w_up, dz2, xh1, rstd1, ln1_g)
    g_w_out = _mm_tn(mixed, dz1, D, 512, "grad_w_out")
    zero = ship_ffn_grads(g_w_down, _grad_w_up(dpre, x1b), g_w_out)
    dattn, du1, g_an, g_cn, g_clg, g_clb, g_cb = _mix_bwd(
        dz1, w_out, attn, u1, attn_norm_g + zero, conv_norm_g, conv_ln_g, conv_ln_b)
    dag, cs_ag, g_conv_w = _conv_bwd(du1, ag, conv_w)
    dq, dk, dv, cs_q, cs_k, cs_v2, dbias = _attn_bwd(qkv, attn, lse, dattn, bias)
    g_rel = _rel_table_grad(dbias, buckets)
    pieces = [dq, dk, dv, dag]
    g_w_in_t = _grad_w_in(pieces, x2)

    grads = dict(
        rel_table=g_rel,
        b_in=jnp.concatenate([cs_q, cs_k, cs_v2, cs_ag], axis=1),
        conv_b=g_cb, conv_ln_g=g_clg, conv_ln_b=g_clb, attn_norm_g=g_an, conv_norm_g=g_cn,
        ln1_g=g_ln1_g, ln1_b=g_ln1_b,
        ffn_conv_b=jnp.concatenate([cs_g, cs_v], axis=1),
        ln2_g=g_ln2_g, ln2_b=g_ln2_b,
        conv_w=g_conv_w[:CK],
        ffn_conv_w=jnp.concatenate([gfw_g, gfw_v], axis=1),
    )
    grads["loss"] = loss
    zero11 = ship_tail(g_w_in_t, grads)
    grad_x = _grad_x(pieces, w_in_t, dz1, zero11)
    return loss, grad_x


SMALL = (("rel_table", (NBUCKET, NH)), ("b_in", (1, INW)), ("conv_b", (1, CW)), ("conv_ln_g", (1, CW)),
         ("conv_ln_b", (1, CW)), ("attn_norm_g", (1, AW)), ("conv_norm_g", (1, CW)), ("ln1_g", (1, D)),
         ("ln1_b", (1, D)), ("ffn_conv_b", (1, 2 * DFF)), ("ln2_g", (1, D)), ("ln2_b", (1, D)))
SHARDED_SMALL = (("conv_w", (CK, CW)), ("ffn_conv_w", (FK, 2 * DFF)))


def _pack(parts):
    flat = jnp.concatenate([p.reshape(-1) for p in parts])
    tile = 8 * PACK_LANES
    pad = (-flat.shape[0]) % tile
    return jnp.pad(flat, (0, pad)).reshape(-1, PACK_LANES)


def _unpack(packed, specs):
    flat = packed.reshape(-1)
    out, off = {}, 0
    for name, shp in specs:
        size = int(np.prod(shp))
        out[name] = flat[off:off + size].reshape(shp)
        off += size
    return out


def kernel(x, rel_table, w_in, b_in, conv_w, conv_b, conv_ln_g, conv_ln_b, attn_norm_g, conv_norm_g, w_out, ln1_g, ln1_b, w_up, ffn_conv_w, ffn_conv_b, w_down, ln2_g, ln2_b, loss_target, m_rel_table, m_w_in, m_b_in, m_conv_w, m_conv_b, m_conv_ln_g, m_conv_ln_b, m_attn_norm_g, m_conv_norm_g, m_w_out, m_ln1_g, m_ln1_b, m_w_up, m_ffn_conv_w, m_ffn_conv_b, m_w_down, m_ln2_g, m_ln2_b, v_rel_table, v_w_in, v_b_in, v_conv_w, v_conv_b, v_conv_ln_g, v_conv_ln_b, v_attn_norm_g, v_conv_norm_g, v_w_out, v_ln1_g, v_ln1_b, v_w_up, v_ffn_conv_w, v_ffn_conv_b, v_w_down, v_ln2_g, v_ln2_b):
    given = dict(locals())
    me = 4 * lax.axis_index("x") + 2 * lax.axis_index("y") + lax.axis_index("c")

    cols = lambda a: a.transpose(1, 0, 2).reshape(a.shape[1], NDEV * a.shape[2])
    rows = lambda a: a.reshape(NDEV * a.shape[1], a.shape[2])
    stack = lambda a: a.reshape(NDEV, a.shape[0] // NDEV, a.shape[1])

    small_specs = SMALL + SHARDED_SMALL
    packed_specs = small_specs + (("loss", (1, 1)),)
    grad, delta, new_m, new_v = {}, {}, {}, {}

    def adamw_big(n, partials, transposed=False):
        shp = given[n].shape
        to2d = (lambda a: a.reshape(shp[-2], shp[-1]).T) if transposed else (lambda a: a.reshape(shp[-2], shp[-1]))
        back = (lambda a: a.T.reshape(shp)) if transposed else (lambda a: a.reshape(shp))
        g_, d_, m_, v_ = _sum8_adamw(partials, to2d(given[n]), to2d(given["m_" + n]), to2d(given["v_" + n]), "adamw_" + n)
        grad[n], delta[n], new_m[n], new_v[n] = back(g_), back(d_), back(m_), back(v_)
        return d_

    first_state, zero0 = _exchange_start(
        [(w_in[0].T.astype(BF16), "gather"), (conv_w[0], "gather"), (ffn_conv_w[0], "gather")], "gather_first_start")

    def first_weights(after):
        lands = _exchange_wait(first_state, after, "gather_first_wait")
        return rows(lands[0]), cols(lands[1]), cols(lands[2])

    late_state, zero1 = _exchange_start(
        [(w_out[0].astype(BF16), "gather"), (w_up[0].T.astype(BF16) + zero0.astype(BF16), "gather"),
         (w_down[0].astype(BF16), "gather")], "gather_late_start")

    def late_weights(i, after):
        return rows(_exchange_wait(late_state, after, "gather_late_wait_%d" % i, only=(i,))[i])

    shipped = {}

    def ship_ffn_grads(g_w_down, g_w_up_t, g_w_out):
        shipped["ffn"], zero2 = _exchange_start(
            [(stack(a), "scatter") for a in (g_w_down, g_w_up_t, g_w_out)], "ffn_grads_start")
        return zero2

    def ship_tail(g_w_in_t, small_grads):
        shipped["tail"], zero3 = _exchange_start(
            [(stack(g_w_in_t), "scatter"), (_pack([small_grads[n] for n, _ in packed_specs]), "gather")],
            "tail_grads_start")
        return zero3.reshape(1, 1)

    loss, grad_x = _local_step(
        x.reshape(T, D), loss_target.reshape(T, D), rel_table + zero1, first_weights, b_in, conv_b, conv_ln_g,
        conv_ln_b, attn_norm_g, conv_norm_g, late_weights, ln1_g, ln1_b, ffn_conv_b,
        ln2_g, ln2_b, ship_ffn_grads, ship_tail)

    got_down, got_up, got_out = _exchange_wait(shipped["ffn"], grad_x, "ffn_grads_wait")
    adamw_big("w_down", got_down)
    adamw_big("w_up", got_up, transposed=True)
    last = adamw_big("w_out", got_out)

    got_in, got_small = _exchange_wait(shipped["tail"], last, "tail_grads_wait")
    adamw_big("w_in", got_in, transposed=True)
    small = _unpack(_sum8(got_small, "sum_small"), packed_specs)
    small["conv_w"] = lax.dynamic_slice_in_dim(small["conv_w"], me * (CW // NDEV), CW // NDEV, axis=1)
    small["ffn_conv_w"] = lax.dynamic_slice_in_dim(small["ffn_conv_w"], me * (2 * DFF // NDEV), 2 * DFF // NDEV, axis=1)
    names = [n for n, _ in small_specs]
    two = lambda a: a.reshape(a.shape[-2], a.shape[-1])
    ds, nms, nvs = _adamw_many([two(given[n]) for n in names], [small[n] for n in names],
                               [two(given["m_" + n]) for n in names], [two(given["v_" + n]) for n in names], "adamw_small")
    for n, d_, m_, v_ in zip(names, ds, nms, nvs):
        shp = given[n].shape
        grad[n], delta[n], new_m[n], new_v[n] = small[n].reshape(shp), d_.reshape(shp), m_.reshape(shp), v_.reshape(shp)

    order = ("rel_table", "w_in", "b_in", "conv_w", "conv_b", "conv_ln_g", "conv_ln_b", "attn_norm_g",
             "conv_norm_g", "w_out", "ln1_g", "ln1_b", "w_up", "ffn_conv_w", "ffn_conv_b", "w_down", "ln2_g", "ln2_b")
    return (small["loss"][0, 0], grad_x.reshape(BL, S, D), *[grad[n] for n in order], *[delta[n] for n in order],
            *[new_m[n] for n in order], *[new_v[n] for n in order])
```

```python
import math

import numpy as np
import jax
import jax.numpy as jnp
from jax import lax
from jax.experimental import pallas as pl
from jax.experimental.pallas import tpu as pltpu

F32 = jnp.float32
BF16 = jnp.bfloat16
SDS = jax.ShapeDtypeStruct

NDEV = 8
D = 1024
S = 2048
BL = 2
T = BL * S
NH = 12
HD = 64
AW = NH * HD
CW = D - AW
INW = 3 * AW + 2 * CW
CK = 31
DFF = 2816
FK = 3
BLK = 128
NBUCKET = 32
BRANCHES = ((128, 1), (512, 4), (2048, 16))
ALPHA = 2.0 ** 0.25
LN_EPS = 1e-5
NEG_INF = -1e30
LR, B1, B2, AEPS, WD, STEP = 0.001, 0.9, 0.999, 1e-08, 0.01, 10

TM = 512
FT = 1408
NFT = DFF // FT
TMF = 256
PACK_LANES = 128
GRAD_WIRE = BF16

assert all(w // d == BLK for w, d in BRANCHES)


def _dot(a, b):
    return jnp.dot(a, b, preferred_element_type=F32)


def _dot_nt(a, b):
    return lax.dot_general(a, b, (((1,), (1,)), ((), ())), preferred_element_type=F32)


def _dot_tn(a, b):
    return lax.dot_general(a, b, (((0,), (0,)), ((), ())), preferred_element_type=F32)


def _rowmean(v):
    return jnp.mean(v, axis=-1, keepdims=True)


def _colsum(v):
    return jnp.sum(v, axis=0, keepdims=True)


def _sigmoid(v):
    return jax.nn.sigmoid(v)


_HBM = pl.BlockSpec(memory_space=pltpu.HBM)
_SEM = pl.BlockSpec(memory_space=pltpu.SEMAPHORE)
_EFFECT = pltpu.SideEffectType.DATAFLOW_SIDE_EFFECTING


def _peer_of(k):
    x, y, c = lax.axis_index("x"), lax.axis_index("y"), lax.axis_index("c")
    px = 1 - x if k & 4 else x
    py = 1 - y if k & 2 else y
    pc = 1 - c if k & 1 else c
    return (px, py, pc), 4 * px + 2 * py + pc


def _split_copies(kinds, ins, lands, send_sems, recv_sems, started):
    me = 4 * lax.axis_index("x") + 2 * lax.axis_index("y") + lax.axis_index("c")
    out = []
    for i, kind in enumerate(kinds):
        for k in range(1, NDEV):
            dev, pid = _peer_of(k)
            src = ins[i] if kind == "gather" else ins[i].at[pid]
            dst = lands[i].at[me] if started else lands[i].at[pid]
            slot = i * (NDEV - 1) + k - 1
            out.append(pltpu.make_async_remote_copy(
                src_ref=src, dst_ref=dst, send_sem=send_sems.at[slot], recv_sem=recv_sems.at[slot],
                device_id=dev, device_id_type=pl.DeviceIdType.MESH))
    return out


def _exchange_start(items, name):
    n = len(items)
    kinds = [k for _, k in items]
    srcs = [pltpu.with_memory_space_constraint(a, pltpu.HBM) for a, _ in items]
    lands = []
    for a, k in items:
        shp = (NDEV,) + tuple(a.shape) if k == "gather" else tuple(a.shape)
        lands.append(pltpu.with_memory_space_constraint(lax.empty(shp, a.dtype), pltpu.HBM))

    def body(*refs):
        ins, land_refs = refs[:n], refs[n:2 * n]
        send_sems, recv_sems, own_sems = refs[2 * n:2 * n + 3]
        token = refs[-1]
        for cp in _own_copies(kinds, ins, land_refs, own_sems):
            cp.start()
        for cp in _split_copies(kinds, ins, land_refs, send_sems, recv_sems, True):
            cp.start()
        token[...] = jnp.zeros_like(token)

    sems = pltpu.SemaphoreType.DMA((n * (NDEV - 1),))
    res = pl.pallas_call(
        body, name=name,
        out_shape=(sems, sems, pltpu.SemaphoreType.DMA((n,)),
                   *[pltpu.HBM(a.shape, a.dtype) for a in srcs + lands], SDS((8, 128), F32)),
        in_specs=[_HBM] * (2 * n),
        out_specs=(_SEM, _SEM, _SEM, *[_HBM] * (2 * n), pl.BlockSpec(memory_space=pltpu.VMEM)),
        input_output_aliases={i: 3 + i for i in range(2 * n)},
        compiler_params=pltpu.CompilerParams(has_side_effects=_EFFECT),
    )(*srcs, *lands)
    return (kinds, res[0], res[1], res[2], list(res[3:3 + n]), list(res[3 + n:3 + 2 * n])), res[-1][0, 0]


def _own_copies(kinds, ins, lands, own_sems):
    me = 4 * lax.axis_index("x") + 2 * lax.axis_index("y") + lax.axis_index("c")
    return [pltpu.make_async_copy(ins[i] if kind == "gather" else ins[i].at[me], lands[i].at[me], own_sems.at[i])
            for i, kind in enumerate(kinds)]


def _exchange_wait(state, after, name, only=None):
    kinds, send_sems, recv_sems, own_sems, srcs, lands = state
    n = len(kinds)
    chosen = range(n) if only is None else only

    def body(*refs):
        ins, land_refs = refs[:n], refs[n:2 * n]
        s_sems, r_sems, o_sems = refs[2 * n:2 * n + 3]
        remote = _split_copies(kinds, ins, land_refs, s_sems, r_sems, False)
        own = _own_copies(kinds, ins, land_refs, o_sems)
        for i in chosen:
            for cp in remote[i * (NDEV - 1):(i + 1) * (NDEV - 1)]:
                cp.wait_send()
                cp.wait_recv()
        for i in chosen:
            own[i].wait()

    res = pl.pallas_call(
        body, name=name,
        out_shape=tuple(pltpu.HBM(a.shape, a.dtype) for a in srcs + lands),
        in_specs=[_HBM] * (2 * n) + [_SEM, _SEM, _SEM, pl.BlockSpec(memory_space=pl.ANY)],
        out_specs=tuple([_HBM] * (2 * n)),
        input_output_aliases={i: i for i in range(2 * n)},
        compiler_params=pltpu.CompilerParams(has_side_effects=_EFFECT),
    )(*srcs, *lands, send_sems, recv_sems, own_sems, after)
    srcs[:], lands[:] = res[:n], res[n:]
    return list(lands)


def _cast_x(x2, after):
    def body(x_ref, after_ref, o_ref):
        o_ref[...] = x_ref[...].astype(BF16)

    return pl.pallas_call(
        body, name="cast_x", grid=(T // TM,),
        in_specs=[pl.BlockSpec((TM, D), lambda m: (m, 0)), pl.BlockSpec(memory_space=pl.ANY)],
        out_specs=pl.BlockSpec((TM, D), lambda m: (m, 0)),
        out_shape=SDS((T, D), BF16),
    )(x2, after)


def _proj_in(xb, w_in_t, b_in):
    nq = 3 * AW

    def body(x_ref, w_ref, b_ref, qkv_ref, ag_ref):
        xb = x_ref[...]
        qkv_ref[...] = (_dot_nt(xb, w_ref[pl.ds(0, nq), :]) + b_ref[:, :nq]).astype(BF16)
        ag_ref[...] = _dot_nt(xb, w_ref[pl.ds(nq, 2 * CW), :]) + b_ref[:, nq:]

    return pl.pallas_call(
        body, name="proj_in", grid=(T // TM,),
        in_specs=[pl.BlockSpec((TM, D), lambda m: (m, 0)), pl.BlockSpec((INW, D), lambda m: (0, 0)),
                  pl.BlockSpec((1, INW), lambda m: (0, 0))],
        out_specs=(pl.BlockSpec((TM, nq), lambda m: (m, 0)), pl.BlockSpec((TM, 2 * CW), lambda m: (m, 0))),
        out_shape=(SDS((T, nq), BF16), SDS((T, 2 * CW), F32)),
    )(xb, w_in_t, b_in)


def _grad_x(pieces, w_in_t, dz1, zero):
    widths = [p.shape[1] for p in pieces]

    def body(*refs):
        p_refs = refs[:len(pieces)]
        w_ref, dz_ref, z_ref, o_ref = refs[len(pieces):]
        acc = ALPHA * dz_ref[...] + z_ref[...]
        r0 = 0
        for p_ref, wd in zip(p_refs, widths):
            acc = acc + _dot(p_ref[...], w_ref[pl.ds(r0, wd), :])
            r0 += wd
        o_ref[...] = acc

    row = pl.BlockSpec((TM, D), lambda m: (m, 0))
    return pl.pallas_call(
        body, name="grad_x", grid=(T // TM,),
        in_specs=[pl.BlockSpec((TM, wd), lambda m: (m, 0)) for wd in widths]
        + [pl.BlockSpec((INW, D), lambda m: (0, 0)), row, pl.BlockSpec((1, 1), lambda m: (0, 0))],
        out_specs=row,
        out_shape=SDS((T, D), F32),
    )(*pieces, w_in_t, dz1, zero)


def _grad_w_in(pieces, x2):
    widths = [p.shape[1] for p in pieces]
    tk = 512
    nk = T // tk

    def body(*refs):
        p_refs = refs[:len(pieces)]
        x_ref, o_ref, acc = refs[len(pieces):]
        k = pl.program_id(0)

        @pl.when(k == 0)
        def _():
            acc[...] = jnp.zeros_like(acc)

        xb = x_ref[...]
        r0 = 0
        for p_ref, wd in zip(p_refs, widths):
            acc[pl.ds(r0, wd), :] += _dot_tn(p_ref[...], xb)
            r0 += wd

        @pl.when(k == nk - 1)
        def _():
            o_ref[...] = acc[...].astype(o_ref.dtype)

    return pl.pallas_call(
        body, name="grad_w_in", grid=(nk,),
        in_specs=[pl.BlockSpec((tk, wd), lambda k: (k, 0)) for wd in widths] + [pl.BlockSpec((tk, D), lambda k: (k, 0))],
        out_specs=pl.BlockSpec((INW, D), lambda k: (0, 0)),
        out_shape=SDS((INW, D), GRAD_WIRE),
        scratch_shapes=[pltpu.VMEM((INW, D), F32)],
    )(*pieces, x2)


def _mm_tn(a, b, tn, tk, name):
    t_, na = a.shape
    nb = b.shape[1]
    nk = t_ // tk

    def body(a_ref, b_ref, o_ref, acc):
        k = pl.program_id(1)

        @pl.when(k == 0)
        def _():
            acc[...] = jnp.zeros_like(acc)

        acc[...] += _dot_tn(a_ref[...].astype(BF16), b_ref[...].astype(BF16))

        @pl.when(k == nk - 1)
        def _():
            o_ref[...] = acc[...].astype(o_ref.dtype)

    return pl.pallas_call(
        body, name=name, grid=(na // tn, nk),
        in_specs=[pl.BlockSpec((tk, tn), lambda n, k: (k, n)),
                  pl.BlockSpec((tk, nb), lambda n, k: (k, 0))],
        out_specs=pl.BlockSpec((tn, nb), lambda n, k: (n, 0)),
        out_shape=SDS((na, nb), GRAD_WIRE),
        scratch_shapes=[pltpu.VMEM((tn, nb), F32)],
    )(a, b)


def _bucket_maps():
    qi = np.arange(BLK)[:, None]
    kj = np.arange(2 * BLK)[None, :]
    steps = np.maximum(qi + BLK - kj, 0)
    exact = NBUCKET // 2
    maps = []
    for _, dil in BRANCHES:
        dist = steps * dil
        d_f = np.maximum(dist, 1).astype(np.float32)
        large = exact + (np.log(d_f / np.float32(exact)) / np.float32(math.log(S / exact))
                         * np.float32(NBUCKET - exact)).astype(np.int32)
        large = np.minimum(large, NBUCKET - 1)
        maps.append(np.where(dist < exact, dist, large).astype(np.int32))
    return np.stack(maps)


def _bias_table(rel_table, buckets):
    def body(t_ref, b_ref, o_ref):
        bk = b_ref[0]
        for h in range(NH):
            acc = jnp.zeros((BLK, 2 * BLK), F32)
            for k in range(NBUCKET):
                acc = jnp.where(bk == k, t_ref[k, h], acc)
            o_ref[0, h] = acc

    return pl.pallas_call(
        body, name="bias_table", grid=(len(BRANCHES),),
        in_specs=[pl.BlockSpec(memory_space=pltpu.SMEM),
                  pl.BlockSpec((1, BLK, 2 * BLK), lambda i: (i, 0, 0))],
        out_specs=pl.BlockSpec((1, NH, BLK, 2 * BLK), lambda i: (i, 0, 0, 0)),
        out_shape=SDS((len(BRANCHES), NH, BLK, 2 * BLK), F32),
    )(rel_table, buckets)


def _rel_table_grad(dbias, buckets):
    def body(d_ref, b_ref, o_ref):
        h = pl.program_id(0)
        for k in range(NBUCKET):
            tot = jnp.zeros((1, 1), F32)
            for br in range(len(BRANCHES)):
                sel = jnp.where(b_ref[br] == k, d_ref[br, 0], 0.0)
                tot = tot + jnp.sum(jnp.sum(sel, axis=1, keepdims=True), axis=0, keepdims=True)
            o_ref[0, :, pl.ds(k, 1)] = tot

    out = pl.pallas_call(
        body, name="rel_table_grad", grid=(NH,),
        in_specs=[pl.BlockSpec((len(BRANCHES), 1, BLK, 2 * BLK), lambda h: (0, h, 0, 0)),
                  pl.BlockSpec((len(BRANCHES), BLK, 2 * BLK), lambda h: (0, 0, 0))],
        out_specs=pl.BlockSpec((1, 1, NBUCKET), lambda h: (h, 0, 0)),
        out_shape=SDS((NH, 1, NBUCKET), F32),
    )(dbias, buckets)
    return out.reshape(NH, NBUCKET).T


PADK = BLK
SCALE = 1.0 / math.sqrt(HD)
ATTN_UNROLL = 16


def _branch_geometry(br):
    dil = BRANCHES[br][1]
    sub = S // dil
    return dil, sub, sub // BLK


def _token_rows(br, i):
    dil, _, nblk = _branch_geometry(br)
    if dil == 1:
        return pl.ds(pl.multiple_of(i * BLK, BLK), BLK), i
    r = lax.shift_right_logical(i, nblk.bit_length() - 1)
    n = lax.bitwise_and(i, nblk - 1)
    return pl.ds(r + dil * BLK * n, BLK, stride=dil), n


def _sub_layout_loop(br, step):
    dil, sub, _ = _branch_geometry(br)
    rows = min(sub, 256)
    nchunk = sub // rows

    def it_step(it, carry):
        if dil == 1:
            src = pl.ds(pl.multiple_of(it * rows, rows), rows)
        else:
            r = lax.shift_right_logical(it, nchunk.bit_length() - 1)
            src = pl.ds(r + dil * rows * lax.bitwise_and(it, nchunk - 1), rows, stride=dil)
        step(src, pl.multiple_of(it * rows, BLK), rows)
        return carry

    lax.fori_loop(0, dil * nchunk, it_step, 0)


def _masked_bias(bias_ref, bm):
    qi = lax.broadcasted_iota(jnp.int32, (BLK, 2 * BLK), 0)
    kj = lax.broadcasted_iota(jnp.int32, (BLK, 2 * BLK), 1)
    first = jnp.logical_and(kj >= BLK, kj - BLK <= qi)
    valid = jnp.logical_or(first, jnp.logical_and(kj < BLK, kj >= qi))
    for br in range(len(BRANCHES)):
        for j in range(2):
            b = bias_ref[br, j]
            bm[br, 1, pl.ds(j * BLK, BLK), :] = jnp.where(valid, b, NEG_INF)
            bm[br, 0, pl.ds(j * BLK, BLK), :] = jnp.where(first, b, NEG_INF)


def _head_split(fn):
    def split(t):
        h0 = lax.broadcasted_iota(jnp.int32, t.shape, 1) < HD
        t = fn(t)
        return jnp.where(h0, t, 0.0).astype(BF16), jnp.where(h0, 0.0, t).astype(BF16)
    return split


def _attn_fwd(qkv, bias):
    nbr = len(BRANCHES)

    def body(q_ref, k_ref, v_ref, bias_ref, o_ref, lse_ref, qf, kf, vf, qs0, qs1, ks, vs, bm, ob, mb, lb):
        qf[...] = q_ref[...].astype(F32)
        kf[...] = k_ref[...].astype(F32)
        vf[...] = v_ref[...].astype(F32)
        _masked_bias(bias_ref, bm)
        ks[pl.ds(0, PADK), :] = jnp.zeros((PADK, BLK), BF16)
        vs[pl.ds(0, PADK), :] = jnp.zeros((PADK, BLK), BF16)
        head0 = lax.broadcasted_iota(jnp.int32, (BLK, BLK), 1) < HD
        split_q = _head_split(lambda t: t * SCALE)

        for br in range(nbr):
            nblk = _branch_geometry(br)[2]

            def stage(src, off, rows):
                qs0[pl.ds(off, rows), :], qs1[pl.ds(off, rows), :] = split_q(qf[src, :])
                ks[pl.ds(PADK + off, rows), :] = kf[src, :].astype(BF16)
                vs[pl.ds(PADK + off, rows), :] = vf[src, :].astype(BF16)

            _sub_layout_loop(br, stage)

            def blk(i, carry, br=br, nblk=nblk):
                base = pl.multiple_of(i * BLK, BLK)
                rows, n = _token_rows(br, i)
                q01 = jnp.concatenate([qs0[pl.ds(base, BLK), :], qs1[pl.ds(base, BLK), :]], axis=0)
                if nblk > 1:
                    kcat = ks[pl.ds(base, 2 * BLK), :]
                    vcat = vs[pl.ds(base, 2 * BLK), :]
                    s = _dot_nt(q01, kcat) + bm[br, jnp.minimum(n, 1)]
                else:
                    kcat = ks[pl.ds(PADK + base, BLK), :]
                    vcat = vs[pl.ds(PADK + base, BLK), :]
                    s = _dot_nt(q01, kcat) + bm[br, 0, :, BLK:]
                mx = jnp.max(s, axis=-1, keepdims=True)
                p = jnp.exp(s - mx)
                ls = jnp.sum(p, axis=-1, keepdims=True)
                o = _dot(p.astype(BF16), vcat)
                ob[br, rows, :] = jnp.where(head0, o[:BLK], o[BLK:])
                mb[br, rows, :] = jnp.where(head0, mx[:BLK], mx[BLK:])
                lb[br, rows, :] = jnp.where(head0, ls[:BLK], ls[BLK:])
                return carry

            lax.fori_loop(0, 16, blk, 0, unroll=ATTN_UNROLL)

        def merge(i, carry):
            rows = pl.ds(pl.multiple_of(i * 256, 256), 256)
            m_all = jnp.maximum(jnp.maximum(mb[0, rows, :], mb[1, rows, :]), mb[2, rows, :])
            num = jnp.zeros((256, BLK), F32)
            den = jnp.zeros((256, BLK), F32)
            for br in range(nbr):
                c = jnp.exp(mb[br, rows, :] - m_all)
                num = num + ob[br, rows, :] * c
                den = den + lb[br, rows, :] * c
            o_ref[rows, :] = num / den
            lse_ref[rows, :] = m_all + jnp.log(den)
            return carry

        lax.fori_loop(0, S // 256, merge, 0)

    npair = NH // 2
    blk_spec = lambda off: pl.BlockSpec((S, BLK), lambda b, hp: (b, off + hp))
    return pl.pallas_call(
        body, name="attn_fwd", grid=(BL, npair),
        in_specs=[blk_spec(0), blk_spec(npair), blk_spec(2 * npair),
                  pl.BlockSpec((nbr, 2, BLK, 2 * BLK), lambda b, hp: (0, hp, 0, 0))],
        out_specs=(blk_spec(0), blk_spec(0)),
        out_shape=(SDS((T, AW), F32), SDS((T, AW), F32)),
        scratch_shapes=[pltpu.VMEM((S, BLK), F32)] * 3 + [pltpu.VMEM((S, BLK), BF16)] * 2
        + [pltpu.VMEM((PADK + S, BLK), BF16)] * 2 + [pltpu.VMEM((nbr, 2, 2 * BLK, 2 * BLK), F32)]
        + [pltpu.VMEM((nbr, S, BLK), F32)] * 3,
    )(qkv, qkv, qkv, bias)


def _attn_bwd(qkv, attn, lse, dattn, bias):
    nbr = len(BRANCHES)

    def body(q_ref, k_ref, v_ref, o_ref, lse_ref, do_ref, bias_ref,
             dq_ref, dk_ref, dv_ref, sq_ref, sk_ref, sv_ref, db_ref,
             qf, kf, vf, dl, dqa, dka, dva, qs0, qs1, ds0, ds1, ks, vs, dks, dvs, bm):
        b = pl.program_id(1)
        qf[...] = q_ref[...].astype(F32)
        kf[...] = k_ref[...].astype(F32)
        vf[...] = v_ref[...].astype(F32)
        dqa[...] = jnp.zeros_like(dqa)
        dka[...] = jnp.zeros_like(dka)
        dva[...] = jnp.zeros_like(dva)
        _masked_bias(bias_ref, bm)
        ks[pl.ds(0, PADK), :] = jnp.zeros((PADK, BLK), BF16)
        vs[pl.ds(0, PADK), :] = jnp.zeros((PADK, BLK), BF16)
        head0 = lax.broadcasted_iota(jnp.int32, (BLK, BLK), 1) < HD
        split_q = _head_split(lambda t: t * SCALE)
        split_do = _head_split(lambda t: t)

        @pl.when(b == 0)
        def _():
            db_ref[...] = jnp.zeros_like(db_ref)
            sq_ref[...] = jnp.zeros_like(sq_ref)
            sk_ref[...] = jnp.zeros_like(sk_ref)
            sv_ref[...] = jnp.zeros_like(sv_ref)

        def delta(i, carry):
            rows = pl.ds(pl.multiple_of(i * 256, 256), 256)
            prod = do_ref[rows, :] * o_ref[rows, :]
            h0 = lax.broadcasted_iota(jnp.int32, (256, BLK), 1) < HD
            d0 = jnp.sum(jnp.where(h0, prod, 0.0), axis=-1, keepdims=True)
            d1 = jnp.sum(jnp.where(h0, 0.0, prod), axis=-1, keepdims=True)
            dl[rows, :] = jnp.where(h0, d0, d1)
            return carry

        lax.fori_loop(0, S // 256, delta, 0)

        for br in range(nbr):
            nblk = _branch_geometry(br)[2]

            def stage(src, off, rows):
                qs0[pl.ds(off, rows), :], qs1[pl.ds(off, rows), :] = split_q(qf[src, :])
                ds0[pl.ds(off, rows), :], ds1[pl.ds(off, rows), :] = split_do(do_ref[src, :])
                ks[pl.ds(PADK + off, rows), :] = kf[src, :].astype(BF16)
                vs[pl.ds(PADK + off, rows), :] = vf[src, :].astype(BF16)

            _sub_layout_loop(br, stage)
            dks[...] = jnp.zeros_like(dks)
            dvs[...] = jnp.zeros_like(dvs)

            def blk(i, carry, br=br, nblk=nblk):
                base = pl.multiple_of(i * BLK, BLK)
                rows, n = _token_rows(br, i)
                q01 = jnp.concatenate([qs0[pl.ds(base, BLK), :], qs1[pl.ds(base, BLK), :]], axis=0)
                do01 = jnp.concatenate([ds0[pl.ds(base, BLK), :], ds1[pl.ds(base, BLK), :]], axis=0)
                lse_b = lse_ref[rows, :]
                dl_b = dl[rows, :]
                lse01 = jnp.concatenate([lse_b[:, 0:1], lse_b[:, HD:HD + 1]], axis=0)
                dl01 = jnp.concatenate([dl_b[:, 0:1], dl_b[:, HD:HD + 1]], axis=0)
                if nblk > 1:
                    krows = pl.ds(base, 2 * BLK)
                    bias_m = bm[br, jnp.minimum(n, 1)]
                else:
                    krows = pl.ds(PADK + base, BLK)
                    bias_m = bm[br, 0, :, BLK:]
                kcat = ks[krows, :]
                vcat = vs[krows, :]
                p = jnp.exp(_dot_nt(q01, kcat) + bias_m - lse01)
                dsv = p * (_dot_nt(do01, vcat) - dl01)
                if nblk > 1:
                    db_ref[br, 0] += dsv[:BLK]
                    db_ref[br, 1] += dsv[BLK:]
                else:
                    db_ref[br, 0, :, BLK:] += dsv[:BLK]
                    db_ref[br, 1, :, BLK:] += dsv[BLK:]
                dsb = dsv.astype(BF16)
                dq01 = _dot(dsb, kcat)
                dqa[rows, :] = dqa[rows, :] + jnp.where(head0, dq01[:BLK], dq01[BLK:])
                dks[krows, :] = dks[krows, :] + _dot_tn(dsb, q01)
                dvs[krows, :] = dvs[krows, :] + _dot_tn(p.astype(BF16), do01)
                return carry

            lax.fori_loop(0, 16, blk, 0, unroll=ATTN_UNROLL)

            def fold(src, off, rows):
                dka[src, :] = dka[src, :] + dks[pl.ds(PADK + off, rows), :]
                dva[src, :] = dva[src, :] + dvs[pl.ds(PADK + off, rows), :]

            _sub_layout_loop(br, fold)

        def flush(i, carry):
            rows = pl.ds(pl.multiple_of(i * 256, 256), 256)
            for acc, out, cs, mul in ((dqa, dq_ref, sq_ref, SCALE), (dka, dk_ref, sk_ref, 1.0), (dva, dv_ref, sv_ref, 1.0)):
                val = acc[rows, :] * mul
                out[rows, :] = val.astype(BF16)
                cs[...] += _colsum(val)
            return carry

        lax.fori_loop(0, S // 256, flush, 0)

    npair = NH // 2
    blk_spec = lambda off: pl.BlockSpec((S, BLK), lambda hp, b: (b, off + hp))
    sum_spec = pl.BlockSpec((1, BLK), lambda hp, b: (0, hp))
    return pl.pallas_call(
        body, name="attn_bwd", grid=(npair, BL),
        in_specs=[blk_spec(0), blk_spec(npair), blk_spec(2 * npair), blk_spec(0), blk_spec(0), blk_spec(0),
                  pl.BlockSpec((nbr, 2, BLK, 2 * BLK), lambda hp, b: (0, hp, 0, 0))],
        out_specs=(blk_spec(0), blk_spec(0), blk_spec(0), sum_spec, sum_spec, sum_spec,
                   pl.BlockSpec((nbr, 2, BLK, 2 * BLK), lambda hp, b: (0, hp, 0, 0))),
        out_shape=(SDS((T, AW), BF16), SDS((T, AW), BF16), SDS((T, AW), BF16),
                   SDS((1, AW), F32), SDS((1, AW), F32), SDS((1, AW), F32),
                   SDS((nbr, NH, BLK, 2 * BLK), F32)),
        scratch_shapes=[pltpu.VMEM((S, BLK), F32)] * 7 + [pltpu.VMEM((S, BLK), BF16)] * 4
        + [pltpu.VMEM((PADK + S, BLK), BF16)] * 2 + [pltpu.VMEM((PADK + S, BLK), F32)] * 2
        + [pltpu.VMEM((nbr, 2, 2 * BLK, 2 * BLK), F32)],
    )(qkv, qkv, qkv, attn, lse, dattn, bias)


CH = 256
PADR = 32


def _tap_phases(offset_of_tap):
    taps = sorted((offset_of_tap(k) % 8, offset_of_tap(k) - offset_of_tap(k) % 8, k) for k in range(CK))
    assert all(lo + CH + ph <= CH + PADR for ph, lo, _ in taps)
    return taps


def _rows_up(win):
    made = {0: win}

    def get(phase):
        if phase not in made:
            made[phase] = pltpu.roll(win, win.shape[0] - phase, 0)
        return made[phase]
    return get


def _conv_fwd(ag, conv_w, conv_b):
    def body(ag_ref, w_ref, b_ref, u1_ref, u0p):
        u0p[pl.ds(0, PADR), :] = jnp.zeros((PADR, CW), F32)

        def glu(i, carry):
            t0 = pl.multiple_of(i * CH, CH)
            a = ag_ref[pl.ds(t0, CH), :CW]
            g = ag_ref[pl.ds(t0, CH), CW:]
            u0p[pl.ds(PADR + t0, CH), :] = a * _sigmoid(g)
            return carry

        lax.fori_loop(0, S // CH, glu, 0)

        def conv(i, carry):
            t0 = pl.multiple_of(i * CH, CH)
            win = u0p[pl.ds(t0, CH + PADR), :]
            acc = jnp.zeros((CH, CW), F32) + b_ref[...]
            up = _rows_up(win)
            for phase, lo, k in _tap_phases(lambda k: PADR - (CK - 1) + k):
                acc = acc + up(phase)[lo:lo + CH, :] * w_ref[k:k + 1, :]
            u1_ref[pl.ds(t0, CH), :] = acc
            return carry

        lax.fori_loop(0, S // CH, conv, 0)

    return pl.pallas_call(
        body, name="conv_fwd", grid=(BL,),
        in_specs=[pl.BlockSpec((S, 2 * CW), lambda b: (b, 0)),
                  pl.BlockSpec((CK, CW), lambda b: (0, 0)),
                  pl.BlockSpec((1, CW), lambda b: (0, 0))],
        out_specs=pl.BlockSpec((S, CW), lambda b: (b, 0)),
        out_shape=SDS((T, CW), F32),
        scratch_shapes=[pltpu.VMEM((S + PADR, CW), F32)],
    )(ag, conv_w, conv_b)


def _conv_post(u1, cg, cb):
    mu = _rowmean(u1)
    uc = u1 - mu
    rstd = lax.rsqrt(_rowmean(uc * uc) + LN_EPS)
    xh = uc * rstd
    u2 = xh * cg + cb
    sg = _sigmoid(u2)
    return xh, rstd, u2, sg, u2 * sg


def _mix_fwd(attn, u1, ga, gc, cg, cb):
    def body(a_ref, u_ref, ga_ref, gc_ref, cg_ref, cb_ref, o_ref):
        a = a_ref[...]
        ra = lax.rsqrt(_rowmean(a * a) + LN_EPS)
        o_ref[:, :AW] = (a * ra * ga_ref[...]).astype(BF16)
        _, _, _, _, u3 = _conv_post(u_ref[...], cg_ref[...], cb_ref[...])
        rc = lax.rsqrt(_rowmean(u3 * u3) + LN_EPS)
        o_ref[:, AW:] = (u3 * rc * gc_ref[...]).astype(BF16)

    vec = lambda w: pl.BlockSpec((1, w), lambda m: (0, 0))
    return pl.pallas_call(
        body, name="mix_fwd", grid=(T // TM,),
        in_specs=[pl.BlockSpec((TM, AW), lambda m: (m, 0)), pl.BlockSpec((TM, CW), lambda m: (m, 0)),
                  vec(AW), vec(CW), vec(CW), vec(CW)],
        out_specs=pl.BlockSpec((TM, D), lambda m: (m, 0)),
        out_shape=SDS((T, D), BF16),
    )(attn, u1, ga, gc, cg, cb)


def _mix_bwd(dz1, w_out, attn, u1, ga, gc, cg, cb):
    def body(dz_ref, w_ref, a_ref, u_ref, ga_ref, gc_ref, cg_ref, cb_ref,
             da_ref, du_ref, g_an, g_cn, g_lg, g_lb, g_cb):
        @pl.when(pl.program_id(0) == 0)
        def _():
            for r in (g_an, g_cn, g_lg, g_lb, g_cb):
                r[...] = jnp.zeros_like(r)

        dm = _dot_nt(dz_ref[...].astype(BF16), w_ref[...])
        a = a_ref[...]
        dna = dm[:, :AW]
        ra = lax.rsqrt(_rowmean(a * a) + LN_EPS)
        g_an[...] += _colsum(dna * a * ra)
        dat = dna * ga_ref[...]
        da_ref[...] = ra * dat - a * (ra * ra * ra) * _rowmean(dat * a)

        xh, rstd, u2, sg, u3 = _conv_post(u_ref[...], cg_ref[...], cb_ref[...])
        dnc = dm[:, AW:]
        rc = lax.rsqrt(_rowmean(u3 * u3) + LN_EPS)
        g_cn[...] += _colsum(dnc * u3 * rc)
        dut = dnc * gc_ref[...]
        du3 = rc * dut - u3 * (rc * rc * rc) * _rowmean(dut * u3)
        du2 = du3 * sg * (1.0 + u2 * (1.0 - sg))
        g_lg[...] += _colsum(du2 * xh)
        g_lb[...] += _colsum(du2)
        dxh = du2 * cg_ref[...]
        du1 = rstd * (dxh - _rowmean(dxh) - xh * _rowmean(dxh * xh))
        g_cb[...] += _colsum(du1)
        du_ref[...] = du1

    vec = lambda w: pl.BlockSpec((1, w), lambda m: (0, 0))
    return pl.pallas_call(
        body, name="mix_bwd", grid=(T // TM,),
        in_specs=[pl.BlockSpec((TM, D), lambda m: (m, 0)), pl.BlockSpec((D, D), lambda m: (0, 0)),
                  pl.BlockSpec((TM, AW), lambda m: (m, 0)),
                  pl.BlockSpec((TM, CW), lambda m: (m, 0)), vec(AW), vec(CW), vec(CW), vec(CW)],
        out_specs=(pl.BlockSpec((TM, AW), lambda m: (m, 0)), pl.BlockSpec((TM, CW), lambda m: (m, 0)),
                   vec(AW), vec(CW), vec(CW), vec(CW), vec(CW)),
        out_shape=(SDS((T, AW), F32), SDS((T, CW), F32),
                   SDS((1, AW), F32), SDS((1, CW), F32), SDS((1, CW), F32), SDS((1, CW), F32), SDS((1, CW), F32)),
    )(dz1, w_out, attn, u1, ga, gc, cg, cb)


def _conv_bwd(du1, ag, conv_w):
    def body(du_ref, ag_ref, w_ref, dag_ref, cs_ref, gw_ref, u0p, dup):
        @pl.when(pl.program_id(0) == 0)
        def _():
            cs_ref[...] = jnp.zeros_like(cs_ref)
            gw_ref[...] = jnp.zeros_like(gw_ref)

        u0p[pl.ds(0, PADR), :] = jnp.zeros((PADR, CW), F32)
        dup[pl.ds(S, PADR), :] = jnp.zeros((PADR, CW), F32)

        def fill(i, carry):
            t0 = pl.multiple_of(i * CH, CH)
            a = ag_ref[pl.ds(t0, CH), :CW]
            g = ag_ref[pl.ds(t0, CH), CW:]
            u0p[pl.ds(PADR + t0, CH), :] = a * _sigmoid(g)
            dup[pl.ds(t0, CH), :] = du_ref[pl.ds(t0, CH), :]
            return carry

        lax.fori_loop(0, S // CH, fill, 0)

        def chunk(i, carry):
            t0 = pl.multiple_of(i * CH, CH)
            d = dup[pl.ds(t0, CH), :]
            win_u = u0p[pl.ds(t0, CH + PADR), :]
            win_d = dup[pl.ds(t0, CH + PADR), :]
            du0 = jnp.zeros((CH, CW), F32)
            up_u = _rows_up(win_u)
            for phase, lo, k in _tap_phases(lambda k: PADR - (CK - 1) + k):
                gw_ref[k:k + 1, :] += _colsum(d * up_u(phase)[lo:lo + CH, :])
            up_d = _rows_up(win_d)
            for phase, lo, k in _tap_phases(lambda k: CK - 1 - k):
                du0 = du0 + up_d(phase)[lo:lo + CH, :] * w_ref[k:k + 1, :]
            a = ag_ref[pl.ds(t0, CH), :CW]
            sg = _sigmoid(ag_ref[pl.ds(t0, CH), CW:])
            da = du0 * sg
            dg = du0 * a * sg * (1.0 - sg)
            dag_ref[pl.ds(t0, CH), :CW] = da.astype(BF16)
            dag_ref[pl.ds(t0, CH), CW:] = dg.astype(BF16)
            cs_ref[:, :CW] += _colsum(da)
            cs_ref[:, CW:] += _colsum(dg)
            return carry

        lax.fori_loop(0, S // CH, chunk, 0)

    return pl.pallas_call(
        body, name="conv_bwd", grid=(BL,),
        in_specs=[pl.BlockSpec((S, CW), lambda b: (b, 0)), pl.BlockSpec((S, 2 * CW), lambda b: (b, 0)),
                  pl.BlockSpec((CK, CW), lambda b: (0, 0))],
        out_specs=(pl.BlockSpec((S, 2 * CW), lambda b: (b, 0)),
                   pl.BlockSpec((1, 2 * CW), lambda b: (0, 0)),
                   pl.BlockSpec((PADR, CW), lambda b: (0, 0))),
        out_shape=(SDS((T, 2 * CW), BF16), SDS((1, 2 * CW), F32), SDS((PADR, CW), F32)),
        scratch_shapes=[pltpu.VMEM((S + PADR, CW), F32), pltpu.VMEM((S + PADR, CW), F32)],
    )(du1, ag, conv_w)


def _layer_norm_fwd(z):
    mu = _rowmean(z)
    zc = z - mu
    rstd = lax.rsqrt(_rowmean(zc * zc) + LN_EPS)
    return zc * rstd, rstd


def _layer_norm_bwd(dy, xh, rstd, g):
    dxh = dy * g
    return rstd * (dxh - _rowmean(dxh) - xh * _rowmean(dxh * xh))


def _out_proj_ln1(mixed, w_out, x2, g1, b1):
    def body(a_ref, w_ref, x_ref, g_ref, b_ref, xh_ref, rstd_ref, x1_ref):
        z = ALPHA * x_ref[...] + _dot(a_ref[...], w_ref[...])
        xh, rstd = _layer_norm_fwd(z)
        xh_ref[...] = xh
        rstd_ref[...] = rstd
        x1_ref[...] = (xh * g_ref[...] + b_ref[...]).astype(BF16)

    vec = pl.BlockSpec((1, D), lambda m: (0, 0))
    row = pl.BlockSpec((TM, D), lambda m: (m, 0))
    return pl.pallas_call(
        body, name="out_proj_ln1", grid=(T // TM,),
        in_specs=[row, pl.BlockSpec((D, D), lambda m: (0, 0)), row, vec, vec],
        out_specs=(row, pl.BlockSpec((TM, 1), lambda m: (m, 0)), row),
        out_shape=(SDS((T, D), F32), SDS((T, 1), F32), SDS((T, D), BF16)),
    )(mixed, w_out, x2, g1, b1)


def _seq_start(m):
    return lax.bitwise_and(m, S // TMF - 1) == 0


def _shift_down(x, before, k):
    rolled = pltpu.roll(x, k, 0)
    row = lax.broadcasted_iota(jnp.int32, before.shape, 0)
    head = jnp.where(row < k, pltpu.roll(before, k, 0), rolled[:8])
    return jnp.concatenate([head, rolled[8:]], axis=0)


def _shift_up(x, after, k):
    n = x.shape[0]
    rolled = pltpu.roll(x, n - k, 0)
    row = lax.broadcasted_iota(jnp.int32, after.shape, 0)
    tail = jnp.where(row >= 8 - k, pltpu.roll(after, 8 - k, 0), rolled[n - 8:])
    return jnp.concatenate([rolled[:n - 8], tail], axis=0)


def _ffn_up(x1b, w_up, fcw, fcb):
    def body(x_ref, wg_ref, wv_ref, cwg_ref, cwv_ref, cbg_ref, cbv_ref, up_ref, gv_ref, act_ref, prev_g, prev_v):
        @pl.when(_seq_start(pl.program_id(1)))
        def _():
            prev_g[...] = jnp.zeros_like(prev_g)
            prev_v[...] = jnp.zeros_like(prev_v)

        x = x_ref[...]
        outs = []
        for w_ref, cw_ref, cb_ref, prev, lo in ((wg_ref, cwg_ref, cbg_ref, prev_g, 0), (wv_ref, cwv_ref, cbv_ref, prev_v, FT)):
            u = _dot_nt(x, w_ref[...])
            up_ref[:, lo:lo + FT] = u.astype(BF16)
            before = prev[...]
            y = (cw_ref[2:3, :] * u + cw_ref[1:2, :] * _shift_down(u, before, 1)
                 + cw_ref[0:1, :] * _shift_down(u, before, 2) + cb_ref[...])
            prev[...] = u[TMF - 8:]
            gv_ref[:, lo:lo + FT] = y.astype(BF16)
            outs.append(y)
        gate, val = outs
        act_ref[...] = (gate * _sigmoid(gate) * val).astype(BF16)

    wspec = lambda off: pl.BlockSpec((FT, D), lambda n, m: (n + off, 0))
    cwspec = lambda off: pl.BlockSpec((FK, FT), lambda n, m: (0, n + off))
    cbspec = lambda off: pl.BlockSpec((1, FT), lambda n, m: (0, n + off))
    pair = pl.BlockSpec((TMF, 2 * FT), lambda n, m: (m, n))
    return pl.pallas_call(
        body, name="ffn_up", grid=(NFT, T // TMF),
        in_specs=[pl.BlockSpec((TMF, D), lambda n, m: (m, 0)), wspec(0), wspec(NFT),
                  cwspec(0), cwspec(NFT), cbspec(0), cbspec(NFT)],
        out_specs=(pair, pair, pl.BlockSpec((TMF, FT), lambda n, m: (m, n))),
        out_shape=(SDS((T, 2 * DFF), BF16), SDS((T, 2 * DFF), BF16), SDS((T, DFF), BF16)),
        scratch_shapes=[pltpu.VMEM((8, FT), F32)] * 2,
    )(x1b, w_up, w_up, fcw, fcw, fcb, fcb)


def _ffn_down_loss(act, w_down, xh1, g1, b1, g2, b2, target):
    def body(a_ref, w_ref, xh1_ref, g1_ref, b1_ref, g2_ref, b2_ref, t_ref, dz_ref, loss_ref, gg_ref, gb_ref):
        @pl.when(pl.program_id(0) == 0)
        def _():
            loss_ref[...] = jnp.zeros_like(loss_ref)
            gg_ref[...] = jnp.zeros_like(gg_ref)
            gb_ref[...] = jnp.zeros_like(gb_ref)

        for sub in range(TM // TMF):
            rows = pl.ds(sub * TMF, TMF)
            x1 = xh1_ref[rows, :] * g1_ref[...] + b1_ref[...]
            z = ALPHA * x1 + _dot(a_ref[rows, :], w_ref[...])
            xh, rstd = _layer_norm_fwd(z)
            diff = xh * g2_ref[...] + b2_ref[...] - t_ref[rows, :]
            loss_ref[...] += 0.5 * _colsum(_rowmean(diff * diff))
            dout = diff * (1.0 / D)
            gg_ref[...] += _colsum(dout * xh)
            gb_ref[...] += _colsum(dout)
            dz_ref[rows, :] = _layer_norm_bwd(dout, xh, rstd, g2_ref[...])

    vec = pl.BlockSpec((1, D), lambda m: (0, 0))
    row = pl.BlockSpec((TM, D), lambda m: (m, 0))
    return pl.pallas_call(
        body, name="ffn_down_loss", grid=(T // TM,),
        in_specs=[pl.BlockSpec((TM, DFF), lambda m: (m, 0)), pl.BlockSpec((DFF, D), lambda m: (0, 0)),
                  row, vec, vec, vec, vec, row],
        out_specs=(row, pl.BlockSpec((1, 1), lambda m: (0, 0)), vec, vec),
        out_shape=(SDS((T, D), F32), SDS((1, 1), F32), SDS((1, D), F32), SDS((1, D), F32)),
    )(act, w_down, xh1, g1, b1, g2, b2, target)


def _ffn_down_bwd(dz2, w_down, gv, up, fcw):
    tiles = T // TMF

    def body(dz_ref, wd_ref, gv_ref, up_ref, cwg_ref, cwv_ref,
             dpre_ref, csg_ref, csv_ref, gwg_ref, gwv_ref, next_g, next_v):
        step = pl.program_id(1)
        tile = tiles - 1 - step

        @pl.when(step == 0)
        def _():
            for r in (csg_ref, csv_ref, gwg_ref, gwv_ref, next_g, next_v):
                r[...] = jnp.zeros_like(r)

        seq_end = lax.bitwise_and(tile + 1, S // TMF - 1) == 0
        dact = _dot_nt(dz_ref[...].astype(BF16), wd_ref[...])
        gate = gv_ref[:, :FT].astype(F32)
        val = gv_ref[:, FT:].astype(F32)
        sg = _sigmoid(gate)
        gs = gate * sg
        halves = ((dact * val * (sg + gs * (1.0 - sg)), cwg_ref, csg_ref, gwg_ref, next_g, 0),
                  (dact * gs, cwv_ref, csv_ref, gwv_ref, next_v, FT))
        for d0, cw_ref, cs_ref, gw_ref, nxt, lo in halves:
            after = jnp.where(seq_end, 0.0, nxt[...])
            d1 = _shift_up(d0, after, 1)
            d2 = _shift_up(d0, after, 2)
            nxt[...] = d0[:8]
            dpre_ref[:, lo:lo + FT] = (cw_ref[2:3, :] * d0 + cw_ref[1:2, :] * d1 + cw_ref[0:1, :] * d2).astype(BF16)
            cs_ref[...] += _colsum(d0)
            u = up_ref[:, lo:lo + FT].astype(F32)
            for k, dk in enumerate((d2, d1, d0)):
                gw_ref[k:k + 1, :] += _colsum(dk * u)

    cs = pl.BlockSpec((1, FT), lambda n, m: (0, n))
    gw = pl.BlockSpec((FK, FT), lambda n, m: (0, n))
    cwspec = lambda off: pl.BlockSpec((FK, FT), lambda n, m: (0, n + off))
    pair = pl.BlockSpec((TMF, 2 * FT), lambda n, m: (tiles - 1 - m, n))
    return pl.pallas_call(
        body, name="ffn_down_bwd", grid=(NFT, tiles),
        in_specs=[pl.BlockSpec((TMF, D), lambda n, m: (tiles - 1 - m, 0)), pl.BlockSpec((FT, D), lambda n, m: (n, 0)),
                  pair, pair, cwspec(0), cwspec(NFT)],
        out_specs=(pair, cs, cs, gw, gw),
        out_shape=(SDS((T, 2 * DFF), BF16), SDS((1, DFF), F32), SDS((1, DFF), F32),
                   SDS((FK, DFF), F32), SDS((FK, DFF), F32)),
        scratch_shapes=[pltpu.VMEM((8, FT), F32)] * 2,
    )(dz2, w_down, gv, up, fcw, fcw)


def _ffn_up_bwd_ln1(dpre, w_up, dz2, xh1, rstd1, g1):
    def body(a_ref, w_ref, dz2_ref, xh_ref, rstd_ref, g_ref, dz1_ref, gg_ref, gb_ref):
        @pl.when(pl.program_id(0) == 0)
        def _():
            gg_ref[...] = jnp.zeros_like(gg_ref)
            gb_ref[...] = jnp.zeros_like(gb_ref)

        for sub in range(TM // TMF):
            rows = pl.ds(sub * TMF, TMF)
            dx1 = ALPHA * dz2_ref[rows, :]
            for n in range(NFT):
                for half in range(2):
                    a = a_ref[rows, (2 * n + half) * FT:(2 * n + half + 1) * FT]
                    w = w_ref[pl.ds((half * NFT + n) * FT, FT), :]
                    dx1 = dx1 + _dot(a, w)
            xh = xh_ref[rows, :]
            gg_ref[...] += _colsum(dx1 * xh)
            gb_ref[...] += _colsum(dx1)
            dz1_ref[rows, :] = _layer_norm_bwd(dx1, xh, rstd_ref[rows, :], g_ref[...])

    vec = pl.BlockSpec((1, D), lambda m: (0, 0))
    row = pl.BlockSpec((TM, D), lambda m: (m, 0))
    return pl.pallas_call(
        body, name="ffn_up_bwd_ln1", grid=(T // TM,),
        in_specs=[pl.BlockSpec((TM, 2 * DFF), lambda m: (m, 0)),
                  pl.BlockSpec((2 * DFF, D), lambda m: (0, 0), pipeline_mode=pl.Buffered(1)),
                  row, row, pl.BlockSpec((TM, 1), lambda m: (m, 0)), vec],
        out_specs=(row, vec, vec),
        out_shape=(SDS((T, D), F32), SDS((1, D), F32), SDS((1, D), F32)),
    )(dpre, w_up, dz2, xh1, rstd1, g1)


def _grad_w_up(dpre, x1b):
    tk = 1024

    def body(a_ref, b_ref, o_ref, acc):
        k = pl.program_id(1)

        @pl.when(k == 0)
        def _():
            acc[...] = jnp.zeros_like(acc)

        acc[...] += _dot_tn(a_ref[...], b_ref[...])

        @pl.when(k == T // tk - 1)
        def _():
            o_ref[0] = acc[pl.ds(0, FT), :].astype(o_ref.dtype)
            o_ref[1] = acc[pl.ds(FT, FT), :].astype(o_ref.dtype)

    out = pl.pallas_call(
        body, name="grad_w_up", grid=(NFT, T // tk),
        in_specs=[pl.BlockSpec((tk, 2 * FT), lambda n, k: (k, n)), pl.BlockSpec((tk, D), lambda n, k: (k, 0))],
        out_specs=pl.BlockSpec((2, FT, D), lambda n, k: (0, n, 0)),
        out_shape=SDS((2, DFF, D), GRAD_WIRE),
        scratch_shapes=[pltpu.VMEM((2 * FT, D), F32)],
    )(dpre, x1b)
    return out.reshape(2 * DFF, D)


def _row_tile(rows, cols):
    if rows * cols * 4 <= (1 << 20) or rows % 8:
        return rows
    for t in (256, 176, 128, 88, 64, 32, 16, 8):
        if rows % t == 0 and t * cols * 4 <= (1 << 20):
            return t
    return 8


def _sum8(r, name):
    _, rows, cols = r.shape
    tr = _row_tile(rows, cols)

    def body(r_ref, o_ref):
        acc = r_ref[0].astype(F32)
        for p in range(1, NDEV):
            acc = acc + r_ref[p].astype(F32)
        o_ref[...] = acc

    return pl.pallas_call(
        body, name=name, grid=(rows // tr,),
        in_specs=[pl.BlockSpec((NDEV, tr, cols), lambda i: (0, i, 0))],
        out_specs=pl.BlockSpec((tr, cols), lambda i: (i, 0)),
        out_shape=SDS((rows, cols), F32),
    )(r)


def _sum8_adamw(r, w, m, v, name):
    rows, cols = w.shape
    tr = _row_tile(rows, cols)

    def body(r_ref, w_ref, m_ref, v_ref, g_out, d_ref, nm_ref, nv_ref):
        g_ = r_ref[0].astype(F32)
        for p in range(1, NDEV):
            g_ = g_ + r_ref[p].astype(F32)
        m_ = B1 * m_ref[...] + (1.0 - B1) * g_
        v_ = B2 * v_ref[...] + (1.0 - B2) * jnp.square(g_)
        m_hat = m_ / (1.0 - B1 ** STEP)
        v_hat = v_ / (1.0 - B2 ** STEP)
        g_out[...] = g_
        d_ref[...] = -LR * (m_hat / (jnp.sqrt(v_hat) + AEPS) + WD * w_ref[...])
        nm_ref[...] = m_
        nv_ref[...] = v_

    spec = pl.BlockSpec((tr, cols), lambda i: (i, 0))
    shp = SDS((rows, cols), F32)
    return pl.pallas_call(
        body, name=name, grid=(rows // tr,),
        in_specs=[pl.BlockSpec((NDEV, tr, cols), lambda i: (0, i, 0))] + [spec] * 3, out_specs=(spec,) * 4,
        out_shape=(shp,) * 4,
    )(r, w, m, v)


def _adamw_many(ws, gs, ms, vs, name):
    n = len(ws)

    def body(*refs):
        for i in range(n):
            w_ref, g_ref, m_ref, v_ref, d_ref, nm_ref, nv_ref = refs[i::n]
            g_ = g_ref[...]
            m_ = B1 * m_ref[...] + (1.0 - B1) * g_
            v_ = B2 * v_ref[...] + (1.0 - B2) * jnp.square(g_)
            m_hat = m_ / (1.0 - B1 ** STEP)
            v_hat = v_ / (1.0 - B2 ** STEP)
            d_ref[...] = -LR * (m_hat / (jnp.sqrt(v_hat) + AEPS) + WD * w_ref[...])
            nm_ref[...] = m_
            nv_ref[...] = v_

    shapes = tuple(SDS(w.shape, F32) for w in ws)
    res = pl.pallas_call(body, name=name, out_shape=shapes * 3)(*ws, *gs, *ms, *vs)
    return res[:n], res[n:2 * n], res[2 * n:]


def _local_step(x2, target, rel_table, first_weights, b_in, conv_b, conv_ln_g, conv_ln_b, attn_norm_g,
                conv_norm_g, late_weights, ln1_g, ln1_b, ffn_conv_b, ln2_g, ln2_b, ship_ffn_grads, ship_tail):
    buckets = jnp.asarray(_bucket_maps())
    bias = _bias_table(rel_table, buckets)
    xb = _cast_x(x2, bias)
    w_in_t, conv_w, ffn_conv_w = first_weights(xb)

    qkv, ag = _proj_in(xb, w_in_t, b_in)
    attn, lse = _attn_fwd(qkv, bias)
    u1 = _conv_fwd(ag, conv_w, conv_b)
    mixed = _mix_fwd(attn, u1, attn_norm_g, conv_norm_g, conv_ln_g, conv_ln_b)
    w_out = late_weights(0, mixed)
    xh1, rstd1, x1b = _out_proj_ln1(mixed, w_out, x2, ln1_g, ln1_b)
    w_up = late_weights(1, x1b)
    up, gv, act = _ffn_up(x1b, w_up, ffn_conv_w, ffn_conv_b)
    w_down = late_weights(2, act)
    dz2, loss, g_ln2_g, g_ln2_b = _ffn_down_loss(act, w_down, xh1, ln1_g, ln1_b, ln2_g, ln2_b, target)

    dpre, cs_g, cs_v, gfw_g, gfw_v = _ffn_down_bwd(dz2, w_down, gv, up, ffn_conv_w)
    g_w_down = _mm_tn(act, dz2, DFF // 2, 512, "grad_w_down")
    dz1, g_ln1_g, g_ln1_b = _ffn_up_bwd_ln1(dpre, w_up, dz2, xh1, rstd1, ln1_g)
    g_w_out = _mm_tn(mixed, dz1, D, 512, "grad_w_out")
    zero = ship_ffn_grads(g_w_down, _grad_w_up(dpre, x1b), g_w_out)
    dattn, du1, g_an, g_cn, g_clg, g_clb, g_cb = _mix_bwd(
        dz1, w_out, attn, u1, attn_norm_g + zero, conv_norm_g, conv_ln_g, conv_ln_b)
    dag, cs_ag, g_conv_w = _conv_bwd(du1, ag, conv_w)
    dq, dk, dv, cs_q, cs_k, cs_v2, dbias = _attn_bwd(qkv, attn, lse, dattn, bias)
    g_rel = _rel_table_grad(dbias, buckets)
    pieces = [dq, dk, dv, dag]
    g_w_in_t = _grad_w_in(pieces, xb)

    grads = dict(
        rel_table=g_rel,
        b_in=jnp.concatenate([cs_q, cs_k, cs_v2, cs_ag], axis=1),
        conv_b=g_cb, conv_ln_g=g_clg, conv_ln_b=g_clb, attn_norm_g=g_an, conv_norm_g=g_cn,
        ln1_g=g_ln1_g, ln1_b=g_ln1_b,
        ffn_conv_b=jnp.concatenate([cs_g, cs_v], axis=1),
        ln2_g=g_ln2_g, ln2_b=g_ln2_b,
        conv_w=g_conv_w[:CK],
        ffn_conv_w=jnp.concatenate([gfw_g, gfw_v], axis=1),
    )
    grads["loss"] = loss
    zero11 = ship_tail(g_w_in_t, grads)
    grad_x = _grad_x(pieces, w_in_t, dz1, zero11)
    return loss, grad_x


SMALL = (("rel_table", (NBUCKET, NH)), ("b_in", (1, INW)), ("conv_b", (1, CW)), ("conv_ln_g", (1, CW)),
         ("conv_ln_b", (1, CW)), ("attn_norm_g", (1, AW)), ("conv_norm_g", (1, CW)), ("ln1_g", (1, D)),
         ("ln1_b", (1, D)), ("ffn_conv_b", (1, 2 * DFF)), ("ln2_g", (1, D)), ("ln2_b", (1, D)))
SHARDED_SMALL = (("conv_w", (CK, CW)), ("ffn_conv_w", (FK, 2 * DFF)))


def _pack(parts):
    flat = jnp.concatenate([p.reshape(-1) for p in parts])
    tile = 8 * PACK_LANES
    pad = (-flat.shape[0]) % tile
    return jnp.pad(flat, (0, pad)).reshape(-1, PACK_LANES)


def _unpack(packed, specs):
    flat = packed.reshape(-1)
    out, off = {}, 0
    for name, shp in specs:
        size = int(np.prod(shp))
        out[name] = flat[off:off + size].reshape(shp)
        off += size
    return out


def kernel(x, rel_table, w_in, b_in, conv_w, conv_b, conv_ln_g, conv_ln_b, attn_norm_g, conv_norm_g, w_out, ln1_g, ln1_b, w_up, ffn_conv_w, ffn_conv_b, w_down, ln2_g, ln2_b, loss_target, m_rel_table, m_w_in, m_b_in, m_conv_w, m_conv_b, m_conv_ln_g, m_conv_ln_b, m_attn_norm_g, m_conv_norm_g, m_w_out, m_ln1_g, m_ln1_b, m_w_up, m_ffn_conv_w, m_ffn_conv_b, m_w_down, m_ln2_g, m_ln2_b, v_rel_table, v_w_in, v_b_in, v_conv_w, v_conv_b, v_conv_ln_g, v_conv_ln_b, v_attn_norm_g, v_conv_norm_g, v_w_out, v_ln1_g, v_ln1_b, v_w_up, v_ffn_conv_w, v_ffn_conv_b, v_w_down, v_ln2_g, v_ln2_b):
    given = dict(locals())
    me = 4 * lax.axis_index("x") + 2 * lax.axis_index("y") + lax.axis_index("c")

    cols = lambda a: a.transpose(1, 0, 2).reshape(a.shape[1], NDEV * a.shape[2])
    rows = lambda a: a.reshape(NDEV * a.shape[1], a.shape[2])
    stack = lambda a: a.reshape(NDEV, a.shape[0] // NDEV, a.shape[1])

    small_specs = SMALL + SHARDED_SMALL
    packed_specs = small_specs + (("loss", (1, 1)),)
    grad, delta, new_m, new_v = {}, {}, {}, {}

    def adamw_big(n, partials, transposed=False):
        shp = given[n].shape
        to2d = (lambda a: a.reshape(shp[-2], shp[-1]).T) if transposed else (lambda a: a.reshape(shp[-2], shp[-1]))
        back = (lambda a: a.T.reshape(shp)) if transposed else (lambda a: a.reshape(shp))
        g_, d_, m_, v_ = _sum8_adamw(partials, to2d(given[n]), to2d(given["m_" + n]), to2d(given["v_" + n]), "adamw_" + n)
        grad[n], delta[n], new_m[n], new_v[n] = back(g_), back(d_), back(m_), back(v_)
        return d_

    first_state, zero0 = _exchange_start(
        [(w_in[0].T.astype(BF16), "gather"), (conv_w[0], "gather"), (ffn_conv_w[0], "gather")], "gather_first_start")

    def first_weights(after):
        lands = _exchange_wait(first_state, after, "gather_first_wait")
        return rows(lands[0]), cols(lands[1]), cols(lands[2])

    late_state, zero1 = _exchange_start(
        [(w_out[0].astype(BF16), "gather"), (w_up[0].T.astype(BF16) + zero0.astype(BF16), "gather"),
         (w_down[0].astype(BF16), "gather")], "gather_late_start")

    def late_weights(i, after):
        return rows(_exchange_wait(late_state, after, "gather_late_wait_%d" % i, only=(i,))[i])

    shipped = {}

    def ship_ffn_grads(g_w_down, g_w_up_t, g_w_out):
        shipped["ffn"], zero2 = _exchange_start(
            [(stack(a), "scatter") for a in (g_w_down, g_w_up_t, g_w_out)], "ffn_grads_start")
        return zero2

    def ship_tail(g_w_in_t, small_grads):
        shipped["tail"], zero3 = _exchange_start(
            [(stack(g_w_in_t), "scatter"), (_pack([small_grads[n] for n, _ in packed_specs]), "gather")],
            "tail_grads_start")
        return zero3.reshape(1, 1)

    loss, grad_x = _local_step(
        x.reshape(T, D), loss_target.reshape(T, D), rel_table + zero1, first_weights, b_in, conv_b, conv_ln_g,
        conv_ln_b, attn_norm_g, conv_norm_g, late_weights, ln1_g, ln1_b, ffn_conv_b,
        ln2_g, ln2_b, ship_ffn_grads, ship_tail)

    got_down, got_up, got_out = _exchange_wait(shipped["ffn"], grad_x, "ffn_grads_wait")
    adamw_big("w_down", got_down)
    adamw_big("w_up", got_up, transposed=True)
    last = adamw_big("w_out", got_out)

    got_in, got_small = _exchange_wait(shipped["tail"], last, "tail_grads_wait")
    adamw_big("w_in", got_in, transposed=True)
    small = _unpack(_sum8(got_small, "sum_small"), packed_specs)
    small["conv_w"] = lax.dynamic_slice_in_dim(small["conv_w"], me * (CW // NDEV), CW // NDEV, axis=1)
    small["ffn_conv_w"] = lax.dynamic_slice_in_dim(small["ffn_conv_w"], me * (2 * DFF // NDEV), 2 * DFF // NDEV, axis=1)
    names = [n for n, _ in small_specs]
    two = lambda a: a.reshape(a.shape[-2], a.shape[-1])
    ds, nms, nvs = _adamw_many([two(given[n]) for n in names], [small[n] for n in names],
                               [two(given["m_" + n]) for n in names], [two(given["v_" + n]) for n in names], "adamw_small")
    for n, d_, m_, v_ in zip(names, ds, nms, nvs):
        shp = given[n].shape
        grad[n], delta[n], new_m[n], new_v[n] = small[n].reshape(shp), d_.reshape(shp), m_.reshape(shp), v_.reshape(shp)

    order = ("rel_table", "w_in", "b_in", "conv_w", "conv_b", "conv_ln_g", "conv_ln_b", "attn_norm_g",
             "conv_norm_g", "w_out", "ln1_g", "ln1_b", "w_up", "ffn_conv_w", "ffn_conv_b", "w_down", "ln2_g", "ln2_b")
    return (small["loss"][0, 0], grad_x.reshape(BL, S, D), *[grad[n] for n in order], *[delta[n] for n in order],
            *[new_m[n] for n in order], *[new_v[n] for n in order])
```

```python
import math

import numpy as np
import jax
import jax.numpy as jnp
from jax import lax
from jax.experimental import pallas as pl
from jax.experimental.pallas import tpu as pltpu

F32 = jnp.float32
BF16 = jnp.bfloat16
SDS = jax.ShapeDtypeStruct

NDEV = 8
D = 1024
S = 2048
BL = 2
T = BL * S
NH = 12
HD = 64
AW = NH * HD
CW = D - AW
INW = 3 * AW + 2 * CW
CK = 31
DFF = 2816
FK = 3
BLK = 128
NBUCKET = 32
BRANCHES = ((128, 1), (512, 4), (2048, 16))
ALPHA = 2.0 ** 0.25
LN_EPS = 1e-5
NEG_INF = -1e30
LR, B1, B2, AEPS, WD, STEP = 0.001, 0.9, 0.999, 1e-08, 0.01, 10

TM = 512
FT = 1408
NFT = DFF // FT
TMF = 256
PACK_LANES = 128
GRAD_WIRE = BF16

assert all(w // d == BLK for w, d in BRANCHES)


def _dot(a, b):
    return jnp.dot(a, b, preferred_element_type=F32)


def _dot_nt(a, b):
    return lax.dot_general(a, b, (((1,), (1,)), ((), ())), preferred_element_type=F32)


def _dot_tn(a, b):
    return lax.dot_general(a, b, (((0,), (0,)), ((), ())), preferred_element_type=F32)


def _rowmean(v):
    return jnp.mean(v, axis=-1, keepdims=True)


def _colsum(v):
    return jnp.sum(v, axis=0, keepdims=True)


def _sigmoid(v):
    return jax.nn.sigmoid(v)


_HBM = pl.BlockSpec(memory_space=pltpu.HBM)
_SEM = pl.BlockSpec(memory_space=pltpu.SEMAPHORE)
_EFFECT = pltpu.SideEffectType.DATAFLOW_SIDE_EFFECTING


def _peer_of(k):
    x, y, c = lax.axis_index("x"), lax.axis_index("y"), lax.axis_index("c")
    px = 1 - x if k & 4 else x
    py = 1 - y if k & 2 else y
    pc = 1 - c if k & 1 else c
    return (px, py, pc), 4 * px + 2 * py + pc


def _split_copies(kinds, ins, lands, send_sems, recv_sems, started):
    me = 4 * lax.axis_index("x") + 2 * lax.axis_index("y") + lax.axis_index("c")
    out = []
    for i, kind in enumerate(kinds):
        for k in range(1, NDEV):
            dev, pid = _peer_of(k)
            src = ins[i] if kind == "gather" else ins[i].at[pid]
            dst = lands[i].at[me] if started else lands[i].at[pid]
            slot = i * (NDEV - 1) + k - 1
            out.append(pltpu.make_async_remote_copy(
                src_ref=src, dst_ref=dst, send_sem=send_sems.at[slot], recv_sem=recv_sems.at[slot],
                device_id=dev, device_id_type=pl.DeviceIdType.MESH))
    return out


def _exchange_start(items, name):
    n = len(items)
    kinds = [k for _, k in items]
    srcs = [pltpu.with_memory_space_constraint(a, pltpu.HBM) for a, _ in items]
    lands = []
    for a, k in items:
        shp = (NDEV,) + tuple(a.shape) if k == "gather" else tuple(a.shape)
        lands.append(pltpu.with_memory_space_constraint(lax.empty(shp, a.dtype), pltpu.HBM))

    def body(*refs):
        ins, land_refs = refs[:n], refs[n:2 * n]
        send_sems, recv_sems, own_sems = refs[2 * n:2 * n + 3]
        token = refs[-1]
        for cp in _own_copies(kinds, ins, land_refs, own_sems):
            cp.start()
        for cp in _split_copies(kinds, ins, land_refs, send_sems, recv_sems, True):
            cp.start()
        token[...] = jnp.zeros_like(token)

    sems = pltpu.SemaphoreType.DMA((n * (NDEV - 1),))
    res = pl.pallas_call(
        body, name=name,
        out_shape=(sems, sems, pltpu.SemaphoreType.DMA((n,)),
                   *[pltpu.HBM(a.shape, a.dtype) for a in srcs + lands], SDS((8, 128), F32)),
        in_specs=[_HBM] * (2 * n),
        out_specs=(_SEM, _SEM, _SEM, *[_HBM] * (2 * n), pl.BlockSpec(memory_space=pltpu.VMEM)),
        input_output_aliases={i: 3 + i for i in range(2 * n)},
        compiler_params=pltpu.CompilerParams(has_side_effects=_EFFECT),
    )(*srcs, *lands)
    return (kinds, res[0], res[1], res[2], list(res[3:3 + n]), list(res[3 + n:3 + 2 * n])), res[-1][0, 0]


def _own_copies(kinds, ins, lands, own_sems):
    me = 4 * lax.axis_index("x") + 2 * lax.axis_index("y") + lax.axis_index("c")
    return [pltpu.make_async_copy(ins[i] if kind == "gather" else ins[i].at[me], lands[i].at[me], own_sems.at[i])
            for i, kind in enumerate(kinds)]


def _exchange_wait(state, after, name, only=None):
    kinds, send_sems, recv_sems, own_sems, srcs, lands = state
    n = len(kinds)
    chosen = range(n) if only is None else only

    def body(*refs):
        ins, land_refs = refs[:n], refs[n:2 * n]
        s_sems, r_sems, o_sems = refs[2 * n:2 * n + 3]
        remote = _split_copies(kinds, ins, land_refs, s_sems, r_sems, False)
        own = _own_copies(kinds, ins, land_refs, o_sems)
        for i in chosen:
            for cp in remote[i * (NDEV - 1):(i + 1) * (NDEV - 1)]:
                cp.wait_send()
                cp.wait_recv()
        for i in chosen:
            own[i].wait()

    res = pl.pallas_call(
        body, name=name,
        out_shape=tuple(pltpu.HBM(a.shape, a.dtype) for a in srcs + lands),
        in_specs=[_HBM] * (2 * n) + [_SEM, _SEM, _SEM, pl.BlockSpec(memory_space=pl.ANY)],
        out_specs=tuple([_HBM] * (2 * n)),
        input_output_aliases={i: i for i in range(2 * n)},
        compiler_params=pltpu.CompilerParams(has_side_effects=_EFFECT),
    )(*srcs, *lands, send_sems, recv_sems, own_sems, after)
    srcs[:], lands[:] = res[:n], res[n:]
    return list(lands)


def _cast_x(x2, after):
    def body(x_ref, after_ref, o_ref, done_ref):
        o_ref[...] = x_ref[...].astype(BF16)
        done_ref[...] = jnp.zeros_like(done_ref)

    return pl.pallas_call(
        body, name="cast_x", grid=(T // TM,),
        in_specs=[pl.BlockSpec((TM, D), lambda m: (m, 0)), pl.BlockSpec(memory_space=pl.ANY)],
        out_specs=(pl.BlockSpec((TM, D), lambda m: (m, 0)), pl.BlockSpec((8, 128), lambda m: (0, 0))),
        out_shape=(SDS((T, D), BF16), SDS((8, 128), F32)),
    )(x2, after)


def _proj_in(xb, w_in_t, b_in):
    nq = 3 * AW

    def body(x_ref, w_ref, b_ref, qkv_ref, ag_ref):
        xb = x_ref[...]
        qkv_ref[...] = (_dot_nt(xb, w_ref[pl.ds(0, nq), :]) + b_ref[:, :nq]).astype(BF16)
        ag_ref[...] = _dot_nt(xb, w_ref[pl.ds(nq, 2 * CW), :]) + b_ref[:, nq:]

    return pl.pallas_call(
        body, name="proj_in", grid=(T // TM,),
        in_specs=[pl.BlockSpec((TM, D), lambda m: (m, 0)), pl.BlockSpec((INW, D), lambda m: (0, 0)),
                  pl.BlockSpec((1, INW), lambda m: (0, 0))],
        out_specs=(pl.BlockSpec((TM, nq), lambda m: (m, 0)), pl.BlockSpec((TM, 2 * CW), lambda m: (m, 0))),
        out_shape=(SDS((T, nq), BF16), SDS((T, 2 * CW), F32)),
    )(xb, w_in_t, b_in)


def _grad_x(pieces, w_in_t, dz1, zero):
    widths = [p.shape[1] for p in pieces]

    def body(*refs):
        p_refs = refs[:len(pieces)]
        w_ref, dz_ref, z_ref, o_ref = refs[len(pieces):]
        acc = ALPHA * dz_ref[...] + z_ref[...]
        r0 = 0
        for p_ref, wd in zip(p_refs, widths):
            acc = acc + _dot(p_ref[...], w_ref[pl.ds(r0, wd), :])
            r0 += wd
        o_ref[...] = acc

    row = pl.BlockSpec((TM, D), lambda m: (m, 0))
    return pl.pallas_call(
        body, name="grad_x", grid=(T // TM,),
        in_specs=[pl.BlockSpec((TM, wd), lambda m: (m, 0)) for wd in widths]
        + [pl.BlockSpec((INW, D), lambda m: (0, 0)), row, pl.BlockSpec((1, 1), lambda m: (0, 0))],
        out_specs=row,
        out_shape=SDS((T, D), F32),
    )(*pieces, w_in_t, dz1, zero)


def _grad_w_in(pieces, x2):
    widths = [p.shape[1] for p in pieces]
    tk = 512
    nk = T // tk

    def body(*refs):
        p_refs = refs[:len(pieces)]
        x_ref, o_ref, acc = refs[len(pieces):]
        k = pl.program_id(0)

        @pl.when(k == 0)
        def _():
            acc[...] = jnp.zeros_like(acc)

        xb = x_ref[...]
        r0 = 0
        for p_ref, wd in zip(p_refs, widths):
            acc[pl.ds(r0, wd), :] += _dot_tn(p_ref[...], xb)
            r0 += wd

        @pl.when(k == nk - 1)
        def _():
            o_ref[...] = acc[...].astype(o_ref.dtype)

    return pl.pallas_call(
        body, name="grad_w_in", grid=(nk,),
        in_specs=[pl.BlockSpec((tk, wd), lambda k: (k, 0)) for wd in widths] + [pl.BlockSpec((tk, D), lambda k: (k, 0))],
        out_specs=pl.BlockSpec((INW, D), lambda k: (0, 0)),
        out_shape=SDS((INW, D), GRAD_WIRE),
        scratch_shapes=[pltpu.VMEM((INW, D), F32)],
    )(*pieces, x2)


def _mm_tn(a, b, tn, tk, name):
    t_, na = a.shape
    nb = b.shape[1]
    nk = t_ // tk

    def body(a_ref, b_ref, o_ref, acc):
        k = pl.program_id(1)

        @pl.when(k == 0)
        def _():
            acc[...] = jnp.zeros_like(acc)

        acc[...] += _dot_tn(a_ref[...].astype(BF16), b_ref[...].astype(BF16))

        @pl.when(k == nk - 1)
        def _():
            o_ref[...] = acc[...].astype(o_ref.dtype)

    return pl.pallas_call(
        body, name=name, grid=(na // tn, nk),
        in_specs=[pl.BlockSpec((tk, tn), lambda n, k: (k, n)),
                  pl.BlockSpec((tk, nb), lambda n, k: (k, 0))],
        out_specs=pl.BlockSpec((tn, nb), lambda n, k: (n, 0)),
        out_shape=SDS((na, nb), GRAD_WIRE),
        scratch_shapes=[pltpu.VMEM((tn, nb), F32)],
    )(a, b)


def _bucket_maps():
    qi = np.arange(BLK)[:, None]
    kj = np.arange(2 * BLK)[None, :]
    steps = np.maximum(qi + BLK - kj, 0)
    exact = NBUCKET // 2
    maps = []
    for _, dil in BRANCHES:
        dist = steps * dil
        d_f = np.maximum(dist, 1).astype(np.float32)
        large = exact + (np.log(d_f / np.float32(exact)) / np.float32(math.log(S / exact))
                         * np.float32(NBUCKET - exact)).astype(np.int32)
        large = np.minimum(large, NBUCKET - 1)
        maps.append(np.where(dist < exact, dist, large).astype(np.int32))
    return np.stack(maps)


def _bias_table(rel_table, buckets):
    def body(t_ref, b_ref, o_ref):
        bk = b_ref[0]
        for h in range(NH):
            acc = jnp.zeros((BLK, 2 * BLK), F32)
            for k in range(NBUCKET):
                acc = jnp.where(bk == k, t_ref[k, h], acc)
            o_ref[0, h] = acc

    return pl.pallas_call(
        body, name="bias_table", grid=(len(BRANCHES),),
        in_specs=[pl.BlockSpec(memory_space=pltpu.SMEM),
                  pl.BlockSpec((1, BLK, 2 * BLK), lambda i: (i, 0, 0))],
        out_specs=pl.BlockSpec((1, NH, BLK, 2 * BLK), lambda i: (i, 0, 0, 0)),
        out_shape=SDS((len(BRANCHES), NH, BLK, 2 * BLK), F32),
    )(rel_table, buckets)


def _rel_table_grad(dbias, buckets):
    def body(d_ref, b_ref, o_ref):
        h = pl.program_id(0)
        for k in range(NBUCKET):
            tot = jnp.zeros((1, 1), F32)
            for br in range(len(BRANCHES)):
                sel = jnp.where(b_ref[br] == k, d_ref[br, 0], 0.0)
                tot = tot + jnp.sum(jnp.sum(sel, axis=1, keepdims=True), axis=0, keepdims=True)
            o_ref[0, :, pl.ds(k, 1)] = tot

    out = pl.pallas_call(
        body, name="rel_table_grad", grid=(NH,),
        in_specs=[pl.BlockSpec((len(BRANCHES), 1, BLK, 2 * BLK), lambda h: (0, h, 0, 0)),
                  pl.BlockSpec((len(BRANCHES), BLK, 2 * BLK), lambda h: (0, 0, 0))],
        out_specs=pl.BlockSpec((1, 1, NBUCKET), lambda h: (h, 0, 0)),
        out_shape=SDS((NH, 1, NBUCKET), F32),
    )(dbias, buckets)
    return out.reshape(NH, NBUCKET).T


PADK = BLK
SCALE = 1.0 / math.sqrt(HD)
ATTN_UNROLL = 16


def _branch_geometry(br):
    dil = BRANCHES[br][1]
    sub = S // dil
    return dil, sub, sub // BLK


def _token_rows(br, i):
    dil, _, nblk = _branch_geometry(br)
    if dil == 1:
        return pl.ds(pl.multiple_of(i * BLK, BLK), BLK), i
    r = lax.shift_right_logical(i, nblk.bit_length() - 1)
    n = lax.bitwise_and(i, nblk - 1)
    return pl.ds(r + dil * BLK * n, BLK, stride=dil), n


def _sub_layout_loop(br, step):
    dil, sub, _ = _branch_geometry(br)
    rows = min(sub, 256)
    nchunk = sub // rows

    def it_step(it, carry):
        if dil == 1:
            src = pl.ds(pl.multiple_of(it * rows, rows), rows)
        else:
            r = lax.shift_right_logical(it, nchunk.bit_length() - 1)
            src = pl.ds(r + dil * rows * lax.bitwise_and(it, nchunk - 1), rows, stride=dil)
        step(src, pl.multiple_of(it * rows, BLK), rows)
        return carry

    lax.fori_loop(0, dil * nchunk, it_step, 0)


def _masked_bias(bias_ref, bm):
    qi = lax.broadcasted_iota(jnp.int32, (BLK, 2 * BLK), 0)
    kj = lax.broadcasted_iota(jnp.int32, (BLK, 2 * BLK), 1)
    first = jnp.logical_and(kj >= BLK, kj - BLK <= qi)
    valid = jnp.logical_or(first, jnp.logical_and(kj < BLK, kj >= qi))
    for br in range(len(BRANCHES)):
        for j in range(2):
            b = bias_ref[br, j]
            bm[br, 1, pl.ds(j * BLK, BLK), :] = jnp.where(valid, b, NEG_INF)
            bm[br, 0, pl.ds(j * BLK, BLK), :] = jnp.where(first, b, NEG_INF)


def _head_split(fn):
    def split(t):
        h0 = lax.broadcasted_iota(jnp.int32, t.shape, 1) < HD
        t = fn(t)
        return jnp.where(h0, t, 0.0).astype(BF16), jnp.where(h0, 0.0, t).astype(BF16)
    return split


def _attn_fwd(qkv, bias):
    nbr = len(BRANCHES)

    def body(q_ref, k_ref, v_ref, bias_ref, o_ref, lse_ref, qf, kf, vf, qs0, qs1, ks, vs, bm, ob, mb, lb):
        qf[...] = q_ref[...].astype(F32)
        kf[...] = k_ref[...].astype(F32)
        vf[...] = v_ref[...].astype(F32)
        _masked_bias(bias_ref, bm)
        ks[pl.ds(0, PADK), :] = jnp.zeros((PADK, BLK), BF16)
        vs[pl.ds(0, PADK), :] = jnp.zeros((PADK, BLK), BF16)
        head0 = lax.broadcasted_iota(jnp.int32, (BLK, BLK), 1) < HD
        split_q = _head_split(lambda t: t * SCALE)

        for br in range(nbr):
            nblk = _branch_geometry(br)[2]

            def stage(src, off, rows):
                qs0[pl.ds(off, rows), :], qs1[pl.ds(off, rows), :] = split_q(qf[src, :])
                ks[pl.ds(PADK + off, rows), :] = kf[src, :].astype(BF16)
                vs[pl.ds(PADK + off, rows), :] = vf[src, :].astype(BF16)

            _sub_layout_loop(br, stage)

            def blk(i, carry, br=br, nblk=nblk):
                base = pl.multiple_of(i * BLK, BLK)
                rows, n = _token_rows(br, i)
                q01 = jnp.concatenate([qs0[pl.ds(base, BLK), :], qs1[pl.ds(base, BLK), :]], axis=0)
                if nblk > 1:
                    kcat = ks[pl.ds(base, 2 * BLK), :]
                    vcat = vs[pl.ds(base, 2 * BLK), :]
                    s = _dot_nt(q01, kcat) + bm[br, jnp.minimum(n, 1)]
                else:
                    kcat = ks[pl.ds(PADK + base, BLK), :]
                    vcat = vs[pl.ds(PADK + base, BLK), :]
                    s = _dot_nt(q01, kcat) + bm[br, 0, :, BLK:]
                mx = jnp.max(s, axis=-1, keepdims=True)
                p = jnp.exp(s - mx)
                ls = jnp.sum(p, axis=-1, keepdims=True)
                o = _dot(p.astype(BF16), vcat)
                ob[br, rows, :] = jnp.where(head0, o[:BLK], o[BLK:])
                mb[br, rows, :] = jnp.where(head0, mx[:BLK], mx[BLK:])
                lb[br, rows, :] = jnp.where(head0, ls[:BLK], ls[BLK:])
                return carry

            lax.fori_loop(0, 16, blk, 0, unroll=ATTN_UNROLL)

        def merge(i, carry):
            rows = pl.ds(pl.multiple_of(i * 256, 256), 256)
            m_all = jnp.maximum(jnp.maximum(mb[0, rows, :], mb[1, rows, :]), mb[2, rows, :])
            num = jnp.zeros((256, BLK), F32)
            den = jnp.zeros((256, BLK), F32)
            for br in range(nbr):
                c = jnp.exp(mb[br, rows, :] - m_all)
                num = num + ob[br, rows, :] * c
                den = den + lb[br, rows, :] * c
            o_ref[rows, :] = num / den
            lse_ref[rows, :] = m_all + jnp.log(den)
            return carry

        lax.fori_loop(0, S // 256, merge, 0)

    npair = NH // 2
    blk_spec = lambda off: pl.BlockSpec((S, BLK), lambda b, hp: (b, off + hp))
    return pl.pallas_call(
        body, name="attn_fwd", grid=(BL, npair),
        in_specs=[blk_spec(0), blk_spec(npair), blk_spec(2 * npair),
                  pl.BlockSpec((nbr, 2, BLK, 2 * BLK), lambda b, hp: (0, hp, 0, 0))],
        out_specs=(blk_spec(0), blk_spec(0)),
        out_shape=(SDS((T, AW), F32), SDS((T, AW), F32)),
        scratch_shapes=[pltpu.VMEM((S, BLK), F32)] * 3 + [pltpu.VMEM((S, BLK), BF16)] * 2
        + [pltpu.VMEM((PADK + S, BLK), BF16)] * 2 + [pltpu.VMEM((nbr, 2, 2 * BLK, 2 * BLK), F32)]
        + [pltpu.VMEM((nbr, S, BLK), F32)] * 3,
    )(qkv, qkv, qkv, bias)


def _attn_bwd(qkv, attn, lse, dattn, bias):
    nbr = len(BRANCHES)

    def body(q_ref, k_ref, v_ref, o_ref, lse_ref, do_ref, bias_ref,
             dq_ref, dk_ref, dv_ref, sq_ref, sk_ref, sv_ref, db_ref,
             qf, kf, vf, dl, dqa, dka, dva, qs0, qs1, ds0, ds1, ks, vs, dks, dvs, bm):
        b = pl.program_id(1)
        qf[...] = q_ref[...].astype(F32)
        kf[...] = k_ref[...].astype(F32)
        vf[...] = v_ref[...].astype(F32)
        dqa[...] = jnp.zeros_like(dqa)
        dka[...] = jnp.zeros_like(dka)
        dva[...] = jnp.zeros_like(dva)
        _masked_bias(bias_ref, bm)
        ks[pl.ds(0, PADK), :] = jnp.zeros((PADK, BLK), BF16)
        vs[pl.ds(0, PADK), :] = jnp.zeros((PADK, BLK), BF16)
        head0 = lax.broadcasted_iota(jnp.int32, (BLK, BLK), 1) < HD
        split_q = _head_split(lambda t: t * SCALE)
        split_do = _head_split(lambda t: t)

        @pl.when(b == 0)
        def _():
            db_ref[...] = jnp.zeros_like(db_ref)
            sq_ref[...] = jnp.zeros_like(sq_ref)
            sk_ref[...] = jnp.zeros_like(sk_ref)
            sv_ref[...] = jnp.zeros_like(sv_ref)

        def delta(i, carry):
            rows = pl.ds(pl.multiple_of(i * 256, 256), 256)
            prod = do_ref[rows, :] * o_ref[rows, :]
            h0 = lax.broadcasted_iota(jnp.int32, (256, BLK), 1) < HD
            d0 = jnp.sum(jnp.where(h0, prod, 0.0), axis=-1, keepdims=True)
            d1 = jnp.sum(jnp.where(h0, 0.0, prod), axis=-1, keepdims=True)
            dl[rows, :] = jnp.where(h0, d0, d1)
            return carry

        lax.fori_loop(0, S // 256, delta, 0)

        for br in range(nbr):
            nblk = _branch_geometry(br)[2]

            def stage(src, off, rows):
                qs0[pl.ds(off, rows), :], qs1[pl.ds(off, rows), :] = split_q(qf[src, :])
                ds0[pl.ds(off, rows), :], ds1[pl.ds(off, rows), :] = split_do(do_ref[src, :])
                ks[pl.ds(PADK + off, rows), :] = kf[src, :].astype(BF16)
                vs[pl.ds(PADK + off, rows), :] = vf[src, :].astype(BF16)

            _sub_layout_loop(br, stage)
            dks[...] = jnp.zeros_like(dks)
            dvs[...] = jnp.zeros_like(dvs)

            def blk(i, carry, br=br, nblk=nblk):
                base = pl.multiple_of(i * BLK, BLK)
                rows, n = _token_rows(br, i)
                q01 = jnp.concatenate([qs0[pl.ds(base, BLK), :], qs1[pl.ds(base, BLK), :]], axis=0)
                do01 = jnp.concatenate([ds0[pl.ds(base, BLK), :], ds1[pl.ds(base, BLK), :]], axis=0)
                lse_b = lse_ref[rows, :]
                dl_b = dl[rows, :]
                lse01 = jnp.concatenate([lse_b[:, 0:1], lse_b[:, HD:HD + 1]], axis=0)
                dl01 = jnp.concatenate([dl_b[:, 0:1], dl_b[:, HD:HD + 1]], axis=0)
                if nblk > 1:
                    krows = pl.ds(base, 2 * BLK)
                    bias_m = bm[br, jnp.minimum(n, 1)]
                else:
                    krows = pl.ds(PADK + base, BLK)
                    bias_m = bm[br, 0, :, BLK:]
                kcat = ks[krows, :]
                vcat = vs[krows, :]
                p = jnp.exp(_dot_nt(q01, kcat) + bias_m - lse01)
                dsv = p * (_dot_nt(do01, vcat) - dl01)
                if nblk > 1:
                    db_ref[br, 0] += dsv[:BLK]
                    db_ref[br, 1] += dsv[BLK:]
                else:
                    db_ref[br, 0, :, BLK:] += dsv[:BLK]
                    db_ref[br, 1, :, BLK:] += dsv[BLK:]
                dsb = dsv.astype(BF16)
                dq01 = _dot(dsb, kcat)
                dqa[rows, :] = dqa[rows, :] + jnp.where(head0, dq01[:BLK], dq01[BLK:])
                dks[krows, :] = dks[krows, :] + _dot_tn(dsb, q01)
                dvs[krows, :] = dvs[krows, :] + _dot_tn(p.astype(BF16), do01)
                return carry

            lax.fori_loop(0, 16, blk, 0, unroll=ATTN_UNROLL)

            def fold(src, off, rows):
                dka[src, :] = dka[src, :] + dks[pl.ds(PADK + off, rows), :]
                dva[src, :] = dva[src, :] + dvs[pl.ds(PADK + off, rows), :]

            _sub_layout_loop(br, fold)

        def flush(i, carry):
            rows = pl.ds(pl.multiple_of(i * 256, 256), 256)
            for acc, out, cs, mul in ((dqa, dq_ref, sq_ref, SCALE), (dka, dk_ref, sk_ref, 1.0), (dva, dv_ref, sv_ref, 1.0)):
                val = acc[rows, :] * mul
                out[rows, :] = val.astype(BF16)
                cs[...] += _colsum(val)
            return carry

        lax.fori_loop(0, S // 256, flush, 0)

    npair = NH // 2
    blk_spec = lambda off: pl.BlockSpec((S, BLK), lambda hp, b: (b, off + hp))
    sum_spec = pl.BlockSpec((1, BLK), lambda hp, b: (0, hp))
    return pl.pallas_call(
        body, name="attn_bwd", grid=(npair, BL),
        in_specs=[blk_spec(0), blk_spec(npair), blk_spec(2 * npair), blk_spec(0), blk_spec(0), blk_spec(0),
                  pl.BlockSpec((nbr, 2, BLK, 2 * BLK), lambda hp, b: (0, hp, 0, 0))],
        out_specs=(blk_spec(0), blk_spec(0), blk_spec(0), sum_spec, sum_spec, sum_spec,
                   pl.BlockSpec((nbr, 2, BLK, 2 * BLK), lambda hp, b: (0, hp, 0, 0))),
        out_shape=(SDS((T, AW), BF16), SDS((T, AW), BF16), SDS((T, AW), BF16),
                   SDS((1, AW), F32), SDS((1, AW), F32), SDS((1, AW), F32),
                   SDS((nbr, NH, BLK, 2 * BLK), F32)),
        scratch_shapes=[pltpu.VMEM((S, BLK), F32)] * 7 + [pltpu.VMEM((S, BLK), BF16)] * 4
        + [pltpu.VMEM((PADK + S, BLK), BF16)] * 2 + [pltpu.VMEM((PADK + S, BLK), F32)] * 2
        + [pltpu.VMEM((nbr, 2, 2 * BLK, 2 * BLK), F32)],
    )(qkv, qkv, qkv, attn, lse, dattn, bias)


CH = 256
PADR = 32


def _tap_phases(offset_of_tap):
    taps = sorted((offset_of_tap(k) % 8, offset_of_tap(k) - offset_of_tap(k) % 8, k) for k in range(CK))
    assert all(lo + CH + ph <= CH + PADR for ph, lo, _ in taps)
    return taps


def _rows_up(win):
    made = {0: win}

    def get(phase):
        if phase not in made:
            made[phase] = pltpu.roll(win, win.shape[0] - phase, 0)
        return made[phase]
    return get


def _conv_fwd(ag, conv_w, conv_b):
    def body(ag_ref, w_ref, b_ref, u1_ref, u0p):
        u0p[pl.ds(0, PADR), :] = jnp.zeros((PADR, CW), F32)

        def glu(i, carry):
            t0 = pl.multiple_of(i * CH, CH)
            a = ag_ref[pl.ds(t0, CH), :CW]
            g = ag_ref[pl.ds(t0, CH), CW:]
            u0p[pl.ds(PADR + t0, CH), :] = a * _sigmoid(g)
            return carry

        lax.fori_loop(0, S // CH, glu, 0)

        def conv(i, carry):
            t0 = pl.multiple_of(i * CH, CH)
            win = u0p[pl.ds(t0, CH + PADR), :]
            acc = jnp.zeros((CH, CW), F32) + b_ref[...]
            up = _rows_up(win)
            for phase, lo, k in _tap_phases(lambda k: PADR - (CK - 1) + k):
                acc = acc + up(phase)[lo:lo + CH, :] * w_ref[k:k + 1, :]
            u1_ref[pl.ds(t0, CH), :] = acc
            return carry

        lax.fori_loop(0, S // CH, conv, 0)

    return pl.pallas_call(
        body, name="conv_fwd", grid=(BL,),
        in_specs=[pl.BlockSpec((S, 2 * CW), lambda b: (b, 0)),
                  pl.BlockSpec((CK, CW), lambda b: (0, 0)),
                  pl.BlockSpec((1, CW), lambda b: (0, 0))],
        out_specs=pl.BlockSpec((S, CW), lambda b: (b, 0)),
        out_shape=SDS((T, CW), F32),
        scratch_shapes=[pltpu.VMEM((S + PADR, CW), F32)],
    )(ag, conv_w, conv_b)


def _conv_post(u1, cg, cb):
    mu = _rowmean(u1)
    uc = u1 - mu
    rstd = lax.rsqrt(_rowmean(uc * uc) + LN_EPS)
    xh = uc * rstd
    u2 = xh * cg + cb
    sg = _sigmoid(u2)
    return xh, rstd, u2, sg, u2 * sg


def _mix_fwd(attn, u1, ga, gc, cg, cb):
    def body(a_ref, u_ref, ga_ref, gc_ref, cg_ref, cb_ref, o_ref):
        a = a_ref[...]
        ra = lax.rsqrt(_rowmean(a * a) + LN_EPS)
        o_ref[:, :AW] = (a * ra * ga_ref[...]).astype(BF16)
        _, _, _, _, u3 = _conv_post(u_ref[...], cg_ref[...], cb_ref[...])
        rc = lax.rsqrt(_rowmean(u3 * u3) + LN_EPS)
        o_ref[:, AW:] = (u3 * rc * gc_ref[...]).astype(BF16)

    vec = lambda w: pl.BlockSpec((1, w), lambda m: (0, 0))
    return pl.pallas_call(
        body, name="mix_fwd", grid=(T // TM,),
        in_specs=[pl.BlockSpec((TM, AW), lambda m: (m, 0)), pl.BlockSpec((TM, CW), lambda m: (m, 0)),
                  vec(AW), vec(CW), vec(CW), vec(CW)],
        out_specs=pl.BlockSpec((TM, D), lambda m: (m, 0)),
        out_shape=SDS((T, D), BF16),
    )(attn, u1, ga, gc, cg, cb)


def _mix_bwd(dz1, w_out, attn, u1, ga, gc, cg, cb):
    def body(dz_ref, w_ref, a_ref, u_ref, ga_ref, gc_ref, cg_ref, cb_ref,
             da_ref, du_ref, g_an, g_cn, g_lg, g_lb, g_cb):
        @pl.when(pl.program_id(0) == 0)
        def _():
            for r in (g_an, g_cn, g_lg, g_lb, g_cb):
                r[...] = jnp.zeros_like(r)

        dm = _dot_nt(dz_ref[...].astype(BF16), w_ref[...])
        a = a_ref[...]
        dna = dm[:, :AW]
        ra = lax.rsqrt(_rowmean(a * a) + LN_EPS)
        g_an[...] += _colsum(dna * a * ra)
        dat = dna * ga_ref[...]
        da_ref[...] = ra * dat - a * (ra * ra * ra) * _rowmean(dat * a)

        xh, rstd, u2, sg, u3 = _conv_post(u_ref[...], cg_ref[...], cb_ref[...])
        dnc = dm[:, AW:]
        rc = lax.rsqrt(_rowmean(u3 * u3) + LN_EPS)
        g_cn[...] += _colsum(dnc * u3 * rc)
        dut = dnc * gc_ref[...]
        du3 = rc * dut - u3 * (rc * rc * rc) * _rowmean(dut * u3)
        du2 = du3 * sg * (1.0 + u2 * (1.0 - sg))
        g_lg[...] += _colsum(du2 * xh)
        g_lb[...] += _colsum(du2)
        dxh = du2 * cg_ref[...]
        du1 = rstd * (dxh - _rowmean(dxh) - xh * _rowmean(dxh * xh))
        g_cb[...] += _colsum(du1)
        du_ref[...] = du1

    vec = lambda w: pl.BlockSpec((1, w), lambda m: (0, 0))
    return pl.pallas_call(
        body, name="mix_bwd", grid=(T // TM,),
        in_specs=[pl.BlockSpec((TM, D), lambda m: (m, 0)), pl.BlockSpec((D, D), lambda m: (0, 0)),
                  pl.BlockSpec((TM, AW), lambda m: (m, 0)),
                  pl.BlockSpec((TM, CW), lambda m: (m, 0)), vec(AW), vec(CW), vec(CW), vec(CW)],
        out_specs=(pl.BlockSpec((TM, AW), lambda m: (m, 0)), pl.BlockSpec((TM, CW), lambda m: (m, 0)),
                   vec(AW), vec(CW), vec(CW), vec(CW), vec(CW)),
        out_shape=(SDS((T, AW), F32), SDS((T, CW), F32),
                   SDS((1, AW), F32), SDS((1, CW), F32), SDS((1, CW), F32), SDS((1, CW), F32), SDS((1, CW), F32)),
    )(dz1, w_out, attn, u1, ga, gc, cg, cb)


def _conv_bwd(du1, ag, conv_w):
    def body(du_ref, ag_ref, w_ref, dag_ref, cs_ref, gw_ref, u0p, dup):
        @pl.when(pl.program_id(0) == 0)
        def _():
            cs_ref[...] = jnp.zeros_like(cs_ref)
            gw_ref[...] = jnp.zeros_like(gw_ref)

        u0p[pl.ds(0, PADR), :] = jnp.zeros((PADR, CW), F32)
        dup[pl.ds(S, PADR), :] = jnp.zeros((PADR, CW), F32)

        def fill(i, carry):
            t0 = pl.multiple_of(i * CH, CH)
            a = ag_ref[pl.ds(t0, CH), :CW]
            g = ag_ref[pl.ds(t0, CH), CW:]
            u0p[pl.ds(PADR + t0, CH), :] = a * _sigmoid(g)
            dup[pl.ds(t0, CH), :] = du_ref[pl.ds(t0, CH), :]
            return carry

        lax.fori_loop(0, S // CH, fill, 0)

        def chunk(i, carry):
            t0 = pl.multiple_of(i * CH, CH)
            d = dup[pl.ds(t0, CH), :]
            win_u = u0p[pl.ds(t0, CH + PADR), :]
            win_d = dup[pl.ds(t0, CH + PADR), :]
            du0 = jnp.zeros((CH, CW), F32)
            up_u = _rows_up(win_u)
            for phase, lo, k in _tap_phases(lambda k: PADR - (CK - 1) + k):
                gw_ref[k:k + 1, :] += _colsum(d * up_u(phase)[lo:lo + CH, :])
            up_d = _rows_up(win_d)
            for phase, lo, k in _tap_phases(lambda k: CK - 1 - k):
                du0 = du0 + up_d(phase)[lo:lo + CH, :] * w_ref[k:k + 1, :]
            a = ag_ref[pl.ds(t0, CH), :CW]
            sg = _sigmoid(ag_ref[pl.ds(t0, CH), CW:])
            da = du0 * sg
            dg = du0 * a * sg * (1.0 - sg)
            dag_ref[pl.ds(t0, CH), :CW] = da.astype(BF16)
            dag_ref[pl.ds(t0, CH), CW:] = dg.astype(BF16)
            cs_ref[:, :CW] += _colsum(da)
            cs_ref[:, CW:] += _colsum(dg)
            return carry

        lax.fori_loop(0, S // CH, chunk, 0)

    return pl.pallas_call(
        body, name="conv_bwd", grid=(BL,),
        in_specs=[pl.BlockSpec((S, CW), lambda b: (b, 0)), pl.BlockSpec((S, 2 * CW), lambda b: (b, 0)),
                  pl.BlockSpec((CK, CW), lambda b: (0, 0))],
        out_specs=(pl.BlockSpec((S, 2 * CW), lambda b: (b, 0)),
                   pl.BlockSpec((1, 2 * CW), lambda b: (0, 0)),
                   pl.BlockSpec((PADR, CW), lambda b: (0, 0))),
        out_shape=(SDS((T, 2 * CW), BF16), SDS((1, 2 * CW), F32), SDS((PADR, CW), F32)),
        scratch_shapes=[pltpu.VMEM((S + PADR, CW), F32), pltpu.VMEM((S + PADR, CW), F32)],
    )(du1, ag, conv_w)


def _layer_norm_fwd(z):
    mu = _rowmean(z)
    zc = z - mu
    rstd = lax.rsqrt(_rowmean(zc * zc) + LN_EPS)
    return zc * rstd, rstd


def _layer_norm_bwd(dy, xh, rstd, g):
    dxh = dy * g
    return rstd * (dxh - _rowmean(dxh) - xh * _rowmean(dxh * xh))


def _out_proj_ln1(mixed, w_out, x2, g1, b1):
    def body(a_ref, w_ref, x_ref, g_ref, b_ref, xh_ref, rstd_ref, x1_ref):
        z = ALPHA * x_ref[...] + _dot(a_ref[...], w_ref[...])
        xh, rstd = _layer_norm_fwd(z)
        xh_ref[...] = xh
        rstd_ref[...] = rstd
        x1_ref[...] = (xh * g_ref[...] + b_ref[...]).astype(BF16)

    vec = pl.BlockSpec((1, D), lambda m: (0, 0))
    row = pl.BlockSpec((TM, D), lambda m: (m, 0))
    return pl.pallas_call(
        body, name="out_proj_ln1", grid=(T // TM,),
        in_specs=[row, pl.BlockSpec((D, D), lambda m: (0, 0)), row, vec, vec],
        out_specs=(row, pl.BlockSpec((TM, 1), lambda m: (m, 0)), row),
        out_shape=(SDS((T, D), F32), SDS((T, 1), F32), SDS((T, D), BF16)),
    )(mixed, w_out, x2, g1, b1)


def _seq_start(m):
    return lax.bitwise_and(m, S // TMF - 1) == 0


def _shift_down(x, before, k):
    rolled = pltpu.roll(x, k, 0)
    row = lax.broadcasted_iota(jnp.int32, before.shape, 0)
    head = jnp.where(row < k, pltpu.roll(before, k, 0), rolled[:8])
    return jnp.concatenate([head, rolled[8:]], axis=0)


def _shift_up(x, after, k):
    n = x.shape[0]
    rolled = pltpu.roll(x, n - k, 0)
    row = lax.broadcasted_iota(jnp.int32, after.shape, 0)
    tail = jnp.where(row >= 8 - k, pltpu.roll(after, 8 - k, 0), rolled[n - 8:])
    return jnp.concatenate([rolled[:n - 8], tail], axis=0)


def _ffn_up(x1b, w_up, fcw, fcb):
    def body(x_ref, wg_ref, wv_ref, cwg_ref, cwv_ref, cbg_ref, cbv_ref, up_ref, gv_ref, act_ref, prev_g, prev_v):
        @pl.when(_seq_start(pl.program_id(1)))
        def _():
            prev_g[...] = jnp.zeros_like(prev_g)
            prev_v[...] = jnp.zeros_like(prev_v)

        x = x_ref[...]
        outs = []
        for w_ref, cw_ref, cb_ref, prev, lo in ((wg_ref, cwg_ref, cbg_ref, prev_g, 0), (wv_ref, cwv_ref, cbv_ref, prev_v, FT)):
            u = _dot_nt(x, w_ref[...])
            up_ref[:, lo:lo + FT] = u.astype(BF16)
            before = prev[...]
            y = (cw_ref[2:3, :] * u + cw_ref[1:2, :] * _shift_down(u, before, 1)
                 + cw_ref[0:1, :] * _shift_down(u, before, 2) + cb_ref[...])
            prev[...] = u[TMF - 8:]
            gv_ref[:, lo:lo + FT] = y.astype(BF16)
            outs.append(y)
        gate, val = outs
        act_ref[...] = (gate * _sigmoid(gate) * val).astype(BF16)

    wspec = lambda off: pl.BlockSpec((FT, D), lambda n, m: (n + off, 0))
    cwspec = lambda off: pl.BlockSpec((FK, FT), lambda n, m: (0, n + off))
    cbspec = lambda off: pl.BlockSpec((1, FT), lambda n, m: (0, n + off))
    pair = pl.BlockSpec((TMF, 2 * FT), lambda n, m: (m, n))
    return pl.pallas_call(
        body, name="ffn_up", grid=(NFT, T // TMF),
        in_specs=[pl.BlockSpec((TMF, D), lambda n, m: (m, 0)), wspec(0), wspec(NFT),
                  cwspec(0), cwspec(NFT), cbspec(0), cbspec(NFT)],
        out_specs=(pair, pair, pl.BlockSpec((TMF, FT), lambda n, m: (m, n))),
        out_shape=(SDS((T, 2 * DFF), BF16), SDS((T, 2 * DFF), BF16), SDS((T, DFF), BF16)),
        scratch_shapes=[pltpu.VMEM((8, FT), F32)] * 2,
    )(x1b, w_up, w_up, fcw, fcw, fcb, fcb)


def _ffn_down_loss(act, w_down, xh1, g1, b1, g2, b2, target):
    def body(a_ref, w_ref, xh1_ref, g1_ref, b1_ref, g2_ref, b2_ref, t_ref, dz_ref, loss_ref, gg_ref, gb_ref):
        @pl.when(pl.program_id(0) == 0)
        def _():
            loss_ref[...] = jnp.zeros_like(loss_ref)
            gg_ref[...] = jnp.zeros_like(gg_ref)
            gb_ref[...] = jnp.zeros_like(gb_ref)

        for sub in range(TM // TMF):
            rows = pl.ds(sub * TMF, TMF)
            x1 = xh1_ref[rows, :] * g1_ref[...] + b1_ref[...]
            z = ALPHA * x1 + _dot(a_ref[rows, :], w_ref[...])
            xh, rstd = _layer_norm_fwd(z)
            diff = xh * g2_ref[...] + b2_ref[...] - t_ref[rows, :]
            loss_ref[...] += 0.5 * _colsum(_rowmean(diff * diff))
            dout = diff * (1.0 / D)
            gg_ref[...] += _colsum(dout * xh)
            gb_ref[...] += _colsum(dout)
            dz_ref[rows, :] = _layer_norm_bwd(dout, xh, rstd, g2_ref[...])

    vec = pl.BlockSpec((1, D), lambda m: (0, 0))
    row = pl.BlockSpec((TM, D), lambda m: (m, 0))
    return pl.pallas_call(
        body, name="ffn_down_loss", grid=(T // TM,),
        in_specs=[pl.BlockSpec((TM, DFF), lambda m: (m, 0)), pl.BlockSpec((DFF, D), lambda m: (0, 0)),
                  row, vec, vec, vec, vec, row],
        out_specs=(row, pl.BlockSpec((1, 1), lambda m: (0, 0)), vec, vec),
        out_shape=(SDS((T, D), F32), SDS((1, 1), F32), SDS((1, D), F32), SDS((1, D), F32)),
    )(act, w_down, xh1, g1, b1, g2, b2, target)


def _ffn_down_bwd(dz2, w_down, gv, up, fcw):
    tiles = T // TMF

    def body(dz_ref, wd_ref, gv_ref, up_ref, cwg_ref, cwv_ref,
             dpre_ref, csg_ref, csv_ref, gwg_ref, gwv_ref, next_g, next_v):
        step = pl.program_id(1)
        tile = tiles - 1 - step

        @pl.when(step == 0)
        def _():
            for r in (csg_ref, csv_ref, gwg_ref, gwv_ref, next_g, next_v):
                r[...] = jnp.zeros_like(r)

        seq_end = lax.bitwise_and(tile + 1, S // TMF - 1) == 0
        dact = _dot_nt(dz_ref[...].astype(BF16), wd_ref[...])
        gate = gv_ref[:, :FT].astype(F32)
        val = gv_ref[:, FT:].astype(F32)
        sg = _sigmoid(gate)
        gs = gate * sg
        halves = ((dact * val * (sg + gs * (1.0 - sg)), cwg_ref, csg_ref, gwg_ref, next_g, 0),
                  (dact * gs, cwv_ref, csv_ref, gwv_ref, next_v, FT))
        for d0, cw_ref, cs_ref, gw_ref, nxt, lo in halves:
            after = jnp.where(seq_end, 0.0, nxt[...])
            d1 = _shift_up(d0, after, 1)
            d2 = _shift_up(d0, after, 2)
            nxt[...] = d0[:8]
            dpre_ref[:, lo:lo + FT] = (cw_ref[2:3, :] * d0 + cw_ref[1:2, :] * d1 + cw_ref[0:1, :] * d2).astype(BF16)
            cs_ref[...] += _colsum(d0)
            u = up_ref[:, lo:lo + FT].astype(F32)
            for k, dk in enumerate((d2, d1, d0)):
                gw_ref[k:k + 1, :] += _colsum(dk * u)

    cs = pl.BlockSpec((1, FT), lambda n, m: (0, n))
    gw = pl.BlockSpec((FK, FT), lambda n, m: (0, n))
    cwspec = lambda off: pl.BlockSpec((FK, FT), lambda n, m: (0, n + off))
    pair = pl.BlockSpec((TMF, 2 * FT), lambda n, m: (tiles - 1 - m, n))
    return pl.pallas_call(
        body, name="ffn_down_bwd", grid=(NFT, tiles),
        in_specs=[pl.BlockSpec((TMF, D), lambda n, m: (tiles - 1 - m, 0)), pl.BlockSpec((FT, D), lambda n, m: (n, 0)),
                  pair, pair, cwspec(0), cwspec(NFT)],
        out_specs=(pair, cs, cs, gw, gw),
        out_shape=(SDS((T, 2 * DFF), BF16), SDS((1, DFF), F32), SDS((1, DFF), F32),
                   SDS((FK, DFF), F32), SDS((FK, DFF), F32)),
        scratch_shapes=[pltpu.VMEM((8, FT), F32)] * 2,
    )(dz2, w_down, gv, up, fcw, fcw)


def _ffn_up_bwd_ln1(dpre, w_up, dz2, xh1, rstd1, g1):
    def body(a_ref, w_ref, dz2_ref, xh_ref, rstd_ref, g_ref, dz1_ref, gg_ref, gb_ref):
        @pl.when(pl.program_id(0) == 0)
        def _():
            gg_ref[...] = jnp.zeros_like(gg_ref)
            gb_ref[...] = jnp.zeros_like(gb_ref)

        for sub in range(TM // TMF):
            rows = pl.ds(sub * TMF, TMF)
            dx1 = ALPHA * dz2_ref[rows, :]
            for n in range(NFT):
                for half in range(2):
                    a = a_ref[rows, (2 * n + half) * FT:(2 * n + half + 1) * FT]
                    w = w_ref[pl.ds((half * NFT + n) * FT, FT), :]
                    dx1 = dx1 + _dot(a, w)
            xh = xh_ref[rows, :]
            gg_ref[...] += _colsum(dx1 * xh)
            gb_ref[...] += _colsum(dx1)
            dz1_ref[rows, :] = _layer_norm_bwd(dx1, xh, rstd_ref[rows, :], g_ref[...])

    vec = pl.BlockSpec((1, D), lambda m: (0, 0))
    row = pl.BlockSpec((TM, D), lambda m: (m, 0))
    return pl.pallas_call(
        body, name="ffn_up_bwd_ln1", grid=(T // TM,),
        in_specs=[pl.BlockSpec((TM, 2 * DFF), lambda m: (m, 0)),
                  pl.BlockSpec((2 * DFF, D), lambda m: (0, 0), pipeline_mode=pl.Buffered(1)),
                  row, row, pl.BlockSpec((TM, 1), lambda m: (m, 0)), vec],
        out_specs=(row, vec, vec),
        out_shape=(SDS((T, D), F32), SDS((1, D), F32), SDS((1, D), F32)),
    )(dpre, w_up, dz2, xh1, rstd1, g1)


def _grad_w_up(dpre, x1b):
    tk = 1024

    def body(a_ref, b_ref, o_ref, acc):
        k = pl.program_id(1)

        @pl.when(k == 0)
        def _():
            acc[...] = jnp.zeros_like(acc)

        acc[...] += _dot_tn(a_ref[...], b_ref[...])

        @pl.when(k == T // tk - 1)
        def _():
            o_ref[0] = acc[pl.ds(0, FT), :].astype(o_ref.dtype)
            o_ref[1] = acc[pl.ds(FT, FT), :].astype(o_ref.dtype)

    out = pl.pallas_call(
        body, name="grad_w_up", grid=(NFT, T // tk),
        in_specs=[pl.BlockSpec((tk, 2 * FT), lambda n, k: (k, n)), pl.BlockSpec((tk, D), lambda n, k: (k, 0))],
        out_specs=pl.BlockSpec((2, FT, D), lambda n, k: (0, n, 0)),
        out_shape=SDS((2, DFF, D), GRAD_WIRE),
        scratch_shapes=[pltpu.VMEM((2 * FT, D), F32)],
    )(dpre, x1b)
    return out.reshape(2 * DFF, D)


def _row_tile(rows, cols):
    if rows * cols * 4 <= (1 << 20) or rows % 8:
        return rows
    for t in (256, 176, 128, 88, 64, 32, 16, 8):
        if rows % t == 0 and t * cols * 4 <= (1 << 20):
            return t
    return 8


def _sum8(r, name):
    _, rows, cols = r.shape
    tr = _row_tile(rows, cols)

    def body(r_ref, o_ref):
        acc = r_ref[0].astype(F32)
        for p in range(1, NDEV):
            acc = acc + r_ref[p].astype(F32)
        o_ref[...] = acc

    return pl.pallas_call(
        body, name=name, grid=(rows // tr,),
        in_specs=[pl.BlockSpec((NDEV, tr, cols), lambda i: (0, i, 0))],
        out_specs=pl.BlockSpec((tr, cols), lambda i: (i, 0)),
        out_shape=SDS((rows, cols), F32),
    )(r)


def _sum8_adamw(r, w, m, v, name):
    rows, cols = w.shape
    tr = _row_tile(rows, cols)

    def body(r_ref, w_ref, m_ref, v_ref, g_out, d_ref, nm_ref, nv_ref):
        g_ = r_ref[0].astype(F32)
        for p in range(1, NDEV):
            g_ = g_ + r_ref[p].astype(F32)
        m_ = B1 * m_ref[...] + (1.0 - B1) * g_
        v_ = B2 * v_ref[...] + (1.0 - B2) * jnp.square(g_)
        m_hat = m_ / (1.0 - B1 ** STEP)
        v_hat = v_ / (1.0 - B2 ** STEP)
        g_out[...] = g_
        d_ref[...] = -LR * (m_hat / (jnp.sqrt(v_hat) + AEPS) + WD * w_ref[...])
        nm_ref[...] = m_
        nv_ref[...] = v_

    spec = pl.BlockSpec((tr, cols), lambda i: (i, 0))
    shp = SDS((rows, cols), F32)
    return pl.pallas_call(
        body, name=name, grid=(rows // tr,),
        in_specs=[pl.BlockSpec((NDEV, tr, cols), lambda i: (0, i, 0))] + [spec] * 3, out_specs=(spec,) * 4,
        out_shape=(shp,) * 4,
    )(r, w, m, v)


def _adamw_many(ws, gs, ms, vs, name):
    n = len(ws)

    def body(*refs):
        for i in range(n):
            w_ref, g_ref, m_ref, v_ref, d_ref, nm_ref, nv_ref = refs[i::n]
            g_ = g_ref[...]
            m_ = B1 * m_ref[...] + (1.0 - B1) * g_
            v_ = B2 * v_ref[...] + (1.0 - B2) * jnp.square(g_)
            m_hat = m_ / (1.0 - B1 ** STEP)
            v_hat = v_ / (1.0 - B2 ** STEP)
            d_ref[...] = -LR * (m_hat / (jnp.sqrt(v_hat) + AEPS) + WD * w_ref[...])
            nm_ref[...] = m_
            nv_ref[...] = v_

    shapes = tuple(SDS(w.shape, F32) for w in ws)
    res = pl.pallas_call(body, name=name, out_shape=shapes * 3)(*ws, *gs, *ms, *vs)
    return res[:n], res[n:2 * n], res[2 * n:]


def _local_step(x2, target, rel_table, first_weights, b_in, conv_b, conv_ln_g, conv_ln_b, attn_norm_g,
                conv_norm_g, late_weights, ln1_g, ln1_b, ffn_conv_b, ln2_g, ln2_b, ship_ffn_grads, ship_tail):
    buckets = jnp.asarray(_bucket_maps())
    bias = _bias_table(rel_table, buckets)
    xb, cast_done = _cast_x(x2, bias[0, 0, :8, :BLK])
    w_in_t, conv_w, ffn_conv_w = first_weights(cast_done)

    qkv, ag = _proj_in(xb, w_in_t, b_in)
    attn, lse = _attn_fwd(qkv, bias)
    u1 = _conv_fwd(ag, conv_w, conv_b)
    mixed = _mix_fwd(attn, u1, attn_norm_g, conv_norm_g, conv_ln_g, conv_ln_b)
    w_out = late_weights(0, mixed)
    xh1, rstd1, x1b = _out_proj_ln1(mixed, w_out, x2, ln1_g, ln1_b)
    w_up = late_weights(1, x1b)
    up, gv, act = _ffn_up(x1b, w_up, ffn_conv_w, ffn_conv_b)
    w_down = late_weights(2, act)
    dz2, loss, g_ln2_g, g_ln2_b = _ffn_down_loss(act, w_down, xh1, ln1_g, ln1_b, ln2_g, ln2_b, target)

    dpre, cs_g, cs_v, gfw_g, gfw_v = _ffn_down_bwd(dz2, w_down, gv, up, ffn_conv_w)
    g_w_down = _mm_tn(act, dz2, DFF // 2, 512, "grad_w_down")
    dz1, g_ln1_g, g_ln1_b = _ffn_up_bwd_ln1(dpre, w_up, dz2, xh1, rstd1, ln1_g)
    g_w_out = _mm_tn(mixed, dz1, D, 512, "grad_w_out")
    zero = ship_ffn_grads(g_w_down, _grad_w_up(dpre, x1b), g_w_out)
    dattn, du1, g_an, g_cn, g_clg, g_clb, g_cb = _mix_bwd(
        dz1, w_out, attn, u1, attn_norm_g + zero, conv_norm_g, conv_ln_g, conv_ln_b)
    dag, cs_ag, g_conv_w = _conv_bwd(du1, ag, conv_w)
    dq, dk, dv, cs_q, cs_k, cs_v2, dbias = _attn_bwd(qkv, attn, lse, dattn, bias)
    g_rel = _rel_table_grad(dbias, buckets)
    pieces = [dq, dk, dv, dag]
    g_w_in_t = _grad_w_in(pieces, xb)

    grads = dict(
        rel_table=g_rel,
        b_in=jnp.concatenate([cs_q, cs_k, cs_v2, cs_ag], axis=1),
        conv_b=g_cb, conv_ln_g=g_clg, conv_ln_b=g_clb, attn_norm_g=g_an, conv_norm_g=g_cn,
        ln1_g=g_ln1_g, ln1_b=g_ln1_b,
        ffn_conv_b=jnp.concatenate([cs_g, cs_v], axis=1),
        ln2_g=g_ln2_g, ln2_b=g_ln2_b,
        conv_w=g_conv_w[:CK],
        ffn_conv_w=jnp.concatenate([gfw_g, gfw_v], axis=1),
    )
    grads["loss"] = loss
    zero11 = ship_tail(g_w_in_t, grads)
    grad_x = _grad_x(pieces, w_in_t, dz1, zero11)
    return loss, grad_x


SMALL = (("rel_table", (NBUCKET, NH)), ("b_in", (1, INW)), ("conv_b", (1, CW)), ("conv_ln_g", (1, CW)),
         ("conv_ln_b", (1, CW)), ("attn_norm_g", (1, AW)), ("conv_norm_g", (1, CW)), ("ln1_g", (1, D)),
         ("ln1_b", (1, D)), ("ffn_conv_b", (1, 2 * DFF)), ("ln2_g", (1, D)), ("ln2_b", (1, D)))
SHARDED_SMALL = (("conv_w", (CK, CW)), ("ffn_conv_w", (FK, 2 * DFF)))


def _pack(parts):
    flat = jnp.concatenate([p.reshape(-1) for p in parts])
    tile = 8 * PACK_LANES
    pad = (-flat.shape[0]) % tile
    return jnp.pad(flat, (0, pad)).reshape(-1, PACK_LANES)


def _unpack(packed, specs):
    flat = packed.reshape(-1)
    out, off = {}, 0
    for name, shp in specs:
        size = int(np.prod(shp))
        out[name] = flat[off:off + size].reshape(shp)
        off += size
    return out


def kernel(x, rel_table, w_in, b_in, conv_w, conv_b, conv_ln_g, conv_ln_b, attn_norm_g, conv_norm_g, w_out, ln1_g, ln1_b, w_up, ffn_conv_w, ffn_conv_b, w_down, ln2_g, ln2_b, loss_target, m_rel_table, m_w_in, m_b_in, m_conv_w, m_conv_b, m_conv_ln_g, m_conv_ln_b, m_attn_norm_g, m_conv_norm_g, m_w_out, m_ln1_g, m_ln1_b, m_w_up, m_ffn_conv_w, m_ffn_conv_b, m_w_down, m_ln2_g, m_ln2_b, v_rel_table, v_w_in, v_b_in, v_conv_w, v_conv_b, v_conv_ln_g, v_conv_ln_b, v_attn_norm_g, v_conv_norm_g, v_w_out, v_ln1_g, v_ln1_b, v_w_up, v_ffn_conv_w, v_ffn_conv_b, v_w_down, v_ln2_g, v_ln2_b):
    given = dict(locals())
    me = 4 * lax.axis_index("x") + 2 * lax.axis_index("y") + lax.axis_index("c")

    cols = lambda a: a.transpose(1, 0, 2).reshape(a.shape[1], NDEV * a.shape[2])
    rows = lambda a: a.reshape(NDEV * a.shape[1], a.shape[2])
    stack = lambda a: a.reshape(NDEV, a.shape[0] // NDEV, a.shape[1])

    small_specs = SMALL + SHARDED_SMALL
    packed_specs = small_specs + (("loss", (1, 1)),)
    grad, delta, new_m, new_v = {}, {}, {}, {}

    def adamw_big(n, partials, transposed=False):
        shp = given[n].shape
        to2d = (lambda a: a.reshape(shp[-2], shp[-1]).T) if transposed else (lambda a: a.reshape(shp[-2], shp[-1]))
        back = (lambda a: a.T.reshape(shp)) if transposed else (lambda a: a.reshape(shp))
        g_, d_, m_, v_ = _sum8_adamw(partials, to2d(given[n]), to2d(given["m_" + n]), to2d(given["v_" + n]), "adamw_" + n)
        grad[n], delta[n], new_m[n], new_v[n] = back(g_), back(d_), back(m_), back(v_)
        return d_

    first_state, zero0 = _exchange_start(
        [(w_in[0].T.astype(BF16), "gather"), (conv_w[0], "gather"), (ffn_conv_w[0], "gather")], "gather_first_start")

    def first_weights(after):
        lands = _exchange_wait(first_state, after, "gather_first_wait")
        return rows(lands[0]), cols(lands[1]), cols(lands[2])

    late_state, zero1 = _exchange_start(
        [(w_out[0].astype(BF16), "gather"), (w_up[0].T.astype(BF16) + zero0.astype(BF16), "gather"),
         (w_down[0].astype(BF16), "gather")], "gather_late_start")

    def late_weights(i, after):
        return rows(_exchange_wait(late_state, after, "gather_late_wait_%d" % i, only=(i,))[i])

    shipped = {}

    def ship_ffn_grads(g_w_down, g_w_up_t, g_w_out):
        shipped["ffn"], zero2 = _exchange_start(
            [(stack(a), "scatter") for a in (g_w_down, g_w_up_t, g_w_out)], "ffn_grads_start")
        return zero2

    def ship_tail(g_w_in_t, small_grads):
        shipped["tail"], zero3 = _exchange_start(
            [(stack(g_w_in_t), "scatter"), (_pack([small_grads[n] for n, _ in packed_specs]), "gather")],
            "tail_grads_start")
        return zero3.reshape(1, 1)

    loss, grad_x = _local_step(
        x.reshape(T, D), loss_target.reshape(T, D), rel_table + zero1, first_weights, b_in, conv_b, conv_ln_g,
        conv_ln_b, attn_norm_g, conv_norm_g, late_weights, ln1_g, ln1_b, ffn_conv_b,
        ln2_g, ln2_b, ship_ffn_grads, ship_tail)

    got_down, got_up, got_out = _exchange_wait(shipped["ffn"], grad_x, "ffn_grads_wait")
    adamw_big("w_down", got_down)
    adamw_big("w_up", got_up, transposed=True)
    last = adamw_big("w_out", got_out)

    got_in, got_small = _exchange_wait(shipped["tail"], last, "tail_grads_wait")
    adamw_big("w_in", got_in, transposed=True)
    small = _unpack(_sum8(got_small, "sum_small"), packed_specs)
    small["conv_w"] = lax.dynamic_slice_in_dim(small["conv_w"], me * (CW // NDEV), CW // NDEV, axis=1)
    small["ffn_conv_w"] = lax.dynamic_slice_in_dim(small["ffn_conv_w"], me * (2 * DFF // NDEV), 2 * DFF // NDEV, axis=1)
    names = [n for n, _ in small_specs]
    two = lambda a: a.reshape(a.shape[-2], a.shape[-1])
    ds, nms, nvs = _adamw_many([two(given[n]) for n in names], [small[n] for n in names],
                               [two(given["m_" + n]) for n in names], [two(given["v_" + n]) for n in names], "adamw_small")
    for n, d_, m_, v_ in zip(names, ds, nms, nvs):
        shp = given[n].shape
        grad[n], delta[n], new_m[n], new_v[n] = small[n].reshape(shp), d_.reshape(shp), m_.reshape(shp), v_.reshape(shp)

    order = ("rel_table", "w_in", "b_in", "conv_w", "conv_b", "conv_ln_g", "conv_ln_b", "attn_norm_g",
             "conv_norm_g", "w_out", "ln1_g", "ln1_b", "w_up", "ffn_conv_w", "ffn_conv_b", "w_down", "ln2_g", "ln2_b")
    return (small["loss"][0, 0], grad_x.reshape(BL, S, D), *[grad[n] for n in order], *[delta[n] for n in order],
            *[new_m[n] for n in order], *[new_v[n] for n in order])
```

```python
import math

import numpy as np
import jax
import jax.numpy as jnp
from jax import lax
from jax.experimental import pallas as pl
from jax.experimental.pallas import tpu as pltpu

F32 = jnp.float32
BF16 = jnp.bfloat16
SDS = jax.ShapeDtypeStruct

NDEV = 8
D = 1024
S = 2048
BL = 2
T = BL * S
NH = 12
HD = 64
AW = NH * HD
CW = D - AW
INW = 3 * AW + 2 * CW
CK = 31
DFF = 2816
FK = 3
BLK = 128
NBUCKET = 32
BRANCHES = ((128, 1), (512, 4), (2048, 16))
ALPHA = 2.0 ** 0.25
LN_EPS = 1e-5
NEG_INF = -1e30
LR, B1, B2, AEPS, WD, STEP = 0.001, 0.9, 0.999, 1e-08, 0.01, 10

TM = 512
FT = 1408
NFT = DFF // FT
TMF = 256
PACK_LANES = 128
GRAD_WIRE = BF16

assert all(w // d == BLK for w, d in BRANCHES)


def _dot(a, b):
    return jnp.dot(a, b, preferred_element_type=F32)


def _dot_nt(a, b):
    return lax.dot_general(a, b, (((1,), (1,)), ((), ())), preferred_element_type=F32)


def _dot_tn(a, b):
    return lax.dot_general(a, b, (((0,), (0,)), ((), ())), preferred_element_type=F32)


def _rowmean(v):
    return jnp.mean(v, axis=-1, keepdims=True)


def _colsum(v):
    return jnp.sum(v, axis=0, keepdims=True)


def _sigmoid(v):
    return jax.nn.sigmoid(v)


_HBM = pl.BlockSpec(memory_space=pltpu.HBM)
_SEM = pl.BlockSpec(memory_space=pltpu.SEMAPHORE)
_EFFECT = pltpu.SideEffectType.DATAFLOW_SIDE_EFFECTING


def _peer_of(k):
    x, y, c = lax.axis_index("x"), lax.axis_index("y"), lax.axis_index("c")
    px = 1 - x if k & 4 else x
    py = 1 - y if k & 2 else y
    pc = 1 - c if k & 1 else c
    return (px, py, pc), 4 * px + 2 * py + pc


def _split_copies(kinds, ins, lands, send_sems, recv_sems, started):
    me = 4 * lax.axis_index("x") + 2 * lax.axis_index("y") + lax.axis_index("c")
    out = []
    for i, kind in enumerate(kinds):
        for k in range(1, NDEV):
            dev, pid = _peer_of(k)
            src = ins[i] if kind == "gather" else ins[i].at[pid]
            dst = lands[i].at[me] if started else lands[i].at[pid]
            slot = i * (NDEV - 1) + k - 1
            out.append(pltpu.make_async_remote_copy(
                src_ref=src, dst_ref=dst, send_sem=send_sems.at[slot], recv_sem=recv_sems.at[slot],
                device_id=dev, device_id_type=pl.DeviceIdType.MESH))
    return out


def _exchange_start(items, name):
    n = len(items)
    kinds = [k for _, k in items]
    srcs = [pltpu.with_memory_space_constraint(a, pltpu.HBM) for a, _ in items]
    lands = []
    for a, k in items:
        shp = (NDEV,) + tuple(a.shape) if k == "gather" else tuple(a.shape)
        lands.append(pltpu.with_memory_space_constraint(lax.empty(shp, a.dtype), pltpu.HBM))

    def body(*refs):
        ins, land_refs = refs[:n], refs[n:2 * n]
        send_sems, recv_sems, own_sems = refs[2 * n:2 * n + 3]
        token = refs[-1]
        for cp in _own_copies(kinds, ins, land_refs, own_sems):
            cp.start()
        for cp in _split_copies(kinds, ins, land_refs, send_sems, recv_sems, True):
            cp.start()
        token[...] = jnp.zeros_like(token)

    sems = pltpu.SemaphoreType.DMA((n * (NDEV - 1),))
    res = pl.pallas_call(
        body, name=name,
        out_shape=(sems, sems, pltpu.SemaphoreType.DMA((n,)),
                   *[pltpu.HBM(a.shape, a.dtype) for a in srcs + lands], SDS((8, 128), F32)),
        in_specs=[_HBM] * (2 * n),
        out_specs=(_SEM, _SEM, _SEM, *[_HBM] * (2 * n), pl.BlockSpec(memory_space=pltpu.VMEM)),
        input_output_aliases={i: 3 + i for i in range(2 * n)},
        compiler_params=pltpu.CompilerParams(has_side_effects=_EFFECT),
    )(*srcs, *lands)
    return (kinds, res[0], res[1], res[2], list(res[3:3 + n]), list(res[3 + n:3 + 2 * n])), res[-1][0, 0]


def _own_copies(kinds, ins, lands, own_sems):
    me = 4 * lax.axis_index("x") + 2 * lax.axis_index("y") + lax.axis_index("c")
    return [pltpu.make_async_copy(ins[i] if kind == "gather" else ins[i].at[me], lands[i].at[me], own_sems.at[i])
            for i, kind in enumerate(kinds)]


def _exchange_wait(state, after, name, only=None):
    kinds, send_sems, recv_sems, own_sems, srcs, lands = state
    n = len(kinds)
    chosen = range(n) if only is None else only

    def body(*refs):
        ins, land_refs = refs[:n], refs[n:2 * n]
        s_sems, r_sems, o_sems = refs[2 * n:2 * n + 3]
        remote = _split_copies(kinds, ins, land_refs, s_sems, r_sems, False)
        own = _own_copies(kinds, ins, land_refs, o_sems)
        for i in chosen:
            for cp in remote[i * (NDEV - 1):(i + 1) * (NDEV - 1)]:
                cp.wait_send()
                cp.wait_recv()
        for i in chosen:
            own[i].wait()

    res = pl.pallas_call(
        body, name=name,
        out_shape=tuple(pltpu.HBM(a.shape, a.dtype) for a in srcs + lands),
        in_specs=[_HBM] * (2 * n) + [_SEM, _SEM, _SEM, pl.BlockSpec(memory_space=pl.ANY)],
        out_specs=tuple([_HBM] * (2 * n)),
        input_output_aliases={i: i for i in range(2 * n)},
        compiler_params=pltpu.CompilerParams(has_side_effects=_EFFECT),
    )(*srcs, *lands, send_sems, recv_sems, own_sems, after)
    srcs[:], lands[:] = res[:n], res[n:]
    return list(lands)


def _cast_x(x2, after):
    def body(x_ref, after_ref, o_ref, done_ref):
        o_ref[...] = x_ref[...].astype(BF16)
        done_ref[...] = jnp.zeros_like(done_ref)

    return pl.pallas_call(
        body, name="cast_x", grid=(T // TM,),
        in_specs=[pl.BlockSpec((TM, D), lambda m: (m, 0)), pl.BlockSpec(memory_space=pl.ANY)],
        out_specs=(pl.BlockSpec((TM, D), lambda m: (m, 0)), pl.BlockSpec((8, 128), lambda m: (0, 0))),
        out_shape=(SDS((T, D), BF16), SDS((8, 128), F32)),
    )(x2, after)


def _proj_in(xb, w_in_t, b_in):
    nq = 3 * AW

    def body(x_ref, w_ref, b_ref, qkv_ref, ag_ref):
        xb = x_ref[...]
        qkv_ref[...] = (_dot_nt(xb, w_ref[pl.ds(0, nq), :]) + b_ref[:, :nq]).astype(BF16)
        ag_ref[...] = _dot_nt(xb, w_ref[pl.ds(nq, 2 * CW), :]) + b_ref[:, nq:]

    return pl.pallas_call(
        body, name="proj_in", grid=(T // TM,),
        in_specs=[pl.BlockSpec((TM, D), lambda m: (m, 0)), pl.BlockSpec((INW, D), lambda m: (0, 0)),
                  pl.BlockSpec((1, INW), lambda m: (0, 0))],
        out_specs=(pl.BlockSpec((TM, nq), lambda m: (m, 0)), pl.BlockSpec((TM, 2 * CW), lambda m: (m, 0))),
        out_shape=(SDS((T, nq), BF16), SDS((T, 2 * CW), F32)),
    )(xb, w_in_t, b_in)


def _grad_x(pieces, w_in_t, dz1, zero):
    widths = [p.shape[1] for p in pieces]

    def body(*refs):
        p_refs = refs[:len(pieces)]
        w_ref, dz_ref, z_ref, o_ref = refs[len(pieces):]
        acc = ALPHA * dz_ref[...] + z_ref[...]
        r0 = 0
        for p_ref, wd in zip(p_refs, widths):
            acc = acc + _dot(p_ref[...], w_ref[pl.ds(r0, wd), :])
            r0 += wd
        o_ref[...] = acc

    row = pl.BlockSpec((TM, D), lambda m: (m, 0))
    return pl.pallas_call(
        body, name="grad_x", grid=(T // TM,),
        in_specs=[pl.BlockSpec((TM, wd), lambda m: (m, 0)) for wd in widths]
        + [pl.BlockSpec((INW, D), lambda m: (0, 0)), row, pl.BlockSpec((1, 1), lambda m: (0, 0))],
        out_specs=row,
        out_shape=SDS((T, D), F32),
    )(*pieces, w_in_t, dz1, zero)


def _grad_w_in(pieces, x2):
    widths = [p.shape[1] for p in pieces]
    tk = 512
    nk = T // tk

    def body(*refs):
        p_refs = refs[:len(pieces)]
        x_ref, o_ref, acc = refs[len(pieces):]
        k = pl.program_id(0)

        @pl.when(k == 0)
        def _():
            acc[...] = jnp.zeros_like(acc)

        xb = x_ref[...].astype(BF16)
        r0 = 0
        for p_ref, wd in zip(p_refs, widths):
            acc[pl.ds(r0, wd), :] += _dot_tn(p_ref[...], xb)
            r0 += wd

        @pl.when(k == nk - 1)
        def _():
            o_ref[...] = acc[...].astype(o_ref.dtype)

    return pl.pallas_call(
        body, name="grad_w_in", grid=(nk,),
        in_specs=[pl.BlockSpec((tk, wd), lambda k: (k, 0)) for wd in widths] + [pl.BlockSpec((tk, D), lambda k: (k, 0))],
        out_specs=pl.BlockSpec((INW, D), lambda k: (0, 0)),
        out_shape=SDS((INW, D), GRAD_WIRE),
        scratch_shapes=[pltpu.VMEM((INW, D), F32)],
    )(*pieces, x2)


def _mm_tn(a, b, tn, tk, name):
    t_, na = a.shape
    nb = b.shape[1]
    nk = t_ // tk

    def body(a_ref, b_ref, o_ref, acc):
        k = pl.program_id(1)

        @pl.when(k == 0)
        def _():
            acc[...] = jnp.zeros_like(acc)

        acc[...] += _dot_tn(a_ref[...].astype(BF16), b_ref[...].astype(BF16))

        @pl.when(k == nk - 1)
        def _():
            o_ref[...] = acc[...].astype(o_ref.dtype)

    return pl.pallas_call(
        body, name=name, grid=(na // tn, nk),
        in_specs=[pl.BlockSpec((tk, tn), lambda n, k: (k, n)),
                  pl.BlockSpec((tk, nb), lambda n, k: (k, 0))],
        out_specs=pl.BlockSpec((tn, nb), lambda n, k: (n, 0)),
        out_shape=SDS((na, nb), GRAD_WIRE),
        scratch_shapes=[pltpu.VMEM((tn, nb), F32)],
    )(a, b)


def _bucket_maps():
    qi = np.arange(BLK)[:, None]
    kj = np.arange(2 * BLK)[None, :]
    steps = np.maximum(qi + BLK - kj, 0)
    exact = NBUCKET // 2
    maps = []
    for _, dil in BRANCHES:
        dist = steps * dil
        d_f = np.maximum(dist, 1).astype(np.float32)
        large = exact + (np.log(d_f / np.float32(exact)) / np.float32(math.log(S / exact))
                         * np.float32(NBUCKET - exact)).astype(np.int32)
        large = np.minimum(large, NBUCKET - 1)
        maps.append(np.where(dist < exact, dist, large).astype(np.int32))
    return np.stack(maps)


def _bias_table(rel_table, buckets):
    def body(t_ref, b_ref, o_ref):
        bk = b_ref[0]
        for h in range(NH):
            acc = jnp.zeros((BLK, 2 * BLK), F32)
            for k in range(NBUCKET):
                acc = jnp.where(bk == k, t_ref[k, h], acc)
            o_ref[0, h] = acc

    return pl.pallas_call(
        body, name="bias_table", grid=(len(BRANCHES),),
        in_specs=[pl.BlockSpec(memory_space=pltpu.SMEM),
                  pl.BlockSpec((1, BLK, 2 * BLK), lambda i: (i, 0, 0))],
        out_specs=pl.BlockSpec((1, NH, BLK, 2 * BLK), lambda i: (i, 0, 0, 0)),
        out_shape=SDS((len(BRANCHES), NH, BLK, 2 * BLK), F32),
    )(rel_table, buckets)


def _rel_table_grad(dbias, buckets, after):
    def body(d_ref, b_ref, after_ref, o_ref):
        h = pl.program_id(0)
        for k in range(NBUCKET):
            tot = jnp.zeros((1, 1), F32)
            for br in range(len(BRANCHES)):
                sel = jnp.where(b_ref[br] == k, d_ref[br, 0], 0.0)
                tot = tot + jnp.sum(jnp.sum(sel, axis=1, keepdims=True), axis=0, keepdims=True)
            o_ref[0, :, pl.ds(k, 1)] = tot

    out = pl.pallas_call(
        body, name="rel_table_grad", grid=(NH,),
        in_specs=[pl.BlockSpec((len(BRANCHES), 1, BLK, 2 * BLK), lambda h: (0, h, 0, 0)),
                  pl.BlockSpec((len(BRANCHES), BLK, 2 * BLK), lambda h: (0, 0, 0)),
                  pl.BlockSpec(memory_space=pl.ANY)],
        out_specs=pl.BlockSpec((1, 1, NBUCKET), lambda h: (h, 0, 0)),
        out_shape=SDS((NH, 1, NBUCKET), F32),
    )(dbias, buckets, after)
    return out.reshape(NH, NBUCKET).T


PADK = BLK
SCALE = 1.0 / math.sqrt(HD)
ATTN_UNROLL = 16


def _branch_geometry(br):
    dil = BRANCHES[br][1]
    sub = S // dil
    return dil, sub, sub // BLK


def _token_rows(br, i):
    dil, _, nblk = _branch_geometry(br)
    if dil == 1:
        return pl.ds(pl.multiple_of(i * BLK, BLK), BLK), i
    r = lax.shift_right_logical(i, nblk.bit_length() - 1)
    n = lax.bitwise_and(i, nblk - 1)
    return pl.ds(r + dil * BLK * n, BLK, stride=dil), n


def _sub_layout_loop(br, step):
    dil, sub, _ = _branch_geometry(br)
    rows = min(sub, 256)
    nchunk = sub // rows

    def it_step(it, carry):
        if dil == 1:
            src = pl.ds(pl.multiple_of(it * rows, rows), rows)
        else:
            r = lax.shift_right_logical(it, nchunk.bit_length() - 1)
            src = pl.ds(r + dil * rows * lax.bitwise_and(it, nchunk - 1), rows, stride=dil)
        step(src, pl.multiple_of(it * rows, BLK), rows)
        return carry

    lax.fori_loop(0, dil * nchunk, it_step, 0)


def _masked_bias(bias_ref, bm):
    qi = lax.broadcasted_iota(jnp.int32, (BLK, 2 * BLK), 0)
    kj = lax.broadcasted_iota(jnp.int32, (BLK, 2 * BLK), 1)
    first = jnp.logical_and(kj >= BLK, kj - BLK <= qi)
    valid = jnp.logical_or(first, jnp.logical_and(kj < BLK, kj >= qi))
    for br in range(len(BRANCHES)):
        for j in range(2):
            b = bias_ref[br, j]
            bm[br, 1, pl.ds(j * BLK, BLK), :] = jnp.where(valid, b, NEG_INF)
            bm[br, 0, pl.ds(j * BLK, BLK), :] = jnp.where(first, b, NEG_INF)


def _head_split(fn):
    def split(t):
        h0 = lax.broadcasted_iota(jnp.int32, t.shape, 1) < HD
        t = fn(t)
        return jnp.where(h0, t, 0.0).astype(BF16), jnp.where(h0, 0.0, t).astype(BF16)
    return split


def _attn_fwd(qkv, bias):
    nbr = len(BRANCHES)

    def body(q_ref, k_ref, v_ref, bias_ref, o_ref, lse_ref, qf, kf, vf, qs0, qs1, ks, vs, bm, ob, mb, lb):
        qf[...] = q_ref[...].astype(F32)
        kf[...] = k_ref[...].astype(F32)
        vf[...] = v_ref[...].astype(F32)
        _masked_bias(bias_ref, bm)
        ks[pl.ds(0, PADK), :] = jnp.zeros((PADK, BLK), BF16)
        vs[pl.ds(0, PADK), :] = jnp.zeros((PADK, BLK), BF16)
        head0 = lax.broadcasted_iota(jnp.int32, (BLK, BLK), 1) < HD
        split_q = _head_split(lambda t: t * SCALE)

        for br in range(nbr):
            nblk = _branch_geometry(br)[2]

            def stage(src, off, rows):
                qs0[pl.ds(off, rows), :], qs1[pl.ds(off, rows), :] = split_q(qf[src, :])
                ks[pl.ds(PADK + off, rows), :] = kf[src, :].astype(BF16)
                vs[pl.ds(PADK + off, rows), :] = vf[src, :].astype(BF16)

            _sub_layout_loop(br, stage)

            def blk(i, carry, br=br, nblk=nblk):
                base = pl.multiple_of(i * BLK, BLK)
                rows, n = _token_rows(br, i)
                q01 = jnp.concatenate([qs0[pl.ds(base, BLK), :], qs1[pl.ds(base, BLK), :]], axis=0)
                if nblk > 1:
                    kcat = ks[pl.ds(base, 2 * BLK), :]
                    vcat = vs[pl.ds(base, 2 * BLK), :]
                    s = _dot_nt(q01, kcat) + bm[br, jnp.minimum(n, 1)]
                else:
                    kcat = ks[pl.ds(PADK + base, BLK), :]
                    vcat = vs[pl.ds(PADK + base, BLK), :]
                    s = _dot_nt(q01, kcat) + bm[br, 0, :, BLK:]
                mx = jnp.max(s, axis=-1, keepdims=True)
                p = jnp.exp(s - mx)
                ls = jnp.sum(p, axis=-1, keepdims=True)
                o = _dot(p.astype(BF16), vcat)
                ob[br, rows, :] = jnp.where(head0, o[:BLK], o[BLK:])
                mb[br, rows, :] = jnp.where(head0, mx[:BLK], mx[BLK:])
                lb[br, rows, :] = jnp.where(head0, ls[:BLK], ls[BLK:])
                return carry

            lax.fori_loop(0, 16, blk, 0, unroll=ATTN_UNROLL)

        def merge(i, carry):
            rows = pl.ds(pl.multiple_of(i * 256, 256), 256)
            m_all = jnp.maximum(jnp.maximum(mb[0, rows, :], mb[1, rows, :]), mb[2, rows, :])
            num = jnp.zeros((256, BLK), F32)
            den = jnp.zeros((256, BLK), F32)
            for br in range(nbr):
                c = jnp.exp(mb[br, rows, :] - m_all)
                num = num + ob[br, rows, :] * c
                den = den + lb[br, rows, :] * c
            o_ref[rows, :] = num / den
            lse_ref[rows, :] = m_all + jnp.log(den)
            return carry

        lax.fori_loop(0, S // 256, merge, 0)

    npair = NH // 2
    blk_spec = lambda off: pl.BlockSpec((S, BLK), lambda b, hp: (b, off + hp))
    return pl.pallas_call(
        body, name="attn_fwd", grid=(BL, npair),
        in_specs=[blk_spec(0), blk_spec(npair), blk_spec(2 * npair),
                  pl.BlockSpec((nbr, 2, BLK, 2 * BLK), lambda b, hp: (0, hp, 0, 0))],
        out_specs=(blk_spec(0), blk_spec(0)),
        out_shape=(SDS((T, AW), F32), SDS((T, AW), F32)),
        scratch_shapes=[pltpu.VMEM((S, BLK), F32)] * 3 + [pltpu.VMEM((S, BLK), BF16)] * 2
        + [pltpu.VMEM((PADK + S, BLK), BF16)] * 2 + [pltpu.VMEM((nbr, 2, 2 * BLK, 2 * BLK), F32)]
        + [pltpu.VMEM((nbr, S, BLK), F32)] * 3,
    )(qkv, qkv, qkv, bias)


def _attn_bwd(qkv, attn, lse, dattn, bias):
    nbr = len(BRANCHES)

    def body(q_ref, k_ref, v_ref, o_ref, lse_ref, do_ref, bias_ref,
             dq_ref, dk_ref, dv_ref, sq_ref, sk_ref, sv_ref, db_ref,
             qf, kf, vf, dl, dqa, dka, dva, qs0, qs1, ds0, ds1, ks, vs, dks, dvs, bm):
        b = pl.program_id(1)
        qf[...] = q_ref[...].astype(F32)
        kf[...] = k_ref[...].astype(F32)
        vf[...] = v_ref[...].astype(F32)
        dqa[...] = jnp.zeros_like(dqa)
        dka[...] = jnp.zeros_like(dka)
        dva[...] = jnp.zeros_like(dva)
        _masked_bias(bias_ref, bm)
        ks[pl.ds(0, PADK), :] = jnp.zeros((PADK, BLK), BF16)
        vs[pl.ds(0, PADK), :] = jnp.zeros((PADK, BLK), BF16)
        head0 = lax.broadcasted_iota(jnp.int32, (BLK, BLK), 1) < HD
        split_q = _head_split(lambda t: t * SCALE)
        split_do = _head_split(lambda t: t)

        @pl.when(b == 0)
        def _():
            db_ref[...] = jnp.zeros_like(db_ref)
            sq_ref[...] = jnp.zeros_like(sq_ref)
            sk_ref[...] = jnp.zeros_like(sk_ref)
            sv_ref[...] = jnp.zeros_like(sv_ref)

        def delta(i, carry):
            rows = pl.ds(pl.multiple_of(i * 256, 256), 256)
            prod = do_ref[rows, :] * o_ref[rows, :]
            h0 = lax.broadcasted_iota(jnp.int32, (256, BLK), 1) < HD
            d0 = jnp.sum(jnp.where(h0, prod, 0.0), axis=-1, keepdims=True)
            d1 = jnp.sum(jnp.where(h0, 0.0, prod), axis=-1, keepdims=True)
            dl[rows, :] = jnp.where(h0, d0, d1)
            return carry

        lax.fori_loop(0, S // 256, delta, 0)

        for br in range(nbr):
            nblk = _branch_geometry(br)[2]

            def stage(src, off, rows):
                qs0[pl.ds(off, rows), :], qs1[pl.ds(off, rows), :] = split_q(qf[src, :])
                ds0[pl.ds(off, rows), :], ds1[pl.ds(off, rows), :] = split_do(do_ref[src, :])
                ks[pl.ds(PADK + off, rows), :] = kf[src, :].astype(BF16)
                vs[pl.ds(PADK + off, rows), :] = vf[src, :].astype(BF16)

            _sub_layout_loop(br, stage)
            dks[...] = jnp.zeros_like(dks)
            dvs[...] = jnp.zeros_like(dvs)

            def blk(i, carry, br=br, nblk=nblk):
                base = pl.multiple_of(i * BLK, BLK)
                rows, n = _token_rows(br, i)
                q01 = jnp.concatenate([qs0[pl.ds(base, BLK), :], qs1[pl.ds(base, BLK), :]], axis=0)
                do01 = jnp.concatenate([ds0[pl.ds(base, BLK), :], ds1[pl.ds(base, BLK), :]], axis=0)
                lse_b = lse_ref[rows, :]
                dl_b = dl[rows, :]
                lse01 = jnp.concatenate([lse_b[:, 0:1], lse_b[:, HD:HD + 1]], axis=0)
                dl01 = jnp.concatenate([dl_b[:, 0:1], dl_b[:, HD:HD + 1]], axis=0)
                if nblk > 1:
                    krows = pl.ds(base, 2 * BLK)
                    bias_m = bm[br, jnp.minimum(n, 1)]
                else:
                    krows = pl.ds(PADK + base, BLK)
                    bias_m = bm[br, 0, :, BLK:]
                kcat = ks[krows, :]
                vcat = vs[krows, :]
                p = jnp.exp(_dot_nt(q01, kcat) + bias_m - lse01)
                dsv = p * (_dot_nt(do01, vcat) - dl01)
                if nblk > 1:
                    db_ref[br, 0] += dsv[:BLK]
                    db_ref[br, 1] += dsv[BLK:]
                else:
                    db_ref[br, 0, :, BLK:] += dsv[:BLK]
                    db_ref[br, 1, :, BLK:] += dsv[BLK:]
                dsb = dsv.astype(BF16)
                dq01 = _dot(dsb, kcat)
                dqa[rows, :] = dqa[rows, :] + jnp.where(head0, dq01[:BLK], dq01[BLK:])
                dks[krows, :] = dks[krows, :] + _dot_tn(dsb, q01)
                dvs[krows, :] = dvs[krows, :] + _dot_tn(p.astype(BF16), do01)
                return carry

            lax.fori_loop(0, 16, blk, 0, unroll=ATTN_UNROLL)

            def fold(src, off, rows):
                dka[src, :] = dka[src, :] + dks[pl.ds(PADK + off, rows), :]
                dva[src, :] = dva[src, :] + dvs[pl.ds(PADK + off, rows), :]

            _sub_layout_loop(br, fold)

        def flush(i, carry):
            rows = pl.ds(pl.multiple_of(i * 256, 256), 256)
            for acc, out, cs, mul in ((dqa, dq_ref, sq_ref, SCALE), (dka, dk_ref, sk_ref, 1.0), (dva, dv_ref, sv_ref, 1.0)):
                val = acc[rows, :] * mul
                out[rows, :] = val.astype(BF16)
                cs[...] += _colsum(val)
            return carry

        lax.fori_loop(0, S // 256, flush, 0)

    npair = NH // 2
    blk_spec = lambda off: pl.BlockSpec((S, BLK), lambda hp, b: (b, off + hp))
    sum_spec = pl.BlockSpec((1, BLK), lambda hp, b: (0, hp))
    return pl.pallas_call(
        body, name="attn_bwd", grid=(npair, BL),
        in_specs=[blk_spec(0), blk_spec(npair), blk_spec(2 * npair), blk_spec(0), blk_spec(0), blk_spec(0),
                  pl.BlockSpec((nbr, 2, BLK, 2 * BLK), lambda hp, b: (0, hp, 0, 0))],
        out_specs=(blk_spec(0), blk_spec(0), blk_spec(0), sum_spec, sum_spec, sum_spec,
                   pl.BlockSpec((nbr, 2, BLK, 2 * BLK), lambda hp, b: (0, hp, 0, 0))),
        out_shape=(SDS((T, AW), BF16), SDS((T, AW), BF16), SDS((T, AW), BF16),
                   SDS((1, AW), F32), SDS((1, AW), F32), SDS((1, AW), F32),
                   SDS((nbr, NH, BLK, 2 * BLK), F32)),
        scratch_shapes=[pltpu.VMEM((S, BLK), F32)] * 7 + [pltpu.VMEM((S, BLK), BF16)] * 4
        + [pltpu.VMEM((PADK + S, BLK), BF16)] * 2 + [pltpu.VMEM((PADK + S, BLK), F32)] * 2
        + [pltpu.VMEM((nbr, 2, 2 * BLK, 2 * BLK), F32)],
    )(qkv, qkv, qkv, attn, lse, dattn, bias)


CH = 256
PADR = 32


def _tap_phases(offset_of_tap):
    taps = sorted((offset_of_tap(k) % 8, offset_of_tap(k) - offset_of_tap(k) % 8, k) for k in range(CK))
    assert all(lo + CH + ph <= CH + PADR for ph, lo, _ in taps)
    return taps


def _rows_up(win):
    made = {0: win}

    def get(phase):
        if phase not in made:
            made[phase] = pltpu.roll(win, win.shape[0] - phase, 0)
        return made[phase]
    return get


def _conv_fwd(ag, conv_w, conv_b):
    def body(ag_ref, w_ref, b_ref, u1_ref, u0p):
        u0p[pl.ds(0, PADR), :] = jnp.zeros((PADR, CW), F32)

        def glu(i, carry):
            t0 = pl.multiple_of(i * CH, CH)
            a = ag_ref[pl.ds(t0, CH), :CW]
            g = ag_ref[pl.ds(t0, CH), CW:]
            u0p[pl.ds(PADR + t0, CH), :] = a * _sigmoid(g)
            return carry

        lax.fori_loop(0, S // CH, glu, 0)

        def conv(i, carry):
            t0 = pl.multiple_of(i * CH, CH)
            win = u0p[pl.ds(t0, CH + PADR), :]
            acc = jnp.zeros((CH, CW), F32) + b_ref[...]
            up = _rows_up(win)
            for phase, lo, k in _tap_phases(lambda k: PADR - (CK - 1) + k):
                acc = acc + up(phase)[lo:lo + CH, :] * w_ref[k:k + 1, :]
            u1_ref[pl.ds(t0, CH), :] = acc
            return carry

        lax.fori_loop(0, S // CH, conv, 0)

    return pl.pallas_call(
        body, name="conv_fwd", grid=(BL,),
        in_specs=[pl.BlockSpec((S, 2 * CW), lambda b: (b, 0)),
                  pl.BlockSpec((CK, CW), lambda b: (0, 0)),
                  pl.BlockSpec((1, CW), lambda b: (0, 0))],
        out_specs=pl.BlockSpec((S, CW), lambda b: (b, 0)),
        out_shape=SDS((T, CW), F32),
        scratch_shapes=[pltpu.VMEM((S + PADR, CW), F32)],
    )(ag, conv_w, conv_b)


def _conv_post(u1, cg, cb):
    mu = _rowmean(u1)
    uc = u1 - mu
    rstd = lax.rsqrt(_rowmean(uc * uc) + LN_EPS)
    xh = uc * rstd
    u2 = xh * cg + cb
    sg = _sigmoid(u2)
    return xh, rstd, u2, sg, u2 * sg


def _mix_fwd(attn, u1, ga, gc, cg, cb):
    def body(a_ref, u_ref, ga_ref, gc_ref, cg_ref, cb_ref, o_ref):
        a = a_ref[...]
        ra = lax.rsqrt(_rowmean(a * a) + LN_EPS)
        o_ref[:, :AW] = (a * ra * ga_ref[...]).astype(BF16)
        _, _, _, _, u3 = _conv_post(u_ref[...], cg_ref[...], cb_ref[...])
        rc = lax.rsqrt(_rowmean(u3 * u3) + LN_EPS)
        o_ref[:, AW:] = (u3 * rc * gc_ref[...]).astype(BF16)

    vec = lambda w: pl.BlockSpec((1, w), lambda m: (0, 0))
    return pl.pallas_call(
        body, name="mix_fwd", grid=(T // TM,),
        in_specs=[pl.BlockSpec((TM, AW), lambda m: (m, 0)), pl.BlockSpec((TM, CW), lambda m: (m, 0)),
                  vec(AW), vec(CW), vec(CW), vec(CW)],
        out_specs=pl.BlockSpec((TM, D), lambda m: (m, 0)),
        out_shape=SDS((T, D), BF16),
    )(attn, u1, ga, gc, cg, cb)


def _mix_bwd(dz1, w_out, attn, u1, ga, gc, cg, cb):
    def body(dz_ref, w_ref, a_ref, u_ref, ga_ref, gc_ref, cg_ref, cb_ref,
             da_ref, du_ref, g_an, g_cn, g_lg, g_lb, g_cb):
        @pl.when(pl.program_id(0) == 0)
        def _():
            for r in (g_an, g_cn, g_lg, g_lb, g_cb):
                r[...] = jnp.zeros_like(r)

        dm = _dot_nt(dz_ref[...].astype(BF16), w_ref[...])
        a = a_ref[...]
        dna = dm[:, :AW]
        ra = lax.rsqrt(_rowmean(a * a) + LN_EPS)
        g_an[...] += _colsum(dna * a * ra)
        dat = dna * ga_ref[...]
        da_ref[...] = ra * dat - a * (ra * ra * ra) * _rowmean(dat * a)

        xh, rstd, u2, sg, u3 = _conv_post(u_ref[...], cg_ref[...], cb_ref[...])
        dnc = dm[:, AW:]
        rc = lax.rsqrt(_rowmean(u3 * u3) + LN_EPS)
        g_cn[...] += _colsum(dnc * u3 * rc)
        dut = dnc * gc_ref[...]
        du3 = rc * dut - u3 * (rc * rc * rc) * _rowmean(dut * u3)
        du2 = du3 * sg * (1.0 + u2 * (1.0 - sg))
        g_lg[...] += _colsum(du2 * xh)
        g_lb[...] += _colsum(du2)
        dxh = du2 * cg_ref[...]
        du1 = rstd * (dxh - _rowmean(dxh) - xh * _rowmean(dxh * xh))
        g_cb[...] += _colsum(du1)
        du_ref[...] = du1

    vec = lambda w: pl.BlockSpec((1, w), lambda m: (0, 0))
    return pl.pallas_call(
        body, name="mix_bwd", grid=(T // TM,),
        in_specs=[pl.BlockSpec((TM, D), lambda m: (m, 0)), pl.BlockSpec((D, D), lambda m: (0, 0)),
                  pl.BlockSpec((TM, AW), lambda m: (m, 0)),
                  pl.BlockSpec((TM, CW), lambda m: (m, 0)), vec(AW), vec(CW), vec(CW), vec(CW)],
        out_specs=(pl.BlockSpec((TM, AW), lambda m: (m, 0)), pl.BlockSpec((TM, CW), lambda m: (m, 0)),
                   vec(AW), vec(CW), vec(CW), vec(CW), vec(CW)),
        out_shape=(SDS((T, AW), F32), SDS((T, CW), F32),
                   SDS((1, AW), F32), SDS((1, CW), F32), SDS((1, CW), F32), SDS((1, CW), F32), SDS((1, CW), F32)),
    )(dz1, w_out, attn, u1, ga, gc, cg, cb)


def _conv_bwd(du1, ag, conv_w):
    def body(du_ref, ag_ref, w_ref, dag_ref, cs_ref, gw_ref, u0p, dup):
        @pl.when(pl.program_id(0) == 0)
        def _():
            cs_ref[...] = jnp.zeros_like(cs_ref)
            gw_ref[...] = jnp.zeros_like(gw_ref)

        u0p[pl.ds(0, PADR), :] = jnp.zeros((PADR, CW), F32)
        dup[pl.ds(S, PADR), :] = jnp.zeros((PADR, CW), F32)

        def fill(i, carry):
            t0 = pl.multiple_of(i * CH, CH)
            a = ag_ref[pl.ds(t0, CH), :CW]
            g = ag_ref[pl.ds(t0, CH), CW:]
            u0p[pl.ds(PADR + t0, CH), :] = a * _sigmoid(g)
            dup[pl.ds(t0, CH), :] = du_ref[pl.ds(t0, CH), :]
            return carry

        lax.fori_loop(0, S // CH, fill, 0)

        def chunk(i, carry):
            t0 = pl.multiple_of(i * CH, CH)
            d = dup[pl.ds(t0, CH), :]
            win_u = u0p[pl.ds(t0, CH + PADR), :]
            win_d = dup[pl.ds(t0, CH + PADR), :]
            du0 = jnp.zeros((CH, CW), F32)
            up_u = _rows_up(win_u)
            for phase, lo, k in _tap_phases(lambda k: PADR - (CK - 1) + k):
                gw_ref[k:k + 1, :] += _colsum(d * up_u(phase)[lo:lo + CH, :])
            up_d = _rows_up(win_d)
            for phase, lo, k in _tap_phases(lambda k: CK - 1 - k):
                du0 = du0 + up_d(phase)[lo:lo + CH, :] * w_ref[k:k + 1, :]
            a = ag_ref[pl.ds(t0, CH), :CW]
            sg = _sigmoid(ag_ref[pl.ds(t0, CH), CW:])
            da = du0 * sg
            dg = du0 * a * sg * (1.0 - sg)
            dag_ref[pl.ds(t0, CH), :CW] = da.astype(BF16)
            dag_ref[pl.ds(t0, CH), CW:] = dg.astype(BF16)
            cs_ref[:, :CW] += _colsum(da)
            cs_ref[:, CW:] += _colsum(dg)
            return carry

        lax.fori_loop(0, S // CH, chunk, 0)

    return pl.pallas_call(
        body, name="conv_bwd", grid=(BL,),
        in_specs=[pl.BlockSpec((S, CW), lambda b: (b, 0)), pl.BlockSpec((S, 2 * CW), lambda b: (b, 0)),
                  pl.BlockSpec((CK, CW), lambda b: (0, 0))],
        out_specs=(pl.BlockSpec((S, 2 * CW), lambda b: (b, 0)),
                   pl.BlockSpec((1, 2 * CW), lambda b: (0, 0)),
                   pl.BlockSpec((PADR, CW), lambda b: (0, 0))),
        out_shape=(SDS((T, 2 * CW), BF16), SDS((1, 2 * CW), F32), SDS((PADR, CW), F32)),
        scratch_shapes=[pltpu.VMEM((S + PADR, CW), F32), pltpu.VMEM((S + PADR, CW), F32)],
    )(du1, ag, conv_w)


def _layer_norm_fwd(z):
    mu = _rowmean(z)
    zc = z - mu
    rstd = lax.rsqrt(_rowmean(zc * zc) + LN_EPS)
    return zc * rstd, rstd


def _layer_norm_bwd(dy, xh, rstd, g):
    dxh = dy * g
    return rstd * (dxh - _rowmean(dxh) - xh * _rowmean(dxh * xh))


def _out_proj_ln1(mixed, w_out, x2, g1, b1):
    def body(a_ref, w_ref, x_ref, g_ref, b_ref, xh_ref, rstd_ref, x1_ref):
        z = ALPHA * x_ref[...] + _dot(a_ref[...], w_ref[...])
        xh, rstd = _layer_norm_fwd(z)
        xh_ref[...] = xh
        rstd_ref[...] = rstd
        x1_ref[...] = (xh * g_ref[...] + b_ref[...]).astype(BF16)

    vec = pl.BlockSpec((1, D), lambda m: (0, 0))
    row = pl.BlockSpec((TM, D), lambda m: (m, 0))
    return pl.pallas_call(
        body, name="out_proj_ln1", grid=(T // TM,),
        in_specs=[row, pl.BlockSpec((D, D), lambda m: (0, 0)), row, vec, vec],
        out_specs=(row, pl.BlockSpec((TM, 1), lambda m: (m, 0)), row),
        out_shape=(SDS((T, D), F32), SDS((T, 1), F32), SDS((T, D), BF16)),
    )(mixed, w_out, x2, g1, b1)


def _seq_start(m):
    return lax.bitwise_and(m, S // TMF - 1) == 0


def _shift_down(x, before, k):
    rolled = pltpu.roll(x, k, 0)
    row = lax.broadcasted_iota(jnp.int32, before.shape, 0)
    head = jnp.where(row < k, pltpu.roll(before, k, 0), rolled[:8])
    return jnp.concatenate([head, rolled[8:]], axis=0)


def _shift_up(x, after, k):
    n = x.shape[0]
    rolled = pltpu.roll(x, n - k, 0)
    row = lax.broadcasted_iota(jnp.int32, after.shape, 0)
    tail = jnp.where(row >= 8 - k, pltpu.roll(after, 8 - k, 0), rolled[n - 8:])
    return jnp.concatenate([rolled[:n - 8], tail], axis=0)


def _ffn_up(x1b, w_up, fcw, fcb):
    def body(x_ref, wg_ref, wv_ref, cwg_ref, cwv_ref, cbg_ref, cbv_ref, up_ref, gv_ref, act_ref, prev_g, prev_v):
        @pl.when(_seq_start(pl.program_id(1)))
        def _():
            prev_g[...] = jnp.zeros_like(prev_g)
            prev_v[...] = jnp.zeros_like(prev_v)

        x = x_ref[...]
        outs = []
        for w_ref, cw_ref, cb_ref, prev, lo in ((wg_ref, cwg_ref, cbg_ref, prev_g, 0), (wv_ref, cwv_ref, cbv_ref, prev_v, FT)):
            u = _dot_nt(x, w_ref[...])
            up_ref[:, lo:lo + FT] = u.astype(BF16)
            before = prev[...]
            y = (cw_ref[2:3, :] * u + cw_ref[1:2, :] * _shift_down(u, before, 1)
                 + cw_ref[0:1, :] * _shift_down(u, before, 2) + cb_ref[...])
            prev[...] = u[TMF - 8:]
            gv_ref[:, lo:lo + FT] = y.astype(BF16)
            outs.append(y)
        gate, val = outs
        act_ref[...] = (gate * _sigmoid(gate) * val).astype(BF16)

    wspec = lambda off: pl.BlockSpec((FT, D), lambda n, m: (n + off, 0))
    cwspec = lambda off: pl.BlockSpec((FK, FT), lambda n, m: (0, n + off))
    cbspec = lambda off: pl.BlockSpec((1, FT), lambda n, m: (0, n + off))
    pair = pl.BlockSpec((TMF, 2 * FT), lambda n, m: (m, n))
    return pl.pallas_call(
        body, name="ffn_up", grid=(NFT, T // TMF),
        in_specs=[pl.BlockSpec((TMF, D), lambda n, m: (m, 0)), wspec(0), wspec(NFT),
                  cwspec(0), cwspec(NFT), cbspec(0), cbspec(NFT)],
        out_specs=(pair, pair, pl.BlockSpec((TMF, FT), lambda n, m: (m, n))),
        out_shape=(SDS((T, 2 * DFF), BF16), SDS((T, 2 * DFF), BF16), SDS((T, DFF), BF16)),
        scratch_shapes=[pltpu.VMEM((8, FT), F32)] * 2,
    )(x1b, w_up, w_up, fcw, fcw, fcb, fcb)


def _ffn_down_loss(act, w_down, xh1, g1, b1, g2, b2, target):
    def body(a_ref, w_ref, xh1_ref, g1_ref, b1_ref, g2_ref, b2_ref, t_ref, dz_ref, loss_ref, gg_ref, gb_ref):
        @pl.when(pl.program_id(0) == 0)
        def _():
            loss_ref[...] = jnp.zeros_like(loss_ref)
            gg_ref[...] = jnp.zeros_like(gg_ref)
            gb_ref[...] = jnp.zeros_like(gb_ref)

        for sub in range(TM // TMF):
            rows = pl.ds(sub * TMF, TMF)
            x1 = xh1_ref[rows, :] * g1_ref[...] + b1_ref[...]
            z = ALPHA * x1 + _dot(a_ref[rows, :], w_ref[...])
            xh, rstd = _layer_norm_fwd(z)
            diff = xh * g2_ref[...] + b2_ref[...] - t_ref[rows, :]
            loss_ref[...] += 0.5 * _colsum(_rowmean(diff * diff))
            dout = diff * (1.0 / D)
            gg_ref[...] += _colsum(dout * xh)
            gb_ref[...] += _colsum(dout)
            dz_ref[rows, :] = _layer_norm_bwd(dout, xh, rstd, g2_ref[...])

    vec = pl.BlockSpec((1, D), lambda m: (0, 0))
    row = pl.BlockSpec((TM, D), lambda m: (m, 0))
    return pl.pallas_call(
        body, name="ffn_down_loss", grid=(T // TM,),
        in_specs=[pl.BlockSpec((TM, DFF), lambda m: (m, 0)), pl.BlockSpec((DFF, D), lambda m: (0, 0)),
                  row, vec, vec, vec, vec, row],
        out_specs=(row, pl.BlockSpec((1, 1), lambda m: (0, 0)), vec, vec),
        out_shape=(SDS((T, D), F32), SDS((1, 1), F32), SDS((1, D), F32), SDS((1, D), F32)),
    )(act, w_down, xh1, g1, b1, g2, b2, target)


def _ffn_down_bwd(dz2, w_down, gv, up, fcw):
    tiles = T // TMF

    def body(dz_ref, wd_ref, gv_ref, up_ref, cwg_ref, cwv_ref,
             dpre_ref, csg_ref, csv_ref, gwg_ref, gwv_ref, next_g, next_v):
        step = pl.program_id(1)
        tile = tiles - 1 - step

        @pl.when(step == 0)
        def _():
            for r in (csg_ref, csv_ref, gwg_ref, gwv_ref, next_g, next_v):
                r[...] = jnp.zeros_like(r)

        seq_end = lax.bitwise_and(tile + 1, S // TMF - 1) == 0
        dact = _dot_nt(dz_ref[...].astype(BF16), wd_ref[...])
        gate = gv_ref[:, :FT].astype(F32)
        val = gv_ref[:, FT:].astype(F32)
        sg = _sigmoid(gate)
        gs = gate * sg
        halves = ((dact * val * (sg + gs * (1.0 - sg)), cwg_ref, csg_ref, gwg_ref, next_g, 0),
                  (dact * gs, cwv_ref, csv_ref, gwv_ref, next_v, FT))
        for d0, cw_ref, cs_ref, gw_ref, nxt, lo in halves:
            after = jnp.where(seq_end, 0.0, nxt[...])
            d1 = _shift_up(d0, after, 1)
            d2 = _shift_up(d0, after, 2)
            nxt[...] = d0[:8]
            dpre_ref[:, lo:lo + FT] = (cw_ref[2:3, :] * d0 + cw_ref[1:2, :] * d1 + cw_ref[0:1, :] * d2).astype(BF16)
            cs_ref[...] += _colsum(d0)
            u = up_ref[:, lo:lo + FT].astype(F32)
            for k, dk in enumerate((d2, d1, d0)):
                gw_ref[k:k + 1, :] += _colsum(dk * u)

    cs = pl.BlockSpec((1, FT), lambda n, m: (0, n))
    gw = pl.BlockSpec((FK, FT), lambda n, m: (0, n))
    cwspec = lambda off: pl.BlockSpec((FK, FT), lambda n, m: (0, n + off))
    pair = pl.BlockSpec((TMF, 2 * FT), lambda n, m: (tiles - 1 - m, n))
    return pl.pallas_call(
        body, name="ffn_down_bwd", grid=(NFT, tiles),
        in_specs=[pl.BlockSpec((TMF, D), lambda n, m: (tiles - 1 - m, 0)), pl.BlockSpec((FT, D), lambda n, m: (n, 0)),
                  pair, pair, cwspec(0), cwspec(NFT)],
        out_specs=(pair, cs, cs, gw, gw),
        out_shape=(SDS((T, 2 * DFF), BF16), SDS((1, DFF), F32), SDS((1, DFF), F32),
                   SDS((FK, DFF), F32), SDS((FK, DFF), F32)),
        scratch_shapes=[pltpu.VMEM((8, FT), F32)] * 2,
    )(dz2, w_down, gv, up, fcw, fcw)


def _ffn_up_bwd_ln1(dpre, w_up, dz2, xh1, rstd1, g1):
    def body(a_ref, w_ref, dz2_ref, xh_ref, rstd_ref, g_ref, dz1_ref, gg_ref, gb_ref):
        @pl.when(pl.program_id(0) == 0)
        def _():
            gg_ref[...] = jnp.zeros_like(gg_ref)
            gb_ref[...] = jnp.zeros_like(gb_ref)

        for sub in range(TM // TMF):
            rows = pl.ds(sub * TMF, TMF)
            dx1 = ALPHA * dz2_ref[rows, :]
            for n in range(NFT):
                for half in range(2):
                    a = a_ref[rows, (2 * n + half) * FT:(2 * n + half + 1) * FT]
                    w = w_ref[pl.ds((half * NFT + n) * FT, FT), :]
                    dx1 = dx1 + _dot(a, w)
            xh = xh_ref[rows, :]
            gg_ref[...] += _colsum(dx1 * xh)
            gb_ref[...] += _colsum(dx1)
            dz1_ref[rows, :] = _layer_norm_bwd(dx1, xh, rstd_ref[rows, :], g_ref[...])

    vec = pl.BlockSpec((1, D), lambda m: (0, 0))
    row = pl.BlockSpec((TM, D), lambda m: (m, 0))
    return pl.pallas_call(
        body, name="ffn_up_bwd_ln1", grid=(T // TM,),
        in_specs=[pl.BlockSpec((TM, 2 * DFF), lambda m: (m, 0)),
                  pl.BlockSpec((2 * DFF, D), lambda m: (0, 0), pipeline_mode=pl.Buffered(1)),
                  row, row, pl.BlockSpec((TM, 1), lambda m: (m, 0)), vec],
        out_specs=(row, vec, vec),
        out_shape=(SDS((T, D), F32), SDS((1, D), F32), SDS((1, D), F32)),
    )(dpre, w_up, dz2, xh1, rstd1, g1)


def _grad_w_up(dpre, x1b):
    tk = 1024

    def body(a_ref, b_ref, o_ref, acc):
        k = pl.program_id(1)

        @pl.when(k == 0)
        def _():
            acc[...] = jnp.zeros_like(acc)

        acc[...] += _dot_tn(a_ref[...], b_ref[...])

        @pl.when(k == T // tk - 1)
        def _():
            o_ref[0] = acc[pl.ds(0, FT), :].astype(o_ref.dtype)
            o_ref[1] = acc[pl.ds(FT, FT), :].astype(o_ref.dtype)

    out = pl.pallas_call(
        body, name="grad_w_up", grid=(NFT, T // tk),
        in_specs=[pl.BlockSpec((tk, 2 * FT), lambda n, k: (k, n)), pl.BlockSpec((tk, D), lambda n, k: (k, 0))],
        out_specs=pl.BlockSpec((2, FT, D), lambda n, k: (0, n, 0)),
        out_shape=SDS((2, DFF, D), GRAD_WIRE),
        scratch_shapes=[pltpu.VMEM((2 * FT, D), F32)],
    )(dpre, x1b)
    return out.reshape(2 * DFF, D)


def _row_tile(rows, cols):
    if rows * cols * 4 <= (1 << 20) or rows % 8:
        return rows
    for t in (256, 176, 128, 88, 64, 32, 16, 8):
        if rows % t == 0 and t * cols * 4 <= (1 << 20):
            return t
    return 8


def _sum8(r, name):
    _, rows, cols = r.shape
    tr = _row_tile(rows, cols)

    def body(r_ref, o_ref):
        acc = r_ref[0].astype(F32)
        for p in range(1, NDEV):
            acc = acc + r_ref[p].astype(F32)
        o_ref[...] = acc

    return pl.pallas_call(
        body, name=name, grid=(rows // tr,),
        in_specs=[pl.BlockSpec((NDEV, tr, cols), lambda i: (0, i, 0))],
        out_specs=pl.BlockSpec((tr, cols), lambda i: (i, 0)),
        out_shape=SDS((rows, cols), F32),
    )(r)


def _sum8_adamw(r, w, m, v, name):
    rows, cols = w.shape
    tr = _row_tile(rows, cols)

    def body(r_ref, w_ref, m_ref, v_ref, g_out, d_ref, nm_ref, nv_ref):
        g_ = r_ref[0].astype(F32)
        for p in range(1, NDEV):
            g_ = g_ + r_ref[p].astype(F32)
        m_ = B1 * m_ref[...] + (1.0 - B1) * g_
        v_ = B2 * v_ref[...] + (1.0 - B2) * jnp.square(g_)
        m_hat = m_ / (1.0 - B1 ** STEP)
        v_hat = v_ / (1.0 - B2 ** STEP)
        g_out[...] = g_
        d_ref[...] = -LR * (m_hat / (jnp.sqrt(v_hat) + AEPS) + WD * w_ref[...])
        nm_ref[...] = m_
        nv_ref[...] = v_

    spec = pl.BlockSpec((tr, cols), lambda i: (i, 0))
    shp = SDS((rows, cols), F32)
    return pl.pallas_call(
        body, name=name, grid=(rows // tr,),
        in_specs=[pl.BlockSpec((NDEV, tr, cols), lambda i: (0, i, 0))] + [spec] * 3, out_specs=(spec,) * 4,
        out_shape=(shp,) * 4,
    )(r, w, m, v)


def _adamw_many(ws, gs, ms, vs, name):
    n = len(ws)

    def body(*refs):
        for i in range(n):
            w_ref, g_ref, m_ref, v_ref, d_ref, nm_ref, nv_ref = refs[i::n]
            g_ = g_ref[...]
            m_ = B1 * m_ref[...] + (1.0 - B1) * g_
            v_ = B2 * v_ref[...] + (1.0 - B2) * jnp.square(g_)
            m_hat = m_ / (1.0 - B1 ** STEP)
            v_hat = v_ / (1.0 - B2 ** STEP)
            d_ref[...] = -LR * (m_hat / (jnp.sqrt(v_hat) + AEPS) + WD * w_ref[...])
            nm_ref[...] = m_
            nv_ref[...] = v_

    shapes = tuple(SDS(w.shape, F32) for w in ws)
    res = pl.pallas_call(body, name=name, out_shape=shapes * 3)(*ws, *gs, *ms, *vs)
    return res[:n], res[n:2 * n], res[2 * n:]


def _local_step(x2, target, rel_table, first_weights, b_in, conv_b, conv_ln_g, conv_ln_b, attn_norm_g,
                conv_norm_g, late_weights, ln1_g, ln1_b, ffn_conv_b, ln2_g, ln2_b, ship_ffn_grads, ship_w_in_grads,
                ship_small_grads):
    buckets = jnp.asarray(_bucket_maps())
    bias = _bias_table(rel_table, buckets)
    xb, cast_done = _cast_x(x2, bias[0, 0, :8, :BLK])
    w_in_t, conv_w, ffn_conv_w = first_weights(cast_done)

    qkv, ag = _proj_in(xb, w_in_t, b_in)
    attn, lse = _attn_fwd(qkv, bias)
    u1 = _conv_fwd(ag, conv_w, conv_b)
    mixed = _mix_fwd(attn, u1, attn_norm_g, conv_norm_g, conv_ln_g, conv_ln_b)
    w_out = late_weights(0, mixed)
    xh1, rstd1, x1b = _out_proj_ln1(mixed, w_out, x2, ln1_g, ln1_b)
    w_up = late_weights(1, x1b)
    up, gv, act = _ffn_up(x1b, w_up, ffn_conv_w, ffn_conv_b)
    w_down = late_weights(2, act)
    dz2, loss, g_ln2_g, g_ln2_b = _ffn_down_loss(act, w_down, xh1, ln1_g, ln1_b, ln2_g, ln2_b, target)

    dpre, cs_g, cs_v, gfw_g, gfw_v = _ffn_down_bwd(dz2, w_down, gv, up, ffn_conv_w)
    g_w_down = _mm_tn(act, dz2, DFF // 2, 512, "grad_w_down")
    dz1, g_ln1_g, g_ln1_b = _ffn_up_bwd_ln1(dpre, w_up, dz2, xh1, rstd1, ln1_g)
    g_w_out = _mm_tn(mixed, dz1, D, 512, "grad_w_out")
    zero = ship_ffn_grads(g_w_down, _grad_w_up(dpre, x1b), g_w_out)
    dattn, du1, g_an, g_cn, g_clg, g_clb, g_cb = _mix_bwd(
        dz1, w_out, attn, u1, attn_norm_g + zero, conv_norm_g, conv_ln_g, conv_ln_b)
    dag, cs_ag, g_conv_w = _conv_bwd(du1, ag, conv_w)
    dq, dk, dv, cs_q, cs_k, cs_v2, dbias = _attn_bwd(qkv, attn, lse, dattn, bias)
    pieces = [dq, dk, dv, dag]
    zero_a = ship_w_in_grads(_grad_w_in(pieces, x2))
    g_rel = _rel_table_grad(dbias, buckets, zero_a)

    grads = dict(
        rel_table=g_rel,
        b_in=jnp.concatenate([cs_q, cs_k, cs_v2, cs_ag], axis=1),
        conv_b=g_cb, conv_ln_g=g_clg, conv_ln_b=g_clb, attn_norm_g=g_an, conv_norm_g=g_cn,
        ln1_g=g_ln1_g, ln1_b=g_ln1_b,
        ffn_conv_b=jnp.concatenate([cs_g, cs_v], axis=1),
        ln2_g=g_ln2_g, ln2_b=g_ln2_b,
        conv_w=g_conv_w[:CK],
        ffn_conv_w=jnp.concatenate([gfw_g, gfw_v], axis=1),
    )
    grads["loss"] = loss
    grad_x = _grad_x(pieces, w_in_t, dz1, ship_small_grads(grads))
    return loss, grad_x


SMALL = (("rel_table", (NBUCKET, NH)), ("b_in", (1, INW)), ("conv_b", (1, CW)), ("conv_ln_g", (1, CW)),
         ("conv_ln_b", (1, CW)), ("attn_norm_g", (1, AW)), ("conv_norm_g", (1, CW)), ("ln1_g", (1, D)),
         ("ln1_b", (1, D)), ("ffn_conv_b", (1, 2 * DFF)), ("ln2_g", (1, D)), ("ln2_b", (1, D)))
SHARDED_SMALL = (("conv_w", (CK, CW)), ("ffn_conv_w", (FK, 2 * DFF)))


def _pack(parts):
    flat = jnp.concatenate([p.reshape(-1) for p in parts])
    tile = 8 * PACK_LANES
    pad = (-flat.shape[0]) % tile
    return jnp.pad(flat, (0, pad)).reshape(-1, PACK_LANES)


def _unpack(packed, specs):
    flat = packed.reshape(-1)
    out, off = {}, 0
    for name, shp in specs:
        size = int(np.prod(shp))
        out[name] = flat[off:off + size].reshape(shp)
        off += size
    return out


def kernel(x, rel_table, w_in, b_in, conv_w, conv_b, conv_ln_g, conv_ln_b, attn_norm_g, conv_norm_g, w_out, ln1_g, ln1_b, w_up, ffn_conv_w, ffn_conv_b, w_down, ln2_g, ln2_b, loss_target, m_rel_table, m_w_in, m_b_in, m_conv_w, m_conv_b, m_conv_ln_g, m_conv_ln_b, m_attn_norm_g, m_conv_norm_g, m_w_out, m_ln1_g, m_ln1_b, m_w_up, m_ffn_conv_w, m_ffn_conv_b, m_w_down, m_ln2_g, m_ln2_b, v_rel_table, v_w_in, v_b_in, v_conv_w, v_conv_b, v_conv_ln_g, v_conv_ln_b, v_attn_norm_g, v_conv_norm_g, v_w_out, v_ln1_g, v_ln1_b, v_w_up, v_ffn_conv_w, v_ffn_conv_b, v_w_down, v_ln2_g, v_ln2_b):
    given = dict(locals())
    me = 4 * lax.axis_index("x") + 2 * lax.axis_index("y") + lax.axis_index("c")

    cols = lambda a: a.transpose(1, 0, 2).reshape(a.shape[1], NDEV * a.shape[2])
    rows = lambda a: a.reshape(NDEV * a.shape[1], a.shape[2])
    stack = lambda a: a.reshape(NDEV, a.shape[0] // NDEV, a.shape[1])

    small_specs = SMALL + SHARDED_SMALL
    packed_specs = small_specs + (("loss", (1, 1)),)
    grad, delta, new_m, new_v = {}, {}, {}, {}

    def adamw_big(n, partials, transposed=False):
        shp = given[n].shape
        to2d = (lambda a: a.reshape(shp[-2], shp[-1]).T) if transposed else (lambda a: a.reshape(shp[-2], shp[-1]))
        back = (lambda a: a.T.reshape(shp)) if transposed else (lambda a: a.reshape(shp))
        g_, d_, m_, v_ = _sum8_adamw(partials, to2d(given[n]), to2d(given["m_" + n]), to2d(given["v_" + n]), "adamw_" + n)
        grad[n], delta[n], new_m[n], new_v[n] = back(g_), back(d_), back(m_), back(v_)
        return d_

    first_state, zero0 = _exchange_start(
        [(w_in[0].T.astype(BF16), "gather"), (conv_w[0], "gather"), (ffn_conv_w[0], "gather")], "gather_first_start")

    def first_weights(after):
        lands = _exchange_wait(first_state, after, "gather_first_wait")
        return rows(lands[0]), cols(lands[1]), cols(lands[2])

    late_state, zero1 = _exchange_start(
        [(w_out[0].astype(BF16), "gather"), (w_up[0].T.astype(BF16) + zero0.astype(BF16), "gather"),
         (w_down[0].astype(BF16), "gather")], "gather_late_start")

    def late_weights(i, after):
        return rows(_exchange_wait(late_state, after, "gather_late_wait_%d" % i, only=(i,))[i])

    shipped = {}

    def ship_ffn_grads(g_w_down, g_w_up_t, g_w_out):
        shipped["ffn"], zero2 = _exchange_start(
            [(stack(a), "scatter") for a in (g_w_down, g_w_up_t, g_w_out)], "ffn_grads_start")
        return zero2

    def ship_w_in_grads(g_w_in_t):
        shipped["w_in"], zero3 = _exchange_start([(stack(g_w_in_t), "scatter")], "w_in_grads_start")
        return zero3.reshape(1, 1)

    def ship_small_grads(small_grads):
        shipped["small"], zero4 = _exchange_start(
            [(_pack([small_grads[n] for n, _ in packed_specs]), "gather")], "small_grads_start")
        return zero4.reshape(1, 1)

    loss, grad_x = _local_step(
        x.reshape(T, D), loss_target.reshape(T, D), rel_table + zero1, first_weights, b_in, conv_b, conv_ln_g,
        conv_ln_b, attn_norm_g, conv_norm_g, late_weights, ln1_g, ln1_b, ffn_conv_b,
        ln2_g, ln2_b, ship_ffn_grads, ship_w_in_grads, ship_small_grads)

    got_down, got_up, got_out = _exchange_wait(shipped["ffn"], grad_x, "ffn_grads_wait")
    adamw_big("w_down", got_down)
    adamw_big("w_up", got_up, transposed=True)
    last = adamw_big("w_out", got_out)

    (got_in,) = _exchange_wait(shipped["w_in"], last, "w_in_grads_wait")
    (got_small,) = _exchange_wait(shipped["small"], last, "small_grads_wait")
    adamw_big("w_in", got_in, transposed=True)
    small = _unpack(_sum8(got_small, "sum_small"), packed_specs)
    small["conv_w"] = lax.dynamic_slice_in_dim(small["conv_w"], me * (CW // NDEV), CW // NDEV, axis=1)
    small["ffn_conv_w"] = lax.dynamic_slice_in_dim(small["ffn_conv_w"], me * (2 * DFF // NDEV), 2 * DFF // NDEV, axis=1)
    names = [n for n, _ in small_specs]
    two = lambda a: a.reshape(a.shape[-2], a.shape[-1])
    ds, nms, nvs = _adamw_many([two(given[n]) for n in names], [small[n] for n in names],
                               [two(given["m_" + n]) for n in names], [two(given["v_" + n]) for n in names], "adamw_small")
    for n, d_, m_, v_ in zip(names, ds, nms, nvs):
        shp = given[n].shape
        grad[n], delta[n], new_m[n], new_v[n] = small[n].reshape(shp), d_.reshape(shp), m_.reshape(shp), v_.reshape(shp)

    order = ("rel_table", "w_in", "b_in", "conv_w", "conv_b", "conv_ln_g", "conv_ln_b", "attn_norm_g",
             "conv_norm_g", "w_out", "ln1_g", "ln1_b", "w_up", "ffn_conv_w", "ffn_conv_b", "w_down", "ln2_g", "ln2_b")
    return (small["loss"][0, 0], grad_x.reshape(BL, S, D), *[grad[n] for n in order], *[delta[n] for n in order],
            *[new_m[n] for n in order], *[new_v[n] for n in order])
```

```python
import math

import numpy as np
import jax
import jax.numpy as jnp
from jax import lax
from jax.experimental import pallas as pl
from jax.experimental.pallas import tpu as pltpu

F32 = jnp.float32
BF16 = jnp.bfloat16
SDS = jax.ShapeDtypeStruct

NDEV = 8
D = 1024
S = 2048
BL = 2
T = BL * S
NH = 12
HD = 64
AW = NH * HD
CW = D - AW
INW = 3 * AW + 2 * CW
CK = 31
DFF = 2816
FK = 3
BLK = 128
NBUCKET = 32
BRANCHES = ((128, 1), (512, 4), (2048, 16))
ALPHA = 2.0 ** 0.25
LN_EPS = 1e-5
NEG_INF = -1e30
LR, B1, B2, AEPS, WD, STEP = 0.001, 0.9, 0.999, 1e-08, 0.01, 10

TM = 512
FT = 1408
NFT = DFF // FT
TMF = 256
PACK_LANES = 128
GRAD_WIRE = BF16

assert all(w // d == BLK for w, d in BRANCHES)


def _dot(a, b):
    return jnp.dot(a, b, preferred_element_type=F32)


def _dot_nt(a, b):
    return lax.dot_general(a, b, (((1,), (1,)), ((), ())), preferred_element_type=F32)


def _dot_tn(a, b):
    return lax.dot_general(a, b, (((0,), (0,)), ((), ())), preferred_element_type=F32)


def _rowmean(v):
    return jnp.mean(v, axis=-1, keepdims=True)


def _colsum(v):
    return jnp.sum(v, axis=0, keepdims=True)


def _sigmoid(v):
    return jax.nn.sigmoid(v)


_HBM = pl.BlockSpec(memory_space=pltpu.HBM)
_SEM = pl.BlockSpec(memory_space=pltpu.SEMAPHORE)
_EFFECT = pltpu.SideEffectType.DATAFLOW_SIDE_EFFECTING


def _peer_of(k):
    x, y, c = lax.axis_index("x"), lax.axis_index("y"), lax.axis_index("c")
    px = 1 - x if k & 4 else x
    py = 1 - y if k & 2 else y
    pc = 1 - c if k & 1 else c
    return (px, py, pc), 4 * px + 2 * py + pc


def _split_copies(kinds, ins, lands, send_sems, recv_sems, started):
    me = 4 * lax.axis_index("x") + 2 * lax.axis_index("y") + lax.axis_index("c")
    out = []
    for i, kind in enumerate(kinds):
        for k in range(1, NDEV):
            dev, pid = _peer_of(k)
            src = ins[i] if kind == "gather" else ins[i].at[pid]
            dst = lands[i].at[me] if started else lands[i].at[pid]
            slot = i * (NDEV - 1) + k - 1
            out.append(pltpu.make_async_remote_copy(
                src_ref=src, dst_ref=dst, send_sem=send_sems.at[slot], recv_sem=recv_sems.at[slot],
                device_id=dev, device_id_type=pl.DeviceIdType.MESH))
    return out


def _exchange_start(items, name):
    n = len(items)
    kinds = [k for _, k in items]
    srcs = [pltpu.with_memory_space_constraint(a, pltpu.HBM) for a, _ in items]
    lands = []
    for a, k in items:
        shp = (NDEV,) + tuple(a.shape) if k == "gather" else tuple(a.shape)
        lands.append(pltpu.with_memory_space_constraint(lax.empty(shp, a.dtype), pltpu.HBM))

    def body(*refs):
        ins, land_refs = refs[:n], refs[n:2 * n]
        send_sems, recv_sems, own_sems = refs[2 * n:2 * n + 3]
        token = refs[-1]
        for cp in _own_copies(kinds, ins, land_refs, own_sems):
            cp.start()
        for cp in _split_copies(kinds, ins, land_refs, send_sems, recv_sems, True):
            cp.start()
        token[...] = jnp.zeros_like(token)

    sems = pltpu.SemaphoreType.DMA((n * (NDEV - 1),))
    res = pl.pallas_call(
        body, name=name,
        out_shape=(sems, sems, pltpu.SemaphoreType.DMA((n,)),
                   *[pltpu.HBM(a.shape, a.dtype) for a in srcs + lands], SDS((8, 128), F32)),
        in_specs=[_HBM] * (2 * n),
        out_specs=(_SEM, _SEM, _SEM, *[_HBM] * (2 * n), pl.BlockSpec(memory_space=pltpu.VMEM)),
        input_output_aliases={i: 3 + i for i in range(2 * n)},
        compiler_params=pltpu.CompilerParams(has_side_effects=_EFFECT),
    )(*srcs, *lands)
    return (kinds, res[0], res[1], res[2], list(res[3:3 + n]), list(res[3 + n:3 + 2 * n])), res[-1][0, 0]


def _own_copies(kinds, ins, lands, own_sems):
    me = 4 * lax.axis_index("x") + 2 * lax.axis_index("y") + lax.axis_index("c")
    return [pltpu.make_async_copy(ins[i] if kind == "gather" else ins[i].at[me], lands[i].at[me], own_sems.at[i])
            for i, kind in enumerate(kinds)]


def _exchange_wait(state, after, name, only=None):
    kinds, send_sems, recv_sems, own_sems, srcs, lands = state
    n = len(kinds)
    chosen = range(n) if only is None else only

    def body(*refs):
        ins, land_refs = refs[:n], refs[n:2 * n]
        s_sems, r_sems, o_sems = refs[2 * n:2 * n + 3]
        remote = _split_copies(kinds, ins, land_refs, s_sems, r_sems, False)
        own = _own_copies(kinds, ins, land_refs, o_sems)
        for i in chosen:
            for cp in remote[i * (NDEV - 1):(i + 1) * (NDEV - 1)]:
                cp.wait_send()
                cp.wait_recv()
        for i in chosen:
            own[i].wait()

    res = pl.pallas_call(
        body, name=name,
        out_shape=tuple(pltpu.HBM(a.shape, a.dtype) for a in srcs + lands),
        in_specs=[_HBM] * (2 * n) + [_SEM, _SEM, _SEM, pl.BlockSpec(memory_space=pl.ANY)],
        out_specs=tuple([_HBM] * (2 * n)),
        input_output_aliases={i: i for i in range(2 * n)},
        compiler_params=pltpu.CompilerParams(has_side_effects=_EFFECT),
    )(*srcs, *lands, send_sems, recv_sems, own_sems, after)
    srcs[:], lands[:] = res[:n], res[n:]
    return list(lands)


def _cast_x(x2, after):
    def body(x_ref, after_ref, o_ref, done_ref):
        o_ref[...] = x_ref[...].astype(BF16)
        done_ref[...] = jnp.zeros_like(done_ref)

    return pl.pallas_call(
        body, name="cast_x", grid=(T // TM,),
        in_specs=[pl.BlockSpec((TM, D), lambda m: (m, 0)), pl.BlockSpec(memory_space=pl.ANY)],
        out_specs=(pl.BlockSpec((TM, D), lambda m: (m, 0)), pl.BlockSpec((8, 128), lambda m: (0, 0))),
        out_shape=(SDS((T, D), BF16), SDS((8, 128), F32)),
    )(x2, after)


def _proj_in(xb, w_in_t, b_in):
    nq = 3 * AW

    def body(x_ref, w_ref, b_ref, qkv_ref, ag_ref):
        xb = x_ref[...]
        qkv_ref[...] = (_dot_nt(xb, w_ref[pl.ds(0, nq), :]) + b_ref[:, :nq]).astype(BF16)
        ag_ref[...] = _dot_nt(xb, w_ref[pl.ds(nq, 2 * CW), :]) + b_ref[:, nq:]

    return pl.pallas_call(
        body, name="proj_in", grid=(T // TM,),
        in_specs=[pl.BlockSpec((TM, D), lambda m: (m, 0)), pl.BlockSpec((INW, D), lambda m: (0, 0)),
                  pl.BlockSpec((1, INW), lambda m: (0, 0))],
        out_specs=(pl.BlockSpec((TM, nq), lambda m: (m, 0)), pl.BlockSpec((TM, 2 * CW), lambda m: (m, 0))),
        out_shape=(SDS((T, nq), BF16), SDS((T, 2 * CW), F32)),
    )(xb, w_in_t, b_in)


def _grad_x(pieces, w_in_t, dz1, zero):
    widths = [p.shape[1] for p in pieces]

    def body(*refs):
        p_refs = refs[:len(pieces)]
        w_ref, dz_ref, z_ref, o_ref = refs[len(pieces):]
        acc = ALPHA * dz_ref[...] + z_ref[...]
        r0 = 0
        for p_ref, wd in zip(p_refs, widths):
            acc = acc + _dot(p_ref[...], w_ref[pl.ds(r0, wd), :])
            r0 += wd
        o_ref[...] = acc

    row = pl.BlockSpec((TM, D), lambda m: (m, 0))
    return pl.pallas_call(
        body, name="grad_x", grid=(T // TM,),
        in_specs=[pl.BlockSpec((TM, wd), lambda m: (m, 0)) for wd in widths]
        + [pl.BlockSpec((INW, D), lambda m: (0, 0)), row, pl.BlockSpec((1, 1), lambda m: (0, 0))],
        out_specs=row,
        out_shape=SDS((T, D), F32),
    )(*pieces, w_in_t, dz1, zero)


def _grad_w_in(pieces, x2):
    widths = [p.shape[1] for p in pieces]
    tk = 512
    nk = T // tk

    def body(*refs):
        p_refs = refs[:len(pieces)]
        x_ref, o_ref, acc = refs[len(pieces):]
        k = pl.program_id(0)

        @pl.when(k == 0)
        def _():
            acc[...] = jnp.zeros_like(acc)

        xb = x_ref[...].astype(BF16)
        r0 = 0
        for p_ref, wd in zip(p_refs, widths):
            acc[pl.ds(r0, wd), :] += _dot_tn(p_ref[...], xb)
            r0 += wd

        @pl.when(k == nk - 1)
        def _():
            o_ref[...] = acc[...].astype(o_ref.dtype)

    return pl.pallas_call(
        body, name="grad_w_in", grid=(nk,),
        in_specs=[pl.BlockSpec((tk, wd), lambda k: (k, 0)) for wd in widths] + [pl.BlockSpec((tk, D), lambda k: (k, 0))],
        out_specs=pl.BlockSpec((INW, D), lambda k: (0, 0)),
        out_shape=SDS((INW, D), GRAD_WIRE),
        scratch_shapes=[pltpu.VMEM((INW, D), F32)],
    )(*pieces, x2)


def _mm_tn(a, b, tn, tk, name):
    t_, na = a.shape
    nb = b.shape[1]
    nk = t_ // tk

    def body(a_ref, b_ref, o_ref, acc):
        k = pl.program_id(1)

        @pl.when(k == 0)
        def _():
            acc[...] = jnp.zeros_like(acc)

        acc[...] += _dot_tn(a_ref[...].astype(BF16), b_ref[...].astype(BF16))

        @pl.when(k == nk - 1)
        def _():
            o_ref[...] = acc[...].astype(o_ref.dtype)

    return pl.pallas_call(
        body, name=name, grid=(na // tn, nk),
        in_specs=[pl.BlockSpec((tk, tn), lambda n, k: (k, n)),
                  pl.BlockSpec((tk, nb), lambda n, k: (k, 0))],
        out_specs=pl.BlockSpec((tn, nb), lambda n, k: (n, 0)),
        out_shape=SDS((na, nb), GRAD_WIRE),
        scratch_shapes=[pltpu.VMEM((tn, nb), F32)],
    )(a, b)


def _bucket_maps():
    qi = np.arange(BLK)[:, None]
    kj = np.arange(2 * BLK)[None, :]
    steps = np.maximum(qi + BLK - kj, 0)
    exact = NBUCKET // 2
    maps = []
    for _, dil in BRANCHES:
        dist = steps * dil
        d_f = np.maximum(dist, 1).astype(np.float32)
        large = exact + (np.log(d_f / np.float32(exact)) / np.float32(math.log(S / exact))
                         * np.float32(NBUCKET - exact)).astype(np.int32)
        large = np.minimum(large, NBUCKET - 1)
        maps.append(np.where(dist < exact, dist, large).astype(np.int32))
    return np.stack(maps)


def _bias_table(rel_table, buckets):
    def body(t_ref, b_ref, o_ref):
        bk = b_ref[0]
        for h in range(NH):
            acc = jnp.zeros((BLK, 2 * BLK), F32)
            for k in range(NBUCKET):
                acc = jnp.where(bk == k, t_ref[k, h], acc)
            o_ref[0, h] = acc

    return pl.pallas_call(
        body, name="bias_table", grid=(len(BRANCHES),),
        in_specs=[pl.BlockSpec(memory_space=pltpu.SMEM),
                  pl.BlockSpec((1, BLK, 2 * BLK), lambda i: (i, 0, 0))],
        out_specs=pl.BlockSpec((1, NH, BLK, 2 * BLK), lambda i: (i, 0, 0, 0)),
        out_shape=SDS((len(BRANCHES), NH, BLK, 2 * BLK), F32),
    )(rel_table, buckets)


def _rel_table_grad(dbias, buckets, after):
    def body(d_ref, b_ref, after_ref, o_ref):
        for k in range(NBUCKET):
            tot = jnp.zeros((1, 1), F32)
            for br in range(len(BRANCHES)):
                sel = jnp.where(b_ref[br] == k, d_ref[br, 0], 0.0)
                tot = tot + jnp.sum(jnp.sum(sel, axis=1, keepdims=True), axis=0, keepdims=True)
            o_ref[0, :, pl.ds(k, 1)] = tot

    out = pl.pallas_call(
        body, name="rel_table_grad", grid=(NH,),
        in_specs=[pl.BlockSpec((len(BRANCHES), 1, BLK, 2 * BLK), lambda h: (0, h, 0, 0)),
                  pl.BlockSpec((len(BRANCHES), BLK, 2 * BLK), lambda h: (0, 0, 0)),
                  pl.BlockSpec(memory_space=pl.ANY)],
        out_specs=pl.BlockSpec((1, 1, NBUCKET), lambda h: (h, 0, 0)),
        out_shape=SDS((NH, 1, NBUCKET), F32),
    )(dbias, buckets, after)
    return out.reshape(NH, NBUCKET).T


PADK = BLK
SCALE = 1.0 / math.sqrt(HD)
NBLK = S // BLK
ACH = 256


def _branch_geometry(br):
    dil = BRANCHES[br][1]
    sub = S // dil
    return dil, sub, sub // BLK


def _token_rows(br, i):
    dil, _, nblk = _branch_geometry(br)
    if dil == 1:
        return pl.ds(pl.multiple_of(i * BLK, BLK), BLK), i
    r = lax.shift_right_logical(i, nblk.bit_length() - 1)
    n = lax.bitwise_and(i, nblk - 1)
    return pl.ds(r + dil * BLK * n, BLK, stride=dil), n


def _sub_layout_loop(br, step):
    dil, sub, _ = _branch_geometry(br)
    rows = min(sub, ACH)
    nchunk = sub // rows

    def it_step(it, carry):
        if dil == 1:
            src = pl.ds(pl.multiple_of(it * rows, rows), rows)
        else:
            r = lax.shift_right_logical(it, nchunk.bit_length() - 1)
            src = pl.ds(r + dil * rows * lax.bitwise_and(it, nchunk - 1), rows, stride=dil)
        step(src, pl.multiple_of(it * rows, BLK), rows)
        return carry

    lax.fori_loop(0, dil * nchunk, it_step, 0)


def _masked_bias(bias_ref, bm):
    qi = lax.broadcasted_iota(jnp.int32, (BLK, 2 * BLK), 0)
    kj = lax.broadcasted_iota(jnp.int32, (BLK, 2 * BLK), 1)
    first = jnp.logical_and(kj >= BLK, kj - BLK <= qi)
    valid = jnp.logical_or(first, jnp.logical_and(kj < BLK, kj >= qi))
    for br in range(len(BRANCHES)):
        for j in range(2):
            b = bias_ref[br, j]
            bm[br, 1, pl.ds(j * BLK, BLK), :] = jnp.where(valid, b, NEG_INF)
            bm[br, 0, pl.ds(j * BLK, BLK), :] = jnp.where(first, b, NEG_INF)


def _head_split(fn):
    def split(t):
        h0 = lax.broadcasted_iota(jnp.int32, t.shape, 1) < HD
        t = fn(t)
        return jnp.where(h0, t, 0.0).astype(BF16), jnp.where(h0, 0.0, t).astype(BF16)
    return split


def _attn_fwd(qkv, bias):
    nbr = len(BRANCHES)

    def body(q_ref, k_ref, v_ref, bias_ref, o_ref, lse_ref, qf, kf, vf, qs0, qs1, ks, vs, bm, ob, mb, lb):
        qf[...] = q_ref[...].astype(F32)
        kf[...] = k_ref[...].astype(F32)
        vf[...] = v_ref[...].astype(F32)
        _masked_bias(bias_ref, bm)
        ks[pl.ds(0, PADK), :] = jnp.zeros((PADK, BLK), BF16)
        vs[pl.ds(0, PADK), :] = jnp.zeros((PADK, BLK), BF16)
        head0 = lax.broadcasted_iota(jnp.int32, (BLK, BLK), 1) < HD
        split_q = _head_split(lambda t: t * SCALE)

        for br in range(nbr):
            nblk = _branch_geometry(br)[2]

            def stage(src, off, rows):
                qs0[pl.ds(off, rows), :], qs1[pl.ds(off, rows), :] = split_q(qf[src, :])
                ks[pl.ds(PADK + off, rows), :] = kf[src, :].astype(BF16)
                vs[pl.ds(PADK + off, rows), :] = vf[src, :].astype(BF16)

            _sub_layout_loop(br, stage)

            def blk(i, carry, br=br, nblk=nblk):
                base = pl.multiple_of(i * BLK, BLK)
                rows, n = _token_rows(br, i)
                q01 = jnp.concatenate([qs0[pl.ds(base, BLK), :], qs1[pl.ds(base, BLK), :]], axis=0)
                if nblk > 1:
                    kcat = ks[pl.ds(base, 2 * BLK), :]
                    vcat = vs[pl.ds(base, 2 * BLK), :]
                    s = _dot_nt(q01, kcat) + bm[br, jnp.minimum(n, 1)]
                else:
                    kcat = ks[pl.ds(PADK + base, BLK), :]
                    vcat = vs[pl.ds(PADK + base, BLK), :]
                    s = _dot_nt(q01, kcat) + bm[br, 0, :, BLK:]
                mx = jnp.max(s, axis=-1, keepdims=True)
                p = jnp.exp(s - mx)
                ls = jnp.sum(p, axis=-1, keepdims=True)
                o = _dot(p.astype(BF16), vcat)
                ob[br, rows, :] = jnp.where(head0, o[:BLK], o[BLK:])
                mb[br, rows, :] = jnp.where(head0, mx[:BLK], mx[BLK:])
                lb[br, rows, :] = jnp.where(head0, ls[:BLK], ls[BLK:])
                return carry

            lax.fori_loop(0, NBLK, blk, 0, unroll=True)

        def merge(i, carry):
            rows = pl.ds(pl.multiple_of(i * ACH, ACH), ACH)
            m_all = jnp.maximum(jnp.maximum(mb[0, rows, :], mb[1, rows, :]), mb[2, rows, :])
            num = jnp.zeros((ACH, BLK), F32)
            den = jnp.zeros((ACH, BLK), F32)
            for br in range(nbr):
                c = jnp.exp(mb[br, rows, :] - m_all)
                num = num + ob[br, rows, :] * c
                den = den + lb[br, rows, :] * c
            o_ref[rows, :] = num / den
            lse_ref[rows, :] = m_all + jnp.log(den)
            return carry

        lax.fori_loop(0, S // ACH,merge, 0)

    npair = NH // 2
    blk_spec = lambda off: pl.BlockSpec((S, BLK), lambda b, hp: (b, off + hp))
    return pl.pallas_call(
        body, name="attn_fwd", grid=(BL, npair),
        in_specs=[blk_spec(0), blk_spec(npair), blk_spec(2 * npair),
                  pl.BlockSpec((nbr, 2, BLK, 2 * BLK), lambda b, hp: (0, hp, 0, 0))],
        out_specs=(blk_spec(0), blk_spec(0)),
        out_shape=(SDS((T, AW), F32), SDS((T, AW), F32)),
        scratch_shapes=[pltpu.VMEM((S, BLK), F32)] * 3 + [pltpu.VMEM((S, BLK), BF16)] * 2
        + [pltpu.VMEM((PADK + S, BLK), BF16)] * 2 + [pltpu.VMEM((nbr, 2, 2 * BLK, 2 * BLK), F32)]
        + [pltpu.VMEM((nbr, S, BLK), F32)] * 3,
    )(qkv, qkv, qkv, bias)


def _attn_bwd(qkv, attn, lse, dattn, bias):
    nbr = len(BRANCHES)

    def body(q_ref, k_ref, v_ref, o_ref, lse_ref, do_ref, bias_ref,
             dq_ref, dk_ref, dv_ref, sq_ref, sk_ref, sv_ref, db_ref,
             qf, kf, vf, dl, dqa, dka, dva, qs0, qs1, ds0, ds1, ks, vs, dks, dvs, bm):
        b = pl.program_id(1)
        qf[...] = q_ref[...].astype(F32)
        kf[...] = k_ref[...].astype(F32)
        vf[...] = v_ref[...].astype(F32)
        dqa[...] = jnp.zeros_like(dqa)
        dka[...] = jnp.zeros_like(dka)
        dva[...] = jnp.zeros_like(dva)
        _masked_bias(bias_ref, bm)
        ks[pl.ds(0, PADK), :] = jnp.zeros((PADK, BLK), BF16)
        vs[pl.ds(0, PADK), :] = jnp.zeros((PADK, BLK), BF16)
        head0 = lax.broadcasted_iota(jnp.int32, (BLK, BLK), 1) < HD
        split_q = _head_split(lambda t: t * SCALE)
        split_do = _head_split(lambda t: t)

        @pl.when(b == 0)
        def _():
            db_ref[...] = jnp.zeros_like(db_ref)
            sq_ref[...] = jnp.zeros_like(sq_ref)
            sk_ref[...] = jnp.zeros_like(sk_ref)
            sv_ref[...] = jnp.zeros_like(sv_ref)

        def delta(i, carry):
            rows = pl.ds(pl.multiple_of(i * ACH, ACH), ACH)
            prod = do_ref[rows, :] * o_ref[rows, :]
            h0 = lax.broadcasted_iota(jnp.int32, (ACH, BLK), 1) < HD
            d0 = jnp.sum(jnp.where(h0, prod, 0.0), axis=-1, keepdims=True)
            d1 = jnp.sum(jnp.where(h0, 0.0, prod), axis=-1, keepdims=True)
            dl[rows, :] = jnp.where(h0, d0, d1)
            return carry

        lax.fori_loop(0, S // ACH,delta, 0)

        for br in range(nbr):
            nblk = _branch_geometry(br)[2]

            def stage(src, off, rows):
                qs0[pl.ds(off, rows), :], qs1[pl.ds(off, rows), :] = split_q(qf[src, :])
                ds0[pl.ds(off, rows), :], ds1[pl.ds(off, rows), :] = split_do(do_ref[src, :])
                ks[pl.ds(PADK + off, rows), :] = kf[src, :].astype(BF16)
                vs[pl.ds(PADK + off, rows), :] = vf[src, :].astype(BF16)

            _sub_layout_loop(br, stage)
            dks[...] = jnp.zeros_like(dks)
            dvs[...] = jnp.zeros_like(dvs)

            def blk(i, carry, br=br, nblk=nblk):
                base = pl.multiple_of(i * BLK, BLK)
                rows, n = _token_rows(br, i)
                q01 = jnp.concatenate([qs0[pl.ds(base, BLK), :], qs1[pl.ds(base, BLK), :]], axis=0)
                do01 = jnp.concatenate([ds0[pl.ds(base, BLK), :], ds1[pl.ds(base, BLK), :]], axis=0)
                lse_b = lse_ref[rows, :]
                dl_b = dl[rows, :]
                lse01 = jnp.concatenate([lse_b[:, 0:1], lse_b[:, HD:HD + 1]], axis=0)
                dl01 = jnp.concatenate([dl_b[:, 0:1], dl_b[:, HD:HD + 1]], axis=0)
                if nblk > 1:
                    krows = pl.ds(base, 2 * BLK)
                    bias_m = bm[br, jnp.minimum(n, 1)]
                else:
                    krows = pl.ds(PADK + base, BLK)
                    bias_m = bm[br, 0, :, BLK:]
                kcat = ks[krows, :]
                vcat = vs[krows, :]
                p = jnp.exp(_dot_nt(q01, kcat) + bias_m - lse01)
                dsv = p * (_dot_nt(do01, vcat) - dl01)
                if nblk > 1:
                    db_ref[br, 0] += dsv[:BLK]
                    db_ref[br, 1] += dsv[BLK:]
                else:
                    db_ref[br, 0, :, BLK:] += dsv[:BLK]
                    db_ref[br, 1, :, BLK:] += dsv[BLK:]
                dsb = dsv.astype(BF16)
                dq01 = _dot(dsb, kcat)
                dqa[rows, :] = dqa[rows, :] + jnp.where(head0, dq01[:BLK], dq01[BLK:])
                dks[krows, :] = dks[krows, :] + _dot_tn(dsb, q01)
                dvs[krows, :] = dvs[krows, :] + _dot_tn(p.astype(BF16), do01)
                return carry

            lax.fori_loop(0, NBLK, blk, 0, unroll=True)

            def fold(src, off, rows):
                dka[src, :] = dka[src, :] + dks[pl.ds(PADK + off, rows), :]
                dva[src, :] = dva[src, :] + dvs[pl.ds(PADK + off, rows), :]

            _sub_layout_loop(br, fold)

        def flush(i, carry):
            rows = pl.ds(pl.multiple_of(i * ACH, ACH), ACH)
            for acc, out, cs, mul in ((dqa, dq_ref, sq_ref, SCALE), (dka, dk_ref, sk_ref, 1.0), (dva, dv_ref, sv_ref, 1.0)):
                val = acc[rows, :] * mul
                out[rows, :] = val.astype(BF16)
                cs[...] += _colsum(val)
            return carry

        lax.fori_loop(0, S // ACH,flush, 0)

    npair = NH // 2
    blk_spec = lambda off: pl.BlockSpec((S, BLK), lambda hp, b: (b, off + hp))
    sum_spec = pl.BlockSpec((1, BLK), lambda hp, b: (0, hp))
    return pl.pallas_call(
        body, name="attn_bwd", grid=(npair, BL),
        in_specs=[blk_spec(0), blk_spec(npair), blk_spec(2 * npair), blk_spec(0), blk_spec(0), blk_spec(0),
                  pl.BlockSpec((nbr, 2, BLK, 2 * BLK), lambda hp, b: (0, hp, 0, 0))],
        out_specs=(blk_spec(0), blk_spec(0), blk_spec(0), sum_spec, sum_spec, sum_spec,
                   pl.BlockSpec((nbr, 2, BLK, 2 * BLK), lambda hp, b: (0, hp, 0, 0))),
        out_shape=(SDS((T, AW), BF16), SDS((T, AW), BF16), SDS((T, AW), BF16),
                   SDS((1, AW), F32), SDS((1, AW), F32), SDS((1, AW), F32),
                   SDS((nbr, NH, BLK, 2 * BLK), F32)),
        scratch_shapes=[pltpu.VMEM((S, BLK), F32)] * 7 + [pltpu.VMEM((S, BLK), BF16)] * 4
        + [pltpu.VMEM((PADK + S, BLK), BF16)] * 2 + [pltpu.VMEM((PADK + S, BLK), F32)] * 2
        + [pltpu.VMEM((nbr, 2, 2 * BLK, 2 * BLK), F32)],
    )(qkv, qkv, qkv, attn, lse, dattn, bias)


CH = 256
PADR = 32


def _tap_phases(offset_of_tap):
    taps = sorted((offset_of_tap(k) % 8, offset_of_tap(k) - offset_of_tap(k) % 8, k) for k in range(CK))
    assert all(lo + CH + ph <= CH + PADR for ph, lo, _ in taps)
    return taps


def _rows_up(win):
    made = {0: win}

    def get(phase):
        if phase not in made:
            made[phase] = pltpu.roll(win, win.shape[0] - phase, 0)
        return made[phase]
    return get


def _conv_fwd(ag, conv_w, conv_b):
    def body(ag_ref, w_ref, b_ref, u1_ref, u0p):
        u0p[pl.ds(0, PADR), :] = jnp.zeros((PADR, CW), F32)

        def glu(i, carry):
            t0 = pl.multiple_of(i * CH, CH)
            a = ag_ref[pl.ds(t0, CH), :CW]
            g = ag_ref[pl.ds(t0, CH), CW:]
            u0p[pl.ds(PADR + t0, CH), :] = a * _sigmoid(g)
            return carry

        lax.fori_loop(0, S // CH, glu, 0)

        def conv(i, carry):
            t0 = pl.multiple_of(i * CH, CH)
            win = u0p[pl.ds(t0, CH + PADR), :]
            acc = jnp.zeros((CH, CW), F32) + b_ref[...]
            up = _rows_up(win)
            for phase, lo, k in _tap_phases(lambda k: PADR - (CK - 1) + k):
                acc = acc + up(phase)[lo:lo + CH, :] * w_ref[k:k + 1, :]
            u1_ref[pl.ds(t0, CH), :] = acc
            return carry

        lax.fori_loop(0, S // CH, conv, 0)

    return pl.pallas_call(
        body, name="conv_fwd", grid=(BL,),
        in_specs=[pl.BlockSpec((S, 2 * CW), lambda b: (b, 0)),
                  pl.BlockSpec((CK, CW), lambda b: (0, 0)),
                  pl.BlockSpec((1, CW), lambda b: (0, 0))],
        out_specs=pl.BlockSpec((S, CW), lambda b: (b, 0)),
        out_shape=SDS((T, CW), F32),
        scratch_shapes=[pltpu.VMEM((S + PADR, CW), F32)],
    )(ag, conv_w, conv_b)


def _conv_post(u1, cg, cb):
    mu = _rowmean(u1)
    uc = u1 - mu
    rstd = lax.rsqrt(_rowmean(uc * uc) + LN_EPS)
    xh = uc * rstd
    u2 = xh * cg + cb
    sg = _sigmoid(u2)
    return xh, rstd, u2, sg, u2 * sg


def _mix_fwd(attn, u1, ga, gc, cg, cb):
    def body(a_ref, u_ref, ga_ref, gc_ref, cg_ref, cb_ref, o_ref):
        a = a_ref[...]
        ra = lax.rsqrt(_rowmean(a * a) + LN_EPS)
        o_ref[:, :AW] = (a * ra * ga_ref[...]).astype(BF16)
        _, _, _, _, u3 = _conv_post(u_ref[...], cg_ref[...], cb_ref[...])
        rc = lax.rsqrt(_rowmean(u3 * u3) + LN_EPS)
        o_ref[:, AW:] = (u3 * rc * gc_ref[...]).astype(BF16)

    vec = lambda w: pl.BlockSpec((1, w), lambda m: (0, 0))
    return pl.pallas_call(
        body, name="mix_fwd", grid=(T // TM,),
        in_specs=[pl.BlockSpec((TM, AW), lambda m: (m, 0)), pl.BlockSpec((TM, CW), lambda m: (m, 0)),
                  vec(AW), vec(CW), vec(CW), vec(CW)],
        out_specs=pl.BlockSpec((TM, D), lambda m: (m, 0)),
        out_shape=SDS((T, D), BF16),
    )(attn, u1, ga, gc, cg, cb)


def _mix_bwd(dz1, w_out, attn, u1, ga, gc, cg, cb):
    def body(dz_ref, w_ref, a_ref, u_ref, ga_ref, gc_ref, cg_ref, cb_ref,
             da_ref, du_ref, g_an, g_cn, g_lg, g_lb, g_cb):
        @pl.when(pl.program_id(0) == 0)
        def _():
            for r in (g_an, g_cn, g_lg, g_lb, g_cb):
                r[...] = jnp.zeros_like(r)

        dm = _dot_nt(dz_ref[...].astype(BF16), w_ref[...])
        a = a_ref[...]
        dna = dm[:, :AW]
        ra = lax.rsqrt(_rowmean(a * a) + LN_EPS)
        g_an[...] += _colsum(dna * a * ra)
        dat = dna * ga_ref[...]
        da_ref[...] = ra * dat - a * (ra * ra * ra) * _rowmean(dat * a)

        xh, rstd, u2, sg, u3 = _conv_post(u_ref[...], cg_ref[...], cb_ref[...])
        dnc = dm[:, AW:]
        rc = lax.rsqrt(_rowmean(u3 * u3) + LN_EPS)
        g_cn[...] += _colsum(dnc * u3 * rc)
        dut = dnc * gc_ref[...]
        du3 = rc * dut - u3 * (rc * rc * rc) * _rowmean(dut * u3)
        du2 = du3 * sg * (1.0 + u2 * (1.0 - sg))
        g_lg[...] += _colsum(du2 * xh)
        g_lb[...] += _colsum(du2)
        dxh = du2 * cg_ref[...]
        du1 = rstd * (dxh - _rowmean(dxh) - xh * _rowmean(dxh * xh))
        g_cb[...] += _colsum(du1)
        du_ref[...] = du1

    vec = lambda w: pl.BlockSpec((1, w), lambda m: (0, 0))
    return pl.pallas_call(
        body, name="mix_bwd", grid=(T // TM,),
        in_specs=[pl.BlockSpec((TM, D), lambda m: (m, 0)), pl.BlockSpec((D, D), lambda m: (0, 0)),
                  pl.BlockSpec((TM, AW), lambda m: (m, 0)),
                  pl.BlockSpec((TM, CW), lambda m: (m, 0)), vec(AW), vec(CW), vec(CW), vec(CW)],
        out_specs=(pl.BlockSpec((TM, AW), lambda m: (m, 0)), pl.BlockSpec((TM, CW), lambda m: (m, 0)),
                   vec(AW), vec(CW), vec(CW), vec(CW), vec(CW)),
        out_shape=(SDS((T, AW), F32), SDS((T, CW), F32),
                   SDS((1, AW), F32), SDS((1, CW), F32), SDS((1, CW), F32), SDS((1, CW), F32), SDS((1, CW), F32)),
    )(dz1, w_out, attn, u1, ga, gc, cg, cb)


def _conv_bwd(du1, ag, conv_w):
    def body(du_ref, ag_ref, w_ref, dag_ref, cs_ref, gw_ref, u0p, dup):
        @pl.when(pl.program_id(0) == 0)
        def _():
            cs_ref[...] = jnp.zeros_like(cs_ref)
            gw_ref[...] = jnp.zeros_like(gw_ref)

        u0p[pl.ds(0, PADR), :] = jnp.zeros((PADR, CW), F32)
        dup[pl.ds(S, PADR), :] = jnp.zeros((PADR, CW), F32)

        def fill(i, carry):
            t0 = pl.multiple_of(i * CH, CH)
            a = ag_ref[pl.ds(t0, CH), :CW]
            g = ag_ref[pl.ds(t0, CH), CW:]
            u0p[pl.ds(PADR + t0, CH), :] = a * _sigmoid(g)
            dup[pl.ds(t0, CH), :] = du_ref[pl.ds(t0, CH), :]
            return carry

        lax.fori_loop(0, S // CH, fill, 0)

        def chunk(i, carry):
            t0 = pl.multiple_of(i * CH, CH)
            d = dup[pl.ds(t0, CH), :]
            win_u = u0p[pl.ds(t0, CH + PADR), :]
            win_d = dup[pl.ds(t0, CH + PADR), :]
            du0 = jnp.zeros((CH, CW), F32)
            up_u = _rows_up(win_u)
            for phase, lo, k in _tap_phases(lambda k: PADR - (CK - 1) + k):
                gw_ref[k:k + 1, :] += _colsum(d * up_u(phase)[lo:lo + CH, :])
            up_d = _rows_up(win_d)
            for phase, lo, k in _tap_phases(lambda k: CK - 1 - k):
                du0 = du0 + up_d(phase)[lo:lo + CH, :] * w_ref[k:k + 1, :]
            a = ag_ref[pl.ds(t0, CH), :CW]
            sg = _sigmoid(ag_ref[pl.ds(t0, CH), CW:])
            da = du0 * sg
            dg = du0 * a * sg * (1.0 - sg)
            dag_ref[pl.ds(t0, CH), :CW] = da.astype(BF16)
            dag_ref[pl.ds(t0, CH), CW:] = dg.astype(BF16)
            cs_ref[:, :CW] += _colsum(da)
            cs_ref[:, CW:] += _colsum(dg)
            return carry

        lax.fori_loop(0, S // CH, chunk, 0)

    return pl.pallas_call(
        body, name="conv_bwd", grid=(BL,),
        in_specs=[pl.BlockSpec((S, CW), lambda b: (b, 0)), pl.BlockSpec((S, 2 * CW), lambda b: (b, 0)),
                  pl.BlockSpec((CK, CW), lambda b: (0, 0))],
        out_specs=(pl.BlockSpec((S, 2 * CW), lambda b: (b, 0)),
                   pl.BlockSpec((1, 2 * CW), lambda b: (0, 0)),
                   pl.BlockSpec((PADR, CW), lambda b: (0, 0))),
        out_shape=(SDS((T, 2 * CW), BF16), SDS((1, 2 * CW), F32), SDS((PADR, CW), F32)),
        scratch_shapes=[pltpu.VMEM((S + PADR, CW), F32), pltpu.VMEM((S + PADR, CW), F32)],
    )(du1, ag, conv_w)


def _layer_norm_fwd(z):
    mu = _rowmean(z)
    zc = z - mu
    rstd = lax.rsqrt(_rowmean(zc * zc) + LN_EPS)
    return zc * rstd, rstd


def _layer_norm_bwd(dy, xh, rstd, g):
    dxh = dy * g
    return rstd * (dxh - _rowmean(dxh) - xh * _rowmean(dxh * xh))


def _out_proj_ln1(mixed, w_out, x2, g1, b1):
    def body(a_ref, w_ref, x_ref, g_ref, b_ref, xh_ref, rstd_ref, x1_ref):
        z = ALPHA * x_ref[...] + _dot(a_ref[...], w_ref[...])
        xh, rstd = _layer_norm_fwd(z)
        xh_ref[...] = xh
        rstd_ref[...] = rstd
        x1_ref[...] = (xh * g_ref[...] + b_ref[...]).astype(BF16)

    vec = pl.BlockSpec((1, D), lambda m: (0, 0))
    row = pl.BlockSpec((TM, D), lambda m: (m, 0))
    return pl.pallas_call(
        body, name="out_proj_ln1", grid=(T // TM,),
        in_specs=[row, pl.BlockSpec((D, D), lambda m: (0, 0)), row, vec, vec],
        out_specs=(row, pl.BlockSpec((TM, 1), lambda m: (m, 0)), row),
        out_shape=(SDS((T, D), F32), SDS((T, 1), F32), SDS((T, D), BF16)),
    )(mixed, w_out, x2, g1, b1)


def _seq_start(m):
    return lax.bitwise_and(m, S // TMF - 1) == 0


def _shift_down(x, before, k):
    rolled = pltpu.roll(x, k, 0)
    row = lax.broadcasted_iota(jnp.int32, before.shape, 0)
    head = jnp.where(row < k, pltpu.roll(before, k, 0), rolled[:8])
    return jnp.concatenate([head, rolled[8:]], axis=0)


def _shift_up(x, after, k):
    n = x.shape[0]
    rolled = pltpu.roll(x, n - k, 0)
    row = lax.broadcasted_iota(jnp.int32, after.shape, 0)
    tail = jnp.where(row >= 8 - k, pltpu.roll(after, 8 - k, 0), rolled[n - 8:])
    return jnp.concatenate([rolled[:n - 8], tail], axis=0)


def _ffn_up(x1b, w_up, fcw, fcb):
    def body(x_ref, wg_ref, wv_ref, cwg_ref, cwv_ref, cbg_ref, cbv_ref, up_ref, gv_ref, act_ref, prev_g, prev_v):
        @pl.when(_seq_start(pl.program_id(1)))
        def _():
            prev_g[...] = jnp.zeros_like(prev_g)
            prev_v[...] = jnp.zeros_like(prev_v)

        x = x_ref[...]
        outs = []
        for w_ref, cw_ref, cb_ref, prev, lo in ((wg_ref, cwg_ref, cbg_ref, prev_g, 0), (wv_ref, cwv_ref, cbv_ref, prev_v, FT)):
            u = _dot_nt(x, w_ref[...])
            up_ref[:, lo:lo + FT] = u.astype(BF16)
            before = prev[...]
            y = (cw_ref[2:3, :] * u + cw_ref[1:2, :] * _shift_down(u, before, 1)
                 + cw_ref[0:1, :] * _shift_down(u, before, 2) + cb_ref[...])
            prev[...] = u[TMF - 8:]
            gv_ref[:, lo:lo + FT] = y.astype(BF16)
            outs.append(y)
        gate, val = outs
        act_ref[...] = (gate * _sigmoid(gate) * val).astype(BF16)

    wspec = lambda off: pl.BlockSpec((FT, D), lambda n, m: (n + off, 0))
    cwspec = lambda off: pl.BlockSpec((FK, FT), lambda n, m: (0, n + off))
    cbspec = lambda off: pl.BlockSpec((1, FT), lambda n, m: (0, n + off))
    pair = pl.BlockSpec((TMF, 2 * FT), lambda n, m: (m, n))
    return pl.pallas_call(
        body, name="ffn_up", grid=(NFT, T // TMF),
        in_specs=[pl.BlockSpec((TMF, D), lambda n, m: (m, 0)), wspec(0), wspec(NFT),
                  cwspec(0), cwspec(NFT), cbspec(0), cbspec(NFT)],
        out_specs=(pair, pair, pl.BlockSpec((TMF, FT), lambda n, m: (m, n))),
        out_shape=(SDS((T, 2 * DFF), BF16), SDS((T, 2 * DFF), BF16), SDS((T, DFF), BF16)),
        scratch_shapes=[pltpu.VMEM((8, FT), F32)] * 2,
    )(x1b, w_up, w_up, fcw, fcw, fcb, fcb)


def _ffn_down_loss(act, w_down, xh1, g1, b1, g2, b2, target):
    def body(a_ref, w_ref, xh1_ref, g1_ref, b1_ref, g2_ref, b2_ref, t_ref, dz_ref, loss_ref, gg_ref, gb_ref):
        @pl.when(pl.program_id(0) == 0)
        def _():
            loss_ref[...] = jnp.zeros_like(loss_ref)
            gg_ref[...] = jnp.zeros_like(gg_ref)
            gb_ref[...] = jnp.zeros_like(gb_ref)

        for sub in range(TM // TMF):
            rows = pl.ds(sub * TMF, TMF)
            x1 = xh1_ref[rows, :] * g1_ref[...] + b1_ref[...]
            z = ALPHA * x1 + _dot(a_ref[rows, :], w_ref[...])
            xh, rstd = _layer_norm_fwd(z)
            diff = xh * g2_ref[...] + b2_ref[...] - t_ref[rows, :]
            loss_ref[...] += 0.5 * _colsum(_rowmean(diff * diff))
            dout = diff * (1.0 / D)
            gg_ref[...] += _colsum(dout * xh)
            gb_ref[...] += _colsum(dout)
            dz_ref[rows, :] = _layer_norm_bwd(dout, xh, rstd, g2_ref[...])

    vec = pl.BlockSpec((1, D), lambda m: (0, 0))
    row = pl.BlockSpec((TM, D), lambda m: (m, 0))
    return pl.pallas_call(
        body, name="ffn_down_loss", grid=(T // TM,),
        in_specs=[pl.BlockSpec((TM, DFF), lambda m: (m, 0)), pl.BlockSpec((DFF, D), lambda m: (0, 0)),
                  row, vec, vec, vec, vec, row],
        out_specs=(row, pl.BlockSpec((1, 1), lambda m: (0, 0)), vec, vec),
        out_shape=(SDS((T, D), F32), SDS((1, 1), F32), SDS((1, D), F32), SDS((1, D), F32)),
    )(act, w_down, xh1, g1, b1, g2, b2, target)


def _ffn_down_bwd(dz2, w_down, gv, up, fcw):
    tiles = T // TMF

    def body(dz_ref, wd_ref, gv_ref, up_ref, cwg_ref, cwv_ref,
             dpre_ref, csg_ref, csv_ref, gwg_ref, gwv_ref, next_g, next_v):
        step = pl.program_id(1)
        tile = tiles - 1 - step

        @pl.when(step == 0)
        def _():
            for r in (csg_ref, csv_ref, gwg_ref, gwv_ref, next_g, next_v):
                r[...] = jnp.zeros_like(r)

        seq_end = lax.bitwise_and(tile + 1, S // TMF - 1) == 0
        dact = _dot_nt(dz_ref[...].astype(BF16), wd_ref[...])
        gate = gv_ref[:, :FT].astype(F32)
        val = gv_ref[:, FT:].astype(F32)
        sg = _sigmoid(gate)
        gs = gate * sg
        halves = ((dact * val * (sg + gs * (1.0 - sg)), cwg_ref, csg_ref, gwg_ref, next_g, 0),
                  (dact * gs, cwv_ref, csv_ref, gwv_ref, next_v, FT))
        for d0, cw_ref, cs_ref, gw_ref, nxt, lo in halves:
            after = jnp.where(seq_end, 0.0, nxt[...])
            d1 = _shift_up(d0, after, 1)
            d2 = _shift_up(d0, after, 2)
            nxt[...] = d0[:8]
            dpre_ref[:, lo:lo + FT] = (cw_ref[2:3, :] * d0 + cw_ref[1:2, :] * d1 + cw_ref[0:1, :] * d2).astype(BF16)
            cs_ref[...] += _colsum(d0)
            u = up_ref[:, lo:lo + FT].astype(F32)
            for k, dk in enumerate((d2, d1, d0)):
                gw_ref[k:k + 1, :] += _colsum(dk * u)

    cs = pl.BlockSpec((1, FT), lambda n, m: (0, n))
    gw = pl.BlockSpec((FK, FT), lambda n, m: (0, n))
    cwspec = lambda off: pl.BlockSpec((FK, FT), lambda n, m: (0, n + off))
    pair = pl.BlockSpec((TMF, 2 * FT), lambda n, m: (tiles - 1 - m, n))
    return pl.pallas_call(
        body, name="ffn_down_bwd", grid=(NFT, tiles),
        in_specs=[pl.BlockSpec((TMF, D), lambda n, m: (tiles - 1 - m, 0)), pl.BlockSpec((FT, D), lambda n, m: (n, 0)),
                  pair, pair, cwspec(0), cwspec(NFT)],
        out_specs=(pair, cs, cs, gw, gw),
        out_shape=(SDS((T, 2 * DFF), BF16), SDS((1, DFF), F32), SDS((1, DFF), F32),
                   SDS((FK, DFF), F32), SDS((FK, DFF), F32)),
        scratch_shapes=[pltpu.VMEM((8, FT), F32)] * 2,
    )(dz2, w_down, gv, up, fcw, fcw)


def _ffn_up_bwd_ln1(dpre, w_up, dz2, xh1, rstd1, g1):
    def body(a_ref, w_ref, dz2_ref, xh_ref, rstd_ref, g_ref, dz1_ref, gg_ref, gb_ref):
        @pl.when(pl.program_id(0) == 0)
        def _():
            gg_ref[...] = jnp.zeros_like(gg_ref)
            gb_ref[...] = jnp.zeros_like(gb_ref)

        for sub in range(TM // TMF):
            rows = pl.ds(sub * TMF, TMF)
            dx1 = ALPHA * dz2_ref[rows, :]
            for n in range(NFT):
                for half in range(2):
                    a = a_ref[rows, (2 * n + half) * FT:(2 * n + half + 1) * FT]
                    w = w_ref[pl.ds((half * NFT + n) * FT, FT), :]
                    dx1 = dx1 + _dot(a, w)
            xh = xh_ref[rows, :]
            gg_ref[...] += _colsum(dx1 * xh)
            gb_ref[...] += _colsum(dx1)
            dz1_ref[rows, :] = _layer_norm_bwd(dx1, xh, rstd_ref[rows, :], g_ref[...])

    vec = pl.BlockSpec((1, D), lambda m: (0, 0))
    row = pl.BlockSpec((TM, D), lambda m: (m, 0))
    return pl.pallas_call(
        body, name="ffn_up_bwd_ln1", grid=(T // TM,),
        in_specs=[pl.BlockSpec((TM, 2 * DFF), lambda m: (m, 0)),
                  pl.BlockSpec((2 * DFF, D), lambda m: (0, 0), pipeline_mode=pl.Buffered(1)),
                  row, row, pl.BlockSpec((TM, 1), lambda m: (m, 0)), vec],
        out_specs=(row, vec, vec),
        out_shape=(SDS((T, D), F32), SDS((1, D), F32), SDS((1, D), F32)),
    )(dpre, w_up, dz2, xh1, rstd1, g1)


def _grad_w_up(dpre, x1b):
    tk = 1024

    def body(a_ref, b_ref, o_ref, acc):
        k = pl.program_id(1)

        @pl.when(k == 0)
        def _():
            acc[...] = jnp.zeros_like(acc)

        acc[...] += _dot_tn(a_ref[...], b_ref[...])

        @pl.when(k == T // tk - 1)
        def _():
            o_ref[0] = acc[pl.ds(0, FT), :].astype(o_ref.dtype)
            o_ref[1] = acc[pl.ds(FT, FT), :].astype(o_ref.dtype)

    out = pl.pallas_call(
        body, name="grad_w_up", grid=(NFT, T // tk),
        in_specs=[pl.BlockSpec((tk, 2 * FT), lambda n, k: (k, n)), pl.BlockSpec((tk, D), lambda n, k: (k, 0))],
        out_specs=pl.BlockSpec((2, FT, D), lambda n, k: (0, n, 0)),
        out_shape=SDS((2, DFF, D), GRAD_WIRE),
        scratch_shapes=[pltpu.VMEM((2 * FT, D), F32)],
    )(dpre, x1b)
    return out.reshape(2 * DFF, D)


def _row_tile(rows, cols):
    if rows * cols * 4 <= (1 << 20) or rows % 8:
        return rows
    for t in (256, 176, 128, 88, 64, 32, 16, 8):
        if rows % t == 0 and t * cols * 4 <= (1 << 20):
            return t
    return 8


def _sum8(r, name):
    _, rows, cols = r.shape
    tr = _row_tile(rows, cols)

    def body(r_ref, o_ref):
        acc = r_ref[0].astype(F32)
        for p in range(1, NDEV):
            acc = acc + r_ref[p].astype(F32)
        o_ref[...] = acc

    return pl.pallas_call(
        body, name=name, grid=(rows // tr,),
        in_specs=[pl.BlockSpec((NDEV, tr, cols), lambda i: (0, i, 0))],
        out_specs=pl.BlockSpec((tr, cols), lambda i: (i, 0)),
        out_shape=SDS((rows, cols), F32),
    )(r)


def _sum8_adamw(r, w, m, v, name):
    rows, cols = w.shape
    tr = _row_tile(rows, cols)

    def body(r_ref, w_ref, m_ref, v_ref, g_out, d_ref, nm_ref, nv_ref):
        g_ = r_ref[0].astype(F32)
        for p in range(1, NDEV):
            g_ = g_ + r_ref[p].astype(F32)
        m_ = B1 * m_ref[...] + (1.0 - B1) * g_
        v_ = B2 * v_ref[...] + (1.0 - B2) * jnp.square(g_)
        m_hat = m_ / (1.0 - B1 ** STEP)
        v_hat = v_ / (1.0 - B2 ** STEP)
        g_out[...] = g_
        d_ref[...] = -LR * (m_hat / (jnp.sqrt(v_hat) + AEPS) + WD * w_ref[...])
        nm_ref[...] = m_
        nv_ref[...] = v_

    spec = pl.BlockSpec((tr, cols), lambda i: (i, 0))
    shp = SDS((rows, cols), F32)
    return pl.pallas_call(
        body, name=name, grid=(rows // tr,),
        in_specs=[pl.BlockSpec((NDEV, tr, cols), lambda i: (0, i, 0))] + [spec] * 3, out_specs=(spec,) * 4,
        out_shape=(shp,) * 4,
    )(r, w, m, v)


def _adamw_many(ws, gs, ms, vs, name):
    n = len(ws)

    def body(*refs):
        for i in range(n):
            w_ref, g_ref, m_ref, v_ref, d_ref, nm_ref, nv_ref = refs[i::n]
            g_ = g_ref[...]
            m_ = B1 * m_ref[...] + (1.0 - B1) * g_
            v_ = B2 * v_ref[...] + (1.0 - B2) * jnp.square(g_)
            m_hat = m_ / (1.0 - B1 ** STEP)
            v_hat = v_ / (1.0 - B2 ** STEP)
            d_ref[...] = -LR * (m_hat / (jnp.sqrt(v_hat) + AEPS) + WD * w_ref[...])
            nm_ref[...] = m_
            nv_ref[...] = v_

    shapes = tuple(SDS(w.shape, F32) for w in ws)
    res = pl.pallas_call(body, name=name, out_shape=shapes * 3)(*ws, *gs, *ms, *vs)
    return res[:n], res[n:2 * n], res[2 * n:]


def _local_step(x2, target, rel_table, first_weights, b_in, conv_b, conv_ln_g, conv_ln_b, attn_norm_g,
                conv_norm_g, late_weights, ln1_g, ln1_b, ffn_conv_b, ln2_g, ln2_b, ship_ffn_grads, ship_w_in_grads,
                ship_small_grads):
    buckets = jnp.asarray(_bucket_maps())
    bias = _bias_table(rel_table, buckets)
    xb, cast_done = _cast_x(x2, bias[0, 0, :8, :BLK])
    w_in_t, conv_w, ffn_conv_w = first_weights(cast_done)

    qkv, ag = _proj_in(xb, w_in_t, b_in)
    attn, lse = _attn_fwd(qkv, bias)
    u1 = _conv_fwd(ag, conv_w, conv_b)
    mixed = _mix_fwd(attn, u1, attn_norm_g, conv_norm_g, conv_ln_g, conv_ln_b)
    w_out = late_weights(0, mixed)
    xh1, rstd1, x1b = _out_proj_ln1(mixed, w_out, x2, ln1_g, ln1_b)
    w_up = late_weights(1, x1b)
    up, gv, act = _ffn_up(x1b, w_up, ffn_conv_w, ffn_conv_b)
    w_down = late_weights(2, act)
    dz2, loss, g_ln2_g, g_ln2_b = _ffn_down_loss(act, w_down, xh1, ln1_g, ln1_b, ln2_g, ln2_b, target)

    dpre, cs_g, cs_v, gfw_g, gfw_v = _ffn_down_bwd(dz2, w_down, gv, up, ffn_conv_w)
    g_w_down = _mm_tn(act, dz2, DFF // 2, 512, "grad_w_down")
    dz1, g_ln1_g, g_ln1_b = _ffn_up_bwd_ln1(dpre, w_up, dz2, xh1, rstd1, ln1_g)
    g_w_out = _mm_tn(mixed, dz1, D, 512, "grad_w_out")
    zero = ship_ffn_grads(g_w_down, _grad_w_up(dpre, x1b), g_w_out)
    dattn, du1, g_an, g_cn, g_clg, g_clb, g_cb = _mix_bwd(
        dz1, w_out, attn, u1, attn_norm_g + zero, conv_norm_g, conv_ln_g, conv_ln_b)
    dag, cs_ag, g_conv_w = _conv_bwd(du1, ag, conv_w)
    dq, dk, dv, cs_q, cs_k, cs_v2, dbias = _attn_bwd(qkv, attn, lse, dattn, bias)
    pieces = [dq, dk, dv, dag]
    zero_a = ship_w_in_grads(_grad_w_in(pieces, x2))
    g_rel = _rel_table_grad(dbias, buckets, zero_a)

    grads = dict(
        rel_table=g_rel,
        b_in=jnp.concatenate([cs_q, cs_k, cs_v2, cs_ag], axis=1),
        conv_b=g_cb, conv_ln_g=g_clg, conv_ln_b=g_clb, attn_norm_g=g_an, conv_norm_g=g_cn,
        ln1_g=g_ln1_g, ln1_b=g_ln1_b,
        ffn_conv_b=jnp.concatenate([cs_g, cs_v], axis=1),
        ln2_g=g_ln2_g, ln2_b=g_ln2_b,
        conv_w=g_conv_w[:CK],
        ffn_conv_w=jnp.concatenate([gfw_g, gfw_v], axis=1),
    )
    grads["loss"] = loss
    grad_x = _grad_x(pieces, w_in_t, dz1, ship_small_grads(grads))
    return loss, grad_x


SMALL = (("rel_table", (NBUCKET, NH)), ("b_in", (1, INW)), ("conv_b", (1, CW)), ("conv_ln_g", (1, CW)),
         ("conv_ln_b", (1, CW)), ("attn_norm_g", (1, AW)), ("conv_norm_g", (1, CW)), ("ln1_g", (1, D)),
         ("ln1_b", (1, D)), ("ffn_conv_b", (1, 2 * DFF)), ("ln2_g", (1, D)), ("ln2_b", (1, D)))
SHARDED_SMALL = (("conv_w", (CK, CW)), ("ffn_conv_w", (FK, 2 * DFF)))


def _pack(parts):
    flat = jnp.concatenate([p.reshape(-1) for p in parts])
    tile = 8 * PACK_LANES
    pad = (-flat.shape[0]) % tile
    return jnp.pad(flat, (0, pad)).reshape(-1, PACK_LANES)


def _unpack(packed, specs):
    flat = packed.reshape(-1)
    out, off = {}, 0
    for name, shp in specs:
        size = int(np.prod(shp))
        out[name] = flat[off:off + size].reshape(shp)
        off += size
    return out


def kernel(x, rel_table, w_in, b_in, conv_w, conv_b, conv_ln_g, conv_ln_b, attn_norm_g, conv_norm_g, w_out, ln1_g, ln1_b, w_up, ffn_conv_w, ffn_conv_b, w_down, ln2_g, ln2_b, loss_target, m_rel_table, m_w_in, m_b_in, m_conv_w, m_conv_b, m_conv_ln_g, m_conv_ln_b, m_attn_norm_g, m_conv_norm_g, m_w_out, m_ln1_g, m_ln1_b, m_w_up, m_ffn_conv_w, m_ffn_conv_b, m_w_down, m_ln2_g, m_ln2_b, v_rel_table, v_w_in, v_b_in, v_conv_w, v_conv_b, v_conv_ln_g, v_conv_ln_b, v_attn_norm_g, v_conv_norm_g, v_w_out, v_ln1_g, v_ln1_b, v_w_up, v_ffn_conv_w, v_ffn_conv_b, v_w_down, v_ln2_g, v_ln2_b):
    given = dict(locals())
    me = 4 * lax.axis_index("x") + 2 * lax.axis_index("y") + lax.axis_index("c")

    cols = lambda a: a.transpose(1, 0, 2).reshape(a.shape[1], NDEV * a.shape[2])
    rows = lambda a: a.reshape(NDEV * a.shape[1], a.shape[2])
    stack = lambda a: a.reshape(NDEV, a.shape[0] // NDEV, a.shape[1])

    small_specs = SMALL + SHARDED_SMALL
    packed_specs = small_specs + (("loss", (1, 1)),)
    grad, delta, new_m, new_v = {}, {}, {}, {}

    def adamw_big(n, partials, transposed=False):
        shp = given[n].shape
        to2d = (lambda a: a.reshape(shp[-2], shp[-1]).T) if transposed else (lambda a: a.reshape(shp[-2], shp[-1]))
        back = (lambda a: a.T.reshape(shp)) if transposed else (lambda a: a.reshape(shp))
        g_, d_, m_, v_ = _sum8_adamw(partials, to2d(given[n]), to2d(given["m_" + n]), to2d(given["v_" + n]), "adamw_" + n)
        grad[n], delta[n], new_m[n], new_v[n] = back(g_), back(d_), back(m_), back(v_)
        return d_

    first = 3
    weights_state, zero1 = _exchange_start(
        [(w_in[0].T.astype(BF16), "gather"), (conv_w[0], "gather"), (ffn_conv_w[0], "gather"),
         (w_out[0].astype(BF16), "gather"), (w_up[0].T.astype(BF16), "gather"), (w_down[0].astype(BF16), "gather")],
        "gather_weights_start")

    def first_weights(after):
        lands = _exchange_wait(weights_state, after, "gather_first_wait", only=tuple(range(first)))
        return rows(lands[0]), cols(lands[1]), cols(lands[2])

    def late_weights(i, after):
        return rows(_exchange_wait(weights_state, after, "gather_late_wait_%d" % i, only=(first + i,))[first + i])

    shipped = {}

    def ship_ffn_grads(g_w_down, g_w_up_t, g_w_out):
        shipped["ffn"], zero2 = _exchange_start(
            [(stack(a), "scatter") for a in (g_w_down, g_w_up_t, g_w_out)], "ffn_grads_start")
        return zero2

    def ship_w_in_grads(g_w_in_t):
        shipped["w_in"], zero3 = _exchange_start([(stack(g_w_in_t), "scatter")], "w_in_grads_start")
        return zero3.reshape(1, 1)

    def ship_small_grads(small_grads):
        shipped["small"], zero4 = _exchange_start(
            [(_pack([small_grads[n] for n, _ in packed_specs]), "gather")], "small_grads_start")
        return zero4.reshape(1, 1)

    loss, grad_x = _local_step(
        x.reshape(T, D), loss_target.reshape(T, D), rel_table + zero1, first_weights, b_in, conv_b, conv_ln_g,
        conv_ln_b, attn_norm_g, conv_norm_g, late_weights, ln1_g, ln1_b, ffn_conv_b,
        ln2_g, ln2_b, ship_ffn_grads, ship_w_in_grads, ship_small_grads)

    got_down, got_up, got_out = _exchange_wait(shipped["ffn"], grad_x, "ffn_grads_wait")
    adamw_big("w_down", got_down)
    adamw_big("w_up", got_up, transposed=True)
    last = adamw_big("w_out", got_out)

    (got_in,) = _exchange_wait(shipped["w_in"], last, "w_in_grads_wait")
    (got_small,) = _exchange_wait(shipped["small"], last, "small_grads_wait")
    adamw_big("w_in", got_in, transposed=True)
    small = _unpack(_sum8(got_small, "sum_small"), packed_specs)
    small["conv_w"] = lax.dynamic_slice_in_dim(small["conv_w"], me * (CW // NDEV), CW // NDEV, axis=1)
    small["ffn_conv_w"] = lax.dynamic_slice_in_dim(small["ffn_conv_w"], me * (2 * DFF // NDEV), 2 * DFF // NDEV, axis=1)
    names = [n for n, _ in small_specs]
    two = lambda a: a.reshape(a.shape[-2], a.shape[-1])
    ds, nms, nvs = _adamw_many([two(given[n]) for n in names], [small[n] for n in names],
                               [two(given["m_" + n]) for n in names], [two(given["v_" + n]) for n in names], "adamw_small")
    for n, d_, m_, v_ in zip(names, ds, nms, nvs):
        shp = given[n].shape
        grad[n], delta[n], new_m[n], new_v[n] = small[n].reshape(shp), d_.reshape(shp), m_.reshape(shp), v_.reshape(shp)

    order = ("rel_table", "w_in", "b_in", "conv_w", "conv_b", "conv_ln_g", "conv_ln_b", "attn_norm_g",
             "conv_norm_g", "w_out", "ln1_g", "ln1_b", "w_up", "ffn_conv_w", "ffn_conv_b", "w_down", "ln2_g", "ln2_b")
    return (small["loss"][0, 0], grad_x.reshape(BL, S, D), *[grad[n] for n in order], *[delta[n] for n in order],
            *[new_m[n] for n in order], *[new_v[n] for n in order])
```

```python
import math

import numpy as np
import jax
import jax.numpy as jnp
from jax import lax
from jax.experimental import pallas as pl
from jax.experimental.pallas import tpu as pltpu

F32 = jnp.float32
BF16 = jnp.bfloat16
SDS = jax.ShapeDtypeStruct

NDEV = 8
D = 1024
S = 2048
BL = 2
T = BL * S
NH = 12
HD = 64
AW = NH * HD
CW = D - AW
INW = 3 * AW + 2 * CW
CK = 31
DFF = 2816
FK = 3
BLK = 128
NBUCKET = 32
BRANCHES = ((128, 1), (512, 4), (2048, 16))
ALPHA = 2.0 ** 0.25
LN_EPS = 1e-5
NEG_INF = -1e30
LR, B1, B2, AEPS, WD, STEP = 0.001, 0.9, 0.999, 1e-08, 0.01, 10

TM = 512
FT = 1408
NFT = DFF // FT
TMF = 256
PACK_LANES = 128
GRAD_WIRE = BF16

assert all(w // d == BLK for w, d in BRANCHES)


def _dot(a, b):
    return jnp.dot(a, b, preferred_element_type=F32)


def _dot_nt(a, b):
    return lax.dot_general(a, b, (((1,), (1,)), ((), ())), preferred_element_type=F32)


def _dot_tn(a, b):
    return lax.dot_general(a, b, (((0,), (0,)), ((), ())), preferred_element_type=F32)


def _rowmean(v):
    return jnp.mean(v, axis=-1, keepdims=True)


def _colsum(v):
    return jnp.sum(v, axis=0, keepdims=True)


def _sigmoid(v):
    return jax.nn.sigmoid(v)


_HBM = pl.BlockSpec(memory_space=pltpu.HBM)
_SEM = pl.BlockSpec(memory_space=pltpu.SEMAPHORE)
_EFFECT = pltpu.SideEffectType.DATAFLOW_SIDE_EFFECTING


def _peer_of(k):
    x, y, c = lax.axis_index("x"), lax.axis_index("y"), lax.axis_index("c")
    px = 1 - x if k & 4 else x
    py = 1 - y if k & 2 else y
    pc = 1 - c if k & 1 else c
    return (px, py, pc), 4 * px + 2 * py + pc


def _split_copies(kinds, ins, lands, send_sems, recv_sems, started):
    me = 4 * lax.axis_index("x") + 2 * lax.axis_index("y") + lax.axis_index("c")
    out = []
    for i, kind in enumerate(kinds):
        for k in range(1, NDEV):
            dev, pid = _peer_of(k)
            src = ins[i] if kind == "gather" else ins[i].at[pid]
            dst = lands[i].at[me] if started else lands[i].at[pid]
            slot = i * (NDEV - 1) + k - 1
            out.append(pltpu.make_async_remote_copy(
                src_ref=src, dst_ref=dst, send_sem=send_sems.at[slot], recv_sem=recv_sems.at[slot],
                device_id=dev, device_id_type=pl.DeviceIdType.MESH))
    return out


def _exchange_start(items, name):
    n = len(items)
    kinds = [k for _, k in items]
    srcs = [pltpu.with_memory_space_constraint(a, pltpu.HBM) for a, _ in items]
    lands = []
    for a, k in items:
        shp = (NDEV,) + tuple(a.shape) if k == "gather" else tuple(a.shape)
        lands.append(pltpu.with_memory_space_constraint(lax.empty(shp, a.dtype), pltpu.HBM))

    def body(*refs):
        ins, land_refs = refs[:n], refs[n:2 * n]
        send_sems, recv_sems, own_sems = refs[2 * n:2 * n + 3]
        token = refs[-1]
        for cp in _own_copies(kinds, ins, land_refs, own_sems):
            cp.start()
        for cp in _split_copies(kinds, ins, land_refs, send_sems, recv_sems, True):
            cp.start()
        token[...] = jnp.zeros_like(token)

    sems = pltpu.SemaphoreType.DMA((n * (NDEV - 1),))
    res = pl.pallas_call(
        body, name=name,
        out_shape=(sems, sems, pltpu.SemaphoreType.DMA((n,)),
                   *[pltpu.HBM(a.shape, a.dtype) for a in srcs + lands], SDS((8, 128), F32)),
        in_specs=[_HBM] * (2 * n),
        out_specs=(_SEM, _SEM, _SEM, *[_HBM] * (2 * n), pl.BlockSpec(memory_space=pltpu.VMEM)),
        input_output_aliases={i: 3 + i for i in range(2 * n)},
        compiler_params=pltpu.CompilerParams(has_side_effects=_EFFECT),
    )(*srcs, *lands)
    return (kinds, res[0], res[1], res[2], list(res[3:3 + n]), list(res[3 + n:3 + 2 * n])), res[-1][0, 0]


def _own_copies(kinds, ins, lands, own_sems):
    me = 4 * lax.axis_index("x") + 2 * lax.axis_index("y") + lax.axis_index("c")
    return [pltpu.make_async_copy(ins[i] if kind == "gather" else ins[i].at[me], lands[i].at[me], own_sems.at[i])
            for i, kind in enumerate(kinds)]


def _exchange_wait(state, after, name, only=None):
    kinds, send_sems, recv_sems, own_sems, srcs, lands = state
    n = len(kinds)
    chosen = range(n) if only is None else only

    def body(*refs):
        ins, land_refs = refs[:n], refs[n:2 * n]
        s_sems, r_sems, o_sems = refs[2 * n:2 * n + 3]
        remote = _split_copies(kinds, ins, land_refs, s_sems, r_sems, False)
        own = _own_copies(kinds, ins, land_refs, o_sems)
        for i in chosen:
            for cp in remote[i * (NDEV - 1):(i + 1) * (NDEV - 1)]:
                cp.wait_send()
                cp.wait_recv()
        for i in chosen:
            own[i].wait()

    res = pl.pallas_call(
        body, name=name,
        out_shape=tuple(pltpu.HBM(a.shape, a.dtype) for a in srcs + lands),
        in_specs=[_HBM] * (2 * n) + [_SEM, _SEM, _SEM, pl.BlockSpec(memory_space=pl.ANY)],
        out_specs=tuple([_HBM] * (2 * n)),
        input_output_aliases={i: i for i in range(2 * n)},
        compiler_params=pltpu.CompilerParams(has_side_effects=_EFFECT),
    )(*srcs, *lands, send_sems, recv_sems, own_sems, after)
    srcs[:], lands[:] = res[:n], res[n:]
    return list(lands)


def _cast_x(x2, after):
    def body(x_ref, after_ref, o_ref, done_ref):
        o_ref[...] = x_ref[...].astype(BF16)
        done_ref[...] = jnp.zeros_like(done_ref)

    return pl.pallas_call(
        body, name="cast_x", grid=(T // TM,),
        in_specs=[pl.BlockSpec((TM, D), lambda m: (m, 0)), pl.BlockSpec(memory_space=pl.ANY)],
        out_specs=(pl.BlockSpec((TM, D), lambda m: (m, 0)), pl.BlockSpec((8, 128), lambda m: (0, 0))),
        out_shape=(SDS((T, D), BF16), SDS((8, 128), F32)),
    )(x2, after)


def _proj_in(xb, w_in_t, b_in):
    nq = 3 * AW

    def body(x_ref, w_ref, b_ref, qkv_ref, ag_ref):
        xb = x_ref[...]
        qkv_ref[...] = (_dot_nt(xb, w_ref[pl.ds(0, nq), :]) + b_ref[:, :nq]).astype(BF16)
        ag_ref[...] = _dot_nt(xb, w_ref[pl.ds(nq, 2 * CW), :]) + b_ref[:, nq:]

    return pl.pallas_call(
        body, name="proj_in", grid=(T // TM,),
        in_specs=[pl.BlockSpec((TM, D), lambda m: (m, 0)), pl.BlockSpec((INW, D), lambda m: (0, 0)),
                  pl.BlockSpec((1, INW), lambda m: (0, 0))],
        out_specs=(pl.BlockSpec((TM, nq), lambda m: (m, 0)), pl.BlockSpec((TM, 2 * CW), lambda m: (m, 0))),
        out_shape=(SDS((T, nq), BF16), SDS((T, 2 * CW), F32)),
    )(xb, w_in_t, b_in)


def _grad_x(pieces, w_in_t, dz1, zero):
    widths = [p.shape[1] for p in pieces]

    def body(*refs):
        p_refs = refs[:len(pieces)]
        w_ref, dz_ref, z_ref, o_ref = refs[len(pieces):]
        acc = ALPHA * dz_ref[...] + z_ref[...]
        r0 = 0
        for p_ref, wd in zip(p_refs, widths):
            acc = acc + _dot(p_ref[...], w_ref[pl.ds(r0, wd), :])
            r0 += wd
        o_ref[...] = acc

    row = pl.BlockSpec((TM, D), lambda m: (m, 0))
    return pl.pallas_call(
        body, name="grad_x", grid=(T // TM,),
        in_specs=[pl.BlockSpec((TM, wd), lambda m: (m, 0)) for wd in widths]
        + [pl.BlockSpec((INW, D), lambda m: (0, 0)), row, pl.BlockSpec((1, 1), lambda m: (0, 0))],
        out_specs=row,
        out_shape=SDS((T, D), F32),
    )(*pieces, w_in_t, dz1, zero)


def _grad_w_in(pieces, x2):
    widths = [p.shape[1] for p in pieces]
    tk = 512
    nk = T // tk

    def body(*refs):
        p_refs = refs[:len(pieces)]
        x_ref, o_ref, acc = refs[len(pieces):]
        k = pl.program_id(0)

        @pl.when(k == 0)
        def _():
            acc[...] = jnp.zeros_like(acc)

        xb = x_ref[...].astype(BF16)
        r0 = 0
        for p_ref, wd in zip(p_refs, widths):
            acc[pl.ds(r0, wd), :] += _dot_tn(p_ref[...], xb)
            r0 += wd

        @pl.when(k == nk - 1)
        def _():
            o_ref[...] = acc[...].astype(o_ref.dtype)

    return pl.pallas_call(
        body, name="grad_w_in", grid=(nk,),
        in_specs=[pl.BlockSpec((tk, wd), lambda k: (k, 0)) for wd in widths] + [pl.BlockSpec((tk, D), lambda k: (k, 0))],
        out_specs=pl.BlockSpec((INW, D), lambda k: (0, 0)),
        out_shape=SDS((INW, D), GRAD_WIRE),
        scratch_shapes=[pltpu.VMEM((INW, D), F32)],
    )(*pieces, x2)


def _mm_tn(a, b, tn, tk, name):
    t_, na = a.shape
    nb = b.shape[1]
    nk = t_ // tk

    def body(a_ref, b_ref, o_ref, acc):
        k = pl.program_id(1)

        @pl.when(k == 0)
        def _():
            acc[...] = jnp.zeros_like(acc)

        acc[...] += _dot_tn(a_ref[...].astype(BF16), b_ref[...].astype(BF16))

        @pl.when(k == nk - 1)
        def _():
            o_ref[...] = acc[...].astype(o_ref.dtype)

    return pl.pallas_call(
        body, name=name, grid=(na // tn, nk),
        in_specs=[pl.BlockSpec((tk, tn), lambda n, k: (k, n)),
                  pl.BlockSpec((tk, nb), lambda n, k: (k, 0))],
        out_specs=pl.BlockSpec((tn, nb), lambda n, k: (n, 0)),
        out_shape=SDS((na, nb), GRAD_WIRE),
        scratch_shapes=[pltpu.VMEM((tn, nb), F32)],
    )(a, b)


def _bucket_maps():
    qi = np.arange(BLK)[:, None]
    kj = np.arange(2 * BLK)[None, :]
    steps = np.maximum(qi + BLK - kj, 0)
    exact = NBUCKET // 2
    maps = []
    for _, dil in BRANCHES:
        dist = steps * dil
        d_f = np.maximum(dist, 1).astype(np.float32)
        large = exact + (np.log(d_f / np.float32(exact)) / np.float32(math.log(S / exact))
                         * np.float32(NBUCKET - exact)).astype(np.int32)
        large = np.minimum(large, NBUCKET - 1)
        maps.append(np.where(dist < exact, dist, large).astype(np.int32))
    return np.stack(maps)


def _bias_table(rel_table, buckets):
    def body(t_ref, b_ref, o_ref):
        bk = b_ref[0]
        for h in range(NH):
            acc = jnp.zeros((BLK, 2 * BLK), F32)
            for k in range(NBUCKET):
                acc = jnp.where(bk == k, t_ref[k, h], acc)
            o_ref[0, h] = acc

    return pl.pallas_call(
        body, name="bias_table", grid=(len(BRANCHES),),
        in_specs=[pl.BlockSpec(memory_space=pltpu.SMEM),
                  pl.BlockSpec((1, BLK, 2 * BLK), lambda i: (i, 0, 0))],
        out_specs=pl.BlockSpec((1, NH, BLK, 2 * BLK), lambda i: (i, 0, 0, 0)),
        out_shape=SDS((len(BRANCHES), NH, BLK, 2 * BLK), F32),
    )(rel_table, buckets)


def _rel_table_grad(dbias, buckets, after):
    def body(d_ref, b_ref, after_ref, o_ref):
        for k in range(NBUCKET):
            tot = jnp.zeros((1, 1), F32)
            for br in range(len(BRANCHES)):
                sel = jnp.where(b_ref[br] == k, d_ref[br, 0], 0.0)
                tot = tot + jnp.sum(jnp.sum(sel, axis=1, keepdims=True), axis=0, keepdims=True)
            o_ref[0, :, pl.ds(k, 1)] = tot

    out = pl.pallas_call(
        body, name="rel_table_grad", grid=(NH,),
        in_specs=[pl.BlockSpec((len(BRANCHES), 1, BLK, 2 * BLK), lambda h: (0, h, 0, 0)),
                  pl.BlockSpec((len(BRANCHES), BLK, 2 * BLK), lambda h: (0, 0, 0)),
                  pl.BlockSpec(memory_space=pl.ANY)],
        out_specs=pl.BlockSpec((1, 1, NBUCKET), lambda h: (h, 0, 0)),
        out_shape=SDS((NH, 1, NBUCKET), F32),
    )(dbias, buckets, after)
    return out.reshape(NH, NBUCKET).T


PADK = BLK
SCALE = 1.0 / math.sqrt(HD)
NBLK = S // BLK
ACH = 256


def _branch_geometry(br):
    dil = BRANCHES[br][1]
    sub = S // dil
    return dil, sub, sub // BLK


def _token_rows(br, i):
    dil, _, nblk = _branch_geometry(br)
    if dil == 1:
        return pl.ds(pl.multiple_of(i * BLK, BLK), BLK), i
    r = lax.shift_right_logical(i, nblk.bit_length() - 1)
    n = lax.bitwise_and(i, nblk - 1)
    return pl.ds(r + dil * BLK * n, BLK, stride=dil), n


def _sub_layout_loop(br, step):
    dil, sub, _ = _branch_geometry(br)
    rows = min(sub, ACH)
    nchunk = sub // rows

    def it_step(it, carry):
        if dil == 1:
            src = pl.ds(pl.multiple_of(it * rows, rows), rows)
        else:
            r = lax.shift_right_logical(it, nchunk.bit_length() - 1)
            src = pl.ds(r + dil * rows * lax.bitwise_and(it, nchunk - 1), rows, stride=dil)
        step(src, pl.multiple_of(it * rows, BLK), rows)
        return carry

    lax.fori_loop(0, dil * nchunk, it_step, 0, unroll=True)


def _masked_bias(bias_ref, bm):
    qi = lax.broadcasted_iota(jnp.int32, (BLK, 2 * BLK), 0)
    kj = lax.broadcasted_iota(jnp.int32, (BLK, 2 * BLK), 1)
    first = jnp.logical_and(kj >= BLK, kj - BLK <= qi)
    valid = jnp.logical_or(first, jnp.logical_and(kj < BLK, kj >= qi))
    for br in range(len(BRANCHES)):
        for j in range(2):
            b = bias_ref[br, j]
            bm[br, 1, pl.ds(j * BLK, BLK), :] = jnp.where(valid, b, NEG_INF)
            bm[br, 0, pl.ds(j * BLK, BLK), :] = jnp.where(first, b, NEG_INF)


def _head_split(fn):
    def split(t):
        h0 = lax.broadcasted_iota(jnp.int32, t.shape, 1) < HD
        t = fn(t)
        return jnp.where(h0, t, 0.0).astype(BF16), jnp.where(h0, 0.0, t).astype(BF16)
    return split


def _attn_fwd(qkv, bias):
    nbr = len(BRANCHES)

    def body(q_ref, k_ref, v_ref, bias_ref, o_ref, lse_ref, qf, kf, vf, qs0, qs1, ks, vs, bm, ob, mb, lb):
        qf[...] = q_ref[...].astype(F32)
        kf[...] = k_ref[...].astype(F32)
        vf[...] = v_ref[...].astype(F32)
        _masked_bias(bias_ref, bm)
        ks[pl.ds(0, PADK), :] = jnp.zeros((PADK, BLK), BF16)
        vs[pl.ds(0, PADK), :] = jnp.zeros((PADK, BLK), BF16)
        head0 = lax.broadcasted_iota(jnp.int32, (BLK, BLK), 1) < HD
        split_q = _head_split(lambda t: t * SCALE)

        for br in range(nbr):
            nblk = _branch_geometry(br)[2]

            def stage(src, off, rows):
                qs0[pl.ds(off, rows), :], qs1[pl.ds(off, rows), :] = split_q(qf[src, :])
                ks[pl.ds(PADK + off, rows), :] = kf[src, :].astype(BF16)
                vs[pl.ds(PADK + off, rows), :] = vf[src, :].astype(BF16)

            _sub_layout_loop(br, stage)

            def blk(i, carry, br=br, nblk=nblk):
                base = pl.multiple_of(i * BLK, BLK)
                rows, n = _token_rows(br, i)
                q01 = jnp.concatenate([qs0[pl.ds(base, BLK), :], qs1[pl.ds(base, BLK), :]], axis=0)
                if nblk > 1:
                    kcat = ks[pl.ds(base, 2 * BLK), :]
                    vcat = vs[pl.ds(base, 2 * BLK), :]
                    s = _dot_nt(q01, kcat) + bm[br, jnp.minimum(n, 1)]
                else:
                    kcat = ks[pl.ds(PADK + base, BLK), :]
                    vcat = vs[pl.ds(PADK + base, BLK), :]
                    s = _dot_nt(q01, kcat) + bm[br, 0, :, BLK:]
                mx = jnp.max(s, axis=-1, keepdims=True)
                p = jnp.exp(s - mx)
                ls = jnp.sum(p, axis=-1, keepdims=True)
                o = _dot(p.astype(BF16), vcat)
                ob[br, rows, :] = jnp.where(head0, o[:BLK], o[BLK:])
                mb[br, rows, :] = jnp.where(head0, mx[:BLK], mx[BLK:])
                lb[br, rows, :] = jnp.where(head0, ls[:BLK], ls[BLK:])
                return carry

            lax.fori_loop(0, NBLK, blk, 0, unroll=True)

        def merge(i, carry):
            rows = pl.ds(pl.multiple_of(i * ACH, ACH), ACH)
            m_all = jnp.maximum(jnp.maximum(mb[0, rows, :], mb[1, rows, :]), mb[2, rows, :])
            num = jnp.zeros((ACH, BLK), F32)
            den = jnp.zeros((ACH, BLK), F32)
            for br in range(nbr):
                c = jnp.exp(mb[br, rows, :] - m_all)
                num = num + ob[br, rows, :] * c
                den = den + lb[br, rows, :] * c
            o_ref[rows, :] = num / den
            lse_ref[rows, :] = m_all + jnp.log(den)
            return carry

        lax.fori_loop(0, S // ACH, merge, 0)

    npair = NH // 2
    blk_spec = lambda off: pl.BlockSpec((S, BLK), lambda b, hp: (b, off + hp))
    return pl.pallas_call(
        body, name="attn_fwd", grid=(BL, npair),
        in_specs=[blk_spec(0), blk_spec(npair), blk_spec(2 * npair),
                  pl.BlockSpec((nbr, 2, BLK, 2 * BLK), lambda b, hp: (0, hp, 0, 0))],
        out_specs=(blk_spec(0), blk_spec(0)),
        out_shape=(SDS((T, AW), F32), SDS((T, AW), F32)),
        scratch_shapes=[pltpu.VMEM((S, BLK), F32)] * 3 + [pltpu.VMEM((S, BLK), BF16)] * 2
        + [pltpu.VMEM((PADK + S, BLK), BF16)] * 2 + [pltpu.VMEM((nbr, 2, 2 * BLK, 2 * BLK), F32)]
        + [pltpu.VMEM((nbr, S, BLK), F32)] * 3,
    )(qkv, qkv, qkv, bias)


def _attn_bwd(qkv, attn, lse, dattn, bias):
    nbr = len(BRANCHES)

    def body(q_ref, k_ref, v_ref, o_ref, lse_ref, do_ref, bias_ref,
             dq_ref, dk_ref, dv_ref, sq_ref, sk_ref, sv_ref, db_ref,
             qf, kf, vf, dl, dqa, dka, dva, qs0, qs1, ds0, ds1, ks, vs, dks, dvs, bm):
        b = pl.program_id(1)
        qf[...] = q_ref[...].astype(F32)
        kf[...] = k_ref[...].astype(F32)
        vf[...] = v_ref[...].astype(F32)
        dqa[...] = jnp.zeros_like(dqa)
        dka[...] = jnp.zeros_like(dka)
        dva[...] = jnp.zeros_like(dva)
        _masked_bias(bias_ref, bm)
        ks[pl.ds(0, PADK), :] = jnp.zeros((PADK, BLK), BF16)
        vs[pl.ds(0, PADK), :] = jnp.zeros((PADK, BLK), BF16)
        head0 = lax.broadcasted_iota(jnp.int32, (BLK, BLK), 1) < HD
        split_q = _head_split(lambda t: t * SCALE)
        split_do = _head_split(lambda t: t)

        @pl.when(b == 0)
        def _():
            db_ref[...] = jnp.zeros_like(db_ref)
            sq_ref[...] = jnp.zeros_like(sq_ref)
            sk_ref[...] = jnp.zeros_like(sk_ref)
            sv_ref[...] = jnp.zeros_like(sv_ref)

        def delta(i, carry):
            rows = pl.ds(pl.multiple_of(i * ACH, ACH), ACH)
            prod = do_ref[rows, :] * o_ref[rows, :]
            h0 = lax.broadcasted_iota(jnp.int32, (ACH, BLK), 1) < HD
            d0 = jnp.sum(jnp.where(h0, prod, 0.0), axis=-1, keepdims=True)
            d1 = jnp.sum(jnp.where(h0, 0.0, prod), axis=-1, keepdims=True)
            dl[rows, :] = jnp.where(h0, d0, d1)
            return carry

        lax.fori_loop(0, S // ACH, delta, 0)

        for br in range(nbr):
            nblk = _branch_geometry(br)[2]

            def stage(src, off, rows):
                qs0[pl.ds(off, rows), :], qs1[pl.ds(off, rows), :] = split_q(qf[src, :])
                ds0[pl.ds(off, rows), :], ds1[pl.ds(off, rows), :] = split_do(do_ref[src, :])
                ks[pl.ds(PADK + off, rows), :] = kf[src, :].astype(BF16)
                vs[pl.ds(PADK + off, rows), :] = vf[src, :].astype(BF16)

            _sub_layout_loop(br, stage)
            dks[...] = jnp.zeros_like(dks)
            dvs[...] = jnp.zeros_like(dvs)

            def blk(i, carry, br=br, nblk=nblk):
                base = pl.multiple_of(i * BLK, BLK)
                rows, n = _token_rows(br, i)
                q01 = jnp.concatenate([qs0[pl.ds(base, BLK), :], qs1[pl.ds(base, BLK), :]], axis=0)
                do01 = jnp.concatenate([ds0[pl.ds(base, BLK), :], ds1[pl.ds(base, BLK), :]], axis=0)
                lse_b = lse_ref[rows, :]
                dl_b = dl[rows, :]
                lse01 = jnp.concatenate([lse_b[:, 0:1], lse_b[:, HD:HD + 1]], axis=0)
                dl01 = jnp.concatenate([dl_b[:, 0:1], dl_b[:, HD:HD + 1]], axis=0)
                if nblk > 1:
                    krows = pl.ds(base, 2 * BLK)
                    bias_m = bm[br, jnp.minimum(n, 1)]
                else:
                    krows = pl.ds(PADK + base, BLK)
                    bias_m = bm[br, 0, :, BLK:]
                kcat = ks[krows, :]
                vcat = vs[krows, :]
                p = jnp.exp(_dot_nt(q01, kcat) + bias_m - lse01)
                dsv = p * (_dot_nt(do01, vcat) - dl01)
                if nblk > 1:
                    db_ref[br, 0] += dsv[:BLK]
                    db_ref[br, 1] += dsv[BLK:]
                else:
                    db_ref[br, 0, :, BLK:] += dsv[:BLK]
                    db_ref[br, 1, :, BLK:] += dsv[BLK:]
                dsb = dsv.astype(BF16)
                dq01 = _dot(dsb, kcat)
                dqa[rows, :] = dqa[rows, :] + jnp.where(head0, dq01[:BLK], dq01[BLK:])
                dks[krows, :] = dks[krows, :] + _dot_tn(dsb, q01)
                dvs[krows, :] = dvs[krows, :] + _dot_tn(p.astype(BF16), do01)
                return carry

            lax.fori_loop(0, NBLK, blk, 0, unroll=True)

            def fold(src, off, rows):
                dka[src, :] = dka[src, :] + dks[pl.ds(PADK + off, rows), :]
                dva[src, :] = dva[src, :] + dvs[pl.ds(PADK + off, rows), :]

            _sub_layout_loop(br, fold)

        def flush(i, carry):
            rows = pl.ds(pl.multiple_of(i * ACH, ACH), ACH)
            for acc, out, cs, mul in ((dqa, dq_ref, sq_ref, SCALE), (dka, dk_ref, sk_ref, 1.0), (dva, dv_ref, sv_ref, 1.0)):
                val = acc[rows, :] * mul
                out[rows, :] = val.astype(BF16)
                cs[...] += _colsum(val)
            return carry

        lax.fori_loop(0, S // ACH, flush, 0)

    npair = NH // 2
    blk_spec = lambda off: pl.BlockSpec((S, BLK), lambda hp, b: (b, off + hp))
    sum_spec = pl.BlockSpec((1, BLK), lambda hp, b: (0, hp))
    return pl.pallas_call(
        body, name="attn_bwd", grid=(npair, BL),
        in_specs=[blk_spec(0), blk_spec(npair), blk_spec(2 * npair), blk_spec(0), blk_spec(0), blk_spec(0),
                  pl.BlockSpec((nbr, 2, BLK, 2 * BLK), lambda hp, b: (0, hp, 0, 0))],
        out_specs=(blk_spec(0), blk_spec(0), blk_spec(0), sum_spec, sum_spec, sum_spec,
                   pl.BlockSpec((nbr, 2, BLK, 2 * BLK), lambda hp, b: (0, hp, 0, 0))),
        out_shape=(SDS((T, AW), BF16), SDS((T, AW), BF16), SDS((T, AW), BF16),
                   SDS((1, AW), F32), SDS((1, AW), F32), SDS((1, AW), F32),
                   SDS((nbr, NH, BLK, 2 * BLK), F32)),
        scratch_shapes=[pltpu.VMEM((S, BLK), F32)] * 7 + [pltpu.VMEM((S, BLK), BF16)] * 4
        + [pltpu.VMEM((PADK + S, BLK), BF16)] * 2 + [pltpu.VMEM((PADK + S, BLK), F32)] * 2
        + [pltpu.VMEM((nbr, 2, 2 * BLK, 2 * BLK), F32)],
    )(qkv, qkv, qkv, attn, lse, dattn, bias)


CH = 256
PADR = 32


def _tap_phases(offset_of_tap):
    taps = sorted((offset_of_tap(k) % 8, offset_of_tap(k) - offset_of_tap(k) % 8, k) for k in range(CK))
    assert all(lo + CH + ph <= CH + PADR for ph, lo, _ in taps)
    return taps


def _rows_up(win):
    made = {0: win}

    def get(phase):
        if phase not in made:
            made[phase] = pltpu.roll(win, win.shape[0] - phase, 0)
        return made[phase]
    return get


def _conv_fwd(ag, conv_w, conv_b):
    def body(ag_ref, w_ref, b_ref, u1_ref, u0p):
        u0p[pl.ds(0, PADR), :] = jnp.zeros((PADR, CW), F32)

        def glu(i, carry):
            t0 = pl.multiple_of(i * CH, CH)
            a = ag_ref[pl.ds(t0, CH), :CW]
            g = ag_ref[pl.ds(t0, CH), CW:]
            u0p[pl.ds(PADR + t0, CH), :] = a * _sigmoid(g)
            return carry

        lax.fori_loop(0, S // CH, glu, 0)

        def conv(i, carry):
            t0 = pl.multiple_of(i * CH, CH)
            win = u0p[pl.ds(t0, CH + PADR), :]
            acc = jnp.zeros((CH, CW), F32) + b_ref[...]
            up = _rows_up(win)
            for phase, lo, k in _tap_phases(lambda k: PADR - (CK - 1) + k):
                acc = acc + up(phase)[lo:lo + CH, :] * w_ref[k:k + 1, :]
            u1_ref[pl.ds(t0, CH), :] = acc
            return carry

        lax.fori_loop(0, S // CH, conv, 0)

    return pl.pallas_call(
        body, name="conv_fwd", grid=(BL,),
        in_specs=[pl.BlockSpec((S, 2 * CW), lambda b: (b, 0)),
                  pl.BlockSpec((CK, CW), lambda b: (0, 0)),
                  pl.BlockSpec((1, CW), lambda b: (0, 0))],
        out_specs=pl.BlockSpec((S, CW), lambda b: (b, 0)),
        out_shape=SDS((T, CW), F32),
        scratch_shapes=[pltpu.VMEM((S + PADR, CW), F32)],
    )(ag, conv_w, conv_b)


def _conv_post(u1, cg, cb):
    mu = _rowmean(u1)
    uc = u1 - mu
    rstd = lax.rsqrt(_rowmean(uc * uc) + LN_EPS)
    xh = uc * rstd
    u2 = xh * cg + cb
    sg = _sigmoid(u2)
    return xh, rstd, u2, sg, u2 * sg


def _mix_fwd(attn, u1, ga, gc, cg, cb):
    def body(a_ref, u_ref, ga_ref, gc_ref, cg_ref, cb_ref, o_ref):
        a = a_ref[...]
        ra = lax.rsqrt(_rowmean(a * a) + LN_EPS)
        o_ref[:, :AW] = (a * ra * ga_ref[...]).astype(BF16)
        _, _, _, _, u3 = _conv_post(u_ref[...], cg_ref[...], cb_ref[...])
        rc = lax.rsqrt(_rowmean(u3 * u3) + LN_EPS)
        o_ref[:, AW:] = (u3 * rc * gc_ref[...]).astype(BF16)

    vec = lambda w: pl.BlockSpec((1, w), lambda m: (0, 0))
    return pl.pallas_call(
        body, name="mix_fwd", grid=(T // TM,),
        in_specs=[pl.BlockSpec((TM, AW), lambda m: (m, 0)), pl.BlockSpec((TM, CW), lambda m: (m, 0)),
                  vec(AW), vec(CW), vec(CW), vec(CW)],
        out_specs=pl.BlockSpec((TM, D), lambda m: (m, 0)),
        out_shape=SDS((T, D), BF16),
    )(attn, u1, ga, gc, cg, cb)


def _mix_bwd(dz1, w_out, attn, u1, ga, gc, cg, cb):
    def body(dz_ref, w_ref, a_ref, u_ref, ga_ref, gc_ref, cg_ref, cb_ref,
             da_ref, du_ref, g_an, g_cn, g_lg, g_lb, g_cb):
        @pl.when(pl.program_id(0) == 0)
        def _():
            for r in (g_an, g_cn, g_lg, g_lb, g_cb):
                r[...] = jnp.zeros_like(r)

        dm = _dot_nt(dz_ref[...].astype(BF16), w_ref[...])
        a = a_ref[...]
        dna = dm[:, :AW]
        ra = lax.rsqrt(_rowmean(a * a) + LN_EPS)
        g_an[...] += _colsum(dna * a * ra)
        dat = dna * ga_ref[...]
        da_ref[...] = ra * dat - a * (ra * ra * ra) * _rowmean(dat * a)

        xh, rstd, u2, sg, u3 = _conv_post(u_ref[...], cg_ref[...], cb_ref[...])
        dnc = dm[:, AW:]
        rc = lax.rsqrt(_rowmean(u3 * u3) + LN_EPS)
        g_cn[...] += _colsum(dnc * u3 * rc)
        dut = dnc * gc_ref[...]
        du3 = rc * dut - u3 * (rc * rc * rc) * _rowmean(dut * u3)
        du2 = du3 * sg * (1.0 + u2 * (1.0 - sg))
        g_lg[...] += _colsum(du2 * xh)
        g_lb[...] += _colsum(du2)
        dxh = du2 * cg_ref[...]
        du1 = rstd * (dxh - _rowmean(dxh) - xh * _rowmean(dxh * xh))
        g_cb[...] += _colsum(du1)
        du_ref[...] = du1

    vec = lambda w: pl.BlockSpec((1, w), lambda m: (0, 0))
    return pl.pallas_call(
        body, name="mix_bwd", grid=(T // TM,),
        in_specs=[pl.BlockSpec((TM, D), lambda m: (m, 0)), pl.BlockSpec((D, D), lambda m: (0, 0)),
                  pl.BlockSpec((TM, AW), lambda m: (m, 0)),
                  pl.BlockSpec((TM, CW), lambda m: (m, 0)), vec(AW), vec(CW), vec(CW), vec(CW)],
        out_specs=(pl.BlockSpec((TM, AW), lambda m: (m, 0)), pl.BlockSpec((TM, CW), lambda m: (m, 0)),
                   vec(AW), vec(CW), vec(CW), vec(CW), vec(CW)),
        out_shape=(SDS((T, AW), F32), SDS((T, CW), F32),
                   SDS((1, AW), F32), SDS((1, CW), F32), SDS((1, CW), F32), SDS((1, CW), F32), SDS((1, CW), F32)),
    )(dz1, w_out, attn, u1, ga, gc, cg, cb)


def _conv_bwd(du1, ag, conv_w):
    def body(du_ref, ag_ref, w_ref, dag_ref, cs_ref, gw_ref, u0p, dup):
        @pl.when(pl.program_id(0) == 0)
        def _():
            cs_ref[...] = jnp.zeros_like(cs_ref)
            gw_ref[...] = jnp.zeros_like(gw_ref)

        u0p[pl.ds(0, PADR), :] = jnp.zeros((PADR, CW), F32)
        dup[pl.ds(S, PADR), :] = jnp.zeros((PADR, CW), F32)

        def fill(i, carry):
            t0 = pl.multiple_of(i * CH, CH)
            a = ag_ref[pl.ds(t0, CH), :CW]
            g = ag_ref[pl.ds(t0, CH), CW:]
            u0p[pl.ds(PADR + t0, CH), :] = a * _sigmoid(g)
            dup[pl.ds(t0, CH), :] = du_ref[pl.ds(t0, CH), :]
            return carry

        lax.fori_loop(0, S // CH, fill, 0)

        def chunk(i, carry):
            t0 = pl.multiple_of(i * CH, CH)
            d = dup[pl.ds(t0, CH), :]
            win_u = u0p[pl.ds(t0, CH + PADR), :]
            win_d = dup[pl.ds(t0, CH + PADR), :]
            du0 = jnp.zeros((CH, CW), F32)
            up_u = _rows_up(win_u)
            for phase, lo, k in _tap_phases(lambda k: PADR - (CK - 1) + k):
                gw_ref[k:k + 1, :] += _colsum(d * up_u(phase)[lo:lo + CH, :])
            up_d = _rows_up(win_d)
            for phase, lo, k in _tap_phases(lambda k: CK - 1 - k):
                du0 = du0 + up_d(phase)[lo:lo + CH, :] * w_ref[k:k + 1, :]
            a = ag_ref[pl.ds(t0, CH), :CW]
            sg = _sigmoid(ag_ref[pl.ds(t0, CH), CW:])
            da = du0 * sg
            dg = du0 * a * sg * (1.0 - sg)
            dag_ref[pl.ds(t0, CH), :CW] = da.astype(BF16)
            dag_ref[pl.ds(t0, CH), CW:] = dg.astype(BF16)
            cs_ref[:, :CW] += _colsum(da)
            cs_ref[:, CW:] += _colsum(dg)
            return carry

        lax.fori_loop(0, S // CH, chunk, 0)

    return pl.pallas_call(
        body, name="conv_bwd", grid=(BL,),
        in_specs=[pl.BlockSpec((S, CW), lambda b: (b, 0)), pl.BlockSpec((S, 2 * CW), lambda b: (b, 0)),
                  pl.BlockSpec((CK, CW), lambda b: (0, 0))],
        out_specs=(pl.BlockSpec((S, 2 * CW), lambda b: (b, 0)),
                   pl.BlockSpec((1, 2 * CW), lambda b: (0, 0)),
                   pl.BlockSpec((PADR, CW), lambda b: (0, 0))),
        out_shape=(SDS((T, 2 * CW), BF16), SDS((1, 2 * CW), F32), SDS((PADR, CW), F32)),
        scratch_shapes=[pltpu.VMEM((S + PADR, CW), F32), pltpu.VMEM((S + PADR, CW), F32)],
    )(du1, ag, conv_w)


def _layer_norm_fwd(z):
    mu = _rowmean(z)
    zc = z - mu
    rstd = lax.rsqrt(_rowmean(zc * zc) + LN_EPS)
    return zc * rstd, rstd


def _layer_norm_bwd(dy, xh, rstd, g):
    dxh = dy * g
    return rstd * (dxh - _rowmean(dxh) - xh * _rowmean(dxh * xh))


def _out_proj_ln1(mixed, w_out, x2, g1, b1):
    def body(a_ref, w_ref, x_ref, g_ref, b_ref, xh_ref, rstd_ref, x1_ref):
        z = ALPHA * x_ref[...] + _dot(a_ref[...], w_ref[...])
        xh, rstd = _layer_norm_fwd(z)
        xh_ref[...] = xh
        rstd_ref[...] = rstd
        x1_ref[...] = (xh * g_ref[...] + b_ref[...]).astype(BF16)

    vec = pl.BlockSpec((1, D), lambda m: (0, 0))
    row = pl.BlockSpec((TM, D), lambda m: (m, 0))
    return pl.pallas_call(
        body, name="out_proj_ln1", grid=(T // TM,),
        in_specs=[row, pl.BlockSpec((D, D), lambda m: (0, 0)), row, vec, vec],
        out_specs=(row, pl.BlockSpec((TM, 1), lambda m: (m, 0)), row),
        out_shape=(SDS((T, D), F32), SDS((T, 1), F32), SDS((T, D), BF16)),
    )(mixed, w_out, x2, g1, b1)


def _seq_start(m):
    return lax.bitwise_and(m, S // TMF - 1) == 0


def _shift_down(x, before, k):
    rolled = pltpu.roll(x, k, 0)
    row = lax.broadcasted_iota(jnp.int32, before.shape, 0)
    head = jnp.where(row < k, pltpu.roll(before, k, 0), rolled[:8])
    return jnp.concatenate([head, rolled[8:]], axis=0)


def _shift_up(x, after, k):
    n = x.shape[0]
    rolled = pltpu.roll(x, n - k, 0)
    row = lax.broadcasted_iota(jnp.int32, after.shape, 0)
    tail = jnp.where(row >= 8 - k, pltpu.roll(after, 8 - k, 0), rolled[n - 8:])
    return jnp.concatenate([rolled[:n - 8], tail], axis=0)


def _ffn_up(x1b, w_up, fcw, fcb):
    def body(x_ref, wg_ref, wv_ref, cwg_ref, cwv_ref, cbg_ref, cbv_ref, up_ref, gv_ref, act_ref, prev_g, prev_v):
        @pl.when(_seq_start(pl.program_id(1)))
        def _():
            prev_g[...] = jnp.zeros_like(prev_g)
            prev_v[...] = jnp.zeros_like(prev_v)

        x = x_ref[...]
        outs = []
        for w_ref, cw_ref, cb_ref, prev, lo in ((wg_ref, cwg_ref, cbg_ref, prev_g, 0), (wv_ref, cwv_ref, cbv_ref, prev_v, FT)):
            u = _dot_nt(x, w_ref[...])
            up_ref[:, lo:lo + FT] = u.astype(BF16)
            before = prev[...]
            y = (cw_ref[2:3, :] * u + cw_ref[1:2, :] * _shift_down(u, before, 1)
                 + cw_ref[0:1, :] * _shift_down(u, before, 2) + cb_ref[...])
            prev[...] = u[TMF - 8:]
            gv_ref[:, lo:lo + FT] = y.astype(BF16)
            outs.append(y)
        gate, val = outs
        act_ref[...] = (gate * _sigmoid(gate) * val).astype(BF16)

    wspec = lambda off: pl.BlockSpec((FT, D), lambda n, m: (n + off, 0))
    cwspec = lambda off: pl.BlockSpec((FK, FT), lambda n, m: (0, n + off))
    cbspec = lambda off: pl.BlockSpec((1, FT), lambda n, m: (0, n + off))
    pair = pl.BlockSpec((TMF, 2 * FT), lambda n, m: (m, n))
    return pl.pallas_call(
        body, name="ffn_up", grid=(NFT, T // TMF),
        in_specs=[pl.BlockSpec((TMF, D), lambda n, m: (m, 0)), wspec(0), wspec(NFT),
                  cwspec(0), cwspec(NFT), cbspec(0), cbspec(NFT)],
        out_specs=(pair, pair, pl.BlockSpec((TMF, FT), lambda n, m: (m, n))),
        out_shape=(SDS((T, 2 * DFF), BF16), SDS((T, 2 * DFF), BF16), SDS((T, DFF), BF16)),
        scratch_shapes=[pltpu.VMEM((8, FT), F32)] * 2,
    )(x1b, w_up, w_up, fcw, fcw, fcb, fcb)


def _ffn_down_loss(act, w_down, xh1, g1, b1, g2, b2, target):
    def body(a_ref, w_ref, xh1_ref, g1_ref, b1_ref, g2_ref, b2_ref, t_ref, dz_ref, loss_ref, gg_ref, gb_ref):
        @pl.when(pl.program_id(0) == 0)
        def _():
            loss_ref[...] = jnp.zeros_like(loss_ref)
            gg_ref[...] = jnp.zeros_like(gg_ref)
            gb_ref[...] = jnp.zeros_like(gb_ref)

        for sub in range(TM // TMF):
            rows = pl.ds(sub * TMF, TMF)
            x1 = xh1_ref[rows, :] * g1_ref[...] + b1_ref[...]
            z = ALPHA * x1 + _dot(a_ref[rows, :], w_ref[...])
            xh, rstd = _layer_norm_fwd(z)
            diff = xh * g2_ref[...] + b2_ref[...] - t_ref[rows, :]
            loss_ref[...] += 0.5 * _colsum(_rowmean(diff * diff))
            dout = diff * (1.0 / D)
            gg_ref[...] += _colsum(dout * xh)
            gb_ref[...] += _colsum(dout)
            dz_ref[rows, :] = _layer_norm_bwd(dout, xh, rstd, g2_ref[...])

    vec = pl.BlockSpec((1, D), lambda m: (0, 0))
    row = pl.BlockSpec((TM, D), lambda m: (m, 0))
    return pl.pallas_call(
        body, name="ffn_down_loss", grid=(T // TM,),
        in_specs=[pl.BlockSpec((TM, DFF), lambda m: (m, 0)), pl.BlockSpec((DFF, D), lambda m: (0, 0)),
                  row, vec, vec, vec, vec, row],
        out_specs=(row, pl.BlockSpec((1, 1), lambda m: (0, 0)), vec, vec),
        out_shape=(SDS((T, D), F32), SDS((1, 1), F32), SDS((1, D), F32), SDS((1, D), F32)),
    )(act, w_down, xh1, g1, b1, g2, b2, target)


def _ffn_down_bwd(dz2, w_down, gv, up, fcw):
    tiles = T // TMF

    def body(dz_ref, wd_ref, gv_ref, up_ref, cwg_ref, cwv_ref,
             dpre_ref, csg_ref, csv_ref, gwg_ref, gwv_ref, next_g, next_v):
        step = pl.program_id(1)
        tile = tiles - 1 - step

        @pl.when(step == 0)
        def _():
            for r in (csg_ref, csv_ref, gwg_ref, gwv_ref, next_g, next_v):
                r[...] = jnp.zeros_like(r)

        seq_end = lax.bitwise_and(tile + 1, S // TMF - 1) == 0
        dact = _dot_nt(dz_ref[...].astype(BF16), wd_ref[...])
        gate = gv_ref[:, :FT].astype(F32)
        val = gv_ref[:, FT:].astype(F32)
        sg = _sigmoid(gate)
        gs = gate * sg
        halves = ((dact * val * (sg + gs * (1.0 - sg)), cwg_ref, csg_ref, gwg_ref, next_g, 0),
                  (dact * gs, cwv_ref, csv_ref, gwv_ref, next_v, FT))
        for d0, cw_ref, cs_ref, gw_ref, nxt, lo in halves:
            after = jnp.where(seq_end, 0.0, nxt[...])
            d1 = _shift_up(d0, after, 1)
            d2 = _shift_up(d0, after, 2)
            nxt[...] = d0[:8]
            dpre_ref[:, lo:lo + FT] = (cw_ref[2:3, :] * d0 + cw_ref[1:2, :] * d1 + cw_ref[0:1, :] * d2).astype(BF16)
            cs_ref[...] += _colsum(d0)
            u = up_ref[:, lo:lo + FT].astype(F32)
            for k, dk in enumerate((d2, d1, d0)):
                gw_ref[k:k + 1, :] += _colsum(dk * u)

    cs = pl.BlockSpec((1, FT), lambda n, m: (0, n))
    gw = pl.BlockSpec((FK, FT), lambda n, m: (0, n))
    cwspec = lambda off: pl.BlockSpec((FK, FT), lambda n, m: (0, n + off))
    pair = pl.BlockSpec((TMF, 2 * FT), lambda n, m: (tiles - 1 - m, n))
    return pl.pallas_call(
        body, name="ffn_down_bwd", grid=(NFT, tiles),
        in_specs=[pl.BlockSpec((TMF, D), lambda n, m: (tiles - 1 - m, 0)), pl.BlockSpec((FT, D), lambda n, m: (n, 0)),
                  pair, pair, cwspec(0), cwspec(NFT)],
        out_specs=(pair, cs, cs, gw, gw),
        out_shape=(SDS((T, 2 * DFF), BF16), SDS((1, DFF), F32), SDS((1, DFF), F32),
                   SDS((FK, DFF), F32), SDS((FK, DFF), F32)),
        scratch_shapes=[pltpu.VMEM((8, FT), F32)] * 2,
    )(dz2, w_down, gv, up, fcw, fcw)


def _ffn_up_bwd_ln1(dpre, w_up, dz2, xh1, rstd1, g1):
    def body(a_ref, w_ref, dz2_ref, xh_ref, rstd_ref, g_ref, dz1_ref, gg_ref, gb_ref):
        @pl.when(pl.program_id(0) == 0)
        def _():
            gg_ref[...] = jnp.zeros_like(gg_ref)
            gb_ref[...] = jnp.zeros_like(gb_ref)

        for sub in range(TM // TMF):
            rows = pl.ds(sub * TMF, TMF)
            dx1 = ALPHA * dz2_ref[rows, :]
            for n in range(NFT):
                for half in range(2):
                    a = a_ref[rows, (2 * n + half) * FT:(2 * n + half + 1) * FT]
                    w = w_ref[pl.ds((half * NFT + n) * FT, FT), :]
                    dx1 = dx1 + _dot(a, w)
            xh = xh_ref[rows, :]
            gg_ref[...] += _colsum(dx1 * xh)
            gb_ref[...] += _colsum(dx1)
            dz1_ref[rows, :] = _layer_norm_bwd(dx1, xh, rstd_ref[rows, :], g_ref[...])

    vec = pl.BlockSpec((1, D), lambda m: (0, 0))
    row = pl.BlockSpec((TM, D), lambda m: (m, 0))
    return pl.pallas_call(
        body, name="ffn_up_bwd_ln1", grid=(T // TM,),
        in_specs=[pl.BlockSpec((TM, 2 * DFF), lambda m: (m, 0)),
                  pl.BlockSpec((2 * DFF, D), lambda m: (0, 0), pipeline_mode=pl.Buffered(1)),
                  row, row, pl.BlockSpec((TM, 1), lambda m: (m, 0)), vec],
        out_specs=(row, vec, vec),
        out_shape=(SDS((T, D), F32), SDS((1, D), F32), SDS((1, D), F32)),
    )(dpre, w_up, dz2, xh1, rstd1, g1)


def _grad_w_up(dpre, x1b):
    tk = 1024

    def body(a_ref, b_ref, o_ref, acc):
        k = pl.program_id(1)

        @pl.when(k == 0)
        def _():
            acc[...] = jnp.zeros_like(acc)

        acc[...] += _dot_tn(a_ref[...], b_ref[...])

        @pl.when(k == T // tk - 1)
        def _():
            o_ref[0] = acc[pl.ds(0, FT), :].astype(o_ref.dtype)
            o_ref[1] = acc[pl.ds(FT, FT), :].astype(o_ref.dtype)

    out = pl.pallas_call(
        body, name="grad_w_up", grid=(NFT, T // tk),
        in_specs=[pl.BlockSpec((tk, 2 * FT), lambda n, k: (k, n)), pl.BlockSpec((tk, D), lambda n, k: (k, 0))],
        out_specs=pl.BlockSpec((2, FT, D), lambda n, k: (0, n, 0)),
        out_shape=SDS((2, DFF, D), GRAD_WIRE),
        scratch_shapes=[pltpu.VMEM((2 * FT, D), F32)],
    )(dpre, x1b)
    return out.reshape(2 * DFF, D)


def _row_tile(rows, cols):
    if rows * cols * 4 <= (1 << 20) or rows % 8:
        return rows
    for t in (256, 176, 128, 88, 64, 32, 16, 8):
        if rows % t == 0 and t * cols * 4 <= (1 << 20):
            return t
    return 8


def _sum8(r, name):
    _, rows, cols = r.shape
    tr = _row_tile(rows, cols)

    def body(r_ref, o_ref):
        acc = r_ref[0].astype(F32)
        for p in range(1, NDEV):
            acc = acc + r_ref[p].astype(F32)
        o_ref[...] = acc

    return pl.pallas_call(
        body, name=name, grid=(rows // tr,),
        in_specs=[pl.BlockSpec((NDEV, tr, cols), lambda i: (0, i, 0))],
        out_specs=pl.BlockSpec((tr, cols), lambda i: (i, 0)),
        out_shape=SDS((rows, cols), F32),
    )(r)


def _sum8_adamw(r, w, m, v, name):
    rows, cols = w.shape
    tr = _row_tile(rows, cols)

    def body(r_ref, w_ref, m_ref, v_ref, g_out, d_ref, nm_ref, nv_ref):
        g_ = r_ref[0].astype(F32)
        for p in range(1, NDEV):
            g_ = g_ + r_ref[p].astype(F32)
        m_ = B1 * m_ref[...] + (1.0 - B1) * g_
        v_ = B2 * v_ref[...] + (1.0 - B2) * jnp.square(g_)
        m_hat = m_ / (1.0 - B1 ** STEP)
        v_hat = v_ / (1.0 - B2 ** STEP)
        g_out[...] = g_
        d_ref[...] = -LR * (m_hat / (jnp.sqrt(v_hat) + AEPS) + WD * w_ref[...])
        nm_ref[...] = m_
        nv_ref[...] = v_

    spec = pl.BlockSpec((tr, cols), lambda i: (i, 0))
    shp = SDS((rows, cols), F32)
    return pl.pallas_call(
        body, name=name, grid=(rows // tr,),
        in_specs=[pl.BlockSpec((NDEV, tr, cols), lambda i: (0, i, 0))] + [spec] * 3, out_specs=(spec,) * 4,
        out_shape=(shp,) * 4,
    )(r, w, m, v)


def _adamw_many(ws, gs, ms, vs, name):
    n = len(ws)

    def body(*refs):
        for i in range(n):
            w_ref, g_ref, m_ref, v_ref, d_ref, nm_ref, nv_ref = refs[i::n]
            g_ = g_ref[...]
            m_ = B1 * m_ref[...] + (1.0 - B1) * g_
            v_ = B2 * v_ref[...] + (1.0 - B2) * jnp.square(g_)
            m_hat = m_ / (1.0 - B1 ** STEP)
            v_hat = v_ / (1.0 - B2 ** STEP)
            d_ref[...] = -LR * (m_hat / (jnp.sqrt(v_hat) + AEPS) + WD * w_ref[...])
            nm_ref[...] = m_
            nv_ref[...] = v_

    shapes = tuple(SDS(w.shape, F32) for w in ws)
    res = pl.pallas_call(body, name=name, out_shape=shapes * 3)(*ws, *gs, *ms, *vs)
    return res[:n], res[n:2 * n], res[2 * n:]


def _local_step(x2, target, rel_table, first_weights, b_in, conv_b, conv_ln_g, conv_ln_b, attn_norm_g,
                conv_norm_g, late_weights, ln1_g, ln1_b, ffn_conv_b, ln2_g, ln2_b, ship_ffn_grads, ship_w_in_grads,
                ship_small_grads):
    buckets = jnp.asarray(_bucket_maps())
    bias = _bias_table(rel_table, buckets)
    xb, cast_done = _cast_x(x2, bias[0, 0, :8, :BLK])
    w_in_t, conv_w, ffn_conv_w = first_weights(cast_done)

    qkv, ag = _proj_in(xb, w_in_t, b_in)
    attn, lse = _attn_fwd(qkv, bias)
    u1 = _conv_fwd(ag, conv_w, conv_b)
    mixed = _mix_fwd(attn, u1, attn_norm_g, conv_norm_g, conv_ln_g, conv_ln_b)
    w_out = late_weights(0, mixed)
    xh1, rstd1, x1b = _out_proj_ln1(mixed, w_out, x2, ln1_g, ln1_b)
    w_up = late_weights(1, x1b)
    up, gv, act = _ffn_up(x1b, w_up, ffn_conv_w, ffn_conv_b)
    w_down = late_weights(2, act)
    dz2, loss, g_ln2_g, g_ln2_b = _ffn_down_loss(act, w_down, xh1, ln1_g, ln1_b, ln2_g, ln2_b, target)

    dpre, cs_g, cs_v, gfw_g, gfw_v = _ffn_down_bwd(dz2, w_down, gv, up, ffn_conv_w)
    g_w_down = _mm_tn(act, dz2, DFF // 2, 512, "grad_w_down")
    dz1, g_ln1_g, g_ln1_b = _ffn_up_bwd_ln1(dpre, w_up, dz2, xh1, rstd1, ln1_g)
    g_w_out = _mm_tn(mixed, dz1, D, 512, "grad_w_out")
    zero = ship_ffn_grads(g_w_down, _grad_w_up(dpre, x1b), g_w_out)
    dattn, du1, g_an, g_cn, g_clg, g_clb, g_cb = _mix_bwd(
        dz1, w_out, attn, u1, attn_norm_g + zero, conv_norm_g, conv_ln_g, conv_ln_b)
    dag, cs_ag, g_conv_w = _conv_bwd(du1, ag, conv_w)
    dq, dk, dv, cs_q, cs_k, cs_v2, dbias = _attn_bwd(qkv, attn, lse, dattn, bias)
    pieces = [dq, dk, dv, dag]
    zero_a = ship_w_in_grads(_grad_w_in(pieces, x2))
    g_rel = _rel_table_grad(dbias, buckets, zero_a)

    grads = dict(
        rel_table=g_rel,
        b_in=jnp.concatenate([cs_q, cs_k, cs_v2, cs_ag], axis=1),
        conv_b=g_cb, conv_ln_g=g_clg, conv_ln_b=g_clb, attn_norm_g=g_an, conv_norm_g=g_cn,
        ln1_g=g_ln1_g, ln1_b=g_ln1_b,
        ffn_conv_b=jnp.concatenate([cs_g, cs_v], axis=1),
        ln2_g=g_ln2_g, ln2_b=g_ln2_b,
        conv_w=g_conv_w[:CK],
        ffn_conv_w=jnp.concatenate([gfw_g, gfw_v], axis=1),
    )
    grads["loss"] = loss
    grad_x = _grad_x(pieces, w_in_t, dz1, ship_small_grads(grads))
    return loss, grad_x


SMALL = (("rel_table", (NBUCKET, NH)), ("b_in", (1, INW)), ("conv_b", (1, CW)), ("conv_ln_g", (1, CW)),
         ("conv_ln_b", (1, CW)), ("attn_norm_g", (1, AW)), ("conv_norm_g", (1, CW)), ("ln1_g", (1, D)),
         ("ln1_b", (1, D)), ("ffn_conv_b", (1, 2 * DFF)), ("ln2_g", (1, D)), ("ln2_b", (1, D)))
SHARDED_SMALL = (("conv_w", (CK, CW)), ("ffn_conv_w", (FK, 2 * DFF)))


def _pack(parts):
    flat = jnp.concatenate([p.reshape(-1) for p in parts])
    tile = 8 * PACK_LANES
    pad = (-flat.shape[0]) % tile
    return jnp.pad(flat, (0, pad)).reshape(-1, PACK_LANES)


def _unpack(packed, specs):
    flat = packed.reshape(-1)
    out, off = {}, 0
    for name, shp in specs:
        size = int(np.prod(shp))
        out[name] = flat[off:off + size].reshape(shp)
        off += size
    return out


def kernel(x, rel_table, w_in, b_in, conv_w, conv_b, conv_ln_g, conv_ln_b, attn_norm_g, conv_norm_g, w_out, ln1_g, ln1_b, w_up, ffn_conv_w, ffn_conv_b, w_down, ln2_g, ln2_b, loss_target, m_rel_table, m_w_in, m_b_in, m_conv_w, m_conv_b, m_conv_ln_g, m_conv_ln_b, m_attn_norm_g, m_conv_norm_g, m_w_out, m_ln1_g, m_ln1_b, m_w_up, m_ffn_conv_w, m_ffn_conv_b, m_w_down, m_ln2_g, m_ln2_b, v_rel_table, v_w_in, v_b_in, v_conv_w, v_conv_b, v_conv_ln_g, v_conv_ln_b, v_attn_norm_g, v_conv_norm_g, v_w_out, v_ln1_g, v_ln1_b, v_w_up, v_ffn_conv_w, v_ffn_conv_b, v_w_down, v_ln2_g, v_ln2_b):
    given = dict(locals())
    me = 4 * lax.axis_index("x") + 2 * lax.axis_index("y") + lax.axis_index("c")

    cols = lambda a: a.transpose(1, 0, 2).reshape(a.shape[1], NDEV * a.shape[2])
    rows = lambda a: a.reshape(NDEV * a.shape[1], a.shape[2])
    stack = lambda a: a.reshape(NDEV, a.shape[0] // NDEV, a.shape[1])

    small_specs = SMALL + SHARDED_SMALL
    packed_specs = small_specs + (("loss", (1, 1)),)
    grad, delta, new_m, new_v = {}, {}, {}, {}

    def adamw_big(n, partials, transposed=False):
        shp = given[n].shape
        to2d = (lambda a: a.reshape(shp[-2], shp[-1]).T) if transposed else (lambda a: a.reshape(shp[-2], shp[-1]))
        back = (lambda a: a.T.reshape(shp)) if transposed else (lambda a: a.reshape(shp))
        g_, d_, m_, v_ = _sum8_adamw(partials, to2d(given[n]), to2d(given["m_" + n]), to2d(given["v_" + n]), "adamw_" + n)
        grad[n], delta[n], new_m[n], new_v[n] = back(g_), back(d_), back(m_), back(v_)
        return d_

    first_state, zero0 = _exchange_start(
        [(w_in[0].T.astype(BF16), "gather"), (conv_w[0], "gather"), (ffn_conv_w[0], "gather")], "gather_first_start")

    def first_weights(after):
        lands = _exchange_wait(first_state, after, "gather_first_wait")
        return rows(lands[0]), cols(lands[1]), cols(lands[2])

    late_state, zero1 = _exchange_start(
        [(w_out[0].astype(BF16), "gather"), (w_up[0].T.astype(BF16) + zero0.astype(BF16), "gather"),
         (w_down[0].astype(BF16), "gather")], "gather_late_start")

    def late_weights(i, after):
        return rows(_exchange_wait(late_state, after, "gather_late_wait_%d" % i, only=(i,))[i])

    shipped = {}

    def ship_ffn_grads(g_w_down, g_w_up_t, g_w_out):
        shipped["ffn"], zero2 = _exchange_start(
            [(stack(a), "scatter") for a in (g_w_down, g_w_up_t, g_w_out)], "ffn_grads_start")
        return zero2

    def ship_w_in_grads(g_w_in_t):
        shipped["w_in"], zero3 = _exchange_start([(stack(g_w_in_t), "scatter")], "w_in_grads_start")
        return zero3.reshape(1, 1)

    def ship_small_grads(small_grads):
        shipped["small"], zero4 = _exchange_start(
            [(_pack([small_grads[n] for n, _ in packed_specs]), "gather")], "small_grads_start")
        return zero4.reshape(1, 1)

    loss, grad_x = _local_step(
        x.reshape(T, D), loss_target.reshape(T, D), rel_table + zero1, first_weights, b_in, conv_b, conv_ln_g,
        conv_ln_b, attn_norm_g, conv_norm_g, late_weights, ln1_g, ln1_b, ffn_conv_b,
        ln2_g, ln2_b, ship_ffn_grads, ship_w_in_grads, ship_small_grads)

    got_down, got_up, got_out = _exchange_wait(shipped["ffn"], grad_x, "ffn_grads_wait")
    adamw_big("w_down", got_down)
    adamw_big("w_up", got_up, transposed=True)
    last = adamw_big("w_out", got_out)

    (got_in,) = _exchange_wait(shipped["w_in"], last, "w_in_grads_wait")
    (got_small,) = _exchange_wait(shipped["small"], last, "small_grads_wait")
    adamw_big("w_in", got_in, transposed=True)
    small = _unpack(_sum8(got_small, "sum_small"), packed_specs)
    small["conv_w"] = lax.dynamic_slice_in_dim(small["conv_w"], me * (CW // NDEV), CW // NDEV, axis=1)
    small["ffn_conv_w"] = lax.dynamic_slice_in_dim(small["ffn_conv_w"], me * (2 * DFF // NDEV), 2 * DFF // NDEV, axis=1)
    names = [n for n, _ in small_specs]
    two = lambda a: a.reshape(a.shape[-2], a.shape[-1])
    ds, nms, nvs = _adamw_many([two(given[n]) for n in names], [small[n] for n in names],
                               [two(given["m_" + n]) for n in names], [two(given["v_" + n]) for n in names], "adamw_small")
    for n, d_, m_, v_ in zip(names, ds, nms, nvs):
        shp = given[n].shape
        grad[n], delta[n], new_m[n], new_v[n] = small[n].reshape(shp), d_.reshape(shp), m_.reshape(shp), v_.reshape(shp)

    order = ("rel_table", "w_in", "b_in", "conv_w", "conv_b", "conv_ln_g", "conv_ln_b", "attn_norm_g",
             "conv_norm_g", "w_out", "ln1_g", "ln1_b", "w_up", "ffn_conv_w", "ffn_conv_b", "w_down", "ln2_g", "ln2_b")
    return (small["loss"][0, 0], grad_x.reshape(BL, S, D), *[grad[n] for n in order], *[delta[n] for n in order],
            *[new_m[n] for n in order], *[new_v[n] for n in order])
```

```python
import math

import numpy as np
import jax
import jax.numpy as jnp
from jax import lax
from jax.experimental import pallas as pl
from jax.experimental.pallas import tpu as pltpu

F32 = jnp.float32
BF16 = jnp.bfloat16
SDS = jax.ShapeDtypeStruct

NDEV = 8
D = 1024
S = 2048
BL = 2
T = BL * S
NH = 12
HD = 64
AW = NH * HD
CW = D - AW
INW = 3 * AW + 2 * CW
CK = 31
DFF = 2816
FK = 3
BLK = 128
NBUCKET = 32
BRANCHES = ((128, 1), (512, 4), (2048, 16))
ALPHA = 2.0 ** 0.25
LN_EPS = 1e-5
NEG_INF = -1e30
LR, B1, B2, AEPS, WD, STEP = 0.001, 0.9, 0.999, 1e-08, 0.01, 10

TM = 512
FT = 1408
NFT = DFF // FT
TMF = 256
PACK_LANES = 128
GRAD_WIRE = BF16

assert all(w // d == BLK for w, d in BRANCHES)


def _dot(a, b):
    return jnp.dot(a, b, preferred_element_type=F32)


def _dot_nt(a, b):
    return lax.dot_general(a, b, (((1,), (1,)), ((), ())), preferred_element_type=F32)


def _dot_tn(a, b):
    return lax.dot_general(a, b, (((0,), (0,)), ((), ())), preferred_element_type=F32)


def _rowmean(v):
    return jnp.mean(v, axis=-1, keepdims=True)


def _colsum(v):
    return jnp.sum(v, axis=0, keepdims=True)


def _sigmoid(v):
    return jax.nn.sigmoid(v)


_HBM = pl.BlockSpec(memory_space=pltpu.HBM)
_SEM = pl.BlockSpec(memory_space=pltpu.SEMAPHORE)
_EFFECT = pltpu.SideEffectType.DATAFLOW_SIDE_EFFECTING


def _peer_of(k):
    x, y, c = lax.axis_index("x"), lax.axis_index("y"), lax.axis_index("c")
    px = 1 - x if k & 4 else x
    py = 1 - y if k & 2 else y
    pc = 1 - c if k & 1 else c
    return (px, py, pc), 4 * px + 2 * py + pc


def _split_copies(kinds, ins, lands, send_sems, recv_sems, started):
    me = 4 * lax.axis_index("x") + 2 * lax.axis_index("y") + lax.axis_index("c")
    out = []
    for i, kind in enumerate(kinds):
        for k in range(1, NDEV):
            dev, pid = _peer_of(k)
            src = ins[i] if kind == "gather" else ins[i].at[pid]
            dst = lands[i].at[me] if started else lands[i].at[pid]
            slot = i * (NDEV - 1) + k - 1
            out.append(pltpu.make_async_remote_copy(
                src_ref=src, dst_ref=dst, send_sem=send_sems.at[slot], recv_sem=recv_sems.at[slot],
                device_id=dev, device_id_type=pl.DeviceIdType.MESH))
    return out


def _exchange_start(items, name):
    n = len(items)
    kinds = [k for _, k in items]
    srcs = [pltpu.with_memory_space_constraint(a, pltpu.HBM) for a, _ in items]
    lands = []
    for a, k in items:
        shp = (NDEV,) + tuple(a.shape) if k == "gather" else tuple(a.shape)
        lands.append(pltpu.with_memory_space_constraint(lax.empty(shp, a.dtype), pltpu.HBM))

    def body(*refs):
        ins, land_refs = refs[:n], refs[n:2 * n]
        send_sems, recv_sems, own_sems = refs[2 * n:2 * n + 3]
        token = refs[-1]
        for cp in _own_copies(kinds, ins, land_refs, own_sems):
            cp.start()
        for cp in _split_copies(kinds, ins, land_refs, send_sems, recv_sems, True):
            cp.start()
        token[...] = jnp.zeros_like(token)

    sems = pltpu.SemaphoreType.DMA((n * (NDEV - 1),))
    res = pl.pallas_call(
        body, name=name,
        out_shape=(sems, sems, pltpu.SemaphoreType.DMA((n,)),
                   *[pltpu.HBM(a.shape, a.dtype) for a in srcs + lands], SDS((8, 128), F32)),
        in_specs=[_HBM] * (2 * n),
        out_specs=(_SEM, _SEM, _SEM, *[_HBM] * (2 * n), pl.BlockSpec(memory_space=pltpu.VMEM)),
        input_output_aliases={i: 3 + i for i in range(2 * n)},
        compiler_params=pltpu.CompilerParams(has_side_effects=_EFFECT),
    )(*srcs, *lands)
    return (kinds, res[0], res[1], res[2], list(res[3:3 + n]), list(res[3 + n:3 + 2 * n])), res[-1][0, 0]


def _own_copies(kinds, ins, lands, own_sems):
    me = 4 * lax.axis_index("x") + 2 * lax.axis_index("y") + lax.axis_index("c")
    return [pltpu.make_async_copy(ins[i] if kind == "gather" else ins[i].at[me], lands[i].at[me], own_sems.at[i])
            for i, kind in enumerate(kinds)]


def _exchange_wait(state, after, name, only=None):
    kinds, send_sems, recv_sems, own_sems, srcs, lands = state
    n = len(kinds)
    chosen = range(n) if only is None else only

    def body(*refs):
        ins, land_refs = refs[:n], refs[n:2 * n]
        s_sems, r_sems, o_sems = refs[2 * n:2 * n + 3]
        remote = _split_copies(kinds, ins, land_refs, s_sems, r_sems, False)
        own = _own_copies(kinds, ins, land_refs, o_sems)
        for i in chosen:
            for cp in remote[i * (NDEV - 1):(i + 1) * (NDEV - 1)]:
                cp.wait_send()
                cp.wait_recv()
        for i in chosen:
            own[i].wait()

    res = pl.pallas_call(
        body, name=name,
        out_shape=tuple(pltpu.HBM(a.shape, a.dtype) for a in srcs + lands),
        in_specs=[_HBM] * (2 * n) + [_SEM, _SEM, _SEM, pl.BlockSpec(memory_space=pl.ANY)],
        out_specs=tuple([_HBM] * (2 * n)),
        input_output_aliases={i: i for i in range(2 * n)},
        compiler_params=pltpu.CompilerParams(has_side_effects=_EFFECT),
    )(*srcs, *lands, send_sems, recv_sems, own_sems, after)
    srcs[:], lands[:] = res[:n], res[n:]
    return list(lands)


def _cast_x(x2, after):
    def body(x_ref, after_ref, o_ref, done_ref):
        o_ref[...] = x_ref[...].astype(BF16)
        done_ref[...] = jnp.zeros_like(done_ref)

    return pl.pallas_call(
        body, name="cast_x", grid=(T // TM,),
        in_specs=[pl.BlockSpec((TM, D), lambda m: (m, 0)), pl.BlockSpec(memory_space=pl.ANY)],
        out_specs=(pl.BlockSpec((TM, D), lambda m: (m, 0)), pl.BlockSpec((8, 128), lambda m: (0, 0))),
        out_shape=(SDS((T, D), BF16), SDS((8, 128), F32)),
    )(x2, after)


def _proj_in(xb, w_in_t, b_in):
    nq = 3 * AW

    def body(x_ref, w_ref, b_ref, qkv_ref, ag_ref):
        xb = x_ref[...]
        qkv_ref[...] = (_dot_nt(xb, w_ref[pl.ds(0, nq), :]) + b_ref[:, :nq]).astype(BF16)
        ag_ref[...] = _dot_nt(xb, w_ref[pl.ds(nq, 2 * CW), :]) + b_ref[:, nq:]

    return pl.pallas_call(
        body, name="proj_in", grid=(T // TM,),
        in_specs=[pl.BlockSpec((TM, D), lambda m: (m, 0)), pl.BlockSpec((INW, D), lambda m: (0, 0)),
                  pl.BlockSpec((1, INW), lambda m: (0, 0))],
        out_specs=(pl.BlockSpec((TM, nq), lambda m: (m, 0)), pl.BlockSpec((TM, 2 * CW), lambda m: (m, 0))),
        out_shape=(SDS((T, nq), BF16), SDS((T, 2 * CW), F32)),
    )(xb, w_in_t, b_in)


def _grad_x(pieces, w_in_t, dz1, zero):
    widths = [p.shape[1] for p in pieces]

    def body(*refs):
        p_refs = refs[:len(pieces)]
        w_ref, dz_ref, z_ref, o_ref = refs[len(pieces):]
        acc = ALPHA * dz_ref[...] + z_ref[...]
        r0 = 0
        for p_ref, wd in zip(p_refs, widths):
            acc = acc + _dot(p_ref[...], w_ref[pl.ds(r0, wd), :])
            r0 += wd
        o_ref[...] = acc

    row = pl.BlockSpec((TM, D), lambda m: (m, 0))
    return pl.pallas_call(
        body, name="grad_x", grid=(T // TM,),
        in_specs=[pl.BlockSpec((TM, wd), lambda m: (m, 0)) for wd in widths]
        + [pl.BlockSpec((INW, D), lambda m: (0, 0)), row, pl.BlockSpec((1, 1), lambda m: (0, 0))],
        out_specs=row,
        out_shape=SDS((T, D), F32),
    )(*pieces, w_in_t, dz1, zero)


def _grad_w_in(pieces, x2):
    widths = [p.shape[1] for p in pieces]
    tk = 512
    nk = T // tk

    def body(*refs):
        p_refs = refs[:len(pieces)]
        x_ref, o_ref, acc = refs[len(pieces):]
        k = pl.program_id(0)

        @pl.when(k == 0)
        def _():
            acc[...] = jnp.zeros_like(acc)

        xb = x_ref[...].astype(BF16)
        r0 = 0
        for p_ref, wd in zip(p_refs, widths):
            acc[pl.ds(r0, wd), :] += _dot_tn(p_ref[...], xb)
            r0 += wd

        @pl.when(k == nk - 1)
        def _():
            o_ref[...] = acc[...].astype(o_ref.dtype)

    return pl.pallas_call(
        body, name="grad_w_in", grid=(nk,),
        in_specs=[pl.BlockSpec((tk, wd), lambda k: (k, 0)) for wd in widths] + [pl.BlockSpec((tk, D), lambda k: (k, 0))],
        out_specs=pl.BlockSpec((INW, D), lambda k: (0, 0)),
        out_shape=SDS((INW, D), GRAD_WIRE),
        scratch_shapes=[pltpu.VMEM((INW, D), F32)],
    )(*pieces, x2)


def _mm_tn(a, b, tn, tk, name):
    t_, na = a.shape
    nb = b.shape[1]
    nk = t_ // tk

    def body(a_ref, b_ref, o_ref, acc):
        k = pl.program_id(1)

        @pl.when(k == 0)
        def _():
            acc[...] = jnp.zeros_like(acc)

        acc[...] += _dot_tn(a_ref[...].astype(BF16), b_ref[...].astype(BF16))

        @pl.when(k == nk - 1)
        def _():
            o_ref[...] = acc[...].astype(o_ref.dtype)

    return pl.pallas_call(
        body, name=name, grid=(na // tn, nk),
        in_specs=[pl.BlockSpec((tk, tn), lambda n, k: (k, n)),
                  pl.BlockSpec((tk, nb), lambda n, k: (k, 0))],
        out_specs=pl.BlockSpec((tn, nb), lambda n, k: (n, 0)),
        out_shape=SDS((na, nb), GRAD_WIRE),
        scratch_shapes=[pltpu.VMEM((tn, nb), F32)],
    )(a, b)


def _bucket_maps():
    qi = np.arange(BLK)[:, None]
    kj = np.arange(2 * BLK)[None, :]
    steps = np.maximum(qi + BLK - kj, 0)
    exact = NBUCKET // 2
    maps = []
    for _, dil in BRANCHES:
        dist = steps * dil
        d_f = np.maximum(dist, 1).astype(np.float32)
        large = exact + (np.log(d_f / np.float32(exact)) / np.float32(math.log(S / exact))
                         * np.float32(NBUCKET - exact)).astype(np.int32)
        large = np.minimum(large, NBUCKET - 1)
        maps.append(np.where(dist < exact, dist, large).astype(np.int32))
    return np.stack(maps)


def _bias_table(rel_table, buckets):
    def body(t_ref, b_ref, o_ref):
        bk = b_ref[0]
        for h in range(NH):
            acc = jnp.zeros((BLK, 2 * BLK), F32)
            for k in range(NBUCKET):
                acc = jnp.where(bk == k, t_ref[k, h], acc)
            o_ref[0, h] = acc

    return pl.pallas_call(
        body, name="bias_table", grid=(len(BRANCHES),),
        in_specs=[pl.BlockSpec(memory_space=pltpu.SMEM),
                  pl.BlockSpec((1, BLK, 2 * BLK), lambda i: (i, 0, 0))],
        out_specs=pl.BlockSpec((1, NH, BLK, 2 * BLK), lambda i: (i, 0, 0, 0)),
        out_shape=SDS((len(BRANCHES), NH, BLK, 2 * BLK), F32),
    )(rel_table, buckets)


def _rel_table_grad(dbias, buckets, after):
    def body(d_ref, b_ref, after_ref, o_ref):
        for k in range(NBUCKET):
            tot = jnp.zeros((1, 1), F32)
            for br in range(len(BRANCHES)):
                sel = jnp.where(b_ref[br] == k, d_ref[br, 0], 0.0)
                tot = tot + jnp.sum(jnp.sum(sel, axis=1, keepdims=True), axis=0, keepdims=True)
            o_ref[0, :, pl.ds(k, 1)] = tot

    out = pl.pallas_call(
        body, name="rel_table_grad", grid=(NH,),
        in_specs=[pl.BlockSpec((len(BRANCHES), 1, BLK, 2 * BLK), lambda h: (0, h, 0, 0)),
                  pl.BlockSpec((len(BRANCHES), BLK, 2 * BLK), lambda h: (0, 0, 0)),
                  pl.BlockSpec(memory_space=pl.ANY)],
        out_specs=pl.BlockSpec((1, 1, NBUCKET), lambda h: (h, 0, 0)),
        out_shape=SDS((NH, 1, NBUCKET), F32),
    )(dbias, buckets, after)
    return out.reshape(NH, NBUCKET).T


PADK = BLK
SCALE = 1.0 / math.sqrt(HD)
NBLK = S // BLK
ACH = 256


def _branch_geometry(br):
    dil = BRANCHES[br][1]
    sub = S // dil
    return dil, sub, sub // BLK


def _token_rows(br, i):
    dil, _, nblk = _branch_geometry(br)
    if dil == 1:
        return pl.ds(pl.multiple_of(i * BLK, BLK), BLK), i
    r = lax.shift_right_logical(i, nblk.bit_length() - 1)
    n = lax.bitwise_and(i, nblk - 1)
    return pl.ds(r + dil * BLK * n, BLK, stride=dil), n


def _sub_layout_loop(br, step):
    dil, sub, _ = _branch_geometry(br)
    rows = min(sub, ACH)
    nchunk = sub // rows

    def it_step(it, carry):
        if dil == 1:
            src = pl.ds(pl.multiple_of(it * rows, rows), rows)
        else:
            r = lax.shift_right_logical(it, nchunk.bit_length() - 1)
            src = pl.ds(r + dil * rows * lax.bitwise_and(it, nchunk - 1), rows, stride=dil)
        step(src, pl.multiple_of(it * rows, BLK), rows)
        return carry

    lax.fori_loop(0, dil * nchunk, it_step, 0, unroll=True)


def _masked_bias(bias_ref, bm):
    qi = lax.broadcasted_iota(jnp.int32, (BLK, 2 * BLK), 0)
    kj = lax.broadcasted_iota(jnp.int32, (BLK, 2 * BLK), 1)
    first = jnp.logical_and(kj >= BLK, kj - BLK <= qi)
    valid = jnp.logical_or(first, jnp.logical_and(kj < BLK, kj >= qi))
    for br in range(len(BRANCHES)):
        for j in range(2):
            b = bias_ref[br, j]
            bm[br, 1, pl.ds(j * BLK, BLK), :] = jnp.where(valid, b, NEG_INF)
            bm[br, 0, pl.ds(j * BLK, BLK), :] = jnp.where(first, b, NEG_INF)


def _head_split(fn):
    def split(t):
        h0 = lax.broadcasted_iota(jnp.int32, t.shape, 1) < HD
        t = fn(t)
        return jnp.where(h0, t, 0.0).astype(BF16), jnp.where(h0, 0.0, t).astype(BF16)
    return split


def _attn_fwd(qkv, bias):
    nbr = len(BRANCHES)

    def body(q_ref, k_ref, v_ref, bias_ref, o_ref, lse_ref, qf, kf, vf, qs0, qs1, ks, vs, bm, ob, mb, lb):
        qf[...] = q_ref[...].astype(F32)
        kf[...] = k_ref[...].astype(F32)
        vf[...] = v_ref[...].astype(F32)
        _masked_bias(bias_ref, bm)
        ks[pl.ds(0, PADK), :] = jnp.zeros((PADK, BLK), BF16)
        vs[pl.ds(0, PADK), :] = jnp.zeros((PADK, BLK), BF16)
        head0 = lax.broadcasted_iota(jnp.int32, (BLK, BLK), 1) < HD
        split_q = _head_split(lambda t: t * SCALE)

        for br in range(nbr):
            nblk = _branch_geometry(br)[2]

            def stage(src, off, rows):
                qs0[pl.ds(off, rows), :], qs1[pl.ds(off, rows), :] = split_q(qf[src, :])
                ks[pl.ds(PADK + off, rows), :] = kf[src, :].astype(BF16)
                vs[pl.ds(PADK + off, rows), :] = vf[src, :].astype(BF16)

            _sub_layout_loop(br, stage)

            def blk(i, carry, br=br, nblk=nblk):
                base = pl.multiple_of(i * BLK, BLK)
                rows, n = _token_rows(br, i)
                q01 = jnp.concatenate([qs0[pl.ds(base, BLK), :], qs1[pl.ds(base, BLK), :]], axis=0)
                if nblk > 1:
                    kcat = ks[pl.ds(base, 2 * BLK), :]
                    vcat = vs[pl.ds(base, 2 * BLK), :]
                    s = _dot_nt(q01, kcat) + bm[br, jnp.minimum(n, 1)]
                else:
                    kcat = ks[pl.ds(PADK + base, BLK), :]
                    vcat = vs[pl.ds(PADK + base, BLK), :]
                    s = _dot_nt(q01, kcat) + bm[br, 0, :, BLK:]
                mx = jnp.max(s, axis=-1, keepdims=True)
                p = jnp.exp(s - mx)
                ls = jnp.sum(p, axis=-1, keepdims=True)
                o = _dot(p.astype(BF16), vcat)
                ob[br, rows, :] = jnp.where(head0, o[:BLK], o[BLK:])
                mb[br, rows, :] = jnp.where(head0, mx[:BLK], mx[BLK:])
                lb[br, rows, :] = jnp.where(head0, ls[:BLK], ls[BLK:])
                return carry

            lax.fori_loop(0, NBLK, blk, 0, unroll=True)

        def merge(i, carry):
            rows = pl.ds(pl.multiple_of(i * ACH, ACH), ACH)
            m_all = jnp.maximum(jnp.maximum(mb[0, rows, :], mb[1, rows, :]), mb[2, rows, :])
            num = jnp.zeros((ACH, BLK), F32)
            den = jnp.zeros((ACH, BLK), F32)
            for br in range(nbr):
                c = jnp.exp(mb[br, rows, :] - m_all)
                num = num + ob[br, rows, :] * c
                den = den + lb[br, rows, :] * c
            o_ref[rows, :] = num / den
            lse_ref[rows, :] = m_all + jnp.log(den)
            return carry

        lax.fori_loop(0, S // ACH, merge, 0)

    npair = NH // 2
    blk_spec = lambda off: pl.BlockSpec((S, BLK), lambda b, hp: (b, off + hp))
    return pl.pallas_call(
        body, name="attn_fwd", grid=(BL, npair),
        in_specs=[blk_spec(0), blk_spec(npair), blk_spec(2 * npair),
                  pl.BlockSpec((nbr, 2, BLK, 2 * BLK), lambda b, hp: (0, hp, 0, 0))],
        out_specs=(blk_spec(0), blk_spec(0)),
        out_shape=(SDS((T, AW), F32), SDS((T, AW), F32)),
        scratch_shapes=[pltpu.VMEM((S, BLK), F32)] * 3 + [pltpu.VMEM((S, BLK), BF16)] * 2
        + [pltpu.VMEM((PADK + S, BLK), BF16)] * 2 + [pltpu.VMEM((nbr, 2, 2 * BLK, 2 * BLK), F32)]
        + [pltpu.VMEM((nbr, S, BLK), F32)] * 3,
    )(qkv, qkv, qkv, bias)


def _attn_bwd(qkv, attn, lse, dattn, bias):
    nbr = len(BRANCHES)

    def body(q_ref, k_ref, v_ref, o_ref, lse_ref, do_ref, bias_ref,
             dq_ref, dk_ref, dv_ref, sq_ref, sk_ref, sv_ref, db_ref,
             qf, kf, vf, dl, dqa, dka, dva, qs0, qs1, ds0, ds1, ks, vs, dks, dvs, bm):
        b = pl.program_id(1)
        qf[...] = q_ref[...].astype(F32)
        kf[...] = k_ref[...].astype(F32)
        vf[...] = v_ref[...].astype(F32)
        dqa[...] = jnp.zeros_like(dqa)
        dka[...] = jnp.zeros_like(dka)
        dva[...] = jnp.zeros_like(dva)
        _masked_bias(bias_ref, bm)
        ks[pl.ds(0, PADK), :] = jnp.zeros((PADK, BLK), BF16)
        vs[pl.ds(0, PADK), :] = jnp.zeros((PADK, BLK), BF16)
        head0 = lax.broadcasted_iota(jnp.int32, (BLK, BLK), 1) < HD
        split_q = _head_split(lambda t: t * SCALE)
        split_do = _head_split(lambda t: t)

        @pl.when(b == 0)
        def _():
            db_ref[...] = jnp.zeros_like(db_ref)
            sq_ref[...] = jnp.zeros_like(sq_ref)
            sk_ref[...] = jnp.zeros_like(sk_ref)
            sv_ref[...] = jnp.zeros_like(sv_ref)

        def delta(i, carry):
            rows = pl.ds(pl.multiple_of(i * ACH, ACH), ACH)
            prod = do_ref[rows, :] * o_ref[rows, :]
            h0 = lax.broadcasted_iota(jnp.int32, (ACH, BLK), 1) < HD
            d0 = jnp.sum(jnp.where(h0, prod, 0.0), axis=-1, keepdims=True)
            d1 = jnp.sum(jnp.where(h0, 0.0, prod), axis=-1, keepdims=True)
            dl[rows, :] = jnp.where(h0, d0, d1)
            return carry

        lax.fori_loop(0, S // ACH, delta, 0, unroll=True)

        for br in range(nbr):
            nblk = _branch_geometry(br)[2]

            def stage(src, off, rows):
                qs0[pl.ds(off, rows), :], qs1[pl.ds(off, rows), :] = split_q(qf[src, :])
                ds0[pl.ds(off, rows), :], ds1[pl.ds(off, rows), :] = split_do(do_ref[src, :])
                ks[pl.ds(PADK + off, rows), :] = kf[src, :].astype(BF16)
                vs[pl.ds(PADK + off, rows), :] = vf[src, :].astype(BF16)

            _sub_layout_loop(br, stage)
            dks[...] = jnp.zeros_like(dks)
            dvs[...] = jnp.zeros_like(dvs)

            def blk(i, carry, br=br, nblk=nblk):
                base = pl.multiple_of(i * BLK, BLK)
                rows, n = _token_rows(br, i)
                q01 = jnp.concatenate([qs0[pl.ds(base, BLK), :], qs1[pl.ds(base, BLK), :]], axis=0)
                do01 = jnp.concatenate([ds0[pl.ds(base, BLK), :], ds1[pl.ds(base, BLK), :]], axis=0)
                lse_b = lse_ref[rows, :]
                dl_b = dl[rows, :]
                lse01 = jnp.concatenate([lse_b[:, 0:1], lse_b[:, HD:HD + 1]], axis=0)
                dl01 = jnp.concatenate([dl_b[:, 0:1], dl_b[:, HD:HD + 1]], axis=0)
                if nblk > 1:
                    krows = pl.ds(base, 2 * BLK)
                    bias_m = bm[br, jnp.minimum(n, 1)]
                else:
                    krows = pl.ds(PADK + base, BLK)
                    bias_m = bm[br, 0, :, BLK:]
                kcat = ks[krows, :]
                vcat = vs[krows, :]
                p = jnp.exp(_dot_nt(q01, kcat) + bias_m - lse01)
                dsv = p * (_dot_nt(do01, vcat) - dl01)
                if nblk > 1:
                    db_ref[br, 0] += dsv[:BLK]
                    db_ref[br, 1] += dsv[BLK:]
                else:
                    db_ref[br, 0, :, BLK:] += dsv[:BLK]
                    db_ref[br, 1, :, BLK:] += dsv[BLK:]
                dsb = dsv.astype(BF16)
                dq01 = _dot(dsb, kcat)
                dqa[rows, :] = dqa[rows, :] + jnp.where(head0, dq01[:BLK], dq01[BLK:])
                dks[krows, :] = dks[krows, :] + _dot_tn(dsb, q01)
                dvs[krows, :] = dvs[krows, :] + _dot_tn(p.astype(BF16), do01)
                return carry

            lax.fori_loop(0, NBLK, blk, 0, unroll=True)

            def fold(src, off, rows):
                dka[src, :] = dka[src, :] + dks[pl.ds(PADK + off, rows), :]
                dva[src, :] = dva[src, :] + dvs[pl.ds(PADK + off, rows), :]

            _sub_layout_loop(br, fold)

        def flush(i, carry):
            rows = pl.ds(pl.multiple_of(i * ACH, ACH), ACH)
            for acc, out, cs, mul in ((dqa, dq_ref, sq_ref, SCALE), (dka, dk_ref, sk_ref, 1.0), (dva, dv_ref, sv_ref, 1.0)):
                val = acc[rows, :] * mul
                out[rows, :] = val.astype(BF16)
                cs[...] += _colsum(val)
            return carry

        lax.fori_loop(0, S // ACH, flush, 0, unroll=True)

    npair = NH // 2
    blk_spec = lambda off: pl.BlockSpec((S, BLK), lambda hp, b: (b, off + hp))
    sum_spec = pl.BlockSpec((1, BLK), lambda hp, b: (0, hp))
    return pl.pallas_call(
        body, name="attn_bwd", grid=(npair, BL),
        in_specs=[blk_spec(0), blk_spec(npair), blk_spec(2 * npair), blk_spec(0), blk_spec(0), blk_spec(0),
                  pl.BlockSpec((nbr, 2, BLK, 2 * BLK), lambda hp, b: (0, hp, 0, 0))],
        out_specs=(blk_spec(0), blk_spec(0), blk_spec(0), sum_spec, sum_spec, sum_spec,
                   pl.BlockSpec((nbr, 2, BLK, 2 * BLK), lambda hp, b: (0, hp, 0, 0))),
        out_shape=(SDS((T, AW), BF16), SDS((T, AW), BF16), SDS((T, AW), BF16),
                   SDS((1, AW), F32), SDS((1, AW), F32), SDS((1, AW), F32),
                   SDS((nbr, NH, BLK, 2 * BLK), F32)),
        scratch_shapes=[pltpu.VMEM((S, BLK), F32)] * 7 + [pltpu.VMEM((S, BLK), BF16)] * 4
        + [pltpu.VMEM((PADK + S, BLK), BF16)] * 2 + [pltpu.VMEM((PADK + S, BLK), F32)] * 2
        + [pltpu.VMEM((nbr, 2, 2 * BLK, 2 * BLK), F32)],
    )(qkv, qkv, qkv, attn, lse, dattn, bias)


CH = 256
PADR = 32


def _tap_phases(offset_of_tap):
    taps = sorted((offset_of_tap(k) % 8, offset_of_tap(k) - offset_of_tap(k) % 8, k) for k in range(CK))
    assert all(lo + CH + ph <= CH + PADR for ph, lo, _ in taps)
    return taps


def _rows_up(win):
    made = {0: win}

    def get(phase):
        if phase not in made:
            made[phase] = pltpu.roll(win, win.shape[0] - phase, 0)
        return made[phase]
    return get


def _conv_fwd(ag, conv_w, conv_b):
    def body(ag_ref, w_ref, b_ref, u1_ref, u0p):
        u0p[pl.ds(0, PADR), :] = jnp.zeros((PADR, CW), F32)

        def glu(i, carry):
            t0 = pl.multiple_of(i * CH, CH)
            a = ag_ref[pl.ds(t0, CH), :CW]
            g = ag_ref[pl.ds(t0, CH), CW:]
            u0p[pl.ds(PADR + t0, CH), :] = a * _sigmoid(g)
            return carry

        lax.fori_loop(0, S // CH, glu, 0)

        def conv(i, carry):
            t0 = pl.multiple_of(i * CH, CH)
            win = u0p[pl.ds(t0, CH + PADR), :]
            acc = jnp.zeros((CH, CW), F32) + b_ref[...]
            up = _rows_up(win)
            for phase, lo, k in _tap_phases(lambda k: PADR - (CK - 1) + k):
                acc = acc + up(phase)[lo:lo + CH, :] * w_ref[k:k + 1, :]
            u1_ref[pl.ds(t0, CH), :] = acc
            return carry

        lax.fori_loop(0, S // CH, conv, 0)

    return pl.pallas_call(
        body, name="conv_fwd", grid=(BL,),
        in_specs=[pl.BlockSpec((S, 2 * CW), lambda b: (b, 0)),
                  pl.BlockSpec((CK, CW), lambda b: (0, 0)),
                  pl.BlockSpec((1, CW), lambda b: (0, 0))],
        out_specs=pl.BlockSpec((S, CW), lambda b: (b, 0)),
        out_shape=SDS((T, CW), F32),
        scratch_shapes=[pltpu.VMEM((S + PADR, CW), F32)],
    )(ag, conv_w, conv_b)


def _conv_post(u1, cg, cb):
    mu = _rowmean(u1)
    uc = u1 - mu
    rstd = lax.rsqrt(_rowmean(uc * uc) + LN_EPS)
    xh = uc * rstd
    u2 = xh * cg + cb
    sg = _sigmoid(u2)
    return xh, rstd, u2, sg, u2 * sg


def _mix_fwd(attn, u1, ga, gc, cg, cb):
    def body(a_ref, u_ref, ga_ref, gc_ref, cg_ref, cb_ref, o_ref):
        a = a_ref[...]
        ra = lax.rsqrt(_rowmean(a * a) + LN_EPS)
        o_ref[:, :AW] = (a * ra * ga_ref[...]).astype(BF16)
        _, _, _, _, u3 = _conv_post(u_ref[...], cg_ref[...], cb_ref[...])
        rc = lax.rsqrt(_rowmean(u3 * u3) + LN_EPS)
        o_ref[:, AW:] = (u3 * rc * gc_ref[...]).astype(BF16)

    vec = lambda w: pl.BlockSpec((1, w), lambda m: (0, 0))
    return pl.pallas_call(
        body, name="mix_fwd", grid=(T // TM,),
        in_specs=[pl.BlockSpec((TM, AW), lambda m: (m, 0)), pl.BlockSpec((TM, CW), lambda m: (m, 0)),
                  vec(AW), vec(CW), vec(CW), vec(CW)],
        out_specs=pl.BlockSpec((TM, D), lambda m: (m, 0)),
        out_shape=SDS((T, D), BF16),
    )(attn, u1, ga, gc, cg, cb)


def _mix_bwd(dz1, w_out, attn, u1, ga, gc, cg, cb):
    def body(dz_ref, w_ref, a_ref, u_ref, ga_ref, gc_ref, cg_ref, cb_ref,
             da_ref, du_ref, g_an, g_cn, g_lg, g_lb, g_cb):
        @pl.when(pl.program_id(0) == 0)
        def _():
            for r in (g_an, g_cn, g_lg, g_lb, g_cb):
                r[...] = jnp.zeros_like(r)

        dm = _dot_nt(dz_ref[...].astype(BF16), w_ref[...])
        a = a_ref[...]
        dna = dm[:, :AW]
        ra = lax.rsqrt(_rowmean(a * a) + LN_EPS)
        g_an[...] += _colsum(dna * a * ra)
        dat = dna * ga_ref[...]
        da_ref[...] = ra * dat - a * (ra * ra * ra) * _rowmean(dat * a)

        xh, rstd, u2, sg, u3 = _conv_post(u_ref[...], cg_ref[...], cb_ref[...])
        dnc = dm[:, AW:]
        rc = lax.rsqrt(_rowmean(u3 * u3) + LN_EPS)
        g_cn[...] += _colsum(dnc * u3 * rc)
        dut = dnc * gc_ref[...]
        du3 = rc * dut - u3 * (rc * rc * rc) * _rowmean(dut * u3)
        du2 = du3 * sg * (1.0 + u2 * (1.0 - sg))
        g_lg[...] += _colsum(du2 * xh)
        g_lb[...] += _colsum(du2)
        dxh = du2 * cg_ref[...]
        du1 = rstd * (dxh - _rowmean(dxh) - xh * _rowmean(dxh * xh))
        g_cb[...] += _colsum(du1)
        du_ref[...] = du1

    vec = lambda w: pl.BlockSpec((1, w), lambda m: (0, 0))
    return pl.pallas_call(
        body, name="mix_bwd", grid=(T // TM,),
        in_specs=[pl.BlockSpec((TM, D), lambda m: (m, 0)), pl.BlockSpec((D, D), lambda m: (0, 0)),
                  pl.BlockSpec((TM, AW), lambda m: (m, 0)),
                  pl.BlockSpec((TM, CW), lambda m: (m, 0)), vec(AW), vec(CW), vec(CW), vec(CW)],
        out_specs=(pl.BlockSpec((TM, AW), lambda m: (m, 0)), pl.BlockSpec((TM, CW), lambda m: (m, 0)),
                   vec(AW), vec(CW), vec(CW), vec(CW), vec(CW)),
        out_shape=(SDS((T, AW), F32), SDS((T, CW), F32),
                   SDS((1, AW), F32), SDS((1, CW), F32), SDS((1, CW), F32), SDS((1, CW), F32), SDS((1, CW), F32)),
    )(dz1, w_out, attn, u1, ga, gc, cg, cb)


def _conv_bwd(du1, ag, conv_w):
    def body(du_ref, ag_ref, w_ref, dag_ref, cs_ref, gw_ref, u0p, dup):
        @pl.when(pl.program_id(0) == 0)
        def _():
            cs_ref[...] = jnp.zeros_like(cs_ref)
            gw_ref[...] = jnp.zeros_like(gw_ref)

        u0p[pl.ds(0, PADR), :] = jnp.zeros((PADR, CW), F32)
        dup[pl.ds(S, PADR), :] = jnp.zeros((PADR, CW), F32)

        def fill(i, carry):
            t0 = pl.multiple_of(i * CH, CH)
            a = ag_ref[pl.ds(t0, CH), :CW]
            g = ag_ref[pl.ds(t0, CH), CW:]
            u0p[pl.ds(PADR + t0, CH), :] = a * _sigmoid(g)
            dup[pl.ds(t0, CH), :] = du_ref[pl.ds(t0, CH), :]
            return carry

        lax.fori_loop(0, S // CH, fill, 0)

        def chunk(i, carry):
            t0 = pl.multiple_of(i * CH, CH)
            d = dup[pl.ds(t0, CH), :]
            win_u = u0p[pl.ds(t0, CH + PADR), :]
            win_d = dup[pl.ds(t0, CH + PADR), :]
            du0 = jnp.zeros((CH, CW), F32)
            up_u = _rows_up(win_u)
            for phase, lo, k in _tap_phases(lambda k: PADR - (CK - 1) + k):
                gw_ref[k:k + 1, :] += _colsum(d * up_u(phase)[lo:lo + CH, :])
            up_d = _rows_up(win_d)
            for phase, lo, k in _tap_phases(lambda k: CK - 1 - k):
                du0 = du0 + up_d(phase)[lo:lo + CH, :] * w_ref[k:k + 1, :]
            a = ag_ref[pl.ds(t0, CH), :CW]
            sg = _sigmoid(ag_ref[pl.ds(t0, CH), CW:])
            da = du0 * sg
            dg = du0 * a * sg * (1.0 - sg)
            dag_ref[pl.ds(t0, CH), :CW] = da.astype(BF16)
            dag_ref[pl.ds(t0, CH), CW:] = dg.astype(BF16)
            cs_ref[:, :CW] += _colsum(da)
            cs_ref[:, CW:] += _colsum(dg)
            return carry

        lax.fori_loop(0, S // CH, chunk, 0)

    return pl.pallas_call(
        body, name="conv_bwd", grid=(BL,),
        in_specs=[pl.BlockSpec((S, CW), lambda b: (b, 0)), pl.BlockSpec((S, 2 * CW), lambda b: (b, 0)),
                  pl.BlockSpec((CK, CW), lambda b: (0, 0))],
        out_specs=(pl.BlockSpec((S, 2 * CW), lambda b: (b, 0)),
                   pl.BlockSpec((1, 2 * CW), lambda b: (0, 0)),
                   pl.BlockSpec((PADR, CW), lambda b: (0, 0))),
        out_shape=(SDS((T, 2 * CW), BF16), SDS((1, 2 * CW), F32), SDS((PADR, CW), F32)),
        scratch_shapes=[pltpu.VMEM((S + PADR, CW), F32), pltpu.VMEM((S + PADR, CW), F32)],
    )(du1, ag, conv_w)


def _layer_norm_fwd(z):
    mu = _rowmean(z)
    zc = z - mu
    rstd = lax.rsqrt(_rowmean(zc * zc) + LN_EPS)
    return zc * rstd, rstd


def _layer_norm_bwd(dy, xh, rstd, g):
    dxh = dy * g
    return rstd * (dxh - _rowmean(dxh) - xh * _rowmean(dxh * xh))


def _out_proj_ln1(mixed, w_out, x2, g1, b1):
    def body(a_ref, w_ref, x_ref, g_ref, b_ref, xh_ref, rstd_ref, x1_ref):
        z = ALPHA * x_ref[...] + _dot(a_ref[...], w_ref[...])
        xh, rstd = _layer_norm_fwd(z)
        xh_ref[...] = xh
        rstd_ref[...] = rstd
        x1_ref[...] = (xh * g_ref[...] + b_ref[...]).astype(BF16)

    vec = pl.BlockSpec((1, D), lambda m: (0, 0))
    row = pl.BlockSpec((TM, D), lambda m: (m, 0))
    return pl.pallas_call(
        body, name="out_proj_ln1", grid=(T // TM,),
        in_specs=[row, pl.BlockSpec((D, D), lambda m: (0, 0)), row, vec, vec],
        out_specs=(row, pl.BlockSpec((TM, 1), lambda m: (m, 0)), row),
        out_shape=(SDS((T, D), F32), SDS((T, 1), F32), SDS((T, D), BF16)),
    )(mixed, w_out, x2, g1, b1)


def _seq_start(m):
    return lax.bitwise_and(m, S // TMF - 1) == 0


def _shift_down(x, before, k):
    rolled = pltpu.roll(x, k, 0)
    row = lax.broadcasted_iota(jnp.int32, before.shape, 0)
    head = jnp.where(row < k, pltpu.roll(before, k, 0), rolled[:8])
    return jnp.concatenate([head, rolled[8:]], axis=0)


def _shift_up(x, after, k):
    n = x.shape[0]
    rolled = pltpu.roll(x, n - k, 0)
    row = lax.broadcasted_iota(jnp.int32, after.shape, 0)
    tail = jnp.where(row >= 8 - k, pltpu.roll(after, 8 - k, 0), rolled[n - 8:])
    return jnp.concatenate([rolled[:n - 8], tail], axis=0)


def _ffn_up(x1b, w_up, fcw, fcb):
    def body(x_ref, wg_ref, wv_ref, cwg_ref, cwv_ref, cbg_ref, cbv_ref, up_ref, gv_ref, act_ref, prev_g, prev_v):
        @pl.when(_seq_start(pl.program_id(1)))
        def _():
            prev_g[...] = jnp.zeros_like(prev_g)
            prev_v[...] = jnp.zeros_like(prev_v)

        x = x_ref[...]
        outs = []
        for w_ref, cw_ref, cb_ref, prev, lo in ((wg_ref, cwg_ref, cbg_ref, prev_g, 0), (wv_ref, cwv_ref, cbv_ref, prev_v, FT)):
            u = _dot_nt(x, w_ref[...])
            up_ref[:, lo:lo + FT] = u.astype(BF16)
            before = prev[...]
            y = (cw_ref[2:3, :] * u + cw_ref[1:2, :] * _shift_down(u, before, 1)
                 + cw_ref[0:1, :] * _shift_down(u, before, 2) + cb_ref[...])
            prev[...] = u[TMF - 8:]
            gv_ref[:, lo:lo + FT] = y.astype(BF16)
            outs.append(y)
        gate, val = outs
        act_ref[...] = (gate * _sigmoid(gate) * val).astype(BF16)

    wspec = lambda off: pl.BlockSpec((FT, D), lambda n, m: (n + off, 0))
    cwspec = lambda off: pl.BlockSpec((FK, FT), lambda n, m: (0, n + off))
    cbspec = lambda off: pl.BlockSpec((1, FT), lambda n, m: (0, n + off))
    pair = pl.BlockSpec((TMF, 2 * FT), lambda n, m: (m, n))
    return pl.pallas_call(
        body, name="ffn_up", grid=(NFT, T // TMF),
        in_specs=[pl.BlockSpec((TMF, D), lambda n, m: (m, 0)), wspec(0), wspec(NFT),
                  cwspec(0), cwspec(NFT), cbspec(0), cbspec(NFT)],
        out_specs=(pair, pair, pl.BlockSpec((TMF, FT), lambda n, m: (m, n))),
        out_shape=(SDS((T, 2 * DFF), BF16), SDS((T, 2 * DFF), BF16), SDS((T, DFF), BF16)),
        scratch_shapes=[pltpu.VMEM((8, FT), F32)] * 2,
    )(x1b, w_up, w_up, fcw, fcw, fcb, fcb)


def _ffn_down_loss(act, w_down, xh1, g1, b1, g2, b2, target):
    def body(a_ref, w_ref, xh1_ref, g1_ref, b1_ref, g2_ref, b2_ref, t_ref, dz_ref, loss_ref, gg_ref, gb_ref):
        @pl.when(pl.program_id(0) == 0)
        def _():
            loss_ref[...] = jnp.zeros_like(loss_ref)
            gg_ref[...] = jnp.zeros_like(gg_ref)
            gb_ref[...] = jnp.zeros_like(gb_ref)

        for sub in range(TM // TMF):
            rows = pl.ds(sub * TMF, TMF)
            x1 = xh1_ref[rows, :] * g1_ref[...] + b1_ref[...]
            z = ALPHA * x1 + _dot(a_ref[rows, :], w_ref[...])
            xh, rstd = _layer_norm_fwd(z)
            diff = xh * g2_ref[...] + b2_ref[...] - t_ref[rows, :]
            loss_ref[...] += 0.5 * _colsum(_rowmean(diff * diff))
            dout = diff * (1.0 / D)
            gg_ref[...] += _colsum(dout * xh)
            gb_ref[...] += _colsum(dout)
            dz_ref[rows, :] = _layer_norm_bwd(dout, xh, rstd, g2_ref[...])

    vec = pl.BlockSpec((1, D), lambda m: (0, 0))
    row = pl.BlockSpec((TM, D), lambda m: (m, 0))
    return pl.pallas_call(
        body, name="ffn_down_loss", grid=(T // TM,),
        in_specs=[pl.BlockSpec((TM, DFF), lambda m: (m, 0)), pl.BlockSpec((DFF, D), lambda m: (0, 0)),
                  row, vec, vec, vec, vec, row],
        out_specs=(row, pl.BlockSpec((1, 1), lambda m: (0, 0)), vec, vec),
        out_shape=(SDS((T, D), F32), SDS((1, 1), F32), SDS((1, D), F32), SDS((1, D), F32)),
    )(act, w_down, xh1, g1, b1, g2, b2, target)


def _ffn_down_bwd(dz2, w_down, gv, up, fcw):
    tiles = T // TMF

    def body(dz_ref, wd_ref, gv_ref, up_ref, cwg_ref, cwv_ref,
             dpre_ref, csg_ref, csv_ref, gwg_ref, gwv_ref, next_g, next_v):
        step = pl.program_id(1)
        tile = tiles - 1 - step

        @pl.when(step == 0)
        def _():
            for r in (csg_ref, csv_ref, gwg_ref, gwv_ref, next_g, next_v):
                r[...] = jnp.zeros_like(r)

        seq_end = lax.bitwise_and(tile + 1, S // TMF - 1) == 0
        dact = _dot_nt(dz_ref[...].astype(BF16), wd_ref[...])
        gate = gv_ref[:, :FT].astype(F32)
        val = gv_ref[:, FT:].astype(F32)
        sg = _sigmoid(gate)
        gs = gate * sg
        halves = ((dact * val * (sg + gs * (1.0 - sg)), cwg_ref, csg_ref, gwg_ref, next_g, 0),
                  (dact * gs, cwv_ref, csv_ref, gwv_ref, next_v, FT))
        for d0, cw_ref, cs_ref, gw_ref, nxt, lo in halves:
            after = jnp.where(seq_end, 0.0, nxt[...])
            d1 = _shift_up(d0, after, 1)
            d2 = _shift_up(d0, after, 2)
            nxt[...] = d0[:8]
            dpre_ref[:, lo:lo + FT] = (cw_ref[2:3, :] * d0 + cw_ref[1:2, :] * d1 + cw_ref[0:1, :] * d2).astype(BF16)
            cs_ref[...] += _colsum(d0)
            u = up_ref[:, lo:lo + FT].astype(F32)
            for k, dk in enumerate((d2, d1, d0)):
                gw_ref[k:k + 1, :] += _colsum(dk * u)

    cs = pl.BlockSpec((1, FT), lambda n, m: (0, n))
    gw = pl.BlockSpec((FK, FT), lambda n, m: (0, n))
    cwspec = lambda off: pl.BlockSpec((FK, FT), lambda n, m: (0, n + off))
    pair = pl.BlockSpec((TMF, 2 * FT), lambda n, m: (tiles - 1 - m, n))
    return pl.pallas_call(
        body, name="ffn_down_bwd", grid=(NFT, tiles),
        in_specs=[pl.BlockSpec((TMF, D), lambda n, m: (tiles - 1 - m, 0)), pl.BlockSpec((FT, D), lambda n, m: (n, 0)),
                  pair, pair, cwspec(0), cwspec(NFT)],
        out_specs=(pair, cs, cs, gw, gw),
        out_shape=(SDS((T, 2 * DFF), BF16), SDS((1, DFF), F32), SDS((1, DFF), F32),
                   SDS((FK, DFF), F32), SDS((FK, DFF), F32)),
        scratch_shapes=[pltpu.VMEM((8, FT), F32)] * 2,
    )(dz2, w_down, gv, up, fcw, fcw)


def _ffn_up_bwd_ln1(dpre, w_up, dz2, xh1, rstd1, g1):
    def body(a_ref, w_ref, dz2_ref, xh_ref, rstd_ref, g_ref, dz1_ref, gg_ref, gb_ref):
        @pl.when(pl.program_id(0) == 0)
        def _():
            gg_ref[...] = jnp.zeros_like(gg_ref)
            gb_ref[...] = jnp.zeros_like(gb_ref)

        for sub in range(TM // TMF):
            rows = pl.ds(sub * TMF, TMF)
            dx1 = ALPHA * dz2_ref[rows, :]
            for n in range(NFT):
                for half in range(2):
                    a = a_ref[rows, (2 * n + half) * FT:(2 * n + half + 1) * FT]
                    w = w_ref[pl.ds((half * NFT + n) * FT, FT), :]
                    dx1 = dx1 + _dot(a, w)
            xh = xh_ref[rows, :]
            gg_ref[...] += _colsum(dx1 * xh)
            gb_ref[...] += _colsum(dx1)
            dz1_ref[rows, :] = _layer_norm_bwd(dx1, xh, rstd_ref[rows, :], g_ref[...])

    vec = pl.BlockSpec((1, D), lambda m: (0, 0))
    row = pl.BlockSpec((TM, D), lambda m: (m, 0))
    return pl.pallas_call(
        body, name="ffn_up_bwd_ln1", grid=(T // TM,),
        in_specs=[pl.BlockSpec((TM, 2 * DFF), lambda m: (m, 0)),
                  pl.BlockSpec((2 * DFF, D), lambda m: (0, 0), pipeline_mode=pl.Buffered(1)),
                  row, row, pl.BlockSpec((TM, 1), lambda m: (m, 0)), vec],
        out_specs=(row, vec, vec),
        out_shape=(SDS((T, D), F32), SDS((1, D), F32), SDS((1, D), F32)),
    )(dpre, w_up, dz2, xh1, rstd1, g1)


def _grad_w_up(dpre, x1b):
    tk = 1024

    def body(a_ref, b_ref, o_ref, acc):
        k = pl.program_id(1)

        @pl.when(k == 0)
        def _():
            acc[...] = jnp.zeros_like(acc)

        acc[...] += _dot_tn(a_ref[...], b_ref[...])

        @pl.when(k == T // tk - 1)
        def _():
            o_ref[0] = acc[pl.ds(0, FT), :].astype(o_ref.dtype)
            o_ref[1] = acc[pl.ds(FT, FT), :].astype(o_ref.dtype)

    out = pl.pallas_call(
        body, name="grad_w_up", grid=(NFT, T // tk),
        in_specs=[pl.BlockSpec((tk, 2 * FT), lambda n, k: (k, n)), pl.BlockSpec((tk, D), lambda n, k: (k, 0))],
        out_specs=pl.BlockSpec((2, FT, D), lambda n, k: (0, n, 0)),
        out_shape=SDS((2, DFF, D), GRAD_WIRE),
        scratch_shapes=[pltpu.VMEM((2 * FT, D), F32)],
    )(dpre, x1b)
    return out.reshape(2 * DFF, D)


def _row_tile(rows, cols):
    if rows * cols * 4 <= (1 << 20) or rows % 8:
        return rows
    for t in (256, 176, 128, 88, 64, 32, 16, 8):
        if rows % t == 0 and t * cols * 4 <= (1 << 20):
            return t
    return 8


def _sum8(r, name):
    _, rows, cols = r.shape
    tr = _row_tile(rows, cols)

    def body(r_ref, o_ref):
        acc = r_ref[0].astype(F32)
        for p in range(1, NDEV):
            acc = acc + r_ref[p].astype(F32)
        o_ref[...] = acc

    return pl.pallas_call(
        body, name=name, grid=(rows // tr,),
        in_specs=[pl.BlockSpec((NDEV, tr, cols), lambda i: (0, i, 0))],
        out_specs=pl.BlockSpec((tr, cols), lambda i: (i, 0)),
        out_shape=SDS((rows, cols), F32),
    )(r)


def _sum8_adamw(r, w, m, v, name):
    rows, cols = w.shape
    tr = _row_tile(rows, cols)

    def body(r_ref, w_ref, m_ref, v_ref, g_out, d_ref, nm_ref, nv_ref):
        g_ = r_ref[0].astype(F32)
        for p in range(1, NDEV):
            g_ = g_ + r_ref[p].astype(F32)
        m_ = B1 * m_ref[...] + (1.0 - B1) * g_
        v_ = B2 * v_ref[...] + (1.0 - B2) * jnp.square(g_)
        m_hat = m_ / (1.0 - B1 ** STEP)
        v_hat = v_ / (1.0 - B2 ** STEP)
        g_out[...] = g_
        d_ref[...] = -LR * (m_hat / (jnp.sqrt(v_hat) + AEPS) + WD * w_ref[...])
        nm_ref[...] = m_
        nv_ref[...] = v_

    spec = pl.BlockSpec((tr, cols), lambda i: (i, 0))
    shp = SDS((rows, cols), F32)
    return pl.pallas_call(
        body, name=name, grid=(rows // tr,),
        in_specs=[pl.BlockSpec((NDEV, tr, cols), lambda i: (0, i, 0))] + [spec] * 3, out_specs=(spec,) * 4,
        out_shape=(shp,) * 4,
    )(r, w, m, v)


def _adamw_many(ws, gs, ms, vs, name):
    n = len(ws)

    def body(*refs):
        for i in range(n):
            w_ref, g_ref, m_ref, v_ref, d_ref, nm_ref, nv_ref = refs[i::n]
            g_ = g_ref[...]
            m_ = B1 * m_ref[...] + (1.0 - B1) * g_
            v_ = B2 * v_ref[...] + (1.0 - B2) * jnp.square(g_)
            m_hat = m_ / (1.0 - B1 ** STEP)
            v_hat = v_ / (1.0 - B2 ** STEP)
            d_ref[...] = -LR * (m_hat / (jnp.sqrt(v_hat) + AEPS) + WD * w_ref[...])
            nm_ref[...] = m_
            nv_ref[...] = v_

    shapes = tuple(SDS(w.shape, F32) for w in ws)
    res = pl.pallas_call(body, name=name, out_shape=shapes * 3)(*ws, *gs, *ms, *vs)
    return res[:n], res[n:2 * n], res[2 * n:]


def _local_step(x2, target, rel_table, first_weights, b_in, conv_b, conv_ln_g, conv_ln_b, attn_norm_g,
                conv_norm_g, late_weights, ln1_g, ln1_b, ffn_conv_b, ln2_g, ln2_b, ship_ffn_grads, ship_w_in_grads,
                ship_small_grads):
    buckets = jnp.asarray(_bucket_maps())
    bias = _bias_table(rel_table, buckets)
    xb, cast_done = _cast_x(x2, bias[0, 0, :8, :BLK])
    w_in_t, conv_w, ffn_conv_w = first_weights(cast_done)

    qkv, ag = _proj_in(xb, w_in_t, b_in)
    attn, lse = _attn_fwd(qkv, bias)
    u1 = _conv_fwd(ag, conv_w, conv_b)
    mixed = _mix_fwd(attn, u1, attn_norm_g, conv_norm_g, conv_ln_g, conv_ln_b)
    w_out = late_weights(0, mixed)
    xh1, rstd1, x1b = _out_proj_ln1(mixed, w_out, x2, ln1_g, ln1_b)
    w_up = late_weights(1, x1b)
    up, gv, act = _ffn_up(x1b, w_up, ffn_conv_w, ffn_conv_b)
    w_down = late_weights(2, act)
    dz2, loss, g_ln2_g, g_ln2_b = _ffn_down_loss(act, w_down, xh1, ln1_g, ln1_b, ln2_g, ln2_b, target)

    dpre, cs_g, cs_v, gfw_g, gfw_v = _ffn_down_bwd(dz2, w_down, gv, up, ffn_conv_w)
    g_w_down = _mm_tn(act, dz2, DFF // 2, 512, "grad_w_down")
    dz1, g_ln1_g, g_ln1_b = _ffn_up_bwd_ln1(dpre, w_up, dz2, xh1, rstd1, ln1_g)
    g_w_out = _mm_tn(mixed, dz1, D, 512, "grad_w_out")
    zero = ship_ffn_grads(g_w_down, _grad_w_up(dpre, x1b), g_w_out)
    dattn, du1, g_an, g_cn, g_clg, g_clb, g_cb = _mix_bwd(
        dz1, w_out, attn, u1, attn_norm_g + zero, conv_norm_g, conv_ln_g, conv_ln_b)
    dag, cs_ag, g_conv_w = _conv_bwd(du1, ag, conv_w)
    dq, dk, dv, cs_q, cs_k, cs_v2, dbias = _attn_bwd(qkv, attn, lse, dattn, bias)
    pieces = [dq, dk, dv, dag]
    zero_a = ship_w_in_grads(_grad_w_in(pieces, x2))
    g_rel = _rel_table_grad(dbias, buckets, zero_a)

    grads = dict(
        rel_table=g_rel,
        b_in=jnp.concatenate([cs_q, cs_k, cs_v2, cs_ag], axis=1),
        conv_b=g_cb, conv_ln_g=g_clg, conv_ln_b=g_clb, attn_norm_g=g_an, conv_norm_g=g_cn,
        ln1_g=g_ln1_g, ln1_b=g_ln1_b,
        ffn_conv_b=jnp.concatenate([cs_g, cs_v], axis=1),
        ln2_g=g_ln2_g, ln2_b=g_ln2_b,
        conv_w=g_conv_w[:CK],
        ffn_conv_w=jnp.concatenate([gfw_g, gfw_v], axis=1),
    )
    grads["loss"] = loss
    grad_x = _grad_x(pieces, w_in_t, dz1, ship_small_grads(grads))
    return loss, grad_x


SMALL = (("rel_table", (NBUCKET, NH)), ("b_in", (1, INW)), ("conv_b", (1, CW)), ("conv_ln_g", (1, CW)),
         ("conv_ln_b", (1, CW)), ("attn_norm_g", (1, AW)), ("conv_norm_g", (1, CW)), ("ln1_g", (1, D)),
         ("ln1_b", (1, D)), ("ffn_conv_b", (1, 2 * DFF)), ("ln2_g", (1, D)), ("ln2_b", (1, D)))
SHARDED_SMALL = (("conv_w", (CK, CW)), ("ffn_conv_w", (FK, 2 * DFF)))


def _pack(parts):
    flat = jnp.concatenate([p.reshape(-1) for p in parts])
    tile = 8 * PACK_LANES
    pad = (-flat.shape[0]) % tile
    return jnp.pad(flat, (0, pad)).reshape(-1, PACK_LANES)


def _unpack(packed, specs):
    flat = packed.reshape(-1)
    out, off = {}, 0
    for name, shp in specs:
        size = int(np.prod(shp))
        out[name] = flat[off:off + size].reshape(shp)
        off += size
    return out


def kernel(x, rel_table, w_in, b_in, conv_w, conv_b, conv_ln_g, conv_ln_b, attn_norm_g, conv_norm_g, w_out, ln1_g, ln1_b, w_up, ffn_conv_w, ffn_conv_b, w_down, ln2_g, ln2_b, loss_target, m_rel_table, m_w_in, m_b_in, m_conv_w, m_conv_b, m_conv_ln_g, m_conv_ln_b, m_attn_norm_g, m_conv_norm_g, m_w_out, m_ln1_g, m_ln1_b, m_w_up, m_ffn_conv_w, m_ffn_conv_b, m_w_down, m_ln2_g, m_ln2_b, v_rel_table, v_w_in, v_b_in, v_conv_w, v_conv_b, v_conv_ln_g, v_conv_ln_b, v_attn_norm_g, v_conv_norm_g, v_w_out, v_ln1_g, v_ln1_b, v_w_up, v_ffn_conv_w, v_ffn_conv_b, v_w_down, v_ln2_g, v_ln2_b):
    given = dict(locals())
    me = 4 * lax.axis_index("x") + 2 * lax.axis_index("y") + lax.axis_index("c")

    cols = lambda a: a.transpose(1, 0, 2).reshape(a.shape[1], NDEV * a.shape[2])
    rows = lambda a: a.reshape(NDEV * a.shape[1], a.shape[2])
    stack = lambda a: a.reshape(NDEV, a.shape[0] // NDEV, a.shape[1])

    small_specs = SMALL + SHARDED_SMALL
    packed_specs = small_specs + (("loss", (1, 1)),)
    grad, delta, new_m, new_v = {}, {}, {}, {}

    def adamw_big(n, partials, transposed=False):
        shp = given[n].shape
        to2d = (lambda a: a.reshape(shp[-2], shp[-1]).T) if transposed else (lambda a: a.reshape(shp[-2], shp[-1]))
        back = (lambda a: a.T.reshape(shp)) if transposed else (lambda a: a.reshape(shp))
        g_, d_, m_, v_ = _sum8_adamw(partials, to2d(given[n]), to2d(given["m_" + n]), to2d(given["v_" + n]), "adamw_" + n)
        grad[n], delta[n], new_m[n], new_v[n] = back(g_), back(d_), back(m_), back(v_)
        return d_

    first_state, zero0 = _exchange_start(
        [(w_in[0].T.astype(BF16), "gather"), (conv_w[0], "gather"), (ffn_conv_w[0], "gather")], "gather_first_start")

    def first_weights(after):
        lands = _exchange_wait(first_state, after, "gather_first_wait")
        return rows(lands[0]), cols(lands[1]), cols(lands[2])

    late_state, zero1 = _exchange_start(
        [(w_out[0].astype(BF16), "gather"), (w_up[0].T.astype(BF16) + zero0.astype(BF16), "gather"),
         (w_down[0].astype(BF16), "gather")], "gather_late_start")

    def late_weights(i, after):
        return rows(_exchange_wait(late_state, after, "gather_late_wait_%d" % i, only=(i,))[i])

    shipped = {}

    def ship_ffn_grads(g_w_down, g_w_up_t, g_w_out):
        shipped["ffn"], zero2 = _exchange_start(
            [(stack(a), "scatter") for a in (g_w_down, g_w_up_t, g_w_out)], "ffn_grads_start")
        return zero2

    def ship_w_in_grads(g_w_in_t):
        shipped["w_in"], zero3 = _exchange_start([(stack(g_w_in_t), "scatter")], "w_in_grads_start")
        return zero3.reshape(1, 1)

    def ship_small_grads(small_grads):
        shipped["small"], zero4 = _exchange_start(
            [(_pack([small_grads[n] for n, _ in packed_specs]), "gather")], "small_grads_start")
        return zero4.reshape(1, 1)

    loss, grad_x = _local_step(
        x.reshape(T, D), loss_target.reshape(T, D), rel_table + zero1, first_weights, b_in, conv_b, conv_ln_g,
        conv_ln_b, attn_norm_g, conv_norm_g, late_weights, ln1_g, ln1_b, ffn_conv_b,
        ln2_g, ln2_b, ship_ffn_grads, ship_w_in_grads, ship_small_grads)

    got_down, got_up, got_out = _exchange_wait(shipped["ffn"], grad_x, "ffn_grads_wait")
    adamw_big("w_down", got_down)
    adamw_big("w_up", got_up, transposed=True)
    last = adamw_big("w_out", got_out)

    (got_in,) = _exchange_wait(shipped["w_in"], last, "w_in_grads_wait")
    (got_small,) = _exchange_wait(shipped["small"], last, "small_grads_wait")
    adamw_big("w_in", got_in, transposed=True)
    small = _unpack(_sum8(got_small, "sum_small"), packed_specs)
    small["conv_w"] = lax.dynamic_slice_in_dim(small["conv_w"], me * (CW // NDEV), CW // NDEV, axis=1)
    small["ffn_conv_w"] = lax.dynamic_slice_in_dim(small["ffn_conv_w"], me * (2 * DFF // NDEV), 2 * DFF // NDEV, axis=1)
    names = [n for n, _ in small_specs]
    two = lambda a: a.reshape(a.shape[-2], a.shape[-1])
    ds, nms, nvs = _adamw_many([two(given[n]) for n in names], [small[n] for n in names],
                               [two(given["m_" + n]) for n in names], [two(given["v_" + n]) for n in names], "adamw_small")
    for n, d_, m_, v_ in zip(names, ds, nms, nvs):
        shp = given[n].shape
        grad[n], delta[n], new_m[n], new_v[n] = small[n].reshape(shp), d_.reshape(shp), m_.reshape(shp), v_.reshape(shp)

    order = ("rel_table", "w_in", "b_in", "conv_w", "conv_b", "conv_ln_g", "conv_ln_b", "attn_norm_g",
             "conv_norm_g", "w_out", "ln1_g", "ln1_b", "w_up", "ffn_conv_w", "ffn_conv_b", "w_down", "ln2_g", "ln2_b")
    return (small["loss"][0, 0], grad_x.reshape(BL, S, D), *[grad[n] for n in order], *[delta[n] for n in order],
            *[new_m[n] for n in order], *[new_v[n] for n in order])
```

```python
import math

import numpy as np
import jax
import jax.numpy as jnp
from jax import lax
from jax.experimental import pallas as pl
from jax.experimental.pallas import tpu as pltpu

F32 = jnp.float32
BF16 = jnp.bfloat16
SDS = jax.ShapeDtypeStruct

NDEV = 8
D = 1024
S = 2048
BL = 2
T = BL * S
NH = 12
HD = 64
AW = NH * HD
CW = D - AW
INW = 3 * AW + 2 * CW
CK = 31
DFF = 2816
FK = 3
BLK = 128
NBUCKET = 32
BRANCHES = ((128, 1), (512, 4), (2048, 16))
ALPHA = 2.0 ** 0.25
LN_EPS = 1e-5
NEG_INF = -1e30
LR, B1, B2, AEPS, WD, STEP = 0.001, 0.9, 0.999, 1e-08, 0.01, 10

TM = 512
FT = 1408
NFT = DFF // FT
TMF = 256
PACK_LANES = 128
GRAD_WIRE = BF16

assert all(w // d == BLK for w, d in BRANCHES)


def _dot(a, b):
    return jnp.dot(a, b, preferred_element_type=F32)


def _dot_nt(a, b):
    return lax.dot_general(a, b, (((1,), (1,)), ((), ())), preferred_element_type=F32)


def _dot_tn(a, b):
    return lax.dot_general(a, b, (((0,), (0,)), ((), ())), preferred_element_type=F32)


def _rowmean(v):
    return jnp.mean(v, axis=-1, keepdims=True)


def _colsum(v):
    return jnp.sum(v, axis=0, keepdims=True)


def _sigmoid(v):
    return jax.nn.sigmoid(v)


_HBM = pl.BlockSpec(memory_space=pltpu.HBM)
_SEM = pl.BlockSpec(memory_space=pltpu.SEMAPHORE)
_EFFECT = pltpu.SideEffectType.DATAFLOW_SIDE_EFFECTING


def _peer_of(k):
    x, y, c = lax.axis_index("x"), lax.axis_index("y"), lax.axis_index("c")
    px = 1 - x if k & 4 else x
    py = 1 - y if k & 2 else y
    pc = 1 - c if k & 1 else c
    return (px, py, pc), 4 * px + 2 * py + pc


def _split_copies(kinds, ins, lands, send_sems, recv_sems, started):
    me = 4 * lax.axis_index("x") + 2 * lax.axis_index("y") + lax.axis_index("c")
    out = []
    for i, kind in enumerate(kinds):
        for k in range(1, NDEV):
            dev, pid = _peer_of(k)
            src = ins[i] if kind == "gather" else ins[i].at[pid]
            dst = lands[i].at[me] if started else lands[i].at[pid]
            slot = i * (NDEV - 1) + k - 1
            out.append(pltpu.make_async_remote_copy(
                src_ref=src, dst_ref=dst, send_sem=send_sems.at[slot], recv_sem=recv_sems.at[slot],
                device_id=dev, device_id_type=pl.DeviceIdType.MESH))
    return out


def _exchange_start(items, name):
    n = len(items)
    kinds = [k for _, k in items]
    srcs = [pltpu.with_memory_space_constraint(a, pltpu.HBM) for a, _ in items]
    lands = []
    for a, k in items:
        shp = (NDEV,) + tuple(a.shape) if k == "gather" else tuple(a.shape)
        lands.append(pltpu.with_memory_space_constraint(lax.empty(shp, a.dtype), pltpu.HBM))

    def body(*refs):
        ins, land_refs = refs[:n], refs[n:2 * n]
        send_sems, recv_sems, own_sems = refs[2 * n:2 * n + 3]
        token = refs[-1]
        for cp in _own_copies(kinds, ins, land_refs, own_sems):
            cp.start()
        for cp in _split_copies(kinds, ins, land_refs, send_sems, recv_sems, True):
            cp.start()
        token[...] = jnp.zeros_like(token)

    sems = pltpu.SemaphoreType.DMA((n * (NDEV - 1),))
    res = pl.pallas_call(
        body, name=name,
        out_shape=(sems, sems, pltpu.SemaphoreType.DMA((n,)),
                   *[pltpu.HBM(a.shape, a.dtype) for a in srcs + lands], SDS((8, 128), F32)),
        in_specs=[_HBM] * (2 * n),
        out_specs=(_SEM, _SEM, _SEM, *[_HBM] * (2 * n), pl.BlockSpec(memory_space=pltpu.VMEM)),
        input_output_aliases={i: 3 + i for i in range(2 * n)},
        compiler_params=pltpu.CompilerParams(has_side_effects=_EFFECT),
    )(*srcs, *lands)
    return (kinds, res[0], res[1], res[2], list(res[3:3 + n]), list(res[3 + n:3 + 2 * n])), res[-1][0, 0]


def _own_copies(kinds, ins, lands, own_sems):
    me = 4 * lax.axis_index("x") + 2 * lax.axis_index("y") + lax.axis_index("c")
    return [pltpu.make_async_copy(ins[i] if kind == "gather" else ins[i].at[me], lands[i].at[me], own_sems.at[i])
            for i, kind in enumerate(kinds)]


def _exchange_wait(state, after, name, only=None):
    kinds, send_sems, recv_sems, own_sems, srcs, lands = state
    n = len(kinds)
    chosen = range(n) if only is None else only

    def body(*refs):
        ins, land_refs = refs[:n], refs[n:2 * n]
        s_sems, r_sems, o_sems = refs[2 * n:2 * n + 3]
        remote = _split_copies(kinds, ins, land_refs, s_sems, r_sems, False)
        own = _own_copies(kinds, ins, land_refs, o_sems)
        for i in chosen:
            for cp in remote[i * (NDEV - 1):(i + 1) * (NDEV - 1)]:
                cp.wait_send()
                cp.wait_recv()
        for i in chosen:
            own[i].wait()

    res = pl.pallas_call(
        body, name=name,
        out_shape=tuple(pltpu.HBM(a.shape, a.dtype) for a in srcs + lands),
        in_specs=[_HBM] * (2 * n) + [_SEM, _SEM, _SEM, pl.BlockSpec(memory_space=pl.ANY)],
        out_specs=tuple([_HBM] * (2 * n)),
        input_output_aliases={i: i for i in range(2 * n)},
        compiler_params=pltpu.CompilerParams(has_side_effects=_EFFECT),
    )(*srcs, *lands, send_sems, recv_sems, own_sems, after)
    srcs[:], lands[:] = res[:n], res[n:]
    return list(lands)


def _cast_x(x2, after):
    def body(x_ref, after_ref, o_ref, done_ref):
        o_ref[...] = x_ref[...].astype(BF16)
        done_ref[...] = jnp.zeros_like(done_ref)

    return pl.pallas_call(
        body, name="cast_x", grid=(T // TM,),
        in_specs=[pl.BlockSpec((TM, D), lambda m: (m, 0)), pl.BlockSpec(memory_space=pl.ANY)],
        out_specs=(pl.BlockSpec((TM, D), lambda m: (m, 0)), pl.BlockSpec((8, 128), lambda m: (0, 0))),
        out_shape=(SDS((T, D), BF16), SDS((8, 128), F32)),
    )(x2, after)


def _proj_in(xb, w_in_t, b_in):
    nq = 3 * AW

    def body(x_ref, w_ref, b_ref, qkv_ref, ag_ref):
        xb = x_ref[...]
        qkv_ref[...] = (_dot_nt(xb, w_ref[pl.ds(0, nq), :]) + b_ref[:, :nq]).astype(BF16)
        ag_ref[...] = _dot_nt(xb, w_ref[pl.ds(nq, 2 * CW), :]) + b_ref[:, nq:]

    return pl.pallas_call(
        body, name="proj_in", grid=(T // TM,),
        in_specs=[pl.BlockSpec((TM, D), lambda m: (m, 0)), pl.BlockSpec((INW, D), lambda m: (0, 0)),
                  pl.BlockSpec((1, INW), lambda m: (0, 0))],
        out_specs=(pl.BlockSpec((TM, nq), lambda m: (m, 0)), pl.BlockSpec((TM, 2 * CW), lambda m: (m, 0))),
        out_shape=(SDS((T, nq), BF16), SDS((T, 2 * CW), F32)),
    )(xb, w_in_t, b_in)


def _grad_x(pieces, w_in_t, dz1, zero):
    widths = [p.shape[1] for p in pieces]

    def body(*refs):
        p_refs = refs[:len(pieces)]
        w_ref, dz_ref, z_ref, o_ref = refs[len(pieces):]
        acc = ALPHA * dz_ref[...] + z_ref[...]
        r0 = 0
        for p_ref, wd in zip(p_refs, widths):
            acc = acc + _dot(p_ref[...], w_ref[pl.ds(r0, wd), :])
            r0 += wd
        o_ref[...] = acc

    row = pl.BlockSpec((TM, D), lambda m: (m, 0))
    return pl.pallas_call(
        body, name="grad_x", grid=(T // TM,),
        in_specs=[pl.BlockSpec((TM, wd), lambda m: (m, 0)) for wd in widths]
        + [pl.BlockSpec((INW, D), lambda m: (0, 0)), row, pl.BlockSpec((1, 1), lambda m: (0, 0))],
        out_specs=row,
        out_shape=SDS((T, D), F32),
    )(*pieces, w_in_t, dz1, zero)


def _grad_w_in(pieces, x2):
    widths = [p.shape[1] for p in pieces]
    tk = 512
    nk = T // tk

    def body(*refs):
        p_refs = refs[:len(pieces)]
        x_ref, o_ref, acc = refs[len(pieces):]
        k = pl.program_id(0)

        @pl.when(k == 0)
        def _():
            acc[...] = jnp.zeros_like(acc)

        xb = x_ref[...].astype(BF16)
        r0 = 0
        for p_ref, wd in zip(p_refs, widths):
            acc[pl.ds(r0, wd), :] += _dot_tn(p_ref[...], xb)
            r0 += wd

        @pl.when(k == nk - 1)
        def _():
            o_ref[...] = acc[...].astype(o_ref.dtype)

    return pl.pallas_call(
        body, name="grad_w_in", grid=(nk,),
        in_specs=[pl.BlockSpec((tk, wd), lambda k: (k, 0)) for wd in widths] + [pl.BlockSpec((tk, D), lambda k: (k, 0))],
        out_specs=pl.BlockSpec((INW, D), lambda k: (0, 0)),
        out_shape=SDS((INW, D), GRAD_WIRE),
        scratch_shapes=[pltpu.VMEM((INW, D), F32)],
    )(*pieces, x2)


def _mm_tn(a, b, tn, tk, name):
    t_, na = a.shape
    nb = b.shape[1]
    nk = t_ // tk

    def body(a_ref, b_ref, o_ref, acc):
        k = pl.program_id(1)

        @pl.when(k == 0)
        def _():
            acc[...] = jnp.zeros_like(acc)

        acc[...] += _dot_tn(a_ref[...].astype(BF16), b_ref[...].astype(BF16))

        @pl.when(k == nk - 1)
        def _():
            o_ref[...] = acc[...].astype(o_ref.dtype)

    return pl.pallas_call(
        body, name=name, grid=(na // tn, nk),
        in_specs=[pl.BlockSpec((tk, tn), lambda n, k: (k, n)),
                  pl.BlockSpec((tk, nb), lambda n, k: (k, 0))],
        out_specs=pl.BlockSpec((tn, nb), lambda n, k: (n, 0)),
        out_shape=SDS((na, nb), GRAD_WIRE),
        scratch_shapes=[pltpu.VMEM((tn, nb), F32)],
    )(a, b)


def _bucket_maps():
    qi = np.arange(BLK)[:, None]
    kj = np.arange(2 * BLK)[None, :]
    steps = np.maximum(qi + BLK - kj, 0)
    exact = NBUCKET // 2
    maps = []
    for _, dil in BRANCHES:
        dist = steps * dil
        d_f = np.maximum(dist, 1).astype(np.float32)
        large = exact + (np.log(d_f / np.float32(exact)) / np.float32(math.log(S / exact))
                         * np.float32(NBUCKET - exact)).astype(np.int32)
        large = np.minimum(large, NBUCKET - 1)
        maps.append(np.where(dist < exact, dist, large).astype(np.int32))
    return np.stack(maps)


def _bias_table(rel_table, buckets):
    def body(t_ref, b_ref, o_ref):
        bk = b_ref[0]
        for h in range(NH):
            acc = jnp.zeros((BLK, 2 * BLK), F32)
            for k in range(NBUCKET):
                acc = jnp.where(bk == k, t_ref[k, h], acc)
            o_ref[0, h] = acc

    return pl.pallas_call(
        body, name="bias_table", grid=(len(BRANCHES),),
        in_specs=[pl.BlockSpec(memory_space=pltpu.SMEM),
                  pl.BlockSpec((1, BLK, 2 * BLK), lambda i: (i, 0, 0))],
        out_specs=pl.BlockSpec((1, NH, BLK, 2 * BLK), lambda i: (i, 0, 0, 0)),
        out_shape=SDS((len(BRANCHES), NH, BLK, 2 * BLK), F32),
    )(rel_table, buckets)


def _rel_table_grad(dbias, buckets, after):
    def body(d_ref, b_ref, after_ref, o_ref):
        for k in range(NBUCKET):
            tot = jnp.zeros((1, 1), F32)
            for br in range(len(BRANCHES)):
                sel = jnp.where(b_ref[br] == k, d_ref[br, 0], 0.0)
                tot = tot + jnp.sum(jnp.sum(sel, axis=1, keepdims=True), axis=0, keepdims=True)
            o_ref[0, :, pl.ds(k, 1)] = tot

    out = pl.pallas_call(
        body, name="rel_table_grad", grid=(NH,),
        in_specs=[pl.BlockSpec((len(BRANCHES), 1, BLK, 2 * BLK), lambda h: (0, h, 0, 0)),
                  pl.BlockSpec((len(BRANCHES), BLK, 2 * BLK), lambda h: (0, 0, 0)),
                  pl.BlockSpec(memory_space=pl.ANY)],
        out_specs=pl.BlockSpec((1, 1, NBUCKET), lambda h: (h, 0, 0)),
        out_shape=SDS((NH, 1, NBUCKET), F32),
    )(dbias, buckets, after)
    return out.reshape(NH, NBUCKET).T


PADK = BLK
SCALE = 1.0 / math.sqrt(HD)
NBLK = S // BLK
ACH = 256


def _branch_geometry(br):
    dil = BRANCHES[br][1]
    sub = S // dil
    return dil, sub, sub // BLK


def _token_rows(br, i):
    dil, _, nblk = _branch_geometry(br)
    if dil == 1:
        return pl.ds(pl.multiple_of(i * BLK, BLK), BLK), i
    r = lax.shift_right_logical(i, nblk.bit_length() - 1)
    n = lax.bitwise_and(i, nblk - 1)
    return pl.ds(r + dil * BLK * n, BLK, stride=dil), n


def _sub_layout_loop(br, step):
    dil, sub, _ = _branch_geometry(br)
    rows = min(sub, ACH)
    nchunk = sub // rows

    def it_step(it, carry):
        if dil == 1:
            src = pl.ds(pl.multiple_of(it * rows, rows), rows)
        else:
            r = lax.shift_right_logical(it, nchunk.bit_length() - 1)
            src = pl.ds(r + dil * rows * lax.bitwise_and(it, nchunk - 1), rows, stride=dil)
        step(src, pl.multiple_of(it * rows, BLK), rows)
        return carry

    lax.fori_loop(0, dil * nchunk, it_step, 0, unroll=True)


def _masked_bias(bias_ref, bm):
    qi = lax.broadcasted_iota(jnp.int32, (BLK, 2 * BLK), 0)
    kj = lax.broadcasted_iota(jnp.int32, (BLK, 2 * BLK), 1)
    first = jnp.logical_and(kj >= BLK, kj - BLK <= qi)
    valid = jnp.logical_or(first, jnp.logical_and(kj < BLK, kj >= qi))
    for br in range(len(BRANCHES)):
        for j in range(2):
            b = bias_ref[br, j]
            bm[br, 1, pl.ds(j * BLK, BLK), :] = jnp.where(valid, b, NEG_INF)
            bm[br, 0, pl.ds(j * BLK, BLK), :] = jnp.where(first, b, NEG_INF)


def _head_split(fn):
    def split(t):
        h0 = lax.broadcasted_iota(jnp.int32, t.shape, 1) < HD
        t = fn(t)
        return jnp.where(h0, t, 0.0).astype(BF16), jnp.where(h0, 0.0, t).astype(BF16)
    return split


def _attn_fwd(qkv, bias):
    nbr = len(BRANCHES)

    def body(q_ref, k_ref, v_ref, bias_ref, o_ref, lse_ref, qf, kf, vf, qs0, qs1, ks, vs, bm, ob, mb, lb):
        qf[...] = q_ref[...].astype(F32)
        kf[...] = k_ref[...].astype(F32)
        vf[...] = v_ref[...].astype(F32)
        _masked_bias(bias_ref, bm)
        ks[pl.ds(0, PADK), :] = jnp.zeros((PADK, BLK), BF16)
        vs[pl.ds(0, PADK), :] = jnp.zeros((PADK, BLK), BF16)
        head0 = lax.broadcasted_iota(jnp.int32, (BLK, BLK), 1) < HD
        split_q = _head_split(lambda t: t * SCALE)

        for br in range(nbr):
            nblk = _branch_geometry(br)[2]

            def stage(src, off, rows):
                qs0[pl.ds(off, rows), :], qs1[pl.ds(off, rows), :] = split_q(qf[src, :])
                ks[pl.ds(PADK + off, rows), :] = kf[src, :].astype(BF16)
                vs[pl.ds(PADK + off, rows), :] = vf[src, :].astype(BF16)

            _sub_layout_loop(br, stage)

            def blk(i, carry, br=br, nblk=nblk):
                base = pl.multiple_of(i * BLK, BLK)
                rows, n = _token_rows(br, i)
                q01 = jnp.concatenate([qs0[pl.ds(base, BLK), :], qs1[pl.ds(base, BLK), :]], axis=0)
                if nblk > 1:
                    kcat = ks[pl.ds(base, 2 * BLK), :]
                    vcat = vs[pl.ds(base, 2 * BLK), :]
                    s = _dot_nt(q01, kcat) + bm[br, jnp.minimum(n, 1)]
                else:
                    kcat = ks[pl.ds(PADK + base, BLK), :]
                    vcat = vs[pl.ds(PADK + base, BLK), :]
                    s = _dot_nt(q01, kcat) + bm[br, 0, :, BLK:]
                mx = jnp.max(s, axis=-1, keepdims=True)
                p = jnp.exp(s - mx)
                ls = jnp.sum(p, axis=-1, keepdims=True)
                o = _dot(p.astype(BF16), vcat)
                ob[br, rows, :] = jnp.where(head0, o[:BLK], o[BLK:])
                mb[br, rows, :] = jnp.where(head0, mx[:BLK], mx[BLK:])
                lb[br, rows, :] = jnp.where(head0, ls[:BLK], ls[BLK:])
                return carry

            lax.fori_loop(0, NBLK, blk, 0, unroll=True)

        def merge(i, carry):
            rows = pl.ds(pl.multiple_of(i * ACH, ACH), ACH)
            m_all = jnp.maximum(jnp.maximum(mb[0, rows, :], mb[1, rows, :]), mb[2, rows, :])
            num = jnp.zeros((ACH, BLK), F32)
            den = jnp.zeros((ACH, BLK), F32)
            for br in range(nbr):
                c = jnp.exp(mb[br, rows, :] - m_all)
                num = num + ob[br, rows, :] * c
                den = den + lb[br, rows, :] * c
            o_ref[rows, :] = num / den
            lse_ref[rows, :] = m_all + jnp.log(den)
            return carry

        lax.fori_loop(0, S // ACH, merge, 0)

    npair = NH // 2
    blk_spec = lambda off: pl.BlockSpec((S, BLK), lambda b, hp: (b, off + hp))
    return pl.pallas_call(
        body, name="attn_fwd", grid=(BL, npair),
        in_specs=[blk_spec(0), blk_spec(npair), blk_spec(2 * npair),
                  pl.BlockSpec((nbr, 2, BLK, 2 * BLK), lambda b, hp: (0, hp, 0, 0))],
        out_specs=(blk_spec(0), blk_spec(0)),
        out_shape=(SDS((T, AW), F32), SDS((T, AW), F32)),
        scratch_shapes=[pltpu.VMEM((S, BLK), F32)] * 3 + [pltpu.VMEM((S, BLK), BF16)] * 2
        + [pltpu.VMEM((PADK + S, BLK), BF16)] * 2 + [pltpu.VMEM((nbr, 2, 2 * BLK, 2 * BLK), F32)]
        + [pltpu.VMEM((nbr, S, BLK), F32)] * 3,
    )(qkv, qkv, qkv, bias)


def _attn_bwd(qkv, attn, lse, dattn, bias):
    nbr = len(BRANCHES)

    def body(q_ref, k_ref, v_ref, o_ref, lse_ref, do_ref, bias_ref,
             dq_ref, dk_ref, dv_ref, sq_ref, sk_ref, sv_ref, db_ref,
             qf, kf, vf, dl, dqa, dka, dva, qs0, qs1, ds0, ds1, ks, vs, dks, dvs, bm):
        b = pl.program_id(1)
        qf[...] = q_ref[...].astype(F32)
        kf[...] = k_ref[...].astype(F32)
        vf[...] = v_ref[...].astype(F32)
        dqa[...] = jnp.zeros_like(dqa)
        dka[...] = jnp.zeros_like(dka)
        dva[...] = jnp.zeros_like(dva)
        _masked_bias(bias_ref, bm)
        ks[pl.ds(0, PADK), :] = jnp.zeros((PADK, BLK), BF16)
        vs[pl.ds(0, PADK), :] = jnp.zeros((PADK, BLK), BF16)
        head0 = lax.broadcasted_iota(jnp.int32, (BLK, BLK), 1) < HD
        split_q = _head_split(lambda t: t * SCALE)
        split_do = _head_split(lambda t: t)

        @pl.when(b == 0)
        def _():
            db_ref[...] = jnp.zeros_like(db_ref)
            sq_ref[...] = jnp.zeros_like(sq_ref)
            sk_ref[...] = jnp.zeros_like(sk_ref)
            sv_ref[...] = jnp.zeros_like(sv_ref)

        def delta(i, carry):
            rows = pl.ds(pl.multiple_of(i * ACH, ACH), ACH)
            prod = do_ref[rows, :] * o_ref[rows, :]
            h0 = lax.broadcasted_iota(jnp.int32, (ACH, BLK), 1) < HD
            d0 = jnp.sum(jnp.where(h0, prod, 0.0), axis=-1, keepdims=True)
            d1 = jnp.sum(jnp.where(h0, 0.0, prod), axis=-1, keepdims=True)
            dl[rows, :] = jnp.where(h0, d0, d1)
            return carry

        lax.fori_loop(0, S // ACH, delta, 0, unroll=True)

        for br in range(nbr):
            nblk = _branch_geometry(br)[2]

            def stage(src, off, rows):
                qs0[pl.ds(off, rows), :], qs1[pl.ds(off, rows), :] = split_q(qf[src, :])
                ds0[pl.ds(off, rows), :], ds1[pl.ds(off, rows), :] = split_do(do_ref[src, :])
                ks[pl.ds(PADK + off, rows), :] = kf[src, :].astype(BF16)
                vs[pl.ds(PADK + off, rows), :] = vf[src, :].astype(BF16)

            _sub_layout_loop(br, stage)
            dks[...] = jnp.zeros_like(dks)
            dvs[...] = jnp.zeros_like(dvs)

            def blk(i, carry, br=br, nblk=nblk):
                base = pl.multiple_of(i * BLK, BLK)
                rows, n = _token_rows(br, i)
                q01 = jnp.concatenate([qs0[pl.ds(base, BLK), :], qs1[pl.ds(base, BLK), :]], axis=0)
                do01 = jnp.concatenate([ds0[pl.ds(base, BLK), :], ds1[pl.ds(base, BLK), :]], axis=0)
                lse_b = lse_ref[rows, :]
                dl_b = dl[rows, :]
                lse01 = jnp.concatenate([lse_b[:, 0:1], lse_b[:, HD:HD + 1]], axis=0)
                dl01 = jnp.concatenate([dl_b[:, 0:1], dl_b[:, HD:HD + 1]], axis=0)
                if nblk > 1:
                    krows = pl.ds(base, 2 * BLK)
                    bias_m = bm[br, jnp.minimum(n, 1)]
                else:
                    krows = pl.ds(PADK + base, BLK)
                    bias_m = bm[br, 0, :, BLK:]
                kcat = ks[krows, :]
                vcat = vs[krows, :]
                p = jnp.exp(_dot_nt(q01, kcat) + bias_m - lse01)
                dsv = p * (_dot_nt(do01, vcat) - dl01)
                if nblk > 1:
                    db_ref[br, 0] += dsv[:BLK]
                    db_ref[br, 1] += dsv[BLK:]
                else:
                    db_ref[br, 0, :, BLK:] += dsv[:BLK]
                    db_ref[br, 1, :, BLK:] += dsv[BLK:]
                dsb = dsv.astype(BF16)
                dq01 = _dot(dsb, kcat)
                dqa[rows, :] = dqa[rows, :] + jnp.where(head0, dq01[:BLK], dq01[BLK:])
                dks[krows, :] = dks[krows, :] + _dot_tn(dsb, q01)
                dvs[krows, :] = dvs[krows, :] + _dot_tn(p.astype(BF16), do01)
                return carry

            lax.fori_loop(0, NBLK, blk, 0, unroll=True)

            def fold(src, off, rows):
                dka[src, :] = dka[src, :] + dks[pl.ds(PADK + off, rows), :]
                dva[src, :] = dva[src, :] + dvs[pl.ds(PADK + off, rows), :]

            _sub_layout_loop(br, fold)

        def flush(i, carry):
            rows = pl.ds(pl.multiple_of(i * ACH, ACH), ACH)
            for acc, out, cs, mul in ((dqa, dq_ref, sq_ref, SCALE), (dka, dk_ref, sk_ref, 1.0), (dva, dv_ref, sv_ref, 1.0)):
                val = acc[rows, :] * mul
                out[rows, :] = val.astype(BF16)
                cs[...] += _colsum(val)
            return carry

        lax.fori_loop(0, S // ACH, flush, 0, unroll=True)

    npair = NH // 2
    blk_spec = lambda off: pl.BlockSpec((S, BLK), lambda hp, b: (b, off + hp))
    sum_spec = pl.BlockSpec((1, BLK), lambda hp, b: (0, hp))
    return pl.pallas_call(
        body, name="attn_bwd", grid=(npair, BL),
        in_specs=[blk_spec(0), blk_spec(npair), blk_spec(2 * npair), blk_spec(0), blk_spec(0), blk_spec(0),
                  pl.BlockSpec((nbr, 2, BLK, 2 * BLK), lambda hp, b: (0, hp, 0, 0))],
        out_specs=(blk_spec(0), blk_spec(0), blk_spec(0), sum_spec, sum_spec, sum_spec,
                   pl.BlockSpec((nbr, 2, BLK, 2 * BLK), lambda hp, b: (0, hp, 0, 0))),
        out_shape=(SDS((T, AW), BF16), SDS((T, AW), BF16), SDS((T, AW), BF16),
                   SDS((1, AW), F32), SDS((1, AW), F32), SDS((1, AW), F32),
                   SDS((nbr, NH, BLK, 2 * BLK), F32)),
        scratch_shapes=[pltpu.VMEM((S, BLK), F32)] * 7 + [pltpu.VMEM((S, BLK), BF16)] * 4
        + [pltpu.VMEM((PADK + S, BLK), BF16)] * 2 + [pltpu.VMEM((PADK + S, BLK), F32)] * 2
        + [pltpu.VMEM((nbr, 2, 2 * BLK, 2 * BLK), F32)],
    )(qkv, qkv, qkv, attn, lse, dattn, bias)


CH = 256
PADR = 32


def _tap_phases(offset_of_tap):
    taps = sorted((offset_of_tap(k) % 8, offset_of_tap(k) - offset_of_tap(k) % 8, k) for k in range(CK))
    assert all(lo + CH + ph <= CH + PADR for ph, lo, _ in taps)
    return taps


def _rows_up(win):
    made = {0: win}

    def get(phase):
        if phase not in made:
            made[phase] = pltpu.roll(win, win.shape[0] - phase, 0)
        return made[phase]
    return get


def _conv_fwd(ag, conv_w, conv_b):
    def body(ag_ref, w_ref, b_ref, u1_ref, u0p):
        u0p[pl.ds(0, PADR), :] = jnp.zeros((PADR, CW), F32)

        def glu(i, carry):
            t0 = pl.multiple_of(i * CH, CH)
            a = ag_ref[pl.ds(t0, CH), :CW]
            g = ag_ref[pl.ds(t0, CH), CW:]
            u0p[pl.ds(PADR + t0, CH), :] = a * _sigmoid(g)
            return carry

        lax.fori_loop(0, S // CH, glu, 0)

        def conv(i, carry):
            t0 = pl.multiple_of(i * CH, CH)
            win = u0p[pl.ds(t0, CH + PADR), :]
            acc = jnp.zeros((CH, CW), F32) + b_ref[...]
            up = _rows_up(win)
            for phase, lo, k in _tap_phases(lambda k: PADR - (CK - 1) + k):
                acc = acc + up(phase)[lo:lo + CH, :] * w_ref[k:k + 1, :]
            u1_ref[pl.ds(t0, CH), :] = acc
            return carry

        lax.fori_loop(0, S // CH, conv, 0)

    return pl.pallas_call(
        body, name="conv_fwd", grid=(BL,),
        in_specs=[pl.BlockSpec((S, 2 * CW), lambda b: (b, 0)),
                  pl.BlockSpec((CK, CW), lambda b: (0, 0)),
                  pl.BlockSpec((1, CW), lambda b: (0, 0))],
        out_specs=pl.BlockSpec((S, CW), lambda b: (b, 0)),
        out_shape=SDS((T, CW), F32),
        scratch_shapes=[pltpu.VMEM((S + PADR, CW), F32)],
    )(ag, conv_w, conv_b)


def _conv_post(u1, cg, cb):
    mu = _rowmean(u1)
    uc = u1 - mu
    rstd = lax.rsqrt(_rowmean(uc * uc) + LN_EPS)
    xh = uc * rstd
    u2 = xh * cg + cb
    sg = _sigmoid(u2)
    return xh, rstd, u2, sg, u2 * sg


def _mix_bwd(dz1, w_out, attn, u1, ga, gc, cg, cb):
    def body(dz_ref, w_ref, a_ref, u_ref, ga_ref, gc_ref, cg_ref, cb_ref,
             da_ref, du_ref, g_an, g_cn, g_lg, g_lb, g_cb):
        @pl.when(pl.program_id(0) == 0)
        def _():
            for r in (g_an, g_cn, g_lg, g_lb, g_cb):
                r[...] = jnp.zeros_like(r)

        dm = _dot_nt(dz_ref[...].astype(BF16), w_ref[...])
        a = a_ref[...]
        dna = dm[:, :AW]
        ra = lax.rsqrt(_rowmean(a * a) + LN_EPS)
        g_an[...] += _colsum(dna * a * ra)
        dat = dna * ga_ref[...]
        da_ref[...] = ra * dat - a * (ra * ra * ra) * _rowmean(dat * a)

        xh, rstd, u2, sg, u3 = _conv_post(u_ref[...], cg_ref[...], cb_ref[...])
        dnc = dm[:, AW:]
        rc = lax.rsqrt(_rowmean(u3 * u3) + LN_EPS)
        g_cn[...] += _colsum(dnc * u3 * rc)
        dut = dnc * gc_ref[...]
        du3 = rc * dut - u3 * (rc * rc * rc) * _rowmean(dut * u3)
        du2 = du3 * sg * (1.0 + u2 * (1.0 - sg))
        g_lg[...] += _colsum(du2 * xh)
        g_lb[...] += _colsum(du2)
        dxh = du2 * cg_ref[...]
        du1 = rstd * (dxh - _rowmean(dxh) - xh * _rowmean(dxh * xh))
        g_cb[...] += _colsum(du1)
        du_ref[...] = du1

    vec = lambda w: pl.BlockSpec((1, w), lambda m: (0, 0))
    return pl.pallas_call(
        body, name="mix_bwd", grid=(T // TM,),
        in_specs=[pl.BlockSpec((TM, D), lambda m: (m, 0)), pl.BlockSpec((D, D), lambda m: (0, 0)),
                  pl.BlockSpec((TM, AW), lambda m: (m, 0)),
                  pl.BlockSpec((TM, CW), lambda m: (m, 0)), vec(AW), vec(CW), vec(CW), vec(CW)],
        out_specs=(pl.BlockSpec((TM, AW), lambda m: (m, 0)), pl.BlockSpec((TM, CW), lambda m: (m, 0)),
                   vec(AW), vec(CW), vec(CW), vec(CW), vec(CW)),
        out_shape=(SDS((T, AW), F32), SDS((T, CW), F32),
                   SDS((1, AW), F32), SDS((1, CW), F32), SDS((1, CW), F32), SDS((1, CW), F32), SDS((1, CW), F32)),
    )(dz1, w_out, attn, u1, ga, gc, cg, cb)


def _conv_bwd(du1, ag, conv_w):
    def body(du_ref, ag_ref, w_ref, dag_ref, cs_ref, gw_ref, u0p, dup):
        @pl.when(pl.program_id(0) == 0)
        def _():
            cs_ref[...] = jnp.zeros_like(cs_ref)
            gw_ref[...] = jnp.zeros_like(gw_ref)

        u0p[pl.ds(0, PADR), :] = jnp.zeros((PADR, CW), F32)
        dup[pl.ds(S, PADR), :] = jnp.zeros((PADR, CW), F32)

        def fill(i, carry):
            t0 = pl.multiple_of(i * CH, CH)
            a = ag_ref[pl.ds(t0, CH), :CW]
            g = ag_ref[pl.ds(t0, CH), CW:]
            u0p[pl.ds(PADR + t0, CH), :] = a * _sigmoid(g)
            dup[pl.ds(t0, CH), :] = du_ref[pl.ds(t0, CH), :]
            return carry

        lax.fori_loop(0, S // CH, fill, 0)

        def chunk(i, carry):
            t0 = pl.multiple_of(i * CH, CH)
            d = dup[pl.ds(t0, CH), :]
            win_u = u0p[pl.ds(t0, CH + PADR), :]
            win_d = dup[pl.ds(t0, CH + PADR), :]
            du0 = jnp.zeros((CH, CW), F32)
            up_u = _rows_up(win_u)
            for phase, lo, k in _tap_phases(lambda k: PADR - (CK - 1) + k):
                gw_ref[k:k + 1, :] += _colsum(d * up_u(phase)[lo:lo + CH, :])
            up_d = _rows_up(win_d)
            for phase, lo, k in _tap_phases(lambda k: CK - 1 - k):
                du0 = du0 + up_d(phase)[lo:lo + CH, :] * w_ref[k:k + 1, :]
            a = ag_ref[pl.ds(t0, CH), :CW]
            sg = _sigmoid(ag_ref[pl.ds(t0, CH), CW:])
            da = du0 * sg
            dg = du0 * a * sg * (1.0 - sg)
            dag_ref[pl.ds(t0, CH), :CW] = da.astype(BF16)
            dag_ref[pl.ds(t0, CH), CW:] = dg.astype(BF16)
            cs_ref[:, :CW] += _colsum(da)
            cs_ref[:, CW:] += _colsum(dg)
            return carry

        lax.fori_loop(0, S // CH, chunk, 0)

    return pl.pallas_call(
        body, name="conv_bwd", grid=(BL,),
        in_specs=[pl.BlockSpec((S, CW), lambda b: (b, 0)), pl.BlockSpec((S, 2 * CW), lambda b: (b, 0)),
                  pl.BlockSpec((CK, CW), lambda b: (0, 0))],
        out_specs=(pl.BlockSpec((S, 2 * CW), lambda b: (b, 0)),
                   pl.BlockSpec((1, 2 * CW), lambda b: (0, 0)),
                   pl.BlockSpec((PADR, CW), lambda b: (0, 0))),
        out_shape=(SDS((T, 2 * CW), BF16), SDS((1, 2 * CW), F32), SDS((PADR, CW), F32)),
        scratch_shapes=[pltpu.VMEM((S + PADR, CW), F32), pltpu.VMEM((S + PADR, CW), F32)],
    )(du1, ag, conv_w)


def _layer_norm_fwd(z):
    mu = _rowmean(z)
    zc = z - mu
    rstd = lax.rsqrt(_rowmean(zc * zc) + LN_EPS)
    return zc * rstd, rstd


def _layer_norm_bwd(dy, xh, rstd, g):
    dxh = dy * g
    return rstd * (dxh - _rowmean(dxh) - xh * _rowmean(dxh * xh))


def _mix_out_proj_ln1(attn, u1, ga, gc, cg, cb, w_out, x2, g1, b1):
    def body(a_ref, u_ref, ga_ref, gc_ref, cg_ref, cb_ref, w_ref, x_ref, g_ref, b_ref,
             mix_ref, xh_ref, rstd_ref, x1_ref):
        a = a_ref[...]
        ra = lax.rsqrt(_rowmean(a * a) + LN_EPS)
        na = (a * ra * ga_ref[...]).astype(BF16)
        _, _, _, _, u3 = _conv_post(u_ref[...], cg_ref[...], cb_ref[...])
        rc = lax.rsqrt(_rowmean(u3 * u3) + LN_EPS)
        nc = (u3 * rc * gc_ref[...]).astype(BF16)
        mix_ref[:, :AW] = na
        mix_ref[:, AW:] = nc
        z = ALPHA * x_ref[...] + _dot(na, w_ref[pl.ds(0, AW), :]) + _dot(nc, w_ref[pl.ds(AW, CW), :])
        xh, rstd = _layer_norm_fwd(z)
        xh_ref[...] = xh
        rstd_ref[...] = rstd
        x1_ref[...] = (xh * g_ref[...] + b_ref[...]).astype(BF16)

    vec = lambda w: pl.BlockSpec((1, w), lambda m: (0, 0))
    row = pl.BlockSpec((TM, D), lambda m: (m, 0))
    return pl.pallas_call(
        body, name="mix_out_proj_ln1", grid=(T // TM,),
        in_specs=[pl.BlockSpec((TM, AW), lambda m: (m, 0)), pl.BlockSpec((TM, CW), lambda m: (m, 0)),
                  vec(AW), vec(CW), vec(CW), vec(CW), pl.BlockSpec((D, D), lambda m: (0, 0)), row, vec(D), vec(D)],
        out_specs=(row, row, pl.BlockSpec((TM, 1), lambda m: (m, 0)), row),
        out_shape=(SDS((T, D), BF16), SDS((T, D), F32), SDS((T, 1), F32), SDS((T, D), BF16)),
    )(attn, u1, ga, gc, cg, cb, w_out, x2, g1, b1)


def _seq_start(m):
    return lax.bitwise_and(m, S // TMF - 1) == 0


def _shift_down(x, before, k):
    rolled = pltpu.roll(x, k, 0)
    row = lax.broadcasted_iota(jnp.int32, before.shape, 0)
    head = jnp.where(row < k, pltpu.roll(before, k, 0), rolled[:8])
    return jnp.concatenate([head, rolled[8:]], axis=0)


def _shift_up(x, after, k):
    n = x.shape[0]
    rolled = pltpu.roll(x, n - k, 0)
    row = lax.broadcasted_iota(jnp.int32, after.shape, 0)
    tail = jnp.where(row >= 8 - k, pltpu.roll(after, 8 - k, 0), rolled[n - 8:])
    return jnp.concatenate([rolled[:n - 8], tail], axis=0)


def _ffn_up(x1b, w_up, fcw, fcb):
    def body(x_ref, wg_ref, wv_ref, cwg_ref, cwv_ref, cbg_ref, cbv_ref, up_ref, gv_ref, act_ref, prev_g, prev_v):
        @pl.when(_seq_start(pl.program_id(1)))
        def _():
            prev_g[...] = jnp.zeros_like(prev_g)
            prev_v[...] = jnp.zeros_like(prev_v)

        x = x_ref[...]
        outs = []
        for w_ref, cw_ref, cb_ref, prev, lo in ((wg_ref, cwg_ref, cbg_ref, prev_g, 0), (wv_ref, cwv_ref, cbv_ref, prev_v, FT)):
            u = _dot_nt(x, w_ref[...])
            up_ref[:, lo:lo + FT] = u.astype(BF16)
            before = prev[...]
            y = (cw_ref[2:3, :] * u + cw_ref[1:2, :] * _shift_down(u, before, 1)
                 + cw_ref[0:1, :] * _shift_down(u, before, 2) + cb_ref[...])
            prev[...] = u[TMF - 8:]
            gv_ref[:, lo:lo + FT] = y.astype(BF16)
            outs.append(y)
        gate, val = outs
        act_ref[...] = (gate * _sigmoid(gate) * val).astype(BF16)

    wspec = lambda off: pl.BlockSpec((FT, D), lambda n, m: (n + off, 0))
    cwspec = lambda off: pl.BlockSpec((FK, FT), lambda n, m: (0, n + off))
    cbspec = lambda off: pl.BlockSpec((1, FT), lambda n, m: (0, n + off))
    pair = pl.BlockSpec((TMF, 2 * FT), lambda n, m: (m, n))
    return pl.pallas_call(
        body, name="ffn_up", grid=(NFT, T // TMF),
        in_specs=[pl.BlockSpec((TMF, D), lambda n, m: (m, 0)), wspec(0), wspec(NFT),
                  cwspec(0), cwspec(NFT), cbspec(0), cbspec(NFT)],
        out_specs=(pair, pair, pl.BlockSpec((TMF, FT), lambda n, m: (m, n))),
        out_shape=(SDS((T, 2 * DFF), BF16), SDS((T, 2 * DFF), BF16), SDS((T, DFF), BF16)),
        scratch_shapes=[pltpu.VMEM((8, FT), F32)] * 2,
    )(x1b, w_up, w_up, fcw, fcw, fcb, fcb)


def _ffn_down_loss(act, w_down, xh1, g1, b1, g2, b2, target):
    def body(a_ref, w_ref, xh1_ref, g1_ref, b1_ref, g2_ref, b2_ref, t_ref, dz_ref, loss_ref, gg_ref, gb_ref):
        @pl.when(pl.program_id(0) == 0)
        def _():
            loss_ref[...] = jnp.zeros_like(loss_ref)
            gg_ref[...] = jnp.zeros_like(gg_ref)
            gb_ref[...] = jnp.zeros_like(gb_ref)

        for sub in range(TM // TMF):
            rows = pl.ds(sub * TMF, TMF)
            x1 = xh1_ref[rows, :] * g1_ref[...] + b1_ref[...]
            z = ALPHA * x1 + _dot(a_ref[rows, :], w_ref[...])
            xh, rstd = _layer_norm_fwd(z)
            diff = xh * g2_ref[...] + b2_ref[...] - t_ref[rows, :]
            loss_ref[...] += 0.5 * _colsum(_rowmean(diff * diff))
            dout = diff * (1.0 / D)
            gg_ref[...] += _colsum(dout * xh)
            gb_ref[...] += _colsum(dout)
            dz_ref[rows, :] = _layer_norm_bwd(dout, xh, rstd, g2_ref[...])

    vec = pl.BlockSpec((1, D), lambda m: (0, 0))
    row = pl.BlockSpec((TM, D), lambda m: (m, 0))
    return pl.pallas_call(
        body, name="ffn_down_loss", grid=(T // TM,),
        in_specs=[pl.BlockSpec((TM, DFF), lambda m: (m, 0)), pl.BlockSpec((DFF, D), lambda m: (0, 0)),
                  row, vec, vec, vec, vec, row],
        out_specs=(row, pl.BlockSpec((1, 1), lambda m: (0, 0)), vec, vec),
        out_shape=(SDS((T, D), F32), SDS((1, 1), F32), SDS((1, D), F32), SDS((1, D), F32)),
    )(act, w_down, xh1, g1, b1, g2, b2, target)


def _ffn_down_bwd(dz2, w_down, gv, up, fcw):
    tiles = T // TMF

    def body(dz_ref, wd_ref, gv_ref, up_ref, cwg_ref, cwv_ref,
             dpre_ref, csg_ref, csv_ref, gwg_ref, gwv_ref, next_g, next_v):
        step = pl.program_id(1)
        tile = tiles - 1 - step

        @pl.when(step == 0)
        def _():
            for r in (csg_ref, csv_ref, gwg_ref, gwv_ref, next_g, next_v):
                r[...] = jnp.zeros_like(r)

        seq_end = lax.bitwise_and(tile + 1, S // TMF - 1) == 0
        dact = _dot_nt(dz_ref[...].astype(BF16), wd_ref[...])
        gate = gv_ref[:, :FT].astype(F32)
        val = gv_ref[:, FT:].astype(F32)
        sg = _sigmoid(gate)
        gs = gate * sg
        halves = ((dact * val * (sg + gs * (1.0 - sg)), cwg_ref, csg_ref, gwg_ref, next_g, 0),
                  (dact * gs, cwv_ref, csv_ref, gwv_ref, next_v, FT))
        for d0, cw_ref, cs_ref, gw_ref, nxt, lo in halves:
            after = jnp.where(seq_end, 0.0, nxt[...])
            d1 = _shift_up(d0, after, 1)
            d2 = _shift_up(d0, after, 2)
            nxt[...] = d0[:8]
            dpre_ref[:, lo:lo + FT] = (cw_ref[2:3, :] * d0 + cw_ref[1:2, :] * d1 + cw_ref[0:1, :] * d2).astype(BF16)
            cs_ref[...] += _colsum(d0)
            u = up_ref[:, lo:lo + FT].astype(F32)
            for k, dk in enumerate((d2, d1, d0)):
                gw_ref[k:k + 1, :] += _colsum(dk * u)

    cs = pl.BlockSpec((1, FT), lambda n, m: (0, n))
    gw = pl.BlockSpec((FK, FT), lambda n, m: (0, n))
    cwspec = lambda off: pl.BlockSpec((FK, FT), lambda n, m: (0, n + off))
    pair = pl.BlockSpec((TMF, 2 * FT), lambda n, m: (tiles - 1 - m, n))
    return pl.pallas_call(
        body, name="ffn_down_bwd", grid=(NFT, tiles),
        in_specs=[pl.BlockSpec((TMF, D), lambda n, m: (tiles - 1 - m, 0)), pl.BlockSpec((FT, D), lambda n, m: (n, 0)),
                  pair, pair, cwspec(0), cwspec(NFT)],
        out_specs=(pair, cs, cs, gw, gw),
        out_shape=(SDS((T, 2 * DFF), BF16), SDS((1, DFF), F32), SDS((1, DFF), F32),
                   SDS((FK, DFF), F32), SDS((FK, DFF), F32)),
        scratch_shapes=[pltpu.VMEM((8, FT), F32)] * 2,
    )(dz2, w_down, gv, up, fcw, fcw)


def _ffn_up_bwd_ln1(dpre, w_up, dz2, xh1, rstd1, g1):
    def body(a_ref, w_ref, dz2_ref, xh_ref, rstd_ref, g_ref, dz1_ref, gg_ref, gb_ref):
        @pl.when(pl.program_id(0) == 0)
        def _():
            gg_ref[...] = jnp.zeros_like(gg_ref)
            gb_ref[...] = jnp.zeros_like(gb_ref)

        for sub in range(TM // TMF):
            rows = pl.ds(sub * TMF, TMF)
            dx1 = ALPHA * dz2_ref[rows, :]
            for n in range(NFT):
                for half in range(2):
                    a = a_ref[rows, (2 * n + half) * FT:(2 * n + half + 1) * FT]
                    w = w_ref[pl.ds((half * NFT + n) * FT, FT), :]
                    dx1 = dx1 + _dot(a, w)
            xh = xh_ref[rows, :]
            gg_ref[...] += _colsum(dx1 * xh)
            gb_ref[...] += _colsum(dx1)
            dz1_ref[rows, :] = _layer_norm_bwd(dx1, xh, rstd_ref[rows, :], g_ref[...])

    vec = pl.BlockSpec((1, D), lambda m: (0, 0))
    row = pl.BlockSpec((TM, D), lambda m: (m, 0))
    return pl.pallas_call(
        body, name="ffn_up_bwd_ln1", grid=(T // TM,),
        in_specs=[pl.BlockSpec((TM, 2 * DFF), lambda m: (m, 0)),
                  pl.BlockSpec((2 * DFF, D), lambda m: (0, 0), pipeline_mode=pl.Buffered(1)),
                  row, row, pl.BlockSpec((TM, 1), lambda m: (m, 0)), vec],
        out_specs=(row, vec, vec),
        out_shape=(SDS((T, D), F32), SDS((1, D), F32), SDS((1, D), F32)),
    )(dpre, w_up, dz2, xh1, rstd1, g1)


def _grad_w_up(dpre, x1b):
    tk = 1024

    def body(a_ref, b_ref, o_ref, acc):
        k = pl.program_id(1)

        @pl.when(k == 0)
        def _():
            acc[...] = jnp.zeros_like(acc)

        acc[...] += _dot_tn(a_ref[...], b_ref[...])

        @pl.when(k == T // tk - 1)
        def _():
            o_ref[0] = acc[pl.ds(0, FT), :].astype(o_ref.dtype)
            o_ref[1] = acc[pl.ds(FT, FT), :].astype(o_ref.dtype)

    out = pl.pallas_call(
        body, name="grad_w_up", grid=(NFT, T // tk),
        in_specs=[pl.BlockSpec((tk, 2 * FT), lambda n, k: (k, n)), pl.BlockSpec((tk, D), lambda n, k: (k, 0))],
        out_specs=pl.BlockSpec((2, FT, D), lambda n, k: (0, n, 0)),
        out_shape=SDS((2, DFF, D), GRAD_WIRE),
        scratch_shapes=[pltpu.VMEM((2 * FT, D), F32)],
    )(dpre, x1b)
    return out.reshape(2 * DFF, D)


def _row_tile(rows, cols):
    if rows * cols * 4 <= (1 << 20) or rows % 8:
        return rows
    for t in (256, 176, 128, 88, 64, 32, 16, 8):
        if rows % t == 0 and t * cols * 4 <= (1 << 20):
            return t
    return 8


def _sum8(r, name):
    _, rows, cols = r.shape
    tr = _row_tile(rows, cols)

    def body(r_ref, o_ref):
        acc = r_ref[0].astype(F32)
        for p in range(1, NDEV):
            acc = acc + r_ref[p].astype(F32)
        o_ref[...] = acc

    return pl.pallas_call(
        body, name=name, grid=(rows // tr,),
        in_specs=[pl.BlockSpec((NDEV, tr, cols), lambda i: (0, i, 0))],
        out_specs=pl.BlockSpec((tr, cols), lambda i: (i, 0)),
        out_shape=SDS((rows, cols), F32),
    )(r)


def _sum8_adamw(r, w, m, v, name):
    rows, cols = w.shape
    tr = _row_tile(rows, cols)

    def body(r_ref, w_ref, m_ref, v_ref, g_out, d_ref, nm_ref, nv_ref):
        g_ = r_ref[0].astype(F32)
        for p in range(1, NDEV):
            g_ = g_ + r_ref[p].astype(F32)
        m_ = B1 * m_ref[...] + (1.0 - B1) * g_
        v_ = B2 * v_ref[...] + (1.0 - B2) * jnp.square(g_)
        m_hat = m_ / (1.0 - B1 ** STEP)
        v_hat = v_ / (1.0 - B2 ** STEP)
        g_out[...] = g_
        d_ref[...] = -LR * (m_hat / (jnp.sqrt(v_hat) + AEPS) + WD * w_ref[...])
        nm_ref[...] = m_
        nv_ref[...] = v_

    spec = pl.BlockSpec((tr, cols), lambda i: (i, 0))
    shp = SDS((rows, cols), F32)
    return pl.pallas_call(
        body, name=name, grid=(rows // tr,),
        in_specs=[pl.BlockSpec((NDEV, tr, cols), lambda i: (0, i, 0))] + [spec] * 3, out_specs=(spec,) * 4,
        out_shape=(shp,) * 4,
    )(r, w, m, v)


def _adamw_many(ws, gs, ms, vs, name):
    n = len(ws)

    def body(*refs):
        for i in range(n):
            w_ref, g_ref, m_ref, v_ref, d_ref, nm_ref, nv_ref = refs[i::n]
            g_ = g_ref[...]
            m_ = B1 * m_ref[...] + (1.0 - B1) * g_
            v_ = B2 * v_ref[...] + (1.0 - B2) * jnp.square(g_)
            m_hat = m_ / (1.0 - B1 ** STEP)
            v_hat = v_ / (1.0 - B2 ** STEP)
            d_ref[...] = -LR * (m_hat / (jnp.sqrt(v_hat) + AEPS) + WD * w_ref[...])
            nm_ref[...] = m_
            nv_ref[...] = v_

    shapes = tuple(SDS(w.shape, F32) for w in ws)
    res = pl.pallas_call(body, name=name, out_shape=shapes * 3)(*ws, *gs, *ms, *vs)
    return res[:n], res[n:2 * n], res[2 * n:]


def _local_step(x2, target, rel_table, first_weights, b_in, conv_b, conv_ln_g, conv_ln_b, attn_norm_g,
                conv_norm_g, late_weights, ln1_g, ln1_b, ffn_conv_b, ln2_g, ln2_b, ship_ffn_grads, ship_w_in_grads,
                ship_small_grads):
    buckets = jnp.asarray(_bucket_maps())
    bias = _bias_table(rel_table, buckets)
    xb, cast_done = _cast_x(x2, bias[0, 0, :8, :BLK])
    w_in_t, conv_w, ffn_conv_w = first_weights(cast_done)

    qkv, ag = _proj_in(xb, w_in_t, b_in)
    attn, lse = _attn_fwd(qkv, bias)
    u1 = _conv_fwd(ag, conv_w, conv_b)
    w_out = late_weights(0, attn[:8, :BLK] + u1[:8, :BLK])
    mixed, xh1, rstd1, x1b = _mix_out_proj_ln1(attn, u1, attn_norm_g, conv_norm_g, conv_ln_g, conv_ln_b,
                                               w_out, x2, ln1_g, ln1_b)
    w_up = late_weights(1, x1b)
    up, gv, act = _ffn_up(x1b, w_up, ffn_conv_w, ffn_conv_b)
    w_down = late_weights(2, act)
    dz2, loss, g_ln2_g, g_ln2_b = _ffn_down_loss(act, w_down, xh1, ln1_g, ln1_b, ln2_g, ln2_b, target)

    dpre, cs_g, cs_v, gfw_g, gfw_v = _ffn_down_bwd(dz2, w_down, gv, up, ffn_conv_w)
    g_w_down = _mm_tn(act, dz2, DFF // 2, 512, "grad_w_down")
    dz1, g_ln1_g, g_ln1_b = _ffn_up_bwd_ln1(dpre, w_up, dz2, xh1, rstd1, ln1_g)
    g_w_out = _mm_tn(mixed, dz1, D, 512, "grad_w_out")
    zero = ship_ffn_grads(g_w_down, _grad_w_up(dpre, x1b), g_w_out)
    dattn, du1, g_an, g_cn, g_clg, g_clb, g_cb = _mix_bwd(
        dz1, w_out, attn, u1, attn_norm_g + zero, conv_norm_g, conv_ln_g, conv_ln_b)
    dag, cs_ag, g_conv_w = _conv_bwd(du1, ag, conv_w)
    dq, dk, dv, cs_q, cs_k, cs_v2, dbias = _attn_bwd(qkv, attn, lse, dattn, bias)
    pieces = [dq, dk, dv, dag]
    zero_a = ship_w_in_grads(_grad_w_in(pieces, x2))
    g_rel = _rel_table_grad(dbias, buckets, zero_a)

    grads = dict(
        rel_table=g_rel,
        b_in=jnp.concatenate([cs_q, cs_k, cs_v2, cs_ag], axis=1),
        conv_b=g_cb, conv_ln_g=g_clg, conv_ln_b=g_clb, attn_norm_g=g_an, conv_norm_g=g_cn,
        ln1_g=g_ln1_g, ln1_b=g_ln1_b,
        ffn_conv_b=jnp.concatenate([cs_g, cs_v], axis=1),
        ln2_g=g_ln2_g, ln2_b=g_ln2_b,
        conv_w=g_conv_w[:CK],
        ffn_conv_w=jnp.concatenate([gfw_g, gfw_v], axis=1),
    )
    grads["loss"] = loss
    grad_x = _grad_x(pieces, w_in_t, dz1, ship_small_grads(grads))
    return loss, grad_x


SMALL = (("rel_table", (NBUCKET, NH)), ("b_in", (1, INW)), ("conv_b", (1, CW)), ("conv_ln_g", (1, CW)),
         ("conv_ln_b", (1, CW)), ("attn_norm_g", (1, AW)), ("conv_norm_g", (1, CW)), ("ln1_g", (1, D)),
         ("ln1_b", (1, D)), ("ffn_conv_b", (1, 2 * DFF)), ("ln2_g", (1, D)), ("ln2_b", (1, D)))
SHARDED_SMALL = (("conv_w", (CK, CW)), ("ffn_conv_w", (FK, 2 * DFF)))


def _pack(parts):
    flat = jnp.concatenate([p.reshape(-1) for p in parts])
    tile = 8 * PACK_LANES
    pad = (-flat.shape[0]) % tile
    return jnp.pad(flat, (0, pad)).reshape(-1, PACK_LANES)


def _unpack(packed, specs):
    flat = packed.reshape(-1)
    out, off = {}, 0
    for name, shp in specs:
        size = int(np.prod(shp))
        out[name] = flat[off:off + size].reshape(shp)
        off += size
    return out


def kernel(x, rel_table, w_in, b_in, conv_w, conv_b, conv_ln_g, conv_ln_b, attn_norm_g, conv_norm_g, w_out, ln1_g, ln1_b, w_up, ffn_conv_w, ffn_conv_b, w_down, ln2_g, ln2_b, loss_target, m_rel_table, m_w_in, m_b_in, m_conv_w, m_conv_b, m_conv_ln_g, m_conv_ln_b, m_attn_norm_g, m_conv_norm_g, m_w_out, m_ln1_g, m_ln1_b, m_w_up, m_ffn_conv_w, m_ffn_conv_b, m_w_down, m_ln2_g, m_ln2_b, v_rel_table, v_w_in, v_b_in, v_conv_w, v_conv_b, v_conv_ln_g, v_conv_ln_b, v_attn_norm_g, v_conv_norm_g, v_w_out, v_ln1_g, v_ln1_b, v_w_up, v_ffn_conv_w, v_ffn_conv_b, v_w_down, v_ln2_g, v_ln2_b):
    given = dict(locals())
    me = 4 * lax.axis_index("x") + 2 * lax.axis_index("y") + lax.axis_index("c")

    cols = lambda a: a.transpose(1, 0, 2).reshape(a.shape[1], NDEV * a.shape[2])
    rows = lambda a: a.reshape(NDEV * a.shape[1], a.shape[2])
    stack = lambda a: a.reshape(NDEV, a.shape[0] // NDEV, a.shape[1])

    small_specs = SMALL + SHARDED_SMALL
    packed_specs = small_specs + (("loss", (1, 1)),)
    grad, delta, new_m, new_v = {}, {}, {}, {}

    def adamw_big(n, partials, transposed=False):
        shp = given[n].shape
        to2d = (lambda a: a.reshape(shp[-2], shp[-1]).T) if transposed else (lambda a: a.reshape(shp[-2], shp[-1]))
        back = (lambda a: a.T.reshape(shp)) if transposed else (lambda a: a.reshape(shp))
        g_, d_, m_, v_ = _sum8_adamw(partials, to2d(given[n]), to2d(given["m_" + n]), to2d(given["v_" + n]), "adamw_" + n)
        grad[n], delta[n], new_m[n], new_v[n] = back(g_), back(d_), back(m_), back(v_)
        return d_

    first_state, zero0 = _exchange_start(
        [(w_in[0].T.astype(BF16), "gather"), (conv_w[0], "gather"), (ffn_conv_w[0], "gather")], "gather_first_start")

    def first_weights(after):
        lands = _exchange_wait(first_state, after, "gather_first_wait")
        return rows(lands[0]), cols(lands[1]), cols(lands[2])

    late_state, zero1 = _exchange_start(
        [(w_out[0].astype(BF16), "gather"), (w_up[0].T.astype(BF16) + zero0.astype(BF16), "gather"),
         (w_down[0].astype(BF16), "gather")], "gather_late_start")

    def late_weights(i, after):
        return rows(_exchange_wait(late_state, after, "gather_late_wait_%d" % i, only=(i,))[i])

    shipped = {}

    def ship_ffn_grads(g_w_down, g_w_up_t, g_w_out):
        shipped["ffn"], zero2 = _exchange_start(
            [(stack(a), "scatter") for a in (g_w_down, g_w_up_t, g_w_out)], "ffn_grads_start")
        return zero2

    def ship_w_in_grads(g_w_in_t):
        shipped["w_in"], zero3 = _exchange_start([(stack(g_w_in_t), "scatter")], "w_in_grads_start")
        return zero3.reshape(1, 1)

    def ship_small_grads(small_grads):
        shipped["small"], zero4 = _exchange_start(
            [(_pack([small_grads[n] for n, _ in packed_specs]), "gather")], "small_grads_start")
        return zero4.reshape(1, 1)

    loss, grad_x = _local_step(
        x.reshape(T, D), loss_target.reshape(T, D), rel_table + zero1, first_weights, b_in, conv_b, conv_ln_g,
        conv_ln_b, attn_norm_g, conv_norm_g, late_weights, ln1_g, ln1_b, ffn_conv_b,
        ln2_g, ln2_b, ship_ffn_grads, ship_w_in_grads, ship_small_grads)

    got_down, got_up, got_out = _exchange_wait(shipped["ffn"], grad_x, "ffn_grads_wait")
    adamw_big("w_down", got_down)
    adamw_big("w_up", got_up, transposed=True)
    last = adamw_big("w_out", got_out)

    (got_in,) = _exchange_wait(shipped["w_in"], last, "w_in_grads_wait")
    (got_small,) = _exchange_wait(shipped["small"], last, "small_grads_wait")
    adamw_big("w_in", got_in, transposed=True)
    small = _unpack(_sum8(got_small, "sum_small"), packed_specs)
    small["conv_w"] = lax.dynamic_slice_in_dim(small["conv_w"], me * (CW // NDEV), CW // NDEV, axis=1)
    small["ffn_conv_w"] = lax.dynamic_slice_in_dim(small["ffn_conv_w"], me * (2 * DFF // NDEV), 2 * DFF // NDEV, axis=1)
    names = [n for n, _ in small_specs]
    two = lambda a: a.reshape(a.shape[-2], a.shape[-1])
    ds, nms, nvs = _adamw_many([two(given[n]) for n in names], [small[n] for n in names],
                               [two(given["m_" + n]) for n in names], [two(given["v_" + n]) for n in names], "adamw_small")
    for n, d_, m_, v_ in zip(names, ds, nms, nvs):
        shp = given[n].shape
        grad[n], delta[n], new_m[n], new_v[n] = small[n].reshape(shp), d_.reshape(shp), m_.reshape(shp), v_.reshape(shp)

    order = ("rel_table", "w_in", "b_in", "conv_w", "conv_b", "conv_ln_g", "conv_ln_b", "attn_norm_g",
             "conv_norm_g", "w_out", "ln1_g", "ln1_b", "w_up", "ffn_conv_w", "ffn_conv_b", "w_down", "ln2_g", "ln2_b")
    return (small["loss"][0, 0], grad_x.reshape(BL, S, D), *[grad[n] for n in order], *[delta[n] for n in order],
            *[new_m[n] for n in order], *[new_v[n] for n in order])
```
